```python
import math
import jax, jax.numpy as jnp
from jax import lax
import numpy as np

D_MODEL = 1024
BATCH = 8
SEQ = 8192
DEPTH = 4

N_MIXERS = 4
HEAD_DIM = 64
GROUP_W = D_MODEL // N_MIXERS
HEADS_PER_MIXER = GROUP_W // HEAD_DIM
D_MIX = N_MIXERS * GROUP_W
N_CHUNKS = 13
D_IN = N_CHUNKS * GROUP_W
SHORT_CONV_W = 3
RG_CONV_W = 4
RG_C = 8.0
GMLP_CHUNK = 128
ATTN_PATTERNS = ((128, 1), (512, 4), (2048, 16))
ATTN_BLOCK = 128
NORM_EPS = 1e-6

kernel_name = "hybrid_parallel_groups_conv_rglru_gmlp_dilated_attn"


def rms_norm(x, g):
    xf = x.astype(jnp.float32)
    y = xf * lax.rsqrt(jnp.mean(xf * xf, axis=-1, keepdims=True) + NORM_EPS)
    return (y * g.astype(jnp.float32)).astype(x.dtype)


def causal_depthwise_conv(x, w):
    K, C = w.shape
    return lax.conv_general_dilated(
        x, w[:, None, :].astype(x.dtype), window_strides=(1,), padding=[(K - 1, 0)],
        dimension_numbers=("NWC", "WIO", "NWC"), feature_group_count=C)


def rg_lru(xb, wa, ba, wx, bx, lam):
    B, S, C = xb.shape
    xh = xb.reshape(B, S, HEADS_PER_MIXER, HEAD_DIM)
    r = jax.nn.sigmoid(jnp.einsum('bshi,hij->bshj', xh, wa).reshape(B, S, C) + ba)
    i = jax.nn.sigmoid(jnp.einsum('bshi,hij->bshj', xh, wx).reshape(B, S, C) + bx)
    log_a = (-RG_C * r.astype(jnp.float32)) * jax.nn.softplus(-lam.astype(jnp.float32))
    a = jnp.exp(log_a)
    mult = jnp.sqrt(-jnp.expm1(2.0 * log_a))
    b = mult * (i * xb).astype(jnp.float32)

    def combine(e1, e2):
        a1, b1 = e1
        a2, b2 = e2
        return a1 * a2, a2 * b1 + b2

    _, h = lax.associative_scan(combine, (a, b), axis=1)
    return h.astype(xb.dtype)


def dilated_window_attention(q, k, v, slopes, window, dil):
    B, S, H, Dh = q.shape
    n_back = window // dil
    span = dil * ATTN_BLOCK
    Sp = -(-S // span) * span
    pad = Sp - S
    if pad:
        pw = ((0, 0), (0, pad), (0, 0), (0, 0))
        q, k, v = jnp.pad(q, pw), jnp.pad(k, pw), jnp.pad(v, pw)
    L = Sp // dil
    nb = L // ATTN_BLOCK
    to_blocks = lambda t: t.reshape(B, nb, ATTN_BLOCK, dil, H, Dh)
    qb, kb, vb = to_blocks(q), to_blocks(k), to_blocks(v)

    def with_prev(t):
        prev = jnp.pad(t[:, :-1], ((0, 0), (1, 0), (0, 0), (0, 0), (0, 0), (0, 0)))
        return jnp.concatenate([prev, t], axis=2)

    kk, vv = with_prev(kb), with_prev(vb)
    scale = 1.0 / math.sqrt(Dh)
    s = jnp.einsum('bnqrhd,bnkrhd->bnrhqk', qb.astype(jnp.float32),
                   kk.astype(jnp.float32)) * scale
    qi = jnp.arange(ATTN_BLOCK)[:, None]
    ki = jnp.arange(2 * ATTN_BLOCK)[None, :]
    delta = qi + ATTN_BLOCK - ki
    band = (delta >= 0) & (delta <= n_back)
    first_ok = (jnp.arange(nb)[:, None, None] > 0) | (ki[None] >= ATTN_BLOCK)
    mask = band[None] & first_ok
    bias = -slopes[:, None, None] * (delta * dil).astype(jnp.float32)[None]
    s = jnp.where(mask[None, :, None, None], s + bias, -jnp.inf)
    m = jnp.max(s, axis=-1, keepdims=True)
    p = jnp.exp(s - m)
    l = jnp.sum(p, axis=-1, keepdims=True)
    o = jnp.einsum('bnrhqk,bnkrhd->bnqrhd', p, vv.astype(jnp.float32))
    o = o / jnp.transpose(l, (0, 1, 4, 2, 3, 5))
    lse = jnp.transpose((m + jnp.log(l))[..., 0], (0, 1, 4, 2, 3))
    o = o.reshape(B, Sp, H, Dh)[:, :S]
    lse = lse.reshape(B, Sp, H)[:, :S]
    return o, lse


def _fwd_setup_inputs(seed: int = 0) -> dict:
    key = jax.random.key(seed)
    ks = jax.random.split(key, 16)
    f32 = jnp.float32
    nrm = lambda k, shape, sc: jax.random.normal(k, shape, f32) * sc
    x = jax.random.normal(ks[0], (BATCH, SEQ, D_MODEL), f32)
    norm_g = 1.0 + nrm(ks[1], (DEPTH, D_MODEL), 0.01)
    w_in = nrm(ks[2], (DEPTH, D_MODEL, D_IN), D_MODEL ** -0.5)
    conv_a_w = nrm(ks[3], (DEPTH, SHORT_CONV_W, GROUP_W), SHORT_CONV_W ** -0.5)
    conv_r_w = nrm(ks[4], (DEPTH, RG_CONV_W, GROUP_W), RG_CONV_W ** -0.5)
    conv_r_b = nrm(ks[5], (DEPTH, GROUP_W), 0.01)
    lru_wa = nrm(ks[6], (DEPTH, HEADS_PER_MIXER, HEAD_DIM, HEAD_DIM), HEAD_DIM ** -0.5)
    lru_ba = nrm(ks[7], (DEPTH, GROUP_W), 0.01)
    lru_wx = nrm(ks[8], (DEPTH, HEADS_PER_MIXER, HEAD_DIM, HEAD_DIM), HEAD_DIM ** -0.5)
    lru_bx = nrm(ks[9], (DEPTH, GROUP_W), 0.01)
    a_c = jax.random.uniform(ks[10], (DEPTH, GROUP_W), f32, 0.9, 0.999)
    sig = a_c ** (1.0 / RG_C)
    lru_lambda = jnp.log(sig) - jnp.log1p(-sig)
    gmlp_norm_g = 1.0 + nrm(ks[11], (DEPTH, GROUP_W), 0.01)
    gmlp_ws = nrm(ks[12], (DEPTH, HEADS_PER_MIXER, GMLP_CHUNK, GMLP_CHUNK), GMLP_CHUNK ** -0.5)
    gmlp_bs = 1.0 + nrm(ks[13], (DEPTH, HEADS_PER_MIXER, GMLP_CHUNK), 0.1)
    w_out = nrm(ks[14], (DEPTH, D_MIX, D_MODEL), D_MIX ** -0.5)
    final_g = 1.0 + nrm(ks[15], (D_MODEL,), 0.01)
    return {"x": x, "norm_g": norm_g, "w_in": w_in, "conv_a_w": conv_a_w,
            "conv_r_w": conv_r_w, "conv_r_b": conv_r_b, "lru_wa": lru_wa,
            "lru_ba": lru_ba, "lru_wx": lru_wx, "lru_bx": lru_bx,
            "lru_lambda": lru_lambda, "gmlp_norm_g": gmlp_norm_g,
            "gmlp_ws": gmlp_ws, "gmlp_bs": gmlp_bs, "w_out": w_out,
            "final_g": final_g}


def _fwd_reference(x, norm_g, w_in, conv_a_w, conv_r_w, conv_r_b, lru_wa, lru_ba,
              lru_wx, lru_bx, lru_lambda, gmlp_norm_g, gmlp_ws, gmlp_bs, w_out,
              final_g):
    B, S, _ = x.shape
    H = HEADS_PER_MIXER
    slopes = 2.0 ** (-8.0 * jnp.arange(1, H + 1, dtype=jnp.float32) / H)
    causal_chunk = jnp.tril(jnp.ones((GMLP_CHUNK, GMLP_CHUNK), dtype=bool))
    n_chunk = S // GMLP_CHUNK
    for l in range(DEPTH):
        h = rms_norm(x, norm_g[l])
        z = jnp.einsum('bsd,de->bse', h, w_in[l])
        (a_x, a_b, a_c, a_g,
         r_x, r_g,
         c_u, c_v, c_g,
         d_q, d_k, d_v, d_g) = jnp.split(z, N_CHUNKS, axis=-1)

        y_a = a_b * causal_depthwise_conv(a_c * a_x, conv_a_w[l]) * jax.nn.silu(a_g)

        xb = causal_depthwise_conv(r_x, conv_r_w[l]) + conv_r_b[l]
        y_b = rg_lru(xb, lru_wa[l], lru_ba[l], lru_wx[l], lru_bx[l], lru_lambda[l]) * jax.nn.silu(r_g)

        u = jax.nn.gelu(c_u)
        vv = rms_norm(jax.nn.gelu(c_v), gmlp_norm_g[l])
        vv = vv.reshape(B, n_chunk, GMLP_CHUNK, H, HEAD_DIM)
        ws = jnp.where(causal_chunk[None], gmlp_ws[l], 0.0).astype(vv.dtype)
        sp = jnp.einsum('hts,bnshc->bnthc', ws, vv) + jnp.transpose(gmlp_bs[l])[:, :, None]
        y_c = u * sp.reshape(B, S, GROUP_W) * jax.nn.silu(c_g)

        q = d_q.reshape(B, S, H, HEAD_DIM)
        k = d_k.reshape(B, S, H, HEAD_DIM)
        v = d_v.reshape(B, S, H, HEAD_DIM)
        outs, lses = [], []
        for window, dil in ATTN_PATTERNS:
            o_p, lse_p = dilated_window_attention(q, k, v, slopes, window, dil)
            outs.append(o_p)
            lses.append(lse_p)
        wts = jax.nn.softmax(jnp.stack(lses), axis=0)
        o = jnp.einsum('pbsh,pbshd->bshd', wts, jnp.stack(outs))
        y_d = o.reshape(B, S, GROUP_W).astype(x.dtype) * jax.nn.silu(d_g)

        y = jnp.concatenate([y_a, y_b, y_c, y_d], axis=-1)
        x = x + jnp.einsum('bse,ed->bsd', y, w_out[l])
    return rms_norm(x, final_g)


import jax as _jax
import jax.numpy as _jnp

TWIN_FORMAT = 'train_step'
FWD_PARAMS = ['x', 'norm_g', 'w_in', 'conv_a_w', 'conv_r_w', 'conv_r_b', 'lru_wa', 'lru_ba', 'lru_wx', 'lru_bx', 'lru_lambda', 'gmlp_norm_g', 'gmlp_ws', 'gmlp_bs', 'w_out', 'final_g']
TWIN_WEIGHTS = ['norm_g', 'w_in', 'conv_a_w', 'conv_r_w', 'conv_r_b', 'lru_wa', 'lru_ba', 'lru_wx', 'lru_bx', 'lru_lambda', 'gmlp_norm_g', 'gmlp_ws', 'gmlp_bs', 'w_out', 'final_g']
TWIN_DIFF_INPUT = 'x'
TWIN_INPUTS = ['x', 'norm_g', 'w_in', 'conv_a_w', 'conv_r_w', 'conv_r_b', 'lru_wa', 'lru_ba', 'lru_wx', 'lru_bx', 'lru_lambda', 'gmlp_norm_g', 'gmlp_ws', 'gmlp_bs', 'w_out', 'final_g', 'loss_target', 'm_norm_g', 'm_w_in', 'm_conv_a_w', 'm_conv_r_w', 'm_conv_r_b', 'm_lru_wa', 'm_lru_ba', 'm_lru_wx', 'm_lru_bx', 'm_lru_lambda', 'm_gmlp_norm_g', 'm_gmlp_ws', 'm_gmlp_bs', 'm_w_out', 'm_final_g', 'v_norm_g', 'v_w_in', 'v_conv_a_w', 'v_conv_r_w', 'v_conv_r_b', 'v_lru_wa', 'v_lru_ba', 'v_lru_wx', 'v_lru_bx', 'v_lru_lambda', 'v_gmlp_norm_g', 'v_gmlp_ws', 'v_gmlp_bs', 'v_w_out', 'v_final_g']
TWIN_OUTPUTS = ['loss', 'grad_x', 'grad_norm_g', 'grad_w_in', 'grad_conv_a_w', 'grad_conv_r_w', 'grad_conv_r_b', 'grad_lru_wa', 'grad_lru_ba', 'grad_lru_wx', 'grad_lru_bx', 'grad_lru_lambda', 'grad_gmlp_norm_g', 'grad_gmlp_ws', 'grad_gmlp_bs', 'grad_w_out', 'grad_final_g', 'delta_norm_g', 'delta_w_in', 'delta_conv_a_w', 'delta_conv_r_w', 'delta_conv_r_b', 'delta_lru_wa', 'delta_lru_ba', 'delta_lru_wx', 'delta_lru_bx', 'delta_lru_lambda', 'delta_gmlp_norm_g', 'delta_gmlp_ws', 'delta_gmlp_bs', 'delta_w_out', 'delta_final_g', 'new_m_norm_g', 'new_m_w_in', 'new_m_conv_a_w', 'new_m_conv_r_w', 'new_m_conv_r_b', 'new_m_lru_wa', 'new_m_lru_ba', 'new_m_lru_wx', 'new_m_lru_bx', 'new_m_lru_lambda', 'new_m_gmlp_norm_g', 'new_m_gmlp_ws', 'new_m_gmlp_bs', 'new_m_w_out', 'new_m_final_g', 'new_v_norm_g', 'new_v_w_in', 'new_v_conv_a_w', 'new_v_conv_r_w', 'new_v_conv_r_b', 'new_v_lru_wa', 'new_v_lru_ba', 'new_v_lru_wx', 'new_v_lru_bx', 'new_v_lru_lambda', 'new_v_gmlp_norm_g', 'new_v_gmlp_ws', 'new_v_gmlp_bs', 'new_v_w_out', 'new_v_final_g']
TWIN_LEAF_KINDS = {'loss': 'loss', 'grad_x': 'grad_x', 'grad_norm_g': 'grad_w', 'grad_w_in': 'grad_w', 'grad_conv_a_w': 'grad_w', 'grad_conv_r_w': 'grad_w', 'grad_conv_r_b': 'grad_w', 'grad_lru_wa': 'grad_w', 'grad_lru_ba': 'grad_w', 'grad_lru_wx': 'grad_w', 'grad_lru_bx': 'grad_w', 'grad_lru_lambda': 'grad_w', 'grad_gmlp_norm_g': 'grad_w', 'grad_gmlp_ws': 'grad_w', 'grad_gmlp_bs': 'grad_w', 'grad_w_out': 'grad_w', 'grad_final_g': 'grad_w', 'delta_norm_g': 'delta_w', 'delta_w_in': 'delta_w', 'delta_conv_a_w': 'delta_w', 'delta_conv_r_w': 'delta_w', 'delta_conv_r_b': 'delta_w', 'delta_lru_wa': 'delta_w', 'delta_lru_ba': 'delta_w', 'delta_lru_wx': 'delta_w', 'delta_lru_bx': 'delta_w', 'delta_lru_lambda': 'delta_w', 'delta_gmlp_norm_g': 'delta_w', 'delta_gmlp_ws': 'delta_w', 'delta_gmlp_bs': 'delta_w', 'delta_w_out': 'delta_w', 'delta_final_g': 'delta_w', 'new_m_norm_g': 'new_m', 'new_m_w_in': 'new_m', 'new_m_conv_a_w': 'new_m', 'new_m_conv_r_w': 'new_m', 'new_m_conv_r_b': 'new_m', 'new_m_lru_wa': 'new_m', 'new_m_lru_ba': 'new_m', 'new_m_lru_wx': 'new_m', 'new_m_lru_bx': 'new_m', 'new_m_lru_lambda': 'new_m', 'new_m_gmlp_norm_g': 'new_m', 'new_m_gmlp_ws': 'new_m', 'new_m_gmlp_bs': 'new_m', 'new_m_w_out': 'new_m', 'new_m_final_g': 'new_m', 'new_v_norm_g': 'new_v', 'new_v_w_in': 'new_v', 'new_v_conv_a_w': 'new_v', 'new_v_conv_r_w': 'new_v', 'new_v_conv_r_b': 'new_v', 'new_v_lru_wa': 'new_v', 'new_v_lru_ba': 'new_v', 'new_v_lru_wx': 'new_v', 'new_v_lru_bx': 'new_v', 'new_v_lru_lambda': 'new_v', 'new_v_gmlp_norm_g': 'new_v', 'new_v_gmlp_ws': 'new_v', 'new_v_gmlp_bs': 'new_v', 'new_v_w_out': 'new_v', 'new_v_final_g': 'new_v'}


def _forward(args):
    return _fwd_reference(*[args[k] for k in FWD_PARAMS])


def _output_shape():
    def fwd():
        inp = _fwd_setup_inputs(0)
        return _fwd_reference(*[inp[k] for k in FWD_PARAMS])
    out = _jax.eval_shape(fwd)
    return out.shape, out.dtype

N_MICROBATCH = 1
ADAM_LR = 0.001
ADAM_B1 = 0.9
ADAM_B2 = 0.999
ADAM_EPS = 1e-08
ADAM_WD = 0.01
ADAM_STEP = 10
PER_EXAMPLE_BATCH_AXIS = {'x': 0, 'loss_target': 0}
SHARED_INPUTS = []
_WEIGHT_DTYPES = {'norm_g': _jnp.float32, 'w_in': _jnp.float32, 'conv_a_w': _jnp.float32, 'conv_r_w': _jnp.float32, 'conv_r_b': _jnp.float32, 'lru_wa': _jnp.float32, 'lru_ba': _jnp.float32, 'lru_wx': _jnp.float32, 'lru_bx': _jnp.float32, 'lru_lambda': _jnp.float32, 'gmlp_norm_g': _jnp.float32, 'gmlp_ws': _jnp.float32, 'gmlp_bs': _jnp.float32, 'w_out': _jnp.float32, 'final_g': _jnp.float32}
MOMENT_SCALE = {'norm_g': 1.967080e-01, 'w_in': 1.114848e-01, 'conv_a_w': 1.497283e-01, 'conv_r_w': 1.399492e-01, 'conv_r_b': 1.451058e+00, 'lru_wa': 6.303206e-02, 'lru_ba': 4.414774e-02, 'lru_wx': 1.181202e-01, 'lru_bx': 5.109765e-02, 'lru_lambda': 7.067218e-02, 'gmlp_norm_g': 6.694316e-02, 'gmlp_ws': 4.927366e-02, 'gmlp_bs': 6.996659e-02, 'w_out': 1.223475e-01, 'final_g': 6.401436e+01}


def _to_microbatches(a, axis):
    t = _jnp.moveaxis(a, axis, 0)
    t = t.reshape((N_MICROBATCH, t.shape[0] // N_MICROBATCH) + t.shape[1:])
    return _jnp.moveaxis(t, 1, axis + 1)


def setup_inputs(seed: int = 0) -> dict:
    inp = _fwd_setup_inputs(seed)
    key = _jax.random.fold_in(_jax.random.key(seed), 7919)
    shape, _ = _output_shape()
    out = dict(inp)
    out["loss_target"] = _jax.random.normal(_jax.random.fold_in(key, 0), shape, _jnp.float32)
    for i, name in enumerate(TWIN_WEIGHTS):
        w = inp[name].astype(_jnp.float32)
        if MOMENT_SCALE is None:
            s = _jnp.sqrt(_jnp.mean(_jnp.square(w)) + 1e-30)
        else:
            s = MOMENT_SCALE[name]
        km, kv = _jax.random.split(_jax.random.fold_in(key, i + 1))
        out[name] = w
        out["m_" + name] = s * _jax.random.normal(km, w.shape, _jnp.float32)
        out["v_" + name] = (s * s) * _jax.random.uniform(kv, w.shape, _jnp.float32, 0.5, 1.5)
    if N_MICROBATCH > 1:
        for name, axis in PER_EXAMPLE_BATCH_AXIS.items():
            out[name] = _to_microbatches(out[name], axis)
    return {'x': out['x'], 'norm_g': out['norm_g'], 'w_in': out['w_in'], 'conv_a_w': out['conv_a_w'], 'conv_r_w': out['conv_r_w'], 'conv_r_b': out['conv_r_b'], 'lru_wa': out['lru_wa'], 'lru_ba': out['lru_ba'], 'lru_wx': out['lru_wx'], 'lru_bx': out['lru_bx'], 'lru_lambda': out['lru_lambda'], 'gmlp_norm_g': out['gmlp_norm_g'], 'gmlp_ws': out['gmlp_ws'], 'gmlp_bs': out['gmlp_bs'], 'w_out': out['w_out'], 'final_g': out['final_g'], 'loss_target': out['loss_target'], 'm_norm_g': out['m_norm_g'], 'm_w_in': out['m_w_in'], 'm_conv_a_w': out['m_conv_a_w'], 'm_conv_r_w': out['m_conv_r_w'], 'm_conv_r_b': out['m_conv_r_b'], 'm_lru_wa': out['m_lru_wa'], 'm_lru_ba': out['m_lru_ba'], 'm_lru_wx': out['m_lru_wx'], 'm_lru_bx': out['m_lru_bx'], 'm_lru_lambda': out['m_lru_lambda'], 'm_gmlp_norm_g': out['m_gmlp_norm_g'], 'm_gmlp_ws': out['m_gmlp_ws'], 'm_gmlp_bs': out['m_gmlp_bs'], 'm_w_out': out['m_w_out'], 'm_final_g': out['m_final_g'], 'v_norm_g': out['v_norm_g'], 'v_w_in': out['v_w_in'], 'v_conv_a_w': out['v_conv_a_w'], 'v_conv_r_w': out['v_conv_r_w'], 'v_conv_r_b': out['v_conv_r_b'], 'v_lru_wa': out['v_lru_wa'], 'v_lru_ba': out['v_lru_ba'], 'v_lru_wx': out['v_lru_wx'], 'v_lru_bx': out['v_lru_bx'], 'v_lru_lambda': out['v_lru_lambda'], 'v_gmlp_norm_g': out['v_gmlp_norm_g'], 'v_gmlp_ws': out['v_gmlp_ws'], 'v_gmlp_bs': out['v_gmlp_bs'], 'v_w_out': out['v_w_out'], 'v_final_g': out['v_final_g']}


def _loss(weights, diff, rest, loss_target):
    with _jax.named_scope("forward"):
        args = {**rest, TWIN_DIFF_INPUT: diff, **{k: w.astype(_WEIGHT_DTYPES[k]) for k, w in weights.items()}}
        y = _forward(args)
    with _jax.named_scope("loss_head"):
        err = _jnp.square(y.astype(_jnp.float32) - loss_target)
        return 0.5 * _jnp.sum(_jnp.mean(err, axis=-1)) if err.ndim else 0.5 * err


def _adamw(w, g, m, v):
    m = ADAM_B1 * m + (1.0 - ADAM_B1) * g
    v = ADAM_B2 * v + (1.0 - ADAM_B2) * _jnp.square(g)
    m_hat = m / (1.0 - ADAM_B1 ** ADAM_STEP)
    v_hat = v / (1.0 - ADAM_B2 ** ADAM_STEP)
    delta = -ADAM_LR * (m_hat / (_jnp.sqrt(v_hat) + ADAM_EPS) + ADAM_WD * w)
    return delta, m, v


def reference(x, norm_g, w_in, conv_a_w, conv_r_w, conv_r_b, lru_wa, lru_ba, lru_wx, lru_bx, lru_lambda, gmlp_norm_g, gmlp_ws, gmlp_bs, w_out, final_g, loss_target, m_norm_g, m_w_in, m_conv_a_w, m_conv_r_w, m_conv_r_b, m_lru_wa, m_lru_ba, m_lru_wx, m_lru_bx, m_lru_lambda, m_gmlp_norm_g, m_gmlp_ws, m_gmlp_bs, m_w_out, m_final_g, v_norm_g, v_w_in, v_conv_a_w, v_conv_r_w, v_conv_r_b, v_lru_wa, v_lru_ba, v_lru_wx, v_lru_bx, v_lru_lambda, v_gmlp_norm_g, v_gmlp_ws, v_gmlp_bs, v_w_out, v_final_g):
    given = dict(x=x, norm_g=norm_g, w_in=w_in, conv_a_w=conv_a_w, conv_r_w=conv_r_w, conv_r_b=conv_r_b, lru_wa=lru_wa, lru_ba=lru_ba, lru_wx=lru_wx, lru_bx=lru_bx, lru_lambda=lru_lambda, gmlp_norm_g=gmlp_norm_g, gmlp_ws=gmlp_ws, gmlp_bs=gmlp_bs, w_out=w_out, final_g=final_g, loss_target=loss_target, m_norm_g=m_norm_g, m_w_in=m_w_in, m_conv_a_w=m_conv_a_w, m_conv_r_w=m_conv_r_w, m_conv_r_b=m_conv_r_b, m_lru_wa=m_lru_wa, m_lru_ba=m_lru_ba, m_lru_wx=m_lru_wx, m_lru_bx=m_lru_bx, m_lru_lambda=m_lru_lambda, m_gmlp_norm_g=m_gmlp_norm_g, m_gmlp_ws=m_gmlp_ws, m_gmlp_bs=m_gmlp_bs, m_w_out=m_w_out, m_final_g=m_final_g, v_norm_g=v_norm_g, v_w_in=v_w_in, v_conv_a_w=v_conv_a_w, v_conv_r_w=v_conv_r_w, v_conv_r_b=v_conv_r_b, v_lru_wa=v_lru_wa, v_lru_ba=v_lru_ba, v_lru_wx=v_lru_wx, v_lru_bx=v_lru_bx, v_lru_lambda=v_lru_lambda, v_gmlp_norm_g=v_gmlp_norm_g, v_gmlp_ws=v_gmlp_ws, v_gmlp_bs=v_gmlp_bs, v_w_out=v_w_out, v_final_g=v_final_g)
    weights = {n: given[n] for n in TWIN_WEIGHTS}
    shared = {n: given[n] for n in SHARED_INPUTS}
    per_example = {n: given[n] for n in ['x']}
    grad_fn = _jax.value_and_grad(_loss, argnums=(0, 1))

    def one_microbatch(ex, loss_target):
        ex = dict(ex)
        diff = ex.pop(TWIN_DIFF_INPUT)
        return grad_fn(weights, diff, {**shared, **ex}, loss_target)

    if N_MICROBATCH == 1:
        loss, (grad_w, grad_x) = one_microbatch(per_example, given["loss_target"])
    else:
        def body(carry, xs):
            loss_sum, grad_sum = carry
            l_k, (gw_k, gx_k) = one_microbatch(xs[0], xs[1])
            with _jax.named_scope("update"):
                return (loss_sum + l_k, _jax.tree.map(_jnp.add, grad_sum, gw_k)), gx_k

        init = (_jnp.zeros((), _jnp.float32), _jax.tree.map(_jnp.zeros_like, weights))
        (loss, grad_w), grad_x = _jax.lax.scan(body, init, (per_example, given["loss_target"]))
    with _jax.named_scope("update"):
        delta_w, new_m, new_v = {}, {}, {}
        for n in TWIN_WEIGHTS:
            delta_w[n], new_m[n], new_v[n] = _adamw(weights[n], grad_w[n], given["m_" + n], given["v_" + n])
    return (loss, grad_x, *[grad_w[n] for n in TWIN_WEIGHTS], *[delta_w[n] for n in TWIN_WEIGHTS],
            *[new_m[n] for n in TWIN_WEIGHTS], *[new_v[n] for n in TWIN_WEIGHTS])
```

```python
import functools
import math

import jax
import jax.numpy as jnp
from jax import lax
from jax.experimental import pallas as pl
from jax.experimental.pallas import tpu as pltpu

F32 = jnp.float32
BF16 = jnp.bfloat16

D_MODEL = 1024
GROUP_W = 256
N_HEADS = 4
HEAD_DIM = 64
N_CHUNKS = 13
D_IN = N_CHUNKS * GROUP_W
D_MIX = 4 * GROUP_W
NORM_EPS = 1e-6
RG_C = 8.0
GMLP_CHUNK = 128
ATTN_BLOCK = 128
PATTERN_DILS = (1, 4, 16)
N_PATTERNS = len(PATTERN_DILS)
ALIBI_SLOPES = tuple(2.0 ** (-8.0 * (h + 1) / N_HEADS) for h in range(N_HEADS))
ATTN_SCALE = 1.0 / math.sqrt(HEAD_DIM)
NEG_BIG = -1e30

ADAM_LR = 0.001
ADAM_B1 = 0.9
ADAM_B2 = 0.999
ADAM_EPS = 1e-08
ADAM_WD = 0.01
ADAM_STEP = 10

C_AX, C_AB, C_AC, C_AG, C_RX, C_RG, C_CU, C_CV, C_CG, C_DQ, C_DK, C_DV, C_DG = range(13)

SUBLANES = 8
VMEM_LIMIT = 56 * 1024 * 1024
TILE_IN = 256
TILE_MIX = 256
TILE_DW = 512
ATTN_QB = 4
GELU_K0 = math.sqrt(2.0 / math.pi)
GELU_K1 = 0.044715


def _params(sem):
    return pltpu.CompilerParams(dimension_semantics=sem, vmem_limit_bytes=VMEM_LIMIT)


def _silu(x):
    return x * jax.nn.sigmoid(x)


def _dsilu(x):
    s = jax.nn.sigmoid(x)
    return s * (1.0 + x * (1.0 - s))


def _gelu(x):
    return 0.5 * x * (1.0 + jnp.tanh(GELU_K0 * (x + GELU_K1 * x * x * x)))


def _gelu_and_grad(x):
    t = jnp.tanh(GELU_K0 * (x + GELU_K1 * x * x * x))
    g = 0.5 * x * (1.0 + t)
    dg = 0.5 * (1.0 + t) + 0.5 * x * (1.0 - t * t) * GELU_K0 * (1.0 + 3.0 * GELU_K1 * x * x)
    return g, dg


def _neg_expm1(x):
    series = x * (1.0 + x * (0.5 + x * (1.0 / 6.0 + x * (1.0 / 24.0 + x * (1.0 / 120.0)))))
    return -jnp.where(x > -0.05, series, jnp.exp(x) - 1.0)


def _shift_down(v, halo, k):
    r = pltpu.roll(v, k, 0)
    rh = pltpu.roll(halo, k, 0)
    row = lax.broadcasted_iota(jnp.int32, halo.shape, 0)
    top = jnp.where(row < k, rh, r[:SUBLANES])
    return jnp.concatenate([top, r[SUBLANES:]], axis=0)


def _shift_up(v, halo, k):
    t = v.shape[0]
    r = pltpu.roll(v, t - k, 0)
    rh = pltpu.roll(halo, SUBLANES - k, 0)
    row = lax.broadcasted_iota(jnp.int32, halo.shape, 0)
    bot = jnp.where(row >= SUBLANES - k, rh, r[t - SUBLANES:])
    return jnp.concatenate([r[:t - SUBLANES], bot], axis=0)


def _scan_causal(a, b):
    t = a.shape[0]
    row = lax.broadcasted_iota(jnp.int32, a.shape, 0)
    d = 1
    while d < t:
        m = row >= d
        a_s = jnp.where(m, pltpu.roll(a, d, 0), 1.0)
        b_s = jnp.where(m, pltpu.roll(b, d, 0), 0.0)
        b = a * b_s + b
        a = a * a_s
        d *= 2
    return a, b


def _scan_anticausal(a, b):
    t = a.shape[0]
    row = lax.broadcasted_iota(jnp.int32, a.shape, 0)
    d = 1
    while d < t:
        m = row < t - d
        a_s = jnp.where(m, pltpu.roll(a, t - d, 0), 1.0)
        b_s = jnp.where(m, pltpu.roll(b, t - d, 0), 0.0)
        b = a * b_s + b
        a = a * a_s
        d *= 2
    return a, b


def _head_of_lane(shape):
    return lax.broadcasted_iota(jnp.int32, shape, len(shape) - 1) // HEAD_DIM


def _put_row(acc_shape, k, row_vec):
    row = lax.broadcasted_iota(jnp.int32, acc_shape, 0)
    return jnp.where(row == k, jnp.broadcast_to(row_vec, acc_shape), 0.0)


def _dot(a, b):
    return jnp.dot(a, b, preferred_element_type=F32)


def _dot_nt(a, b):
    return lax.dot_general(a, b, (((1,), (1,)), ((), ())), preferred_element_type=F32)


def _dot_tn(a, b):
    return lax.dot_general(a, b, (((0,), (0,)), ((), ())), preferred_element_type=F32)


def in_fwd(x, g, w):
    s = x.shape[0]

    def body(x_ref, g_ref, w_ref, z_ref, h_ref):
        xv = x_ref[...]
        rs = lax.rsqrt(jnp.mean(xv * xv, axis=-1, keepdims=True) + NORM_EPS)
        h = (xv * rs * g_ref[...]).astype(BF16)
        h_ref[...] = h
        z_ref[...] = _dot(h, w_ref[...])

    return pl.pallas_call(
        body, name="in_fwd", grid=(s // TILE_IN,),
        in_specs=[pl.BlockSpec((TILE_IN, D_MODEL), lambda i: (i, 0)),
                  pl.BlockSpec((1, D_MODEL), lambda i: (0, 0)),
                  pl.BlockSpec((D_MODEL, D_IN), lambda i: (0, 0))],
        out_specs=[pl.BlockSpec((TILE_IN, D_IN), lambda i: (i, 0)),
                   pl.BlockSpec((TILE_IN, D_MODEL), lambda i: (i, 0))],
        out_shape=[jax.ShapeDtypeStruct((s, D_IN), F32), jax.ShapeDtypeStruct((s, D_MODEL), BF16)],
        compiler_params=_params(("parallel",)),
    )(x, g, w)


def out_fwd(y, w, x):
    s = x.shape[0]

    def body(y_ref, w_ref, x_ref, o_ref):
        o_ref[...] = x_ref[...] + _dot(y_ref[...], w_ref[...])

    return pl.pallas_call(
        body, name="out_fwd", grid=(s // TILE_IN,),
        in_specs=[pl.BlockSpec((TILE_IN, D_MIX), lambda i: (i, 0)),
                  pl.BlockSpec((D_MIX, D_MODEL), lambda i: (0, 0)),
                  pl.BlockSpec((TILE_IN, D_MODEL), lambda i: (i, 0))],
        out_specs=pl.BlockSpec((TILE_IN, D_MODEL), lambda i: (i, 0)),
        out_shape=jax.ShapeDtypeStruct((s, D_MODEL), F32),
        compiler_params=_params(("parallel",)),
    )(y, w, x)


def out_bwd(dx, w):
    s = dx.shape[0]

    def body(dx_ref, w_ref, o_ref):
        o_ref[...] = _dot_nt(dx_ref[...].astype(BF16), w_ref[...])

    return pl.pallas_call(
        body, name="out_bwd", grid=(s // TILE_IN,),
        in_specs=[pl.BlockSpec((TILE_IN, D_MODEL), lambda i: (i, 0)),
                  pl.BlockSpec((D_MIX, D_MODEL), lambda i: (0, 0))],
        out_specs=pl.BlockSpec((TILE_IN, D_MIX), lambda i: (i, 0)),
        out_shape=jax.ShapeDtypeStruct((s, D_MIX), F32),
        compiler_params=_params(("parallel",)),
    )(dx, w)


def in_bwd(dz, w, x, g, dx_next):
    s = x.shape[0]

    def body(dz_ref, w_ref, x_ref, g_ref, dxn_ref, dx_ref, dg_ref):
        @pl.when(pl.program_id(0) == 0)
        def _():
            dg_ref[...] = jnp.zeros_like(dg_ref)

        dh = _dot_nt(dz_ref[...], w_ref[...])
        xv = x_ref[...]
        rs = lax.rsqrt(jnp.mean(xv * xv, axis=-1, keepdims=True) + NORM_EPS)
        xh = xv * rs
        dg_ref[...] += _put_row(dg_ref.shape, 0, jnp.sum(dh * xh, axis=0, keepdims=True))
        dn = dh * g_ref[...]
        dx_ref[...] = dxn_ref[...] + rs * (dn - xh * jnp.mean(dn * xh, axis=-1, keepdims=True))

    return pl.pallas_call(
        body, name="in_bwd", grid=(s // TILE_IN,),
        in_specs=[pl.BlockSpec((TILE_IN, D_IN), lambda i: (i, 0)),
                  pl.BlockSpec((D_MODEL, D_IN), lambda i: (0, 0)),
                  pl.BlockSpec((TILE_IN, D_MODEL), lambda i: (i, 0)),
                  pl.BlockSpec((1, D_MODEL), lambda i: (0, 0)),
                  pl.BlockSpec((TILE_IN, D_MODEL), lambda i: (i, 0))],
        out_specs=[pl.BlockSpec((TILE_IN, D_MODEL), lambda i: (i, 0)),
                   pl.BlockSpec((SUBLANES, D_MODEL), lambda i: (0, 0))],
        out_shape=[jax.ShapeDtypeStruct((s, D_MODEL), F32), jax.ShapeDtypeStruct((SUBLANES, D_MODEL), F32)],
        compiler_params=_params(("arbitrary",)),
    )(dz, w, x, g, dx_next)


def matmul_tn(a, b, n_split):
    s, m = a.shape
    n = b.shape[1]
    tn = n // n_split

    def body(a_ref, b_ref, o_ref):
        @pl.when(pl.program_id(1) == 0)
        def _():
            o_ref[...] = jnp.zeros_like(o_ref)

        o_ref[...] += _dot_tn(a_ref[...], b_ref[...].astype(BF16))

    return pl.pallas_call(
        body, name="matmul_tn", grid=(n_split, s // TILE_DW),
        in_specs=[pl.BlockSpec((TILE_DW, m), lambda j, k: (k, 0)),
                  pl.BlockSpec((TILE_DW, tn), lambda j, k: (k, j))],
        out_specs=pl.BlockSpec((m, tn), lambda j, k: (0, j)),
        out_shape=jax.ShapeDtypeStruct((m, n), F32),
        compiler_params=_params(("parallel", "arbitrary")),
    )(a, b)


def loss_head(x, g, tgt):
    s = x.shape[0]

    def body(x_ref, g_ref, t_ref, l_ref, dx_ref, dg_ref):
        @pl.when(pl.program_id(0) == 0)
        def _():
            l_ref[...] = jnp.zeros_like(l_ref)
            dg_ref[...] = jnp.zeros_like(dg_ref)

        xv = x_ref[...]
        gv = g_ref[...]
        rs = lax.rsqrt(jnp.mean(xv * xv, axis=-1, keepdims=True) + NORM_EPS)
        xh = xv * rs
        e = xh * gv - t_ref[...]
        part = 0.5 * jnp.sum(jnp.mean(e * e, axis=-1, keepdims=True), axis=0, keepdims=True)
        l_ref[...] += jnp.broadcast_to(part, l_ref.shape)
        dy = e * (1.0 / D_MODEL)
        dg_ref[...] += _put_row(dg_ref.shape, 0, jnp.sum(dy * xh, axis=0, keepdims=True))
        dn = dy * gv
        dx_ref[...] = rs * (dn - xh * jnp.mean(dn * xh, axis=-1, keepdims=True))

    return pl.pallas_call(
        body, name="loss_head", grid=(s // TILE_IN,),
        in_specs=[pl.BlockSpec((TILE_IN, D_MODEL), lambda i: (i, 0)),
                  pl.BlockSpec((1, D_MODEL), lambda i: (0, 0)),
                  pl.BlockSpec((TILE_IN, D_MODEL), lambda i: (i, 0))],
        out_specs=[pl.BlockSpec((SUBLANES, 128), lambda i: (0, 0)),
                   pl.BlockSpec((TILE_IN, D_MODEL), lambda i: (i, 0)),
                   pl.BlockSpec((SUBLANES, D_MODEL), lambda i: (0, 0))],
        out_shape=[jax.ShapeDtypeStruct((SUBLANES, 128), F32), jax.ShapeDtypeStruct((s, D_MODEL), F32),
                   jax.ShapeDtypeStruct((SUBLANES, D_MODEL), F32)],
        compiler_params=_params(("arbitrary",)),
    )(x, g, tgt)


def _deinterleave(a, dil):
    s, c = a.shape
    return a.reshape(s // dil, dil, c).transpose(1, 0, 2).reshape(s, c)


def _interleave(a, dil):
    s, c = a.shape
    return a.reshape(dil, s // dil, c).transpose(1, 0, 2).reshape(s, c)


def _pattern_scalars(p, n_blocks):
    dil = jnp.where(p == 0, PATTERN_DILS[0], jnp.where(p == 1, PATTERN_DILS[1], PATTERN_DILS[2]))
    return dil, n_blocks // dil


def _band(first, nk):
    qi = lax.broadcasted_iota(jnp.int32, (ATTN_BLOCK, nk), 0)
    ki = lax.broadcasted_iota(jnp.int32, (ATTN_BLOCK, nk), 1)
    delta = qi + ATTN_BLOCK - ki
    ok = (delta >= 0) & (delta <= ATTN_BLOCK) & ((ki >= ATTN_BLOCK) | jnp.logical_not(first))
    return ok, delta.astype(F32)


def attn_fwd(q, k, v):
    n_pat, s, _ = q.shape
    n_blocks = s // ATTN_BLOCK
    rows = ATTN_QB * ATTN_BLOCK

    def body(q_ref, kp_ref, k_ref, vp_ref, v_ref, o_ref, lse_ref):
        p = pl.program_id(0)
        n = pl.program_id(1)
        dil, bps = _pattern_scalars(p, n_blocks)
        dil_f = dil.astype(F32)
        head = _head_of_lane((ATTN_BLOCK, GROUP_W))
        for j in range(ATTN_QB):
            first = ((n * ATTN_QB + j) % bps) == 0
            ok, delta = _band(first, 2 * ATTN_BLOCK)
            qj = q_ref[0, j * ATTN_BLOCK:(j + 1) * ATTN_BLOCK, :]
            if j == 0:
                kprev, vprev = kp_ref[0], vp_ref[0]
            else:
                kprev = k_ref[0, (j - 1) * ATTN_BLOCK:j * ATTN_BLOCK, :]
                vprev = v_ref[0, (j - 1) * ATTN_BLOCK:j * ATTN_BLOCK, :]
            kk = jnp.concatenate([kprev, k_ref[0, j * ATTN_BLOCK:(j + 1) * ATTN_BLOCK, :]], axis=0)
            vv = jnp.concatenate([vprev, v_ref[0, j * ATTN_BLOCK:(j + 1) * ATTN_BLOCK, :]], axis=0)
            o_acc = jnp.zeros((ATTN_BLOCK, GROUP_W), F32)
            l_acc = jnp.zeros((ATTN_BLOCK, GROUP_W), F32)
            for h in range(N_HEADS):
                qm = jnp.where(head == h, qj, jnp.zeros_like(qj))
                sc = _dot_nt(qm, kk) * ATTN_SCALE - (ALIBI_SLOPES[h] * dil_f) * delta
                sc = jnp.where(ok, sc, NEG_BIG)
                m = jnp.max(sc, axis=-1, keepdims=True)
                pr = jnp.exp(sc - m)
                l = jnp.sum(pr, axis=-1, keepdims=True)
                oh = _dot(pr.astype(BF16), vv) / l
                o_acc = jnp.where(head == h, oh, o_acc)
                l_acc = jnp.where(head == h, m + jnp.log(l), l_acc)
            o_ref[0, j * ATTN_BLOCK:(j + 1) * ATTN_BLOCK, :] = o_acc
            lse_ref[0, j * ATTN_BLOCK:(j + 1) * ATTN_BLOCK, :] = l_acc

    cur = pl.BlockSpec((1, rows, GROUP_W), lambda p, n: (p, n, 0))
    prev = pl.BlockSpec((1, ATTN_BLOCK, GROUP_W), lambda p, n: (p, jnp.maximum(n * ATTN_QB - 1, 0), 0))
    return pl.pallas_call(
        body, name="attn_fwd", grid=(n_pat, n_blocks // ATTN_QB),
        in_specs=[cur, prev, cur, prev, cur],
        out_specs=[cur, cur],
        out_shape=[jax.ShapeDtypeStruct((n_pat, s, GROUP_W), F32), jax.ShapeDtypeStruct((n_pat, s, GROUP_W), F32)],
        compiler_params=_params(("parallel", "parallel")),
    )(q, k, k, v, v)


def attn_bwd(q, k, v, do, lse, dlt):
    n_pat, s, _ = q.shape
    n_blocks = s // ATTN_BLOCK
    rows = ATTN_QB * ATTN_BLOCK

    def body(q_ref, qn_ref, kp_ref, k_ref, vp_ref, v_ref, do_ref, don_ref, lse_ref, lsen_ref, dl_ref, dln_ref,
             dq_ref, dk_ref, dv_ref, dk_acc, dv_acc):
        p = pl.program_id(0)
        n = pl.program_id(1)
        dil, bps = _pattern_scalars(p, n_blocks)
        dil_f = dil.astype(F32)
        head = _head_of_lane((ATTN_BLOCK, GROUP_W))
        dk_acc[...] = jnp.zeros_like(dk_acc)
        dv_acc[...] = jnp.zeros_like(dv_acc)

        def pair(qj, doj, lsej, dlj, kk, vv, ok, delta, want_dq):
            dq = jnp.zeros((ATTN_BLOCK, GROUP_W), F32)
            dks = jnp.zeros((kk.shape[0], GROUP_W), F32)
            dvs = jnp.zeros((kk.shape[0], GROUP_W), F32)
            for h in range(N_HEADS):
                qm = jnp.where(head == h, qj, jnp.zeros_like(qj))
                dom = jnp.where(head == h, doj, jnp.zeros_like(doj))
                lse_h = lsej[:, h * HEAD_DIM:h * HEAD_DIM + 1]
                dl_h = dlj[:, h * HEAD_DIM:h * HEAD_DIM + 1]
                sc = _dot_nt(qm, kk) * ATTN_SCALE - (ALIBI_SLOPES[h] * dil_f) * delta
                pr = jnp.where(ok, jnp.exp(jnp.where(ok, sc, NEG_BIG) - lse_h), 0.0)
                dp = _dot_nt(dom, vv)
                ds = (pr * (dp - dl_h) * ATTN_SCALE).astype(BF16)
                if want_dq:
                    dq = jnp.where(head == h, _dot(ds, kk), dq)
                dks = dks + _dot_tn(ds, qm)
                dvs = dvs + _dot_tn(pr.astype(BF16), dom)
            return dq, dks, dvs

        for j in range(ATTN_QB):
            sl = slice(j * ATTN_BLOCK, (j + 1) * ATTN_BLOCK)
            first = ((n * ATTN_QB + j) % bps) == 0
            ok, delta = _band(first, 2 * ATTN_BLOCK)
            if j == 0:
                kprev, vprev = kp_ref[0], vp_ref[0]
            else:
                kprev = k_ref[0, (j - 1) * ATTN_BLOCK:j * ATTN_BLOCK, :]
                vprev = v_ref[0, (j - 1) * ATTN_BLOCK:j * ATTN_BLOCK, :]
            kk = jnp.concatenate([kprev, k_ref[0, sl, :]], axis=0)
            vv = jnp.concatenate([vprev, v_ref[0, sl, :]], axis=0)
            dq, dks, dvs = pair(q_ref[0, sl, :], do_ref[0, sl, :], lse_ref[0, sl, :], dl_ref[0, sl, :],
                                kk, vv, ok, delta, True)
            dq_ref[0, sl, :] = dq
            acc = slice(j * ATTN_BLOCK, (j + 2) * ATTN_BLOCK)
            dk_acc[acc, :] += dks
            dv_acc[acc, :] += dvs

        nxt = n * ATTN_QB + ATTN_QB
        valid = (nxt < n_blocks) & ((nxt % bps) != 0)
        ok, delta = _band(jnp.logical_not(valid), ATTN_BLOCK)
        ok = ok & valid
        last = slice((ATTN_QB - 1) * ATTN_BLOCK, ATTN_QB * ATTN_BLOCK)
        _, dks, dvs = pair(qn_ref[0], don_ref[0], lsen_ref[0], dln_ref[0], k_ref[0, last, :], v_ref[0, last, :],
                           ok, delta, False)
        acc = slice(ATTN_QB * ATTN_BLOCK, (ATTN_QB + 1) * ATTN_BLOCK)
        dk_acc[acc, :] += dks
        dv_acc[acc, :] += dvs
        dk_ref[0] = dk_acc[ATTN_BLOCK:, :]
        dv_ref[0] = dv_acc[ATTN_BLOCK:, :]

    cur = pl.BlockSpec((1, rows, GROUP_W), lambda p, n: (p, n, 0))
    prev = pl.BlockSpec((1, ATTN_BLOCK, GROUP_W), lambda p, n: (p, jnp.maximum(n * ATTN_QB - 1, 0), 0))
    nxt = pl.BlockSpec((1, ATTN_BLOCK, GROUP_W),
                       lambda p, n: (p, jnp.minimum(n * ATTN_QB + ATTN_QB, n_blocks - 1), 0))
    out = jax.ShapeDtypeStruct((n_pat, s, GROUP_W), F32)
    return pl.pallas_call(
        body, name="attn_bwd", grid=(n_pat, n_blocks // ATTN_QB),
        in_specs=[cur, nxt, prev, cur, prev, cur, cur, nxt, cur, nxt, cur, nxt],
        out_specs=[cur, cur, cur],
        out_shape=[out, out, out],
        scratch_shapes=[pltpu.VMEM(((ATTN_QB + 1) * ATTN_BLOCK, GROUP_W), F32),
                        pltpu.VMEM(((ATTN_QB + 1) * ATTN_BLOCK, GROUP_W), F32)],
        compiler_params=_params(("parallel", "parallel")),
    )(q, q, k, k, v, v, do, do, lse, lse, dlt, dlt)


def _zcol(c):
    return pl.BlockSpec((TILE_MIX, GROUP_W), lambda i, c=c: (i, c))


def _zhalo(c):
    per = TILE_MIX // SUBLANES
    return pl.BlockSpec((SUBLANES, GROUP_W), lambda i, c=c: (jnp.maximum(i * per - 1, 0), c))


def _full(shape):
    return pl.BlockSpec(shape, lambda i: tuple(0 for _ in shape))


def _lru_gates(xb, wa_ref, wx_ref, ba, bx, lam):
    xbb = xb.astype(BF16)
    r = jax.nn.sigmoid(_dot(xbb, wa_ref[...]) + ba)
    ig = jax.nn.sigmoid(_dot(xbb, wx_ref[...]) + bx)
    nl = -lam
    sp = jnp.maximum(nl, 0.0) + jnp.log1p(jnp.exp(-jnp.abs(nl)))
    log_a = (-RG_C * r) * sp
    a = jnp.exp(log_a)
    mult = jnp.sqrt(_neg_expm1(2.0 * log_a))
    return r, ig, sp, a, mult


def _gmlp_spatial(ws_ref, vvb, head):
    outs = []
    for j in range(vvb.shape[0] // GMLP_CHUNK):
        blk = vvb[j * GMLP_CHUNK:(j + 1) * GMLP_CHUNK, :]
        acc = jnp.zeros((GMLP_CHUNK, GROUP_W), F32)
        for h in range(N_HEADS):
            acc = jnp.where(head[:GMLP_CHUNK] == h, _dot(ws_ref[h], blk), acc)
        outs.append(acc)
    return jnp.concatenate(outs, axis=0)


def mix_fwd(z, op, lsep, wts):
    s = z.shape[0]
    n_pat = op.shape[0]

    def body(ax_ref, ab_ref, ac_ref, ag_ref, rx_ref, rg_ref, cu_ref, cv_ref, cg_ref, dg_ref,
             axh_ref, ach_ref, rxh_ref, op_ref, lp_ref,
             caw_ref, crw_ref, crb_ref, wa_ref, wx_ref, ba_ref, bx_ref, lam_ref, gng_ref, ws_ref, bs_ref,
             y_ref, hl_ref, o_ref, lse_ref, carry):
        i = pl.program_id(0)

        @pl.when(i == 0)
        def _():
            carry[...] = jnp.zeros_like(carry)

        nz = (i > 0).astype(F32)
        head = _head_of_lane((TILE_MIX, GROUP_W))

        pa = ac_ref[...] * ax_ref[...]
        pah = ach_ref[...] * axh_ref[...] * nz
        cv = caw_ref[2:3, :] * pa + caw_ref[1:2, :] * _shift_down(pa, pah, 1) + caw_ref[0:1, :] * _shift_down(pa, pah, 2)
        y_ref[:, 0:GROUP_W] = (ab_ref[...] * cv * _silu(ag_ref[...])).astype(BF16)

        rx = rx_ref[...]
        rxh = rxh_ref[...] * nz
        xb = (crw_ref[3:4, :] * rx + crw_ref[2:3, :] * _shift_down(rx, rxh, 1) + crw_ref[1:2, :] * _shift_down(rx, rxh, 2)
              + crw_ref[0:1, :] * _shift_down(rx, rxh, 3) + crb_ref[...])
        _, ig, _, a, mult = _lru_gates(xb, wa_ref, wx_ref, ba_ref[...], bx_ref[...], lam_ref[...])
        ca, cb = _scan_causal(a, mult * (ig * xb))
        hl = cb + ca * carry[SUBLANES - 1:SUBLANES, :]
        hl_ref[...] = hl
        carry[...] = hl[TILE_MIX - SUBLANES:, :]
        y_ref[:, GROUP_W:2 * GROUP_W] = (hl * _silu(rg_ref[...])).astype(BF16)

        u = _gelu(cu_ref[...])
        gv = _gelu(cv_ref[...])
        rs = lax.rsqrt(jnp.mean(gv * gv, axis=-1, keepdims=True) + NORM_EPS)
        vvb = (gv * rs * gng_ref[...]).astype(BF16)
        sp = _gmlp_spatial(ws_ref, vvb, head) + jnp.concatenate([bs_ref[...]] * (TILE_MIX // GMLP_CHUNK), axis=0)
        y_ref[:, 2 * GROUP_W:3 * GROUP_W] = (u * sp * _silu(cg_ref[...])).astype(BF16)

        m = lp_ref[0]
        for p in range(1, n_pat):
            m = jnp.maximum(m, lp_ref[p])
        zsum = jnp.zeros_like(m)
        o = jnp.zeros_like(m)
        for p in range(n_pat):
            w = jnp.exp(lp_ref[p] - m)
            zsum = zsum + w
            o = o + w * op_ref[p]
        o = o / zsum
        o_ref[...] = o
        lse_ref[...] = m + jnp.log(zsum)
        y_ref[:, 3 * GROUP_W:4 * GROUP_W] = (o * _silu(dg_ref[...])).astype(BF16)

    pat = pl.BlockSpec((n_pat, TILE_MIX, GROUP_W), lambda i: (0, i, 0))
    row = pl.BlockSpec((TILE_MIX, GROUP_W), lambda i: (i, 0))
    in_specs = ([_zcol(c) for c in (C_AX, C_AB, C_AC, C_AG, C_RX, C_RG, C_CU, C_CV, C_CG, C_DG)]
                + [_zhalo(C_AX), _zhalo(C_AC), _zhalo(C_RX), pat, pat]
                + [_full(wts[k].shape) for k in ("caw", "crw", "crb", "wa", "wx", "ba", "bx", "lam", "gng", "ws", "bs")])
    return pl.pallas_call(
        body, name="mix_fwd", grid=(s // TILE_MIX,),
        in_specs=in_specs,
        out_specs=[pl.BlockSpec((TILE_MIX, D_MIX), lambda i: (i, 0)), row, row, row],
        out_shape=[jax.ShapeDtypeStruct((s, D_MIX), BF16)] + [jax.ShapeDtypeStruct((s, GROUP_W), F32)] * 3,
        scratch_shapes=[pltpu.VMEM((SUBLANES, GROUP_W), F32)],
        compiler_params=_params(("arbitrary",)),
    )(*([z] * 13), op, lsep,
      *[wts[k] for k in ("caw", "crw", "crb", "wa", "wx", "ba", "bx", "lam", "gng", "ws", "bs")])


def attn_pre_bwd(dy, z, o):
    s = z.shape[0]

    def body(dy_ref, dg_ref, o_ref, do_ref, dl_ref, ddg_ref):
        head = _head_of_lane((TILE_MIX, GROUP_W))
        dg = dg_ref[...]
        o = o_ref[...]
        dyd = dy_ref[...]
        do = dyd * _silu(dg)
        ddg_ref[...] = dyd * o * _dsilu(dg)
        do_ref[...] = do.astype(BF16)
        prod = do * o
        dl = jnp.zeros_like(prod)
        for h in range(N_HEADS):
            sm = jnp.sum(jnp.where(head == h, prod, 0.0), axis=-1, keepdims=True)
            dl = jnp.where(head == h, sm, dl)
        dl_ref[...] = dl

    row = pl.BlockSpec((TILE_MIX, GROUP_W), lambda i: (i, 0))
    return pl.pallas_call(
        body, name="attn_pre_bwd", grid=(s // TILE_MIX,),
        in_specs=[pl.BlockSpec((TILE_MIX, GROUP_W), lambda i: (i, 3)), _zcol(C_DG), row],
        out_specs=[row, row, row],
        out_shape=[jax.ShapeDtypeStruct((s, GROUP_W), BF16), jax.ShapeDtypeStruct((s, GROUP_W), F32),
                   jax.ShapeDtypeStruct((s, GROUP_W), F32)],
        compiler_params=_params(("parallel",)),
    )(dy, z, o)


def mix_bwd(dy, z, hl, dqkv, ddg, wts):
    s = z.shape[0]
    n_pat = dqkv.shape[0]
    n_tiles = s // TILE_MIX

    def body(dya_ref, dyb_ref, dyc_ref, ax_ref, ab_ref, ac_ref, ag_ref, rx_ref, rg_ref, cu_ref, cv_ref, cg_ref,
             axh_ref, ach_ref, rxh_ref, hl_ref, hlh_ref, dqkv_ref, ddg_ref,
             caw_ref, crw_ref, crb_ref, wa_ref, wx_ref, ba_ref, bx_ref, lam_ref, gng_ref, ws_ref, wst_ref, bs_ref,
             dz_ref, ga_ref, gr_ref, gn_ref, gwa_ref, gwx_ref, gws_ref, gbs_ref,
             c_dcv, c_g, c_a, c_dxb):
        step = pl.program_id(0)
        i = n_tiles - 1 - step

        @pl.when(step == 0)
        def _():
            for r in (c_dcv, c_g, c_a, c_dxb, ga_ref, gr_ref, gn_ref, gwa_ref, gwx_ref, gws_ref, gbs_ref):
                r[...] = jnp.zeros_like(r)

        nz = (i > 0).astype(F32)
        head = _head_of_lane((TILE_MIX, GROUP_W))
        shp8 = (SUBLANES, GROUP_W)
        colsum = lambda v: jnp.sum(v, axis=0, keepdims=True)

        ax, ab, ac, ag = ax_ref[...], ab_ref[...], ac_ref[...], ag_ref[...]
        dya = dya_ref[...]
        pa = ac * ax
        pah = ach_ref[...] * axh_ref[...] * nz
        pa1 = _shift_down(pa, pah, 1)
        pa2 = _shift_down(pa, pah, 2)
        cv = caw_ref[2:3, :] * pa + caw_ref[1:2, :] * pa1 + caw_ref[0:1, :] * pa2
        sg = _silu(ag)
        dz_ref[:, C_AB * GROUP_W:(C_AB + 1) * GROUP_W] = (dya * cv * sg).astype(BF16)
        dz_ref[:, C_AG * GROUP_W:(C_AG + 1) * GROUP_W] = (dya * ab * cv * _dsilu(ag)).astype(BF16)
        dcv = dya * ab * sg
        nxt = c_dcv[...]
        dpa = caw_ref[2:3, :] * dcv + caw_ref[1:2, :] * _shift_up(dcv, nxt, 1) + caw_ref[0:1, :] * _shift_up(dcv, nxt, 2)
        c_dcv[...] = dcv[:SUBLANES, :]
        dz_ref[:, C_AC * GROUP_W:(C_AC + 1) * GROUP_W] = (dpa * ax).astype(BF16)
        dz_ref[:, C_AX * GROUP_W:(C_AX + 1) * GROUP_W] = (dpa * ac).astype(BF16)
        ga_ref[...] += (_put_row(shp8, 2, colsum(dcv * pa)) + _put_row(shp8, 1, colsum(dcv * pa1))
                        + _put_row(shp8, 0, colsum(dcv * pa2)))

        rx, rg = rx_ref[...], rg_ref[...]
        dyb = dyb_ref[...]
        rxh = rxh_ref[...] * nz
        rx1, rx2, rx3 = _shift_down(rx, rxh, 1), _shift_down(rx, rxh, 2), _shift_down(rx, rxh, 3)
        xb = crw_ref[3:4, :] * rx + crw_ref[2:3, :] * rx1 + crw_ref[1:2, :] * rx2 + crw_ref[0:1, :] * rx3 + crb_ref[...]
        lam = lam_ref[...]
        r, ig, sp, a, mult = _lru_gates(xb, wa_ref, wx_ref, ba_ref[...], bx_ref[...], lam)
        hl = hl_ref[...]
        hprev = _shift_down(hl, hlh_ref[...] * nz, 1)
        dz_ref[:, C_RG * GROUP_W:(C_RG + 1) * GROUP_W] = (dyb * hl * _dsilu(rg)).astype(BF16)
        dh = dyb * _silu(rg)
        a_next = _shift_up(a, c_a[...], 1)
        ca, cb = _scan_anticausal(a_next, dh)
        g = cb + ca * c_g[0:1, :]
        c_g[...] = g[:SUBLANES, :]
        c_a[...] = a[:SUBLANES, :]
        u = ig * xb
        da = g * hprev
        dmult = g * u
        du = g * mult
        dlog_a = da * a - dmult * (a * a) / mult
        dr = dlog_a * (-RG_C * sp)
        dga = dr * r * (1.0 - r)
        dgx = (du * xb) * ig * (1.0 - ig)
        dgab, dgxb = dga.astype(BF16), dgx.astype(BF16)
        dxb = du * ig + _dot_nt(dgab, wa_ref[...]) + _dot_nt(dgxb, wx_ref[...])
        xbb = xb.astype(BF16)
        gwa_ref[...] += _dot_tn(xbb, dgab)
        gwx_ref[...] += _dot_tn(xbb, dgxb)
        nxt = c_dxb[...]
        drx = (crw_ref[3:4, :] * dxb + crw_ref[2:3, :] * _shift_up(dxb, nxt, 1) + crw_ref[1:2, :] * _shift_up(dxb, nxt, 2)
               + crw_ref[0:1, :] * _shift_up(dxb, nxt, 3))
        c_dxb[...] = dxb[:SUBLANES, :]
        dz_ref[:, C_RX * GROUP_W:(C_RX + 1) * GROUP_W] = drx.astype(BF16)
        dlam = colsum(dlog_a * (-RG_C * r)) * (-jax.nn.sigmoid(-lam))
        gr_ref[...] += (_put_row(shp8, 3, colsum(dxb * rx)) + _put_row(shp8, 2, colsum(dxb * rx1))
                        + _put_row(shp8, 1, colsum(dxb * rx2)) + _put_row(shp8, 0, colsum(dxb * rx3))
                        + _put_row(shp8, 4, colsum(dxb)) + _put_row(shp8, 5, colsum(dga))
                        + _put_row(shp8, 6, colsum(dgx)) + _put_row(shp8, 7, dlam))

        cu, cvv, cg = cu_ref[...], cv_ref[...], cg_ref[...]
        dyc = dyc_ref[...]
        u_c, du_c = _gelu_and_grad(cu)
        gv, dgv_c = _gelu_and_grad(cvv)
        rs = lax.rsqrt(jnp.mean(gv * gv, axis=-1, keepdims=True) + NORM_EPS)
        vh = gv * rs
        gng = gng_ref[...]
        vvb = (vh * gng).astype(BF16)
        spat = _gmlp_spatial(ws_ref, vvb, head) + jnp.concatenate([bs_ref[...]] * (TILE_MIX // GMLP_CHUNK), axis=0)
        sgc = _silu(cg)
        dz_ref[:, C_CU * GROUP_W:(C_CU + 1) * GROUP_W] = (dyc * spat * sgc * du_c).astype(BF16)
        dz_ref[:, C_CG * GROUP_W:(C_CG + 1) * GROUP_W] = (dyc * u_c * spat * _dsilu(cg)).astype(BF16)
        dsp = dyc * u_c * sgc
        dspb = dsp.astype(BF16)
        tril = (lax.broadcasted_iota(jnp.int32, (GMLP_CHUNK, GMLP_CHUNK), 0)
                >= lax.broadcasted_iota(jnp.int32, (GMLP_CHUNK, GMLP_CHUNK), 1))
        head_c = head[:GMLP_CHUNK]
        dvv_parts = []
        gbs = jnp.zeros((GMLP_CHUNK, GROUP_W), F32)
        for j in range(TILE_MIX // GMLP_CHUNK):
            sl = slice(j * GMLP_CHUNK, (j + 1) * GMLP_CHUNK)
            dblk = dspb[sl, :]
            vblk = vvb[sl, :]
            gbs = gbs + dsp[sl, :]
            acc = jnp.zeros((GMLP_CHUNK, GROUP_W), F32)
            for h in range(N_HEADS):
                acc = jnp.where(head_c == h, _dot(wst_ref[h], dblk), acc)
                dm = jnp.where(head_c == h, dblk, jnp.zeros_like(dblk))
                gws_ref[h] += jnp.where(tril, _dot_nt(dm, vblk), 0.0)
            dvv_parts.append(acc)
        gbs_ref[...] += gbs
        dvv = jnp.concatenate(dvv_parts, axis=0)
        gn_ref[...] += _put_row(shp8, 0, colsum(dvv * vh))
        dvh = dvv * gng
        dgv = rs * (dvh - vh * jnp.mean(dvh * vh, axis=-1, keepdims=True))
        dz_ref[:, C_CV * GROUP_W:(C_CV + 1) * GROUP_W] = (dgv * dgv_c).astype(BF16)

        dsum = dqkv_ref[0]
        for p in range(1, n_pat):
            dsum = dsum + dqkv_ref[p]
        dz_ref[:, C_DQ * GROUP_W:(C_DV + 1) * GROUP_W] = dsum.astype(BF16)
        dz_ref[:, C_DG * GROUP_W:(C_DG + 1) * GROUP_W] = ddg_ref[...].astype(BF16)

    per = TILE_MIX // SUBLANES
    rev = lambda c: pl.BlockSpec((TILE_MIX, GROUP_W), lambda t, c=c: (n_tiles - 1 - t, c))
    revh = lambda c: pl.BlockSpec((SUBLANES, GROUP_W),
                                  lambda t, c=c: (jnp.maximum((n_tiles - 1 - t) * per - 1, 0), c))
    names = ("caw", "crw", "crb", "wa", "wx", "ba", "bx", "lam", "gng", "ws", "wst", "bs")
    in_specs = ([rev(0), rev(1), rev(2)]
                + [rev(c) for c in (C_AX, C_AB, C_AC, C_AG, C_RX, C_RG, C_CU, C_CV, C_CG)]
                + [revh(C_AX), revh(C_AC), revh(C_RX), rev(0), revh(0),
                   pl.BlockSpec((n_pat, TILE_MIX, 3 * GROUP_W), lambda t: (0, n_tiles - 1 - t, 0)), rev(0)]
                + [_full(wts[k].shape) for k in names])
    small = jax.ShapeDtypeStruct((SUBLANES, GROUP_W), F32)
    sq = jax.ShapeDtypeStruct((GROUP_W, GROUP_W), F32)
    out_shape = [jax.ShapeDtypeStruct((s, D_IN), BF16), small, small, small, sq, sq,
                 jax.ShapeDtypeStruct((N_HEADS, GMLP_CHUNK, GMLP_CHUNK), F32),
                 jax.ShapeDtypeStruct((GMLP_CHUNK, GROUP_W), F32)]
    out_specs = ([pl.BlockSpec((TILE_MIX, D_IN), lambda t: (n_tiles - 1 - t, 0))]
                 + [_full(o.shape) for o in out_shape[1:]])
    return pl.pallas_call(
        body, name="mix_bwd", grid=(n_tiles,),
        in_specs=in_specs, out_specs=out_specs, out_shape=out_shape,
        scratch_shapes=[pltpu.VMEM((SUBLANES, GROUP_W), F32)] * 4,
        compiler_params=_params(("arbitrary",)),
    )(dy, dy, dy, *([z] * 12), hl, hl, dqkv, ddg, *[wts[k] for k in names])


def _block_diag(w):
    eye = jnp.eye(N_HEADS, dtype=w.dtype)
    return (w[:, :, None, :] * eye[:, None, :, None]).reshape(GROUP_W, GROUP_W)


def _diag_blocks(g):
    g4 = g.reshape(N_HEADS, HEAD_DIM, N_HEADS, HEAD_DIM)
    return jnp.stack([g4[h, :, h, :] for h in range(N_HEADS)])


def _layer_weights(p, l):
    tril = jnp.tril(jnp.ones((GMLP_CHUNK, GMLP_CHUNK), dtype=bool))
    ws = jnp.where(tril[None], p["gmlp_ws"][l], 0.0).astype(BF16)
    row = lambda a: a[l][None, :]
    return dict(
        caw=p["conv_a_w"][l], crw=p["conv_r_w"][l], crb=row(p["conv_r_b"]),
        wa=_block_diag(p["lru_wa"][l]).astype(BF16), wx=_block_diag(p["lru_wx"][l]).astype(BF16),
        ba=row(p["lru_ba"]), bx=row(p["lru_bx"]), lam=row(p["lru_lambda"]), gng=row(p["gmlp_norm_g"]),
        ws=ws, wst=jnp.transpose(ws, (0, 2, 1)),
        bs=jnp.repeat(jnp.transpose(p["gmlp_bs"][l]), HEAD_DIM, axis=1))


def _per_pattern(a, fn):
    return jnp.stack([fn(a, d) for d in PATTERN_DILS])


def _per_pattern_each(a, fn):
    return jnp.stack([fn(a[i], d) for i, d in enumerate(PATTERN_DILS)])


def local_step(x, tgt, p, w_in_b, w_out_b):
    depth = w_in_b.shape[0]
    saved = []
    for l in range(depth):
        wts = _layer_weights(p, l)
        z, h = in_fwd(x, p["norm_g"][l][None, :], w_in_b[l])
        qkv = z[:, C_DQ * GROUP_W:(C_DV + 1) * GROUP_W].astype(BF16)
        qd, kd, vd = (_per_pattern(qkv[:, i * GROUP_W:(i + 1) * GROUP_W], _deinterleave) for i in range(3))
        op_d, lsep_d = attn_fwd(qd, kd, vd)
        op = _per_pattern_each(op_d, _interleave)
        lsep = _per_pattern_each(lsep_d, _interleave)
        y, hl, o, lse = mix_fwd(z, op, lsep, wts)
        saved.append(dict(x=x, z=z, h=h, y=y, hl=hl, o=o, lse=lse, qd=qd, kd=kd, vd=vd, wts=wts))
        x = out_fwd(y, w_out_b[l], x)

    loss, dx, dfg = loss_head(x, p["final_g"][None, :], tgt)
    grads = {k: [None] * depth for k in
             ("norm_g", "w_in", "conv_a_w", "conv_r_w", "conv_r_b", "lru_wa", "lru_ba", "lru_wx", "lru_bx",
              "lru_lambda", "gmlp_norm_g", "gmlp_ws", "gmlp_bs", "w_out")}
    for l in reversed(range(depth)):
        sv = saved[l]
        dy = out_bwd(dx, w_out_b[l])
        grads["w_out"][l] = matmul_tn(sv["y"], dx, 1)
        do, dlt, ddg = attn_pre_bwd(dy, sv["z"], sv["o"])
        dod = _per_pattern(do, _deinterleave)
        lsed = _per_pattern(sv["lse"], _deinterleave)
        dltd = _per_pattern(dlt, _deinterleave)
        dqd, dkd, dvd = attn_bwd(sv["qd"], sv["kd"], sv["vd"], dod, lsed, dltd)
        dqkv = jnp.stack([jnp.concatenate([_interleave(t[i], d) for t in (dqd, dkd, dvd)], axis=1)
                          for i, d in enumerate(PATTERN_DILS)])
        dz, ga, gr, gn, gwa, gwx, gws, gbs = mix_bwd(dy, sv["z"], sv["hl"], dqkv, ddg, sv["wts"])
        dx, dgn = in_bwd(dz, w_in_b[l], sv["x"], p["norm_g"][l][None, :], dx)
        grads["w_in"][l] = matmul_tn(sv["h"], dz, 2)
        grads["norm_g"][l] = dgn[0]
        grads["conv_a_w"][l] = ga[:3]
        grads["conv_r_w"][l] = gr[:4]
        grads["conv_r_b"][l] = gr[4]
        grads["lru_ba"][l] = gr[5]
        grads["lru_bx"][l] = gr[6]
        grads["lru_lambda"][l] = gr[7]
        grads["gmlp_norm_g"][l] = gn[0]
        grads["lru_wa"][l] = _diag_blocks(gwa)
        grads["lru_wx"][l] = _diag_blocks(gwx)
        grads["gmlp_ws"][l] = gws
        grads["gmlp_bs"][l] = jnp.transpose(gbs.reshape(GMLP_CHUNK, N_HEADS, HEAD_DIM).sum(-1))
    grads = {k: jnp.stack(v) for k, v in grads.items()}
    grads["final_g"] = dfg[0]
    return loss, dx, grads


MESH = pl.DeviceIdType.MESH
N_CHIPS = 4
N_DEV = 8
ANY = pl.BlockSpec(memory_space=pl.ANY)


def _place():
    x, y, c = lax.axis_index("x"), lax.axis_index("y"), lax.axis_index("c")
    chips = [(1 - x, y), (x, 1 - y), (1 - x, 1 - y)]
    return x, y, c, chips


def _remote(src, dst, ssem, rsem, to):
    return pltpu.make_async_remote_copy(src_ref=src, dst_ref=dst, send_sem=ssem, recv_sem=rsem,
                                        device_id=to, device_id_type=MESH)


def gather_sharded(w1, w2, w3):
    h1, h2 = w1.shape[0] // 2, w2.shape[0] // 2

    def body(w1_ref, w2_ref, w3_ref, g1_ref, g2_ref, g3_ref, ssem, rsem, fssem, frsem, lsem):
        x, y, c, chips = _place()
        me = 2 * x + y
        sibling = (x, y, 1 - c)
        big = ((w1_ref, g1_ref, h1), (w2_ref, g2_ref, h2))

        local = [pltpu.make_async_copy(w1_ref, g1_ref.at[me], lsem.at[0]),
                 pltpu.make_async_copy(w2_ref, g2_ref.at[me], lsem.at[1]),
                 pltpu.make_async_copy(w3_ref, g3_ref.at[me], lsem.at[2])]
        for cp in local:
            cp.start()

        def ici(a, k, origin, half):
            if a == 2:
                src = w3_ref if origin is None else g3_ref.at[origin]
                dst = g3_ref.at[me if origin is None else origin]
            else:
                w_ref, g_ref, h = big[a]
                rows = pl.ds(half * h, h)
                src = w_ref.at[rows, :] if origin is None else g_ref.at[origin, rows, :]
                dst = g_ref.at[me if origin is None else origin, rows, :]
            return _remote(src, dst, ssem.at[3 * a + k], rsem.at[3 * a + k], (*chips[k], c))

        def fwd(a, k, origin, half):
            w_ref, g_ref, h = big[a]
            rows = pl.ds(half * h, h)
            blk = g_ref.at[origin, rows, :]
            return _remote(blk, blk, fssem.at[2 * k + a], frsem.at[2 * k + a], sibling)

        sends = [ici(a, k, None, c) for k in range(3) for a in range(3)]
        for cp in sends:
            cp.start()
        passed = []
        for k, (cx, cy) in enumerate(chips):
            origin = 2 * cx + cy
            for a in range(3):
                ici(a, k, origin, c).wait_recv()
                if a < 2:
                    f = fwd(a, k, origin, c)
                    f.start()
                    passed.append(f)
        for k, (cx, cy) in enumerate(chips):
            for a in range(2):
                fwd(a, k, 2 * cx + cy, 1 - c).wait_recv()
        for cp in sends + passed:
            cp.wait_send()
        for cp in local:
            cp.wait()

    return pl.pallas_call(
        body, name="gather_sharded",
        in_specs=[ANY, ANY, ANY], out_specs=[ANY, ANY, ANY],
        out_shape=[jax.ShapeDtypeStruct((N_CHIPS,) + w.shape, w.dtype) for w in (w1, w2, w3)],
        scratch_shapes=[pltpu.SemaphoreType.DMA((9,)), pltpu.SemaphoreType.DMA((9,)),
                        pltpu.SemaphoreType.DMA((6,)), pltpu.SemaphoreType.DMA((6,)),
                        pltpu.SemaphoreType.DMA((3,))],
    )(w1, w2, w3)


def scatter_to_chips(s1, s2):
    def body(s1_ref, s2_ref, r1_ref, r2_ref, ssem, rsem):
        x, y, c, chips = _place()
        copies = []
        for k, (cx, cy) in enumerate(chips):
            dest = 2 * cx + cy
            for a, (s_ref, r_ref) in enumerate(((s1_ref, r1_ref), (s2_ref, r2_ref))):
                copies.append(_remote(s_ref.at[dest], r_ref.at[k], ssem.at[2 * k + a], rsem.at[2 * k + a], (cx, cy, c)))
        for cp in copies:
            cp.start()
        for cp in copies:
            cp.wait()

    return pl.pallas_call(
        body, name="scatter_to_chips",
        in_specs=[ANY, ANY], out_specs=[ANY, ANY],
        out_shape=[jax.ShapeDtypeStruct((3,) + s.shape[1:], s.dtype) for s in (s1, s2)],
        scratch_shapes=[pltpu.SemaphoreType.DMA((6,)), pltpu.SemaphoreType.DMA((6,))],
    )(s1, s2)


def sibling_exchange(p1, p2):
    def body(p1_ref, p2_ref, q1_ref, q2_ref, ssem, rsem):
        x, y, c, _ = _place()
        copies = [_remote(p_ref, q_ref, ssem.at[a], rsem.at[a], (x, y, 1 - c))
                  for a, (p_ref, q_ref) in enumerate(((p1_ref, q1_ref), (p2_ref, q2_ref)))]
        for cp in copies:
            cp.start()
        for cp in copies:
            cp.wait()

    return pl.pallas_call(
        body, name="sibling_exchange",
        in_specs=[ANY, ANY], out_specs=[ANY, ANY],
        out_shape=[jax.ShapeDtypeStruct(p.shape, p.dtype) for p in (p1, p2)],
        scratch_shapes=[pltpu.SemaphoreType.DMA((2,)), pltpu.SemaphoreType.DMA((2,))],
    )(p1, p2)


def gather_all(v):
    m_per, n = v.shape

    def body(x_ref, out_ref, send_sems, recv_sems, local_sem):
        x, y, c, chips = _place()
        me, sibling = (x, y, c), (x, y, 1 - c)

        def rows(px, py, pc):
            return out_ref.at[pl.ds((4 * px + 2 * py + pc) * m_per, m_per), :]

        def copy(k, block, to, src=None):
            return _remote(rows(*block) if src is None else src, rows(*block), send_sems.at[k], recv_sems.at[k], to)

        mine = pltpu.make_async_copy(x_ref, rows(*me), local_sem)
        mine.start()
        first = [copy(0, me, sibling, src=x_ref)]
        first += [copy(1 + j, me, (*chip, c), src=x_ref) for j, chip in enumerate(chips)]
        for cp in first:
            cp.start()
        passed = [copy(4 + j, (*chip, c), sibling) for j, chip in enumerate(chips)]
        for j, chip in enumerate(chips):
            copy(1 + j, (*chip, c), me).wait_recv()
            passed[j].start()
        copy(0, sibling, me).wait_recv()
        for j, chip in enumerate(chips):
            copy(4 + j, (*chip, 1 - c), me).wait_recv()
        for cp in first + passed:
            cp.wait_send()
        mine.wait()

    return pl.pallas_call(
        body, name="gather_all",
        out_shape=jax.ShapeDtypeStruct((N_DEV * m_per, n), v.dtype),
        in_specs=[pl.BlockSpec(memory_space=pltpu.VMEM)],
        out_specs=pl.BlockSpec(memory_space=pltpu.VMEM),
        scratch_shapes=[pltpu.SemaphoreType.DMA((7,)), pltpu.SemaphoreType.DMA((7,)), pltpu.SemaphoreType.DMA],
        compiler_params=pltpu.CompilerParams(vmem_limit_bytes=VMEM_LIMIT),
    )(v)


TILE_ROWS = 256


def sum_partials(own, parts):
    r, c = own.shape
    k = parts.shape[0]

    def body(o_ref, p_ref, out_ref):
        acc = o_ref[...]
        for i in range(k):
            acc = acc + p_ref[i].astype(F32)
        out_ref[...] = acc

    return pl.pallas_call(
        body, name="sum_partials", grid=(r // TILE_ROWS,),
        in_specs=[pl.BlockSpec((TILE_ROWS, c), lambda i: (i, 0)), pl.BlockSpec((k, TILE_ROWS, c), lambda i: (0, i, 0))],
        out_specs=pl.BlockSpec((TILE_ROWS, c), lambda i: (i, 0)),
        out_shape=jax.ShapeDtypeStruct((r, c), F32),
        compiler_params=_params(("parallel",)),
    )(own, parts)


def adamw(w, ga, gb, m, v):
    r, c = w.shape
    tile = min(TILE_ROWS, r)

    def body(w_ref, ga_ref, gb_ref, m_ref, v_ref, g_ref, d_ref, m2_ref, v2_ref):
        g = ga_ref[...] + gb_ref[...]
        g_ref[...] = g
        m2 = ADAM_B1 * m_ref[...] + (1.0 - ADAM_B1) * g
        v2 = ADAM_B2 * v_ref[...] + (1.0 - ADAM_B2) * (g * g)
        m2_ref[...] = m2
        v2_ref[...] = v2
        m_hat = m2 / (1.0 - ADAM_B1 ** ADAM_STEP)
        v_hat = v2 / (1.0 - ADAM_B2 ** ADAM_STEP)
        d_ref[...] = -ADAM_LR * (m_hat / (jnp.sqrt(v_hat) + ADAM_EPS) + ADAM_WD * w_ref[...])

    spec = pl.BlockSpec((tile, c), lambda i: (i, 0))
    return pl.pallas_call(
        body, name="adamw", grid=(r // tile,),
        in_specs=[spec] * 5, out_specs=[spec] * 4,
        out_shape=[jax.ShapeDtypeStruct((r, c), F32)] * 4,
        compiler_params=_params(("parallel",)),
    )(w, ga, gb, m, v)


REPLICATED = ("norm_g", "conv_r_b", "lru_wa", "lru_ba", "lru_wx", "lru_bx", "lru_lambda", "gmlp_norm_g",
              "gmlp_ws", "gmlp_bs", "final_g")
CHIP_SHARDED_SMALL = ("conv_a_w", "conv_r_w")
PACK_LANES = 128


def _pack(arrays):
    flat = jnp.concatenate([a.reshape(-1) for a in arrays])
    pad = (-flat.shape[0]) % (TILE_ROWS * PACK_LANES)
    return jnp.pad(flat, (0, pad)).reshape(-1, PACK_LANES)


def _unpack(packed, shapes):
    flat = packed.reshape(-1)
    out, off = [], 0
    for shp in shapes:
        n = math.prod(shp)
        out.append(flat[off:off + n].reshape(shp))
        off += n
    return out


def kernel(x, norm_g, w_in, conv_a_w, conv_r_w, conv_r_b, lru_wa, lru_ba, lru_wx, lru_bx, lru_lambda, gmlp_norm_g, gmlp_ws, gmlp_bs, w_out, final_g, loss_target, m_norm_g, m_w_in, m_conv_a_w, m_conv_r_w, m_conv_r_b, m_lru_wa, m_lru_ba, m_lru_wx, m_lru_bx, m_lru_lambda, m_gmlp_norm_g, m_gmlp_ws, m_gmlp_bs, m_w_out, m_final_g, v_norm_g, v_w_in, v_conv_a_w, v_conv_r_w, v_conv_r_b, v_lru_wa, v_lru_ba, v_lru_wx, v_lru_bx, v_lru_lambda, v_gmlp_norm_g, v_gmlp_ws, v_gmlp_bs, v_w_out, v_final_g):
    names = ("norm_g", "w_in", "conv_a_w", "conv_r_w", "conv_r_b", "lru_wa", "lru_ba", "lru_wx", "lru_bx",
             "lru_lambda", "gmlp_norm_g", "gmlp_ws", "gmlp_bs", "w_out", "final_g")
    w = dict(zip(names, (norm_g, w_in, conv_a_w, conv_r_w, conv_r_b, lru_wa, lru_ba, lru_wx, lru_bx, lru_lambda,
                         gmlp_norm_g, gmlp_ws, gmlp_bs, w_out, final_g)))
    m = dict(zip(names, (m_norm_g, m_w_in, m_conv_a_w, m_conv_r_w, m_conv_r_b, m_lru_wa, m_lru_ba, m_lru_wx, m_lru_bx,
                         m_lru_lambda, m_gmlp_norm_g, m_gmlp_ws, m_gmlp_bs, m_w_out, m_final_g)))
    v = dict(zip(names, (v_norm_g, v_w_in, v_conv_a_w, v_conv_r_w, v_conv_r_b, v_lru_wa, v_lru_ba, v_lru_wx, v_lru_bx,
                         v_lru_lambda, v_gmlp_norm_g, v_gmlp_ws, v_gmlp_bs, v_w_out, v_final_g)))
    depth, _, in_cols = w_in.shape
    out_rows = w_out.shape[1]
    conv_ch = conv_a_w.shape[2]
    chip = 2 * lax.axis_index("x") + lax.axis_index("y")

    taps = conv_a_w.shape[1] + conv_r_w.shape[1]
    g1, g2, g3 = gather_sharded(
        w_in.astype(BF16).reshape(depth * D_MODEL, in_cols),
        w_out.astype(BF16).reshape(depth * out_rows, D_MODEL),
        jnp.concatenate([conv_a_w, conv_r_w], axis=1).reshape(depth * taps, conv_ch))
    w_in_b = g1.reshape(N_CHIPS, depth, D_MODEL, in_cols).transpose(1, 2, 0, 3).reshape(depth, D_MODEL, D_IN)
    w_out_b = g2.reshape(N_CHIPS, depth, out_rows, D_MODEL).transpose(1, 0, 2, 3).reshape(depth, D_MIX, D_MODEL)
    conv = g3.reshape(N_CHIPS, depth, taps, conv_ch).transpose(1, 2, 0, 3).reshape(depth, taps, GROUP_W)
    p = dict(w)
    p["conv_a_w"] = conv[:, :conv_a_w.shape[1]]
    p["conv_r_w"] = conv[:, conv_a_w.shape[1]:]

    loss8, dx, grads = local_step(x[0], loss_target[0], p, w_in_b, w_out_b)
    loss = lax.psum(loss8[0, 0], ("x", "y", "c"))

    gi = grads["w_in"].reshape(depth, D_MODEL, N_CHIPS, in_cols)
    go = grads["w_out"].reshape(depth, N_CHIPS, out_rows, D_MODEL)
    s1 = gi.transpose(2, 0, 1, 3).reshape(N_CHIPS, depth * D_MODEL, in_cols).astype(BF16)
    s2 = go.transpose(1, 0, 2, 3).reshape(N_CHIPS, depth * out_rows, D_MODEL).astype(BF16)
    own1 = lax.dynamic_index_in_dim(gi, chip, axis=2, keepdims=False).reshape(depth * D_MODEL, in_cols)
    own2 = lax.dynamic_index_in_dim(go, chip, axis=1, keepdims=False).reshape(depth * out_rows, D_MODEL)
    r1, r2 = scatter_to_chips(s1, s2)
    p1 = sum_partials(own1, r1)
    p2 = sum_partials(own2, r2)
    q1, q2 = sibling_exchange(p1, p2)
    res = {}
    res["w_in"] = [t.reshape(w_in.shape) for t in
                   adamw(w_in.reshape(p1.shape), p1, q1, m_w_in.reshape(p1.shape), v_w_in.reshape(p1.shape))]
    res["w_out"] = [t.reshape(w_out.shape) for t in
                    adamw(w_out.reshape(p2.shape), p2, q2, m_w_out.reshape(p2.shape), v_w_out.reshape(p2.shape))]

    small = REPLICATED + CHIP_SHARDED_SMALL
    packed = _pack([grads[k] for k in small])
    rows = packed.shape[0]
    allp = gather_all(packed).reshape(N_DEV, rows, PACK_LANES)
    total = sum_partials(allp[0], allp[1:])
    gs = dict(zip(small, _unpack(total, [grads[k].shape for k in small])))
    for k in CHIP_SHARDED_SMALL:
        gs[k] = lax.dynamic_slice_in_dim(gs[k], chip * conv_ch, conv_ch, axis=2)
    wp, gp, mp, vp = (_pack([d[k] for k in small]) for d in (w, gs, m, v))
    outs = adamw(wp, gp, jnp.zeros_like(gp), mp, vp)
    shapes = [w[k].shape for k in small]
    unpacked = [_unpack(o, shapes) for o in outs]
    for i, k in enumerate(small):
        res[k] = [u[i] for u in unpacked]

    return (loss, dx[None], *[res[k][0] for k in names], *[res[k][1] for k in names],
            *[res[k][2] for k in names], *[res[k][3] for k in names])
```

```python
import functools
import math

import jax
import jax.numpy as jnp
import numpy as np
from jax import lax
from jax.experimental import pallas as pl
from jax.experimental.pallas import tpu as pltpu

F32 = jnp.float32
BF16 = jnp.bfloat16

D_MODEL = 1024
GROUP_W = 256
N_HEADS = 4
HEAD_DIM = 64
N_CHUNKS = 13
D_IN = N_CHUNKS * GROUP_W
D_MIX = 4 * GROUP_W
NORM_EPS = 1e-6
RG_C = 8.0
GMLP_CHUNK = 128
ATTN_BLOCK = 128
PATTERN_DILS = (1, 4, 16)
N_PATTERNS = len(PATTERN_DILS)
ALIBI_SLOPES = tuple(2.0 ** (-8.0 * (h + 1) / N_HEADS) for h in range(N_HEADS))
ATTN_SCALE = 1.0 / math.sqrt(HEAD_DIM)
NEG_BIG = -1e30

ADAM_LR = 0.001
ADAM_B1 = 0.9
ADAM_B2 = 0.999
ADAM_EPS = 1e-08
ADAM_WD = 0.01
ADAM_STEP = 10

C_AX, C_AB, C_AC, C_AG, C_RX, C_RG, C_CU, C_CV, C_CG, C_DQ, C_DK, C_DV, C_DG = range(13)

SUBLANES = 8
LANES = 128
VMEM_LIMIT = 56 * 1024 * 1024
TILE_IN = 256
TILE_MIX = 256
TILE_DW = 512
ATTN_QB = 4
GELU_K0 = math.sqrt(2.0 / math.pi)
GELU_K1 = 0.044715


def _params(sem):
    return pltpu.CompilerParams(dimension_semantics=sem, vmem_limit_bytes=VMEM_LIMIT)


def _silu(x):
    return x * jax.nn.sigmoid(x)


def _dsilu(x):
    s = jax.nn.sigmoid(x)
    return s * (1.0 + x * (1.0 - s))


def _gelu(x):
    return 0.5 * x * (1.0 + jnp.tanh(GELU_K0 * (x + GELU_K1 * x * x * x)))


def _gelu_and_grad(x):
    t = jnp.tanh(GELU_K0 * (x + GELU_K1 * x * x * x))
    g = 0.5 * x * (1.0 + t)
    dg = 0.5 * (1.0 + t) + 0.5 * x * (1.0 - t * t) * GELU_K0 * (1.0 + 3.0 * GELU_K1 * x * x)
    return g, dg


def _neg_expm1(x):
    series = x * (1.0 + x * (0.5 + x * (1.0 / 6.0 + x * (1.0 / 24.0 + x * (1.0 / 120.0)))))
    return -jnp.where(x > -0.05, series, jnp.exp(x) - 1.0)


def _shift_down(v, halo, k):
    r = pltpu.roll(v, k, 0)
    rh = pltpu.roll(halo, k, 0)
    row = lax.broadcasted_iota(jnp.int32, halo.shape, 0)
    top = jnp.where(row < k, rh, r[:SUBLANES])
    return jnp.concatenate([top, r[SUBLANES:]], axis=0)


def _shift_up(v, halo, k):
    t = v.shape[0]
    r = pltpu.roll(v, t - k, 0)
    rh = pltpu.roll(halo, SUBLANES - k, 0)
    row = lax.broadcasted_iota(jnp.int32, halo.shape, 0)
    bot = jnp.where(row >= SUBLANES - k, rh, r[t - SUBLANES:])
    return jnp.concatenate([r[:t - SUBLANES], bot], axis=0)


def _scan_causal(a, b):
    t = a.shape[0]
    row = lax.broadcasted_iota(jnp.int32, a.shape, 0)
    d = 1
    while d < t:
        m = row >= d
        a_s = jnp.where(m, pltpu.roll(a, d, 0), 1.0)
        b_s = jnp.where(m, pltpu.roll(b, d, 0), 0.0)
        b = a * b_s + b
        a = a * a_s
        d *= 2
    return a, b


def _scan_anticausal(a, b):
    t = a.shape[0]
    row = lax.broadcasted_iota(jnp.int32, a.shape, 0)
    d = 1
    while d < t:
        m = row < t - d
        a_s = jnp.where(m, pltpu.roll(a, t - d, 0), 1.0)
        b_s = jnp.where(m, pltpu.roll(b, t - d, 0), 0.0)
        b = a * b_s + b
        a = a * a_s
        d *= 2
    return a, b


def _head_of_lane(shape):
    return lax.broadcasted_iota(jnp.int32, shape, len(shape) - 1) // HEAD_DIM


def _put_row(acc_shape, k, row_vec):
    row = lax.broadcasted_iota(jnp.int32, acc_shape, 0)
    return jnp.where(row == k, jnp.broadcast_to(row_vec, acc_shape), 0.0)


def _dot(a, b):
    return jnp.dot(a, b, preferred_element_type=F32)


def _dot_nt(a, b):
    return lax.dot_general(a, b, (((1,), (1,)), ((), ())), preferred_element_type=F32)


def _dot_tn(a, b):
    return lax.dot_general(a, b, (((0,), (0,)), ((), ())), preferred_element_type=F32)


def _deinterleave_store(val, stage, outs):
    t, c = val.shape
    for hh in range(c // LANES):
        stage[hh][...] = val[:, hh * LANES:(hh + 1) * LANES].astype(F32)
    for dil, ref in outs:
        for r in range(dil):
            for hh in range(c // LANES):
                ref[r, :, hh * LANES:(hh + 1) * LANES] = stage[hh][pl.ds(r, t // dil, stride=dil), :].astype(ref.dtype)


def _interleave_load(ref, dil, stage):
    _, n, c = ref.shape
    for r in range(dil):
        for hh in range(c // LANES):
            stage[hh][pl.ds(r, n, stride=dil), :] = ref[r, :, hh * LANES:(hh + 1) * LANES].astype(F32)
    return jnp.concatenate([stage[hh][...] for hh in range(c // LANES)], axis=1)


def _stage_scratch(tile, cols, copies):
    return [pltpu.VMEM((tile, LANES), F32)] * (copies * (cols // LANES))


def _by_residue(s, dil, cols, dtype):
    return jax.ShapeDtypeStruct((dil, s // dil, cols), dtype)


def _residue_block(dil, tile, cols):
    return pl.BlockSpec((dil, tile // dil, cols), lambda i: (0, i, 0))


def in_fwd(x, g, w):
    s = x.shape[0]
    qkv_w = 3 * GROUP_W

    def body(x_ref, g_ref, w_ref, z_ref, h_ref, qkv1_ref, qkv4_ref, qkv16_ref, *stage):
        xv = x_ref[...]
        rs = lax.rsqrt(jnp.mean(xv * xv, axis=-1, keepdims=True) + NORM_EPS)
        h = (xv * rs * g_ref[...]).astype(BF16)
        h_ref[...] = h
        z = _dot(h, w_ref[...])
        z_ref[...] = z
        qkv = z[:, C_DQ * GROUP_W:(C_DV + 1) * GROUP_W]
        qkv1_ref[...] = qkv.astype(BF16)
        _deinterleave_store(qkv, stage, ((PATTERN_DILS[1], qkv4_ref), (PATTERN_DILS[2], qkv16_ref)))

    return pl.pallas_call(
        body, name="in_fwd", grid=(s // TILE_IN,),
        in_specs=[pl.BlockSpec((TILE_IN, D_MODEL), lambda i: (i, 0)),
                  pl.BlockSpec((1, D_MODEL), lambda i: (0, 0)),
                  pl.BlockSpec((D_MODEL, D_IN), lambda i: (0, 0))],
        out_specs=[pl.BlockSpec((TILE_IN, D_IN), lambda i: (i, 0)),
                   pl.BlockSpec((TILE_IN, D_MODEL), lambda i: (i, 0)),
                   pl.BlockSpec((TILE_IN, qkv_w), lambda i: (i, 0)),
                   _residue_block(PATTERN_DILS[1], TILE_IN, qkv_w),
                   _residue_block(PATTERN_DILS[2], TILE_IN, qkv_w)],
        out_shape=[jax.ShapeDtypeStruct((s, D_IN), F32), jax.ShapeDtypeStruct((s, D_MODEL), BF16),
                   jax.ShapeDtypeStruct((s, qkv_w), BF16),
                   _by_residue(s, PATTERN_DILS[1], qkv_w, BF16), _by_residue(s, PATTERN_DILS[2], qkv_w, BF16)],
        scratch_shapes=_stage_scratch(TILE_IN, qkv_w, 1),
        compiler_params=_params(("parallel",)),
    )(x, g, w)


def out_fwd(y, w, x):
    s = x.shape[0]

    def body(y_ref, w_ref, x_ref, o_ref):
        o_ref[...] = x_ref[...] + _dot(y_ref[...], w_ref[...])

    return pl.pallas_call(
        body, name="out_fwd", grid=(s // TILE_IN,),
        in_specs=[pl.BlockSpec((TILE_IN, D_MIX), lambda i: (i, 0)),
                  pl.BlockSpec((D_MIX, D_MODEL), lambda i: (0, 0)),
                  pl.BlockSpec((TILE_IN, D_MODEL), lambda i: (i, 0))],
        out_specs=pl.BlockSpec((TILE_IN, D_MODEL), lambda i: (i, 0)),
        out_shape=jax.ShapeDtypeStruct((s, D_MODEL), F32),
        compiler_params=_params(("parallel",)),
    )(y, w, x)


def out_bwd(dx, w, z, o):
    s = dx.shape[0]
    abc = 3 * GROUP_W

    def body(dx_ref, w_ref, dg_ref, o_ref, dy_ref, ddg_ref, do1_ref, do4_ref, do16_ref, dl1_ref, dl4_ref, dl16_ref,
             *stage):
        stage_a, stage_b = stage[:2], stage[2:]
        dy = _dot_nt(dx_ref[...].astype(BF16), w_ref[...])
        dy_ref[...] = dy[:, :abc]
        dyd = dy[:, abc:]
        head = _head_of_lane((TILE_IN, GROUP_W))
        dg = dg_ref[...]
        o = o_ref[...]
        do = dyd * _silu(dg)
        ddg_ref[...] = dyd * o * _dsilu(dg)
        prod = do * o
        dl = jnp.zeros_like(prod)
        for h in range(N_HEADS):
            sm = jnp.sum(jnp.where(head == h, prod, 0.0), axis=-1, keepdims=True)
            dl = jnp.where(head == h, sm, dl)
        do1_ref[...] = do.astype(BF16)
        dl1_ref[...] = dl
        _deinterleave_store(do, stage_a, ((PATTERN_DILS[1], do4_ref), (PATTERN_DILS[2], do16_ref)))
        _deinterleave_store(dl, stage_b, ((PATTERN_DILS[1], dl4_ref), (PATTERN_DILS[2], dl16_ref)))

    row = pl.BlockSpec((TILE_IN, GROUP_W), lambda i: (i, 0))
    r4 = _residue_block(PATTERN_DILS[1], TILE_IN, GROUP_W)
    r16 = _residue_block(PATTERN_DILS[2], TILE_IN, GROUP_W)
    return pl.pallas_call(
        body, name="out_bwd", grid=(s // TILE_IN,),
        in_specs=[pl.BlockSpec((TILE_IN, D_MODEL), lambda i: (i, 0)),
                  pl.BlockSpec((D_MIX, D_MODEL), lambda i: (0, 0)),
                  pl.BlockSpec((TILE_IN, GROUP_W), lambda i: (i, C_DG)), row],
        out_specs=[pl.BlockSpec((TILE_IN, abc), lambda i: (i, 0)), row, row, r4, r16, row, r4, r16],
        out_shape=[jax.ShapeDtypeStruct((s, abc), F32), jax.ShapeDtypeStruct((s, GROUP_W), F32),
                   jax.ShapeDtypeStruct((s, GROUP_W), BF16),
                   _by_residue(s, PATTERN_DILS[1], GROUP_W, BF16), _by_residue(s, PATTERN_DILS[2], GROUP_W, BF16),
                   jax.ShapeDtypeStruct((s, GROUP_W), F32),
                   _by_residue(s, PATTERN_DILS[1], GROUP_W, F32), _by_residue(s, PATTERN_DILS[2], GROUP_W, F32)],
        scratch_shapes=_stage_scratch(TILE_IN, GROUP_W, 2),
        compiler_params=_params(("parallel",)),
    )(dx, w, z, o)


def in_bwd(dz, w, x, g, dx_next):
    s = x.shape[0]

    def body(dz_ref, w_ref, x_ref, g_ref, dxn_ref, dx_ref, dg_ref):
        @pl.when(pl.program_id(0) == 0)
        def _():
            dg_ref[...] = jnp.zeros_like(dg_ref)

        dh = _dot_nt(dz_ref[...], w_ref[...])
        xv = x_ref[...]
        rs = lax.rsqrt(jnp.mean(xv * xv, axis=-1, keepdims=True) + NORM_EPS)
        xh = xv * rs
        dg_ref[...] += _put_row(dg_ref.shape, 0, jnp.sum(dh * xh, axis=0, keepdims=True))
        dn = dh * g_ref[...]
        dx_ref[...] = dxn_ref[...] + rs * (dn - xh * jnp.mean(dn * xh, axis=-1, keepdims=True))

    return pl.pallas_call(
        body, name="in_bwd", grid=(s // TILE_IN,),
        in_specs=[pl.BlockSpec((TILE_IN, D_IN), lambda i: (i, 0)),
                  pl.BlockSpec((D_MODEL, D_IN), lambda i: (0, 0)),
                  pl.BlockSpec((TILE_IN, D_MODEL), lambda i: (i, 0)),
                  pl.BlockSpec((1, D_MODEL), lambda i: (0, 0)),
                  pl.BlockSpec((TILE_IN, D_MODEL), lambda i: (i, 0))],
        out_specs=[pl.BlockSpec((TILE_IN, D_MODEL), lambda i: (i, 0)),
                   pl.BlockSpec((SUBLANES, D_MODEL), lambda i: (0, 0))],
        out_shape=[jax.ShapeDtypeStruct((s, D_MODEL), F32), jax.ShapeDtypeStruct((SUBLANES, D_MODEL), F32)],
        compiler_params=_params(("arbitrary",)),
    )(dz, w, x, g, dx_next)


def matmul_tn(a, b, n_split):
    s, m = a.shape
    n = b.shape[1]
    tn = n // n_split

    def body(a_ref, b_ref, o_ref):
        @pl.when(pl.program_id(1) == 0)
        def _():
            o_ref[...] = jnp.zeros_like(o_ref)

        o_ref[...] += _dot_tn(a_ref[...], b_ref[...].astype(BF16))

    return pl.pallas_call(
        body, name="matmul_tn", grid=(n_split, s // TILE_DW),
        in_specs=[pl.BlockSpec((TILE_DW, m), lambda j, k: (k, 0)),
                  pl.BlockSpec((TILE_DW, tn), lambda j, k: (k, j))],
        out_specs=pl.BlockSpec((m, tn), lambda j, k: (0, j)),
        out_shape=jax.ShapeDtypeStruct((m, n), F32),
        compiler_params=_params(("parallel", "arbitrary")),
    )(a, b)


def loss_head(x, g, tgt):
    s = x.shape[0]

    def body(x_ref, g_ref, t_ref, l_ref, dx_ref, dg_ref):
        @pl.when(pl.program_id(0) == 0)
        def _():
            l_ref[...] = jnp.zeros_like(l_ref)
            dg_ref[...] = jnp.zeros_like(dg_ref)

        xv = x_ref[...]
        gv = g_ref[...]
        rs = lax.rsqrt(jnp.mean(xv * xv, axis=-1, keepdims=True) + NORM_EPS)
        xh = xv * rs
        e = xh * gv - t_ref[...]
        part = 0.5 * jnp.sum(jnp.mean(e * e, axis=-1, keepdims=True), axis=0, keepdims=True)
        l_ref[...] += jnp.broadcast_to(part, l_ref.shape)
        dy = e * (1.0 / D_MODEL)
        dg_ref[...] += _put_row(dg_ref.shape, 0, jnp.sum(dy * xh, axis=0, keepdims=True))
        dn = dy * gv
        dx_ref[...] = rs * (dn - xh * jnp.mean(dn * xh, axis=-1, keepdims=True))

    return pl.pallas_call(
        body, name="loss_head", grid=(s // TILE_IN,),
        in_specs=[pl.BlockSpec((TILE_IN, D_MODEL), lambda i: (i, 0)),
                  pl.BlockSpec((1, D_MODEL), lambda i: (0, 0)),
                  pl.BlockSpec((TILE_IN, D_MODEL), lambda i: (i, 0))],
        out_specs=[pl.BlockSpec((SUBLANES, 128), lambda i: (0, 0)),
                   pl.BlockSpec((TILE_IN, D_MODEL), lambda i: (i, 0)),
                   pl.BlockSpec((SUBLANES, D_MODEL), lambda i: (0, 0))],
        out_shape=[jax.ShapeDtypeStruct((SUBLANES, 128), F32), jax.ShapeDtypeStruct((s, D_MODEL), F32),
                   jax.ShapeDtypeStruct((SUBLANES, D_MODEL), F32)],
        compiler_params=_params(("arbitrary",)),
    )(x, g, tgt)


def _attn_bias(dil):
    qi = np.arange(ATTN_BLOCK)[:, None]
    ki = np.arange(2 * ATTN_BLOCK)[None, :]
    delta = qi + ATTN_BLOCK - ki
    band = (delta >= 0) & (delta <= ATTN_BLOCK)
    out = np.empty((2, N_HEADS, ATTN_BLOCK, 2 * ATTN_BLOCK), np.float32)
    for f in range(2):
        ok = band & ((ki >= ATTN_BLOCK) | (f == 0))
        for h in range(N_HEADS):
            out[f, h] = np.where(ok, -ALIBI_SLOPES[h] * dil * delta, NEG_BIG)
    return jnp.asarray(out.reshape(2, N_HEADS * ATTN_BLOCK, 2 * ATTN_BLOCK))


def _stack_heads(a, head):
    return jnp.concatenate([jnp.where(head == h, a, jnp.zeros_like(a)) for h in range(N_HEADS)], axis=0)


def _unstack_heads(a, head):
    out = a[:ATTN_BLOCK]
    for h in range(1, N_HEADS):
        out = jnp.where(head == h, a[h * ATTN_BLOCK:(h + 1) * ATTN_BLOCK], out)
    return out


def _head_column(a):
    return jnp.concatenate([a[:, h * HEAD_DIM:h * HEAD_DIM + 1] for h in range(N_HEADS)], axis=0)


def _attn_specs(n_blocks):
    rows = ATTN_QB * ATTN_BLOCK
    cur = lambda c: pl.BlockSpec((rows, GROUP_W), lambda n, c=c: (n, c))
    prev = lambda c: pl.BlockSpec((ATTN_BLOCK, GROUP_W), lambda n, c=c: (jnp.maximum(n * ATTN_QB - 1, 0), c))
    nxt = lambda c: pl.BlockSpec((ATTN_BLOCK, GROUP_W),
                                 lambda n, c=c: (jnp.minimum(n * ATTN_QB + ATTN_QB, n_blocks - 1), c))
    return cur, prev, nxt


def _keys(kp_ref, k_ref, j):
    prev = kp_ref[...] if j == 0 else k_ref[(j - 1) * ATTN_BLOCK:j * ATTN_BLOCK, :]
    return jnp.concatenate([prev, k_ref[j * ATTN_BLOCK:(j + 1) * ATTN_BLOCK, :]], axis=0)


def attn_fwd(qkv, dil):
    s = qkv.shape[0]
    n_blocks = s // ATTN_BLOCK
    bps = n_blocks // dil
    rows = ATTN_QB * ATTN_BLOCK

    def body(q_ref, kp_ref, k_ref, vp_ref, v_ref, bias_ref, o_ref, lse_ref):
        n = pl.program_id(0)
        head = _head_of_lane((ATTN_BLOCK, GROUP_W))
        for j in range(ATTN_QB):
            sl = slice(j * ATTN_BLOCK, (j + 1) * ATTN_BLOCK)
            first = (((n * ATTN_QB + j) % bps) == 0).astype(jnp.int32)
            qs = _stack_heads(q_ref[sl, :], head)
            sc = _dot_nt(qs, _keys(kp_ref, k_ref, j)) * ATTN_SCALE + bias_ref[first]
            m = jnp.max(sc, axis=-1, keepdims=True)
            pr = jnp.exp(sc - m)
            l = jnp.sum(pr, axis=-1, keepdims=True)
            oh = _dot(pr.astype(BF16), _keys(vp_ref, v_ref, j)) / l
            o_ref[sl, :] = _unstack_heads(oh, head)
            lse_ref[sl, :] = _unstack_heads(jnp.broadcast_to(m + jnp.log(l), oh.shape), head)

    cur, prev, _ = _attn_specs(n_blocks)
    bias = _attn_bias(dil)
    out = jax.ShapeDtypeStruct((s, GROUP_W), F32)
    return pl.pallas_call(
        body, name=f"attn_fwd_d{dil}", grid=(n_blocks // ATTN_QB,),
        in_specs=[cur(0), prev(1), cur(1), prev(2), cur(2), pl.BlockSpec(bias.shape, lambda n: (0, 0, 0))],
        out_specs=[cur(0), cur(0)],
        out_shape=[out, out],
        compiler_params=_params(("parallel",)),
    )(qkv, qkv, qkv, qkv, qkv, bias)


def attn_bwd(qkv, do, lse, dlt, dil):
    s = qkv.shape[0]
    n_blocks = s // ATTN_BLOCK
    bps = n_blocks // dil
    rows = ATTN_QB * ATTN_BLOCK

    def body(q_ref, qn_ref, kp_ref, k_ref, vp_ref, v_ref, do_ref, don_ref, lse_ref, lsen_ref, dl_ref, dln_ref,
             bias_ref, out_ref, dk_acc, dv_acc):
        n = pl.program_id(0)
        head = _head_of_lane((ATTN_BLOCK, GROUP_W))
        dk_acc[...] = jnp.zeros_like(dk_acc)
        dv_acc[...] = jnp.zeros_like(dv_acc)

        def pair(qj, doj, lsej, dlj, kk, vv, bias, keep):
            qs = _stack_heads(qj, head)
            dos = _stack_heads(doj, head)
            sc = _dot_nt(qs, kk) * ATTN_SCALE + bias
            if keep is None:
                pr = jnp.exp(sc - _head_column(lsej))
            else:
                pr = jnp.exp(jnp.minimum(sc - _head_column(lsej), 0.0)) * keep
            dp = _dot_nt(dos, vv)
            ds = (pr * (dp - _head_column(dlj)) * ATTN_SCALE).astype(BF16)
            return ds, _dot_tn(ds, qs), _dot_tn(pr.astype(BF16), dos)

        for j in range(ATTN_QB):
            sl = slice(j * ATTN_BLOCK, (j + 1) * ATTN_BLOCK)
            first = (((n * ATTN_QB + j) % bps) == 0).astype(jnp.int32)
            kk = _keys(kp_ref, k_ref, j)
            ds, dks, dvs = pair(q_ref[sl, :], do_ref[sl, :], lse_ref[sl, :], dl_ref[sl, :],
                                kk, _keys(vp_ref, v_ref, j), bias_ref[first], None)
            out_ref[sl, 0:GROUP_W] = _unstack_heads(_dot(ds, kk), head)
            acc = slice(j * ATTN_BLOCK, (j + 2) * ATTN_BLOCK)
            dk_acc[acc, :] += dks
            dv_acc[acc, :] += dvs

        nxt = n * ATTN_QB + ATTN_QB
        valid = ((nxt < n_blocks) & ((nxt % bps) != 0)).astype(F32)
        last = slice((ATTN_QB - 1) * ATTN_BLOCK, ATTN_QB * ATTN_BLOCK)
        _, dks, dvs = pair(qn_ref[...], don_ref[...], lsen_ref[...], dln_ref[...], k_ref[last, :], v_ref[last, :],
                           bias_ref[0][:, :ATTN_BLOCK], valid)
        acc = slice(ATTN_QB * ATTN_BLOCK, (ATTN_QB + 1) * ATTN_BLOCK)
        dk_acc[acc, :] += dks
        dv_acc[acc, :] += dvs
        out_ref[:, GROUP_W:2 * GROUP_W] = dk_acc[ATTN_BLOCK:, :]
        out_ref[:, 2 * GROUP_W:3 * GROUP_W] = dv_acc[ATTN_BLOCK:, :]

    cur, prev, nxt = _attn_specs(n_blocks)
    bias = _attn_bias(dil)
    return pl.pallas_call(
        body, name=f"attn_bwd_d{dil}", grid=(n_blocks // ATTN_QB,),
        in_specs=[cur(0), nxt(0), prev(1), cur(1), prev(2), cur(2), cur(0), nxt(0), cur(0), nxt(0), cur(0), nxt(0),
                  pl.BlockSpec(bias.shape, lambda n: (0, 0, 0))],
        out_specs=pl.BlockSpec((rows, 3 * GROUP_W), lambda n: (n, 0)),
        out_shape=jax.ShapeDtypeStruct((s, 3 * GROUP_W), F32),
        scratch_shapes=[pltpu.VMEM(((ATTN_QB + 1) * ATTN_BLOCK, GROUP_W), F32),
                        pltpu.VMEM(((ATTN_QB + 1) * ATTN_BLOCK, GROUP_W), F32)],
        compiler_params=_params(("parallel",)),
    )(qkv, qkv, qkv, qkv, qkv, qkv, do, do, lse, lse, dlt, dlt, bias)


def _zcol(c):
    return pl.BlockSpec((TILE_MIX, GROUP_W), lambda i, c=c: (i, c))


def _zhalo(c):
    per = TILE_MIX // SUBLANES
    return pl.BlockSpec((SUBLANES, GROUP_W), lambda i, c=c: (jnp.maximum(i * per - 1, 0), c))


def _full(shape):
    return pl.BlockSpec(shape, lambda i: tuple(0 for _ in shape))


def _lru_gates(xb, wa_ref, wx_ref, ba, bx, lam):
    xbb = xb.astype(BF16)
    r = jax.nn.sigmoid(_dot(xbb, wa_ref[...]) + ba)
    ig = jax.nn.sigmoid(_dot(xbb, wx_ref[...]) + bx)
    nl = -lam
    sp = jnp.maximum(nl, 0.0) + jnp.log1p(jnp.exp(-jnp.abs(nl)))
    log_a = (-RG_C * r) * sp
    a = jnp.exp(log_a)
    mult = jnp.sqrt(_neg_expm1(2.0 * log_a))
    return r, ig, sp, a, mult


def _gmlp_spatial(ws_ref, vvb, head):
    outs = []
    for j in range(vvb.shape[0] // GMLP_CHUNK):
        blk = vvb[j * GMLP_CHUNK:(j + 1) * GMLP_CHUNK, :]
        acc = jnp.zeros((GMLP_CHUNK, GROUP_W), F32)
        for h in range(N_HEADS):
            acc = jnp.where(head[:GMLP_CHUNK] == h, _dot(ws_ref[h], blk), acc)
        outs.append(acc)
    return jnp.concatenate(outs, axis=0)


def mix_fwd(z, attn, wts):
    s = z.shape[0]
    d4, d16 = PATTERN_DILS[1], PATTERN_DILS[2]

    def body(ax_ref, ab_ref, ac_ref, ag_ref, rx_ref, rg_ref, cu_ref, cv_ref, cg_ref, dg_ref,
             axh_ref, ach_ref, rxh_ref, o1_ref, l1_ref, o4_ref, l4_ref, o16_ref, l16_ref,
             caw_ref, crw_ref, crb_ref, wa_ref, wx_ref, ba_ref, bx_ref, lam_ref, gng_ref, ws_ref, bs_ref,
             y_ref, hl_ref, o_ref, lse_ref, lse4_ref, lse16_ref, carry, *stage):
        st_a, st_b, st_c, st_d, st_e = (stage[2 * k:2 * k + 2] for k in range(5))
        i = pl.program_id(0)

        @pl.when(i == 0)
        def _():
            carry[...] = jnp.zeros_like(carry)

        nz = (i > 0).astype(F32)
        head = _head_of_lane((TILE_MIX, GROUP_W))

        pa = ac_ref[...] * ax_ref[...]
        pah = ach_ref[...] * axh_ref[...] * nz
        cv = caw_ref[2:3, :] * pa + caw_ref[1:2, :] * _shift_down(pa, pah, 1) + caw_ref[0:1, :] * _shift_down(pa, pah, 2)
        y_ref[:, 0:GROUP_W] = (ab_ref[...] * cv * _silu(ag_ref[...])).astype(BF16)

        rx = rx_ref[...]
        rxh = rxh_ref[...] * nz
        xb = (crw_ref[3:4, :] * rx + crw_ref[2:3, :] * _shift_down(rx, rxh, 1) + crw_ref[1:2, :] * _shift_down(rx, rxh, 2)
              + crw_ref[0:1, :] * _shift_down(rx, rxh, 3) + crb_ref[...])
        _, ig, _, a, mult = _lru_gates(xb, wa_ref, wx_ref, ba_ref[...], bx_ref[...], lam_ref[...])
        ca, cb = _scan_causal(a, mult * (ig * xb))
        hl = cb + ca * carry[SUBLANES - 1:SUBLANES, :]
        hl_ref[...] = hl
        carry[...] = hl[TILE_MIX - SUBLANES:, :]
        y_ref[:, GROUP_W:2 * GROUP_W] = (hl * _silu(rg_ref[...])).astype(BF16)

        u = _gelu(cu_ref[...])
        gv = _gelu(cv_ref[...])
        rs = lax.rsqrt(jnp.mean(gv * gv, axis=-1, keepdims=True) + NORM_EPS)
        vvb = (gv * rs * gng_ref[...]).astype(BF16)
        sp = _gmlp_spatial(ws_ref, vvb, head) + jnp.concatenate([bs_ref[...]] * (TILE_MIX // GMLP_CHUNK), axis=0)
        y_ref[:, 2 * GROUP_W:3 * GROUP_W] = (u * sp * _silu(cg_ref[...])).astype(BF16)

        ops = (o1_ref[...], _interleave_load(o4_ref, d4, st_a), _interleave_load(o16_ref, d16, st_b))
        lps = (l1_ref[...], _interleave_load(l4_ref, d4, st_c), _interleave_load(l16_ref, d16, st_d))
        m = jnp.maximum(jnp.maximum(lps[0], lps[1]), lps[2])
        zsum = jnp.zeros_like(m)
        o = jnp.zeros_like(m)
        for op, lp in zip(ops, lps):
            w = jnp.exp(lp - m)
            zsum = zsum + w
            o = o + w * op
        o = o / zsum
        lse = m + jnp.log(zsum)
        o_ref[...] = o
        lse_ref[...] = lse
        _deinterleave_store(lse, st_e, ((d4, lse4_ref), (d16, lse16_ref)))
        y_ref[:, 3 * GROUP_W:4 * GROUP_W] = (o * _silu(dg_ref[...])).astype(BF16)

    row = pl.BlockSpec((TILE_MIX, GROUP_W), lambda i: (i, 0))
    r4 = _residue_block(d4, TILE_MIX, GROUP_W)
    r16 = _residue_block(d16, TILE_MIX, GROUP_W)
    names = ("caw", "crw", "crb", "wa", "wx", "ba", "bx", "lam", "gng", "ws", "bs")
    in_specs = ([_zcol(c) for c in (C_AX, C_AB, C_AC, C_AG, C_RX, C_RG, C_CU, C_CV, C_CG, C_DG)]
                + [_zhalo(C_AX), _zhalo(C_AC), _zhalo(C_RX), row, row, r4, r4, r16, r16]
                + [_full(wts[k].shape) for k in names])
    return pl.pallas_call(
        body, name="mix_fwd", grid=(s // TILE_MIX,),
        in_specs=in_specs,
        out_specs=[pl.BlockSpec((TILE_MIX, D_MIX), lambda i: (i, 0)), row, row, row, r4, r16],
        out_shape=([jax.ShapeDtypeStruct((s, D_MIX), BF16)] + [jax.ShapeDtypeStruct((s, GROUP_W), F32)] * 3
                   + [_by_residue(s, d4, GROUP_W, F32), _by_residue(s, d16, GROUP_W, F32)]),
        scratch_shapes=[pltpu.VMEM((SUBLANES, GROUP_W), F32)] + _stage_scratch(TILE_MIX, GROUP_W, 5),
        compiler_params=_params(("arbitrary",)),
    )(*([z] * 13), *[a for pair in attn for a in pair], *[wts[k] for k in names])


def mix_bwd(dy, z, hl, dqkv, ddg, wts):
    s = z.shape[0]
    d4, d16 = PATTERN_DILS[1], PATTERN_DILS[2]
    n_tiles = s // TILE_MIX

    def body(dya_ref, dyb_ref, dyc_ref, ax_ref, ab_ref, ac_ref, ag_ref, rx_ref, rg_ref, cu_ref, cv_ref, cg_ref,
             axh_ref, ach_ref, rxh_ref, hl_ref, hlh_ref, dqkv1_ref, dqkv4_ref, dqkv16_ref, ddg_ref,
             caw_ref, crw_ref, crb_ref, wa_ref, wx_ref, ba_ref, bx_ref, lam_ref, gng_ref, ws_ref, wst_ref, bs_ref,
             dz_ref, ga_ref, gr_ref, gn_ref, gwa_ref, gwx_ref, gws_ref, gbs_ref,
             c_dcv, c_g, c_a, c_dxb, *stage):
        st_a, st_b = stage[:len(stage) // 2], stage[len(stage) // 2:]
        step = pl.program_id(0)
        i = n_tiles - 1 - step

        @pl.when(step == 0)
        def _():
            for r in (c_dcv, c_g, c_a, c_dxb, ga_ref, gr_ref, gn_ref, gwa_ref, gwx_ref, gws_ref, gbs_ref):
                r[...] = jnp.zeros_like(r)

        nz = (i > 0).astype(F32)
        head = _head_of_lane((TILE_MIX, GROUP_W))
        shp8 = (SUBLANES, GROUP_W)
        colsum = lambda v: jnp.sum(v, axis=0, keepdims=True)

        ax, ab, ac, ag = ax_ref[...], ab_ref[...], ac_ref[...], ag_ref[...]
        dya = dya_ref[...]
        pa = ac * ax
        pah = ach_ref[...] * axh_ref[...] * nz
        pa1 = _shift_down(pa, pah, 1)
        pa2 = _shift_down(pa, pah, 2)
        cv = caw_ref[2:3, :] * pa + caw_ref[1:2, :] * pa1 + caw_ref[0:1, :] * pa2
        sg = _silu(ag)
        dz_ref[:, C_AB * GROUP_W:(C_AB + 1) * GROUP_W] = (dya * cv * sg).astype(BF16)
        dz_ref[:, C_AG * GROUP_W:(C_AG + 1) * GROUP_W] = (dya * ab * cv * _dsilu(ag)).astype(BF16)
        dcv = dya * ab * sg
        nxt = c_dcv[...]
        dpa = caw_ref[2:3, :] * dcv + caw_ref[1:2, :] * _shift_up(dcv, nxt, 1) + caw_ref[0:1, :] * _shift_up(dcv, nxt, 2)
        c_dcv[...] = dcv[:SUBLANES, :]
        dz_ref[:, C_AC * GROUP_W:(C_AC + 1) * GROUP_W] = (dpa * ax).astype(BF16)
        dz_ref[:, C_AX * GROUP_W:(C_AX + 1) * GROUP_W] = (dpa * ac).astype(BF16)
        ga_ref[...] += (_put_row(shp8, 2, colsum(dcv * pa)) + _put_row(shp8, 1, colsum(dcv * pa1))
                        + _put_row(shp8, 0, colsum(dcv * pa2)))

        rx, rg = rx_ref[...], rg_ref[...]
        dyb = dyb_ref[...]
        rxh = rxh_ref[...] * nz
        rx1, rx2, rx3 = _shift_down(rx, rxh, 1), _shift_down(rx, rxh, 2), _shift_down(rx, rxh, 3)
        xb = crw_ref[3:4, :] * rx + crw_ref[2:3, :] * rx1 + crw_ref[1:2, :] * rx2 + crw_ref[0:1, :] * rx3 + crb_ref[...]
        lam = lam_ref[...]
        r, ig, sp, a, mult = _lru_gates(xb, wa_ref, wx_ref, ba_ref[...], bx_ref[...], lam)
        hl = hl_ref[...]
        hprev = _shift_down(hl, hlh_ref[...] * nz, 1)
        dz_ref[:, C_RG * GROUP_W:(C_RG + 1) * GROUP_W] = (dyb * hl * _dsilu(rg)).astype(BF16)
        dh = dyb * _silu(rg)
        a_next = _shift_up(a, c_a[...], 1)
        ca, cb = _scan_anticausal(a_next, dh)
        g = cb + ca * c_g[0:1, :]
        c_g[...] = g[:SUBLANES, :]
        c_a[...] = a[:SUBLANES, :]
        u = ig * xb
        da = g * hprev
        dmult = g * u
        du = g * mult
        dlog_a = da * a - dmult * (a * a) / mult
        dr = dlog_a * (-RG_C * sp)
        dga = dr * r * (1.0 - r)
        dgx = (du * xb) * ig * (1.0 - ig)
        dgab, dgxb = dga.astype(BF16), dgx.astype(BF16)
        dxb = du * ig + _dot_nt(dgab, wa_ref[...]) + _dot_nt(dgxb, wx_ref[...])
        xbb = xb.astype(BF16)
        gwa_ref[...] += _dot_tn(xbb, dgab)
        gwx_ref[...] += _dot_tn(xbb, dgxb)
        nxt = c_dxb[...]
        drx = (crw_ref[3:4, :] * dxb + crw_ref[2:3, :] * _shift_up(dxb, nxt, 1) + crw_ref[1:2, :] * _shift_up(dxb, nxt, 2)
               + crw_ref[0:1, :] * _shift_up(dxb, nxt, 3))
        c_dxb[...] = dxb[:SUBLANES, :]
        dz_ref[:, C_RX * GROUP_W:(C_RX + 1) * GROUP_W] = drx.astype(BF16)
        dlam = colsum(dlog_a * (-RG_C * r)) * (-jax.nn.sigmoid(-lam))
        gr_ref[...] += (_put_row(shp8, 3, colsum(dxb * rx)) + _put_row(shp8, 2, colsum(dxb * rx1))
                        + _put_row(shp8, 1, colsum(dxb * rx2)) + _put_row(shp8, 0, colsum(dxb * rx3))
                        + _put_row(shp8, 4, colsum(dxb)) + _put_row(shp8, 5, colsum(dga))
                        + _put_row(shp8, 6, colsum(dgx)) + _put_row(shp8, 7, dlam))

        cu, cvv, cg = cu_ref[...], cv_ref[...], cg_ref[...]
        dyc = dyc_ref[...]
        u_c, du_c = _gelu_and_grad(cu)
        gv, dgv_c = _gelu_and_grad(cvv)
        rs = lax.rsqrt(jnp.mean(gv * gv, axis=-1, keepdims=True) + NORM_EPS)
        vh = gv * rs
        gng = gng_ref[...]
        vvb = (vh * gng).astype(BF16)
        spat = _gmlp_spatial(ws_ref, vvb, head) + jnp.concatenate([bs_ref[...]] * (TILE_MIX // GMLP_CHUNK), axis=0)
        sgc = _silu(cg)
        dz_ref[:, C_CU * GROUP_W:(C_CU + 1) * GROUP_W] = (dyc * spat * sgc * du_c).astype(BF16)
        dz_ref[:, C_CG * GROUP_W:(C_CG + 1) * GROUP_W] = (dyc * u_c * spat * _dsilu(cg)).astype(BF16)
        dsp = dyc * u_c * sgc
        dspb = dsp.astype(BF16)
        tril = (lax.broadcasted_iota(jnp.int32, (GMLP_CHUNK, GMLP_CHUNK), 0)
                >= lax.broadcasted_iota(jnp.int32, (GMLP_CHUNK, GMLP_CHUNK), 1))
        head_c = head[:GMLP_CHUNK]
        dvv_parts = []
        gbs = jnp.zeros((GMLP_CHUNK, GROUP_W), F32)
        for j in range(TILE_MIX // GMLP_CHUNK):
            sl = slice(j * GMLP_CHUNK, (j + 1) * GMLP_CHUNK)
            dblk = dspb[sl, :]
            vblk = vvb[sl, :]
            gbs = gbs + dsp[sl, :]
            acc = jnp.zeros((GMLP_CHUNK, GROUP_W), F32)
            for h in range(N_HEADS):
                acc = jnp.where(head_c == h, _dot(wst_ref[h], dblk), acc)
                dm = jnp.where(head_c == h, dblk, jnp.zeros_like(dblk))
                gws_ref[h] += jnp.where(tril, _dot_nt(dm, vblk), 0.0)
            dvv_parts.append(acc)
        gbs_ref[...] += gbs
        dvv = jnp.concatenate(dvv_parts, axis=0)
        gn_ref[...] += _put_row(shp8, 0, colsum(dvv * vh))
        dvh = dvv * gng
        dgv = rs * (dvh - vh * jnp.mean(dvh * vh, axis=-1, keepdims=True))
        dz_ref[:, C_CV * GROUP_W:(C_CV + 1) * GROUP_W] = (dgv * dgv_c).astype(BF16)

        dsum = dqkv1_ref[...] + _interleave_load(dqkv4_ref, d4, st_a) + _interleave_load(dqkv16_ref, d16, st_b)
        dz_ref[:, C_DQ * GROUP_W:(C_DV + 1) * GROUP_W] = dsum.astype(BF16)
        dz_ref[:, C_DG * GROUP_W:(C_DG + 1) * GROUP_W] = ddg_ref[...].astype(BF16)

    per = TILE_MIX // SUBLANES
    qkv_w = 3 * GROUP_W
    rev = lambda c: pl.BlockSpec((TILE_MIX, GROUP_W), lambda t, c=c: (n_tiles - 1 - t, c))
    revh = lambda c: pl.BlockSpec((SUBLANES, GROUP_W),
                                  lambda t, c=c: (jnp.maximum((n_tiles - 1 - t) * per - 1, 0), c))
    revr = lambda dil: pl.BlockSpec((dil, TILE_MIX // dil, qkv_w), lambda t: (0, n_tiles - 1 - t, 0))
    names = ("caw", "crw", "crb", "wa", "wx", "ba", "bx", "lam", "gng", "ws", "wst", "bs")
    in_specs = ([rev(0), rev(1), rev(2)]
                + [rev(c) for c in (C_AX, C_AB, C_AC, C_AG, C_RX, C_RG, C_CU, C_CV, C_CG)]
                + [revh(C_AX), revh(C_AC), revh(C_RX), rev(0), revh(0),
                   pl.BlockSpec((TILE_MIX, qkv_w), lambda t: (n_tiles - 1 - t, 0)), revr(d4), revr(d16), rev(0)]
                + [_full(wts[k].shape) for k in names])
    small = jax.ShapeDtypeStruct((SUBLANES, GROUP_W), F32)
    sq = jax.ShapeDtypeStruct((GROUP_W, GROUP_W), F32)
    out_shape = [jax.ShapeDtypeStruct((s, D_IN), BF16), small, small, small, sq, sq,
                 jax.ShapeDtypeStruct((N_HEADS, GMLP_CHUNK, GMLP_CHUNK), F32),
                 jax.ShapeDtypeStruct((GMLP_CHUNK, GROUP_W), F32)]
    out_specs = ([pl.BlockSpec((TILE_MIX, D_IN), lambda t: (n_tiles - 1 - t, 0))]
                 + [_full(o.shape) for o in out_shape[1:]])
    return pl.pallas_call(
        body, name="mix_bwd", grid=(n_tiles,),
        in_specs=in_specs, out_specs=out_specs, out_shape=out_shape,
        scratch_shapes=[pltpu.VMEM((SUBLANES, GROUP_W), F32)] * 4 + _stage_scratch(TILE_MIX, qkv_w, 2),
        compiler_params=_params(("arbitrary",)),
    )(dy, dy, dy, *([z] * 12), hl, hl, *dqkv, ddg, *[wts[k] for k in names])


def _block_diag(w):
    eye = jnp.eye(N_HEADS, dtype=w.dtype)
    return (w[:, :, None, :] * eye[:, None, :, None]).reshape(GROUP_W, GROUP_W)


def _diag_blocks(g):
    g4 = g.reshape(N_HEADS, HEAD_DIM, N_HEADS, HEAD_DIM)
    return jnp.stack([g4[h, :, h, :] for h in range(N_HEADS)])


def _layer_weights(p, l):
    tril = jnp.tril(jnp.ones((GMLP_CHUNK, GMLP_CHUNK), dtype=bool))
    ws = jnp.where(tril[None], p["gmlp_ws"][l], 0.0).astype(BF16)
    row = lambda a: a[l][None, :]
    return dict(
        caw=p["conv_a_w"][l], crw=p["conv_r_w"][l], crb=row(p["conv_r_b"]),
        wa=_block_diag(p["lru_wa"][l]).astype(BF16), wx=_block_diag(p["lru_wx"][l]).astype(BF16),
        ba=row(p["lru_ba"]), bx=row(p["lru_bx"]), lam=row(p["lru_lambda"]), gng=row(p["gmlp_norm_g"]),
        ws=ws, wst=jnp.transpose(ws, (0, 2, 1)),
        bs=jnp.repeat(jnp.transpose(p["gmlp_bs"][l]), HEAD_DIM, axis=1))


def _flat(a):
    return a.reshape(a.shape[0] * a.shape[1], a.shape[2])


def _split(a, dil):
    return a.reshape(dil, a.shape[0] // dil, a.shape[1])


def local_step(x, tgt, p, w_in_b, w_out_b):
    depth = w_in_b.shape[0]
    saved = []
    for l in range(depth):
        wts = _layer_weights(p, l)
        z, h, *qkvs = in_fwd(x, p["norm_g"][l][None, :], w_in_b[l])
        qkvs = [_flat(q) if q.ndim == 3 else q for q in qkvs]
        attn = []
        for q, d in zip(qkvs, PATTERN_DILS):
            o_p, lse_p = attn_fwd(q, d)
            attn.append((o_p, lse_p) if d == 1 else (_split(o_p, d), _split(lse_p, d)))
        y, hl, o, lse, lse4, lse16 = mix_fwd(z, attn, wts)
        saved.append(dict(x=x, z=z, h=h, y=y, hl=hl, o=o, qkvs=qkvs, lses=(lse, _flat(lse4), _flat(lse16)), wts=wts))
        x = out_fwd(y, w_out_b[l], x)

    loss, dx, dfg = loss_head(x, p["final_g"][None, :], tgt)
    grads = {k: [None] * depth for k in
             ("norm_g", "w_in", "conv_a_w", "conv_r_w", "conv_r_b", "lru_wa", "lru_ba", "lru_wx", "lru_bx",
              "lru_lambda", "gmlp_norm_g", "gmlp_ws", "gmlp_bs", "w_out")}
    for l in reversed(range(depth)):
        sv = saved[l]
        dy, ddg, do1, do4, do16, dl1, dl4, dl16 = out_bwd(dx, w_out_b[l], sv["z"], sv["o"])
        grads["w_out"][l] = matmul_tn(sv["y"], dx, 1)
        dqkv = []
        for q, do, lse, dl, d in zip(sv["qkvs"], (do1, _flat(do4), _flat(do16)), sv["lses"],
                                     (dl1, _flat(dl4), _flat(dl16)), PATTERN_DILS):
            g = attn_bwd(q, do, lse, dl, d)
            dqkv.append(g if d == 1 else _split(g, d))
        dz, ga, gr, gn, gwa, gwx, gws, gbs = mix_bwd(dy, sv["z"], sv["hl"], dqkv, ddg, sv["wts"])
        dx, dgn = in_bwd(dz, w_in_b[l], sv["x"], p["norm_g"][l][None, :], dx)
        grads["w_in"][l] = matmul_tn(sv["h"], dz, 2)
        grads["norm_g"][l] = dgn[0]
        grads["conv_a_w"][l] = ga[:3]
        grads["conv_r_w"][l] = gr[:4]
        grads["conv_r_b"][l] = gr[4]
        grads["lru_ba"][l] = gr[5]
        grads["lru_bx"][l] = gr[6]
        grads["lru_lambda"][l] = gr[7]
        grads["gmlp_norm_g"][l] = gn[0]
        grads["lru_wa"][l] = _diag_blocks(gwa)
        grads["lru_wx"][l] = _diag_blocks(gwx)
        grads["gmlp_ws"][l] = gws
        grads["gmlp_bs"][l] = jnp.transpose(gbs.reshape(GMLP_CHUNK, N_HEADS, HEAD_DIM).sum(-1))
    grads = {k: jnp.stack(v) for k, v in grads.items()}
    grads["final_g"] = dfg[0]
    return loss, dx, grads


MESH = pl.DeviceIdType.MESH
N_CHIPS = 4
N_DEV = 8
ANY = pl.BlockSpec(memory_space=pl.ANY)


def _place():
    x, y, c = lax.axis_index("x"), lax.axis_index("y"), lax.axis_index("c")
    chips = [(1 - x, y), (x, 1 - y), (1 - x, 1 - y)]
    return x, y, c, chips


def _remote(src, dst, ssem, rsem, to):
    return pltpu.make_async_remote_copy(src_ref=src, dst_ref=dst, send_sem=ssem, recv_sem=rsem,
                                        device_id=to, device_id_type=MESH)


def gather_sharded(w1, w2, w3):
    h1, h2 = w1.shape[0] // 2, w2.shape[0] // 2

    def body(w1_ref, w2_ref, w3_ref, g1_ref, g2_ref, g3_ref, ssem, rsem, fssem, frsem, lsem):
        x, y, c, chips = _place()
        me = 2 * x + y
        sibling = (x, y, 1 - c)
        big = ((w1_ref, g1_ref, h1), (w2_ref, g2_ref, h2))

        local = [pltpu.make_async_copy(w1_ref, g1_ref.at[me], lsem.at[0]),
                 pltpu.make_async_copy(w2_ref, g2_ref.at[me], lsem.at[1]),
                 pltpu.make_async_copy(w3_ref, g3_ref.at[me], lsem.at[2])]
        for cp in local:
            cp.start()

        def ici(a, k, origin, half):
            if a == 2:
                src = w3_ref if origin is None else g3_ref.at[origin]
                dst = g3_ref.at[me if origin is None else origin]
            else:
                w_ref, g_ref, h = big[a]
                rows = pl.ds(half * h, h)
                src = w_ref.at[rows, :] if origin is None else g_ref.at[origin, rows, :]
                dst = g_ref.at[me if origin is None else origin, rows, :]
            return _remote(src, dst, ssem.at[3 * a + k], rsem.at[3 * a + k], (*chips[k], c))

        def fwd(a, k, origin, half):
            w_ref, g_ref, h = big[a]
            rows = pl.ds(half * h, h)
            blk = g_ref.at[origin, rows, :]
            return _remote(blk, blk, fssem.at[2 * k + a], frsem.at[2 * k + a], sibling)

        sends = [ici(a, k, None, c) for k in range(3) for a in range(3)]
        for cp in sends:
            cp.start()
        passed = []
        for k, (cx, cy) in enumerate(chips):
            origin = 2 * cx + cy
            for a in range(3):
                ici(a, k, origin, c).wait_recv()
                if a < 2:
                    f = fwd(a, k, origin, c)
                    f.start()
                    passed.append(f)
        for k, (cx, cy) in enumerate(chips):
            for a in range(2):
                fwd(a, k, 2 * cx + cy, 1 - c).wait_recv()
        for cp in sends + passed:
            cp.wait_send()
        for cp in local:
            cp.wait()

    return pl.pallas_call(
        body, name="gather_sharded",
        in_specs=[ANY, ANY, ANY], out_specs=[ANY, ANY, ANY],
        out_shape=[jax.ShapeDtypeStruct((N_CHIPS,) + w.shape, w.dtype) for w in (w1, w2, w3)],
        scratch_shapes=[pltpu.SemaphoreType.DMA((9,)), pltpu.SemaphoreType.DMA((9,)),
                        pltpu.SemaphoreType.DMA((6,)), pltpu.SemaphoreType.DMA((6,)),
                        pltpu.SemaphoreType.DMA((3,))],
    )(w1, w2, w3)


def scatter_to_chips(s1, s2):
    def body(s1_ref, s2_ref, r1_ref, r2_ref, ssem, rsem):
        x, y, c, chips = _place()
        copies = []
        for k, (cx, cy) in enumerate(chips):
            dest = 2 * cx + cy
            for a, (s_ref, r_ref) in enumerate(((s1_ref, r1_ref), (s2_ref, r2_ref))):
                copies.append(_remote(s_ref.at[dest], r_ref.at[k], ssem.at[2 * k + a], rsem.at[2 * k + a], (cx, cy, c)))
        for cp in copies:
            cp.start()
        for cp in copies:
            cp.wait()

    return pl.pallas_call(
        body, name="scatter_to_chips",
        in_specs=[ANY, ANY], out_specs=[ANY, ANY],
        out_shape=[jax.ShapeDtypeStruct((3,) + s.shape[1:], s.dtype) for s in (s1, s2)],
        scratch_shapes=[pltpu.SemaphoreType.DMA((6,)), pltpu.SemaphoreType.DMA((6,))],
    )(s1, s2)


def sibling_exchange(p1, p2):
    def body(p1_ref, p2_ref, q1_ref, q2_ref, ssem, rsem):
        x, y, c, _ = _place()
        copies = [_remote(p_ref, q_ref, ssem.at[a], rsem.at[a], (x, y, 1 - c))
                  for a, (p_ref, q_ref) in enumerate(((p1_ref, q1_ref), (p2_ref, q2_ref)))]
        for cp in copies:
            cp.start()
        for cp in copies:
            cp.wait()

    return pl.pallas_call(
        body, name="sibling_exchange",
        in_specs=[ANY, ANY], out_specs=[ANY, ANY],
        out_shape=[jax.ShapeDtypeStruct(p.shape, p.dtype) for p in (p1, p2)],
        scratch_shapes=[pltpu.SemaphoreType.DMA((2,)), pltpu.SemaphoreType.DMA((2,))],
    )(p1, p2)


def gather_all(v):
    m_per, n = v.shape

    def body(x_ref, out_ref, send_sems, recv_sems, local_sem):
        x, y, c, chips = _place()
        me, sibling = (x, y, c), (x, y, 1 - c)

        def rows(px, py, pc):
            return out_ref.at[pl.ds((4 * px + 2 * py + pc) * m_per, m_per), :]

        def copy(k, block, to, src=None):
            return _remote(rows(*block) if src is None else src, rows(*block), send_sems.at[k], recv_sems.at[k], to)

        mine = pltpu.make_async_copy(x_ref, rows(*me), local_sem)
        mine.start()
        first = [copy(0, me, sibling, src=x_ref)]
        first += [copy(1 + j, me, (*chip, c), src=x_ref) for j, chip in enumerate(chips)]
        for cp in first:
            cp.start()
        passed = [copy(4 + j, (*chip, c), sibling) for j, chip in enumerate(chips)]
        for j, chip in enumerate(chips):
            copy(1 + j, (*chip, c), me).wait_recv()
            passed[j].start()
        copy(0, sibling, me).wait_recv()
        for j, chip in enumerate(chips):
            copy(4 + j, (*chip, 1 - c), me).wait_recv()
        for cp in first + passed:
            cp.wait_send()
        mine.wait()

    return pl.pallas_call(
        body, name="gather_all",
        out_shape=jax.ShapeDtypeStruct((N_DEV * m_per, n), v.dtype),
        in_specs=[pl.BlockSpec(memory_space=pltpu.VMEM)],
        out_specs=pl.BlockSpec(memory_space=pltpu.VMEM),
        scratch_shapes=[pltpu.SemaphoreType.DMA((7,)), pltpu.SemaphoreType.DMA((7,)), pltpu.SemaphoreType.DMA],
        compiler_params=pltpu.CompilerParams(vmem_limit_bytes=VMEM_LIMIT),
    )(v)


TILE_ROWS = 256


def sum_partials(own, parts):
    r, c = own.shape
    k = parts.shape[0]

    def body(o_ref, p_ref, out_ref):
        acc = o_ref[...]
        for i in range(k):
            acc = acc + p_ref[i].astype(F32)
        out_ref[...] = acc

    return pl.pallas_call(
        body, name="sum_partials", grid=(r // TILE_ROWS,),
        in_specs=[pl.BlockSpec((TILE_ROWS, c), lambda i: (i, 0)), pl.BlockSpec((k, TILE_ROWS, c), lambda i: (0, i, 0))],
        out_specs=pl.BlockSpec((TILE_ROWS, c), lambda i: (i, 0)),
        out_shape=jax.ShapeDtypeStruct((r, c), F32),
        compiler_params=_params(("parallel",)),
    )(own, parts)


def adamw(w, ga, gb, m, v):
    r, c = w.shape
    tile = min(TILE_ROWS, r)

    def body(w_ref, ga_ref, gb_ref, m_ref, v_ref, g_ref, d_ref, m2_ref, v2_ref):
        g = ga_ref[...] + gb_ref[...]
        g_ref[...] = g
        m2 = ADAM_B1 * m_ref[...] + (1.0 - ADAM_B1) * g
        v2 = ADAM_B2 * v_ref[...] + (1.0 - ADAM_B2) * (g * g)
        m2_ref[...] = m2
        v2_ref[...] = v2
        m_hat = m2 / (1.0 - ADAM_B1 ** ADAM_STEP)
        v_hat = v2 / (1.0 - ADAM_B2 ** ADAM_STEP)
        d_ref[...] = -ADAM_LR * (m_hat / (jnp.sqrt(v_hat) + ADAM_EPS) + ADAM_WD * w_ref[...])

    spec = pl.BlockSpec((tile, c), lambda i: (i, 0))
    return pl.pallas_call(
        body, name="adamw", grid=(r // tile,),
        in_specs=[spec] * 5, out_specs=[spec] * 4,
        out_shape=[jax.ShapeDtypeStruct((r, c), F32)] * 4,
        compiler_params=_params(("parallel",)),
    )(w, ga, gb, m, v)


REPLICATED = ("norm_g", "conv_r_b", "lru_wa", "lru_ba", "lru_wx", "lru_bx", "lru_lambda", "gmlp_norm_g",
              "gmlp_ws", "gmlp_bs", "final_g")
CHIP_SHARDED_SMALL = ("conv_a_w", "conv_r_w")
PACK_LANES = 128


def _pack(arrays):
    flat = jnp.concatenate([a.reshape(-1) for a in arrays])
    pad = (-flat.shape[0]) % (TILE_ROWS * PACK_LANES)
    return jnp.pad(flat, (0, pad)).reshape(-1, PACK_LANES)


def _unpack(packed, shapes):
    flat = packed.reshape(-1)
    out, off = [], 0
    for shp in shapes:
        n = math.prod(shp)
        out.append(flat[off:off + n].reshape(shp))
        off += n
    return out


def kernel(x, norm_g, w_in, conv_a_w, conv_r_w, conv_r_b, lru_wa, lru_ba, lru_wx, lru_bx, lru_lambda, gmlp_norm_g, gmlp_ws, gmlp_bs, w_out, final_g, loss_target, m_norm_g, m_w_in, m_conv_a_w, m_conv_r_w, m_conv_r_b, m_lru_wa, m_lru_ba, m_lru_wx, m_lru_bx, m_lru_lambda, m_gmlp_norm_g, m_gmlp_ws, m_gmlp_bs, m_w_out, m_final_g, v_norm_g, v_w_in, v_conv_a_w, v_conv_r_w, v_conv_r_b, v_lru_wa, v_lru_ba, v_lru_wx, v_lru_bx, v_lru_lambda, v_gmlp_norm_g, v_gmlp_ws, v_gmlp_bs, v_w_out, v_final_g):
    names = ("norm_g", "w_in", "conv_a_w", "conv_r_w", "conv_r_b", "lru_wa", "lru_ba", "lru_wx", "lru_bx",
             "lru_lambda", "gmlp_norm_g", "gmlp_ws", "gmlp_bs", "w_out", "final_g")
    w = dict(zip(names, (norm_g, w_in, conv_a_w, conv_r_w, conv_r_b, lru_wa, lru_ba, lru_wx, lru_bx, lru_lambda,
                         gmlp_norm_g, gmlp_ws, gmlp_bs, w_out, final_g)))
    m = dict(zip(names, (m_norm_g, m_w_in, m_conv_a_w, m_conv_r_w, m_conv_r_b, m_lru_wa, m_lru_ba, m_lru_wx, m_lru_bx,
                         m_lru_lambda, m_gmlp_norm_g, m_gmlp_ws, m_gmlp_bs, m_w_out, m_final_g)))
    v = dict(zip(names, (v_norm_g, v_w_in, v_conv_a_w, v_conv_r_w, v_conv_r_b, v_lru_wa, v_lru_ba, v_lru_wx, v_lru_bx,
                         v_lru_lambda, v_gmlp_norm_g, v_gmlp_ws, v_gmlp_bs, v_w_out, v_final_g)))
    depth, _, in_cols = w_in.shape
    out_rows = w_out.shape[1]
    conv_ch = conv_a_w.shape[2]
    chip = 2 * lax.axis_index("x") + lax.axis_index("y")

    taps = conv_a_w.shape[1] + conv_r_w.shape[1]
    g1, g2, g3 = gather_sharded(
        w_in.astype(BF16).reshape(depth * D_MODEL, in_cols),
        w_out.astype(BF16).reshape(depth * out_rows, D_MODEL),
        jnp.concatenate([conv_a_w, conv_r_w], axis=1).reshape(depth * taps, conv_ch))
    w_in_b = g1.reshape(N_CHIPS, depth, D_MODEL, in_cols).transpose(1, 2, 0, 3).reshape(depth, D_MODEL, D_IN)
    w_out_b = g2.reshape(N_CHIPS, depth, out_rows, D_MODEL).transpose(1, 0, 2, 3).reshape(depth, D_MIX, D_MODEL)
    conv = g3.reshape(N_CHIPS, depth, taps, conv_ch).transpose(1, 2, 0, 3).reshape(depth, taps, GROUP_W)
    p = dict(w)
    p["conv_a_w"] = conv[:, :conv_a_w.shape[1]]
    p["conv_r_w"] = conv[:, conv_a_w.shape[1]:]

    loss8, dx, grads = local_step(x[0], loss_target[0], p, w_in_b, w_out_b)
    loss = lax.psum(loss8[0, 0], ("x", "y", "c"))

    gi = grads["w_in"].reshape(depth, D_MODEL, N_CHIPS, in_cols)
    go = grads["w_out"].reshape(depth, N_CHIPS, out_rows, D_MODEL)
    s1 = gi.transpose(2, 0, 1, 3).reshape(N_CHIPS, depth * D_MODEL, in_cols).astype(BF16)
    s2 = go.transpose(1, 0, 2, 3).reshape(N_CHIPS, depth * out_rows, D_MODEL).astype(BF16)
    own1 = lax.dynamic_index_in_dim(gi, chip, axis=2, keepdims=False).reshape(depth * D_MODEL, in_cols)
    own2 = lax.dynamic_index_in_dim(go, chip, axis=1, keepdims=False).reshape(depth * out_rows, D_MODEL)
    r1, r2 = scatter_to_chips(s1, s2)
    p1 = sum_partials(own1, r1)
    p2 = sum_partials(own2, r2)
    q1, q2 = sibling_exchange(p1, p2)
    res = {}
    res["w_in"] = [t.reshape(w_in.shape) for t in
                   adamw(w_in.reshape(p1.shape), p1, q1, m_w_in.reshape(p1.shape), v_w_in.reshape(p1.shape))]
    res["w_out"] = [t.reshape(w_out.shape) for t in
                    adamw(w_out.reshape(p2.shape), p2, q2, m_w_out.reshape(p2.shape), v_w_out.reshape(p2.shape))]

    small = REPLICATED + CHIP_SHARDED_SMALL
    packed = _pack([grads[k] for k in small])
    rows = packed.shape[0]
    allp = gather_all(packed).reshape(N_DEV, rows, PACK_LANES)
    total = sum_partials(allp[0], allp[1:])
    gs = dict(zip(small, _unpack(total, [grads[k].shape for k in small])))
    for k in CHIP_SHARDED_SMALL:
        gs[k] = lax.dynamic_slice_in_dim(gs[k], chip * conv_ch, conv_ch, axis=2)
    wp, gp, mp, vp = (_pack([d[k] for k in small]) for d in (w, gs, m, v))
    outs = adamw(wp, gp, jnp.zeros_like(gp), mp, vp)
    shapes = [w[k].shape for k in small]
    unpacked = [_unpack(o, shapes) for o in outs]
    for i, k in enumerate(small):
        res[k] = [u[i] for u in unpacked]

    return (loss, dx[None], *[res[k][0] for k in names], *[res[k][1] for k in names],
            *[res[k][2] for k in names], *[res[k][3] for k in names])
```

```python
import functools
import math

import jax
import jax.numpy as jnp
import numpy as np
from jax import lax
from jax.experimental import pallas as pl
from jax.experimental.pallas import tpu as pltpu

F32 = jnp.float32
BF16 = jnp.bfloat16

D_MODEL = 1024
GROUP_W = 256
N_HEADS = 4
HEAD_DIM = 64
N_CHUNKS = 13
D_IN = N_CHUNKS * GROUP_W
D_MIX = 4 * GROUP_W
NORM_EPS = 1e-6
RG_C = 8.0
GMLP_CHUNK = 128
ATTN_BLOCK = 128
PATTERN_DILS = (1, 4, 16)
N_PATTERNS = len(PATTERN_DILS)
ALIBI_SLOPES = tuple(2.0 ** (-8.0 * (h + 1) / N_HEADS) for h in range(N_HEADS))
ATTN_SCALE = 1.0 / math.sqrt(HEAD_DIM)
NEG_BIG = -1e30

ADAM_LR = 0.001
ADAM_B1 = 0.9
ADAM_B2 = 0.999
ADAM_EPS = 1e-08
ADAM_WD = 0.01
ADAM_STEP = 10

C_AX, C_AB, C_AC, C_AG, C_RX, C_RG, C_CU, C_CV, C_CG, C_DQ, C_DK, C_DV, C_DG = range(13)

SUBLANES = 8
LANES = 128
VMEM_LIMIT = 56 * 1024 * 1024
TILE_IN = 256
TILE_MIX = 256
TILE_DW = 512
ATTN_QB = 4
GELU_K0 = math.sqrt(2.0 / math.pi)
GELU_K1 = 0.044715


def _params(sem):
    return pltpu.CompilerParams(dimension_semantics=sem, vmem_limit_bytes=VMEM_LIMIT)


def _silu(x):
    return x * jax.nn.sigmoid(x)


def _dsilu(x):
    s = jax.nn.sigmoid(x)
    return s * (1.0 + x * (1.0 - s))


def _gelu(x):
    return 0.5 * x * (1.0 + jnp.tanh(GELU_K0 * (x + GELU_K1 * x * x * x)))


def _gelu_and_grad(x):
    t = jnp.tanh(GELU_K0 * (x + GELU_K1 * x * x * x))
    g = 0.5 * x * (1.0 + t)
    dg = 0.5 * (1.0 + t) + 0.5 * x * (1.0 - t * t) * GELU_K0 * (1.0 + 3.0 * GELU_K1 * x * x)
    return g, dg


def _neg_expm1(x):
    series = x * (1.0 + x * (0.5 + x * (1.0 / 6.0 + x * (1.0 / 24.0 + x * (1.0 / 120.0)))))
    return -jnp.where(x > -0.05, series, jnp.exp(x) - 1.0)


def _shift_down(v, halo, k):
    r = pltpu.roll(v, k, 0)
    rh = pltpu.roll(halo, k, 0)
    row = lax.broadcasted_iota(jnp.int32, halo.shape, 0)
    top = jnp.where(row < k, rh, r[:SUBLANES])
    return jnp.concatenate([top, r[SUBLANES:]], axis=0)


def _shift_up(v, halo, k):
    t = v.shape[0]
    r = pltpu.roll(v, t - k, 0)
    rh = pltpu.roll(halo, SUBLANES - k, 0)
    row = lax.broadcasted_iota(jnp.int32, halo.shape, 0)
    bot = jnp.where(row >= SUBLANES - k, rh, r[t - SUBLANES:])
    return jnp.concatenate([r[:t - SUBLANES], bot], axis=0)


def _scan_causal(a, b):
    t = a.shape[0]
    row = lax.broadcasted_iota(jnp.int32, a.shape, 0)
    d = 1
    while d < t:
        m = row >= d
        a_s = jnp.where(m, pltpu.roll(a, d, 0), 1.0)
        b_s = jnp.where(m, pltpu.roll(b, d, 0), 0.0)
        b = a * b_s + b
        a = a * a_s
        d *= 2
    return a, b


def _scan_anticausal(a, b):
    t = a.shape[0]
    row = lax.broadcasted_iota(jnp.int32, a.shape, 0)
    d = 1
    while d < t:
        m = row < t - d
        a_s = jnp.where(m, pltpu.roll(a, t - d, 0), 1.0)
        b_s = jnp.where(m, pltpu.roll(b, t - d, 0), 0.0)
        b = a * b_s + b
        a = a * a_s
        d *= 2
    return a, b


def _head_of_lane(shape):
    return lax.broadcasted_iota(jnp.int32, shape, len(shape) - 1) // HEAD_DIM


def _put_row(acc_shape, k, row_vec):
    row = lax.broadcasted_iota(jnp.int32, acc_shape, 0)
    return jnp.where(row == k, jnp.broadcast_to(row_vec, acc_shape), 0.0)


def _dot(a, b):
    return jnp.dot(a, b, preferred_element_type=F32)


def _dot_nt(a, b):
    return lax.dot_general(a, b, (((1,), (1,)), ((), ())), preferred_element_type=F32)


def _dot_tn(a, b):
    return lax.dot_general(a, b, (((0,), (0,)), ((), ())), preferred_element_type=F32)


def _deinterleave_store(val, stage, outs):
    t, c = val.shape
    for hh in range(c // LANES):
        stage[hh][...] = val[:, hh * LANES:(hh + 1) * LANES].astype(F32)
    for dil, ref in outs:
        for r in range(dil):
            for hh in range(c // LANES):
                ref[r, :, hh * LANES:(hh + 1) * LANES] = stage[hh][pl.ds(r, t // dil, stride=dil), :].astype(ref.dtype)


def _interleave_load(ref, dil, stage):
    _, n, c = ref.shape
    for r in range(dil):
        for hh in range(c // LANES):
            stage[hh][pl.ds(r, n, stride=dil), :] = ref[r, :, hh * LANES:(hh + 1) * LANES].astype(F32)
    return jnp.concatenate([stage[hh][...] for hh in range(c // LANES)], axis=1)


def _stage_scratch(tile, cols, copies):
    return [pltpu.VMEM((tile, LANES), F32)] * (copies * (cols // LANES))


def _by_residue(s, dil, cols, dtype):
    return jax.ShapeDtypeStruct((dil, s // dil, cols), dtype)


def _residue_block(dil, tile, cols):
    return pl.BlockSpec((dil, tile // dil, cols), lambda i: (0, i, 0))


def in_fwd(x, g, w):
    s = x.shape[0]
    qkv_w = 3 * GROUP_W

    def body(x_ref, g_ref, w_ref, z_ref, h_ref, qkv1_ref, qkv4_ref, qkv16_ref, *stage):
        xv = x_ref[...]
        rs = lax.rsqrt(jnp.mean(xv * xv, axis=-1, keepdims=True) + NORM_EPS)
        h = (xv * rs * g_ref[...]).astype(BF16)
        h_ref[...] = h
        z = _dot(h, w_ref[...])
        z_ref[...] = z
        qkv = z[:, C_DQ * GROUP_W:(C_DV + 1) * GROUP_W]
        qkv1_ref[...] = qkv.astype(BF16)
        _deinterleave_store(qkv, stage, ((PATTERN_DILS[1], qkv4_ref), (PATTERN_DILS[2], qkv16_ref)))

    return pl.pallas_call(
        body, name="in_fwd", grid=(s // TILE_IN,),
        in_specs=[pl.BlockSpec((TILE_IN, D_MODEL), lambda i: (i, 0)),
                  pl.BlockSpec((1, D_MODEL), lambda i: (0, 0)),
                  pl.BlockSpec((D_MODEL, D_IN), lambda i: (0, 0))],
        out_specs=[pl.BlockSpec((TILE_IN, D_IN), lambda i: (i, 0)),
                   pl.BlockSpec((TILE_IN, D_MODEL), lambda i: (i, 0)),
                   pl.BlockSpec((TILE_IN, qkv_w), lambda i: (i, 0)),
                   _residue_block(PATTERN_DILS[1], TILE_IN, qkv_w),
                   _residue_block(PATTERN_DILS[2], TILE_IN, qkv_w)],
        out_shape=[jax.ShapeDtypeStruct((s, D_IN), F32), jax.ShapeDtypeStruct((s, D_MODEL), BF16),
                   jax.ShapeDtypeStruct((s, qkv_w), BF16),
                   _by_residue(s, PATTERN_DILS[1], qkv_w, BF16), _by_residue(s, PATTERN_DILS[2], qkv_w, BF16)],
        scratch_shapes=_stage_scratch(TILE_IN, qkv_w, 1),
        compiler_params=_params(("parallel",)),
    )(x, g, w)


def out_fwd(y, w, x):
    s = x.shape[0]

    def body(y_ref, w_ref, x_ref, o_ref):
        o_ref[...] = x_ref[...] + _dot(y_ref[...], w_ref[...])

    return pl.pallas_call(
        body, name="out_fwd", grid=(s // TILE_IN,),
        in_specs=[pl.BlockSpec((TILE_IN, D_MIX), lambda i: (i, 0)),
                  pl.BlockSpec((D_MIX, D_MODEL), lambda i: (0, 0)),
                  pl.BlockSpec((TILE_IN, D_MODEL), lambda i: (i, 0))],
        out_specs=pl.BlockSpec((TILE_IN, D_MODEL), lambda i: (i, 0)),
        out_shape=jax.ShapeDtypeStruct((s, D_MODEL), F32),
        compiler_params=_params(("parallel",)),
    )(y, w, x)


def out_bwd(dx, w, z, o):
    s = dx.shape[0]
    abc = 3 * GROUP_W

    def body(dx_ref, w_ref, dg_ref, o_ref, dy_ref, ddg_ref, do1_ref, do4_ref, do16_ref, dl1_ref, dl4_ref, dl16_ref,
             *stage):
        stage_a, stage_b = stage[:2], stage[2:]
        dy = _dot_nt(dx_ref[...].astype(BF16), w_ref[...])
        dy_ref[...] = dy[:, :abc]
        dyd = dy[:, abc:]
        head = _head_of_lane((TILE_IN, GROUP_W))
        dg = dg_ref[...]
        o = o_ref[...]
        do = dyd * _silu(dg)
        ddg_ref[...] = dyd * o * _dsilu(dg)
        prod = do * o
        dl = jnp.zeros_like(prod)
        for h in range(N_HEADS):
            sm = jnp.sum(jnp.where(head == h, prod, 0.0), axis=-1, keepdims=True)
            dl = jnp.where(head == h, sm, dl)
        do1_ref[...] = do.astype(BF16)
        dl1_ref[...] = dl
        _deinterleave_store(do, stage_a, ((PATTERN_DILS[1], do4_ref), (PATTERN_DILS[2], do16_ref)))
        _deinterleave_store(dl, stage_b, ((PATTERN_DILS[1], dl4_ref), (PATTERN_DILS[2], dl16_ref)))

    row = pl.BlockSpec((TILE_IN, GROUP_W), lambda i: (i, 0))
    r4 = _residue_block(PATTERN_DILS[1], TILE_IN, GROUP_W)
    r16 = _residue_block(PATTERN_DILS[2], TILE_IN, GROUP_W)
    return pl.pallas_call(
        body, name="out_bwd", grid=(s // TILE_IN,),
        in_specs=[pl.BlockSpec((TILE_IN, D_MODEL), lambda i: (i, 0)),
                  pl.BlockSpec((D_MIX, D_MODEL), lambda i: (0, 0)),
                  pl.BlockSpec((TILE_IN, GROUP_W), lambda i: (i, C_DG)), row],
        out_specs=[pl.BlockSpec((TILE_IN, abc), lambda i: (i, 0)), row, row, r4, r16, row, r4, r16],
        out_shape=[jax.ShapeDtypeStruct((s, abc), F32), jax.ShapeDtypeStruct((s, GROUP_W), F32),
                   jax.ShapeDtypeStruct((s, GROUP_W), BF16),
                   _by_residue(s, PATTERN_DILS[1], GROUP_W, BF16), _by_residue(s, PATTERN_DILS[2], GROUP_W, BF16),
                   jax.ShapeDtypeStruct((s, GROUP_W), F32),
                   _by_residue(s, PATTERN_DILS[1], GROUP_W, F32), _by_residue(s, PATTERN_DILS[2], GROUP_W, F32)],
        scratch_shapes=_stage_scratch(TILE_IN, GROUP_W, 2),
        compiler_params=_params(("parallel",)),
    )(dx, w, z, o)


def in_bwd(dz, w, x, g, dx_next):
    s = x.shape[0]

    def body(dz_ref, w_ref, x_ref, g_ref, dxn_ref, dx_ref, dg_ref):
        @pl.when(pl.program_id(0) == 0)
        def _():
            dg_ref[...] = jnp.zeros_like(dg_ref)

        dh = _dot_nt(dz_ref[...], w_ref[...])
        xv = x_ref[...]
        rs = lax.rsqrt(jnp.mean(xv * xv, axis=-1, keepdims=True) + NORM_EPS)
        xh = xv * rs
        dg_ref[...] += _put_row(dg_ref.shape, 0, jnp.sum(dh * xh, axis=0, keepdims=True))
        dn = dh * g_ref[...]
        dx_ref[...] = dxn_ref[...] + rs * (dn - xh * jnp.mean(dn * xh, axis=-1, keepdims=True))

    return pl.pallas_call(
        body, name="in_bwd", grid=(s // TILE_IN,),
        in_specs=[pl.BlockSpec((TILE_IN, D_IN), lambda i: (i, 0)),
                  pl.BlockSpec((D_MODEL, D_IN), lambda i: (0, 0)),
                  pl.BlockSpec((TILE_IN, D_MODEL), lambda i: (i, 0)),
                  pl.BlockSpec((1, D_MODEL), lambda i: (0, 0)),
                  pl.BlockSpec((TILE_IN, D_MODEL), lambda i: (i, 0))],
        out_specs=[pl.BlockSpec((TILE_IN, D_MODEL), lambda i: (i, 0)),
                   pl.BlockSpec((SUBLANES, D_MODEL), lambda i: (0, 0))],
        out_shape=[jax.ShapeDtypeStruct((s, D_MODEL), F32), jax.ShapeDtypeStruct((SUBLANES, D_MODEL), F32)],
        compiler_params=_params(("arbitrary",)),
    )(dz, w, x, g, dx_next)


def matmul_tn(a, b, n_split):
    s, m = a.shape
    n = b.shape[1]
    tn = n // n_split

    def body(a_ref, b_ref, o_ref):
        @pl.when(pl.program_id(1) == 0)
        def _():
            o_ref[...] = jnp.zeros_like(o_ref)

        o_ref[...] += _dot_tn(a_ref[...], b_ref[...].astype(BF16))

    return pl.pallas_call(
        body, name="matmul_tn", grid=(n_split, s // TILE_DW),
        in_specs=[pl.BlockSpec((TILE_DW, m), lambda j, k: (k, 0)),
                  pl.BlockSpec((TILE_DW, tn), lambda j, k: (k, j))],
        out_specs=pl.BlockSpec((m, tn), lambda j, k: (0, j)),
        out_shape=jax.ShapeDtypeStruct((m, n), F32),
        compiler_params=_params(("parallel", "arbitrary")),
    )(a, b)


def loss_head(x, g, tgt):
    s = x.shape[0]

    def body(x_ref, g_ref, t_ref, l_ref, dx_ref, dg_ref):
        @pl.when(pl.program_id(0) == 0)
        def _():
            l_ref[...] = jnp.zeros_like(l_ref)
            dg_ref[...] = jnp.zeros_like(dg_ref)

        xv = x_ref[...]
        gv = g_ref[...]
        rs = lax.rsqrt(jnp.mean(xv * xv, axis=-1, keepdims=True) + NORM_EPS)
        xh = xv * rs
        e = xh * gv - t_ref[...]
        part = 0.5 * jnp.sum(jnp.mean(e * e, axis=-1, keepdims=True), axis=0, keepdims=True)
        l_ref[...] += jnp.broadcast_to(part, l_ref.shape)
        dy = e * (1.0 / D_MODEL)
        dg_ref[...] += _put_row(dg_ref.shape, 0, jnp.sum(dy * xh, axis=0, keepdims=True))
        dn = dy * gv
        dx_ref[...] = rs * (dn - xh * jnp.mean(dn * xh, axis=-1, keepdims=True))

    return pl.pallas_call(
        body, name="loss_head", grid=(s // TILE_IN,),
        in_specs=[pl.BlockSpec((TILE_IN, D_MODEL), lambda i: (i, 0)),
                  pl.BlockSpec((1, D_MODEL), lambda i: (0, 0)),
                  pl.BlockSpec((TILE_IN, D_MODEL), lambda i: (i, 0))],
        out_specs=[pl.BlockSpec((SUBLANES, 128), lambda i: (0, 0)),
                   pl.BlockSpec((TILE_IN, D_MODEL), lambda i: (i, 0)),
                   pl.BlockSpec((SUBLANES, D_MODEL), lambda i: (0, 0))],
        out_shape=[jax.ShapeDtypeStruct((SUBLANES, 128), F32), jax.ShapeDtypeStruct((s, D_MODEL), F32),
                   jax.ShapeDtypeStruct((SUBLANES, D_MODEL), F32)],
        compiler_params=_params(("arbitrary",)),
    )(x, g, tgt)


def _attn_bias(dil):
    qi = np.arange(ATTN_BLOCK)[:, None]
    ki = np.arange(2 * ATTN_BLOCK)[None, :]
    delta = qi + ATTN_BLOCK - ki
    band = (delta >= 0) & (delta <= ATTN_BLOCK)
    out = np.empty((2, N_HEADS, ATTN_BLOCK, 2 * ATTN_BLOCK), np.float32)
    for f in range(2):
        ok = band & ((ki >= ATTN_BLOCK) | (f == 0))
        for h in range(N_HEADS):
            out[f, h] = np.where(ok, -ALIBI_SLOPES[h] * dil * delta, NEG_BIG)
    return jnp.asarray(out.reshape(2, N_HEADS * ATTN_BLOCK, 2 * ATTN_BLOCK))


def _stack_heads(a, head):
    return jnp.concatenate([jnp.where(head == h, a, jnp.zeros_like(a)) for h in range(N_HEADS)], axis=0)


def _unstack_heads(a, head):
    out = a[:ATTN_BLOCK]
    for h in range(1, N_HEADS):
        out = jnp.where(head == h, a[h * ATTN_BLOCK:(h + 1) * ATTN_BLOCK], out)
    return out


def _head_column(a):
    return jnp.concatenate([a[:, h * HEAD_DIM:h * HEAD_DIM + 1] for h in range(N_HEADS)], axis=0)


def _attn_specs(n_blocks):
    rows = ATTN_QB * ATTN_BLOCK
    cur = lambda c: pl.BlockSpec((rows, GROUP_W), lambda n, c=c: (n, c))
    prev = lambda c: pl.BlockSpec((ATTN_BLOCK, GROUP_W), lambda n, c=c: (jnp.maximum(n * ATTN_QB - 1, 0), c))
    nxt = lambda c: pl.BlockSpec((ATTN_BLOCK, GROUP_W),
                                 lambda n, c=c: (jnp.minimum(n * ATTN_QB + ATTN_QB, n_blocks - 1), c))
    return cur, prev, nxt


def _keys(kp_ref, k_ref, j):
    prev = kp_ref[...] if j == 0 else k_ref[(j - 1) * ATTN_BLOCK:j * ATTN_BLOCK, :]
    return jnp.concatenate([prev, k_ref[j * ATTN_BLOCK:(j + 1) * ATTN_BLOCK, :]], axis=0)


def attn_fwd(qkv, dil):
    s = qkv.shape[0]
    n_blocks = s // ATTN_BLOCK
    bps = n_blocks // dil
    rows = ATTN_QB * ATTN_BLOCK

    def body(q_ref, kp_ref, k_ref, vp_ref, v_ref, bias_ref, o_ref, lse_ref):
        n = pl.program_id(0)
        head = _head_of_lane((ATTN_BLOCK, GROUP_W))
        for j in range(ATTN_QB):
            sl = slice(j * ATTN_BLOCK, (j + 1) * ATTN_BLOCK)
            first = (((n * ATTN_QB + j) % bps) == 0).astype(jnp.int32)
            qs = _stack_heads(q_ref[sl, :], head)
            sc = _dot_nt(qs, _keys(kp_ref, k_ref, j)) * ATTN_SCALE + bias_ref[first]
            m = jnp.max(sc, axis=-1, keepdims=True)
            pr = jnp.exp(sc - m)
            l = jnp.sum(pr, axis=-1, keepdims=True)
            oh = _dot(pr.astype(BF16), _keys(vp_ref, v_ref, j)) / l
            o_ref[sl, :] = _unstack_heads(oh, head)
            lse_ref[sl, :] = _unstack_heads(jnp.broadcast_to(m + jnp.log(l), oh.shape), head)

    cur, prev, _ = _attn_specs(n_blocks)
    bias = _attn_bias(dil)
    out = jax.ShapeDtypeStruct((s, GROUP_W), F32)
    return pl.pallas_call(
        body, name=f"attn_fwd_d{dil}", grid=(n_blocks // ATTN_QB,),
        in_specs=[cur(0), prev(1), cur(1), prev(2), cur(2), pl.BlockSpec(bias.shape, lambda n: (0, 0, 0))],
        out_specs=[cur(0), cur(0)],
        out_shape=[out, out],
        compiler_params=_params(("parallel",)),
    )(qkv, qkv, qkv, qkv, qkv, bias)


def attn_bwd(qkv, do, lse, dlt, dil):
    s = qkv.shape[0]
    n_blocks = s // ATTN_BLOCK
    bps = n_blocks // dil
    rows = ATTN_QB * ATTN_BLOCK

    def body(q_ref, qn_ref, kp_ref, k_ref, vp_ref, v_ref, do_ref, don_ref, lse_ref, lsen_ref, dl_ref, dln_ref,
             bias_ref, out_ref, dk_acc, dv_acc):
        n = pl.program_id(0)
        head = _head_of_lane((ATTN_BLOCK, GROUP_W))
        dk_acc[...] = jnp.zeros_like(dk_acc)
        dv_acc[...] = jnp.zeros_like(dv_acc)

        def pair(qj, doj, lsej, dlj, kk, vv, bias, keep):
            qs = _stack_heads(qj, head)
            dos = _stack_heads(doj, head)
            sc = _dot_nt(qs, kk) * ATTN_SCALE + bias
            if keep is None:
                pr = jnp.exp(sc - _head_column(lsej))
            else:
                pr = jnp.exp(jnp.minimum(sc - _head_column(lsej), 0.0)) * keep
            dp = _dot_nt(dos, vv)
            ds = (pr * (dp - _head_column(dlj)) * ATTN_SCALE).astype(BF16)
            return ds, _dot_tn(ds, qs), _dot_tn(pr.astype(BF16), dos)

        for j in range(ATTN_QB):
            sl = slice(j * ATTN_BLOCK, (j + 1) * ATTN_BLOCK)
            first = (((n * ATTN_QB + j) % bps) == 0).astype(jnp.int32)
            kk = _keys(kp_ref, k_ref, j)
            ds, dks, dvs = pair(q_ref[sl, :], do_ref[sl, :], lse_ref[sl, :], dl_ref[sl, :],
                                kk, _keys(vp_ref, v_ref, j), bias_ref[first], None)
            out_ref[sl, 0:GROUP_W] = _unstack_heads(_dot(ds, kk), head)
            acc = slice(j * ATTN_BLOCK, (j + 2) * ATTN_BLOCK)
            dk_acc[acc, :] += dks
            dv_acc[acc, :] += dvs

        nxt = n * ATTN_QB + ATTN_QB
        valid = ((nxt < n_blocks) & ((nxt % bps) != 0)).astype(F32)
        last = slice((ATTN_QB - 1) * ATTN_BLOCK, ATTN_QB * ATTN_BLOCK)
        _, dks, dvs = pair(qn_ref[...], don_ref[...], lsen_ref[...], dln_ref[...], k_ref[last, :], v_ref[last, :],
                           bias_ref[0][:, :ATTN_BLOCK], valid)
        acc = slice(ATTN_QB * ATTN_BLOCK, (ATTN_QB + 1) * ATTN_BLOCK)
        dk_acc[acc, :] += dks
        dv_acc[acc, :] += dvs
        out_ref[:, GROUP_W:2 * GROUP_W] = dk_acc[ATTN_BLOCK:, :]
        out_ref[:, 2 * GROUP_W:3 * GROUP_W] = dv_acc[ATTN_BLOCK:, :]

    cur, prev, nxt = _attn_specs(n_blocks)
    bias = _attn_bias(dil)
    return pl.pallas_call(
        body, name=f"attn_bwd_d{dil}", grid=(n_blocks // ATTN_QB,),
        in_specs=[cur(0), nxt(0), prev(1), cur(1), prev(2), cur(2), cur(0), nxt(0), cur(0), nxt(0), cur(0), nxt(0),
                  pl.BlockSpec(bias.shape, lambda n: (0, 0, 0))],
        out_specs=pl.BlockSpec((rows, 3 * GROUP_W), lambda n: (n, 0)),
        out_shape=jax.ShapeDtypeStruct((s, 3 * GROUP_W), F32),
        scratch_shapes=[pltpu.VMEM(((ATTN_QB + 1) * ATTN_BLOCK, GROUP_W), F32),
                        pltpu.VMEM(((ATTN_QB + 1) * ATTN_BLOCK, GROUP_W), F32)],
        compiler_params=_params(("parallel",)),
    )(qkv, qkv, qkv, qkv, qkv, qkv, do, do, lse, lse, dlt, dlt, bias)


def _zcol(c):
    return pl.BlockSpec((TILE_MIX, GROUP_W), lambda i, c=c: (i, c))


def _zhalo(c):
    per = TILE_MIX // SUBLANES
    return pl.BlockSpec((SUBLANES, GROUP_W), lambda i, c=c: (jnp.maximum(i * per - 1, 0), c))


def _full(shape):
    return pl.BlockSpec(shape, lambda i: tuple(0 for _ in shape))


def _lru_gates(xb, wa_ref, wx_ref, ba, bx, lam):
    xbb = xb.astype(BF16)
    r = jax.nn.sigmoid(_dot(xbb, wa_ref[...]) + ba)
    ig = jax.nn.sigmoid(_dot(xbb, wx_ref[...]) + bx)
    nl = -lam
    sp = jnp.maximum(nl, 0.0) + jnp.log1p(jnp.exp(-jnp.abs(nl)))
    log_a = (-RG_C * r) * sp
    a = jnp.exp(log_a)
    mult = jnp.sqrt(_neg_expm1(2.0 * log_a))
    return r, ig, sp, a, mult


def _gmlp_spatial(ws_ref, vvb, head):
    outs = []
    for j in range(vvb.shape[0] // GMLP_CHUNK):
        blk = vvb[j * GMLP_CHUNK:(j + 1) * GMLP_CHUNK, :]
        acc = jnp.zeros((GMLP_CHUNK, GROUP_W), F32)
        for h in range(N_HEADS):
            acc = jnp.where(head[:GMLP_CHUNK] == h, _dot(ws_ref[h], blk), acc)
        outs.append(acc)
    return jnp.concatenate(outs, axis=0)


def mix_fwd(z, attn, wts):
    s = z.shape[0]
    d4, d16 = PATTERN_DILS[1], PATTERN_DILS[2]

    def body(ax_ref, ab_ref, ac_ref, ag_ref, rx_ref, rg_ref, cu_ref, cv_ref, cg_ref, dg_ref,
             axh_ref, ach_ref, rxh_ref, o1_ref, l1_ref, o4_ref, l4_ref, o16_ref, l16_ref,
             caw_ref, crw_ref, crb_ref, wa_ref, wx_ref, ba_ref, bx_ref, lam_ref, gng_ref, ws_ref, bs_ref,
             y_ref, hl_ref, o_ref, lse_ref, lse4_ref, lse16_ref, carry, *stage):
        st_a, st_b, st_c, st_d, st_e = (stage[2 * k:2 * k + 2] for k in range(5))
        i = pl.program_id(0)

        @pl.when(i == 0)
        def _():
            carry[...] = jnp.zeros_like(carry)

        nz = (i > 0).astype(F32)
        head = _head_of_lane((TILE_MIX, GROUP_W))

        pa = ac_ref[...] * ax_ref[...]
        pah = ach_ref[...] * axh_ref[...] * nz
        cv = caw_ref[2:3, :] * pa + caw_ref[1:2, :] * _shift_down(pa, pah, 1) + caw_ref[0:1, :] * _shift_down(pa, pah, 2)
        y_ref[:, 0:GROUP_W] = (ab_ref[...] * cv * _silu(ag_ref[...])).astype(BF16)

        rx = rx_ref[...]
        rxh = rxh_ref[...] * nz
        xb = (crw_ref[3:4, :] * rx + crw_ref[2:3, :] * _shift_down(rx, rxh, 1) + crw_ref[1:2, :] * _shift_down(rx, rxh, 2)
              + crw_ref[0:1, :] * _shift_down(rx, rxh, 3) + crb_ref[...])
        _, ig, _, a, mult = _lru_gates(xb, wa_ref, wx_ref, ba_ref[...], bx_ref[...], lam_ref[...])
        ca, cb = _scan_causal(a, mult * (ig * xb))
        hl = cb + ca * carry[SUBLANES - 1:SUBLANES, :]
        hl_ref[...] = hl
        carry[...] = hl[TILE_MIX - SUBLANES:, :]
        y_ref[:, GROUP_W:2 * GROUP_W] = (hl * _silu(rg_ref[...])).astype(BF16)

        u = _gelu(cu_ref[...])
        gv = _gelu(cv_ref[...])
        rs = lax.rsqrt(jnp.mean(gv * gv, axis=-1, keepdims=True) + NORM_EPS)
        vvb = (gv * rs * gng_ref[...]).astype(BF16)
        sp = _gmlp_spatial(ws_ref, vvb, head) + jnp.concatenate([bs_ref[...]] * (TILE_MIX // GMLP_CHUNK), axis=0)
        y_ref[:, 2 * GROUP_W:3 * GROUP_W] = (u * sp * _silu(cg_ref[...])).astype(BF16)

        ops = (o1_ref[...], _interleave_load(o4_ref, d4, st_a), _interleave_load(o16_ref, d16, st_b))
        lps = (l1_ref[...], _interleave_load(l4_ref, d4, st_c), _interleave_load(l16_ref, d16, st_d))
        m = jnp.maximum(jnp.maximum(lps[0], lps[1]), lps[2])
        zsum = jnp.zeros_like(m)
        o = jnp.zeros_like(m)
        for op, lp in zip(ops, lps):
            w = jnp.exp(lp - m)
            zsum = zsum + w
            o = o + w * op
        o = o / zsum
        lse = m + jnp.log(zsum)
        o_ref[...] = o
        lse_ref[...] = lse
        _deinterleave_store(lse, st_e, ((d4, lse4_ref), (d16, lse16_ref)))
        y_ref[:, 3 * GROUP_W:4 * GROUP_W] = (o * _silu(dg_ref[...])).astype(BF16)

    row = pl.BlockSpec((TILE_MIX, GROUP_W), lambda i: (i, 0))
    r4 = _residue_block(d4, TILE_MIX, GROUP_W)
    r16 = _residue_block(d16, TILE_MIX, GROUP_W)
    names = ("caw", "crw", "crb", "wa", "wx", "ba", "bx", "lam", "gng", "ws", "bs")
    in_specs = ([_zcol(c) for c in (C_AX, C_AB, C_AC, C_AG, C_RX, C_RG, C_CU, C_CV, C_CG, C_DG)]
                + [_zhalo(C_AX), _zhalo(C_AC), _zhalo(C_RX), row, row, r4, r4, r16, r16]
                + [_full(wts[k].shape) for k in names])
    return pl.pallas_call(
        body, name="mix_fwd", grid=(s // TILE_MIX,),
        in_specs=in_specs,
        out_specs=[pl.BlockSpec((TILE_MIX, D_MIX), lambda i: (i, 0)), row, row, row, r4, r16],
        out_shape=([jax.ShapeDtypeStruct((s, D_MIX), BF16)] + [jax.ShapeDtypeStruct((s, GROUP_W), F32)] * 3
                   + [_by_residue(s, d4, GROUP_W, F32), _by_residue(s, d16, GROUP_W, F32)]),
        scratch_shapes=[pltpu.VMEM((SUBLANES, GROUP_W), F32)] + _stage_scratch(TILE_MIX, GROUP_W, 5),
        compiler_params=_params(("arbitrary",)),
    )(*([z] * 13), *[a for pair in attn for a in pair], *[wts[k] for k in names])


def mix_bwd(dy, z, hl, dqkv, ddg, wts):
    s = z.shape[0]
    d4, d16 = PATTERN_DILS[1], PATTERN_DILS[2]
    n_tiles = s // TILE_MIX

    def body(dya_ref, dyb_ref, dyc_ref, ax_ref, ab_ref, ac_ref, ag_ref, rx_ref, rg_ref, cu_ref, cv_ref, cg_ref,
             axh_ref, ach_ref, rxh_ref, hl_ref, hlh_ref, dqkv1_ref, dqkv4_ref, dqkv16_ref, ddg_ref,
             caw_ref, crw_ref, crb_ref, wa_ref, wx_ref, ba_ref, bx_ref, lam_ref, gng_ref, ws_ref, wst_ref, bs_ref,
             dz_ref, ga_ref, gr_ref, gn_ref, gwa_ref, gwx_ref, gws_ref, gbs_ref,
             c_dcv, c_g, c_a, c_dxb, *stage):
        st_a, st_b = stage[:len(stage) // 2], stage[len(stage) // 2:]
        step = pl.program_id(0)
        i = n_tiles - 1 - step

        @pl.when(step == 0)
        def _():
            for r in (c_dcv, c_g, c_a, c_dxb, ga_ref, gr_ref, gn_ref, gwa_ref, gwx_ref, gws_ref, gbs_ref):
                r[...] = jnp.zeros_like(r)

        nz = (i > 0).astype(F32)
        head = _head_of_lane((TILE_MIX, GROUP_W))
        shp8 = (SUBLANES, GROUP_W)
        colsum = lambda v: jnp.sum(v, axis=0, keepdims=True)

        ax, ab, ac, ag = ax_ref[...], ab_ref[...], ac_ref[...], ag_ref[...]
        dya = dya_ref[...]
        pa = ac * ax
        pah = ach_ref[...] * axh_ref[...] * nz
        pa1 = _shift_down(pa, pah, 1)
        pa2 = _shift_down(pa, pah, 2)
        cv = caw_ref[2:3, :] * pa + caw_ref[1:2, :] * pa1 + caw_ref[0:1, :] * pa2
        sg = _silu(ag)
        dz_ref[:, C_AB * GROUP_W:(C_AB + 1) * GROUP_W] = (dya * cv * sg).astype(BF16)
        dz_ref[:, C_AG * GROUP_W:(C_AG + 1) * GROUP_W] = (dya * ab * cv * _dsilu(ag)).astype(BF16)
        dcv = dya * ab * sg
        nxt = c_dcv[...]
        dpa = caw_ref[2:3, :] * dcv + caw_ref[1:2, :] * _shift_up(dcv, nxt, 1) + caw_ref[0:1, :] * _shift_up(dcv, nxt, 2)
        c_dcv[...] = dcv[:SUBLANES, :]
        dz_ref[:, C_AC * GROUP_W:(C_AC + 1) * GROUP_W] = (dpa * ax).astype(BF16)
        dz_ref[:, C_AX * GROUP_W:(C_AX + 1) * GROUP_W] = (dpa * ac).astype(BF16)
        ga_ref[...] += (_put_row(shp8, 2, colsum(dcv * pa)) + _put_row(shp8, 1, colsum(dcv * pa1))
                        + _put_row(shp8, 0, colsum(dcv * pa2)))

        rx, rg = rx_ref[...], rg_ref[...]
        dyb = dyb_ref[...]
        rxh = rxh_ref[...] * nz
        rx1, rx2, rx3 = _shift_down(rx, rxh, 1), _shift_down(rx, rxh, 2), _shift_down(rx, rxh, 3)
        xb = crw_ref[3:4, :] * rx + crw_ref[2:3, :] * rx1 + crw_ref[1:2, :] * rx2 + crw_ref[0:1, :] * rx3 + crb_ref[...]
        lam = lam_ref[...]
        r, ig, sp, a, mult = _lru_gates(xb, wa_ref, wx_ref, ba_ref[...], bx_ref[...], lam)
        hl = hl_ref[...]
        hprev = _shift_down(hl, hlh_ref[...] * nz, 1)
        dz_ref[:, C_RG * GROUP_W:(C_RG + 1) * GROUP_W] = (dyb * hl * _dsilu(rg)).astype(BF16)
        dh = dyb * _silu(rg)
        a_next = _shift_up(a, c_a[...], 1)
        ca, cb = _scan_anticausal(a_next, dh)
        g = cb + ca * c_g[0:1, :]
        c_g[...] = g[:SUBLANES, :]
        c_a[...] = a[:SUBLANES, :]
        u = ig * xb
        da = g * hprev
        dmult = g * u
        du = g * mult
        dlog_a = da * a - dmult * (a * a) / mult
        dr = dlog_a * (-RG_C * sp)
        dga = dr * r * (1.0 - r)
        dgx = (du * xb) * ig * (1.0 - ig)
        dgab, dgxb = dga.astype(BF16), dgx.astype(BF16)
        dxb = du * ig + _dot_nt(dgab, wa_ref[...]) + _dot_nt(dgxb, wx_ref[...])
        xbb = xb.astype(BF16)
        gwa_ref[...] += _dot_tn(xbb, dgab)
        gwx_ref[...] += _dot_tn(xbb, dgxb)
        nxt = c_dxb[...]
        drx = (crw_ref[3:4, :] * dxb + crw_ref[2:3, :] * _shift_up(dxb, nxt, 1) + crw_ref[1:2, :] * _shift_up(dxb, nxt, 2)
               + crw_ref[0:1, :] * _shift_up(dxb, nxt, 3))
        c_dxb[...] = dxb[:SUBLANES, :]
        dz_ref[:, C_RX * GROUP_W:(C_RX + 1) * GROUP_W] = drx.astype(BF16)
        dlam = colsum(dlog_a * (-RG_C * r)) * (-jax.nn.sigmoid(-lam))
        gr_ref[...] += (_put_row(shp8, 3, colsum(dxb * rx)) + _put_row(shp8, 2, colsum(dxb * rx1))
                        + _put_row(shp8, 1, colsum(dxb * rx2)) + _put_row(shp8, 0, colsum(dxb * rx3))
                        + _put_row(shp8, 4, colsum(dxb)) + _put_row(shp8, 5, colsum(dga))
                        + _put_row(shp8, 6, colsum(dgx)) + _put_row(shp8, 7, dlam))

        cu, cvv, cg = cu_ref[...], cv_ref[...], cg_ref[...]
        dyc = dyc_ref[...]
        u_c, du_c = _gelu_and_grad(cu)
        gv, dgv_c = _gelu_and_grad(cvv)
        rs = lax.rsqrt(jnp.mean(gv * gv, axis=-1, keepdims=True) + NORM_EPS)
        vh = gv * rs
        gng = gng_ref[...]
        vvb = (vh * gng).astype(BF16)
        spat = _gmlp_spatial(ws_ref, vvb, head) + jnp.concatenate([bs_ref[...]] * (TILE_MIX // GMLP_CHUNK), axis=0)
        sgc = _silu(cg)
        dz_ref[:, C_CU * GROUP_W:(C_CU + 1) * GROUP_W] = (dyc * spat * sgc * du_c).astype(BF16)
        dz_ref[:, C_CG * GROUP_W:(C_CG + 1) * GROUP_W] = (dyc * u_c * spat * _dsilu(cg)).astype(BF16)
        dsp = dyc * u_c * sgc
        dspb = dsp.astype(BF16)
        tril = (lax.broadcasted_iota(jnp.int32, (GMLP_CHUNK, GMLP_CHUNK), 0)
                >= lax.broadcasted_iota(jnp.int32, (GMLP_CHUNK, GMLP_CHUNK), 1))
        head_c = head[:GMLP_CHUNK]
        dvv_parts = []
        gbs = jnp.zeros((GMLP_CHUNK, GROUP_W), F32)
        for j in range(TILE_MIX // GMLP_CHUNK):
            sl = slice(j * GMLP_CHUNK, (j + 1) * GMLP_CHUNK)
            dblk = dspb[sl, :]
            vblk = vvb[sl, :]
            gbs = gbs + dsp[sl, :]
            acc = jnp.zeros((GMLP_CHUNK, GROUP_W), F32)
            for h in range(N_HEADS):
                acc = jnp.where(head_c == h, _dot(wst_ref[h], dblk), acc)
                dm = jnp.where(head_c == h, dblk, jnp.zeros_like(dblk))
                gws_ref[h] += jnp.where(tril, _dot_nt(dm, vblk), 0.0)
            dvv_parts.append(acc)
        gbs_ref[...] += gbs
        dvv = jnp.concatenate(dvv_parts, axis=0)
        gn_ref[...] += _put_row(shp8, 0, colsum(dvv * vh))
        dvh = dvv * gng
        dgv = rs * (dvh - vh * jnp.mean(dvh * vh, axis=-1, keepdims=True))
        dz_ref[:, C_CV * GROUP_W:(C_CV + 1) * GROUP_W] = (dgv * dgv_c).astype(BF16)

        dsum = dqkv1_ref[...] + _interleave_load(dqkv4_ref, d4, st_a) + _interleave_load(dqkv16_ref, d16, st_b)
        dz_ref[:, C_DQ * GROUP_W:(C_DV + 1) * GROUP_W] = dsum.astype(BF16)
        dz_ref[:, C_DG * GROUP_W:(C_DG + 1) * GROUP_W] = ddg_ref[...].astype(BF16)

    per = TILE_MIX // SUBLANES
    qkv_w = 3 * GROUP_W
    rev = lambda c: pl.BlockSpec((TILE_MIX, GROUP_W), lambda t, c=c: (n_tiles - 1 - t, c))
    revh = lambda c: pl.BlockSpec((SUBLANES, GROUP_W),
                                  lambda t, c=c: (jnp.maximum((n_tiles - 1 - t) * per - 1, 0), c))
    revr = lambda dil: pl.BlockSpec((dil, TILE_MIX // dil, qkv_w), lambda t: (0, n_tiles - 1 - t, 0))
    names = ("caw", "crw", "crb", "wa", "wx", "ba", "bx", "lam", "gng", "ws", "wst", "bs")
    in_specs = ([rev(0), rev(1), rev(2)]
                + [rev(c) for c in (C_AX, C_AB, C_AC, C_AG, C_RX, C_RG, C_CU, C_CV, C_CG)]
                + [revh(C_AX), revh(C_AC), revh(C_RX), rev(0), revh(0),
                   pl.BlockSpec((TILE_MIX, qkv_w), lambda t: (n_tiles - 1 - t, 0)), revr(d4), revr(d16), rev(0)]
                + [_full(wts[k].shape) for k in names])
    small = jax.ShapeDtypeStruct((SUBLANES, GROUP_W), F32)
    sq = jax.ShapeDtypeStruct((GROUP_W, GROUP_W), F32)
    out_shape = [jax.ShapeDtypeStruct((s, D_IN), BF16), small, small, small, sq, sq,
                 jax.ShapeDtypeStruct((N_HEADS, GMLP_CHUNK, GMLP_CHUNK), F32),
                 jax.ShapeDtypeStruct((GMLP_CHUNK, GROUP_W), F32)]
    out_specs = ([pl.BlockSpec((TILE_MIX, D_IN), lambda t: (n_tiles - 1 - t, 0))]
                 + [_full(o.shape) for o in out_shape[1:]])
    return pl.pallas_call(
        body, name="mix_bwd", grid=(n_tiles,),
        in_specs=in_specs, out_specs=out_specs, out_shape=out_shape,
        scratch_shapes=[pltpu.VMEM((SUBLANES, GROUP_W), F32)] * 4 + _stage_scratch(TILE_MIX, qkv_w, 2),
        compiler_params=_params(("arbitrary",)),
    )(dy, dy, dy, *([z] * 12), hl, hl, *dqkv, ddg, *[wts[k] for k in names])


def _block_diag(w):
    eye = jnp.eye(N_HEADS, dtype=w.dtype)
    return (w[:, :, None, :] * eye[:, None, :, None]).reshape(GROUP_W, GROUP_W)


def _diag_blocks(g):
    g4 = g.reshape(N_HEADS, HEAD_DIM, N_HEADS, HEAD_DIM)
    return jnp.stack([g4[h, :, h, :] for h in range(N_HEADS)])


def _layer_weights(p, l):
    tril = jnp.tril(jnp.ones((GMLP_CHUNK, GMLP_CHUNK), dtype=bool))
    ws = jnp.where(tril[None], p["gmlp_ws"][l], 0.0).astype(BF16)
    row = lambda a: a[l][None, :]
    return dict(
        caw=p["conv_a_w"][l], crw=p["conv_r_w"][l], crb=row(p["conv_r_b"]),
        wa=_block_diag(p["lru_wa"][l]).astype(BF16), wx=_block_diag(p["lru_wx"][l]).astype(BF16),
        ba=row(p["lru_ba"]), bx=row(p["lru_bx"]), lam=row(p["lru_lambda"]), gng=row(p["gmlp_norm_g"]),
        ws=ws, wst=jnp.transpose(ws, (0, 2, 1)),
        bs=jnp.repeat(jnp.transpose(p["gmlp_bs"][l]), HEAD_DIM, axis=1))


def _flat(a):
    return a.reshape(a.shape[0] * a.shape[1], a.shape[2])


def _split(a, dil):
    return a.reshape(dil, a.shape[0] // dil, a.shape[1])


def local_step(x, tgt, final_g, depth, layer_weights, projections_done):
    saved = []
    for l in range(depth):
        gain, w_in_l, w_out_l, wts = layer_weights(l, x)
        z, h, *qkvs = in_fwd(x, gain, w_in_l)
        qkvs = [_flat(q) if q.ndim == 3 else q for q in qkvs]
        attn = []
        for q, d in zip(qkvs, PATTERN_DILS):
            o_p, lse_p = attn_fwd(q, d)
            attn.append((o_p, lse_p) if d == 1 else (_split(o_p, d), _split(lse_p, d)))
        y, hl, o, lse, lse4, lse16 = mix_fwd(z, attn, wts)
        saved.append(dict(x=x, z=z, h=h, y=y, hl=hl, o=o, qkvs=qkvs, lses=(lse, _flat(lse4), _flat(lse16)), wts=wts,
                          gain=gain, w_in=w_in_l, w_out=w_out_l))
        x = out_fwd(y, w_out_l, x)

    loss, dx, dfg = loss_head(x, final_g[None, :], tgt)
    grads = {k: [None] * depth for k in
             ("norm_g", "conv_a_w", "conv_r_w", "conv_r_b", "lru_wa", "lru_ba", "lru_wx", "lru_bx",
              "lru_lambda", "gmlp_norm_g", "gmlp_ws", "gmlp_bs")}
    zero = None
    for l in reversed(range(depth)):
        sv = saved[l]
        dy, ddg, do1, do4, do16, dl1, dl4, dl16 = out_bwd(dx, sv["w_out"], sv["z"], sv["o"])
        g_w_out = matmul_tn(sv["y"], dx, 1)
        dqkv = []
        for q, do, lse, dl, d in zip(sv["qkvs"], (do1, _flat(do4), _flat(do16)), sv["lses"],
                                     (dl1, _flat(dl4), _flat(dl16)), PATTERN_DILS):
            g = attn_bwd(q, do, lse, dl, d)
            dqkv.append(g if d == 1 else _split(g, d))
        dz, ga, gr, gn, gwa, gwx, gws, gbs = mix_bwd(dy, sv["z"], sv["hl"], dqkv, ddg, sv["wts"])
        gain = sv["gain"] if zero is None else sv["gain"] + zero
        dx, dgn = in_bwd(dz, sv["w_in"], sv["x"], gain, dx)
        zero = projections_done(l, matmul_tn(sv["h"], dz, 2), g_w_out)
        grads["norm_g"][l] = dgn[0]
        grads["conv_a_w"][l] = ga[:3]
        grads["conv_r_w"][l] = gr[:4]
        grads["conv_r_b"][l] = gr[4]
        grads["lru_ba"][l] = gr[5]
        grads["lru_bx"][l] = gr[6]
        grads["lru_lambda"][l] = gr[7]
        grads["gmlp_norm_g"][l] = gn[0]
        grads["lru_wa"][l] = _diag_blocks(gwa)
        grads["lru_wx"][l] = _diag_blocks(gwx)
        grads["gmlp_ws"][l] = gws
        grads["gmlp_bs"][l] = jnp.transpose(gbs.reshape(GMLP_CHUNK, N_HEADS, HEAD_DIM).sum(-1))
    grads = {k: jnp.stack(v) for k, v in grads.items()}
    grads["final_g"] = dfg[0]
    return loss, dx, grads


MESH = pl.DeviceIdType.MESH
N_CHIPS = 4
N_DEV = 8
ANY = pl.BlockSpec(memory_space=pl.ANY)


def _place():
    x, y, c = lax.axis_index("x"), lax.axis_index("y"), lax.axis_index("c")
    chips = [(1 - x, y), (x, 1 - y), (1 - x, 1 - y)]
    return x, y, c, chips


def _remote(src, dst, ssem, rsem, to):
    return pltpu.make_async_remote_copy(src_ref=src, dst_ref=dst, send_sem=ssem, recv_sem=rsem,
                                        device_id=to, device_id_type=MESH)


HBM = pl.BlockSpec(memory_space=pltpu.HBM)
SEM = pl.BlockSpec(memory_space=pltpu.SEMAPHORE)
DATAFLOW = pltpu.SideEffectType.DATAFLOW_SIDE_EFFECTING
GATHER, SCATTER = "gather", "scatter"


def _chip_copies(mode, src_refs, land_refs, ssem, rsem):
    x, y, c, chips = _place()
    me = 2 * x + y
    n = len(src_refs)
    copies = []
    for k, (cx, cy) in enumerate(chips):
        for a in range(n):
            if mode == GATHER:
                src, dst = src_refs[a], land_refs[a].at[me]
            else:
                src, dst = src_refs[a].at[2 * cx + cy], land_refs[a].at[k]
            copies.append(_remote(src, dst, ssem.at[n * k + a], rsem.at[n * k + a], (cx, cy, c)))
    return copies


def exchange_start(mode, srcs, after, name):
    n = len(srcs)
    if mode == GATHER:
        lands = [lax.empty((N_CHIPS,) + s.shape, s.dtype) for s in srcs]
    else:
        lands = [lax.empty((N_CHIPS - 1,) + s.shape[1:], s.dtype) for s in srcs]
    extra = [] if after is None else [after]

    def body(*refs):
        src_refs, land_refs = refs[:n], refs[n:2 * n]
        ssem, rsem = refs[2 * n + len(extra)], refs[2 * n + len(extra) + 1]
        token = refs[-1]
        for cp in _chip_copies(mode, src_refs, land_refs, ssem, rsem):
            cp.start()
        token[...] = jnp.zeros_like(token)

    arrays = list(srcs) + lands
    return pl.pallas_call(
        body, name=name,
        out_shape=(pltpu.SemaphoreType.DMA((3 * n,)), pltpu.SemaphoreType.DMA((3 * n,)),
                   *[pltpu.HBM(a.shape, a.dtype) for a in arrays], jax.ShapeDtypeStruct((SUBLANES, LANES), F32)),
        in_specs=[HBM] * (2 * n) + [ANY] * len(extra),
        out_specs=(SEM, SEM, *[HBM] * (2 * n), pl.BlockSpec(memory_space=pltpu.VMEM)),
        input_output_aliases={i: 2 + i for i in range(2 * n)},
        compiler_params=pltpu.CompilerParams(has_side_effects=DATAFLOW),
    )(*[pltpu.with_memory_space_constraint(a, pltpu.HBM) for a in arrays], *extra)


def exchange_wait(mode, started, after, name):
    ssem, rsem, *thru, _ = started
    n = len(thru) // 2

    def body(*refs):
        src_refs, land_refs = refs[:n], refs[n:2 * n]
        ssem_ref, rsem_ref = refs[2 * n], refs[2 * n + 1]
        for cp in _chip_copies(mode, src_refs, land_refs, ssem_ref, rsem_ref):
            cp.wait_send()
            cp.wait_recv()

    outs = pl.pallas_call(
        body, name=name,
        out_shape=[pltpu.HBM(a.shape, a.dtype) for a in thru],
        in_specs=[HBM] * (2 * n) + [SEM, SEM, ANY],
        out_specs=[HBM] * (2 * n),
        input_output_aliases={i: i for i in range(2 * n)},
        compiler_params=pltpu.CompilerParams(has_side_effects=DATAFLOW),
    )(*thru, ssem, rsem, after)
    return outs[n:]


def sibling_exchange(p1, p2):
    def body(p1_ref, p2_ref, q1_ref, q2_ref, ssem, rsem):
        x, y, c, _ = _place()
        copies = [_remote(p_ref, q_ref, ssem.at[a], rsem.at[a], (x, y, 1 - c))
                  for a, (p_ref, q_ref) in enumerate(((p1_ref, q1_ref), (p2_ref, q2_ref)))]
        for cp in copies:
            cp.start()
        for cp in copies:
            cp.wait()

    return pl.pallas_call(
        body, name="sibling_exchange",
        in_specs=[ANY, ANY], out_specs=[ANY, ANY],
        out_shape=[jax.ShapeDtypeStruct(p.shape, p.dtype) for p in (p1, p2)],
        scratch_shapes=[pltpu.SemaphoreType.DMA((2,)), pltpu.SemaphoreType.DMA((2,))],
    )(p1, p2)


def gather_all(v):
    m_per, n = v.shape

    def body(x_ref, out_ref, send_sems, recv_sems, local_sem):
        x, y, c, chips = _place()
        me, sibling = (x, y, c), (x, y, 1 - c)

        def rows(px, py, pc):
            return out_ref.at[pl.ds((4 * px + 2 * py + pc) * m_per, m_per), :]

        def copy(k, block, to, src=None):
            return _remote(rows(*block) if src is None else src, rows(*block), send_sems.at[k], recv_sems.at[k], to)

        mine = pltpu.make_async_copy(x_ref, rows(*me), local_sem)
        mine.start()
        first = [copy(0, me, sibling, src=x_ref)]
        first += [copy(1 + j, me, (*chip, c), src=x_ref) for j, chip in enumerate(chips)]
        for cp in first:
            cp.start()
        passed = [copy(4 + j, (*chip, c), sibling) for j, chip in enumerate(chips)]
        for j, chip in enumerate(chips):
            copy(1 + j, (*chip, c), me).wait_recv()
            passed[j].start()
        copy(0, sibling, me).wait_recv()
        for j, chip in enumerate(chips):
            copy(4 + j, (*chip, 1 - c), me).wait_recv()
        for cp in first + passed:
            cp.wait_send()
        mine.wait()

    return pl.pallas_call(
        body, name="gather_all",
        out_shape=jax.ShapeDtypeStruct((N_DEV * m_per, n), v.dtype),
        in_specs=[pl.BlockSpec(memory_space=pltpu.VMEM)],
        out_specs=pl.BlockSpec(memory_space=pltpu.VMEM),
        scratch_shapes=[pltpu.SemaphoreType.DMA((7,)), pltpu.SemaphoreType.DMA((7,)), pltpu.SemaphoreType.DMA],
        compiler_params=pltpu.CompilerParams(vmem_limit_bytes=VMEM_LIMIT),
    )(v)


TILE_ROWS = 256


def sum_partials(own, parts):
    r, c = own.shape
    k = parts.shape[0]

    def body(o_ref, p_ref, out_ref):
        acc = o_ref[...]
        for i in range(k):
            acc = acc + p_ref[i].astype(F32)
        out_ref[...] = acc

    return pl.pallas_call(
        body, name="sum_partials", grid=(r // TILE_ROWS,),
        in_specs=[pl.BlockSpec((TILE_ROWS, c), lambda i: (i, 0)), pl.BlockSpec((k, TILE_ROWS, c), lambda i: (0, i, 0))],
        out_specs=pl.BlockSpec((TILE_ROWS, c), lambda i: (i, 0)),
        out_shape=jax.ShapeDtypeStruct((r, c), F32),
        compiler_params=_params(("parallel",)),
    )(own, parts)


def adamw(w, ga, gb, m, v):
    r, c = w.shape
    tile = min(TILE_ROWS, r)

    def body(w_ref, ga_ref, gb_ref, m_ref, v_ref, g_ref, d_ref, m2_ref, v2_ref):
        g = ga_ref[...] + gb_ref[...]
        g_ref[...] = g
        m2 = ADAM_B1 * m_ref[...] + (1.0 - ADAM_B1) * g
        v2 = ADAM_B2 * v_ref[...] + (1.0 - ADAM_B2) * (g * g)
        m2_ref[...] = m2
        v2_ref[...] = v2
        m_hat = m2 / (1.0 - ADAM_B1 ** ADAM_STEP)
        v_hat = v2 / (1.0 - ADAM_B2 ** ADAM_STEP)
        d_ref[...] = -ADAM_LR * (m_hat / (jnp.sqrt(v_hat) + ADAM_EPS) + ADAM_WD * w_ref[...])

    spec = pl.BlockSpec((tile, c), lambda i: (i, 0))
    return pl.pallas_call(
        body, name="adamw", grid=(r // tile,),
        in_specs=[spec] * 5, out_specs=[spec] * 4,
        out_shape=[jax.ShapeDtypeStruct((r, c), F32)] * 4,
        compiler_params=_params(("parallel",)),
    )(w, ga, gb, m, v)


REPLICATED = ("norm_g", "conv_r_b", "lru_wa", "lru_ba", "lru_wx", "lru_bx", "lru_lambda", "gmlp_norm_g",
              "gmlp_ws", "gmlp_bs", "final_g")
CHIP_SHARDED_SMALL = ("conv_a_w", "conv_r_w")
PACK_LANES = 128


def _pack(arrays):
    flat = jnp.concatenate([a.reshape(-1) for a in arrays])
    pad = (-flat.shape[0]) % (TILE_ROWS * PACK_LANES)
    return jnp.pad(flat, (0, pad)).reshape(-1, PACK_LANES)


def _unpack(packed, shapes):
    flat = packed.reshape(-1)
    out, off = [], 0
    for shp in shapes:
        n = math.prod(shp)
        out.append(flat[off:off + n].reshape(shp))
        off += n
    return out


def kernel(x, norm_g, w_in, conv_a_w, conv_r_w, conv_r_b, lru_wa, lru_ba, lru_wx, lru_bx, lru_lambda, gmlp_norm_g, gmlp_ws, gmlp_bs, w_out, final_g, loss_target, m_norm_g, m_w_in, m_conv_a_w, m_conv_r_w, m_conv_r_b, m_lru_wa, m_lru_ba, m_lru_wx, m_lru_bx, m_lru_lambda, m_gmlp_norm_g, m_gmlp_ws, m_gmlp_bs, m_w_out, m_final_g, v_norm_g, v_w_in, v_conv_a_w, v_conv_r_w, v_conv_r_b, v_lru_wa, v_lru_ba, v_lru_wx, v_lru_bx, v_lru_lambda, v_gmlp_norm_g, v_gmlp_ws, v_gmlp_bs, v_w_out, v_final_g):
    names = ("norm_g", "w_in", "conv_a_w", "conv_r_w", "conv_r_b", "lru_wa", "lru_ba", "lru_wx", "lru_bx",
             "lru_lambda", "gmlp_norm_g", "gmlp_ws", "gmlp_bs", "w_out", "final_g")
    w = dict(zip(names, (norm_g, w_in, conv_a_w, conv_r_w, conv_r_b, lru_wa, lru_ba, lru_wx, lru_bx, lru_lambda,
                         gmlp_norm_g, gmlp_ws, gmlp_bs, w_out, final_g)))
    m = dict(zip(names, (m_norm_g, m_w_in, m_conv_a_w, m_conv_r_w, m_conv_r_b, m_lru_wa, m_lru_ba, m_lru_wx, m_lru_bx,
                         m_lru_lambda, m_gmlp_norm_g, m_gmlp_ws, m_gmlp_bs, m_w_out, m_final_g)))
    v = dict(zip(names, (v_norm_g, v_w_in, v_conv_a_w, v_conv_r_w, v_conv_r_b, v_lru_wa, v_lru_ba, v_lru_wx, v_lru_bx,
                         v_lru_lambda, v_gmlp_norm_g, v_gmlp_ws, v_gmlp_bs, v_w_out, v_final_g)))
    depth, _, in_cols = w_in.shape
    out_rows = w_out.shape[1]
    conv_ch = conv_a_w.shape[2]
    chip = 2 * lax.axis_index("x") + lax.axis_index("y")

    taps = conv_a_w.shape[1] + conv_r_w.shape[1]
    w_in_h, w_out_h = w_in.astype(BF16), w_out.astype(BF16)
    conv_own = jnp.concatenate([conv_a_w, conv_r_w], axis=1).reshape(depth * taps, conv_ch)
    gathers, token = [], None
    for l in range(depth):
        srcs = [w_in_h[l], w_out_h[l]] + ([conv_own] if l == 0 else [])
        gathers.append(exchange_start(GATHER, srcs, token, f"gather_start_{l}"))
        token = gathers[-1][-1]
    p = dict(w)

    def with_own(land, own):
        return lax.dynamic_update_slice(land, own[None], (chip,) + (0,) * own.ndim)

    def layer_weights(l, x_l):
        lands = exchange_wait(GATHER, gathers[l], x_l, f"gather_wait_{l}")
        w_in_l = with_own(lands[0], w_in_h[l]).transpose(1, 0, 2).reshape(D_MODEL, D_IN)
        w_out_l = with_own(lands[1], w_out_h[l]).reshape(D_MIX, D_MODEL)
        gain = norm_g[l][None, :]
        if l == 0:
            conv = with_own(lands[2], conv_own).reshape(N_CHIPS, depth, taps, conv_ch)
            conv = conv.transpose(1, 2, 0, 3).reshape(depth, taps, GROUP_W)
            p["conv_a_w"] = conv[:, :conv_a_w.shape[1]]
            p["conv_r_w"] = conv[:, conv_a_w.shape[1]:]
            gain = gain + token[0, 0]
        return gain, w_in_l, w_out_l, _layer_weights(p, l)

    scatters, owns = [None] * depth, [None] * depth

    def projections_done(l, g_w_in, g_w_out):
        gi = g_w_in.reshape(D_MODEL, N_CHIPS, in_cols)
        go = g_w_out.reshape(N_CHIPS, out_rows, D_MODEL)
        owns[l] = (lax.dynamic_index_in_dim(gi, chip, axis=1, keepdims=False),
                   lax.dynamic_index_in_dim(go, chip, axis=0, keepdims=False))
        scatters[l] = exchange_start(SCATTER, [gi.transpose(1, 0, 2).astype(BF16), go.astype(BF16)], None,
                                     f"scatter_start_{l}")
        return scatters[l][-1][0, 0]

    loss8, dx, grads = local_step(x[0], loss_target[0], final_g, depth, layer_weights, projections_done)
    loss = lax.psum(loss8[0, 0], ("x", "y", "c"))

    parts = [exchange_wait(SCATTER, scatters[l], dx, f"scatter_wait_{l}") for l in range(depth)]
    r1 = jnp.concatenate([parts[l][0] for l in range(depth)], axis=1)
    r2 = jnp.concatenate([parts[l][1] for l in range(depth)], axis=1)
    own1 = jnp.concatenate([owns[l][0] for l in range(depth)], axis=0)
    own2 = jnp.concatenate([owns[l][1] for l in range(depth)], axis=0)
    p1 = sum_partials(own1, r1)
    p2 = sum_partials(own2, r2)
    q1, q2 = sibling_exchange(p1, p2)
    res = {}
    res["w_in"] = [t.reshape(w_in.shape) for t in
                   adamw(w_in.reshape(p1.shape), p1, q1, m_w_in.reshape(p1.shape), v_w_in.reshape(p1.shape))]
    res["w_out"] = [t.reshape(w_out.shape) for t in
                    adamw(w_out.reshape(p2.shape), p2, q2, m_w_out.reshape(p2.shape), v_w_out.reshape(p2.shape))]

    small = REPLICATED + CHIP_SHARDED_SMALL
    packed = _pack([grads[k] for k in small])
    rows = packed.shape[0]
    allp = gather_all(packed).reshape(N_DEV, rows, PACK_LANES)
    total = sum_partials(allp[0], allp[1:])
    gs = dict(zip(small, _unpack(total, [grads[k].shape for k in small])))
    for k in CHIP_SHARDED_SMALL:
        gs[k] = lax.dynamic_slice_in_dim(gs[k], chip * conv_ch, conv_ch, axis=2)
    wp, gp, mp, vp = (_pack([d[k] for k in small]) for d in (w, gs, m, v))
    outs = adamw(wp, gp, jnp.zeros_like(gp), mp, vp)
    shapes = [w[k].shape for k in small]
    unpacked = [_unpack(o, shapes) for o in outs]
    for i, k in enumerate(small):
        res[k] = [u[i] for u in unpacked]

    return (loss, dx[None], *[res[k][0] for k in names], *[res[k][1] for k in names],
            *[res[k][2] for k in names], *[res[k][3] for k in names])
```

```python
import functools
import math

import jax
import jax.numpy as jnp
import numpy as np
from jax import lax
from jax.experimental import pallas as pl
from jax.experimental.pallas import tpu as pltpu

F32 = jnp.float32
BF16 = jnp.bfloat16

D_MODEL = 1024
GROUP_W = 256
N_HEADS = 4
HEAD_DIM = 64
N_CHUNKS = 13
D_IN = N_CHUNKS * GROUP_W
D_MIX = 4 * GROUP_W
NORM_EPS = 1e-6
RG_C = 8.0
GMLP_CHUNK = 128
ATTN_BLOCK = 128
PATTERN_DILS = (1, 4, 16)
N_PATTERNS = len(PATTERN_DILS)
ALIBI_SLOPES = tuple(2.0 ** (-8.0 * (h + 1) / N_HEADS) for h in range(N_HEADS))
ATTN_SCALE = 1.0 / math.sqrt(HEAD_DIM)
NEG_BIG = -1e30

ADAM_LR = 0.001
ADAM_B1 = 0.9
ADAM_B2 = 0.999
ADAM_EPS = 1e-08
ADAM_WD = 0.01
ADAM_STEP = 10

C_AX, C_AB, C_AC, C_AG, C_RX, C_RG, C_CU, C_CV, C_CG, C_DQ, C_DK, C_DV, C_DG = range(13)

SUBLANES = 8
LANES = 128
VMEM_LIMIT = 56 * 1024 * 1024
TILE_IN = 512
TILE_MIX = 256
TILE_DW = 512
ATTN_QB = 4
GELU_K0 = math.sqrt(2.0 / math.pi)
GELU_K1 = 0.044715


def _params(sem):
    return pltpu.CompilerParams(dimension_semantics=sem, vmem_limit_bytes=VMEM_LIMIT)


def _silu(x):
    return x * jax.nn.sigmoid(x)


def _dsilu(x):
    s = jax.nn.sigmoid(x)
    return s * (1.0 + x * (1.0 - s))


def _gelu(x):
    return 0.5 * x * (1.0 + jnp.tanh(GELU_K0 * (x + GELU_K1 * x * x * x)))


def _gelu_and_grad(x):
    t = jnp.tanh(GELU_K0 * (x + GELU_K1 * x * x * x))
    g = 0.5 * x * (1.0 + t)
    dg = 0.5 * (1.0 + t) + 0.5 * x * (1.0 - t * t) * GELU_K0 * (1.0 + 3.0 * GELU_K1 * x * x)
    return g, dg


def _neg_expm1(x):
    series = x * (1.0 + x * (0.5 + x * (1.0 / 6.0 + x * (1.0 / 24.0 + x * (1.0 / 120.0)))))
    return -jnp.where(x > -0.05, series, jnp.exp(x) - 1.0)


def _shift_down(v, halo, k):
    r = pltpu.roll(v, k, 0)
    rh = pltpu.roll(halo, k, 0)
    row = lax.broadcasted_iota(jnp.int32, halo.shape, 0)
    top = jnp.where(row < k, rh, r[:SUBLANES])
    return jnp.concatenate([top, r[SUBLANES:]], axis=0)


def _shift_up(v, halo, k):
    t = v.shape[0]
    r = pltpu.roll(v, t - k, 0)
    rh = pltpu.roll(halo, SUBLANES - k, 0)
    row = lax.broadcasted_iota(jnp.int32, halo.shape, 0)
    bot = jnp.where(row >= SUBLANES - k, rh, r[t - SUBLANES:])
    return jnp.concatenate([r[:t - SUBLANES], bot], axis=0)


def _scan_causal(a, b):
    t = a.shape[0]
    row = lax.broadcasted_iota(jnp.int32, a.shape, 0)
    d = 1
    while d < t:
        m = row >= d
        a_s = jnp.where(m, pltpu.roll(a, d, 0), 1.0)
        b_s = jnp.where(m, pltpu.roll(b, d, 0), 0.0)
        b = a * b_s + b
        a = a * a_s
        d *= 2
    return a, b


def _scan_anticausal(a, b):
    t = a.shape[0]
    row = lax.broadcasted_iota(jnp.int32, a.shape, 0)
    d = 1
    while d < t:
        m = row < t - d
        a_s = jnp.where(m, pltpu.roll(a, t - d, 0), 1.0)
        b_s = jnp.where(m, pltpu.roll(b, t - d, 0), 0.0)
        b = a * b_s + b
        a = a * a_s
        d *= 2
    return a, b


def _head_of_lane(shape):
    return lax.broadcasted_iota(jnp.int32, shape, len(shape) - 1) // HEAD_DIM


def _put_row(acc_shape, k, row_vec):
    row = lax.broadcasted_iota(jnp.int32, acc_shape, 0)
    return jnp.where(row == k, jnp.broadcast_to(row_vec, acc_shape), 0.0)


def _dot(a, b):
    return jnp.dot(a, b, preferred_element_type=F32)


def _dot_nt(a, b):
    return lax.dot_general(a, b, (((1,), (1,)), ((), ())), preferred_element_type=F32)


def _dot_tn(a, b):
    return lax.dot_general(a, b, (((0,), (0,)), ((), ())), preferred_element_type=F32)


def _deinterleave_store(val, stage, outs):
    t, c = val.shape
    for hh in range(c // LANES):
        stage[hh][...] = val[:, hh * LANES:(hh + 1) * LANES].astype(F32)
    for dil, ref in outs:
        for r in range(dil):
            for hh in range(c // LANES):
                ref[r, :, hh * LANES:(hh + 1) * LANES] = stage[hh][pl.ds(r, t // dil, stride=dil), :].astype(ref.dtype)


def _interleave_load(ref, dil, stage):
    _, n, c = ref.shape
    for r in range(dil):
        for hh in range(c // LANES):
            stage[hh][pl.ds(r, n, stride=dil), :] = ref[r, :, hh * LANES:(hh + 1) * LANES].astype(F32)
    return jnp.concatenate([stage[hh][...] for hh in range(c // LANES)], axis=1)


def _stage_scratch(tile, cols, copies):
    return [pltpu.VMEM((tile, LANES), F32)] * (copies * (cols // LANES))


def _by_residue(s, dil, cols, dtype):
    return jax.ShapeDtypeStruct((dil, s // dil, cols), dtype)


def _residue_block(dil, tile, cols):
    return pl.BlockSpec((dil, tile // dil, cols), lambda i: (0, i, 0))


def in_fwd(x, g, w):
    s = x.shape[0]
    qkv_w = 3 * GROUP_W

    def body(x_ref, g_ref, w_ref, z_ref, h_ref, qkv1_ref, qkv4_ref, qkv16_ref, *stage):
        xv = x_ref[...]
        rs = lax.rsqrt(jnp.mean(xv * xv, axis=-1, keepdims=True) + NORM_EPS)
        h = (xv * rs * g_ref[...]).astype(BF16)
        h_ref[...] = h
        z = _dot(h, w_ref[...])
        z_ref[...] = z
        qkv = z[:, C_DQ * GROUP_W:(C_DV + 1) * GROUP_W]
        qkv1_ref[...] = qkv.astype(BF16)
        _deinterleave_store(qkv, stage, ((PATTERN_DILS[1], qkv4_ref), (PATTERN_DILS[2], qkv16_ref)))

    return pl.pallas_call(
        body, name="in_fwd", grid=(s // TILE_IN,),
        in_specs=[pl.BlockSpec((TILE_IN, D_MODEL), lambda i: (i, 0)),
                  pl.BlockSpec((1, D_MODEL), lambda i: (0, 0)),
                  pl.BlockSpec((D_MODEL, D_IN), lambda i: (0, 0))],
        out_specs=[pl.BlockSpec((TILE_IN, D_IN), lambda i: (i, 0)),
                   pl.BlockSpec((TILE_IN, D_MODEL), lambda i: (i, 0)),
                   pl.BlockSpec((TILE_IN, qkv_w), lambda i: (i, 0)),
                   _residue_block(PATTERN_DILS[1], TILE_IN, qkv_w),
                   _residue_block(PATTERN_DILS[2], TILE_IN, qkv_w)],
        out_shape=[jax.ShapeDtypeStruct((s, D_IN), F32), jax.ShapeDtypeStruct((s, D_MODEL), BF16),
                   jax.ShapeDtypeStruct((s, qkv_w), BF16),
                   _by_residue(s, PATTERN_DILS[1], qkv_w, BF16), _by_residue(s, PATTERN_DILS[2], qkv_w, BF16)],
        scratch_shapes=_stage_scratch(TILE_IN, qkv_w, 1),
        compiler_params=_params(("parallel",)),
    )(x, g, w)


def out_fwd(y, w, x):
    s = x.shape[0]

    def body(y_ref, w_ref, x_ref, o_ref):
        o_ref[...] = x_ref[...] + _dot(y_ref[...], w_ref[...])

    return pl.pallas_call(
        body, name="out_fwd", grid=(s // TILE_IN,),
        in_specs=[pl.BlockSpec((TILE_IN, D_MIX), lambda i: (i, 0)),
                  pl.BlockSpec((D_MIX, D_MODEL), lambda i: (0, 0)),
                  pl.BlockSpec((TILE_IN, D_MODEL), lambda i: (i, 0))],
        out_specs=pl.BlockSpec((TILE_IN, D_MODEL), lambda i: (i, 0)),
        out_shape=jax.ShapeDtypeStruct((s, D_MODEL), F32),
        compiler_params=_params(("parallel",)),
    )(y, w, x)


def out_bwd(dx, w, z, o):
    s = dx.shape[0]
    abc = 3 * GROUP_W

    def body(dx_ref, w_ref, dg_ref, o_ref, dy_ref, ddg_ref, do1_ref, do4_ref, do16_ref, dl1_ref, dl4_ref, dl16_ref,
             *stage):
        stage_a, stage_b = stage[:2], stage[2:]
        dy = _dot_nt(dx_ref[...].astype(BF16), w_ref[...])
        dy_ref[...] = dy[:, :abc]
        dyd = dy[:, abc:]
        head = _head_of_lane((TILE_IN, GROUP_W))
        dg = dg_ref[...]
        o = o_ref[...]
        do = dyd * _silu(dg)
        ddg_ref[...] = dyd * o * _dsilu(dg)
        prod = do * o
        dl = jnp.zeros_like(prod)
        for h in range(N_HEADS):
            sm = jnp.sum(jnp.where(head == h, prod, 0.0), axis=-1, keepdims=True)
            dl = jnp.where(head == h, sm, dl)
        do1_ref[...] = do.astype(BF16)
        dl1_ref[...] = dl
        _deinterleave_store(do, stage_a, ((PATTERN_DILS[1], do4_ref), (PATTERN_DILS[2], do16_ref)))
        _deinterleave_store(dl, stage_b, ((PATTERN_DILS[1], dl4_ref), (PATTERN_DILS[2], dl16_ref)))

    row = pl.BlockSpec((TILE_IN, GROUP_W), lambda i: (i, 0))
    r4 = _residue_block(PATTERN_DILS[1], TILE_IN, GROUP_W)
    r16 = _residue_block(PATTERN_DILS[2], TILE_IN, GROUP_W)
    return pl.pallas_call(
        body, name="out_bwd", grid=(s // TILE_IN,),
        in_specs=[pl.BlockSpec((TILE_IN, D_MODEL), lambda i: (i, 0)),
                  pl.BlockSpec((D_MIX, D_MODEL), lambda i: (0, 0)),
                  pl.BlockSpec((TILE_IN, GROUP_W), lambda i: (i, C_DG)), row],
        out_specs=[pl.BlockSpec((TILE_IN, abc), lambda i: (i, 0)), row, row, r4, r16, row, r4, r16],
        out_shape=[jax.ShapeDtypeStruct((s, abc), F32), jax.ShapeDtypeStruct((s, GROUP_W), F32),
                   jax.ShapeDtypeStruct((s, GROUP_W), BF16),
                   _by_residue(s, PATTERN_DILS[1], GROUP_W, BF16), _by_residue(s, PATTERN_DILS[2], GROUP_W, BF16),
                   jax.ShapeDtypeStruct((s, GROUP_W), F32),
                   _by_residue(s, PATTERN_DILS[1], GROUP_W, F32), _by_residue(s, PATTERN_DILS[2], GROUP_W, F32)],
        scratch_shapes=_stage_scratch(TILE_IN, GROUP_W, 2),
        compiler_params=_params(("parallel",)),
    )(dx, w, z, o)


def in_bwd(dz, w, x, g, dx_next):
    s = x.shape[0]

    def body(dz_ref, w_ref, x_ref, g_ref, dxn_ref, dx_ref, dg_ref):
        @pl.when(pl.program_id(0) == 0)
        def _():
            dg_ref[...] = jnp.zeros_like(dg_ref)

        dh = _dot_nt(dz_ref[...], w_ref[...])
        xv = x_ref[...]
        rs = lax.rsqrt(jnp.mean(xv * xv, axis=-1, keepdims=True) + NORM_EPS)
        xh = xv * rs
        dg_ref[...] += _put_row(dg_ref.shape, 0, jnp.sum(dh * xh, axis=0, keepdims=True))
        dn = dh * g_ref[...]
        dx_ref[...] = dxn_ref[...] + rs * (dn - xh * jnp.mean(dn * xh, axis=-1, keepdims=True))

    return pl.pallas_call(
        body, name="in_bwd", grid=(s // TILE_IN,),
        in_specs=[pl.BlockSpec((TILE_IN, D_IN), lambda i: (i, 0)),
                  pl.BlockSpec((D_MODEL, D_IN), lambda i: (0, 0)),
                  pl.BlockSpec((TILE_IN, D_MODEL), lambda i: (i, 0)),
                  pl.BlockSpec((1, D_MODEL), lambda i: (0, 0)),
                  pl.BlockSpec((TILE_IN, D_MODEL), lambda i: (i, 0))],
        out_specs=[pl.BlockSpec((TILE_IN, D_MODEL), lambda i: (i, 0)),
                   pl.BlockSpec((SUBLANES, D_MODEL), lambda i: (0, 0))],
        out_shape=[jax.ShapeDtypeStruct((s, D_MODEL), F32), jax.ShapeDtypeStruct((SUBLANES, D_MODEL), F32)],
        compiler_params=_params(("arbitrary",)),
    )(dz, w, x, g, dx_next)


def matmul_tn(a, b, n_split):
    s, m = a.shape
    n = b.shape[1]
    tn = n // n_split

    def body(a_ref, b_ref, o_ref):
        @pl.when(pl.program_id(1) == 0)
        def _():
            o_ref[...] = jnp.zeros_like(o_ref)

        o_ref[...] += _dot_tn(a_ref[...], b_ref[...].astype(BF16))

    return pl.pallas_call(
        body, name="matmul_tn", grid=(n_split, s // TILE_DW),
        in_specs=[pl.BlockSpec((TILE_DW, m), lambda j, k: (k, 0)),
                  pl.BlockSpec((TILE_DW, tn), lambda j, k: (k, j))],
        out_specs=pl.BlockSpec((m, tn), lambda j, k: (0, j)),
        out_shape=jax.ShapeDtypeStruct((m, n), F32),
        compiler_params=_params(("parallel", "arbitrary")),
    )(a, b)


def grad_w_in(h, dz, chip):
    s = h.shape[0]
    cols = D_IN // N_CHIPS

    def body(chip_ref, h_ref, dz_ref, staged_ref, own_ref, acc):
        k = pl.program_id(0)

        @pl.when(k == 0)
        def _():
            acc[...] = jnp.zeros_like(acc)

        acc[...] += _dot_tn(h_ref[...], dz_ref[...])

        @pl.when(k == s // TILE_DW - 1)
        def _():
            for j in range(N_CHIPS):
                part = acc[:, j * cols:(j + 1) * cols]
                staged_ref[j] = part.astype(BF16)

                @pl.when(chip_ref[0] == j)
                def _():
                    own_ref[...] = part

    return pl.pallas_call(
        body, name="grad_w_in",
        grid_spec=pltpu.PrefetchScalarGridSpec(
            num_scalar_prefetch=1, grid=(s // TILE_DW,),
            in_specs=[pl.BlockSpec((TILE_DW, D_MODEL), lambda k, c: (k, 0)),
                      pl.BlockSpec((TILE_DW, D_IN), lambda k, c: (k, 0))],
            out_specs=[pl.BlockSpec((N_CHIPS, D_MODEL, cols), lambda k, c: (0, 0, 0)),
                       pl.BlockSpec((D_MODEL, cols), lambda k, c: (0, 0))],
            scratch_shapes=[pltpu.VMEM((D_MODEL, D_IN), F32)]),
        out_shape=[jax.ShapeDtypeStruct((N_CHIPS, D_MODEL, cols), BF16), jax.ShapeDtypeStruct((D_MODEL, cols), F32)],
        compiler_params=_params(("arbitrary",)),
    )(chip, h, dz)


def loss_head(x, g, tgt):
    s = x.shape[0]

    def body(x_ref, g_ref, t_ref, l_ref, dx_ref, dg_ref):
        @pl.when(pl.program_id(0) == 0)
        def _():
            l_ref[...] = jnp.zeros_like(l_ref)
            dg_ref[...] = jnp.zeros_like(dg_ref)

        xv = x_ref[...]
        gv = g_ref[...]
        rs = lax.rsqrt(jnp.mean(xv * xv, axis=-1, keepdims=True) + NORM_EPS)
        xh = xv * rs
        e = xh * gv - t_ref[...]
        part = 0.5 * jnp.sum(jnp.mean(e * e, axis=-1, keepdims=True), axis=0, keepdims=True)
        l_ref[...] += jnp.broadcast_to(part, l_ref.shape)
        dy = e * (1.0 / D_MODEL)
        dg_ref[...] += _put_row(dg_ref.shape, 0, jnp.sum(dy * xh, axis=0, keepdims=True))
        dn = dy * gv
        dx_ref[...] = rs * (dn - xh * jnp.mean(dn * xh, axis=-1, keepdims=True))

    return pl.pallas_call(
        body, name="loss_head", grid=(s // TILE_IN,),
        in_specs=[pl.BlockSpec((TILE_IN, D_MODEL), lambda i: (i, 0)),
                  pl.BlockSpec((1, D_MODEL), lambda i: (0, 0)),
                  pl.BlockSpec((TILE_IN, D_MODEL), lambda i: (i, 0))],
        out_specs=[pl.BlockSpec((SUBLANES, 128), lambda i: (0, 0)),
                   pl.BlockSpec((TILE_IN, D_MODEL), lambda i: (i, 0)),
                   pl.BlockSpec((SUBLANES, D_MODEL), lambda i: (0, 0))],
        out_shape=[jax.ShapeDtypeStruct((SUBLANES, 128), F32), jax.ShapeDtypeStruct((s, D_MODEL), F32),
                   jax.ShapeDtypeStruct((SUBLANES, D_MODEL), F32)],
        compiler_params=_params(("arbitrary",)),
    )(x, g, tgt)


def _attn_bias(dil):
    qi = np.arange(ATTN_BLOCK)[:, None]
    ki = np.arange(2 * ATTN_BLOCK)[None, :]
    delta = qi + ATTN_BLOCK - ki
    band = (delta >= 0) & (delta <= ATTN_BLOCK)
    out = np.empty((2, N_HEADS, ATTN_BLOCK, 2 * ATTN_BLOCK), np.float32)
    for f in range(2):
        ok = band & ((ki >= ATTN_BLOCK) | (f == 0))
        for h in range(N_HEADS):
            out[f, h] = np.where(ok, -ALIBI_SLOPES[h] * dil * delta, NEG_BIG)
    return jnp.asarray(out.reshape(2, N_HEADS * ATTN_BLOCK, 2 * ATTN_BLOCK))


def _stack_heads(a, head):
    return jnp.concatenate([jnp.where(head == h, a, jnp.zeros_like(a)) for h in range(N_HEADS)], axis=0)


def _unstack_heads(a, head):
    out = a[:ATTN_BLOCK]
    for h in range(1, N_HEADS):
        out = jnp.where(head == h, a[h * ATTN_BLOCK:(h + 1) * ATTN_BLOCK], out)
    return out


def _head_column(a):
    return jnp.concatenate([a[:, h * HEAD_DIM:h * HEAD_DIM + 1] for h in range(N_HEADS)], axis=0)


def _attn_specs(n_blocks):
    rows = ATTN_QB * ATTN_BLOCK
    cur = lambda c: pl.BlockSpec((rows, GROUP_W), lambda n, c=c: (n, c))
    prev = lambda c: pl.BlockSpec((ATTN_BLOCK, GROUP_W), lambda n, c=c: (jnp.maximum(n * ATTN_QB - 1, 0), c))
    nxt = lambda c: pl.BlockSpec((ATTN_BLOCK, GROUP_W),
                                 lambda n, c=c: (jnp.minimum(n * ATTN_QB + ATTN_QB, n_blocks - 1), c))
    return cur, prev, nxt


def _keys(kp_ref, k_ref, j):
    prev = kp_ref[...] if j == 0 else k_ref[(j - 1) * ATTN_BLOCK:j * ATTN_BLOCK, :]
    return jnp.concatenate([prev, k_ref[j * ATTN_BLOCK:(j + 1) * ATTN_BLOCK, :]], axis=0)


def attn_fwd(qkv, dil):
    s = qkv.shape[0]
    n_blocks = s // ATTN_BLOCK
    bps = n_blocks // dil
    rows = ATTN_QB * ATTN_BLOCK

    def body(q_ref, kp_ref, k_ref, vp_ref, v_ref, bias_ref, o_ref, lse_ref):
        n = pl.program_id(0)
        head = _head_of_lane((ATTN_BLOCK, GROUP_W))
        for j in range(ATTN_QB):
            sl = slice(j * ATTN_BLOCK, (j + 1) * ATTN_BLOCK)
            first = (((n * ATTN_QB + j) % bps) == 0).astype(jnp.int32)
            qs = _stack_heads(q_ref[sl, :], head)
            sc = _dot_nt(qs, _keys(kp_ref, k_ref, j)) * ATTN_SCALE + bias_ref[first]
            m = jnp.max(sc, axis=-1, keepdims=True)
            pr = jnp.exp(sc - m)
            l = jnp.sum(pr, axis=-1, keepdims=True)
            oh = _dot(pr.astype(BF16), _keys(vp_ref, v_ref, j)) / l
            o_ref[sl, :] = _unstack_heads(oh, head)
            lse_ref[sl, :] = _unstack_heads(jnp.broadcast_to(m + jnp.log(l), oh.shape), head)

    cur, prev, _ = _attn_specs(n_blocks)
    bias = _attn_bias(dil)
    out = jax.ShapeDtypeStruct((s, GROUP_W), F32)
    return pl.pallas_call(
        body, name=f"attn_fwd_d{dil}", grid=(n_blocks // ATTN_QB,),
        in_specs=[cur(0), prev(1), cur(1), prev(2), cur(2), pl.BlockSpec(bias.shape, lambda n: (0, 0, 0))],
        out_specs=[cur(0), cur(0)],
        out_shape=[out, out],
        compiler_params=_params(("parallel",)),
    )(qkv, qkv, qkv, qkv, qkv, bias)


def attn_bwd(qkv, do, lse, dlt, dil):
    s = qkv.shape[0]
    n_blocks = s // ATTN_BLOCK
    bps = n_blocks // dil
    rows = ATTN_QB * ATTN_BLOCK

    def body(q_ref, qn_ref, kp_ref, k_ref, vp_ref, v_ref, do_ref, don_ref, lse_ref, lsen_ref, dl_ref, dln_ref,
             bias_ref, out_ref, dk_acc, dv_acc):
        n = pl.program_id(0)
        head = _head_of_lane((ATTN_BLOCK, GROUP_W))
        dk_acc[...] = jnp.zeros_like(dk_acc)
        dv_acc[...] = jnp.zeros_like(dv_acc)

        def pair(qj, doj, lsej, dlj, kk, vv, bias, keep):
            qs = _stack_heads(qj, head)
            dos = _stack_heads(doj, head)
            sc = _dot_nt(qs, kk) * ATTN_SCALE + bias
            if keep is None:
                pr = jnp.exp(sc - _head_column(lsej))
            else:
                pr = jnp.exp(jnp.minimum(sc - _head_column(lsej), 0.0)) * keep
            dp = _dot_nt(dos, vv)
            ds = (pr * (dp - _head_column(dlj)) * ATTN_SCALE).astype(BF16)
            return ds, _dot_tn(ds, qs), _dot_tn(pr.astype(BF16), dos)

        for j in range(ATTN_QB):
            sl = slice(j * ATTN_BLOCK, (j + 1) * ATTN_BLOCK)
            first = (((n * ATTN_QB + j) % bps) == 0).astype(jnp.int32)
            kk = _keys(kp_ref, k_ref, j)
            ds, dks, dvs = pair(q_ref[sl, :], do_ref[sl, :], lse_ref[sl, :], dl_ref[sl, :],
                                kk, _keys(vp_ref, v_ref, j), bias_ref[first], None)
            out_ref[sl, 0:GROUP_W] = _unstack_heads(_dot(ds, kk), head)
            acc = slice(j * ATTN_BLOCK, (j + 2) * ATTN_BLOCK)
            dk_acc[acc, :] += dks
            dv_acc[acc, :] += dvs

        nxt = n * ATTN_QB + ATTN_QB
        valid = ((nxt < n_blocks) & ((nxt % bps) != 0)).astype(F32)
        last = slice((ATTN_QB - 1) * ATTN_BLOCK, ATTN_QB * ATTN_BLOCK)
        _, dks, dvs = pair(qn_ref[...], don_ref[...], lsen_ref[...], dln_ref[...], k_ref[last, :], v_ref[last, :],
                           bias_ref[0][:, :ATTN_BLOCK], valid)
        acc = slice(ATTN_QB * ATTN_BLOCK, (ATTN_QB + 1) * ATTN_BLOCK)
        dk_acc[acc, :] += dks
        dv_acc[acc, :] += dvs
        out_ref[:, GROUP_W:2 * GROUP_W] = dk_acc[ATTN_BLOCK:, :]
        out_ref[:, 2 * GROUP_W:3 * GROUP_W] = dv_acc[ATTN_BLOCK:, :]

    cur, prev, nxt = _attn_specs(n_blocks)
    bias = _attn_bias(dil)
    return pl.pallas_call(
        body, name=f"attn_bwd_d{dil}", grid=(n_blocks // ATTN_QB,),
        in_specs=[cur(0), nxt(0), prev(1), cur(1), prev(2), cur(2), cur(0), nxt(0), cur(0), nxt(0), cur(0), nxt(0),
                  pl.BlockSpec(bias.shape, lambda n: (0, 0, 0))],
        out_specs=pl.BlockSpec((rows, 3 * GROUP_W), lambda n: (n, 0)),
        out_shape=jax.ShapeDtypeStruct((s, 3 * GROUP_W), F32),
        scratch_shapes=[pltpu.VMEM(((ATTN_QB + 1) * ATTN_BLOCK, GROUP_W), F32),
                        pltpu.VMEM(((ATTN_QB + 1) * ATTN_BLOCK, GROUP_W), F32)],
        compiler_params=_params(("parallel",)),
    )(qkv, qkv, qkv, qkv, qkv, qkv, do, do, lse, lse, dlt, dlt, bias)


def _zcol(c):
    return pl.BlockSpec((TILE_MIX, GROUP_W), lambda i, c=c: (i, c))


def _zhalo(c):
    per = TILE_MIX // SUBLANES
    return pl.BlockSpec((SUBLANES, GROUP_W), lambda i, c=c: (jnp.maximum(i * per - 1, 0), c))


def _full(shape):
    return pl.BlockSpec(shape, lambda i: tuple(0 for _ in shape))


def _lru_gates(xb, wa_ref, wx_ref, ba, bx, lam):
    xbb = xb.astype(BF16)
    r = jax.nn.sigmoid(_dot(xbb, wa_ref[...]) + ba)
    ig = jax.nn.sigmoid(_dot(xbb, wx_ref[...]) + bx)
    nl = -lam
    sp = jnp.maximum(nl, 0.0) + jnp.log1p(jnp.exp(-jnp.abs(nl)))
    log_a = (-RG_C * r) * sp
    a = jnp.exp(log_a)
    mult = jnp.sqrt(_neg_expm1(2.0 * log_a))
    return r, ig, sp, a, mult


def _gmlp_spatial(ws_ref, vvb, head):
    outs = []
    for j in range(vvb.shape[0] // GMLP_CHUNK):
        blk = vvb[j * GMLP_CHUNK:(j + 1) * GMLP_CHUNK, :]
        acc = jnp.zeros((GMLP_CHUNK, GROUP_W), F32)
        for h in range(N_HEADS):
            acc = jnp.where(head[:GMLP_CHUNK] == h, _dot(ws_ref[h], blk), acc)
        outs.append(acc)
    return jnp.concatenate(outs, axis=0)


def mix_fwd(z, attn, wts):
    s = z.shape[0]
    d4, d16 = PATTERN_DILS[1], PATTERN_DILS[2]

    def body(ax_ref, ab_ref, ac_ref, ag_ref, rx_ref, rg_ref, cu_ref, cv_ref, cg_ref, dg_ref,
             axh_ref, ach_ref, rxh_ref, o1_ref, l1_ref, o4_ref, l4_ref, o16_ref, l16_ref,
             caw_ref, crw_ref, crb_ref, wa_ref, wx_ref, ba_ref, bx_ref, lam_ref, gng_ref, ws_ref, bs_ref,
             y_ref, hl_ref, o_ref, lse_ref, lse4_ref, lse16_ref, carry, *stage):
        st_a, st_b, st_c, st_d, st_e = (stage[2 * k:2 * k + 2] for k in range(5))
        i = pl.program_id(0)

        @pl.when(i == 0)
        def _():
            carry[...] = jnp.zeros_like(carry)

        nz = (i > 0).astype(F32)
        head = _head_of_lane((TILE_MIX, GROUP_W))

        pa = ac_ref[...] * ax_ref[...]
        pah = ach_ref[...] * axh_ref[...] * nz
        cv = caw_ref[2:3, :] * pa + caw_ref[1:2, :] * _shift_down(pa, pah, 1) + caw_ref[0:1, :] * _shift_down(pa, pah, 2)
        y_ref[:, 0:GROUP_W] = (ab_ref[...] * cv * _silu(ag_ref[...])).astype(BF16)

        rx = rx_ref[...]
        rxh = rxh_ref[...] * nz
        xb = (crw_ref[3:4, :] * rx + crw_ref[2:3, :] * _shift_down(rx, rxh, 1) + crw_ref[1:2, :] * _shift_down(rx, rxh, 2)
              + crw_ref[0:1, :] * _shift_down(rx, rxh, 3) + crb_ref[...])
        _, ig, _, a, mult = _lru_gates(xb, wa_ref, wx_ref, ba_ref[...], bx_ref[...], lam_ref[...])
        ca, cb = _scan_causal(a, mult * (ig * xb))
        hl = cb + ca * carry[SUBLANES - 1:SUBLANES, :]
        hl_ref[...] = hl
        carry[...] = hl[TILE_MIX - SUBLANES:, :]
        y_ref[:, GROUP_W:2 * GROUP_W] = (hl * _silu(rg_ref[...])).astype(BF16)

        u = _gelu(cu_ref[...])
        gv = _gelu(cv_ref[...])
        rs = lax.rsqrt(jnp.mean(gv * gv, axis=-1, keepdims=True) + NORM_EPS)
        vvb = (gv * rs * gng_ref[...]).astype(BF16)
        sp = _gmlp_spatial(ws_ref, vvb, head) + jnp.concatenate([bs_ref[...]] * (TILE_MIX // GMLP_CHUNK), axis=0)
        y_ref[:, 2 * GROUP_W:3 * GROUP_W] = (u * sp * _silu(cg_ref[...])).astype(BF16)

        ops = (o1_ref[...], _interleave_load(o4_ref, d4, st_a), _interleave_load(o16_ref, d16, st_b))
        lps = (l1_ref[...], _interleave_load(l4_ref, d4, st_c), _interleave_load(l16_ref, d16, st_d))
        m = jnp.maximum(jnp.maximum(lps[0], lps[1]), lps[2])
        zsum = jnp.zeros_like(m)
        o = jnp.zeros_like(m)
        for op, lp in zip(ops, lps):
            w = jnp.exp(lp - m)
            zsum = zsum + w
            o = o + w * op
        o = o / zsum
        lse = m + jnp.log(zsum)
        o_ref[...] = o
        lse_ref[...] = lse
        _deinterleave_store(lse, st_e, ((d4, lse4_ref), (d16, lse16_ref)))
        y_ref[:, 3 * GROUP_W:4 * GROUP_W] = (o * _silu(dg_ref[...])).astype(BF16)

    row = pl.BlockSpec((TILE_MIX, GROUP_W), lambda i: (i, 0))
    r4 = _residue_block(d4, TILE_MIX, GROUP_W)
    r16 = _residue_block(d16, TILE_MIX, GROUP_W)
    names = ("caw", "crw", "crb", "wa", "wx", "ba", "bx", "lam", "gng", "ws", "bs")
    in_specs = ([_zcol(c) for c in (C_AX, C_AB, C_AC, C_AG, C_RX, C_RG, C_CU, C_CV, C_CG, C_DG)]
                + [_zhalo(C_AX), _zhalo(C_AC), _zhalo(C_RX), row, row, r4, r4, r16, r16]
                + [_full(wts[k].shape) for k in names])
    return pl.pallas_call(
        body, name="mix_fwd", grid=(s // TILE_MIX,),
        in_specs=in_specs,
        out_specs=[pl.BlockSpec((TILE_MIX, D_MIX), lambda i: (i, 0)), row, row, row, r4, r16],
        out_shape=([jax.ShapeDtypeStruct((s, D_MIX), BF16)] + [jax.ShapeDtypeStruct((s, GROUP_W), F32)] * 3
                   + [_by_residue(s, d4, GROUP_W, F32), _by_residue(s, d16, GROUP_W, F32)]),
        scratch_shapes=[pltpu.VMEM((SUBLANES, GROUP_W), F32)] + _stage_scratch(TILE_MIX, GROUP_W, 5),
        compiler_params=_params(("arbitrary",)),
    )(*([z] * 13), *[a for pair in attn for a in pair], *[wts[k] for k in names])


def mix_bwd(dy, z, hl, dqkv, ddg, wts):
    s = z.shape[0]
    d4, d16 = PATTERN_DILS[1], PATTERN_DILS[2]
    n_tiles = s // TILE_MIX

    def body(dya_ref, dyb_ref, dyc_ref, ax_ref, ab_ref, ac_ref, ag_ref, rx_ref, rg_ref, cu_ref, cv_ref, cg_ref,
             axh_ref, ach_ref, rxh_ref, hl_ref, hlh_ref, dqkv1_ref, dqkv4_ref, dqkv16_ref, ddg_ref,
             caw_ref, crw_ref, crb_ref, wa_ref, wx_ref, ba_ref, bx_ref, lam_ref, gng_ref, ws_ref, wst_ref, bs_ref,
             dz_ref, ga_ref, gr_ref, gn_ref, gwa_ref, gwx_ref, gws_ref, gbs_ref,
             c_dcv, c_g, c_a, c_dxb, *stage):
        st_a, st_b = stage[:len(stage) // 2], stage[len(stage) // 2:]
        step = pl.program_id(0)
        i = n_tiles - 1 - step

        @pl.when(step == 0)
        def _():
            for r in (c_dcv, c_g, c_a, c_dxb, ga_ref, gr_ref, gn_ref, gwa_ref, gwx_ref, gws_ref, gbs_ref):
                r[...] = jnp.zeros_like(r)

        nz = (i > 0).astype(F32)
        head = _head_of_lane((TILE_MIX, GROUP_W))
        shp8 = (SUBLANES, GROUP_W)
        colsum = lambda v: jnp.sum(v, axis=0, keepdims=True)

        ax, ab, ac, ag = ax_ref[...], ab_ref[...], ac_ref[...], ag_ref[...]
        dya = dya_ref[...]
        pa = ac * ax
        pah = ach_ref[...] * axh_ref[...] * nz
        pa1 = _shift_down(pa, pah, 1)
        pa2 = _shift_down(pa, pah, 2)
        cv = caw_ref[2:3, :] * pa + caw_ref[1:2, :] * pa1 + caw_ref[0:1, :] * pa2
        sg = _silu(ag)
        dz_ref[:, C_AB * GROUP_W:(C_AB + 1) * GROUP_W] = (dya * cv * sg).astype(BF16)
        dz_ref[:, C_AG * GROUP_W:(C_AG + 1) * GROUP_W] = (dya * ab * cv * _dsilu(ag)).astype(BF16)
        dcv = dya * ab * sg
        nxt = c_dcv[...]
        dpa = caw_ref[2:3, :] * dcv + caw_ref[1:2, :] * _shift_up(dcv, nxt, 1) + caw_ref[0:1, :] * _shift_up(dcv, nxt, 2)
        c_dcv[...] = dcv[:SUBLANES, :]
        dz_ref[:, C_AC * GROUP_W:(C_AC + 1) * GROUP_W] = (dpa * ax).astype(BF16)
        dz_ref[:, C_AX * GROUP_W:(C_AX + 1) * GROUP_W] = (dpa * ac).astype(BF16)
        ga_ref[...] += (_put_row(shp8, 2, colsum(dcv * pa)) + _put_row(shp8, 1, colsum(dcv * pa1))
                        + _put_row(shp8, 0, colsum(dcv * pa2)))

        rx, rg = rx_ref[...], rg_ref[...]
        dyb = dyb_ref[...]
        rxh = rxh_ref[...] * nz
        rx1, rx2, rx3 = _shift_down(rx, rxh, 1), _shift_down(rx, rxh, 2), _shift_down(rx, rxh, 3)
        xb = crw_ref[3:4, :] * rx + crw_ref[2:3, :] * rx1 + crw_ref[1:2, :] * rx2 + crw_ref[0:1, :] * rx3 + crb_ref[...]
        lam = lam_ref[...]
        r, ig, sp, a, mult = _lru_gates(xb, wa_ref, wx_ref, ba_ref[...], bx_ref[...], lam)
        hl = hl_ref[...]
        hprev = _shift_down(hl, hlh_ref[...] * nz, 1)
        dz_ref[:, C_RG * GROUP_W:(C_RG + 1) * GROUP_W] = (dyb * hl * _dsilu(rg)).astype(BF16)
        dh = dyb * _silu(rg)
        a_next = _shift_up(a, c_a[...], 1)
        ca, cb = _scan_anticausal(a_next, dh)
        g = cb + ca * c_g[0:1, :]
        c_g[...] = g[:SUBLANES, :]
        c_a[...] = a[:SUBLANES, :]
        u = ig * xb
        da = g * hprev
        dmult = g * u
        du = g * mult
        dlog_a = da * a - dmult * (a * a) / mult
        dr = dlog_a * (-RG_C * sp)
        dga = dr * r * (1.0 - r)
        dgx = (du * xb) * ig * (1.0 - ig)
        dgab, dgxb = dga.astype(BF16), dgx.astype(BF16)
        dxb = du * ig + _dot_nt(dgab, wa_ref[...]) + _dot_nt(dgxb, wx_ref[...])
        xbb = xb.astype(BF16)
        gwa_ref[...] += _dot_tn(xbb, dgab)
        gwx_ref[...] += _dot_tn(xbb, dgxb)
        nxt = c_dxb[...]
        drx = (crw_ref[3:4, :] * dxb + crw_ref[2:3, :] * _shift_up(dxb, nxt, 1) + crw_ref[1:2, :] * _shift_up(dxb, nxt, 2)
               + crw_ref[0:1, :] * _shift_up(dxb, nxt, 3))
        c_dxb[...] = dxb[:SUBLANES, :]
        dz_ref[:, C_RX * GROUP_W:(C_RX + 1) * GROUP_W] = drx.astype(BF16)
        dlam = colsum(dlog_a * (-RG_C * r)) * (-jax.nn.sigmoid(-lam))
        gr_ref[...] += (_put_row(shp8, 3, colsum(dxb * rx)) + _put_row(shp8, 2, colsum(dxb * rx1))
                        + _put_row(shp8, 1, colsum(dxb * rx2)) + _put_row(shp8, 0, colsum(dxb * rx3))
                        + _put_row(shp8, 4, colsum(dxb)) + _put_row(shp8, 5, colsum(dga))
                        + _put_row(shp8, 6, colsum(dgx)) + _put_row(shp8, 7, dlam))

        cu, cvv, cg = cu_ref[...], cv_ref[...], cg_ref[...]
        dyc = dyc_ref[...]
        u_c, du_c = _gelu_and_grad(cu)
        gv, dgv_c = _gelu_and_grad(cvv)
        rs = lax.rsqrt(jnp.mean(gv * gv, axis=-1, keepdims=True) + NORM_EPS)
        vh = gv * rs
        gng = gng_ref[...]
        vvb = (vh * gng).astype(BF16)
        spat = _gmlp_spatial(ws_ref, vvb, head) + jnp.concatenate([bs_ref[...]] * (TILE_MIX // GMLP_CHUNK), axis=0)
        sgc = _silu(cg)
        dz_ref[:, C_CU * GROUP_W:(C_CU + 1) * GROUP_W] = (dyc * spat * sgc * du_c).astype(BF16)
        dz_ref[:, C_CG * GROUP_W:(C_CG + 1) * GROUP_W] = (dyc * u_c * spat * _dsilu(cg)).astype(BF16)
        dsp = dyc * u_c * sgc
        dspb = dsp.astype(BF16)
        tril = (lax.broadcasted_iota(jnp.int32, (GMLP_CHUNK, GMLP_CHUNK), 0)
                >= lax.broadcasted_iota(jnp.int32, (GMLP_CHUNK, GMLP_CHUNK), 1))
        head_c = head[:GMLP_CHUNK]
        dvv_parts = []
        gbs = jnp.zeros((GMLP_CHUNK, GROUP_W), F32)
        for j in range(TILE_MIX // GMLP_CHUNK):
            sl = slice(j * GMLP_CHUNK, (j + 1) * GMLP_CHUNK)
            dblk = dspb[sl, :]
            vblk = vvb[sl, :]
            gbs = gbs + dsp[sl, :]
            acc = jnp.zeros((GMLP_CHUNK, GROUP_W), F32)
            for h in range(N_HEADS):
                acc = jnp.where(head_c == h, _dot(wst_ref[h], dblk), acc)
                dm = jnp.where(head_c == h, dblk, jnp.zeros_like(dblk))
                gws_ref[h] += jnp.where(tril, _dot_nt(dm, vblk), 0.0)
            dvv_parts.append(acc)
        gbs_ref[...] += gbs
        dvv = jnp.concatenate(dvv_parts, axis=0)
        gn_ref[...] += _put_row(shp8, 0, colsum(dvv * vh))
        dvh = dvv * gng
        dgv = rs * (dvh - vh * jnp.mean(dvh * vh, axis=-1, keepdims=True))
        dz_ref[:, C_CV * GROUP_W:(C_CV + 1) * GROUP_W] = (dgv * dgv_c).astype(BF16)

        dsum = dqkv1_ref[...] + _interleave_load(dqkv4_ref, d4, st_a) + _interleave_load(dqkv16_ref, d16, st_b)
        dz_ref[:, C_DQ * GROUP_W:(C_DV + 1) * GROUP_W] = dsum.astype(BF16)
        dz_ref[:, C_DG * GROUP_W:(C_DG + 1) * GROUP_W] = ddg_ref[...].astype(BF16)

    per = TILE_MIX // SUBLANES
    qkv_w = 3 * GROUP_W
    rev = lambda c: pl.BlockSpec((TILE_MIX, GROUP_W), lambda t, c=c: (n_tiles - 1 - t, c))
    revh = lambda c: pl.BlockSpec((SUBLANES, GROUP_W),
                                  lambda t, c=c: (jnp.maximum((n_tiles - 1 - t) * per - 1, 0), c))
    revr = lambda dil: pl.BlockSpec((dil, TILE_MIX // dil, qkv_w), lambda t: (0, n_tiles - 1 - t, 0))
    names = ("caw", "crw", "crb", "wa", "wx", "ba", "bx", "lam", "gng", "ws", "wst", "bs")
    in_specs = ([rev(0), rev(1), rev(2)]
                + [rev(c) for c in (C_AX, C_AB, C_AC, C_AG, C_RX, C_RG, C_CU, C_CV, C_CG)]
                + [revh(C_AX), revh(C_AC), revh(C_RX), rev(0), revh(0),
                   pl.BlockSpec((TILE_MIX, qkv_w), lambda t: (n_tiles - 1 - t, 0)), revr(d4), revr(d16), rev(0)]
                + [_full(wts[k].shape) for k in names])
    small = jax.ShapeDtypeStruct((SUBLANES, GROUP_W), F32)
    sq = jax.ShapeDtypeStruct((GROUP_W, GROUP_W), F32)
    out_shape = [jax.ShapeDtypeStruct((s, D_IN), BF16), small, small, small, sq, sq,
                 jax.ShapeDtypeStruct((N_HEADS, GMLP_CHUNK, GMLP_CHUNK), F32),
                 jax.ShapeDtypeStruct((GMLP_CHUNK, GROUP_W), F32)]
    out_specs = ([pl.BlockSpec((TILE_MIX, D_IN), lambda t: (n_tiles - 1 - t, 0))]
                 + [_full(o.shape) for o in out_shape[1:]])
    return pl.pallas_call(
        body, name="mix_bwd", grid=(n_tiles,),
        in_specs=in_specs, out_specs=out_specs, out_shape=out_shape,
        scratch_shapes=[pltpu.VMEM((SUBLANES, GROUP_W), F32)] * 4 + _stage_scratch(TILE_MIX, qkv_w, 2),
        compiler_params=_params(("arbitrary",)),
    )(dy, dy, dy, *([z] * 12), hl, hl, *dqkv, ddg, *[wts[k] for k in names])


def _block_diag(w):
    eye = jnp.eye(N_HEADS, dtype=w.dtype)
    return (w[:, :, None, :] * eye[:, None, :, None]).reshape(GROUP_W, GROUP_W)


def _diag_blocks(g):
    g4 = g.reshape(N_HEADS, HEAD_DIM, N_HEADS, HEAD_DIM)
    return jnp.stack([g4[h, :, h, :] for h in range(N_HEADS)])


def _layer_weights(p, l):
    tril = jnp.tril(jnp.ones((GMLP_CHUNK, GMLP_CHUNK), dtype=bool))
    ws = jnp.where(tril[None], p["gmlp_ws"][l], 0.0).astype(BF16)
    row = lambda a: a[l][None, :]
    return dict(
        caw=p["conv_a_w"][l], crw=p["conv_r_w"][l], crb=row(p["conv_r_b"]),
        wa=_block_diag(p["lru_wa"][l]).astype(BF16), wx=_block_diag(p["lru_wx"][l]).astype(BF16),
        ba=row(p["lru_ba"]), bx=row(p["lru_bx"]), lam=row(p["lru_lambda"]), gng=row(p["gmlp_norm_g"]),
        ws=ws, wst=jnp.transpose(ws, (0, 2, 1)),
        bs=jnp.repeat(jnp.transpose(p["gmlp_bs"][l]), HEAD_DIM, axis=1))


def _flat(a):
    return a.reshape(a.shape[0] * a.shape[1], a.shape[2])


def _split(a, dil):
    return a.reshape(dil, a.shape[0] // dil, a.shape[1])


def local_step(x, tgt, final_g, depth, chip, layer_weights, projections_done):
    saved = []
    for l in range(depth):
        gain, w_in_l, w_out_l, wts = layer_weights(l, x)
        z, h, *qkvs = in_fwd(x, gain, w_in_l)
        qkvs = [_flat(q) if q.ndim == 3 else q for q in qkvs]
        attn = []
        for q, d in zip(qkvs, PATTERN_DILS):
            o_p, lse_p = attn_fwd(q, d)
            attn.append((o_p, lse_p) if d == 1 else (_split(o_p, d), _split(lse_p, d)))
        y, hl, o, lse, lse4, lse16 = mix_fwd(z, attn, wts)
        saved.append(dict(x=x, z=z, h=h, y=y, hl=hl, o=o, qkvs=qkvs, lses=(lse, _flat(lse4), _flat(lse16)), wts=wts,
                          gain=gain, w_in=w_in_l, w_out=w_out_l))
        x = out_fwd(y, w_out_l, x)

    loss, dx, dfg = loss_head(x, final_g[None, :], tgt)
    grads = {k: [None] * depth for k in
             ("norm_g", "conv_a_w", "conv_r_w", "conv_r_b", "lru_wa", "lru_ba", "lru_wx", "lru_bx",
              "lru_lambda", "gmlp_norm_g", "gmlp_ws", "gmlp_bs")}
    zero = None
    for l in reversed(range(depth)):
        sv = saved[l]
        dy, ddg, do1, do4, do16, dl1, dl4, dl16 = out_bwd(dx, sv["w_out"], sv["z"], sv["o"])
        g_w_out = matmul_tn(sv["y"], dx, 1)
        dqkv = []
        for q, do, lse, dl, d in zip(sv["qkvs"], (do1, _flat(do4), _flat(do16)), sv["lses"],
                                     (dl1, _flat(dl4), _flat(dl16)), PATTERN_DILS):
            g = attn_bwd(q, do, lse, dl, d)
            dqkv.append(g if d == 1 else _split(g, d))
        dz, ga, gr, gn, gwa, gwx, gws, gbs = mix_bwd(dy, sv["z"], sv["hl"], dqkv, ddg, sv["wts"])
        gain = sv["gain"] if zero is None else sv["gain"] + zero
        zero = projections_done(l, *grad_w_in(sv["h"], dz, chip), g_w_out)
        if l == 0 and zero is not None:
            gain = gain + zero
        dx, dgn = in_bwd(dz, sv["w_in"], sv["x"], gain, dx)
        grads["norm_g"][l] = dgn[0]
        grads["conv_a_w"][l] = ga[:3]
        grads["conv_r_w"][l] = gr[:4]
        grads["conv_r_b"][l] = gr[4]
        grads["lru_ba"][l] = gr[5]
        grads["lru_bx"][l] = gr[6]
        grads["lru_lambda"][l] = gr[7]
        grads["gmlp_norm_g"][l] = gn[0]
        grads["lru_wa"][l] = _diag_blocks(gwa)
        grads["lru_wx"][l] = _diag_blocks(gwx)
        grads["gmlp_ws"][l] = gws
        grads["gmlp_bs"][l] = jnp.transpose(gbs.reshape(GMLP_CHUNK, N_HEADS, HEAD_DIM).sum(-1))
    grads = {k: jnp.stack(v) for k, v in grads.items()}
    grads["final_g"] = dfg[0]
    return loss, dx, grads


MESH = pl.DeviceIdType.MESH
N_CHIPS = 4
N_DEV = 8
ANY = pl.BlockSpec(memory_space=pl.ANY)


def _place():
    x, y, c = lax.axis_index("x"), lax.axis_index("y"), lax.axis_index("c")
    chips = [(1 - x, y), (x, 1 - y), (1 - x, 1 - y)]
    return x, y, c, chips


def _remote(src, dst, ssem, rsem, to):
    return pltpu.make_async_remote_copy(src_ref=src, dst_ref=dst, send_sem=ssem, recv_sem=rsem,
                                        device_id=to, device_id_type=MESH)


HBM = pl.BlockSpec(memory_space=pltpu.HBM)
SEM = pl.BlockSpec(memory_space=pltpu.SEMAPHORE)
DATAFLOW = pltpu.SideEffectType.DATAFLOW_SIDE_EFFECTING
GATHER, SCATTER = "gather", "scatter"


def _chip_copies(mode, src_refs, land_refs, ssem, rsem):
    x, y, c, chips = _place()
    me = 2 * x + y
    n = len(src_refs)
    copies = []
    for k, (cx, cy) in enumerate(chips):
        for a in range(n):
            if mode == GATHER:
                src, dst = src_refs[a], land_refs[a].at[me]
            else:
                src, dst = src_refs[a].at[2 * cx + cy], land_refs[a].at[k]
            copies.append(_remote(src, dst, ssem.at[n * k + a], rsem.at[n * k + a], (cx, cy, c)))
    return copies


def exchange_start(mode, srcs, after, name):
    n = len(srcs)
    if mode == GATHER:
        lands = [lax.empty((N_CHIPS,) + s.shape, s.dtype) for s in srcs]
    else:
        lands = [lax.empty((N_CHIPS - 1,) + s.shape[1:], s.dtype) for s in srcs]
    extra = [] if after is None else [after]

    def body(*refs):
        src_refs, land_refs = refs[:n], refs[n:2 * n]
        ssem, rsem = refs[2 * n + len(extra)], refs[2 * n + len(extra) + 1]
        token = refs[-1]
        for cp in _chip_copies(mode, src_refs, land_refs, ssem, rsem):
            cp.start()
        token[...] = jnp.zeros_like(token)

    arrays = list(srcs) + lands
    return pl.pallas_call(
        body, name=name,
        out_shape=(pltpu.SemaphoreType.DMA((3 * n,)), pltpu.SemaphoreType.DMA((3 * n,)),
                   *[pltpu.HBM(a.shape, a.dtype) for a in arrays], jax.ShapeDtypeStruct((SUBLANES, LANES), F32)),
        in_specs=[HBM] * (2 * n) + [ANY] * len(extra),
        out_specs=(SEM, SEM, *[HBM] * (2 * n), pl.BlockSpec(memory_space=pltpu.VMEM)),
        input_output_aliases={i: 2 + i for i in range(2 * n)},
        compiler_params=pltpu.CompilerParams(has_side_effects=DATAFLOW),
    )(*[pltpu.with_memory_space_constraint(a, pltpu.HBM) for a in arrays], *extra)


def exchange_wait(mode, started, after, name):
    ssem, rsem, *thru, _ = started
    n = len(thru) // 2

    def body(*refs):
        src_refs, land_refs = refs[:n], refs[n:2 * n]
        ssem_ref, rsem_ref = refs[2 * n], refs[2 * n + 1]
        for cp in _chip_copies(mode, src_refs, land_refs, ssem_ref, rsem_ref):
            cp.wait_send()
            cp.wait_recv()

    outs = pl.pallas_call(
        body, name=name,
        out_shape=[pltpu.HBM(a.shape, a.dtype) for a in thru],
        in_specs=[HBM] * (2 * n) + [SEM, SEM, ANY],
        out_specs=[HBM] * (2 * n),
        input_output_aliases={i: i for i in range(2 * n)},
        compiler_params=pltpu.CompilerParams(has_side_effects=DATAFLOW),
    )(*thru, ssem, rsem, after)
    return outs[n:]


def sibling_exchange(p1, p2):
    def body(p1_ref, p2_ref, q1_ref, q2_ref, ssem, rsem):
        x, y, c, _ = _place()
        copies = [_remote(p_ref, q_ref, ssem.at[a], rsem.at[a], (x, y, 1 - c))
                  for a, (p_ref, q_ref) in enumerate(((p1_ref, q1_ref), (p2_ref, q2_ref)))]
        for cp in copies:
            cp.start()
        for cp in copies:
            cp.wait()

    return pl.pallas_call(
        body, name="sibling_exchange",
        in_specs=[ANY, ANY], out_specs=[ANY, ANY],
        out_shape=[jax.ShapeDtypeStruct(p.shape, p.dtype) for p in (p1, p2)],
        scratch_shapes=[pltpu.SemaphoreType.DMA((2,)), pltpu.SemaphoreType.DMA((2,))],
    )(p1, p2)


def gather_all(v):
    m_per, n = v.shape

    def body(x_ref, out_ref, send_sems, recv_sems, local_sem):
        x, y, c, chips = _place()
        me, sibling = (x, y, c), (x, y, 1 - c)

        def rows(px, py, pc):
            return out_ref.at[pl.ds((4 * px + 2 * py + pc) * m_per, m_per), :]

        def copy(k, block, to, src=None):
            return _remote(rows(*block) if src is None else src, rows(*block), send_sems.at[k], recv_sems.at[k], to)

        mine = pltpu.make_async_copy(x_ref, rows(*me), local_sem)
        mine.start()
        first = [copy(0, me, sibling, src=x_ref)]
        first += [copy(1 + j, me, (*chip, c), src=x_ref) for j, chip in enumerate(chips)]
        for cp in first:
            cp.start()
        passed = [copy(4 + j, (*chip, c), sibling) for j, chip in enumerate(chips)]
        for j, chip in enumerate(chips):
            copy(1 + j, (*chip, c), me).wait_recv()
            passed[j].start()
        copy(0, sibling, me).wait_recv()
        for j, chip in enumerate(chips):
            copy(4 + j, (*chip, 1 - c), me).wait_recv()
        for cp in first + passed:
            cp.wait_send()
        mine.wait()

    return pl.pallas_call(
        body, name="gather_all",
        out_shape=jax.ShapeDtypeStruct((N_DEV * m_per, n), v.dtype),
        in_specs=[pl.BlockSpec(memory_space=pltpu.VMEM)],
        out_specs=pl.BlockSpec(memory_space=pltpu.VMEM),
        scratch_shapes=[pltpu.SemaphoreType.DMA((7,)), pltpu.SemaphoreType.DMA((7,)), pltpu.SemaphoreType.DMA],
        compiler_params=pltpu.CompilerParams(vmem_limit_bytes=VMEM_LIMIT),
    )(v)


TILE_ROWS = 256


def sum_partials(own, parts):
    r, c = own.shape
    k = parts.shape[0]

    def body(o_ref, p_ref, out_ref):
        acc = o_ref[...]
        for i in range(k):
            acc = acc + p_ref[i].astype(F32)
        out_ref[...] = acc

    return pl.pallas_call(
        body, name="sum_partials", grid=(r // TILE_ROWS,),
        in_specs=[pl.BlockSpec((TILE_ROWS, c), lambda i: (i, 0)), pl.BlockSpec((k, TILE_ROWS, c), lambda i: (0, i, 0))],
        out_specs=pl.BlockSpec((TILE_ROWS, c), lambda i: (i, 0)),
        out_shape=jax.ShapeDtypeStruct((r, c), F32),
        compiler_params=_params(("parallel",)),
    )(own, parts)


def adamw(w, ga, gb, m, v):
    n, r, c = w.shape
    tile = min(TILE_ROWS, r)

    def body(w_ref, ga_ref, gb_ref, m_ref, v_ref, g_ref, d_ref, m2_ref, v2_ref):
        g = ga_ref[...] + gb_ref[...]
        g_ref[...] = g
        m2 = ADAM_B1 * m_ref[...] + (1.0 - ADAM_B1) * g
        v2 = ADAM_B2 * v_ref[...] + (1.0 - ADAM_B2) * (g * g)
        m2_ref[...] = m2
        v2_ref[...] = v2
        m_hat = m2 / (1.0 - ADAM_B1 ** ADAM_STEP)
        v_hat = v2 / (1.0 - ADAM_B2 ** ADAM_STEP)
        d_ref[...] = -ADAM_LR * (m_hat / (jnp.sqrt(v_hat) + ADAM_EPS) + ADAM_WD * w_ref[...])

    spec = pl.BlockSpec((1, tile, c), lambda j, i: (j, i, 0))
    return pl.pallas_call(
        body, name="adamw", grid=(n, r // tile),
        in_specs=[spec] * 5, out_specs=[spec] * 4,
        out_shape=[jax.ShapeDtypeStruct((n, r, c), F32)] * 4,
        compiler_params=_params(("parallel", "parallel")),
    )(w, ga, gb, m, v)


REPLICATED = ("norm_g", "conv_r_b", "lru_wa", "lru_ba", "lru_wx", "lru_bx", "lru_lambda", "gmlp_norm_g",
              "gmlp_ws", "gmlp_bs", "final_g")
CHIP_SHARDED_SMALL = ("conv_a_w", "conv_r_w")
PACK_LANES = 128


def _pack(arrays):
    flat = jnp.concatenate([a.reshape(-1) for a in arrays])
    pad = (-flat.shape[0]) % (TILE_ROWS * PACK_LANES)
    return jnp.pad(flat, (0, pad)).reshape(-1, PACK_LANES)


def _unpack(packed, shapes):
    flat = packed.reshape(-1)
    out, off = [], 0
    for shp in shapes:
        n = math.prod(shp)
        out.append(flat[off:off + n].reshape(shp))
        off += n
    return out


def kernel(x, norm_g, w_in, conv_a_w, conv_r_w, conv_r_b, lru_wa, lru_ba, lru_wx, lru_bx, lru_lambda, gmlp_norm_g, gmlp_ws, gmlp_bs, w_out, final_g, loss_target, m_norm_g, m_w_in, m_conv_a_w, m_conv_r_w, m_conv_r_b, m_lru_wa, m_lru_ba, m_lru_wx, m_lru_bx, m_lru_lambda, m_gmlp_norm_g, m_gmlp_ws, m_gmlp_bs, m_w_out, m_final_g, v_norm_g, v_w_in, v_conv_a_w, v_conv_r_w, v_conv_r_b, v_lru_wa, v_lru_ba, v_lru_wx, v_lru_bx, v_lru_lambda, v_gmlp_norm_g, v_gmlp_ws, v_gmlp_bs, v_w_out, v_final_g):
    names = ("norm_g", "w_in", "conv_a_w", "conv_r_w", "conv_r_b", "lru_wa", "lru_ba", "lru_wx", "lru_bx",
             "lru_lambda", "gmlp_norm_g", "gmlp_ws", "gmlp_bs", "w_out", "final_g")
    w = dict(zip(names, (norm_g, w_in, conv_a_w, conv_r_w, conv_r_b, lru_wa, lru_ba, lru_wx, lru_bx, lru_lambda,
                         gmlp_norm_g, gmlp_ws, gmlp_bs, w_out, final_g)))
    m = dict(zip(names, (m_norm_g, m_w_in, m_conv_a_w, m_conv_r_w, m_conv_r_b, m_lru_wa, m_lru_ba, m_lru_wx, m_lru_bx,
                         m_lru_lambda, m_gmlp_norm_g, m_gmlp_ws, m_gmlp_bs, m_w_out, m_final_g)))
    v = dict(zip(names, (v_norm_g, v_w_in, v_conv_a_w, v_conv_r_w, v_conv_r_b, v_lru_wa, v_lru_ba, v_lru_wx, v_lru_bx,
                         v_lru_lambda, v_gmlp_norm_g, v_gmlp_ws, v_gmlp_bs, v_w_out, v_final_g)))
    depth, _, in_cols = w_in.shape
    out_rows = w_out.shape[1]
    conv_ch = conv_a_w.shape[2]
    chip = 2 * lax.axis_index("x") + lax.axis_index("y")

    taps = conv_a_w.shape[1] + conv_r_w.shape[1]
    w_in_h, w_out_h = w_in.astype(BF16), w_out.astype(BF16)
    conv_own = jnp.concatenate([conv_a_w, conv_r_w], axis=1).reshape(depth * taps, conv_ch)
    gathers, token = [], None
    for l in range(depth):
        srcs = [w_in_h[l], w_out_h[l]] + ([conv_own] if l == 0 else [])
        gathers.append(exchange_start(GATHER, srcs, token, f"gather_start_{l}"))
        token = gathers[-1][-1]
    p = dict(w)

    def with_own(land, own):
        return lax.dynamic_update_slice(land, own[None], (chip,) + (0,) * own.ndim)

    def layer_weights(l, x_l):
        lands = exchange_wait(GATHER, gathers[l], x_l, f"gather_wait_{l}")
        w_in_l = with_own(lands[0], w_in_h[l]).transpose(1, 0, 2).reshape(D_MODEL, D_IN)
        w_out_l = with_own(lands[1], w_out_h[l]).reshape(D_MIX, D_MODEL)
        gain = norm_g[l][None, :]
        if l == 0:
            conv = with_own(lands[2], conv_own).reshape(N_CHIPS, depth, taps, conv_ch)
            conv = conv.transpose(1, 2, 0, 3).reshape(depth, taps, GROUP_W)
            p["conv_a_w"] = conv[:, :conv_a_w.shape[1]]
            p["conv_r_w"] = conv[:, conv_a_w.shape[1]:]
            gain = gain + token[0, 0]
        return gain, w_in_l, w_out_l, _layer_weights(p, l)

    scatters, owns = [None] * depth, [None] * depth

    def projections_done(l, g_w_in_by_chip, g_w_in_own, g_w_out):
        go = g_w_out.reshape(N_CHIPS, out_rows, D_MODEL)
        owns[l] = (g_w_in_own, lax.dynamic_index_in_dim(go, chip, axis=0, keepdims=False))
        scatters[l] = exchange_start(SCATTER, [g_w_in_by_chip, go.astype(BF16)], None, f"scatter_start_{l}")
        return scatters[l][-1][0, 0]

    loss8, dx, grads = local_step(x[0], loss_target[0], final_g, depth, chip.reshape(1), layer_weights,
                                  projections_done)
    loss = lax.psum(loss8[0, 0], ("x", "y", "c"))

    parts = [exchange_wait(SCATTER, scatters[l], dx, f"scatter_wait_{l}") for l in range(depth)]
    r1 = jnp.concatenate([parts[l][0] for l in range(depth)], axis=1)
    r2 = jnp.concatenate([parts[l][1] for l in range(depth)], axis=1)
    own1 = jnp.concatenate([owns[l][0] for l in range(depth)], axis=0)
    own2 = jnp.concatenate([owns[l][1] for l in range(depth)], axis=0)
    p1 = sum_partials(own1, r1)
    p2 = sum_partials(own2, r2)
    q1, q2 = sibling_exchange(p1, p2)
    res = {}
    res["w_in"] = adamw(w_in, p1.reshape(w_in.shape), q1.reshape(w_in.shape), m_w_in, v_w_in)
    res["w_out"] = adamw(w_out, p2.reshape(w_out.shape), q2.reshape(w_out.shape), m_w_out, v_w_out)

    small = REPLICATED + CHIP_SHARDED_SMALL
    packed = _pack([grads[k] for k in small])
    rows = packed.shape[0]
    allp = gather_all(packed).reshape(N_DEV, rows, PACK_LANES)
    total = sum_partials(allp[0], allp[1:])
    gs = dict(zip(small, _unpack(total, [grads[k].shape for k in small])))
    for k in CHIP_SHARDED_SMALL:
        gs[k] = lax.dynamic_slice_in_dim(gs[k], chip * conv_ch, conv_ch, axis=2)
    wp, gp, mp, vp = (_pack([d[k] for k in small])[None] for d in (w, gs, m, v))
    outs = adamw(wp, gp, jnp.zeros_like(gp), mp, vp)
    shapes = [w[k].shape for k in small]
    unpacked = [_unpack(o, shapes) for o in outs]
    for i, k in enumerate(small):
        res[k] = [u[i] for u in unpacked]

    return (loss, dx[None], *[res[k][0] for k in names], *[res[k][1] for k in names],
            *[res[k][2] for k in names], *[res[k][3] for k in names])
```

```python
import functools
import math

import jax
import jax.numpy as jnp
import numpy as np
from jax import lax
from jax.experimental import pallas as pl
from jax.experimental.pallas import tpu as pltpu

F32 = jnp.float32
BF16 = jnp.bfloat16

D_MODEL = 1024
GROUP_W = 256
N_HEADS = 4
HEAD_DIM = 64
N_CHUNKS = 13
D_IN = N_CHUNKS * GROUP_W
D_MIX = 4 * GROUP_W
NORM_EPS = 1e-6
RG_C = 8.0
GMLP_CHUNK = 128
ATTN_BLOCK = 128
PATTERN_DILS = (1, 4, 16)
N_PATTERNS = len(PATTERN_DILS)
ALIBI_SLOPES = tuple(2.0 ** (-8.0 * (h + 1) / N_HEADS) for h in range(N_HEADS))
ATTN_SCALE = 1.0 / math.sqrt(HEAD_DIM)
NEG_BIG = -1e30

ADAM_LR = 0.001
ADAM_B1 = 0.9
ADAM_B2 = 0.999
ADAM_EPS = 1e-08
ADAM_WD = 0.01
ADAM_STEP = 10

C_AX, C_AB, C_AC, C_AG, C_RX, C_RG, C_CU, C_CV, C_CG, C_DQ, C_DK, C_DV, C_DG = range(13)

SUBLANES = 8
LANES = 128
VMEM_LIMIT = 56 * 1024 * 1024
TILE_IN = 512
TILE_MIX = 256
TILE_DW = 512
ATTN_QB = 4
GELU_K0 = math.sqrt(2.0 / math.pi)
GELU_K1 = 0.044715


def _params(sem):
    return pltpu.CompilerParams(dimension_semantics=sem, vmem_limit_bytes=VMEM_LIMIT)


def _silu(x):
    return x * jax.nn.sigmoid(x)


def _dsilu(x):
    s = jax.nn.sigmoid(x)
    return s * (1.0 + x * (1.0 - s))


def _gelu(x):
    return 0.5 * x * (1.0 + jnp.tanh(GELU_K0 * (x + GELU_K1 * x * x * x)))


def _gelu_and_grad(x):
    t = jnp.tanh(GELU_K0 * (x + GELU_K1 * x * x * x))
    g = 0.5 * x * (1.0 + t)
    dg = 0.5 * (1.0 + t) + 0.5 * x * (1.0 - t * t) * GELU_K0 * (1.0 + 3.0 * GELU_K1 * x * x)
    return g, dg


def _neg_expm1(x):
    series = x * (1.0 + x * (0.5 + x * (1.0 / 6.0 + x * (1.0 / 24.0 + x * (1.0 / 120.0)))))
    return -jnp.where(x > -0.05, series, jnp.exp(x) - 1.0)


def _shift_down(v, halo, k):
    r = pltpu.roll(v, k, 0)
    rh = pltpu.roll(halo, k, 0)
    row = lax.broadcasted_iota(jnp.int32, halo.shape, 0)
    top = jnp.where(row < k, rh, r[:SUBLANES])
    return jnp.concatenate([top, r[SUBLANES:]], axis=0)


def _shift_up(v, halo, k):
    t = v.shape[0]
    r = pltpu.roll(v, t - k, 0)
    rh = pltpu.roll(halo, SUBLANES - k, 0)
    row = lax.broadcasted_iota(jnp.int32, halo.shape, 0)
    bot = jnp.where(row >= SUBLANES - k, rh, r[t - SUBLANES:])
    return jnp.concatenate([r[:t - SUBLANES], bot], axis=0)


def _scan_causal(a, b):
    t = a.shape[0]
    row = lax.broadcasted_iota(jnp.int32, a.shape, 0)
    d = 1
    while d < t:
        m = row >= d
        a_s = jnp.where(m, pltpu.roll(a, d, 0), 1.0)
        b_s = jnp.where(m, pltpu.roll(b, d, 0), 0.0)
        b = a * b_s + b
        a = a * a_s
        d *= 2
    return a, b


def _scan_anticausal(a, b):
    t = a.shape[0]
    row = lax.broadcasted_iota(jnp.int32, a.shape, 0)
    d = 1
    while d < t:
        m = row < t - d
        a_s = jnp.where(m, pltpu.roll(a, t - d, 0), 1.0)
        b_s = jnp.where(m, pltpu.roll(b, t - d, 0), 0.0)
        b = a * b_s + b
        a = a * a_s
        d *= 2
    return a, b


def _head_of_lane(shape):
    return lax.broadcasted_iota(jnp.int32, shape, len(shape) - 1) // HEAD_DIM


def _put_row(acc_shape, k, row_vec):
    row = lax.broadcasted_iota(jnp.int32, acc_shape, 0)
    return jnp.where(row == k, jnp.broadcast_to(row_vec, acc_shape), 0.0)


def _dot(a, b):
    return jnp.dot(a, b, preferred_element_type=F32)


def _dot_nt(a, b):
    return lax.dot_general(a, b, (((1,), (1,)), ((), ())), preferred_element_type=F32)


def _dot_tn(a, b):
    return lax.dot_general(a, b, (((0,), (0,)), ((), ())), preferred_element_type=F32)


def _deinterleave_store(val, stage, outs):
    t, c = val.shape
    for hh in range(c // LANES):
        stage[hh][...] = val[:, hh * LANES:(hh + 1) * LANES].astype(F32)
    for dil, ref in outs:
        for r in range(dil):
            for hh in range(c // LANES):
                ref[r, :, hh * LANES:(hh + 1) * LANES] = stage[hh][pl.ds(r, t // dil, stride=dil), :].astype(ref.dtype)


def _interleave_load(ref, dil, stage):
    _, n, c = ref.shape
    for r in range(dil):
        for hh in range(c // LANES):
            stage[hh][pl.ds(r, n, stride=dil), :] = ref[r, :, hh * LANES:(hh + 1) * LANES].astype(F32)
    return jnp.concatenate([stage[hh][...] for hh in range(c // LANES)], axis=1)


def _stage_scratch(tile, cols, copies):
    return [pltpu.VMEM((tile, LANES), F32)] * (copies * (cols // LANES))


def _by_residue(s, dil, cols, dtype):
    return jax.ShapeDtypeStruct((dil, s // dil, cols), dtype)


def _residue_block(dil, tile, cols):
    return pl.BlockSpec((dil, tile // dil, cols), lambda i: (0, i, 0))


def in_fwd(x, g, w):
    s = x.shape[0]
    qkv_w = 3 * GROUP_W

    def body(x_ref, g_ref, w_ref, z_ref, h_ref, qkv1_ref, qkv4_ref, qkv16_ref, *stage):
        xv = x_ref[...]
        rs = lax.rsqrt(jnp.mean(xv * xv, axis=-1, keepdims=True) + NORM_EPS)
        h = (xv * rs * g_ref[...]).astype(BF16)
        h_ref[...] = h
        z = _dot_nt(h, w_ref[...])
        z_ref[...] = z
        qkv = z[:, C_DQ * GROUP_W:(C_DV + 1) * GROUP_W]
        qkv1_ref[...] = qkv.astype(BF16)
        _deinterleave_store(qkv, stage, ((PATTERN_DILS[1], qkv4_ref), (PATTERN_DILS[2], qkv16_ref)))

    return pl.pallas_call(
        body, name="in_fwd", grid=(s // TILE_IN,),
        in_specs=[pl.BlockSpec((TILE_IN, D_MODEL), lambda i: (i, 0)),
                  pl.BlockSpec((1, D_MODEL), lambda i: (0, 0)),
                  pl.BlockSpec((D_IN, D_MODEL), lambda i: (0, 0))],
        out_specs=[pl.BlockSpec((TILE_IN, D_IN), lambda i: (i, 0)),
                   pl.BlockSpec((TILE_IN, D_MODEL), lambda i: (i, 0)),
                   pl.BlockSpec((TILE_IN, qkv_w), lambda i: (i, 0)),
                   _residue_block(PATTERN_DILS[1], TILE_IN, qkv_w),
                   _residue_block(PATTERN_DILS[2], TILE_IN, qkv_w)],
        out_shape=[jax.ShapeDtypeStruct((s, D_IN), F32), jax.ShapeDtypeStruct((s, D_MODEL), BF16),
                   jax.ShapeDtypeStruct((s, qkv_w), BF16),
                   _by_residue(s, PATTERN_DILS[1], qkv_w, BF16), _by_residue(s, PATTERN_DILS[2], qkv_w, BF16)],
        scratch_shapes=_stage_scratch(TILE_IN, qkv_w, 1),
        compiler_params=_params(("parallel",)),
    )(x, g, w)


def out_fwd(y, w, x):
    s = x.shape[0]

    def body(y_ref, w_ref, x_ref, o_ref):
        o_ref[...] = x_ref[...] + _dot(y_ref[...], w_ref[...])

    return pl.pallas_call(
        body, name="out_fwd", grid=(s // TILE_IN,),
        in_specs=[pl.BlockSpec((TILE_IN, D_MIX), lambda i: (i, 0)),
                  pl.BlockSpec((D_MIX, D_MODEL), lambda i: (0, 0)),
                  pl.BlockSpec((TILE_IN, D_MODEL), lambda i: (i, 0))],
        out_specs=pl.BlockSpec((TILE_IN, D_MODEL), lambda i: (i, 0)),
        out_shape=jax.ShapeDtypeStruct((s, D_MODEL), F32),
        compiler_params=_params(("parallel",)),
    )(y, w, x)


def out_bwd(dx, w, z, o):
    s = dx.shape[0]
    abc = 3 * GROUP_W

    def body(dx_ref, w_ref, dg_ref, o_ref, dy_ref, ddg_ref, do1_ref, do4_ref, do16_ref, dl1_ref, dl4_ref, dl16_ref,
             *stage):
        stage_a, stage_b = stage[:2], stage[2:]
        dy = _dot_nt(dx_ref[...].astype(BF16), w_ref[...])
        dy_ref[...] = dy[:, :abc]
        dyd = dy[:, abc:]
        head = _head_of_lane((TILE_IN, GROUP_W))
        dg = dg_ref[...]
        o = o_ref[...]
        do = dyd * _silu(dg)
        ddg_ref[...] = dyd * o * _dsilu(dg)
        prod = do * o
        dl = jnp.zeros_like(prod)
        for h in range(N_HEADS):
            sm = jnp.sum(jnp.where(head == h, prod, 0.0), axis=-1, keepdims=True)
            dl = jnp.where(head == h, sm, dl)
        do1_ref[...] = do.astype(BF16)
        dl1_ref[...] = dl
        _deinterleave_store(do, stage_a, ((PATTERN_DILS[1], do4_ref), (PATTERN_DILS[2], do16_ref)))
        _deinterleave_store(dl, stage_b, ((PATTERN_DILS[1], dl4_ref), (PATTERN_DILS[2], dl16_ref)))

    row = pl.BlockSpec((TILE_IN, GROUP_W), lambda i: (i, 0))
    r4 = _residue_block(PATTERN_DILS[1], TILE_IN, GROUP_W)
    r16 = _residue_block(PATTERN_DILS[2], TILE_IN, GROUP_W)
    return pl.pallas_call(
        body, name="out_bwd", grid=(s // TILE_IN,),
        in_specs=[pl.BlockSpec((TILE_IN, D_MODEL), lambda i: (i, 0)),
                  pl.BlockSpec((D_MIX, D_MODEL), lambda i: (0, 0)),
                  pl.BlockSpec((TILE_IN, GROUP_W), lambda i: (i, C_DG)), row],
        out_specs=[pl.BlockSpec((TILE_IN, abc), lambda i: (i, 0)), row, row, r4, r16, row, r4, r16],
        out_shape=[jax.ShapeDtypeStruct((s, abc), F32), jax.ShapeDtypeStruct((s, GROUP_W), F32),
                   jax.ShapeDtypeStruct((s, GROUP_W), BF16),
                   _by_residue(s, PATTERN_DILS[1], GROUP_W, BF16), _by_residue(s, PATTERN_DILS[2], GROUP_W, BF16),
                   jax.ShapeDtypeStruct((s, GROUP_W), F32),
                   _by_residue(s, PATTERN_DILS[1], GROUP_W, F32), _by_residue(s, PATTERN_DILS[2], GROUP_W, F32)],
        scratch_shapes=_stage_scratch(TILE_IN, GROUP_W, 2),
        compiler_params=_params(("parallel",)),
    )(dx, w, z, o)


def in_bwd(dz, w, x, g, dx_next):
    s = x.shape[0]

    def body(dz_ref, w_ref, x_ref, g_ref, dxn_ref, dx_ref, dg_ref):
        @pl.when(pl.program_id(0) == 0)
        def _():
            dg_ref[...] = jnp.zeros_like(dg_ref)

        dh = _dot(dz_ref[...], w_ref[...])
        xv = x_ref[...]
        rs = lax.rsqrt(jnp.mean(xv * xv, axis=-1, keepdims=True) + NORM_EPS)
        xh = xv * rs
        dg_ref[...] += _put_row(dg_ref.shape, 0, jnp.sum(dh * xh, axis=0, keepdims=True))
        dn = dh * g_ref[...]
        dx_ref[...] = dxn_ref[...] + rs * (dn - xh * jnp.mean(dn * xh, axis=-1, keepdims=True))

    return pl.pallas_call(
        body, name="in_bwd", grid=(s // TILE_IN,),
        in_specs=[pl.BlockSpec((TILE_IN, D_IN), lambda i: (i, 0)),
                  pl.BlockSpec((D_IN, D_MODEL), lambda i: (0, 0)),
                  pl.BlockSpec((TILE_IN, D_MODEL), lambda i: (i, 0)),
                  pl.BlockSpec((1, D_MODEL), lambda i: (0, 0)),
                  pl.BlockSpec((TILE_IN, D_MODEL), lambda i: (i, 0))],
        out_specs=[pl.BlockSpec((TILE_IN, D_MODEL), lambda i: (i, 0)),
                   pl.BlockSpec((SUBLANES, D_MODEL), lambda i: (0, 0))],
        out_shape=[jax.ShapeDtypeStruct((s, D_MODEL), F32), jax.ShapeDtypeStruct((SUBLANES, D_MODEL), F32)],
        compiler_params=_params(("arbitrary",)),
    )(dz, w, x, g, dx_next)


def matmul_tn(a, b, n_split):
    s, m = a.shape
    n = b.shape[1]
    tn = n // n_split

    def body(a_ref, b_ref, o_ref):
        @pl.when(pl.program_id(1) == 0)
        def _():
            o_ref[...] = jnp.zeros_like(o_ref)

        o_ref[...] += _dot_tn(a_ref[...], b_ref[...].astype(BF16))

    return pl.pallas_call(
        body, name="matmul_tn", grid=(n_split, s // TILE_DW),
        in_specs=[pl.BlockSpec((TILE_DW, m), lambda j, k: (k, 0)),
                  pl.BlockSpec((TILE_DW, tn), lambda j, k: (k, j))],
        out_specs=pl.BlockSpec((m, tn), lambda j, k: (0, j)),
        out_shape=jax.ShapeDtypeStruct((m, n), F32),
        compiler_params=_params(("parallel", "arbitrary")),
    )(a, b)


def grad_w_in(h, dz, chip):
    s = h.shape[0]
    rows = D_IN // N_CHIPS

    def body(chip_ref, h_ref, dz_ref, staged_ref, own_ref, acc):
        k = pl.program_id(0)

        @pl.when(k == 0)
        def _():
            acc[...] = jnp.zeros_like(acc)

        acc[...] += _dot_tn(dz_ref[...], h_ref[...])

        @pl.when(k == s // TILE_DW - 1)
        def _():
            for j in range(N_CHIPS):
                part = acc[j * rows:(j + 1) * rows, :]
                staged_ref[j] = part.astype(BF16)

                @pl.when(chip_ref[0] == j)
                def _():
                    own_ref[...] = part

    return pl.pallas_call(
        body, name="grad_w_in",
        grid_spec=pltpu.PrefetchScalarGridSpec(
            num_scalar_prefetch=1, grid=(s // TILE_DW,),
            in_specs=[pl.BlockSpec((TILE_DW, D_MODEL), lambda k, c: (k, 0)),
                      pl.BlockSpec((TILE_DW, D_IN), lambda k, c: (k, 0))],
            out_specs=[pl.BlockSpec((N_CHIPS, rows, D_MODEL), lambda k, c: (0, 0, 0)),
                       pl.BlockSpec((rows, D_MODEL), lambda k, c: (0, 0))],
            scratch_shapes=[pltpu.VMEM((D_IN, D_MODEL), F32)]),
        out_shape=[jax.ShapeDtypeStruct((N_CHIPS, rows, D_MODEL), BF16), jax.ShapeDtypeStruct((rows, D_MODEL), F32)],
        compiler_params=_params(("arbitrary",)),
    )(chip, h, dz)


def loss_head(x, g, tgt):
    s = x.shape[0]

    def body(x_ref, g_ref, t_ref, l_ref, dx_ref, dg_ref):
        @pl.when(pl.program_id(0) == 0)
        def _():
            l_ref[...] = jnp.zeros_like(l_ref)
            dg_ref[...] = jnp.zeros_like(dg_ref)

        xv = x_ref[...]
        gv = g_ref[...]
        rs = lax.rsqrt(jnp.mean(xv * xv, axis=-1, keepdims=True) + NORM_EPS)
        xh = xv * rs
        e = xh * gv - t_ref[...]
        part = 0.5 * jnp.sum(jnp.mean(e * e, axis=-1, keepdims=True), axis=0, keepdims=True)
        l_ref[...] += jnp.broadcast_to(part, l_ref.shape)
        dy = e * (1.0 / D_MODEL)
        dg_ref[...] += _put_row(dg_ref.shape, 0, jnp.sum(dy * xh, axis=0, keepdims=True))
        dn = dy * gv
        dx_ref[...] = rs * (dn - xh * jnp.mean(dn * xh, axis=-1, keepdims=True))

    return pl.pallas_call(
        body, name="loss_head", grid=(s // TILE_IN,),
        in_specs=[pl.BlockSpec((TILE_IN, D_MODEL), lambda i: (i, 0)),
                  pl.BlockSpec((1, D_MODEL), lambda i: (0, 0)),
                  pl.BlockSpec((TILE_IN, D_MODEL), lambda i: (i, 0))],
        out_specs=[pl.BlockSpec((SUBLANES, 128), lambda i: (0, 0)),
                   pl.BlockSpec((TILE_IN, D_MODEL), lambda i: (i, 0)),
                   pl.BlockSpec((SUBLANES, D_MODEL), lambda i: (0, 0))],
        out_shape=[jax.ShapeDtypeStruct((SUBLANES, 128), F32), jax.ShapeDtypeStruct((s, D_MODEL), F32),
                   jax.ShapeDtypeStruct((SUBLANES, D_MODEL), F32)],
        compiler_params=_params(("arbitrary",)),
    )(x, g, tgt)


def _attn_bias(dil):
    qi = np.arange(ATTN_BLOCK)[:, None]
    ki = np.arange(2 * ATTN_BLOCK)[None, :]
    delta = qi + ATTN_BLOCK - ki
    band = (delta >= 0) & (delta <= ATTN_BLOCK)
    out = np.empty((2, N_HEADS, ATTN_BLOCK, 2 * ATTN_BLOCK), np.float32)
    for f in range(2):
        ok = band & ((ki >= ATTN_BLOCK) | (f == 0))
        for h in range(N_HEADS):
            out[f, h] = np.where(ok, -ALIBI_SLOPES[h] * dil * delta, NEG_BIG)
    return jnp.asarray(out.reshape(2, N_HEADS * ATTN_BLOCK, 2 * ATTN_BLOCK))


def _stack_heads(a, head):
    return jnp.concatenate([jnp.where(head == h, a, jnp.zeros_like(a)) for h in range(N_HEADS)], axis=0)


def _unstack_heads(a, head):
    out = a[:ATTN_BLOCK]
    for h in range(1, N_HEADS):
        out = jnp.where(head == h, a[h * ATTN_BLOCK:(h + 1) * ATTN_BLOCK], out)
    return out


def _head_column(a):
    return jnp.concatenate([a[:, h * HEAD_DIM:h * HEAD_DIM + 1] for h in range(N_HEADS)], axis=0)


def _attn_specs(n_blocks):
    rows = ATTN_QB * ATTN_BLOCK
    cur = lambda c: pl.BlockSpec((rows, GROUP_W), lambda n, c=c: (n, c))
    prev = lambda c: pl.BlockSpec((ATTN_BLOCK, GROUP_W), lambda n, c=c: (jnp.maximum(n * ATTN_QB - 1, 0), c))
    nxt = lambda c: pl.BlockSpec((ATTN_BLOCK, GROUP_W),
                                 lambda n, c=c: (jnp.minimum(n * ATTN_QB + ATTN_QB, n_blocks - 1), c))
    return cur, prev, nxt


def _keys(kp_ref, k_ref, j):
    prev = kp_ref[...] if j == 0 else k_ref[(j - 1) * ATTN_BLOCK:j * ATTN_BLOCK, :]
    return jnp.concatenate([prev, k_ref[j * ATTN_BLOCK:(j + 1) * ATTN_BLOCK, :]], axis=0)


def attn_fwd(qkv, dil):
    s = qkv.shape[0]
    n_blocks = s // ATTN_BLOCK
    bps = n_blocks // dil
    rows = ATTN_QB * ATTN_BLOCK

    def body(q_ref, kp_ref, k_ref, vp_ref, v_ref, bias_ref, o_ref, lse_ref):
        n = pl.program_id(0)
        head = _head_of_lane((ATTN_BLOCK, GROUP_W))
        for j in range(ATTN_QB):
            sl = slice(j * ATTN_BLOCK, (j + 1) * ATTN_BLOCK)
            first = (((n * ATTN_QB + j) % bps) == 0).astype(jnp.int32)
            qs = _stack_heads(q_ref[sl, :], head)
            sc = _dot_nt(qs, _keys(kp_ref, k_ref, j)) * ATTN_SCALE + bias_ref[first]
            m = jnp.max(sc, axis=-1, keepdims=True)
            pr = jnp.exp(sc - m)
            l = jnp.sum(pr, axis=-1, keepdims=True)
            oh = _dot(pr.astype(BF16), _keys(vp_ref, v_ref, j)) / l
            o_ref[sl, :] = _unstack_heads(oh, head)
            lse_ref[sl, :] = _unstack_heads(jnp.broadcast_to(m + jnp.log(l), oh.shape), head)

    cur, prev, _ = _attn_specs(n_blocks)
    bias = _attn_bias(dil)
    out = jax.ShapeDtypeStruct((s, GROUP_W), F32)
    return pl.pallas_call(
        body, name=f"attn_fwd_d{dil}", grid=(n_blocks // ATTN_QB,),
        in_specs=[cur(0), prev(1), cur(1), prev(2), cur(2), pl.BlockSpec(bias.shape, lambda n: (0, 0, 0))],
        out_specs=[cur(0), cur(0)],
        out_shape=[out, out],
        compiler_params=_params(("parallel",)),
    )(qkv, qkv, qkv, qkv, qkv, bias)


def attn_bwd(qkv, do, lse, dlt, dil):
    s = qkv.shape[0]
    n_blocks = s // ATTN_BLOCK
    bps = n_blocks // dil
    rows = ATTN_QB * ATTN_BLOCK

    def body(q_ref, qn_ref, kp_ref, k_ref, vp_ref, v_ref, do_ref, don_ref, lse_ref, lsen_ref, dl_ref, dln_ref,
             bias_ref, out_ref, dk_acc, dv_acc):
        n = pl.program_id(0)
        head = _head_of_lane((ATTN_BLOCK, GROUP_W))
        dk_acc[...] = jnp.zeros_like(dk_acc)
        dv_acc[...] = jnp.zeros_like(dv_acc)

        def pair(qj, doj, lsej, dlj, kk, vv, bias, keep):
            qs = _stack_heads(qj, head)
            dos = _stack_heads(doj, head)
            sc = _dot_nt(qs, kk) * ATTN_SCALE + bias
            if keep is None:
                pr = jnp.exp(sc - _head_column(lsej))
            else:
                pr = jnp.exp(jnp.minimum(sc - _head_column(lsej), 0.0)) * keep
            dp = _dot_nt(dos, vv)
            ds = (pr * (dp - _head_column(dlj)) * ATTN_SCALE).astype(BF16)
            return ds, _dot_tn(ds, qs), _dot_tn(pr.astype(BF16), dos)

        for j in range(ATTN_QB):
            sl = slice(j * ATTN_BLOCK, (j + 1) * ATTN_BLOCK)
            first = (((n * ATTN_QB + j) % bps) == 0).astype(jnp.int32)
            kk = _keys(kp_ref, k_ref, j)
            ds, dks, dvs = pair(q_ref[sl, :], do_ref[sl, :], lse_ref[sl, :], dl_ref[sl, :],
                                kk, _keys(vp_ref, v_ref, j), bias_ref[first], None)
            out_ref[sl, 0:GROUP_W] = _unstack_heads(_dot(ds, kk), head)
            acc = slice(j * ATTN_BLOCK, (j + 2) * ATTN_BLOCK)
            dk_acc[acc, :] += dks
            dv_acc[acc, :] += dvs

        nxt = n * ATTN_QB + ATTN_QB
        valid = ((nxt < n_blocks) & ((nxt % bps) != 0)).astype(F32)
        last = slice((ATTN_QB - 1) * ATTN_BLOCK, ATTN_QB * ATTN_BLOCK)
        _, dks, dvs = pair(qn_ref[...], don_ref[...], lsen_ref[...], dln_ref[...], k_ref[last, :], v_ref[last, :],
                           bias_ref[0][:, :ATTN_BLOCK], valid)
        acc = slice(ATTN_QB * ATTN_BLOCK, (ATTN_QB + 1) * ATTN_BLOCK)
        dk_acc[acc, :] += dks
        dv_acc[acc, :] += dvs
        out_ref[:, GROUP_W:2 * GROUP_W] = dk_acc[ATTN_BLOCK:, :]
        out_ref[:, 2 * GROUP_W:3 * GROUP_W] = dv_acc[ATTN_BLOCK:, :]

    cur, prev, nxt = _attn_specs(n_blocks)
    bias = _attn_bias(dil)
    return pl.pallas_call(
        body, name=f"attn_bwd_d{dil}", grid=(n_blocks // ATTN_QB,),
        in_specs=[cur(0), nxt(0), prev(1), cur(1), prev(2), cur(2), cur(0), nxt(0), cur(0), nxt(0), cur(0), nxt(0),
                  pl.BlockSpec(bias.shape, lambda n: (0, 0, 0))],
        out_specs=pl.BlockSpec((rows, 3 * GROUP_W), lambda n: (n, 0)),
        out_shape=jax.ShapeDtypeStruct((s, 3 * GROUP_W), F32),
        scratch_shapes=[pltpu.VMEM(((ATTN_QB + 1) * ATTN_BLOCK, GROUP_W), F32),
                        pltpu.VMEM(((ATTN_QB + 1) * ATTN_BLOCK, GROUP_W), F32)],
        compiler_params=_params(("parallel",)),
    )(qkv, qkv, qkv, qkv, qkv, qkv, do, do, lse, lse, dlt, dlt, bias)


def _zcol(c):
    return pl.BlockSpec((TILE_MIX, GROUP_W), lambda i, c=c: (i, c))


def _zhalo(c):
    per = TILE_MIX // SUBLANES
    return pl.BlockSpec((SUBLANES, GROUP_W), lambda i, c=c: (jnp.maximum(i * per - 1, 0), c))


def _full(shape):
    return pl.BlockSpec(shape, lambda i: tuple(0 for _ in shape))


def _lru_gates(xb, wa_ref, wx_ref, ba, bx, lam):
    xbb = xb.astype(BF16)
    r = jax.nn.sigmoid(_dot(xbb, wa_ref[...]) + ba)
    ig = jax.nn.sigmoid(_dot(xbb, wx_ref[...]) + bx)
    nl = -lam
    sp = jnp.maximum(nl, 0.0) + jnp.log1p(jnp.exp(-jnp.abs(nl)))
    log_a = (-RG_C * r) * sp
    a = jnp.exp(log_a)
    mult = jnp.sqrt(_neg_expm1(2.0 * log_a))
    return r, ig, sp, a, mult


def _gmlp_spatial(ws_ref, vvb, head):
    outs = []
    for j in range(vvb.shape[0] // GMLP_CHUNK):
        blk = vvb[j * GMLP_CHUNK:(j + 1) * GMLP_CHUNK, :]
        acc = jnp.zeros((GMLP_CHUNK, GROUP_W), F32)
        for h in range(N_HEADS):
            acc = jnp.where(head[:GMLP_CHUNK] == h, _dot(ws_ref[h], blk), acc)
        outs.append(acc)
    return jnp.concatenate(outs, axis=0)


def mix_fwd(z, attn, wts):
    s = z.shape[0]
    d4, d16 = PATTERN_DILS[1], PATTERN_DILS[2]

    def body(ax_ref, ab_ref, ac_ref, ag_ref, rx_ref, rg_ref, cu_ref, cv_ref, cg_ref, dg_ref,
             axh_ref, ach_ref, rxh_ref, o1_ref, l1_ref, o4_ref, l4_ref, o16_ref, l16_ref,
             caw_ref, crw_ref, crb_ref, wa_ref, wx_ref, ba_ref, bx_ref, lam_ref, gng_ref, ws_ref, bs_ref,
             y_ref, hl_ref, o_ref, lse_ref, lse4_ref, lse16_ref, carry, *stage):
        st_a, st_b, st_c, st_d, st_e = (stage[2 * k:2 * k + 2] for k in range(5))
        i = pl.program_id(0)

        @pl.when(i == 0)
        def _():
            carry[...] = jnp.zeros_like(carry)

        nz = (i > 0).astype(F32)
        head = _head_of_lane((TILE_MIX, GROUP_W))

        pa = ac_ref[...] * ax_ref[...]
        pah = ach_ref[...] * axh_ref[...] * nz
        cv = caw_ref[2:3, :] * pa + caw_ref[1:2, :] * _shift_down(pa, pah, 1) + caw_ref[0:1, :] * _shift_down(pa, pah, 2)
        y_ref[:, 0:GROUP_W] = (ab_ref[...] * cv * _silu(ag_ref[...])).astype(BF16)

        rx = rx_ref[...]
        rxh = rxh_ref[...] * nz
        xb = (crw_ref[3:4, :] * rx + crw_ref[2:3, :] * _shift_down(rx, rxh, 1) + crw_ref[1:2, :] * _shift_down(rx, rxh, 2)
              + crw_ref[0:1, :] * _shift_down(rx, rxh, 3) + crb_ref[...])
        _, ig, _, a, mult = _lru_gates(xb, wa_ref, wx_ref, ba_ref[...], bx_ref[...], lam_ref[...])
        ca, cb = _scan_causal(a, mult * (ig * xb))
        hl = cb + ca * carry[SUBLANES - 1:SUBLANES, :]
        hl_ref[...] = hl
        carry[...] = hl[TILE_MIX - SUBLANES:, :]
        y_ref[:, GROUP_W:2 * GROUP_W] = (hl * _silu(rg_ref[...])).astype(BF16)

        u = _gelu(cu_ref[...])
        gv = _gelu(cv_ref[...])
        rs = lax.rsqrt(jnp.mean(gv * gv, axis=-1, keepdims=True) + NORM_EPS)
        vvb = (gv * rs * gng_ref[...]).astype(BF16)
        sp = _gmlp_spatial(ws_ref, vvb, head) + jnp.concatenate([bs_ref[...]] * (TILE_MIX // GMLP_CHUNK), axis=0)
        y_ref[:, 2 * GROUP_W:3 * GROUP_W] = (u * sp * _silu(cg_ref[...])).astype(BF16)

        ops = (o1_ref[...], _interleave_load(o4_ref, d4, st_a), _interleave_load(o16_ref, d16, st_b))
        lps = (l1_ref[...], _interleave_load(l4_ref, d4, st_c), _interleave_load(l16_ref, d16, st_d))
        m = jnp.maximum(jnp.maximum(lps[0], lps[1]), lps[2])
        zsum = jnp.zeros_like(m)
        o = jnp.zeros_like(m)
        for op, lp in zip(ops, lps):
            w = jnp.exp(lp - m)
            zsum = zsum + w
            o = o + w * op
        o = o / zsum
        lse = m + jnp.log(zsum)
        o_ref[...] = o
        lse_ref[...] = lse
        _deinterleave_store(lse, st_e, ((d4, lse4_ref), (d16, lse16_ref)))
        y_ref[:, 3 * GROUP_W:4 * GROUP_W] = (o * _silu(dg_ref[...])).astype(BF16)

    row = pl.BlockSpec((TILE_MIX, GROUP_W), lambda i: (i, 0))
    r4 = _residue_block(d4, TILE_MIX, GROUP_W)
    r16 = _residue_block(d16, TILE_MIX, GROUP_W)
    names = ("caw", "crw", "crb", "wa", "wx", "ba", "bx", "lam", "gng", "ws", "bs")
    in_specs = ([_zcol(c) for c in (C_AX, C_AB, C_AC, C_AG, C_RX, C_RG, C_CU, C_CV, C_CG, C_DG)]
                + [_zhalo(C_AX), _zhalo(C_AC), _zhalo(C_RX), row, row, r4, r4, r16, r16]
                + [_full(wts[k].shape) for k in names])
    return pl.pallas_call(
        body, name="mix_fwd", grid=(s // TILE_MIX,),
        in_specs=in_specs,
        out_specs=[pl.BlockSpec((TILE_MIX, D_MIX), lambda i: (i, 0)), row, row, row, r4, r16],
        out_shape=([jax.ShapeDtypeStruct((s, D_MIX), BF16)] + [jax.ShapeDtypeStruct((s, GROUP_W), F32)] * 3
                   + [_by_residue(s, d4, GROUP_W, F32), _by_residue(s, d16, GROUP_W, F32)]),
        scratch_shapes=[pltpu.VMEM((SUBLANES, GROUP_W), F32)] + _stage_scratch(TILE_MIX, GROUP_W, 5),
        compiler_params=_params(("arbitrary",)),
    )(*([z] * 13), *[a for pair in attn for a in pair], *[wts[k] for k in names])


def mix_bwd(dy, z, hl, dqkv, ddg, wts):
    s = z.shape[0]
    d4, d16 = PATTERN_DILS[1], PATTERN_DILS[2]
    n_tiles = s // TILE_MIX

    def body(dya_ref, dyb_ref, dyc_ref, ax_ref, ab_ref, ac_ref, ag_ref, rx_ref, rg_ref, cu_ref, cv_ref, cg_ref,
             axh_ref, ach_ref, rxh_ref, hl_ref, hlh_ref, dqkv1_ref, dqkv4_ref, dqkv16_ref, ddg_ref,
             caw_ref, crw_ref, crb_ref, wa_ref, wx_ref, ba_ref, bx_ref, lam_ref, gng_ref, ws_ref, wst_ref, bs_ref,
             dz_ref, ga_ref, gr_ref, gn_ref, gwa_ref, gwx_ref, gws_ref, gbs_ref,
             c_dcv, c_g, c_a, c_dxb, *stage):
        st_a, st_b = stage[:len(stage) // 2], stage[len(stage) // 2:]
        step = pl.program_id(0)
        i = n_tiles - 1 - step

        @pl.when(step == 0)
        def _():
            for r in (c_dcv, c_g, c_a, c_dxb, ga_ref, gr_ref, gn_ref, gwa_ref, gwx_ref, gws_ref, gbs_ref):
                r[...] = jnp.zeros_like(r)

        nz = (i > 0).astype(F32)
        head = _head_of_lane((TILE_MIX, GROUP_W))
        shp8 = (SUBLANES, GROUP_W)
        colsum = lambda v: jnp.sum(v, axis=0, keepdims=True)

        ax, ab, ac, ag = ax_ref[...], ab_ref[...], ac_ref[...], ag_ref[...]
        dya = dya_ref[...]
        pa = ac * ax
        pah = ach_ref[...] * axh_ref[...] * nz
        pa1 = _shift_down(pa, pah, 1)
        pa2 = _shift_down(pa, pah, 2)
        cv = caw_ref[2:3, :] * pa + caw_ref[1:2, :] * pa1 + caw_ref[0:1, :] * pa2
        sg = _silu(ag)
        dz_ref[:, C_AB * GROUP_W:(C_AB + 1) * GROUP_W] = (dya * cv * sg).astype(BF16)
        dz_ref[:, C_AG * GROUP_W:(C_AG + 1) * GROUP_W] = (dya * ab * cv * _dsilu(ag)).astype(BF16)
        dcv = dya * ab * sg
        nxt = c_dcv[...]
        dpa = caw_ref[2:3, :] * dcv + caw_ref[1:2, :] * _shift_up(dcv, nxt, 1) + caw_ref[0:1, :] * _shift_up(dcv, nxt, 2)
        c_dcv[...] = dcv[:SUBLANES, :]
        dz_ref[:, C_AC * GROUP_W:(C_AC + 1) * GROUP_W] = (dpa * ax).astype(BF16)
        dz_ref[:, C_AX * GROUP_W:(C_AX + 1) * GROUP_W] = (dpa * ac).astype(BF16)
        ga_ref[...] += (_put_row(shp8, 2, colsum(dcv * pa)) + _put_row(shp8, 1, colsum(dcv * pa1))
                        + _put_row(shp8, 0, colsum(dcv * pa2)))

        rx, rg = rx_ref[...], rg_ref[...]
        dyb = dyb_ref[...]
        rxh = rxh_ref[...] * nz
        rx1, rx2, rx3 = _shift_down(rx, rxh, 1), _shift_down(rx, rxh, 2), _shift_down(rx, rxh, 3)
        xb = crw_ref[3:4, :] * rx + crw_ref[2:3, :] * rx1 + crw_ref[1:2, :] * rx2 + crw_ref[0:1, :] * rx3 + crb_ref[...]
        lam = lam_ref[...]
        r, ig, sp, a, mult = _lru_gates(xb, wa_ref, wx_ref, ba_ref[...], bx_ref[...], lam)
        hl = hl_ref[...]
        hprev = _shift_down(hl, hlh_ref[...] * nz, 1)
        dz_ref[:, C_RG * GROUP_W:(C_RG + 1) * GROUP_W] = (dyb * hl * _dsilu(rg)).astype(BF16)
        dh = dyb * _silu(rg)
        a_next = _shift_up(a, c_a[...], 1)
        ca, cb = _scan_anticausal(a_next, dh)
        g = cb + ca * c_g[0:1, :]
        c_g[...] = g[:SUBLANES, :]
        c_a[...] = a[:SUBLANES, :]
        u = ig * xb
        da = g * hprev
        dmult = g * u
        du = g * mult
        dlog_a = da * a - dmult * (a * a) / mult
        dr = dlog_a * (-RG_C * sp)
        dga = dr * r * (1.0 - r)
        dgx = (du * xb) * ig * (1.0 - ig)
        dgab, dgxb = dga.astype(BF16), dgx.astype(BF16)
        dxb = du * ig + _dot_nt(dgab, wa_ref[...]) + _dot_nt(dgxb, wx_ref[...])
        xbb = xb.astype(BF16)
        gwa_ref[...] += _dot_tn(xbb, dgab)
        gwx_ref[...] += _dot_tn(xbb, dgxb)
        nxt = c_dxb[...]
        drx = (crw_ref[3:4, :] * dxb + crw_ref[2:3, :] * _shift_up(dxb, nxt, 1) + crw_ref[1:2, :] * _shift_up(dxb, nxt, 2)
               + crw_ref[0:1, :] * _shift_up(dxb, nxt, 3))
        c_dxb[...] = dxb[:SUBLANES, :]
        dz_ref[:, C_RX * GROUP_W:(C_RX + 1) * GROUP_W] = drx.astype(BF16)
        dlam = colsum(dlog_a * (-RG_C * r)) * (-jax.nn.sigmoid(-lam))
        gr_ref[...] += (_put_row(shp8, 3, colsum(dxb * rx)) + _put_row(shp8, 2, colsum(dxb * rx1))
                        + _put_row(shp8, 1, colsum(dxb * rx2)) + _put_row(shp8, 0, colsum(dxb * rx3))
                        + _put_row(shp8, 4, colsum(dxb)) + _put_row(shp8, 5, colsum(dga))
                        + _put_row(shp8, 6, colsum(dgx)) + _put_row(shp8, 7, dlam))

        cu, cvv, cg = cu_ref[...], cv_ref[...], cg_ref[...]
        dyc = dyc_ref[...]
        u_c, du_c = _gelu_and_grad(cu)
        gv, dgv_c = _gelu_and_grad(cvv)
        rs = lax.rsqrt(jnp.mean(gv * gv, axis=-1, keepdims=True) + NORM_EPS)
        vh = gv * rs
        gng = gng_ref[...]
        vvb = (vh * gng).astype(BF16)
        spat = _gmlp_spatial(ws_ref, vvb, head) + jnp.concatenate([bs_ref[...]] * (TILE_MIX // GMLP_CHUNK), axis=0)
        sgc = _silu(cg)
        dz_ref[:, C_CU * GROUP_W:(C_CU + 1) * GROUP_W] = (dyc * spat * sgc * du_c).astype(BF16)
        dz_ref[:, C_CG * GROUP_W:(C_CG + 1) * GROUP_W] = (dyc * u_c * spat * _dsilu(cg)).astype(BF16)
        dsp = dyc * u_c * sgc
        dspb = dsp.astype(BF16)
        tril = (lax.broadcasted_iota(jnp.int32, (GMLP_CHUNK, GMLP_CHUNK), 0)
                >= lax.broadcasted_iota(jnp.int32, (GMLP_CHUNK, GMLP_CHUNK), 1))
        head_c = head[:GMLP_CHUNK]
        dvv_parts = []
        gbs = jnp.zeros((GMLP_CHUNK, GROUP_W), F32)
        for j in range(TILE_MIX // GMLP_CHUNK):
            sl = slice(j * GMLP_CHUNK, (j + 1) * GMLP_CHUNK)
            dblk = dspb[sl, :]
            vblk = vvb[sl, :]
            gbs = gbs + dsp[sl, :]
            acc = jnp.zeros((GMLP_CHUNK, GROUP_W), F32)
            for h in range(N_HEADS):
                acc = jnp.where(head_c == h, _dot(wst_ref[h], dblk), acc)
                dm = jnp.where(head_c == h, dblk, jnp.zeros_like(dblk))
                gws_ref[h] += jnp.where(tril, _dot_nt(dm, vblk), 0.0)
            dvv_parts.append(acc)
        gbs_ref[...] += gbs
        dvv = jnp.concatenate(dvv_parts, axis=0)
        gn_ref[...] += _put_row(shp8, 0, colsum(dvv * vh))
        dvh = dvv * gng
        dgv = rs * (dvh - vh * jnp.mean(dvh * vh, axis=-1, keepdims=True))
        dz_ref[:, C_CV * GROUP_W:(C_CV + 1) * GROUP_W] = (dgv * dgv_c).astype(BF16)

        dsum = dqkv1_ref[...] + _interleave_load(dqkv4_ref, d4, st_a) + _interleave_load(dqkv16_ref, d16, st_b)
        dz_ref[:, C_DQ * GROUP_W:(C_DV + 1) * GROUP_W] = dsum.astype(BF16)
        dz_ref[:, C_DG * GROUP_W:(C_DG + 1) * GROUP_W] = ddg_ref[...].astype(BF16)

    per = TILE_MIX // SUBLANES
    qkv_w = 3 * GROUP_W
    rev = lambda c: pl.BlockSpec((TILE_MIX, GROUP_W), lambda t, c=c: (n_tiles - 1 - t, c))
    revh = lambda c: pl.BlockSpec((SUBLANES, GROUP_W),
                                  lambda t, c=c: (jnp.maximum((n_tiles - 1 - t) * per - 1, 0), c))
    revr = lambda dil: pl.BlockSpec((dil, TILE_MIX // dil, qkv_w), lambda t: (0, n_tiles - 1 - t, 0))
    names = ("caw", "crw", "crb", "wa", "wx", "ba", "bx", "lam", "gng", "ws", "wst", "bs")
    in_specs = ([rev(0), rev(1), rev(2)]
                + [rev(c) for c in (C_AX, C_AB, C_AC, C_AG, C_RX, C_RG, C_CU, C_CV, C_CG)]
                + [revh(C_AX), revh(C_AC), revh(C_RX), rev(0), revh(0),
                   pl.BlockSpec((TILE_MIX, qkv_w), lambda t: (n_tiles - 1 - t, 0)), revr(d4), revr(d16), rev(0)]
                + [_full(wts[k].shape) for k in names])
    small = jax.ShapeDtypeStruct((SUBLANES, GROUP_W), F32)
    sq = jax.ShapeDtypeStruct((GROUP_W, GROUP_W), F32)
    out_shape = [jax.ShapeDtypeStruct((s, D_IN), BF16), small, small, small, sq, sq,
                 jax.ShapeDtypeStruct((N_HEADS, GMLP_CHUNK, GMLP_CHUNK), F32),
                 jax.ShapeDtypeStruct((GMLP_CHUNK, GROUP_W), F32)]
    out_specs = ([pl.BlockSpec((TILE_MIX, D_IN), lambda t: (n_tiles - 1 - t, 0))]
                 + [_full(o.shape) for o in out_shape[1:]])
    return pl.pallas_call(
        body, name="mix_bwd", grid=(n_tiles,),
        in_specs=in_specs, out_specs=out_specs, out_shape=out_shape,
        scratch_shapes=[pltpu.VMEM((SUBLANES, GROUP_W), F32)] * 4 + _stage_scratch(TILE_MIX, qkv_w, 2),
        compiler_params=_params(("arbitrary",)),
    )(dy, dy, dy, *([z] * 12), hl, hl, *dqkv, ddg, *[wts[k] for k in names])


def _block_diag(w):
    eye = jnp.eye(N_HEADS, dtype=w.dtype)
    return (w[:, :, None, :] * eye[:, None, :, None]).reshape(GROUP_W, GROUP_W)


def _diag_blocks(g):
    g4 = g.reshape(N_HEADS, HEAD_DIM, N_HEADS, HEAD_DIM)
    return jnp.stack([g4[h, :, h, :] for h in range(N_HEADS)])


def _layer_weights(p, l):
    tril = jnp.tril(jnp.ones((GMLP_CHUNK, GMLP_CHUNK), dtype=bool))
    ws = jnp.where(tril[None], p["gmlp_ws"][l], 0.0).astype(BF16)
    row = lambda a: a[l][None, :]
    return dict(
        caw=p["conv_a_w"][l], crw=p["conv_r_w"][l], crb=row(p["conv_r_b"]),
        wa=_block_diag(p["lru_wa"][l]).astype(BF16), wx=_block_diag(p["lru_wx"][l]).astype(BF16),
        ba=row(p["lru_ba"]), bx=row(p["lru_bx"]), lam=row(p["lru_lambda"]), gng=row(p["gmlp_norm_g"]),
        ws=ws, wst=jnp.transpose(ws, (0, 2, 1)),
        bs=jnp.repeat(jnp.transpose(p["gmlp_bs"][l]), HEAD_DIM, axis=1))


def _flat(a):
    return a.reshape(a.shape[0] * a.shape[1], a.shape[2])


def _split(a, dil):
    return a.reshape(dil, a.shape[0] // dil, a.shape[1])


def local_step(x, tgt, final_g, depth, chip, layer_weights, projections_done):
    saved = []
    for l in range(depth):
        gain, w_in_l, w_out_l, wts = layer_weights(l, x)
        z, h, *qkvs = in_fwd(x, gain, w_in_l)
        qkvs = [_flat(q) if q.ndim == 3 else q for q in qkvs]
        attn = []
        for q, d in zip(qkvs, PATTERN_DILS):
            o_p, lse_p = attn_fwd(q, d)
            attn.append((o_p, lse_p) if d == 1 else (_split(o_p, d), _split(lse_p, d)))
        y, hl, o, lse, lse4, lse16 = mix_fwd(z, attn, wts)
        saved.append(dict(x=x, z=z, h=h, y=y, hl=hl, o=o, qkvs=qkvs, lses=(lse, _flat(lse4), _flat(lse16)), wts=wts,
                          gain=gain, w_in=w_in_l, w_out=w_out_l))
        x = out_fwd(y, w_out_l, x)

    loss, dx, dfg = loss_head(x, final_g[None, :], tgt)
    grads = {k: [None] * depth for k in
             ("norm_g", "conv_a_w", "conv_r_w", "conv_r_b", "lru_wa", "lru_ba", "lru_wx", "lru_bx",
              "lru_lambda", "gmlp_norm_g", "gmlp_ws", "gmlp_bs")}
    zero = None
    for l in reversed(range(depth)):
        sv = saved[l]
        dy, ddg, do1, do4, do16, dl1, dl4, dl16 = out_bwd(dx, sv["w_out"], sv["z"], sv["o"])
        g_w_out = matmul_tn(sv["y"], dx, 1)
        dqkv = []
        for q, do, lse, dl, d in zip(sv["qkvs"], (do1, _flat(do4), _flat(do16)), sv["lses"],
                                     (dl1, _flat(dl4), _flat(dl16)), PATTERN_DILS):
            g = attn_bwd(q, do, lse, dl, d)
            dqkv.append(g if d == 1 else _split(g, d))
        dz, ga, gr, gn, gwa, gwx, gws, gbs = mix_bwd(dy, sv["z"], sv["hl"], dqkv, ddg, sv["wts"])
        gain = sv["gain"] if zero is None else sv["gain"] + zero
        zero = projections_done(l, *grad_w_in(sv["h"], dz, chip), g_w_out)
        if l == 0 and zero is not None:
            gain = gain + zero
        dx, dgn = in_bwd(dz, sv["w_in"], sv["x"], gain, dx)
        grads["norm_g"][l] = dgn[0]
        grads["conv_a_w"][l] = ga[:3]
        grads["conv_r_w"][l] = gr[:4]
        grads["conv_r_b"][l] = gr[4]
        grads["lru_ba"][l] = gr[5]
        grads["lru_bx"][l] = gr[6]
        grads["lru_lambda"][l] = gr[7]
        grads["gmlp_norm_g"][l] = gn[0]
        grads["lru_wa"][l] = _diag_blocks(gwa)
        grads["lru_wx"][l] = _diag_blocks(gwx)
        grads["gmlp_ws"][l] = gws
        grads["gmlp_bs"][l] = jnp.transpose(gbs.reshape(GMLP_CHUNK, N_HEADS, HEAD_DIM).sum(-1))
    grads = {k: jnp.stack(v) for k, v in grads.items()}
    grads["final_g"] = dfg[0]
    return loss, dx, grads


MESH = pl.DeviceIdType.MESH
N_CHIPS = 4
N_DEV = 8
ANY = pl.BlockSpec(memory_space=pl.ANY)


def _place():
    x, y, c = lax.axis_index("x"), lax.axis_index("y"), lax.axis_index("c")
    chips = [(1 - x, y), (x, 1 - y), (1 - x, 1 - y)]
    return x, y, c, chips


def _remote(src, dst, ssem, rsem, to):
    return pltpu.make_async_remote_copy(src_ref=src, dst_ref=dst, send_sem=ssem, recv_sem=rsem,
                                        device_id=to, device_id_type=MESH)


HBM = pl.BlockSpec(memory_space=pltpu.HBM)
SEM = pl.BlockSpec(memory_space=pltpu.SEMAPHORE)
DATAFLOW = pltpu.SideEffectType.DATAFLOW_SIDE_EFFECTING
GATHER, SCATTER = "gather", "scatter"


def _chip_copies(mode, src_refs, land_refs, ssem, rsem):
    x, y, c, chips = _place()
    me = 2 * x + y
    n = len(src_refs)
    copies = []
    for k, (cx, cy) in enumerate(chips):
        for a in range(n):
            if mode == GATHER:
                src, dst = src_refs[a], land_refs[a].at[me]
            else:
                src, dst = src_refs[a].at[2 * cx + cy], land_refs[a].at[k]
            copies.append(_remote(src, dst, ssem.at[n * k + a], rsem.at[n * k + a], (cx, cy, c)))
    return copies


def exchange_start(mode, srcs, after, name):
    n = len(srcs)
    if mode == GATHER:
        lands = [lax.empty((N_CHIPS,) + s.shape, s.dtype) for s in srcs]
    else:
        lands = [lax.empty((N_CHIPS - 1,) + s.shape[1:], s.dtype) for s in srcs]
    extra = [] if after is None else [after]

    def body(*refs):
        src_refs, land_refs = refs[:n], refs[n:2 * n]
        ssem, rsem = refs[2 * n + len(extra)], refs[2 * n + len(extra) + 1]
        token = refs[-1]
        for cp in _chip_copies(mode, src_refs, land_refs, ssem, rsem):
            cp.start()
        token[...] = jnp.zeros_like(token)

    arrays = list(srcs) + lands
    return pl.pallas_call(
        body, name=name,
        out_shape=(pltpu.SemaphoreType.DMA((3 * n,)), pltpu.SemaphoreType.DMA((3 * n,)),
                   *[pltpu.HBM(a.shape, a.dtype) for a in arrays], jax.ShapeDtypeStruct((SUBLANES, LANES), F32)),
        in_specs=[HBM] * (2 * n) + [ANY] * len(extra),
        out_specs=(SEM, SEM, *[HBM] * (2 * n), pl.BlockSpec(memory_space=pltpu.VMEM)),
        input_output_aliases={i: 2 + i for i in range(2 * n)},
        compiler_params=pltpu.CompilerParams(has_side_effects=DATAFLOW),
    )(*[pltpu.with_memory_space_constraint(a, pltpu.HBM) for a in arrays], *extra)


def exchange_wait(mode, started, after, name):
    ssem, rsem, *thru, _ = started
    n = len(thru) // 2

    def body(*refs):
        src_refs, land_refs = refs[:n], refs[n:2 * n]
        ssem_ref, rsem_ref = refs[2 * n], refs[2 * n + 1]
        for cp in _chip_copies(mode, src_refs, land_refs, ssem_ref, rsem_ref):
            cp.wait_send()
            cp.wait_recv()

    outs = pl.pallas_call(
        body, name=name,
        out_shape=[pltpu.HBM(a.shape, a.dtype) for a in thru],
        in_specs=[HBM] * (2 * n) + [SEM, SEM, ANY],
        out_specs=[HBM] * (2 * n),
        input_output_aliases={i: i for i in range(2 * n)},
        compiler_params=pltpu.CompilerParams(has_side_effects=DATAFLOW),
    )(*thru, ssem, rsem, after)
    return outs[n:]


def sibling_exchange(p1, p2):
    def body(p1_ref, p2_ref, q1_ref, q2_ref, ssem, rsem):
        x, y, c, _ = _place()
        copies = [_remote(p_ref, q_ref, ssem.at[a], rsem.at[a], (x, y, 1 - c))
                  for a, (p_ref, q_ref) in enumerate(((p1_ref, q1_ref), (p2_ref, q2_ref)))]
        for cp in copies:
            cp.start()
        for cp in copies:
            cp.wait()

    return pl.pallas_call(
        body, name="sibling_exchange",
        in_specs=[ANY, ANY], out_specs=[ANY, ANY],
        out_shape=[jax.ShapeDtypeStruct(p.shape, p.dtype) for p in (p1, p2)],
        scratch_shapes=[pltpu.SemaphoreType.DMA((2,)), pltpu.SemaphoreType.DMA((2,))],
    )(p1, p2)


def gather_all(v):
    m_per, n = v.shape

    def body(x_ref, out_ref, send_sems, recv_sems, local_sem):
        x, y, c, chips = _place()
        me, sibling = (x, y, c), (x, y, 1 - c)

        def rows(px, py, pc):
            return out_ref.at[pl.ds((4 * px + 2 * py + pc) * m_per, m_per), :]

        def copy(k, block, to, src=None):
            return _remote(rows(*block) if src is None else src, rows(*block), send_sems.at[k], recv_sems.at[k], to)

        mine = pltpu.make_async_copy(x_ref, rows(*me), local_sem)
        mine.start()
        first = [copy(0, me, sibling, src=x_ref)]
        first += [copy(1 + j, me, (*chip, c), src=x_ref) for j, chip in enumerate(chips)]
        for cp in first:
            cp.start()
        passed = [copy(4 + j, (*chip, c), sibling) for j, chip in enumerate(chips)]
        for j, chip in enumerate(chips):
            copy(1 + j, (*chip, c), me).wait_recv()
            passed[j].start()
        copy(0, sibling, me).wait_recv()
        for j, chip in enumerate(chips):
            copy(4 + j, (*chip, 1 - c), me).wait_recv()
        for cp in first + passed:
            cp.wait_send()
        mine.wait()

    return pl.pallas_call(
        body, name="gather_all",
        out_shape=jax.ShapeDtypeStruct((N_DEV * m_per, n), v.dtype),
        in_specs=[pl.BlockSpec(memory_space=pltpu.VMEM)],
        out_specs=pl.BlockSpec(memory_space=pltpu.VMEM),
        scratch_shapes=[pltpu.SemaphoreType.DMA((7,)), pltpu.SemaphoreType.DMA((7,)), pltpu.SemaphoreType.DMA],
        compiler_params=pltpu.CompilerParams(vmem_limit_bytes=VMEM_LIMIT),
    )(v)


TILE_ROWS = 256


def sum_partials(own, parts):
    r, c = own.shape
    k = parts.shape[0]

    def body(o_ref, p_ref, out_ref):
        acc = o_ref[...]
        for i in range(k):
            acc = acc + p_ref[i].astype(F32)
        out_ref[...] = acc

    return pl.pallas_call(
        body, name="sum_partials", grid=(r // TILE_ROWS,),
        in_specs=[pl.BlockSpec((TILE_ROWS, c), lambda i: (i, 0)), pl.BlockSpec((k, TILE_ROWS, c), lambda i: (0, i, 0))],
        out_specs=pl.BlockSpec((TILE_ROWS, c), lambda i: (i, 0)),
        out_shape=jax.ShapeDtypeStruct((r, c), F32),
        compiler_params=_params(("parallel",)),
    )(own, parts)


def adamw(w, ga, gb, m, v):
    n, r, c = w.shape
    tile = max(t for t in range(SUBLANES, TILE_ROWS + 1, SUBLANES) if r % t == 0)

    def body(w_ref, ga_ref, gb_ref, m_ref, v_ref, g_ref, d_ref, m2_ref, v2_ref):
        g = ga_ref[...] + gb_ref[...]
        g_ref[...] = g
        m2 = ADAM_B1 * m_ref[...] + (1.0 - ADAM_B1) * g
        v2 = ADAM_B2 * v_ref[...] + (1.0 - ADAM_B2) * (g * g)
        m2_ref[...] = m2
        v2_ref[...] = v2
        m_hat = m2 / (1.0 - ADAM_B1 ** ADAM_STEP)
        v_hat = v2 / (1.0 - ADAM_B2 ** ADAM_STEP)
        d_ref[...] = -ADAM_LR * (m_hat / (jnp.sqrt(v_hat) + ADAM_EPS) + ADAM_WD * w_ref[...])

    spec = pl.BlockSpec((1, tile, c), lambda j, i: (j, i, 0))
    return pl.pallas_call(
        body, name="adamw", grid=(n, r // tile),
        in_specs=[spec] * 5, out_specs=[spec] * 4,
        out_shape=[jax.ShapeDtypeStruct((n, r, c), F32)] * 4,
        compiler_params=_params(("parallel", "parallel")),
    )(w, ga, gb, m, v)


REPLICATED = ("norm_g", "conv_r_b", "lru_wa", "lru_ba", "lru_wx", "lru_bx", "lru_lambda", "gmlp_norm_g",
              "gmlp_ws", "gmlp_bs", "final_g")
CHIP_SHARDED_SMALL = ("conv_a_w", "conv_r_w")
PACK_LANES = 128


def _pack(arrays):
    flat = jnp.concatenate([a.reshape(-1) for a in arrays])
    pad = (-flat.shape[0]) % (TILE_ROWS * PACK_LANES)
    return jnp.pad(flat, (0, pad)).reshape(-1, PACK_LANES)


def _unpack(packed, shapes):
    flat = packed.reshape(-1)
    out, off = [], 0
    for shp in shapes:
        n = math.prod(shp)
        out.append(flat[off:off + n].reshape(shp))
        off += n
    return out


def kernel(x, norm_g, w_in, conv_a_w, conv_r_w, conv_r_b, lru_wa, lru_ba, lru_wx, lru_bx, lru_lambda, gmlp_norm_g, gmlp_ws, gmlp_bs, w_out, final_g, loss_target, m_norm_g, m_w_in, m_conv_a_w, m_conv_r_w, m_conv_r_b, m_lru_wa, m_lru_ba, m_lru_wx, m_lru_bx, m_lru_lambda, m_gmlp_norm_g, m_gmlp_ws, m_gmlp_bs, m_w_out, m_final_g, v_norm_g, v_w_in, v_conv_a_w, v_conv_r_w, v_conv_r_b, v_lru_wa, v_lru_ba, v_lru_wx, v_lru_bx, v_lru_lambda, v_gmlp_norm_g, v_gmlp_ws, v_gmlp_bs, v_w_out, v_final_g):
    names = ("norm_g", "w_in", "conv_a_w", "conv_r_w", "conv_r_b", "lru_wa", "lru_ba", "lru_wx", "lru_bx",
             "lru_lambda", "gmlp_norm_g", "gmlp_ws", "gmlp_bs", "w_out", "final_g")
    w = dict(zip(names, (norm_g, w_in, conv_a_w, conv_r_w, conv_r_b, lru_wa, lru_ba, lru_wx, lru_bx, lru_lambda,
                         gmlp_norm_g, gmlp_ws, gmlp_bs, w_out, final_g)))
    m = dict(zip(names, (m_norm_g, m_w_in, m_conv_a_w, m_conv_r_w, m_conv_r_b, m_lru_wa, m_lru_ba, m_lru_wx, m_lru_bx,
                         m_lru_lambda, m_gmlp_norm_g, m_gmlp_ws, m_gmlp_bs, m_w_out, m_final_g)))
    v = dict(zip(names, (v_norm_g, v_w_in, v_conv_a_w, v_conv_r_w, v_conv_r_b, v_lru_wa, v_lru_ba, v_lru_wx, v_lru_bx,
                         v_lru_lambda, v_gmlp_norm_g, v_gmlp_ws, v_gmlp_bs, v_w_out, v_final_g)))
    depth, _, in_cols = w_in.shape
    out_rows = w_out.shape[1]
    conv_ch = conv_a_w.shape[2]
    chip = 2 * lax.axis_index("x") + lax.axis_index("y")

    taps = conv_a_w.shape[1] + conv_r_w.shape[1]
    w_in_t, m_w_in_t, v_w_in_t = (jnp.swapaxes(a, 1, 2) for a in (w_in, m_w_in, v_w_in))
    w_in_h, w_out_h = w_in_t.astype(BF16), w_out.astype(BF16)
    conv_own = jnp.concatenate([conv_a_w, conv_r_w], axis=1).reshape(depth * taps, conv_ch)
    gathers, token = [], None
    for l in range(depth):
        srcs = [w_in_h[l], w_out_h[l]] + ([conv_own] if l == 0 else [])
        gathers.append(exchange_start(GATHER, srcs, token, f"gather_start_{l}"))
        token = gathers[-1][-1]
    p = dict(w)

    def with_own(land, own):
        return lax.dynamic_update_slice(land, own[None], (chip,) + (0,) * own.ndim)

    def layer_weights(l, x_l):
        lands = exchange_wait(GATHER, gathers[l], x_l, f"gather_wait_{l}")
        w_in_l = with_own(lands[0], w_in_h[l]).reshape(D_IN, D_MODEL)
        w_out_l = with_own(lands[1], w_out_h[l]).reshape(D_MIX, D_MODEL)
        gain = norm_g[l][None, :]
        if l == 0:
            conv = with_own(lands[2], conv_own).reshape(N_CHIPS, depth, taps, conv_ch)
            conv = conv.transpose(1, 2, 0, 3).reshape(depth, taps, GROUP_W)
            p["conv_a_w"] = conv[:, :conv_a_w.shape[1]]
            p["conv_r_w"] = conv[:, conv_a_w.shape[1]:]
            gain = gain + token[0, 0]
        return gain, w_in_l, w_out_l, _layer_weights(p, l)

    scatters, owns = [None] * depth, [None] * depth

    def projections_done(l, g_w_in_by_chip, g_w_in_own, g_w_out):
        go = g_w_out.reshape(N_CHIPS, out_rows, D_MODEL)
        owns[l] = (g_w_in_own, lax.dynamic_index_in_dim(go, chip, axis=0, keepdims=False))
        scatters[l] = exchange_start(SCATTER, [g_w_in_by_chip, go.astype(BF16)], None, f"scatter_start_{l}")
        return scatters[l][-1][0, 0]

    loss8, dx, grads = local_step(x[0], loss_target[0], final_g, depth, chip.reshape(1), layer_weights,
                                  projections_done)
    loss = lax.psum(loss8[0, 0], ("x", "y", "c"))

    parts = [exchange_wait(SCATTER, scatters[l], dx, f"scatter_wait_{l}") for l in range(depth)]
    r1 = jnp.concatenate([parts[l][0] for l in range(depth)], axis=1)
    r2 = jnp.concatenate([parts[l][1] for l in range(depth)], axis=1)
    own1 = jnp.concatenate([owns[l][0] for l in range(depth)], axis=0)
    own2 = jnp.concatenate([owns[l][1] for l in range(depth)], axis=0)
    p1 = sum_partials(own1, r1)
    p2 = sum_partials(own2, r2)
    q1, q2 = sibling_exchange(p1, p2)
    res = {}
    res["w_in"] = [jnp.swapaxes(t, 1, 2) for t in
                   adamw(w_in_t, p1.reshape(w_in_t.shape), q1.reshape(w_in_t.shape), m_w_in_t, v_w_in_t)]
    res["w_out"] = adamw(w_out, p2.reshape(w_out.shape), q2.reshape(w_out.shape), m_w_out, v_w_out)

    small = REPLICATED + CHIP_SHARDED_SMALL
    packed = _pack([grads[k] for k in small])
    rows = packed.shape[0]
    allp = gather_all(packed).reshape(N_DEV, rows, PACK_LANES)
    total = sum_partials(allp[0], allp[1:])
    gs = dict(zip(small, _unpack(total, [grads[k].shape for k in small])))
    for k in CHIP_SHARDED_SMALL:
        gs[k] = lax.dynamic_slice_in_dim(gs[k], chip * conv_ch, conv_ch, axis=2)
    wp, gp, mp, vp = (_pack([d[k] for k in small])[None] for d in (w, gs, m, v))
    outs = adamw(wp, gp, jnp.zeros_like(gp), mp, vp)
    shapes = [w[k].shape for k in small]
    unpacked = [_unpack(o, shapes) for o in outs]
    for i, k in enumerate(small):
        res[k] = [u[i] for u in unpacked]

    return (loss, dx[None], *[res[k][0] for k in names], *[res[k][1] for k in names],
            *[res[k][2] for k in names], *[res[k][3] for k in names])
```

```python
import functools
import math

import jax
import jax.numpy as jnp
import numpy as np
from jax import lax
from jax.experimental import pallas as pl
from jax.experimental.pallas import tpu as pltpu

F32 = jnp.float32
BF16 = jnp.bfloat16

D_MODEL = 1024
GROUP_W = 256
N_HEADS = 4
HEAD_DIM = 64
N_CHUNKS = 13
D_IN = N_CHUNKS * GROUP_W
D_MIX = 4 * GROUP_W
NORM_EPS = 1e-6
RG_C = 8.0
GMLP_CHUNK = 128
ATTN_BLOCK = 128
PATTERN_DILS = (1, 4, 16)
N_PATTERNS = len(PATTERN_DILS)
ALIBI_SLOPES = tuple(2.0 ** (-8.0 * (h + 1) / N_HEADS) for h in range(N_HEADS))
ATTN_SCALE = 1.0 / math.sqrt(HEAD_DIM)
NEG_BIG = -1e30

ADAM_LR = 0.001
ADAM_B1 = 0.9
ADAM_B2 = 0.999
ADAM_EPS = 1e-08
ADAM_WD = 0.01
ADAM_STEP = 10

C_AX, C_AB, C_AC, C_AG, C_RX, C_RG, C_CU, C_CV, C_CG, C_DQ, C_DK, C_DV, C_DG = range(13)

SUBLANES = 8
LANES = 128
VMEM_LIMIT = 56 * 1024 * 1024
TILE_IN = 512
TILE_MIX = 256
TILE_DW = 512
ATTN_QB = 4
GELU_K0 = math.sqrt(2.0 / math.pi)
GELU_K1 = 0.044715


def _params(sem):
    return pltpu.CompilerParams(dimension_semantics=sem, vmem_limit_bytes=VMEM_LIMIT)


def _silu(x):
    return x * jax.nn.sigmoid(x)


def _dsilu(x):
    s = jax.nn.sigmoid(x)
    return s * (1.0 + x * (1.0 - s))


def _gelu(x):
    return 0.5 * x * (1.0 + jnp.tanh(GELU_K0 * (x + GELU_K1 * x * x * x)))


def _gelu_and_grad(x):
    t = jnp.tanh(GELU_K0 * (x + GELU_K1 * x * x * x))
    g = 0.5 * x * (1.0 + t)
    dg = 0.5 * (1.0 + t) + 0.5 * x * (1.0 - t * t) * GELU_K0 * (1.0 + 3.0 * GELU_K1 * x * x)
    return g, dg


def _neg_expm1(x):
    series = x * (1.0 + x * (0.5 + x * (1.0 / 6.0 + x * (1.0 / 24.0 + x * (1.0 / 120.0)))))
    return -jnp.where(x > -0.05, series, jnp.exp(x) - 1.0)


def _shift_down(v, halo, k):
    r = pltpu.roll(v, k, 0)
    rh = pltpu.roll(halo, k, 0)
    row = lax.broadcasted_iota(jnp.int32, halo.shape, 0)
    top = jnp.where(row < k, rh, r[:SUBLANES])
    return jnp.concatenate([top, r[SUBLANES:]], axis=0)


def _shift_up(v, halo, k):
    t = v.shape[0]
    r = pltpu.roll(v, t - k, 0)
    rh = pltpu.roll(halo, SUBLANES - k, 0)
    row = lax.broadcasted_iota(jnp.int32, halo.shape, 0)
    bot = jnp.where(row >= SUBLANES - k, rh, r[t - SUBLANES:])
    return jnp.concatenate([r[:t - SUBLANES], bot], axis=0)


def _scan_causal(a, b):
    t = a.shape[0]
    row = lax.broadcasted_iota(jnp.int32, a.shape, 0)
    d = 1
    while d < t:
        m = row >= d
        a_s = jnp.where(m, pltpu.roll(a, d, 0), 1.0)
        b_s = jnp.where(m, pltpu.roll(b, d, 0), 0.0)
        b = a * b_s + b
        a = a * a_s
        d *= 2
    return a, b


def _scan_anticausal(a, b):
    t = a.shape[0]
    row = lax.broadcasted_iota(jnp.int32, a.shape, 0)
    d = 1
    while d < t:
        m = row < t - d
        a_s = jnp.where(m, pltpu.roll(a, t - d, 0), 1.0)
        b_s = jnp.where(m, pltpu.roll(b, t - d, 0), 0.0)
        b = a * b_s + b
        a = a * a_s
        d *= 2
    return a, b


def _head_of_lane(shape):
    return lax.broadcasted_iota(jnp.int32, shape, len(shape) - 1) // HEAD_DIM


def _put_row(acc_shape, k, row_vec):
    row = lax.broadcasted_iota(jnp.int32, acc_shape, 0)
    return jnp.where(row == k, jnp.broadcast_to(row_vec, acc_shape), 0.0)


def _dot(a, b):
    return jnp.dot(a, b, preferred_element_type=F32)


def _dot_nt(a, b):
    return lax.dot_general(a, b, (((1,), (1,)), ((), ())), preferred_element_type=F32)


def _dot_tn(a, b):
    return lax.dot_general(a, b, (((0,), (0,)), ((), ())), preferred_element_type=F32)


def _deinterleave_store(val, stage, outs):
    t, c = val.shape
    for hh in range(c // LANES):
        stage[hh][...] = val[:, hh * LANES:(hh + 1) * LANES].astype(F32)
    for dil, ref in outs:
        for r in range(dil):
            for hh in range(c // LANES):
                ref[r, :, hh * LANES:(hh + 1) * LANES] = stage[hh][pl.ds(r, t // dil, stride=dil), :].astype(ref.dtype)


def _interleave_load(ref, dil, stage):
    _, n, c = ref.shape
    for r in range(dil):
        for hh in range(c // LANES):
            stage[hh][pl.ds(r, n, stride=dil), :] = ref[r, :, hh * LANES:(hh + 1) * LANES].astype(F32)
    return jnp.concatenate([stage[hh][...] for hh in range(c // LANES)], axis=1)


def _stage_scratch(tile, cols, copies):
    return [pltpu.VMEM((tile, LANES), F32)] * (copies * (cols // LANES))


def _by_residue(s, dil, cols, dtype):
    return jax.ShapeDtypeStruct((dil, s // dil, cols), dtype)


def _residue_block(dil, tile, cols):
    return pl.BlockSpec((dil, tile // dil, cols), lambda i: (0, i, 0))


def in_fwd(x, g, w):
    s = x.shape[0]
    qkv_w = 3 * GROUP_W

    def body(x_ref, g_ref, w_ref, z_ref, h_ref, qkv1_ref, qkv4_ref, qkv16_ref, *stage):
        xv = x_ref[...]
        rs = lax.rsqrt(jnp.mean(xv * xv, axis=-1, keepdims=True) + NORM_EPS)
        h = (xv * rs * g_ref[...]).astype(BF16)
        h_ref[...] = h
        z = _dot_nt(h, w_ref[...])
        z_ref[...] = z
        qkv = z[:, C_DQ * GROUP_W:(C_DV + 1) * GROUP_W]
        qkv1_ref[...] = qkv.astype(BF16)
        _deinterleave_store(qkv, stage, ((PATTERN_DILS[1], qkv4_ref), (PATTERN_DILS[2], qkv16_ref)))

    return pl.pallas_call(
        body, name="in_fwd", grid=(s // TILE_IN,),
        in_specs=[pl.BlockSpec((TILE_IN, D_MODEL), lambda i: (i, 0)),
                  pl.BlockSpec((1, D_MODEL), lambda i: (0, 0)),
                  pl.BlockSpec((D_IN, D_MODEL), lambda i: (0, 0))],
        out_specs=[pl.BlockSpec((TILE_IN, D_IN), lambda i: (i, 0)),
                   pl.BlockSpec((TILE_IN, D_MODEL), lambda i: (i, 0)),
                   pl.BlockSpec((TILE_IN, qkv_w), lambda i: (i, 0)),
                   _residue_block(PATTERN_DILS[1], TILE_IN, qkv_w),
                   _residue_block(PATTERN_DILS[2], TILE_IN, qkv_w)],
        out_shape=[jax.ShapeDtypeStruct((s, D_IN), F32), jax.ShapeDtypeStruct((s, D_MODEL), BF16),
                   jax.ShapeDtypeStruct((s, qkv_w), BF16),
                   _by_residue(s, PATTERN_DILS[1], qkv_w, BF16), _by_residue(s, PATTERN_DILS[2], qkv_w, BF16)],
        scratch_shapes=_stage_scratch(TILE_IN, qkv_w, 1),
        compiler_params=_params(("parallel",)),
    )(x, g, w)


def out_fwd(y, w, x):
    s = x.shape[0]

    def body(y_ref, w_ref, x_ref, o_ref):
        o_ref[...] = x_ref[...] + _dot(y_ref[...], w_ref[...])

    return pl.pallas_call(
        body, name="out_fwd", grid=(s // TILE_IN,),
        in_specs=[pl.BlockSpec((TILE_IN, D_MIX), lambda i: (i, 0)),
                  pl.BlockSpec((D_MIX, D_MODEL), lambda i: (0, 0)),
                  pl.BlockSpec((TILE_IN, D_MODEL), lambda i: (i, 0))],
        out_specs=pl.BlockSpec((TILE_IN, D_MODEL), lambda i: (i, 0)),
        out_shape=jax.ShapeDtypeStruct((s, D_MODEL), F32),
        compiler_params=_params(("parallel",)),
    )(y, w, x)


def out_bwd(dx, w, z, o):
    s = dx.shape[0]
    abc = 3 * GROUP_W

    def body(dx_ref, w_ref, dg_ref, o_ref, dy_ref, ddg_ref, do1_ref, do4_ref, do16_ref, dl1_ref, dl4_ref, dl16_ref,
             *stage):
        stage_a, stage_b = stage[:2], stage[2:]
        dy = _dot_nt(dx_ref[...].astype(BF16), w_ref[...])
        dy_ref[...] = dy[:, :abc]
        dyd = dy[:, abc:]
        head = _head_of_lane((TILE_IN, GROUP_W))
        dg = dg_ref[...]
        o = o_ref[...]
        do = dyd * _silu(dg)
        ddg_ref[...] = dyd * o * _dsilu(dg)
        prod = do * o
        dl = jnp.zeros_like(prod)
        for h in range(N_HEADS):
            sm = jnp.sum(jnp.where(head == h, prod, 0.0), axis=-1, keepdims=True)
            dl = jnp.where(head == h, sm, dl)
        do1_ref[...] = do.astype(BF16)
        dl1_ref[...] = dl
        _deinterleave_store(do, stage_a, ((PATTERN_DILS[1], do4_ref), (PATTERN_DILS[2], do16_ref)))
        _deinterleave_store(dl, stage_b, ((PATTERN_DILS[1], dl4_ref), (PATTERN_DILS[2], dl16_ref)))

    row = pl.BlockSpec((TILE_IN, GROUP_W), lambda i: (i, 0))
    r4 = _residue_block(PATTERN_DILS[1], TILE_IN, GROUP_W)
    r16 = _residue_block(PATTERN_DILS[2], TILE_IN, GROUP_W)
    return pl.pallas_call(
        body, name="out_bwd", grid=(s // TILE_IN,),
        in_specs=[pl.BlockSpec((TILE_IN, D_MODEL), lambda i: (i, 0)),
                  pl.BlockSpec((D_MIX, D_MODEL), lambda i: (0, 0)),
                  pl.BlockSpec((TILE_IN, GROUP_W), lambda i: (i, C_DG)), row],
        out_specs=[pl.BlockSpec((TILE_IN, abc), lambda i: (i, 0)), row, row, r4, r16, row, r4, r16],
        out_shape=[jax.ShapeDtypeStruct((s, abc), F32), jax.ShapeDtypeStruct((s, GROUP_W), F32),
                   jax.ShapeDtypeStruct((s, GROUP_W), BF16),
                   _by_residue(s, PATTERN_DILS[1], GROUP_W, BF16), _by_residue(s, PATTERN_DILS[2], GROUP_W, BF16),
                   jax.ShapeDtypeStruct((s, GROUP_W), F32),
                   _by_residue(s, PATTERN_DILS[1], GROUP_W, F32), _by_residue(s, PATTERN_DILS[2], GROUP_W, F32)],
        scratch_shapes=_stage_scratch(TILE_IN, GROUP_W, 2),
        compiler_params=_params(("parallel",)),
    )(dx, w, z, o)


def in_bwd(dz, w, x, g, dx_next):
    s = x.shape[0]

    def body(dz_ref, w_ref, x_ref, g_ref, dxn_ref, dx_ref, dg_ref):
        @pl.when(pl.program_id(0) == 0)
        def _():
            dg_ref[...] = jnp.zeros_like(dg_ref)

        dh = _dot(dz_ref[...], w_ref[...])
        xv = x_ref[...]
        rs = lax.rsqrt(jnp.mean(xv * xv, axis=-1, keepdims=True) + NORM_EPS)
        xh = xv * rs
        dg_ref[...] += _put_row(dg_ref.shape, 0, jnp.sum(dh * xh, axis=0, keepdims=True))
        dn = dh * g_ref[...]
        dx_ref[...] = dxn_ref[...] + rs * (dn - xh * jnp.mean(dn * xh, axis=-1, keepdims=True))

    return pl.pallas_call(
        body, name="in_bwd", grid=(s // TILE_IN,),
        in_specs=[pl.BlockSpec((TILE_IN, D_IN), lambda i: (i, 0)),
                  pl.BlockSpec((D_IN, D_MODEL), lambda i: (0, 0)),
                  pl.BlockSpec((TILE_IN, D_MODEL), lambda i: (i, 0)),
                  pl.BlockSpec((1, D_MODEL), lambda i: (0, 0)),
                  pl.BlockSpec((TILE_IN, D_MODEL), lambda i: (i, 0))],
        out_specs=[pl.BlockSpec((TILE_IN, D_MODEL), lambda i: (i, 0)),
                   pl.BlockSpec((SUBLANES, D_MODEL), lambda i: (0, 0))],
        out_shape=[jax.ShapeDtypeStruct((s, D_MODEL), F32), jax.ShapeDtypeStruct((SUBLANES, D_MODEL), F32)],
        compiler_params=_params(("arbitrary",)),
    )(dz, w, x, g, dx_next)


def matmul_tn(a, b, n_split):
    s, m = a.shape
    n = b.shape[1]
    tn = n // n_split

    def body(a_ref, b_ref, o_ref):
        @pl.when(pl.program_id(1) == 0)
        def _():
            o_ref[...] = jnp.zeros_like(o_ref)

        o_ref[...] += _dot_tn(a_ref[...], b_ref[...].astype(BF16))

    return pl.pallas_call(
        body, name="matmul_tn", grid=(n_split, s // TILE_DW),
        in_specs=[pl.BlockSpec((TILE_DW, m), lambda j, k: (k, 0)),
                  pl.BlockSpec((TILE_DW, tn), lambda j, k: (k, j))],
        out_specs=pl.BlockSpec((m, tn), lambda j, k: (0, j)),
        out_shape=jax.ShapeDtypeStruct((m, n), F32),
        compiler_params=_params(("parallel", "arbitrary")),
    )(a, b)


def grad_w_in(h, dz, chip):
    s = h.shape[0]
    rows = D_IN // N_CHIPS

    def body(chip_ref, h_ref, dz_ref, staged_ref, own_ref, acc):
        k = pl.program_id(0)

        @pl.when(k == 0)
        def _():
            acc[...] = jnp.zeros_like(acc)

        acc[...] += _dot_tn(dz_ref[...], h_ref[...])

        @pl.when(k == s // TILE_DW - 1)
        def _():
            for j in range(N_CHIPS):
                part = acc[j * rows:(j + 1) * rows, :]
                staged_ref[j] = part.astype(BF16)

                @pl.when(chip_ref[0] == j)
                def _():
                    own_ref[...] = part

    return pl.pallas_call(
        body, name="grad_w_in",
        grid_spec=pltpu.PrefetchScalarGridSpec(
            num_scalar_prefetch=1, grid=(s // TILE_DW,),
            in_specs=[pl.BlockSpec((TILE_DW, D_MODEL), lambda k, c: (k, 0)),
                      pl.BlockSpec((TILE_DW, D_IN), lambda k, c: (k, 0))],
            out_specs=[pl.BlockSpec((N_CHIPS, rows, D_MODEL), lambda k, c: (0, 0, 0)),
                       pl.BlockSpec((rows, D_MODEL), lambda k, c: (0, 0))],
            scratch_shapes=[pltpu.VMEM((D_IN, D_MODEL), F32)]),
        out_shape=[jax.ShapeDtypeStruct((N_CHIPS, rows, D_MODEL), BF16), jax.ShapeDtypeStruct((rows, D_MODEL), F32)],
        compiler_params=_params(("arbitrary",)),
    )(chip, h, dz)


def loss_head(x, g, tgt):
    s = x.shape[0]

    def body(x_ref, g_ref, t_ref, l_ref, dx_ref, dg_ref):
        @pl.when(pl.program_id(0) == 0)
        def _():
            l_ref[...] = jnp.zeros_like(l_ref)
            dg_ref[...] = jnp.zeros_like(dg_ref)

        xv = x_ref[...]
        gv = g_ref[...]
        rs = lax.rsqrt(jnp.mean(xv * xv, axis=-1, keepdims=True) + NORM_EPS)
        xh = xv * rs
        e = xh * gv - t_ref[...]
        part = 0.5 * jnp.sum(jnp.mean(e * e, axis=-1, keepdims=True), axis=0, keepdims=True)
        l_ref[...] += jnp.broadcast_to(part, l_ref.shape)
        dy = e * (1.0 / D_MODEL)
        dg_ref[...] += _put_row(dg_ref.shape, 0, jnp.sum(dy * xh, axis=0, keepdims=True))
        dn = dy * gv
        dx_ref[...] = rs * (dn - xh * jnp.mean(dn * xh, axis=-1, keepdims=True))

    return pl.pallas_call(
        body, name="loss_head", grid=(s // TILE_IN,),
        in_specs=[pl.BlockSpec((TILE_IN, D_MODEL), lambda i: (i, 0)),
                  pl.BlockSpec((1, D_MODEL), lambda i: (0, 0)),
                  pl.BlockSpec((TILE_IN, D_MODEL), lambda i: (i, 0))],
        out_specs=[pl.BlockSpec((SUBLANES, 128), lambda i: (0, 0)),
                   pl.BlockSpec((TILE_IN, D_MODEL), lambda i: (i, 0)),
                   pl.BlockSpec((SUBLANES, D_MODEL), lambda i: (0, 0))],
        out_shape=[jax.ShapeDtypeStruct((SUBLANES, 128), F32), jax.ShapeDtypeStruct((s, D_MODEL), F32),
                   jax.ShapeDtypeStruct((SUBLANES, D_MODEL), F32)],
        compiler_params=_params(("arbitrary",)),
    )(x, g, tgt)


def _attn_bias(dil):
    qi = np.arange(ATTN_BLOCK)[:, None]
    ki = np.arange(2 * ATTN_BLOCK)[None, :]
    delta = qi + ATTN_BLOCK - ki
    band = (delta >= 0) & (delta <= ATTN_BLOCK)
    out = np.empty((2, N_HEADS, ATTN_BLOCK, 2 * ATTN_BLOCK), np.float32)
    for f in range(2):
        ok = band & ((ki >= ATTN_BLOCK) | (f == 0))
        for h in range(N_HEADS):
            out[f, h] = np.where(ok, -ALIBI_SLOPES[h] * dil * delta, NEG_BIG)
    return jnp.asarray(out.reshape(2, N_HEADS * ATTN_BLOCK, 2 * ATTN_BLOCK))


def _stack_heads(a, head):
    return jnp.concatenate([jnp.where(head == h, a, jnp.zeros_like(a)) for h in range(N_HEADS)], axis=0)


def _unstack_heads(a, head):
    out = a[:ATTN_BLOCK]
    for h in range(1, N_HEADS):
        out = jnp.where(head == h, a[h * ATTN_BLOCK:(h + 1) * ATTN_BLOCK], out)
    return out


def _head_column(a):
    return jnp.concatenate([a[:, h * HEAD_DIM:h * HEAD_DIM + 1] for h in range(N_HEADS)], axis=0)


def _attn_specs(n_blocks):
    rows = ATTN_QB * ATTN_BLOCK
    cur = lambda c: pl.BlockSpec((rows, GROUP_W), lambda n, c=c: (n, c))
    prev = lambda c: pl.BlockSpec((ATTN_BLOCK, GROUP_W), lambda n, c=c: (jnp.maximum(n * ATTN_QB - 1, 0), c))
    nxt = lambda c: pl.BlockSpec((ATTN_BLOCK, GROUP_W),
                                 lambda n, c=c: (jnp.minimum(n * ATTN_QB + ATTN_QB, n_blocks - 1), c))
    return cur, prev, nxt


def _keys(kp_ref, k_ref, j):
    prev = kp_ref[...] if j == 0 else k_ref[(j - 1) * ATTN_BLOCK:j * ATTN_BLOCK, :]
    return jnp.concatenate([prev, k_ref[j * ATTN_BLOCK:(j + 1) * ATTN_BLOCK, :]], axis=0)


def attn_fwd(qkv, dil):
    s = qkv.shape[0]
    n_blocks = s // ATTN_BLOCK
    bps = n_blocks // dil
    rows = ATTN_QB * ATTN_BLOCK

    def body(q_ref, kp_ref, k_ref, vp_ref, v_ref, bias_ref, o_ref, lse_ref):
        n = pl.program_id(0)
        head = _head_of_lane((ATTN_BLOCK, GROUP_W))
        for j in range(ATTN_QB):
            sl = slice(j * ATTN_BLOCK, (j + 1) * ATTN_BLOCK)
            first = (((n * ATTN_QB + j) % bps) == 0).astype(jnp.int32)
            qs = _stack_heads(q_ref[sl, :], head)
            sc = _dot_nt(qs, _keys(kp_ref, k_ref, j)) * ATTN_SCALE + bias_ref[first]
            m = jnp.max(sc, axis=-1, keepdims=True)
            pr = jnp.exp(sc - m)
            l = jnp.sum(pr, axis=-1, keepdims=True)
            oh = _dot(pr.astype(BF16), _keys(vp_ref, v_ref, j)) / l
            o_ref[sl, :] = _unstack_heads(oh, head)
            lse_ref[sl, :] = _unstack_heads(jnp.broadcast_to(m + jnp.log(l), oh.shape), head)

    cur, prev, _ = _attn_specs(n_blocks)
    bias = _attn_bias(dil)
    out = jax.ShapeDtypeStruct((s, GROUP_W), F32)
    return pl.pallas_call(
        body, name=f"attn_fwd_d{dil}", grid=(n_blocks // ATTN_QB,),
        in_specs=[cur(0), prev(1), cur(1), prev(2), cur(2), pl.BlockSpec(bias.shape, lambda n: (0, 0, 0))],
        out_specs=[cur(0), cur(0)],
        out_shape=[out, out],
        compiler_params=_params(("parallel",)),
    )(qkv, qkv, qkv, qkv, qkv, bias)


def attn_bwd(qkv, do, lse, dlt, dil):
    s = qkv.shape[0]
    n_blocks = s // ATTN_BLOCK
    bps = n_blocks // dil
    rows = ATTN_QB * ATTN_BLOCK

    def body(q_ref, qn_ref, kp_ref, k_ref, vp_ref, v_ref, do_ref, don_ref, lse_ref, lsen_ref, dl_ref, dln_ref,
             bias_ref, out_ref, dk_acc, dv_acc):
        n = pl.program_id(0)
        head = _head_of_lane((ATTN_BLOCK, GROUP_W))
        dk_acc[...] = jnp.zeros_like(dk_acc)
        dv_acc[...] = jnp.zeros_like(dv_acc)

        def pair(qj, doj, lsej, dlj, kk, vv, bias, keep):
            qs = _stack_heads(qj, head)
            dos = _stack_heads(doj, head)
            sc = _dot_nt(qs, kk) * ATTN_SCALE + bias
            if keep is None:
                pr = jnp.exp(sc - _head_column(lsej))
            else:
                pr = jnp.exp(jnp.minimum(sc - _head_column(lsej), 0.0)) * keep
            dp = _dot_nt(dos, vv)
            ds = (pr * (dp - _head_column(dlj)) * ATTN_SCALE).astype(BF16)
            return ds, _dot_tn(ds, qs), _dot_tn(pr.astype(BF16), dos)

        for j in range(ATTN_QB):
            sl = slice(j * ATTN_BLOCK, (j + 1) * ATTN_BLOCK)
            first = (((n * ATTN_QB + j) % bps) == 0).astype(jnp.int32)
            kk = _keys(kp_ref, k_ref, j)
            ds, dks, dvs = pair(q_ref[sl, :], do_ref[sl, :], lse_ref[sl, :], dl_ref[sl, :],
                                kk, _keys(vp_ref, v_ref, j), bias_ref[first], None)
            out_ref[sl, 0:GROUP_W] = _unstack_heads(_dot(ds, kk), head)
            acc = slice(j * ATTN_BLOCK, (j + 2) * ATTN_BLOCK)
            dk_acc[acc, :] += dks
            dv_acc[acc, :] += dvs

        nxt = n * ATTN_QB + ATTN_QB
        valid = ((nxt < n_blocks) & ((nxt % bps) != 0)).astype(F32)
        last = slice((ATTN_QB - 1) * ATTN_BLOCK, ATTN_QB * ATTN_BLOCK)
        _, dks, dvs = pair(qn_ref[...], don_ref[...], lsen_ref[...], dln_ref[...], k_ref[last, :], v_ref[last, :],
                           bias_ref[0][:, :ATTN_BLOCK], valid)
        acc = slice(ATTN_QB * ATTN_BLOCK, (ATTN_QB + 1) * ATTN_BLOCK)
        dk_acc[acc, :] += dks
        dv_acc[acc, :] += dvs
        out_ref[:, GROUP_W:2 * GROUP_W] = dk_acc[ATTN_BLOCK:, :]
        out_ref[:, 2 * GROUP_W:3 * GROUP_W] = dv_acc[ATTN_BLOCK:, :]

    cur, prev, nxt = _attn_specs(n_blocks)
    bias = _attn_bias(dil)
    return pl.pallas_call(
        body, name=f"attn_bwd_d{dil}", grid=(n_blocks // ATTN_QB,),
        in_specs=[cur(0), nxt(0), prev(1), cur(1), prev(2), cur(2), cur(0), nxt(0), cur(0), nxt(0), cur(0), nxt(0),
                  pl.BlockSpec(bias.shape, lambda n: (0, 0, 0))],
        out_specs=pl.BlockSpec((rows, 3 * GROUP_W), lambda n: (n, 0)),
        out_shape=jax.ShapeDtypeStruct((s, 3 * GROUP_W), F32),
        scratch_shapes=[pltpu.VMEM(((ATTN_QB + 1) * ATTN_BLOCK, GROUP_W), F32),
                        pltpu.VMEM(((ATTN_QB + 1) * ATTN_BLOCK, GROUP_W), F32)],
        compiler_params=_params(("parallel",)),
    )(qkv, qkv, qkv, qkv, qkv, qkv, do, do, lse, lse, dlt, dlt, bias)


def _zcol(c):
    return pl.BlockSpec((TILE_MIX, GROUP_W), lambda i, c=c: (i, c))


def _zhalo(c):
    per = TILE_MIX // SUBLANES
    return pl.BlockSpec((SUBLANES, GROUP_W), lambda i, c=c: (jnp.maximum(i * per - 1, 0), c))


def _full(shape):
    return pl.BlockSpec(shape, lambda i: tuple(0 for _ in shape))


def _lru_gates(xb, wa_ref, wx_ref, ba, bx, lam):
    xbb = xb.astype(BF16)
    r = jax.nn.sigmoid(_dot(xbb, wa_ref[...]) + ba)
    ig = jax.nn.sigmoid(_dot(xbb, wx_ref[...]) + bx)
    nl = -lam
    sp = jnp.maximum(nl, 0.0) + jnp.log1p(jnp.exp(-jnp.abs(nl)))
    log_a = (-RG_C * r) * sp
    a = jnp.exp(log_a)
    mult = jnp.sqrt(_neg_expm1(2.0 * log_a))
    return r, ig, sp, a, mult


def _gmlp_spatial(ws_ref, vvb, head):
    outs = []
    for j in range(vvb.shape[0] // GMLP_CHUNK):
        blk = vvb[j * GMLP_CHUNK:(j + 1) * GMLP_CHUNK, :]
        acc = jnp.zeros((GMLP_CHUNK, GROUP_W), F32)
        for h in range(N_HEADS):
            acc = jnp.where(head[:GMLP_CHUNK] == h, _dot(ws_ref[h], blk), acc)
        outs.append(acc)
    return jnp.concatenate(outs, axis=0)


def mix_fwd(z, attn, wts):
    s = z.shape[0]
    d4, d16 = PATTERN_DILS[1], PATTERN_DILS[2]

    def body(ax_ref, ab_ref, ac_ref, ag_ref, rx_ref, rg_ref, cu_ref, cv_ref, cg_ref, dg_ref,
             axh_ref, ach_ref, rxh_ref, o1_ref, l1_ref, o4_ref, l4_ref, o16_ref, l16_ref,
             caw_ref, crw_ref, crb_ref, wa_ref, wx_ref, ba_ref, bx_ref, lam_ref, gng_ref, ws_ref, bs_ref,
             y_ref, hl_ref, o_ref, lse_ref, lse4_ref, lse16_ref, carry, *stage):
        st_a, st_b, st_c, st_d, st_e = (stage[2 * k:2 * k + 2] for k in range(5))
        i = pl.program_id(0)

        @pl.when(i == 0)
        def _():
            carry[...] = jnp.zeros_like(carry)

        nz = (i > 0).astype(F32)
        head = _head_of_lane((TILE_MIX, GROUP_W))

        pa = ac_ref[...] * ax_ref[...]
        pah = ach_ref[...] * axh_ref[...] * nz
        cv = caw_ref[2:3, :] * pa + caw_ref[1:2, :] * _shift_down(pa, pah, 1) + caw_ref[0:1, :] * _shift_down(pa, pah, 2)
        y_ref[:, 0:GROUP_W] = (ab_ref[...] * cv * _silu(ag_ref[...])).astype(BF16)

        rx = rx_ref[...]
        rxh = rxh_ref[...] * nz
        xb = (crw_ref[3:4, :] * rx + crw_ref[2:3, :] * _shift_down(rx, rxh, 1) + crw_ref[1:2, :] * _shift_down(rx, rxh, 2)
              + crw_ref[0:1, :] * _shift_down(rx, rxh, 3) + crb_ref[...])
        _, ig, _, a, mult = _lru_gates(xb, wa_ref, wx_ref, ba_ref[...], bx_ref[...], lam_ref[...])
        ca, cb = _scan_causal(a, mult * (ig * xb))
        hl = cb + ca * carry[SUBLANES - 1:SUBLANES, :]
        hl_ref[...] = hl
        carry[...] = hl[TILE_MIX - SUBLANES:, :]
        y_ref[:, GROUP_W:2 * GROUP_W] = (hl * _silu(rg_ref[...])).astype(BF16)

        u = _gelu(cu_ref[...])
        gv = _gelu(cv_ref[...])
        rs = lax.rsqrt(jnp.mean(gv * gv, axis=-1, keepdims=True) + NORM_EPS)
        vvb = (gv * rs * gng_ref[...]).astype(BF16)
        sp = _gmlp_spatial(ws_ref, vvb, head) + jnp.concatenate([bs_ref[...]] * (TILE_MIX // GMLP_CHUNK), axis=0)
        y_ref[:, 2 * GROUP_W:3 * GROUP_W] = (u * sp * _silu(cg_ref[...])).astype(BF16)

        ops = (o1_ref[...], _interleave_load(o4_ref, d4, st_a), _interleave_load(o16_ref, d16, st_b))
        lps = (l1_ref[...], _interleave_load(l4_ref, d4, st_c), _interleave_load(l16_ref, d16, st_d))
        m = jnp.maximum(jnp.maximum(lps[0], lps[1]), lps[2])
        zsum = jnp.zeros_like(m)
        o = jnp.zeros_like(m)
        for op, lp in zip(ops, lps):
            w = jnp.exp(lp - m)
            zsum = zsum + w
            o = o + w * op
        o = o / zsum
        lse = m + jnp.log(zsum)
        o_ref[...] = o
        lse_ref[...] = lse
        _deinterleave_store(lse, st_e, ((d4, lse4_ref), (d16, lse16_ref)))
        y_ref[:, 3 * GROUP_W:4 * GROUP_W] = (o * _silu(dg_ref[...])).astype(BF16)

    row = pl.BlockSpec((TILE_MIX, GROUP_W), lambda i: (i, 0))
    r4 = _residue_block(d4, TILE_MIX, GROUP_W)
    r16 = _residue_block(d16, TILE_MIX, GROUP_W)
    names = ("caw", "crw", "crb", "wa", "wx", "ba", "bx", "lam", "gng", "ws", "bs")
    in_specs = ([_zcol(c) for c in (C_AX, C_AB, C_AC, C_AG, C_RX, C_RG, C_CU, C_CV, C_CG, C_DG)]
                + [_zhalo(C_AX), _zhalo(C_AC), _zhalo(C_RX), row, row, r4, r4, r16, r16]
                + [_full(wts[k].shape) for k in names])
    return pl.pallas_call(
        body, name="mix_fwd", grid=(s // TILE_MIX,),
        in_specs=in_specs,
        out_specs=[pl.BlockSpec((TILE_MIX, D_MIX), lambda i: (i, 0)), row, row, row, r4, r16],
        out_shape=([jax.ShapeDtypeStruct((s, D_MIX), BF16)] + [jax.ShapeDtypeStruct((s, GROUP_W), F32)] * 3
                   + [_by_residue(s, d4, GROUP_W, F32), _by_residue(s, d16, GROUP_W, F32)]),
        scratch_shapes=[pltpu.VMEM((SUBLANES, GROUP_W), F32)] + _stage_scratch(TILE_MIX, GROUP_W, 5),
        compiler_params=_params(("arbitrary",)),
    )(*([z] * 13), *[a for pair in attn for a in pair], *[wts[k] for k in names])


def layer_bwd(dy, z, hl, dqkv, ddg, wts, w_in_t, x, gain, dx_next, h):
    s = z.shape[0]
    d4, d16 = PATTERN_DILS[1], PATTERN_DILS[2]
    n_tiles = s // TILE_MIX

    def body(dya_ref, dyb_ref, dyc_ref, ax_ref, ab_ref, ac_ref, ag_ref, rx_ref, rg_ref, cu_ref, cv_ref, cg_ref,
             axh_ref, ach_ref, rxh_ref, hl_ref, hlh_ref, dqkv1_ref, dqkv4_ref, dqkv16_ref, ddg_ref,
             caw_ref, crw_ref, crb_ref, wa_ref, wx_ref, ba_ref, bx_ref, lam_ref, gng_ref, ws_ref, wst_ref, bs_ref,
             wt_hbm, x_ref, gain_ref, dxn_ref, h_ref,
             dx_ref, gg_ref, gw_hbm, ga_ref, gr_ref, gn_ref, gwa_ref, gwx_ref, gws_ref, gbs_ref,
             c_dcv, c_g, c_a, c_dxb, wt, gw_acc, dz_buf, *stage):
        st_a, st_b = stage[:len(stage) // 2], stage[len(stage) // 2:]
        step = pl.program_id(0)
        i = n_tiles - 1 - jnp.minimum(step, n_tiles - 1)
        live = (step < n_tiles).astype(F32)
        slot = step % 2
        dz_ref = dz_buf.at[slot]

        @pl.when(step == 0)
        def _():
            for r in (c_dcv, c_g, c_a, c_dxb, gg_ref, ga_ref, gr_ref, gn_ref, gwa_ref, gwx_ref, gws_ref, gbs_ref, gw_acc):
                r[...] = jnp.zeros_like(r)
            dz_buf[1] = jnp.zeros((TILE_MIX, D_IN), BF16)
            pltpu.sync_copy(wt_hbm, wt)

        dzp = dz_buf[1 - slot]
        dh = _dot(dzp, wt[...])
        xv = x_ref[...]
        rs = lax.rsqrt(jnp.mean(xv * xv, axis=-1, keepdims=True) + NORM_EPS)
        xh = xv * rs
        gg_ref[...] += _put_row(gg_ref.shape, 0, jnp.sum(dh * xh, axis=0, keepdims=True))
        dn = dh * gain_ref[...]
        dx_ref[...] = dxn_ref[...] + rs * (dn - xh * jnp.mean(dn * xh, axis=-1, keepdims=True))
        gw_acc[...] += _dot_tn(dzp, h_ref[...])

        nz = (i > 0).astype(F32)
        head = _head_of_lane((TILE_MIX, GROUP_W))
        shp8 = (SUBLANES, GROUP_W)
        colsum = lambda v: jnp.sum(v, axis=0, keepdims=True)

        ax, ab, ac, ag = ax_ref[...], ab_ref[...], ac_ref[...], ag_ref[...]
        dya = dya_ref[...] * live
        pa = ac * ax
        pah = ach_ref[...] * axh_ref[...] * nz
        pa1 = _shift_down(pa, pah, 1)
        pa2 = _shift_down(pa, pah, 2)
        cv = caw_ref[2:3, :] * pa + caw_ref[1:2, :] * pa1 + caw_ref[0:1, :] * pa2
        sg = _silu(ag)
        dz_ref[:, C_AB * GROUP_W:(C_AB + 1) * GROUP_W] = (dya * cv * sg).astype(BF16)
        dz_ref[:, C_AG * GROUP_W:(C_AG + 1) * GROUP_W] = (dya * ab * cv * _dsilu(ag)).astype(BF16)
        dcv = dya * ab * sg
        nxt = c_dcv[...]
        dpa = caw_ref[2:3, :] * dcv + caw_ref[1:2, :] * _shift_up(dcv, nxt, 1) + caw_ref[0:1, :] * _shift_up(dcv, nxt, 2)
        c_dcv[...] = dcv[:SUBLANES, :]
        dz_ref[:, C_AC * GROUP_W:(C_AC + 1) * GROUP_W] = (dpa * ax).astype(BF16)
        dz_ref[:, C_AX * GROUP_W:(C_AX + 1) * GROUP_W] = (dpa * ac).astype(BF16)
        ga_ref[...] += (_put_row(shp8, 2, colsum(dcv * pa)) + _put_row(shp8, 1, colsum(dcv * pa1))
                        + _put_row(shp8, 0, colsum(dcv * pa2)))

        rx, rg = rx_ref[...], rg_ref[...]
        dyb = dyb_ref[...] * live
        rxh = rxh_ref[...] * nz
        rx1, rx2, rx3 = _shift_down(rx, rxh, 1), _shift_down(rx, rxh, 2), _shift_down(rx, rxh, 3)
        xb = crw_ref[3:4, :] * rx + crw_ref[2:3, :] * rx1 + crw_ref[1:2, :] * rx2 + crw_ref[0:1, :] * rx3 + crb_ref[...]
        lam = lam_ref[...]
        r, ig, sp, a, mult = _lru_gates(xb, wa_ref, wx_ref, ba_ref[...], bx_ref[...], lam)
        hl = hl_ref[...]
        hprev = _shift_down(hl, hlh_ref[...] * nz, 1)
        dz_ref[:, C_RG * GROUP_W:(C_RG + 1) * GROUP_W] = (dyb * hl * _dsilu(rg)).astype(BF16)
        dh = dyb * _silu(rg)
        a_next = _shift_up(a, c_a[...], 1)
        ca, cb = _scan_anticausal(a_next, dh)
        g = (cb + ca * c_g[0:1, :]) * live
        c_g[...] = g[:SUBLANES, :]
        c_a[...] = a[:SUBLANES, :]
        u = ig * xb
        da = g * hprev
        dmult = g * u
        du = g * mult
        dlog_a = da * a - dmult * (a * a) / mult
        dr = dlog_a * (-RG_C * sp)
        dga = dr * r * (1.0 - r)
        dgx = (du * xb) * ig * (1.0 - ig)
        dgab, dgxb = dga.astype(BF16), dgx.astype(BF16)
        dxb = du * ig + _dot_nt(dgab, wa_ref[...]) + _dot_nt(dgxb, wx_ref[...])
        xbb = xb.astype(BF16)
        gwa_ref[...] += _dot_tn(xbb, dgab)
        gwx_ref[...] += _dot_tn(xbb, dgxb)
        nxt = c_dxb[...]
        drx = (crw_ref[3:4, :] * dxb + crw_ref[2:3, :] * _shift_up(dxb, nxt, 1) + crw_ref[1:2, :] * _shift_up(dxb, nxt, 2)
               + crw_ref[0:1, :] * _shift_up(dxb, nxt, 3))
        c_dxb[...] = dxb[:SUBLANES, :]
        dz_ref[:, C_RX * GROUP_W:(C_RX + 1) * GROUP_W] = drx.astype(BF16)
        dlam = colsum(dlog_a * (-RG_C * r)) * (-jax.nn.sigmoid(-lam))
        gr_ref[...] += (_put_row(shp8, 3, colsum(dxb * rx)) + _put_row(shp8, 2, colsum(dxb * rx1))
                        + _put_row(shp8, 1, colsum(dxb * rx2)) + _put_row(shp8, 0, colsum(dxb * rx3))
                        + _put_row(shp8, 4, colsum(dxb)) + _put_row(shp8, 5, colsum(dga))
                        + _put_row(shp8, 6, colsum(dgx)) + _put_row(shp8, 7, dlam))

        cu, cvv, cg = cu_ref[...], cv_ref[...], cg_ref[...]
        dyc = dyc_ref[...] * live
        u_c, du_c = _gelu_and_grad(cu)
        gv, dgv_c = _gelu_and_grad(cvv)
        rs = lax.rsqrt(jnp.mean(gv * gv, axis=-1, keepdims=True) + NORM_EPS)
        vh = gv * rs
        gng = gng_ref[...]
        vvb = (vh * gng).astype(BF16)
        spat = _gmlp_spatial(ws_ref, vvb, head) + jnp.concatenate([bs_ref[...]] * (TILE_MIX // GMLP_CHUNK), axis=0)
        sgc = _silu(cg)
        dz_ref[:, C_CU * GROUP_W:(C_CU + 1) * GROUP_W] = (dyc * spat * sgc * du_c).astype(BF16)
        dz_ref[:, C_CG * GROUP_W:(C_CG + 1) * GROUP_W] = (dyc * u_c * spat * _dsilu(cg)).astype(BF16)
        dsp = dyc * u_c * sgc
        dspb = dsp.astype(BF16)
        tril = (lax.broadcasted_iota(jnp.int32, (GMLP_CHUNK, GMLP_CHUNK), 0)
                >= lax.broadcasted_iota(jnp.int32, (GMLP_CHUNK, GMLP_CHUNK), 1))
        head_c = head[:GMLP_CHUNK]
        dvv_parts = []
        gbs = jnp.zeros((GMLP_CHUNK, GROUP_W), F32)
        for j in range(TILE_MIX // GMLP_CHUNK):
            sl = slice(j * GMLP_CHUNK, (j + 1) * GMLP_CHUNK)
            dblk = dspb[sl, :]
            vblk = vvb[sl, :]
            gbs = gbs + dsp[sl, :]
            acc = jnp.zeros((GMLP_CHUNK, GROUP_W), F32)
            for h in range(N_HEADS):
                acc = jnp.where(head_c == h, _dot(wst_ref[h], dblk), acc)
                dm = jnp.where(head_c == h, dblk, jnp.zeros_like(dblk))
                gws_ref[h] += jnp.where(tril, _dot_nt(dm, vblk), 0.0)
            dvv_parts.append(acc)
        gbs_ref[...] += gbs
        dvv = jnp.concatenate(dvv_parts, axis=0)
        gn_ref[...] += _put_row(shp8, 0, colsum(dvv * vh))
        dvh = dvv * gng
        dgv = rs * (dvh - vh * jnp.mean(dvh * vh, axis=-1, keepdims=True))
        dz_ref[:, C_CV * GROUP_W:(C_CV + 1) * GROUP_W] = (dgv * dgv_c).astype(BF16)

        dsum = dqkv1_ref[...] + _interleave_load(dqkv4_ref, d4, st_a) + _interleave_load(dqkv16_ref, d16, st_b)
        dz_ref[:, C_DQ * GROUP_W:(C_DV + 1) * GROUP_W] = (dsum * live).astype(BF16)
        dz_ref[:, C_DG * GROUP_W:(C_DG + 1) * GROUP_W] = (ddg_ref[...] * live).astype(BF16)

        @pl.when(step == n_tiles)
        def _():
            pltpu.sync_copy(gw_acc, gw_hbm)

    per = TILE_MIX // SUBLANES
    qkv_w = 3 * GROUP_W
    cur = lambda t: n_tiles - 1 - jnp.minimum(t, n_tiles - 1)
    lag = lambda t: jnp.minimum(n_tiles - t, n_tiles - 1)
    rev = lambda c: pl.BlockSpec((TILE_MIX, GROUP_W), lambda t, c=c: (cur(t), c))
    revh = lambda c: pl.BlockSpec((SUBLANES, GROUP_W), lambda t, c=c: (jnp.maximum(cur(t) * per - 1, 0), c))
    revr = lambda dil: pl.BlockSpec((dil, TILE_MIX // dil, qkv_w), lambda t: (0, cur(t), 0))
    lagged = pl.BlockSpec((TILE_MIX, D_MODEL), lambda t: (lag(t), 0))
    names = ("caw", "crw", "crb", "wa", "wx", "ba", "bx", "lam", "gng", "ws", "wst", "bs")
    in_specs = ([rev(0), rev(1), rev(2)]
                + [rev(c) for c in (C_AX, C_AB, C_AC, C_AG, C_RX, C_RG, C_CU, C_CV, C_CG)]
                + [revh(C_AX), revh(C_AC), revh(C_RX), rev(0), revh(0),
                   pl.BlockSpec((TILE_MIX, qkv_w), lambda t: (cur(t), 0)), revr(d4), revr(d16), rev(0)]
                + [_full(wts[k].shape) for k in names]
                + [ANY, lagged, _full((1, D_MODEL)), lagged, lagged])
    small = jax.ShapeDtypeStruct((SUBLANES, GROUP_W), F32)
    sq = jax.ShapeDtypeStruct((GROUP_W, GROUP_W), F32)
    out_shape = [jax.ShapeDtypeStruct((s, D_MODEL), F32), jax.ShapeDtypeStruct((SUBLANES, D_MODEL), F32),
                 jax.ShapeDtypeStruct((D_IN, D_MODEL), F32), small, small, small, sq, sq,
                 jax.ShapeDtypeStruct((N_HEADS, GMLP_CHUNK, GMLP_CHUNK), F32),
                 jax.ShapeDtypeStruct((GMLP_CHUNK, GROUP_W), F32)]
    out_specs = ([lagged, _full((SUBLANES, D_MODEL)), ANY] + [_full(o.shape) for o in out_shape[3:]])
    return pl.pallas_call(
        body, name="layer_bwd", grid=(n_tiles + 1,),
        in_specs=in_specs, out_specs=out_specs, out_shape=out_shape,
        scratch_shapes=([pltpu.VMEM((SUBLANES, GROUP_W), F32)] * 4
                        + [pltpu.VMEM((D_IN, D_MODEL), BF16), pltpu.VMEM((D_IN, D_MODEL), F32),
                           pltpu.VMEM((2, TILE_MIX, D_IN), BF16)]
                        + _stage_scratch(TILE_MIX, qkv_w, 2)),
        compiler_params=_params(("arbitrary",)),
    )(dy, dy, dy, *([z] * 12), hl, hl, *dqkv, ddg, *[wts[k] for k in names], w_in_t, x, gain, dx_next, h)


def _block_diag(w):
    eye = jnp.eye(N_HEADS, dtype=w.dtype)
    return (w[:, :, None, :] * eye[:, None, :, None]).reshape(GROUP_W, GROUP_W)


def _diag_blocks(g):
    g4 = g.reshape(N_HEADS, HEAD_DIM, N_HEADS, HEAD_DIM)
    return jnp.stack([g4[h, :, h, :] for h in range(N_HEADS)])


def _layer_weights(p, l):
    tril = jnp.tril(jnp.ones((GMLP_CHUNK, GMLP_CHUNK), dtype=bool))
    ws = jnp.where(tril[None], p["gmlp_ws"][l], 0.0).astype(BF16)
    row = lambda a: a[l][None, :]
    return dict(
        caw=p["conv_a_w"][l], crw=p["conv_r_w"][l], crb=row(p["conv_r_b"]),
        wa=_block_diag(p["lru_wa"][l]).astype(BF16), wx=_block_diag(p["lru_wx"][l]).astype(BF16),
        ba=row(p["lru_ba"]), bx=row(p["lru_bx"]), lam=row(p["lru_lambda"]), gng=row(p["gmlp_norm_g"]),
        ws=ws, wst=jnp.transpose(ws, (0, 2, 1)),
        bs=jnp.repeat(jnp.transpose(p["gmlp_bs"][l]), HEAD_DIM, axis=1))


def _flat(a):
    return a.reshape(a.shape[0] * a.shape[1], a.shape[2])


def _split(a, dil):
    return a.reshape(dil, a.shape[0] // dil, a.shape[1])


def local_step(x, tgt, final_g, depth, chip, layer_weights, projections_done):
    saved = []
    for l in range(depth):
        gain, w_in_l, w_out_l, wts = layer_weights(l, x)
        z, h, *qkvs = in_fwd(x, gain, w_in_l)
        qkvs = [_flat(q) if q.ndim == 3 else q for q in qkvs]
        attn = []
        for q, d in zip(qkvs, PATTERN_DILS):
            o_p, lse_p = attn_fwd(q, d)
            attn.append((o_p, lse_p) if d == 1 else (_split(o_p, d), _split(lse_p, d)))
        y, hl, o, lse, lse4, lse16 = mix_fwd(z, attn, wts)
        saved.append(dict(x=x, z=z, h=h, y=y, hl=hl, o=o, qkvs=qkvs, lses=(lse, _flat(lse4), _flat(lse16)), wts=wts,
                          gain=gain, w_in=w_in_l, w_out=w_out_l))
        x = out_fwd(y, w_out_l, x)

    loss, dx, dfg = loss_head(x, final_g[None, :], tgt)
    grads = {k: [None] * depth for k in
             ("norm_g", "conv_a_w", "conv_r_w", "conv_r_b", "lru_wa", "lru_ba", "lru_wx", "lru_bx",
              "lru_lambda", "gmlp_norm_g", "gmlp_ws", "gmlp_bs")}
    zero = None
    for l in reversed(range(depth)):
        sv = saved[l]
        dy, ddg, do1, do4, do16, dl1, dl4, dl16 = out_bwd(dx, sv["w_out"], sv["z"], sv["o"])
        g_w_out = matmul_tn(sv["y"], dx, 1)
        dqkv = []
        for q, do, lse, dl, d in zip(sv["qkvs"], (do1, _flat(do4), _flat(do16)), sv["lses"],
                                     (dl1, _flat(dl4), _flat(dl16)), PATTERN_DILS):
            g = attn_bwd(q, do, lse, dl, d)
            dqkv.append(g if d == 1 else _split(g, d))
        gain = sv["gain"] if zero is None else sv["gain"] + zero
        dx, dgn, g_w_in_t, ga, gr, gn, gwa, gwx, gws, gbs = layer_bwd(
            dy, sv["z"], sv["hl"], dqkv, ddg, sv["wts"], sv["w_in"], sv["x"], gain, dx, sv["h"])
        zero = projections_done(l, g_w_in_t, g_w_out)
        grads["norm_g"][l] = dgn[0]
        grads["conv_a_w"][l] = ga[:3]
        grads["conv_r_w"][l] = gr[:4]
        grads["conv_r_b"][l] = gr[4]
        grads["lru_ba"][l] = gr[5]
        grads["lru_bx"][l] = gr[6]
        grads["lru_lambda"][l] = gr[7]
        grads["gmlp_norm_g"][l] = gn[0]
        grads["lru_wa"][l] = _diag_blocks(gwa)
        grads["lru_wx"][l] = _diag_blocks(gwx)
        grads["gmlp_ws"][l] = gws
        grads["gmlp_bs"][l] = jnp.transpose(gbs.reshape(GMLP_CHUNK, N_HEADS, HEAD_DIM).sum(-1))
    grads = {k: jnp.stack(v) for k, v in grads.items()}
    grads["final_g"] = dfg[0]
    return loss, dx, grads


MESH = pl.DeviceIdType.MESH
N_CHIPS = 4
N_DEV = 8
ANY = pl.BlockSpec(memory_space=pl.ANY)


def _place():
    x, y, c = lax.axis_index("x"), lax.axis_index("y"), lax.axis_index("c")
    chips = [(1 - x, y), (x, 1 - y), (1 - x, 1 - y)]
    return x, y, c, chips


def _remote(src, dst, ssem, rsem, to):
    return pltpu.make_async_remote_copy(src_ref=src, dst_ref=dst, send_sem=ssem, recv_sem=rsem,
                                        device_id=to, device_id_type=MESH)


HBM = pl.BlockSpec(memory_space=pltpu.HBM)
SEM = pl.BlockSpec(memory_space=pltpu.SEMAPHORE)
DATAFLOW = pltpu.SideEffectType.DATAFLOW_SIDE_EFFECTING
GATHER, SCATTER = "gather", "scatter"


def _chip_copies(mode, src_refs, land_refs, ssem, rsem):
    x, y, c, chips = _place()
    me = 2 * x + y
    n = len(src_refs)
    copies = []
    for k, (cx, cy) in enumerate(chips):
        for a in range(n):
            if mode == GATHER:
                src, dst = src_refs[a], land_refs[a].at[me]
            else:
                src, dst = src_refs[a].at[2 * cx + cy], land_refs[a].at[k]
            copies.append(_remote(src, dst, ssem.at[n * k + a], rsem.at[n * k + a], (cx, cy, c)))
    return copies


def exchange_start(mode, srcs, after, name):
    n = len(srcs)
    if mode == GATHER:
        lands = [lax.empty((N_CHIPS,) + s.shape, s.dtype) for s in srcs]
    else:
        lands = [lax.empty((N_CHIPS - 1,) + s.shape[1:], s.dtype) for s in srcs]
    extra = [] if after is None else [after]

    def body(*refs):
        src_refs, land_refs = refs[:n], refs[n:2 * n]
        ssem, rsem = refs[2 * n + len(extra)], refs[2 * n + len(extra) + 1]
        token = refs[-1]
        for cp in _chip_copies(mode, src_refs, land_refs, ssem, rsem):
            cp.start()
        token[...] = jnp.zeros_like(token)

    arrays = list(srcs) + lands
    return pl.pallas_call(
        body, name=name,
        out_shape=(pltpu.SemaphoreType.DMA((3 * n,)), pltpu.SemaphoreType.DMA((3 * n,)),
                   *[pltpu.HBM(a.shape, a.dtype) for a in arrays], jax.ShapeDtypeStruct((SUBLANES, LANES), F32)),
        in_specs=[HBM] * (2 * n) + [ANY] * len(extra),
        out_specs=(SEM, SEM, *[HBM] * (2 * n), pl.BlockSpec(memory_space=pltpu.VMEM)),
        input_output_aliases={i: 2 + i for i in range(2 * n)},
        compiler_params=pltpu.CompilerParams(has_side_effects=DATAFLOW),
    )(*[pltpu.with_memory_space_constraint(a, pltpu.HBM) for a in arrays], *extra)


def exchange_wait(mode, started, after, name):
    ssem, rsem, *thru, _ = started
    n = len(thru) // 2

    def body(*refs):
        src_refs, land_refs = refs[:n], refs[n:2 * n]
        ssem_ref, rsem_ref = refs[2 * n], refs[2 * n + 1]
        for cp in _chip_copies(mode, src_refs, land_refs, ssem_ref, rsem_ref):
            cp.wait_send()
            cp.wait_recv()

    outs = pl.pallas_call(
        body, name=name,
        out_shape=[pltpu.HBM(a.shape, a.dtype) for a in thru],
        in_specs=[HBM] * (2 * n) + [SEM, SEM, ANY],
        out_specs=[HBM] * (2 * n),
        input_output_aliases={i: i for i in range(2 * n)},
        compiler_params=pltpu.CompilerParams(has_side_effects=DATAFLOW),
    )(*thru, ssem, rsem, after)
    return outs[n:]


def sibling_exchange(p1, p2):
    def body(p1_ref, p2_ref, q1_ref, q2_ref, ssem, rsem):
        x, y, c, _ = _place()
        copies = [_remote(p_ref, q_ref, ssem.at[a], rsem.at[a], (x, y, 1 - c))
                  for a, (p_ref, q_ref) in enumerate(((p1_ref, q1_ref), (p2_ref, q2_ref)))]
        for cp in copies:
            cp.start()
        for cp in copies:
            cp.wait()

    return pl.pallas_call(
        body, name="sibling_exchange",
        in_specs=[ANY, ANY], out_specs=[ANY, ANY],
        out_shape=[jax.ShapeDtypeStruct(p.shape, p.dtype) for p in (p1, p2)],
        scratch_shapes=[pltpu.SemaphoreType.DMA((2,)), pltpu.SemaphoreType.DMA((2,))],
    )(p1, p2)


def gather_all(v):
    m_per, n = v.shape

    def body(x_ref, out_ref, send_sems, recv_sems, local_sem):
        x, y, c, chips = _place()
        me, sibling = (x, y, c), (x, y, 1 - c)

        def rows(px, py, pc):
            return out_ref.at[pl.ds((4 * px + 2 * py + pc) * m_per, m_per), :]

        def copy(k, block, to, src=None):
            return _remote(rows(*block) if src is None else src, rows(*block), send_sems.at[k], recv_sems.at[k], to)

        mine = pltpu.make_async_copy(x_ref, rows(*me), local_sem)
        mine.start()
        first = [copy(0, me, sibling, src=x_ref)]
        first += [copy(1 + j, me, (*chip, c), src=x_ref) for j, chip in enumerate(chips)]
        for cp in first:
            cp.start()
        passed = [copy(4 + j, (*chip, c), sibling) for j, chip in enumerate(chips)]
        for j, chip in enumerate(chips):
            copy(1 + j, (*chip, c), me).wait_recv()
            passed[j].start()
        copy(0, sibling, me).wait_recv()
        for j, chip in enumerate(chips):
            copy(4 + j, (*chip, 1 - c), me).wait_recv()
        for cp in first + passed:
            cp.wait_send()
        mine.wait()

    return pl.pallas_call(
        body, name="gather_all",
        out_shape=jax.ShapeDtypeStruct((N_DEV * m_per, n), v.dtype),
        in_specs=[pl.BlockSpec(memory_space=pltpu.VMEM)],
        out_specs=pl.BlockSpec(memory_space=pltpu.VMEM),
        scratch_shapes=[pltpu.SemaphoreType.DMA((7,)), pltpu.SemaphoreType.DMA((7,)), pltpu.SemaphoreType.DMA],
        compiler_params=pltpu.CompilerParams(vmem_limit_bytes=VMEM_LIMIT),
    )(v)


TILE_ROWS = 256


def sum_partials(own, parts):
    r, c = own.shape
    k = parts.shape[0]

    def body(o_ref, p_ref, out_ref):
        acc = o_ref[...]
        for i in range(k):
            acc = acc + p_ref[i].astype(F32)
        out_ref[...] = acc

    return pl.pallas_call(
        body, name="sum_partials", grid=(r // TILE_ROWS,),
        in_specs=[pl.BlockSpec((TILE_ROWS, c), lambda i: (i, 0)), pl.BlockSpec((k, TILE_ROWS, c), lambda i: (0, i, 0))],
        out_specs=pl.BlockSpec((TILE_ROWS, c), lambda i: (i, 0)),
        out_shape=jax.ShapeDtypeStruct((r, c), F32),
        compiler_params=_params(("parallel",)),
    )(own, parts)


def adamw(w, ga, gb, m, v):
    n, r, c = w.shape
    tile = max(t for t in range(SUBLANES, TILE_ROWS + 1, SUBLANES) if r % t == 0)

    def body(w_ref, ga_ref, gb_ref, m_ref, v_ref, g_ref, d_ref, m2_ref, v2_ref):
        g = ga_ref[...] + gb_ref[...]
        g_ref[...] = g
        m2 = ADAM_B1 * m_ref[...] + (1.0 - ADAM_B1) * g
        v2 = ADAM_B2 * v_ref[...] + (1.0 - ADAM_B2) * (g * g)
        m2_ref[...] = m2
        v2_ref[...] = v2
        m_hat = m2 / (1.0 - ADAM_B1 ** ADAM_STEP)
        v_hat = v2 / (1.0 - ADAM_B2 ** ADAM_STEP)
        d_ref[...] = -ADAM_LR * (m_hat / (jnp.sqrt(v_hat) + ADAM_EPS) + ADAM_WD * w_ref[...])

    spec = pl.BlockSpec((1, tile, c), lambda j, i: (j, i, 0))
    return pl.pallas_call(
        body, name="adamw", grid=(n, r // tile),
        in_specs=[spec] * 5, out_specs=[spec] * 4,
        out_shape=[jax.ShapeDtypeStruct((n, r, c), F32)] * 4,
        compiler_params=_params(("parallel", "parallel")),
    )(w, ga, gb, m, v)


REPLICATED = ("norm_g", "conv_r_b", "lru_wa", "lru_ba", "lru_wx", "lru_bx", "lru_lambda", "gmlp_norm_g",
              "gmlp_ws", "gmlp_bs", "final_g")
CHIP_SHARDED_SMALL = ("conv_a_w", "conv_r_w")
PACK_LANES = 128


def _pack(arrays):
    flat = jnp.concatenate([a.reshape(-1) for a in arrays])
    pad = (-flat.shape[0]) % (TILE_ROWS * PACK_LANES)
    return jnp.pad(flat, (0, pad)).reshape(-1, PACK_LANES)


def _unpack(packed, shapes):
    flat = packed.reshape(-1)
    out, off = [], 0
    for shp in shapes:
        n = math.prod(shp)
        out.append(flat[off:off + n].reshape(shp))
        off += n
    return out


def kernel(x, norm_g, w_in, conv_a_w, conv_r_w, conv_r_b, lru_wa, lru_ba, lru_wx, lru_bx, lru_lambda, gmlp_norm_g, gmlp_ws, gmlp_bs, w_out, final_g, loss_target, m_norm_g, m_w_in, m_conv_a_w, m_conv_r_w, m_conv_r_b, m_lru_wa, m_lru_ba, m_lru_wx, m_lru_bx, m_lru_lambda, m_gmlp_norm_g, m_gmlp_ws, m_gmlp_bs, m_w_out, m_final_g, v_norm_g, v_w_in, v_conv_a_w, v_conv_r_w, v_conv_r_b, v_lru_wa, v_lru_ba, v_lru_wx, v_lru_bx, v_lru_lambda, v_gmlp_norm_g, v_gmlp_ws, v_gmlp_bs, v_w_out, v_final_g):
    names = ("norm_g", "w_in", "conv_a_w", "conv_r_w", "conv_r_b", "lru_wa", "lru_ba", "lru_wx", "lru_bx",
             "lru_lambda", "gmlp_norm_g", "gmlp_ws", "gmlp_bs", "w_out", "final_g")
    w = dict(zip(names, (norm_g, w_in, conv_a_w, conv_r_w, conv_r_b, lru_wa, lru_ba, lru_wx, lru_bx, lru_lambda,
                         gmlp_norm_g, gmlp_ws, gmlp_bs, w_out, final_g)))
    m = dict(zip(names, (m_norm_g, m_w_in, m_conv_a_w, m_conv_r_w, m_conv_r_b, m_lru_wa, m_lru_ba, m_lru_wx, m_lru_bx,
                         m_lru_lambda, m_gmlp_norm_g, m_gmlp_ws, m_gmlp_bs, m_w_out, m_final_g)))
    v = dict(zip(names, (v_norm_g, v_w_in, v_conv_a_w, v_conv_r_w, v_conv_r_b, v_lru_wa, v_lru_ba, v_lru_wx, v_lru_bx,
                         v_lru_lambda, v_gmlp_norm_g, v_gmlp_ws, v_gmlp_bs, v_w_out, v_final_g)))
    depth, _, in_cols = w_in.shape
    out_rows = w_out.shape[1]
    conv_ch = conv_a_w.shape[2]
    chip = 2 * lax.axis_index("x") + lax.axis_index("y")

    taps = conv_a_w.shape[1] + conv_r_w.shape[1]
    w_in_t, m_w_in_t, v_w_in_t = (jnp.swapaxes(a, 1, 2) for a in (w_in, m_w_in, v_w_in))
    w_in_h, w_out_h = w_in_t.astype(BF16), w_out.astype(BF16)
    conv_own = jnp.concatenate([conv_a_w, conv_r_w], axis=1).reshape(depth * taps, conv_ch)
    gathers, token = [], None
    for l in range(depth):
        srcs = [w_in_h[l], w_out_h[l]] + ([conv_own] if l == 0 else [])
        gathers.append(exchange_start(GATHER, srcs, token, f"gather_start_{l}"))
        token = gathers[-1][-1]
    p = dict(w)

    def with_own(land, own):
        return lax.dynamic_update_slice(land, own[None], (chip,) + (0,) * own.ndim)

    def layer_weights(l, x_l):
        lands = exchange_wait(GATHER, gathers[l], x_l, f"gather_wait_{l}")
        w_in_l = with_own(lands[0], w_in_h[l]).reshape(D_IN, D_MODEL)
        w_out_l = with_own(lands[1], w_out_h[l]).reshape(D_MIX, D_MODEL)
        gain = norm_g[l][None, :]
        if l == 0:
            conv = with_own(lands[2], conv_own).reshape(N_CHIPS, depth, taps, conv_ch)
            conv = conv.transpose(1, 2, 0, 3).reshape(depth, taps, GROUP_W)
            p["conv_a_w"] = conv[:, :conv_a_w.shape[1]]
            p["conv_r_w"] = conv[:, conv_a_w.shape[1]:]
            gain = gain + token[0, 0]
        return gain, w_in_l, w_out_l, _layer_weights(p, l)

    scatters, owns = [None] * depth, [None] * depth

    def projections_done(l, g_w_in_t, g_w_out):
        gi = g_w_in_t.reshape(N_CHIPS, in_cols, D_MODEL)
        go = g_w_out.reshape(N_CHIPS, out_rows, D_MODEL)
        owns[l] = (lax.dynamic_index_in_dim(gi, chip, axis=0, keepdims=False),
                   lax.dynamic_index_in_dim(go, chip, axis=0, keepdims=False))
        scatters[l] = exchange_start(SCATTER, [gi.astype(BF16), go.astype(BF16)], None, f"scatter_start_{l}")
        return scatters[l][-1][0, 0]

    loss8, dx, grads = local_step(x[0], loss_target[0], final_g, depth, chip.reshape(1), layer_weights,
                                  projections_done)
    loss = lax.psum(loss8[0, 0], ("x", "y", "c"))

    parts = [exchange_wait(SCATTER, scatters[l], dx, f"scatter_wait_{l}") for l in range(depth)]
    r1 = jnp.concatenate([parts[l][0] for l in range(depth)], axis=1)
    r2 = jnp.concatenate([parts[l][1] for l in range(depth)], axis=1)
    own1 = jnp.concatenate([owns[l][0] for l in range(depth)], axis=0)
    own2 = jnp.concatenate([owns[l][1] for l in range(depth)], axis=0)
    p1 = sum_partials(own1, r1)
    p2 = sum_partials(own2, r2)
    q1, q2 = sibling_exchange(p1, p2)
    res = {}
    res["w_in"] = [jnp.swapaxes(t, 1, 2) for t in
                   adamw(w_in_t, p1.reshape(w_in_t.shape), q1.reshape(w_in_t.shape), m_w_in_t, v_w_in_t)]
    res["w_out"] = adamw(w_out, p2.reshape(w_out.shape), q2.reshape(w_out.shape), m_w_out, v_w_out)

    small = REPLICATED + CHIP_SHARDED_SMALL
    packed = _pack([grads[k] for k in small])
    rows = packed.shape[0]
    allp = gather_all(packed).reshape(N_DEV, rows, PACK_LANES)
    total = sum_partials(allp[0], allp[1:])
    gs = dict(zip(small, _unpack(total, [grads[k].shape for k in small])))
    for k in CHIP_SHARDED_SMALL:
        gs[k] = lax.dynamic_slice_in_dim(gs[k], chip * conv_ch, conv_ch, axis=2)
    wp, gp, mp, vp = (_pack([d[k] for k in small])[None] for d in (w, gs, m, v))
    outs = adamw(wp, gp, jnp.zeros_like(gp), mp, vp)
    shapes = [w[k].shape for k in small]
    unpacked = [_unpack(o, shapes) for o in outs]
    for i, k in enumerate(small):
        res[k] = [u[i] for u in unpacked]

    return (loss, dx[None], *[res[k][0] for k in names], *[res[k][1] for k in names],
            *[res[k][2] for k in names], *[res[k][3] for k in names])
```

```python
import functools
import math

import jax
import jax.numpy as jnp
import numpy as np
from jax import lax
from jax.experimental import pallas as pl
from jax.experimental.pallas import tpu as pltpu

F32 = jnp.float32
BF16 = jnp.bfloat16

D_MODEL = 1024
GROUP_W = 256
N_HEADS = 4
HEAD_DIM = 64
N_CHUNKS = 13
D_IN = N_CHUNKS * GROUP_W
D_MIX = 4 * GROUP_W
NORM_EPS = 1e-6
RG_C = 8.0
GMLP_CHUNK = 128
ATTN_BLOCK = 128
PATTERN_DILS = (1, 4, 16)
N_PATTERNS = len(PATTERN_DILS)
ALIBI_SLOPES = tuple(2.0 ** (-8.0 * (h + 1) / N_HEADS) for h in range(N_HEADS))
ATTN_SCALE = 1.0 / math.sqrt(HEAD_DIM)
NEG_BIG = -1e30

ADAM_LR = 0.001
ADAM_B1 = 0.9
ADAM_B2 = 0.999
ADAM_EPS = 1e-08
ADAM_WD = 0.01
ADAM_STEP = 10

C_AX, C_AB, C_AC, C_AG, C_RX, C_RG, C_CU, C_CV, C_CG, C_DQ, C_DK, C_DV, C_DG = range(13)

SUBLANES = 8
LANES = 128
VMEM_LIMIT = 56 * 1024 * 1024
TILE_IN = 512
TILE_MIX = 256
TILE_DW = 512
ATTN_QB = 4
GELU_K0 = math.sqrt(2.0 / math.pi)
GELU_K1 = 0.044715


def _params(sem):
    return pltpu.CompilerParams(dimension_semantics=sem, vmem_limit_bytes=VMEM_LIMIT)


def _silu(x):
    return x * jax.nn.sigmoid(x)


def _dsilu(x):
    s = jax.nn.sigmoid(x)
    return s * (1.0 + x * (1.0 - s))


def _gelu(x):
    return 0.5 * x * (1.0 + jnp.tanh(GELU_K0 * (x + GELU_K1 * x * x * x)))


def _gelu_and_grad(x):
    t = jnp.tanh(GELU_K0 * (x + GELU_K1 * x * x * x))
    g = 0.5 * x * (1.0 + t)
    dg = 0.5 * (1.0 + t) + 0.5 * x * (1.0 - t * t) * GELU_K0 * (1.0 + 3.0 * GELU_K1 * x * x)
    return g, dg


def _neg_expm1(x):
    series = x * (1.0 + x * (0.5 + x * (1.0 / 6.0 + x * (1.0 / 24.0 + x * (1.0 / 120.0)))))
    return -jnp.where(x > -0.05, series, jnp.exp(x) - 1.0)


def _shift_down(v, halo, k):
    r = pltpu.roll(v, k, 0)
    rh = pltpu.roll(halo, k, 0)
    row = lax.broadcasted_iota(jnp.int32, halo.shape, 0)
    top = jnp.where(row < k, rh, r[:SUBLANES])
    return jnp.concatenate([top, r[SUBLANES:]], axis=0)


def _shift_up(v, halo, k):
    t = v.shape[0]
    r = pltpu.roll(v, t - k, 0)
    rh = pltpu.roll(halo, SUBLANES - k, 0)
    row = lax.broadcasted_iota(jnp.int32, halo.shape, 0)
    bot = jnp.where(row >= SUBLANES - k, rh, r[t - SUBLANES:])
    return jnp.concatenate([r[:t - SUBLANES], bot], axis=0)


def _scan_causal(a, b):
    t = a.shape[0]
    row = lax.broadcasted_iota(jnp.int32, a.shape, 0)
    d = 1
    while d < t:
        m = row >= d
        a_s = jnp.where(m, pltpu.roll(a, d, 0), 1.0)
        b_s = jnp.where(m, pltpu.roll(b, d, 0), 0.0)
        b = a * b_s + b
        a = a * a_s
        d *= 2
    return a, b


def _scan_anticausal(a, b):
    t = a.shape[0]
    row = lax.broadcasted_iota(jnp.int32, a.shape, 0)
    d = 1
    while d < t:
        m = row < t - d
        a_s = jnp.where(m, pltpu.roll(a, t - d, 0), 1.0)
        b_s = jnp.where(m, pltpu.roll(b, t - d, 0), 0.0)
        b = a * b_s + b
        a = a * a_s
        d *= 2
    return a, b


def _head_of_lane(shape):
    return lax.broadcasted_iota(jnp.int32, shape, len(shape) - 1) // HEAD_DIM


def _put_row(acc_shape, k, row_vec):
    row = lax.broadcasted_iota(jnp.int32, acc_shape, 0)
    return jnp.where(row == k, jnp.broadcast_to(row_vec, acc_shape), 0.0)


def _dot(a, b):
    return jnp.dot(a, b, preferred_element_type=F32)


def _dot_nt(a, b):
    return lax.dot_general(a, b, (((1,), (1,)), ((), ())), preferred_element_type=F32)


def _dot_tn(a, b):
    return lax.dot_general(a, b, (((0,), (0,)), ((), ())), preferred_element_type=F32)


def _deinterleave_store(val, stage, outs):
    t, c = val.shape
    for hh in range(c // LANES):
        stage[hh][...] = val[:, hh * LANES:(hh + 1) * LANES].astype(F32)
    for dil, ref in outs:
        for r in range(dil):
            for hh in range(c // LANES):
                ref[r, :, hh * LANES:(hh + 1) * LANES] = stage[hh][pl.ds(r, t // dil, stride=dil), :].astype(ref.dtype)


def _interleave_load(ref, dil, stage):
    _, n, c = ref.shape
    for r in range(dil):
        for hh in range(c // LANES):
            stage[hh][pl.ds(r, n, stride=dil), :] = ref[r, :, hh * LANES:(hh + 1) * LANES].astype(F32)
    return jnp.concatenate([stage[hh][...] for hh in range(c // LANES)], axis=1)


def _stage_scratch(tile, cols, copies):
    return [pltpu.VMEM((tile, LANES), F32)] * (copies * (cols // LANES))


def _by_residue(s, dil, cols, dtype):
    return jax.ShapeDtypeStruct((dil, s // dil, cols), dtype)


def _residue_block(dil, tile, cols):
    return pl.BlockSpec((dil, tile // dil, cols), lambda i: (0, i, 0))


def in_fwd(x, g, w):
    s = x.shape[0]
    qkv_w = 3 * GROUP_W

    def body(x_ref, g_ref, w_ref, z_ref, h_ref, qkv1_ref, qkv4_ref, qkv16_ref, *stage):
        xv = x_ref[...]
        rs = lax.rsqrt(jnp.mean(xv * xv, axis=-1, keepdims=True) + NORM_EPS)
        h = (xv * rs * g_ref[...]).astype(BF16)
        h_ref[...] = h
        z = _dot_nt(h, w_ref[...])
        z_ref[...] = z
        qkv = z[:, C_DQ * GROUP_W:(C_DV + 1) * GROUP_W]
        qkv1_ref[...] = qkv.astype(BF16)
        _deinterleave_store(qkv, stage, ((PATTERN_DILS[1], qkv4_ref), (PATTERN_DILS[2], qkv16_ref)))

    return pl.pallas_call(
        body, name="in_fwd", grid=(s // TILE_IN,),
        in_specs=[pl.BlockSpec((TILE_IN, D_MODEL), lambda i: (i, 0)),
                  pl.BlockSpec((1, D_MODEL), lambda i: (0, 0)),
                  pl.BlockSpec((D_IN, D_MODEL), lambda i: (0, 0))],
        out_specs=[pl.BlockSpec((TILE_IN, D_IN), lambda i: (i, 0)),
                   pl.BlockSpec((TILE_IN, D_MODEL), lambda i: (i, 0)),
                   pl.BlockSpec((TILE_IN, qkv_w), lambda i: (i, 0)),
                   _residue_block(PATTERN_DILS[1], TILE_IN, qkv_w),
                   _residue_block(PATTERN_DILS[2], TILE_IN, qkv_w)],
        out_shape=[jax.ShapeDtypeStruct((s, D_IN), F32), jax.ShapeDtypeStruct((s, D_MODEL), BF16),
                   jax.ShapeDtypeStruct((s, qkv_w), BF16),
                   _by_residue(s, PATTERN_DILS[1], qkv_w, BF16), _by_residue(s, PATTERN_DILS[2], qkv_w, BF16)],
        scratch_shapes=_stage_scratch(TILE_IN, qkv_w, 1),
        compiler_params=_params(("parallel",)),
    )(x, g, w)


def out_fwd(y, w, x):
    s = x.shape[0]

    def body(y_ref, w_ref, x_ref, o_ref):
        o_ref[...] = x_ref[...] + _dot(y_ref[...], w_ref[...])

    return pl.pallas_call(
        body, name="out_fwd", grid=(s // TILE_IN,),
        in_specs=[pl.BlockSpec((TILE_IN, D_MIX), lambda i: (i, 0)),
                  pl.BlockSpec((D_MIX, D_MODEL), lambda i: (0, 0)),
                  pl.BlockSpec((TILE_IN, D_MODEL), lambda i: (i, 0))],
        out_specs=pl.BlockSpec((TILE_IN, D_MODEL), lambda i: (i, 0)),
        out_shape=jax.ShapeDtypeStruct((s, D_MODEL), F32),
        compiler_params=_params(("parallel",)),
    )(y, w, x)


def out_bwd(dx, w, z, o):
    s = dx.shape[0]
    abc = 3 * GROUP_W

    def body(dx_ref, w_ref, dg_ref, o_ref, dy_ref, ddg_ref, do1_ref, do4_ref, do16_ref, dl1_ref, dl4_ref, dl16_ref,
             *stage):
        stage_a, stage_b = stage[:2], stage[2:]
        dy = _dot_nt(dx_ref[...].astype(BF16), w_ref[...])
        dy_ref[...] = dy[:, :abc]
        dyd = dy[:, abc:]
        head = _head_of_lane((TILE_IN, GROUP_W))
        dg = dg_ref[...]
        o = o_ref[...]
        do = dyd * _silu(dg)
        ddg_ref[...] = dyd * o * _dsilu(dg)
        prod = do * o
        dl = jnp.zeros_like(prod)
        for h in range(N_HEADS):
            sm = jnp.sum(jnp.where(head == h, prod, 0.0), axis=-1, keepdims=True)
            dl = jnp.where(head == h, sm, dl)
        do1_ref[...] = do.astype(BF16)
        dl1_ref[...] = dl
        _deinterleave_store(do, stage_a, ((PATTERN_DILS[1], do4_ref), (PATTERN_DILS[2], do16_ref)))
        _deinterleave_store(dl, stage_b, ((PATTERN_DILS[1], dl4_ref), (PATTERN_DILS[2], dl16_ref)))

    row = pl.BlockSpec((TILE_IN, GROUP_W), lambda i: (i, 0))
    r4 = _residue_block(PATTERN_DILS[1], TILE_IN, GROUP_W)
    r16 = _residue_block(PATTERN_DILS[2], TILE_IN, GROUP_W)
    return pl.pallas_call(
        body, name="out_bwd", grid=(s // TILE_IN,),
        in_specs=[pl.BlockSpec((TILE_IN, D_MODEL), lambda i: (i, 0)),
                  pl.BlockSpec((D_MIX, D_MODEL), lambda i: (0, 0)),
                  pl.BlockSpec((TILE_IN, GROUP_W), lambda i: (i, C_DG)), row],
        out_specs=[pl.BlockSpec((TILE_IN, abc), lambda i: (i, 0)), row, row, r4, r16, row, r4, r16],
        out_shape=[jax.ShapeDtypeStruct((s, abc), F32), jax.ShapeDtypeStruct((s, GROUP_W), F32),
                   jax.ShapeDtypeStruct((s, GROUP_W), BF16),
                   _by_residue(s, PATTERN_DILS[1], GROUP_W, BF16), _by_residue(s, PATTERN_DILS[2], GROUP_W, BF16),
                   jax.ShapeDtypeStruct((s, GROUP_W), F32),
                   _by_residue(s, PATTERN_DILS[1], GROUP_W, F32), _by_residue(s, PATTERN_DILS[2], GROUP_W, F32)],
        scratch_shapes=_stage_scratch(TILE_IN, GROUP_W, 2),
        compiler_params=_params(("parallel",)),
    )(dx, w, z, o)


def in_bwd(dz, w, x, g, dx_next):
    s = x.shape[0]

    def body(dz_ref, w_ref, x_ref, g_ref, dxn_ref, dx_ref, dg_ref):
        @pl.when(pl.program_id(0) == 0)
        def _():
            dg_ref[...] = jnp.zeros_like(dg_ref)

        dh = _dot(dz_ref[...], w_ref[...])
        xv = x_ref[...]
        rs = lax.rsqrt(jnp.mean(xv * xv, axis=-1, keepdims=True) + NORM_EPS)
        xh = xv * rs
        dg_ref[...] += _put_row(dg_ref.shape, 0, jnp.sum(dh * xh, axis=0, keepdims=True))
        dn = dh * g_ref[...]
        dx_ref[...] = dxn_ref[...] + rs * (dn - xh * jnp.mean(dn * xh, axis=-1, keepdims=True))

    return pl.pallas_call(
        body, name="in_bwd", grid=(s // TILE_IN,),
        in_specs=[pl.BlockSpec((TILE_IN, D_IN), lambda i: (i, 0)),
                  pl.BlockSpec((D_IN, D_MODEL), lambda i: (0, 0)),
                  pl.BlockSpec((TILE_IN, D_MODEL), lambda i: (i, 0)),
                  pl.BlockSpec((1, D_MODEL), lambda i: (0, 0)),
                  pl.BlockSpec((TILE_IN, D_MODEL), lambda i: (i, 0))],
        out_specs=[pl.BlockSpec((TILE_IN, D_MODEL), lambda i: (i, 0)),
                   pl.BlockSpec((SUBLANES, D_MODEL), lambda i: (0, 0))],
        out_shape=[jax.ShapeDtypeStruct((s, D_MODEL), F32), jax.ShapeDtypeStruct((SUBLANES, D_MODEL), F32)],
        compiler_params=_params(("arbitrary",)),
    )(dz, w, x, g, dx_next)


def matmul_tn(a, b, n_split):
    s, m = a.shape
    n = b.shape[1]
    tn = n // n_split

    def body(a_ref, b_ref, o_ref):
        @pl.when(pl.program_id(1) == 0)
        def _():
            o_ref[...] = jnp.zeros_like(o_ref)

        o_ref[...] += _dot_tn(a_ref[...], b_ref[...].astype(BF16))

    return pl.pallas_call(
        body, name="matmul_tn", grid=(n_split, s // TILE_DW),
        in_specs=[pl.BlockSpec((TILE_DW, m), lambda j, k: (k, 0)),
                  pl.BlockSpec((TILE_DW, tn), lambda j, k: (k, j))],
        out_specs=pl.BlockSpec((m, tn), lambda j, k: (0, j)),
        out_shape=jax.ShapeDtypeStruct((m, n), F32),
        compiler_params=_params(("parallel", "arbitrary")),
    )(a, b)


def grad_w_in(h, dz, chip):
    s = h.shape[0]
    rows = D_IN // N_CHIPS

    def body(chip_ref, h_ref, dz_ref, staged_ref, own_ref, acc):
        k = pl.program_id(0)

        @pl.when(k == 0)
        def _():
            acc[...] = jnp.zeros_like(acc)

        acc[...] += _dot_tn(dz_ref[...], h_ref[...])

        @pl.when(k == s // TILE_DW - 1)
        def _():
            for j in range(N_CHIPS):
                part = acc[j * rows:(j + 1) * rows, :]
                staged_ref[j] = part.astype(BF16)

                @pl.when(chip_ref[0] == j)
                def _():
                    own_ref[...] = part

    return pl.pallas_call(
        body, name="grad_w_in",
        grid_spec=pltpu.PrefetchScalarGridSpec(
            num_scalar_prefetch=1, grid=(s // TILE_DW,),
            in_specs=[pl.BlockSpec((TILE_DW, D_MODEL), lambda k, c: (k, 0)),
                      pl.BlockSpec((TILE_DW, D_IN), lambda k, c: (k, 0))],
            out_specs=[pl.BlockSpec((N_CHIPS, rows, D_MODEL), lambda k, c: (0, 0, 0)),
                       pl.BlockSpec((rows, D_MODEL), lambda k, c: (0, 0))],
            scratch_shapes=[pltpu.VMEM((D_IN, D_MODEL), F32)]),
        out_shape=[jax.ShapeDtypeStruct((N_CHIPS, rows, D_MODEL), BF16), jax.ShapeDtypeStruct((rows, D_MODEL), F32)],
        compiler_params=_params(("arbitrary",)),
    )(chip, h, dz)


def loss_head(x, g, tgt):
    s = x.shape[0]

    def body(x_ref, g_ref, t_ref, l_ref, dx_ref, dg_ref):
        @pl.when(pl.program_id(0) == 0)
        def _():
            l_ref[...] = jnp.zeros_like(l_ref)
            dg_ref[...] = jnp.zeros_like(dg_ref)

        xv = x_ref[...]
        gv = g_ref[...]
        rs = lax.rsqrt(jnp.mean(xv * xv, axis=-1, keepdims=True) + NORM_EPS)
        xh = xv * rs
        e = xh * gv - t_ref[...]
        part = 0.5 * jnp.sum(jnp.mean(e * e, axis=-1, keepdims=True), axis=0, keepdims=True)
        l_ref[...] += jnp.broadcast_to(part, l_ref.shape)
        dy = e * (1.0 / D_MODEL)
        dg_ref[...] += _put_row(dg_ref.shape, 0, jnp.sum(dy * xh, axis=0, keepdims=True))
        dn = dy * gv
        dx_ref[...] = rs * (dn - xh * jnp.mean(dn * xh, axis=-1, keepdims=True))

    return pl.pallas_call(
        body, name="loss_head", grid=(s // TILE_IN,),
        in_specs=[pl.BlockSpec((TILE_IN, D_MODEL), lambda i: (i, 0)),
                  pl.BlockSpec((1, D_MODEL), lambda i: (0, 0)),
                  pl.BlockSpec((TILE_IN, D_MODEL), lambda i: (i, 0))],
        out_specs=[pl.BlockSpec((SUBLANES, 128), lambda i: (0, 0)),
                   pl.BlockSpec((TILE_IN, D_MODEL), lambda i: (i, 0)),
                   pl.BlockSpec((SUBLANES, D_MODEL), lambda i: (0, 0))],
        out_shape=[jax.ShapeDtypeStruct((SUBLANES, 128), F32), jax.ShapeDtypeStruct((s, D_MODEL), F32),
                   jax.ShapeDtypeStruct((SUBLANES, D_MODEL), F32)],
        compiler_params=_params(("arbitrary",)),
    )(x, g, tgt)


def _attn_bias(dil):
    qi = np.arange(ATTN_BLOCK)[:, None]
    ki = np.arange(2 * ATTN_BLOCK)[None, :]
    delta = qi + ATTN_BLOCK - ki
    band = (delta >= 0) & (delta <= ATTN_BLOCK)
    out = np.empty((2, N_HEADS, ATTN_BLOCK, 2 * ATTN_BLOCK), np.float32)
    for f in range(2):
        ok = band & ((ki >= ATTN_BLOCK) | (f == 0))
        for h in range(N_HEADS):
            out[f, h] = np.where(ok, -ALIBI_SLOPES[h] * dil * delta, NEG_BIG)
    return jnp.asarray(out.reshape(2, N_HEADS * ATTN_BLOCK, 2 * ATTN_BLOCK))


def _stack_heads(a, head):
    return jnp.concatenate([jnp.where(head == h, a, jnp.zeros_like(a)) for h in range(N_HEADS)], axis=0)


def _unstack_heads(a, head):
    out = a[:ATTN_BLOCK]
    for h in range(1, N_HEADS):
        out = jnp.where(head == h, a[h * ATTN_BLOCK:(h + 1) * ATTN_BLOCK], out)
    return out


def _head_column(a):
    return jnp.concatenate([a[:, h * HEAD_DIM:h * HEAD_DIM + 1] for h in range(N_HEADS)], axis=0)


def _attn_specs(n_blocks):
    rows = ATTN_QB * ATTN_BLOCK
    cur = lambda c: pl.BlockSpec((rows, GROUP_W), lambda n, c=c: (n, c))
    prev = lambda c: pl.BlockSpec((ATTN_BLOCK, GROUP_W), lambda n, c=c: (jnp.maximum(n * ATTN_QB - 1, 0), c))
    nxt = lambda c: pl.BlockSpec((ATTN_BLOCK, GROUP_W),
                                 lambda n, c=c: (jnp.minimum(n * ATTN_QB + ATTN_QB, n_blocks - 1), c))
    return cur, prev, nxt


def _keys(kp_ref, k_ref, j):
    prev = kp_ref[...] if j == 0 else k_ref[(j - 1) * ATTN_BLOCK:j * ATTN_BLOCK, :]
    return jnp.concatenate([prev, k_ref[j * ATTN_BLOCK:(j + 1) * ATTN_BLOCK, :]], axis=0)


def attn_fwd(qkv, dil):
    s = qkv.shape[0]
    n_blocks = s // ATTN_BLOCK
    bps = n_blocks // dil
    rows = ATTN_QB * ATTN_BLOCK

    def body(q_ref, kp_ref, k_ref, vp_ref, v_ref, bias_ref, o_ref, lse_ref):
        n = pl.program_id(0)
        head = _head_of_lane((ATTN_BLOCK, GROUP_W))
        for j in range(ATTN_QB):
            sl = slice(j * ATTN_BLOCK, (j + 1) * ATTN_BLOCK)
            first = (((n * ATTN_QB + j) % bps) == 0).astype(jnp.int32)
            qs = _stack_heads(q_ref[sl, :], head)
            sc = _dot_nt(qs, _keys(kp_ref, k_ref, j)) * ATTN_SCALE + bias_ref[first]
            m = jnp.max(sc, axis=-1, keepdims=True)
            pr = jnp.exp(sc - m)
            l = jnp.sum(pr, axis=-1, keepdims=True)
            oh = _dot(pr.astype(BF16), _keys(vp_ref, v_ref, j)) / l
            o_ref[sl, :] = _unstack_heads(oh, head)
            lse_ref[sl, :] = _unstack_heads(jnp.broadcast_to(m + jnp.log(l), oh.shape), head)

    cur, prev, _ = _attn_specs(n_blocks)
    bias = _attn_bias(dil)
    out = jax.ShapeDtypeStruct((s, GROUP_W), F32)
    return pl.pallas_call(
        body, name=f"attn_fwd_d{dil}", grid=(n_blocks // ATTN_QB,),
        in_specs=[cur(0), prev(1), cur(1), prev(2), cur(2), pl.BlockSpec(bias.shape, lambda n: (0, 0, 0))],
        out_specs=[cur(0), cur(0)],
        out_shape=[out, out],
        compiler_params=_params(("parallel",)),
    )(qkv, qkv, qkv, qkv, qkv, bias)


def attn_bwd(qkv, do, lse, dlt, dil):
    s = qkv.shape[0]
    n_blocks = s // ATTN_BLOCK
    bps = n_blocks // dil
    rows = ATTN_QB * ATTN_BLOCK

    def body(q_ref, qn_ref, kp_ref, k_ref, vp_ref, v_ref, do_ref, don_ref, lse_ref, lsen_ref, dl_ref, dln_ref,
             bias_ref, out_ref, dk_acc, dv_acc):
        n = pl.program_id(0)
        head = _head_of_lane((ATTN_BLOCK, GROUP_W))
        dk_acc[...] = jnp.zeros_like(dk_acc)
        dv_acc[...] = jnp.zeros_like(dv_acc)

        def pair(qj, doj, lsej, dlj, kk, vv, bias, keep):
            qs = _stack_heads(qj, head)
            dos = _stack_heads(doj, head)
            sc = _dot_nt(qs, kk) * ATTN_SCALE + bias
            if keep is None:
                pr = jnp.exp(sc - _head_column(lsej))
            else:
                pr = jnp.exp(jnp.minimum(sc - _head_column(lsej), 0.0)) * keep
            dp = _dot_nt(dos, vv)
            ds = (pr * (dp - _head_column(dlj)) * ATTN_SCALE).astype(BF16)
            return ds, _dot_tn(ds, qs), _dot_tn(pr.astype(BF16), dos)

        for j in range(ATTN_QB):
            sl = slice(j * ATTN_BLOCK, (j + 1) * ATTN_BLOCK)
            first = (((n * ATTN_QB + j) % bps) == 0).astype(jnp.int32)
            kk = _keys(kp_ref, k_ref, j)
            ds, dks, dvs = pair(q_ref[sl, :], do_ref[sl, :], lse_ref[sl, :], dl_ref[sl, :],
                                kk, _keys(vp_ref, v_ref, j), bias_ref[first], None)
            out_ref[sl, 0:GROUP_W] = _unstack_heads(_dot(ds, kk), head)
            acc = slice(j * ATTN_BLOCK, (j + 2) * ATTN_BLOCK)
            dk_acc[acc, :] += dks
            dv_acc[acc, :] += dvs

        nxt = n * ATTN_QB + ATTN_QB
        valid = ((nxt < n_blocks) & ((nxt % bps) != 0)).astype(F32)
        last = slice((ATTN_QB - 1) * ATTN_BLOCK, ATTN_QB * ATTN_BLOCK)
        _, dks, dvs = pair(qn_ref[...], don_ref[...], lsen_ref[...], dln_ref[...], k_ref[last, :], v_ref[last, :],
                           bias_ref[0][:, :ATTN_BLOCK], valid)
        acc = slice(ATTN_QB * ATTN_BLOCK, (ATTN_QB + 1) * ATTN_BLOCK)
        dk_acc[acc, :] += dks
        dv_acc[acc, :] += dvs
        out_ref[:, GROUP_W:2 * GROUP_W] = dk_acc[ATTN_BLOCK:, :]
        out_ref[:, 2 * GROUP_W:3 * GROUP_W] = dv_acc[ATTN_BLOCK:, :]

    cur, prev, nxt = _attn_specs(n_blocks)
    bias = _attn_bias(dil)
    return pl.pallas_call(
        body, name=f"attn_bwd_d{dil}", grid=(n_blocks // ATTN_QB,),
        in_specs=[cur(0), nxt(0), prev(1), cur(1), prev(2), cur(2), cur(0), nxt(0), cur(0), nxt(0), cur(0), nxt(0),
                  pl.BlockSpec(bias.shape, lambda n: (0, 0, 0))],
        out_specs=pl.BlockSpec((rows, 3 * GROUP_W), lambda n: (n, 0)),
        out_shape=jax.ShapeDtypeStruct((s, 3 * GROUP_W), F32),
        scratch_shapes=[pltpu.VMEM(((ATTN_QB + 1) * ATTN_BLOCK, GROUP_W), F32),
                        pltpu.VMEM(((ATTN_QB + 1) * ATTN_BLOCK, GROUP_W), F32)],
        compiler_params=_params(("parallel",)),
    )(qkv, qkv, qkv, qkv, qkv, qkv, do, do, lse, lse, dlt, dlt, bias)


def _zcol(c):
    return pl.BlockSpec((TILE_MIX, GROUP_W), lambda i, c=c: (i, c))


def _zhalo(c):
    per = TILE_MIX // SUBLANES
    return pl.BlockSpec((SUBLANES, GROUP_W), lambda i, c=c: (jnp.maximum(i * per - 1, 0), c))


def _full(shape):
    return pl.BlockSpec(shape, lambda i: tuple(0 for _ in shape))


def _of_layer(a, l):
    rest = a.shape[1:]
    return pl.BlockSpec((None,) + rest, lambda i: (l,) + tuple(0 for _ in rest))


def _lru_gates(xb, wa_ref, wx_ref, ba, bx, lam):
    xbb = xb.astype(BF16)
    r = jax.nn.sigmoid(_dot(xbb, wa_ref[...]) + ba)
    ig = jax.nn.sigmoid(_dot(xbb, wx_ref[...]) + bx)
    nl = -lam
    sp = jnp.maximum(nl, 0.0) + jnp.log1p(jnp.exp(-jnp.abs(nl)))
    log_a = (-RG_C * r) * sp
    a = jnp.exp(log_a)
    mult = jnp.sqrt(_neg_expm1(2.0 * log_a))
    return r, ig, sp, a, mult


def _gmlp_spatial(ws_ref, vvb, head):
    outs = []
    for j in range(vvb.shape[0] // GMLP_CHUNK):
        blk = vvb[j * GMLP_CHUNK:(j + 1) * GMLP_CHUNK, :]
        acc = jnp.zeros((GMLP_CHUNK, GROUP_W), F32)
        for h in range(N_HEADS):
            acc = jnp.where(head[:GMLP_CHUNK] == h, _dot(ws_ref[h], blk), acc)
        outs.append(acc)
    return jnp.concatenate(outs, axis=0)


def mix_fwd(z, attn, wts, l):
    s = z.shape[0]
    d4, d16 = PATTERN_DILS[1], PATTERN_DILS[2]

    def body(ax_ref, ab_ref, ac_ref, ag_ref, rx_ref, rg_ref, cu_ref, cv_ref, cg_ref, dg_ref,
             axh_ref, ach_ref, rxh_ref, o1_ref, l1_ref, o4_ref, l4_ref, o16_ref, l16_ref,
             caw_ref, crw_ref, crb_ref, wa_ref, wx_ref, ba_ref, bx_ref, lam_ref, gng_ref, ws_ref, bs_ref,
             y_ref, hl_ref, o_ref, lse_ref, lse4_ref, lse16_ref, carry, *stage):
        st_a, st_b, st_c, st_d, st_e = (stage[2 * k:2 * k + 2] for k in range(5))
        i = pl.program_id(0)

        @pl.when(i == 0)
        def _():
            carry[...] = jnp.zeros_like(carry)

        nz = (i > 0).astype(F32)
        head = _head_of_lane((TILE_MIX, GROUP_W))

        pa = ac_ref[...] * ax_ref[...]
        pah = ach_ref[...] * axh_ref[...] * nz
        cv = caw_ref[2:3, :] * pa + caw_ref[1:2, :] * _shift_down(pa, pah, 1) + caw_ref[0:1, :] * _shift_down(pa, pah, 2)
        y_ref[:, 0:GROUP_W] = (ab_ref[...] * cv * _silu(ag_ref[...])).astype(BF16)

        rx = rx_ref[...]
        rxh = rxh_ref[...] * nz
        xb = (crw_ref[3:4, :] * rx + crw_ref[2:3, :] * _shift_down(rx, rxh, 1) + crw_ref[1:2, :] * _shift_down(rx, rxh, 2)
              + crw_ref[0:1, :] * _shift_down(rx, rxh, 3) + crb_ref[...])
        _, ig, _, a, mult = _lru_gates(xb, wa_ref, wx_ref, ba_ref[...], bx_ref[...], lam_ref[...])
        ca, cb = _scan_causal(a, mult * (ig * xb))
        hl = cb + ca * carry[SUBLANES - 1:SUBLANES, :]
        hl_ref[...] = hl
        carry[...] = hl[TILE_MIX - SUBLANES:, :]
        y_ref[:, GROUP_W:2 * GROUP_W] = (hl * _silu(rg_ref[...])).astype(BF16)

        u = _gelu(cu_ref[...])
        gv = _gelu(cv_ref[...])
        rs = lax.rsqrt(jnp.mean(gv * gv, axis=-1, keepdims=True) + NORM_EPS)
        vvb = (gv * rs * gng_ref[...]).astype(BF16)
        sp = _gmlp_spatial(ws_ref, vvb, head) + jnp.concatenate([bs_ref[...]] * (TILE_MIX // GMLP_CHUNK), axis=0)
        y_ref[:, 2 * GROUP_W:3 * GROUP_W] = (u * sp * _silu(cg_ref[...])).astype(BF16)

        ops = (o1_ref[...], _interleave_load(o4_ref, d4, st_a), _interleave_load(o16_ref, d16, st_b))
        lps = (l1_ref[...], _interleave_load(l4_ref, d4, st_c), _interleave_load(l16_ref, d16, st_d))
        m = jnp.maximum(jnp.maximum(lps[0], lps[1]), lps[2])
        zsum = jnp.zeros_like(m)
        o = jnp.zeros_like(m)
        for op, lp in zip(ops, lps):
            w = jnp.exp(lp - m)
            zsum = zsum + w
            o = o + w * op
        o = o / zsum
        lse = m + jnp.log(zsum)
        o_ref[...] = o
        lse_ref[...] = lse
        _deinterleave_store(lse, st_e, ((d4, lse4_ref), (d16, lse16_ref)))
        y_ref[:, 3 * GROUP_W:4 * GROUP_W] = (o * _silu(dg_ref[...])).astype(BF16)

    row = pl.BlockSpec((TILE_MIX, GROUP_W), lambda i: (i, 0))
    r4 = _residue_block(d4, TILE_MIX, GROUP_W)
    r16 = _residue_block(d16, TILE_MIX, GROUP_W)
    names = ("caw", "crw", "crb", "wa", "wx", "ba", "bx", "lam", "gng", "ws", "bs")
    in_specs = ([_zcol(c) for c in (C_AX, C_AB, C_AC, C_AG, C_RX, C_RG, C_CU, C_CV, C_CG, C_DG)]
                + [_zhalo(C_AX), _zhalo(C_AC), _zhalo(C_RX), row, row, r4, r4, r16, r16]
                + [_of_layer(wts[k], l) for k in names])
    return pl.pallas_call(
        body, name="mix_fwd", grid=(s // TILE_MIX,),
        in_specs=in_specs,
        out_specs=[pl.BlockSpec((TILE_MIX, D_MIX), lambda i: (i, 0)), row, row, row, r4, r16],
        out_shape=([jax.ShapeDtypeStruct((s, D_MIX), BF16)] + [jax.ShapeDtypeStruct((s, GROUP_W), F32)] * 3
                   + [_by_residue(s, d4, GROUP_W, F32), _by_residue(s, d16, GROUP_W, F32)]),
        scratch_shapes=[pltpu.VMEM((SUBLANES, GROUP_W), F32)] + _stage_scratch(TILE_MIX, GROUP_W, 5),
        compiler_params=_params(("arbitrary",)),
    )(*([z] * 13), *[a for pair in attn for a in pair], *[wts[k] for k in names])


def mix_bwd(dy, z, hl, dqkv, ddg, wts, l):
    s = z.shape[0]
    d4, d16 = PATTERN_DILS[1], PATTERN_DILS[2]
    n_tiles = s // TILE_MIX

    def body(dya_ref, dyb_ref, dyc_ref, ax_ref, ab_ref, ac_ref, ag_ref, rx_ref, rg_ref, cu_ref, cv_ref, cg_ref,
             axh_ref, ach_ref, rxh_ref, hl_ref, hlh_ref, dqkv1_ref, dqkv4_ref, dqkv16_ref, ddg_ref,
             caw_ref, crw_ref, crb_ref, wa_ref, wx_ref, ba_ref, bx_ref, lam_ref, gng_ref, ws_ref, wst_ref, bs_ref,
             dz_ref, ga_ref, gr_ref, gn_ref, gwa_ref, gwx_ref, gws_ref, gbs_ref,
             c_dcv, c_g, c_a, c_dxb, *stage):
        st_a, st_b = stage[:len(stage) // 2], stage[len(stage) // 2:]
        step = pl.program_id(0)
        i = n_tiles - 1 - step

        @pl.when(step == 0)
        def _():
            for r in (c_dcv, c_g, c_a, c_dxb, ga_ref, gr_ref, gn_ref, gwa_ref, gwx_ref, gws_ref, gbs_ref):
                r[...] = jnp.zeros_like(r)

        nz = (i > 0).astype(F32)
        head = _head_of_lane((TILE_MIX, GROUP_W))
        shp8 = (SUBLANES, GROUP_W)
        colsum = lambda v: jnp.sum(v, axis=0, keepdims=True)

        ax, ab, ac, ag = ax_ref[...], ab_ref[...], ac_ref[...], ag_ref[...]
        dya = dya_ref[...]
        pa = ac * ax
        pah = ach_ref[...] * axh_ref[...] * nz
        pa1 = _shift_down(pa, pah, 1)
        pa2 = _shift_down(pa, pah, 2)
        cv = caw_ref[2:3, :] * pa + caw_ref[1:2, :] * pa1 + caw_ref[0:1, :] * pa2
        sg = _silu(ag)
        dz_ref[:, C_AB * GROUP_W:(C_AB + 1) * GROUP_W] = (dya * cv * sg).astype(BF16)
        dz_ref[:, C_AG * GROUP_W:(C_AG + 1) * GROUP_W] = (dya * ab * cv * _dsilu(ag)).astype(BF16)
        dcv = dya * ab * sg
        nxt = c_dcv[...]
        dpa = caw_ref[2:3, :] * dcv + caw_ref[1:2, :] * _shift_up(dcv, nxt, 1) + caw_ref[0:1, :] * _shift_up(dcv, nxt, 2)
        c_dcv[...] = dcv[:SUBLANES, :]
        dz_ref[:, C_AC * GROUP_W:(C_AC + 1) * GROUP_W] = (dpa * ax).astype(BF16)
        dz_ref[:, C_AX * GROUP_W:(C_AX + 1) * GROUP_W] = (dpa * ac).astype(BF16)
        ga_ref[...] += (_put_row(shp8, 2, colsum(dcv * pa)) + _put_row(shp8, 1, colsum(dcv * pa1))
                        + _put_row(shp8, 0, colsum(dcv * pa2)))

        rx, rg = rx_ref[...], rg_ref[...]
        dyb = dyb_ref[...]
        rxh = rxh_ref[...] * nz
        rx1, rx2, rx3 = _shift_down(rx, rxh, 1), _shift_down(rx, rxh, 2), _shift_down(rx, rxh, 3)
        xb = crw_ref[3:4, :] * rx + crw_ref[2:3, :] * rx1 + crw_ref[1:2, :] * rx2 + crw_ref[0:1, :] * rx3 + crb_ref[...]
        lam = lam_ref[...]
        r, ig, sp, a, mult = _lru_gates(xb, wa_ref, wx_ref, ba_ref[...], bx_ref[...], lam)
        hl = hl_ref[...]
        hprev = _shift_down(hl, hlh_ref[...] * nz, 1)
        dz_ref[:, C_RG * GROUP_W:(C_RG + 1) * GROUP_W] = (dyb * hl * _dsilu(rg)).astype(BF16)
        dh = dyb * _silu(rg)
        a_next = _shift_up(a, c_a[...], 1)
        ca, cb = _scan_anticausal(a_next, dh)
        g = cb + ca * c_g[0:1, :]
        c_g[...] = g[:SUBLANES, :]
        c_a[...] = a[:SUBLANES, :]
        u = ig * xb
        da = g * hprev
        dmult = g * u
        du = g * mult
        dlog_a = da * a - dmult * (a * a) / mult
        dr = dlog_a * (-RG_C * sp)
        dga = dr * r * (1.0 - r)
        dgx = (du * xb) * ig * (1.0 - ig)
        dgab, dgxb = dga.astype(BF16), dgx.astype(BF16)
        dxb = du * ig + _dot_nt(dgab, wa_ref[...]) + _dot_nt(dgxb, wx_ref[...])
        xbb = xb.astype(BF16)
        gwa_ref[...] += _dot_tn(xbb, dgab)
        gwx_ref[...] += _dot_tn(xbb, dgxb)
        nxt = c_dxb[...]
        drx = (crw_ref[3:4, :] * dxb + crw_ref[2:3, :] * _shift_up(dxb, nxt, 1) + crw_ref[1:2, :] * _shift_up(dxb, nxt, 2)
               + crw_ref[0:1, :] * _shift_up(dxb, nxt, 3))
        c_dxb[...] = dxb[:SUBLANES, :]
        dz_ref[:, C_RX * GROUP_W:(C_RX + 1) * GROUP_W] = drx.astype(BF16)
        dlam = colsum(dlog_a * (-RG_C * r)) * (-jax.nn.sigmoid(-lam))
        gr_ref[...] += (_put_row(shp8, 3, colsum(dxb * rx)) + _put_row(shp8, 2, colsum(dxb * rx1))
                        + _put_row(shp8, 1, colsum(dxb * rx2)) + _put_row(shp8, 0, colsum(dxb * rx3))
                        + _put_row(shp8, 4, colsum(dxb)) + _put_row(shp8, 5, colsum(dga))
                        + _put_row(shp8, 6, colsum(dgx)) + _put_row(shp8, 7, dlam))

        cu, cvv, cg = cu_ref[...], cv_ref[...], cg_ref[...]
        dyc = dyc_ref[...]
        u_c, du_c = _gelu_and_grad(cu)
        gv, dgv_c = _gelu_and_grad(cvv)
        rs = lax.rsqrt(jnp.mean(gv * gv, axis=-1, keepdims=True) + NORM_EPS)
        vh = gv * rs
        gng = gng_ref[...]
        vvb = (vh * gng).astype(BF16)
        spat = _gmlp_spatial(ws_ref, vvb, head) + jnp.concatenate([bs_ref[...]] * (TILE_MIX // GMLP_CHUNK), axis=0)
        sgc = _silu(cg)
        dz_ref[:, C_CU * GROUP_W:(C_CU + 1) * GROUP_W] = (dyc * spat * sgc * du_c).astype(BF16)
        dz_ref[:, C_CG * GROUP_W:(C_CG + 1) * GROUP_W] = (dyc * u_c * spat * _dsilu(cg)).astype(BF16)
        dsp = dyc * u_c * sgc
        dspb = dsp.astype(BF16)
        tril = (lax.broadcasted_iota(jnp.int32, (GMLP_CHUNK, GMLP_CHUNK), 0)
                >= lax.broadcasted_iota(jnp.int32, (GMLP_CHUNK, GMLP_CHUNK), 1))
        head_c = head[:GMLP_CHUNK]
        dvv_parts = []
        gbs = jnp.zeros((GMLP_CHUNK, GROUP_W), F32)
        for j in range(TILE_MIX // GMLP_CHUNK):
            sl = slice(j * GMLP_CHUNK, (j + 1) * GMLP_CHUNK)
            dblk = dspb[sl, :]
            vblk = vvb[sl, :]
            gbs = gbs + dsp[sl, :]
            acc = jnp.zeros((GMLP_CHUNK, GROUP_W), F32)
            for h in range(N_HEADS):
                acc = jnp.where(head_c == h, _dot(wst_ref[h], dblk), acc)
                dm = jnp.where(head_c == h, dblk, jnp.zeros_like(dblk))
                gws_ref[h] += jnp.where(tril, _dot_nt(dm, vblk), 0.0)
            dvv_parts.append(acc)
        gbs_ref[...] += gbs
        dvv = jnp.concatenate(dvv_parts, axis=0)
        gn_ref[...] += _put_row(shp8, 0, colsum(dvv * vh))
        dvh = dvv * gng
        dgv = rs * (dvh - vh * jnp.mean(dvh * vh, axis=-1, keepdims=True))
        dz_ref[:, C_CV * GROUP_W:(C_CV + 1) * GROUP_W] = (dgv * dgv_c).astype(BF16)

        dsum = dqkv1_ref[...] + _interleave_load(dqkv4_ref, d4, st_a) + _interleave_load(dqkv16_ref, d16, st_b)
        dz_ref[:, C_DQ * GROUP_W:(C_DV + 1) * GROUP_W] = dsum.astype(BF16)
        dz_ref[:, C_DG * GROUP_W:(C_DG + 1) * GROUP_W] = ddg_ref[...].astype(BF16)

    per = TILE_MIX // SUBLANES
    qkv_w = 3 * GROUP_W
    rev = lambda c: pl.BlockSpec((TILE_MIX, GROUP_W), lambda t, c=c: (n_tiles - 1 - t, c))
    revh = lambda c: pl.BlockSpec((SUBLANES, GROUP_W),
                                  lambda t, c=c: (jnp.maximum((n_tiles - 1 - t) * per - 1, 0), c))
    revr = lambda dil: pl.BlockSpec((dil, TILE_MIX // dil, qkv_w), lambda t: (0, n_tiles - 1 - t, 0))
    names = ("caw", "crw", "crb", "wa", "wx", "ba", "bx", "lam", "gng", "ws", "wst", "bs")
    in_specs = ([rev(0), rev(1), rev(2)]
                + [rev(c) for c in (C_AX, C_AB, C_AC, C_AG, C_RX, C_RG, C_CU, C_CV, C_CG)]
                + [revh(C_AX), revh(C_AC), revh(C_RX), rev(0), revh(0),
                   pl.BlockSpec((TILE_MIX, qkv_w), lambda t: (n_tiles - 1 - t, 0)), revr(d4), revr(d16), rev(0)]
                + [_of_layer(wts[k], l) for k in names])
    small = jax.ShapeDtypeStruct((SUBLANES, GROUP_W), F32)
    sq = jax.ShapeDtypeStruct((GROUP_W, GROUP_W), F32)
    out_shape = [jax.ShapeDtypeStruct((s, D_IN), BF16), small, small, small, sq, sq,
                 jax.ShapeDtypeStruct((N_HEADS, GMLP_CHUNK, GMLP_CHUNK), F32),
                 jax.ShapeDtypeStruct((GMLP_CHUNK, GROUP_W), F32)]
    out_specs = ([pl.BlockSpec((TILE_MIX, D_IN), lambda t: (n_tiles - 1 - t, 0))]
                 + [_full(o.shape) for o in out_shape[1:]])
    return pl.pallas_call(
        body, name="mix_bwd", grid=(n_tiles,),
        in_specs=in_specs, out_specs=out_specs, out_shape=out_shape,
        scratch_shapes=[pltpu.VMEM((SUBLANES, GROUP_W), F32)] * 4 + _stage_scratch(TILE_MIX, qkv_w, 2),
        compiler_params=_params(("arbitrary",)),
    )(dy, dy, dy, *([z] * 12), hl, hl, *dqkv, ddg, *[wts[k] for k in names])


def prepare_small_weights(p):
    tril = jnp.tril(jnp.ones((GMLP_CHUNK, GMLP_CHUNK), dtype=bool))
    ws = jnp.where(tril, p["gmlp_ws"], 0.0).astype(BF16)
    row = lambda a: a[:, None, :]
    eye = jnp.eye(N_HEADS, dtype=F32)
    bd = lambda w: (w[:, :, :, None, :] * eye[None, :, None, :, None]).reshape(-1, GROUP_W, GROUP_W).astype(BF16)
    return dict(
        caw=p["conv_a_w"], crw=p["conv_r_w"], crb=row(p["conv_r_b"]),
        wa=bd(p["lru_wa"]), wx=bd(p["lru_wx"]),
        ba=row(p["lru_ba"]), bx=row(p["lru_bx"]), lam=row(p["lru_lambda"]), gng=row(p["gmlp_norm_g"]),
        ws=ws, wst=jnp.swapaxes(ws, 2, 3),
        bs=jnp.repeat(jnp.swapaxes(p["gmlp_bs"], 1, 2), HEAD_DIM, axis=2))


def _flat(a):
    return a.reshape(a.shape[0] * a.shape[1], a.shape[2])


def _split(a, dil):
    return a.reshape(dil, a.shape[0] // dil, a.shape[1])


def local_step(x, tgt, final_g, depth, chip, layer_weights, projections_done):
    saved = []
    for l in range(depth):
        gain, w_in_l, w_out_l, wts = layer_weights(l, x)
        z, h, *qkvs = in_fwd(x, gain, w_in_l)
        qkvs = [_flat(q) if q.ndim == 3 else q for q in qkvs]
        attn = []
        for q, d in zip(qkvs, PATTERN_DILS):
            o_p, lse_p = attn_fwd(q, d)
            attn.append((o_p, lse_p) if d == 1 else (_split(o_p, d), _split(lse_p, d)))
        y, hl, o, lse, lse4, lse16 = mix_fwd(z, attn, wts, l)
        saved.append(dict(x=x, z=z, h=h, y=y, hl=hl, o=o, qkvs=qkvs, lses=(lse, _flat(lse4), _flat(lse16)), wts=wts,
                          gain=gain, w_in=w_in_l, w_out=w_out_l))
        x = out_fwd(y, w_out_l, x)

    loss, dx, dfg = loss_head(x, final_g[None, :], tgt)
    raw = {k: [None] * depth for k in ("gain", "a", "r", "n", "wa", "wx", "ws", "bs")}
    zero = None
    for l in reversed(range(depth)):
        sv = saved[l]
        dy, ddg, do1, do4, do16, dl1, dl4, dl16 = out_bwd(dx, sv["w_out"], sv["z"], sv["o"])
        g_w_out = matmul_tn(sv["y"], dx, 1)
        dqkv = []
        for q, do, lse, dl, d in zip(sv["qkvs"], (do1, _flat(do4), _flat(do16)), sv["lses"],
                                     (dl1, _flat(dl4), _flat(dl16)), PATTERN_DILS):
            g = attn_bwd(q, do, lse, dl, d)
            dqkv.append(g if d == 1 else _split(g, d))
        dz, ga, gr, gn, gwa, gwx, gws, gbs = mix_bwd(dy, sv["z"], sv["hl"], dqkv, ddg, sv["wts"], l)
        gain = sv["gain"] if zero is None else sv["gain"] + zero
        zero = projections_done(l, *grad_w_in(sv["h"], dz, chip), g_w_out)
        if l == 0 and zero is not None:
            gain = gain + zero
        dx, dgn = in_bwd(dz, sv["w_in"], sv["x"], gain, dx)
        for k, g in zip(("gain", "a", "r", "n", "wa", "wx", "ws", "bs"), (dgn, ga, gr, gn, gwa, gwx, gws, gbs)):
            raw[k][l] = g
    st = {k: jnp.stack(v) for k, v in raw.items()}
    eye = jnp.eye(N_HEADS, dtype=F32)[None, :, None, :, None]
    diag = lambda g: (g.reshape(depth, N_HEADS, HEAD_DIM, N_HEADS, HEAD_DIM) * eye).sum(axis=3)
    grads = dict(
        norm_g=st["gain"][:, 0], conv_a_w=st["a"][:, :3], conv_r_w=st["r"][:, :4], conv_r_b=st["r"][:, 4],
        lru_ba=st["r"][:, 5], lru_bx=st["r"][:, 6], lru_lambda=st["r"][:, 7], gmlp_norm_g=st["n"][:, 0],
        lru_wa=diag(st["wa"]), lru_wx=diag(st["wx"]), gmlp_ws=st["ws"],
        gmlp_bs=jnp.swapaxes(st["bs"].reshape(depth, GMLP_CHUNK, N_HEADS, HEAD_DIM).sum(-1), 1, 2),
        final_g=dfg[0])
    return loss, dx, grads


MESH = pl.DeviceIdType.MESH
N_CHIPS = 4
N_DEV = 8
ANY = pl.BlockSpec(memory_space=pl.ANY)


def _place():
    x, y, c = lax.axis_index("x"), lax.axis_index("y"), lax.axis_index("c")
    chips = [(1 - x, y), (x, 1 - y), (1 - x, 1 - y)]
    return x, y, c, chips


def _remote(src, dst, ssem, rsem, to):
    return pltpu.make_async_remote_copy(src_ref=src, dst_ref=dst, send_sem=ssem, recv_sem=rsem,
                                        device_id=to, device_id_type=MESH)


HBM = pl.BlockSpec(memory_space=pltpu.HBM)
SEM = pl.BlockSpec(memory_space=pltpu.SEMAPHORE)
DATAFLOW = pltpu.SideEffectType.DATAFLOW_SIDE_EFFECTING
GATHER, SCATTER = "gather", "scatter"


def _chip_copies(mode, src_refs, land_refs, ssem, rsem):
    x, y, c, chips = _place()
    me = 2 * x + y
    n = len(src_refs)
    copies = []
    for k, (cx, cy) in enumerate(chips):
        for a in range(n):
            if mode == GATHER:
                src, dst = src_refs[a], land_refs[a].at[me]
            else:
                src, dst = src_refs[a].at[2 * cx + cy], land_refs[a].at[k]
            copies.append(_remote(src, dst, ssem.at[n * k + a], rsem.at[n * k + a], (cx, cy, c)))
    return copies


def exchange_start(mode, srcs, after, name):
    n = len(srcs)
    if mode == GATHER:
        lands = [lax.empty((N_CHIPS,) + s.shape, s.dtype) for s in srcs]
    else:
        lands = [lax.empty((N_CHIPS - 1,) + s.shape[1:], s.dtype) for s in srcs]
    extra = [] if after is None else [after]

    def body(*refs):
        src_refs, land_refs = refs[:n], refs[n:2 * n]
        ssem, rsem = refs[2 * n + len(extra)], refs[2 * n + len(extra) + 1]
        token = refs[-1]
        for cp in _chip_copies(mode, src_refs, land_refs, ssem, rsem):
            cp.start()
        token[...] = jnp.zeros_like(token)

    arrays = list(srcs) + lands
    return pl.pallas_call(
        body, name=name,
        out_shape=(pltpu.SemaphoreType.DMA((3 * n,)), pltpu.SemaphoreType.DMA((3 * n,)),
                   *[pltpu.HBM(a.shape, a.dtype) for a in arrays], jax.ShapeDtypeStruct((SUBLANES, LANES), F32)),
        in_specs=[HBM] * (2 * n) + [ANY] * len(extra),
        out_specs=(SEM, SEM, *[HBM] * (2 * n), pl.BlockSpec(memory_space=pltpu.VMEM)),
        input_output_aliases={i: 2 + i for i in range(2 * n)},
        compiler_params=pltpu.CompilerParams(has_side_effects=DATAFLOW),
    )(*[pltpu.with_memory_space_constraint(a, pltpu.HBM) for a in arrays], *extra)


def exchange_wait(mode, started, after, name):
    ssem, rsem, *thru, _ = started
    n = len(thru) // 2

    def body(*refs):
        src_refs, land_refs = refs[:n], refs[n:2 * n]
        ssem_ref, rsem_ref = refs[2 * n], refs[2 * n + 1]
        for cp in _chip_copies(mode, src_refs, land_refs, ssem_ref, rsem_ref):
            cp.wait_send()
            cp.wait_recv()

    outs = pl.pallas_call(
        body, name=name,
        out_shape=[pltpu.HBM(a.shape, a.dtype) for a in thru],
        in_specs=[HBM] * (2 * n) + [SEM, SEM, ANY],
        out_specs=[HBM] * (2 * n),
        input_output_aliases={i: i for i in range(2 * n)},
        compiler_params=pltpu.CompilerParams(has_side_effects=DATAFLOW),
    )(*thru, ssem, rsem, after)
    return outs[n:]


def sibling_exchange(p1, p2):
    def body(p1_ref, p2_ref, q1_ref, q2_ref, ssem, rsem):
        x, y, c, _ = _place()
        copies = [_remote(p_ref, q_ref, ssem.at[a], rsem.at[a], (x, y, 1 - c))
                  for a, (p_ref, q_ref) in enumerate(((p1_ref, q1_ref), (p2_ref, q2_ref)))]
        for cp in copies:
            cp.start()
        for cp in copies:
            cp.wait()

    return pl.pallas_call(
        body, name="sibling_exchange",
        in_specs=[ANY, ANY], out_specs=[ANY, ANY],
        out_shape=[jax.ShapeDtypeStruct(p.shape, p.dtype) for p in (p1, p2)],
        scratch_shapes=[pltpu.SemaphoreType.DMA((2,)), pltpu.SemaphoreType.DMA((2,))],
    )(p1, p2)


def gather_all(v):
    m_per, n = v.shape

    def body(x_ref, out_ref, send_sems, recv_sems, local_sem):
        x, y, c, chips = _place()
        me, sibling = (x, y, c), (x, y, 1 - c)

        def rows(px, py, pc):
            return out_ref.at[pl.ds((4 * px + 2 * py + pc) * m_per, m_per), :]

        def copy(k, block, to, src=None):
            return _remote(rows(*block) if src is None else src, rows(*block), send_sems.at[k], recv_sems.at[k], to)

        mine = pltpu.make_async_copy(x_ref, rows(*me), local_sem)
        mine.start()
        first = [copy(0, me, sibling, src=x_ref)]
        first += [copy(1 + j, me, (*chip, c), src=x_ref) for j, chip in enumerate(chips)]
        for cp in first:
            cp.start()
        passed = [copy(4 + j, (*chip, c), sibling) for j, chip in enumerate(chips)]
        for j, chip in enumerate(chips):
            copy(1 + j, (*chip, c), me).wait_recv()
            passed[j].start()
        copy(0, sibling, me).wait_recv()
        for j, chip in enumerate(chips):
            copy(4 + j, (*chip, 1 - c), me).wait_recv()
        for cp in first + passed:
            cp.wait_send()
        mine.wait()

    return pl.pallas_call(
        body, name="gather_all",
        out_shape=jax.ShapeDtypeStruct((N_DEV * m_per, n), v.dtype),
        in_specs=[pl.BlockSpec(memory_space=pltpu.VMEM)],
        out_specs=pl.BlockSpec(memory_space=pltpu.VMEM),
        scratch_shapes=[pltpu.SemaphoreType.DMA((7,)), pltpu.SemaphoreType.DMA((7,)), pltpu.SemaphoreType.DMA],
        compiler_params=pltpu.CompilerParams(vmem_limit_bytes=VMEM_LIMIT),
    )(v)


TILE_ROWS = 256


def sum_partials(own, parts):
    k, r, c = parts.shape
    first = [] if own is None else [own]

    def body(*refs):
        p_ref, out_ref = refs[-2], refs[-1]
        acc = p_ref[0].astype(F32) if own is None else refs[0][...]
        for i in range(1 if own is None else 0, k):
            acc = acc + p_ref[i].astype(F32)
        out_ref[...] = acc

    row = pl.BlockSpec((TILE_ROWS, c), lambda i: (i, 0))
    return pl.pallas_call(
        body, name="sum_partials", grid=(r // TILE_ROWS,),
        in_specs=[row] * len(first) + [pl.BlockSpec((k, TILE_ROWS, c), lambda i: (0, i, 0))],
        out_specs=row,
        out_shape=jax.ShapeDtypeStruct((r, c), F32),
        compiler_params=_params(("parallel",)),
    )(*first, parts)


def _adamw_update(w, g, m, v):
    m2 = ADAM_B1 * m + (1.0 - ADAM_B1) * g
    v2 = ADAM_B2 * v + (1.0 - ADAM_B2) * (g * g)
    m_hat = m2 / (1.0 - ADAM_B1 ** ADAM_STEP)
    v_hat = v2 / (1.0 - ADAM_B2 ** ADAM_STEP)
    return -ADAM_LR * (m_hat / (jnp.sqrt(v_hat) + ADAM_EPS) + ADAM_WD * w), m2, v2


def adamw_small(ws, gs, ms, vs):
    n = len(ws)

    def body(*refs):
        ins, outs = refs[:4 * n], refs[4 * n:]
        for i in range(n):
            d, m2, v2 = _adamw_update(ins[i][...], ins[n + i][...], ins[2 * n + i][...], ins[3 * n + i][...])
            outs[3 * i][...] = d
            outs[3 * i + 1][...] = m2
            outs[3 * i + 2][...] = v2

    outs = pl.pallas_call(
        body, name="adamw_small",
        out_shape=[jax.ShapeDtypeStruct(w.shape, F32) for w in ws for _ in range(3)],
    )(*ws, *gs, *ms, *vs)
    return [tuple(outs[3 * i:3 * i + 3]) for i in range(n)]


def adamw(w, ga, gb, m, v):
    n, r, c = w.shape
    tile = max(t for t in range(SUBLANES, TILE_ROWS + 1, SUBLANES) if r % t == 0)

    def body(w_ref, ga_ref, gb_ref, m_ref, v_ref, g_ref, d_ref, m2_ref, v2_ref):
        g = ga_ref[...] + gb_ref[...]
        g_ref[...] = g
        d_ref[...], m2_ref[...], v2_ref[...] = _adamw_update(w_ref[...], g, m_ref[...], v_ref[...])

    spec = pl.BlockSpec((1, tile, c), lambda j, i: (j, i, 0))
    return pl.pallas_call(
        body, name="adamw", grid=(n, r // tile),
        in_specs=[spec] * 5, out_specs=[spec] * 4,
        out_shape=[jax.ShapeDtypeStruct((n, r, c), F32)] * 4,
        compiler_params=_params(("parallel", "parallel")),
    )(w, ga, gb, m, v)


REPLICATED = ("norm_g", "conv_r_b", "lru_wa", "lru_ba", "lru_wx", "lru_bx", "lru_lambda", "gmlp_norm_g",
              "gmlp_ws", "gmlp_bs", "final_g")
CHIP_SHARDED_SMALL = ("conv_a_w", "conv_r_w")
PACK_LANES = 128


def _pack(arrays):
    flat = jnp.concatenate([a.reshape(-1) for a in arrays])
    pad = (-flat.shape[0]) % (TILE_ROWS * PACK_LANES)
    return jnp.pad(flat, (0, pad)).reshape(-1, PACK_LANES)


def _unpack(packed, shapes):
    flat = packed.reshape(-1)
    out, off = [], 0
    for shp in shapes:
        n = math.prod(shp)
        out.append(flat[off:off + n].reshape(shp))
        off += n
    return out


def kernel(x, norm_g, w_in, conv_a_w, conv_r_w, conv_r_b, lru_wa, lru_ba, lru_wx, lru_bx, lru_lambda, gmlp_norm_g, gmlp_ws, gmlp_bs, w_out, final_g, loss_target, m_norm_g, m_w_in, m_conv_a_w, m_conv_r_w, m_conv_r_b, m_lru_wa, m_lru_ba, m_lru_wx, m_lru_bx, m_lru_lambda, m_gmlp_norm_g, m_gmlp_ws, m_gmlp_bs, m_w_out, m_final_g, v_norm_g, v_w_in, v_conv_a_w, v_conv_r_w, v_conv_r_b, v_lru_wa, v_lru_ba, v_lru_wx, v_lru_bx, v_lru_lambda, v_gmlp_norm_g, v_gmlp_ws, v_gmlp_bs, v_w_out, v_final_g):
    names = ("norm_g", "w_in", "conv_a_w", "conv_r_w", "conv_r_b", "lru_wa", "lru_ba", "lru_wx", "lru_bx",
             "lru_lambda", "gmlp_norm_g", "gmlp_ws", "gmlp_bs", "w_out", "final_g")
    w = dict(zip(names, (norm_g, w_in, conv_a_w, conv_r_w, conv_r_b, lru_wa, lru_ba, lru_wx, lru_bx, lru_lambda,
                         gmlp_norm_g, gmlp_ws, gmlp_bs, w_out, final_g)))
    m = dict(zip(names, (m_norm_g, m_w_in, m_conv_a_w, m_conv_r_w, m_conv_r_b, m_lru_wa, m_lru_ba, m_lru_wx, m_lru_bx,
                         m_lru_lambda, m_gmlp_norm_g, m_gmlp_ws, m_gmlp_bs, m_w_out, m_final_g)))
    v = dict(zip(names, (v_norm_g, v_w_in, v_conv_a_w, v_conv_r_w, v_conv_r_b, v_lru_wa, v_lru_ba, v_lru_wx, v_lru_bx,
                         v_lru_lambda, v_gmlp_norm_g, v_gmlp_ws, v_gmlp_bs, v_w_out, v_final_g)))
    depth, _, in_cols = w_in.shape
    out_rows = w_out.shape[1]
    conv_ch = conv_a_w.shape[2]
    chip = 2 * lax.axis_index("x") + lax.axis_index("y")

    taps = conv_a_w.shape[1] + conv_r_w.shape[1]
    w_in_t, m_w_in_t, v_w_in_t = (jnp.swapaxes(a, 1, 2) for a in (w_in, m_w_in, v_w_in))
    w_in_h, w_out_h = w_in_t.astype(BF16), w_out.astype(BF16)
    conv_own = jnp.concatenate([conv_a_w, conv_r_w], axis=1).reshape(depth * taps, conv_ch)
    gathers, token = [], None
    for l in range(depth):
        srcs = [w_in_h[l], w_out_h[l]] + ([conv_own] if l == 0 else [])
        gathers.append(exchange_start(GATHER, srcs, token, f"gather_start_{l}"))
        token = gathers[-1][-1]
    p = dict(w)

    def with_own(land, own):
        return lax.dynamic_update_slice(land, own[None], (chip,) + (0,) * own.ndim)

    def layer_weights(l, x_l):
        lands = exchange_wait(GATHER, gathers[l], x_l, f"gather_wait_{l}")
        w_in_l = with_own(lands[0], w_in_h[l]).reshape(D_IN, D_MODEL)
        w_out_l = with_own(lands[1], w_out_h[l]).reshape(D_MIX, D_MODEL)
        gain = norm_g[l][None, :]
        if l == 0:
            conv = with_own(lands[2], conv_own).reshape(N_CHIPS, depth, taps, conv_ch)
            conv = conv.transpose(1, 2, 0, 3).reshape(depth, taps, GROUP_W)
            p["conv_a_w"] = conv[:, :conv_a_w.shape[1]]
            p["conv_r_w"] = conv[:, conv_a_w.shape[1]:]
            p["prepared"] = prepare_small_weights(p)
            gain = gain + token[0, 0]
        return gain, w_in_l, w_out_l, p["prepared"]

    scatters, owns = [None] * depth, [None] * depth

    def projections_done(l, g_w_in_by_chip, g_w_in_own, g_w_out):
        go = g_w_out.reshape(N_CHIPS, out_rows, D_MODEL)
        owns[l] = (g_w_in_own, lax.dynamic_index_in_dim(go, chip, axis=0, keepdims=False))
        scatters[l] = exchange_start(SCATTER, [g_w_in_by_chip, go.astype(BF16)], None, f"scatter_start_{l}")
        return scatters[l][-1][0, 0]

    loss8, dx, grads = local_step(x[0], loss_target[0], final_g, depth, chip.reshape(1), layer_weights,
                                  projections_done)
    loss = lax.psum(loss8[0, 0], ("x", "y", "c"))

    parts = [exchange_wait(SCATTER, scatters[l], dx, f"scatter_wait_{l}") for l in range(depth)]
    r1 = jnp.concatenate([parts[l][0] for l in range(depth)], axis=1)
    r2 = jnp.concatenate([parts[l][1] for l in range(depth)], axis=1)
    own1 = jnp.concatenate([owns[l][0] for l in range(depth)], axis=0)
    own2 = jnp.concatenate([owns[l][1] for l in range(depth)], axis=0)
    p1 = sum_partials(own1, r1)
    p2 = sum_partials(own2, r2)
    q1, q2 = sibling_exchange(p1, p2)
    res = {}
    res["w_in"] = [jnp.swapaxes(t, 1, 2) for t in
                   adamw(w_in_t, p1.reshape(w_in_t.shape), q1.reshape(w_in_t.shape), m_w_in_t, v_w_in_t)]
    res["w_out"] = adamw(w_out, p2.reshape(w_out.shape), q2.reshape(w_out.shape), m_w_out, v_w_out)

    small = REPLICATED + CHIP_SHARDED_SMALL
    packed = _pack([grads[k] for k in small])
    rows = packed.shape[0]
    allp = gather_all(packed).reshape(N_DEV, rows, PACK_LANES)
    total = sum_partials(None, allp)
    gs = dict(zip(small, _unpack(total, [grads[k].shape for k in small])))
    for k in CHIP_SHARDED_SMALL:
        gs[k] = lax.dynamic_slice_in_dim(gs[k], chip * conv_ch, conv_ch, axis=2)
    as2d = lambda a: a[None] if a.ndim == 1 else a
    outs = adamw_small(*[[as2d(d[k]) for k in small] for d in (w, gs, m, v)])
    for k, (delta, m2, v2) in zip(small, outs):
        res[k] = [t.reshape(w[k].shape) for t in (gs[k], delta, m2, v2)]

    return (loss, dx[None], *[res[k][0] for k in names], *[res[k][1] for k in names],
            *[res[k][2] for k in names], *[res[k][3] for k in names])
```

```python
import functools
import math

import jax
import jax.numpy as jnp
import numpy as np
from jax import lax
from jax.experimental import pallas as pl
from jax.experimental.pallas import tpu as pltpu

F32 = jnp.float32
BF16 = jnp.bfloat16

D_MODEL = 1024
GROUP_W = 256
N_HEADS = 4
HEAD_DIM = 64
N_CHUNKS = 13
D_IN = N_CHUNKS * GROUP_W
D_MIX = 4 * GROUP_W
NORM_EPS = 1e-6
RG_C = 8.0
GMLP_CHUNK = 128
ATTN_BLOCK = 128
PATTERN_DILS = (1, 4, 16)
N_PATTERNS = len(PATTERN_DILS)
ALIBI_SLOPES = tuple(2.0 ** (-8.0 * (h + 1) / N_HEADS) for h in range(N_HEADS))
ATTN_SCALE = 1.0 / math.sqrt(HEAD_DIM)
NEG_BIG = -1e30

ADAM_LR = 0.001
ADAM_B1 = 0.9
ADAM_B2 = 0.999
ADAM_EPS = 1e-08
ADAM_WD = 0.01
ADAM_STEP = 10

C_AX, C_AB, C_AC, C_AG, C_RX, C_RG, C_CU, C_CV, C_CG, C_DQ, C_DK, C_DV, C_DG = range(13)

SUBLANES = 8
LANES = 128
VMEM_LIMIT = 56 * 1024 * 1024
TILE_IN = 512
TILE_MIX = 512
TILE_DW = 512
ATTN_QB = 4
GELU_K0 = math.sqrt(2.0 / math.pi)
GELU_K1 = 0.044715


def _params(sem):
    return pltpu.CompilerParams(dimension_semantics=sem, vmem_limit_bytes=VMEM_LIMIT)


def _silu(x):
    return x * jax.nn.sigmoid(x)


def _dsilu(x):
    s = jax.nn.sigmoid(x)
    return s * (1.0 + x * (1.0 - s))


def _gelu(x):
    return 0.5 * x * (1.0 + jnp.tanh(GELU_K0 * (x + GELU_K1 * x * x * x)))


def _gelu_and_grad(x):
    t = jnp.tanh(GELU_K0 * (x + GELU_K1 * x * x * x))
    g = 0.5 * x * (1.0 + t)
    dg = 0.5 * (1.0 + t) + 0.5 * x * (1.0 - t * t) * GELU_K0 * (1.0 + 3.0 * GELU_K1 * x * x)
    return g, dg


def _neg_expm1(x):
    series = x * (1.0 + x * (0.5 + x * (1.0 / 6.0 + x * (1.0 / 24.0 + x * (1.0 / 120.0)))))
    return -jnp.where(x > -0.05, series, jnp.exp(x) - 1.0)


def _shift_down(v, halo, k):
    r = pltpu.roll(v, k, 0)
    rh = pltpu.roll(halo, k, 0)
    row = lax.broadcasted_iota(jnp.int32, halo.shape, 0)
    top = jnp.where(row < k, rh, r[:SUBLANES])
    return jnp.concatenate([top, r[SUBLANES:]], axis=0)


def _shift_up(v, halo, k):
    t = v.shape[0]
    r = pltpu.roll(v, t - k, 0)
    rh = pltpu.roll(halo, SUBLANES - k, 0)
    row = lax.broadcasted_iota(jnp.int32, halo.shape, 0)
    bot = jnp.where(row >= SUBLANES - k, rh, r[t - SUBLANES:])
    return jnp.concatenate([r[:t - SUBLANES], bot], axis=0)


def _scan_causal(a, b, h_in):
    t = a.shape[0]
    row8 = lax.broadcasted_iota(jnp.int32, a.shape, 0) % SUBLANES
    d = 1
    while d < SUBLANES:
        m = row8 >= d
        a_s = jnp.where(m, pltpu.roll(a, d, 0), 1.0)
        b_s = jnp.where(m, pltpu.roll(b, d, 0), 0.0)
        b = a * b_s + b
        a = a * a_s
        d *= 2
    out, carry = [], h_in
    for g in range(t // SUBLANES):
        sl = slice(g * SUBLANES, (g + 1) * SUBLANES)
        hg = b[sl] + a[sl] * carry
        out.append(hg)
        carry = hg[SUBLANES - 1:SUBLANES]
    return jnp.concatenate(out, axis=0)


def _scan_anticausal(a, b, g_in):
    t = a.shape[0]
    row8 = lax.broadcasted_iota(jnp.int32, a.shape, 0) % SUBLANES
    d = 1
    while d < SUBLANES:
        m = row8 < SUBLANES - d
        a_s = jnp.where(m, pltpu.roll(a, t - d, 0), 1.0)
        b_s = jnp.where(m, pltpu.roll(b, t - d, 0), 0.0)
        b = a * b_s + b
        a = a * a_s
        d *= 2
    out, carry = [], g_in
    for g in reversed(range(t // SUBLANES)):
        sl = slice(g * SUBLANES, (g + 1) * SUBLANES)
        gg = b[sl] + a[sl] * carry
        out.append(gg)
        carry = gg[0:1]
    return jnp.concatenate(out[::-1], axis=0)


def _head_of_lane(shape):
    return lax.broadcasted_iota(jnp.int32, shape, len(shape) - 1) // HEAD_DIM


def _put_row(acc_shape, k, row_vec):
    row = lax.broadcasted_iota(jnp.int32, acc_shape, 0)
    return jnp.where(row == k, jnp.broadcast_to(row_vec, acc_shape), 0.0)


def _dot(a, b):
    return jnp.dot(a, b, preferred_element_type=F32)


def _dot_nt(a, b):
    return lax.dot_general(a, b, (((1,), (1,)), ((), ())), preferred_element_type=F32)


def _dot_tn(a, b):
    return lax.dot_general(a, b, (((0,), (0,)), ((), ())), preferred_element_type=F32)


def _deinterleave_store(val, stage, outs):
    t, c = val.shape
    for hh in range(c // LANES):
        stage[hh][...] = val[:, hh * LANES:(hh + 1) * LANES].astype(F32)
    for dil, ref in outs:
        for r in range(dil):
            for hh in range(c // LANES):
                ref[r, :, hh * LANES:(hh + 1) * LANES] = stage[hh][pl.ds(r, t // dil, stride=dil), :].astype(ref.dtype)


def _interleave_load(ref, dil, stage):
    _, n, c = ref.shape
    for r in range(dil):
        for hh in range(c // LANES):
            stage[hh][pl.ds(r, n, stride=dil), :] = ref[r, :, hh * LANES:(hh + 1) * LANES].astype(F32)
    return jnp.concatenate([stage[hh][...] for hh in range(c // LANES)], axis=1)


def _stage_scratch(tile, cols, copies):
    return [pltpu.VMEM((tile, LANES), F32)] * (copies * (cols // LANES))


def _by_residue(s, dil, cols, dtype):
    return jax.ShapeDtypeStruct((dil, s // dil, cols), dtype)


def _residue_block(dil, tile, cols):
    return pl.BlockSpec((dil, tile // dil, cols), lambda i: (0, i, 0))


def in_fwd(x, g, w):
    s = x.shape[0]
    qkv_w = 3 * GROUP_W

    def body(x_ref, g_ref, w_ref, z_ref, h_ref, qkv1_ref, qkv4_ref, qkv16_ref, *stage):
        xv = x_ref[...]
        rs = lax.rsqrt(jnp.mean(xv * xv, axis=-1, keepdims=True) + NORM_EPS)
        h = (xv * rs * g_ref[...]).astype(BF16)
        h_ref[...] = h
        z = _dot_nt(h, w_ref[...])
        z_ref[...] = z
        qkv = z[:, C_DQ * GROUP_W:(C_DV + 1) * GROUP_W]
        qkv1_ref[...] = qkv.astype(BF16)
        _deinterleave_store(qkv, stage, ((PATTERN_DILS[1], qkv4_ref), (PATTERN_DILS[2], qkv16_ref)))

    return pl.pallas_call(
        body, name="in_fwd", grid=(s // TILE_IN,),
        in_specs=[pl.BlockSpec((TILE_IN, D_MODEL), lambda i: (i, 0)),
                  pl.BlockSpec((1, D_MODEL), lambda i: (0, 0)),
                  pl.BlockSpec((D_IN, D_MODEL), lambda i: (0, 0))],
        out_specs=[pl.BlockSpec((TILE_IN, D_IN), lambda i: (i, 0)),
                   pl.BlockSpec((TILE_IN, D_MODEL), lambda i: (i, 0)),
                   pl.BlockSpec((TILE_IN, qkv_w), lambda i: (i, 0)),
                   _residue_block(PATTERN_DILS[1], TILE_IN, qkv_w),
                   _residue_block(PATTERN_DILS[2], TILE_IN, qkv_w)],
        out_shape=[jax.ShapeDtypeStruct((s, D_IN), F32), jax.ShapeDtypeStruct((s, D_MODEL), BF16),
                   jax.ShapeDtypeStruct((s, qkv_w), BF16),
                   _by_residue(s, PATTERN_DILS[1], qkv_w, BF16), _by_residue(s, PATTERN_DILS[2], qkv_w, BF16)],
        scratch_shapes=_stage_scratch(TILE_IN, qkv_w, 1),
        compiler_params=_params(("parallel",)),
    )(x, g, w)


def out_fwd(y, w, x):
    s = x.shape[0]

    def body(y_ref, w_ref, x_ref, o_ref):
        o_ref[...] = x_ref[...] + _dot(y_ref[...], w_ref[...])

    return pl.pallas_call(
        body, name="out_fwd", grid=(s // TILE_IN,),
        in_specs=[pl.BlockSpec((TILE_IN, D_MIX), lambda i: (i, 0)),
                  pl.BlockSpec((D_MIX, D_MODEL), lambda i: (0, 0)),
                  pl.BlockSpec((TILE_IN, D_MODEL), lambda i: (i, 0))],
        out_specs=pl.BlockSpec((TILE_IN, D_MODEL), lambda i: (i, 0)),
        out_shape=jax.ShapeDtypeStruct((s, D_MODEL), F32),
        compiler_params=_params(("parallel",)),
    )(y, w, x)


def out_bwd(dx, w, z, o):
    s = dx.shape[0]
    abc = 3 * GROUP_W

    def body(dx_ref, w_ref, dg_ref, o_ref, dy_ref, ddg_ref, do1_ref, do4_ref, do16_ref, dl1_ref, dl4_ref, dl16_ref,
             *stage):
        stage_a, stage_b = stage[:2], stage[2:]
        dy = _dot_nt(dx_ref[...].astype(BF16), w_ref[...])
        dy_ref[...] = dy[:, :abc]
        dyd = dy[:, abc:]
        head = _head_of_lane((TILE_IN, GROUP_W))
        dg = dg_ref[...]
        o = o_ref[...]
        do = dyd * _silu(dg)
        ddg_ref[...] = dyd * o * _dsilu(dg)
        prod = do * o
        dl = jnp.zeros_like(prod)
        for h in range(N_HEADS):
            sm = jnp.sum(jnp.where(head == h, prod, 0.0), axis=-1, keepdims=True)
            dl = jnp.where(head == h, sm, dl)
        do1_ref[...] = do.astype(BF16)
        dl1_ref[...] = dl
        _deinterleave_store(do, stage_a, ((PATTERN_DILS[1], do4_ref), (PATTERN_DILS[2], do16_ref)))
        _deinterleave_store(dl, stage_b, ((PATTERN_DILS[1], dl4_ref), (PATTERN_DILS[2], dl16_ref)))

    row = pl.BlockSpec((TILE_IN, GROUP_W), lambda i: (i, 0))
    r4 = _residue_block(PATTERN_DILS[1], TILE_IN, GROUP_W)
    r16 = _residue_block(PATTERN_DILS[2], TILE_IN, GROUP_W)
    return pl.pallas_call(
        body, name="out_bwd", grid=(s // TILE_IN,),
        in_specs=[pl.BlockSpec((TILE_IN, D_MODEL), lambda i: (i, 0)),
                  pl.BlockSpec((D_MIX, D_MODEL), lambda i: (0, 0)),
                  pl.BlockSpec((TILE_IN, GROUP_W), lambda i: (i, C_DG)), row],
        out_specs=[pl.BlockSpec((TILE_IN, abc), lambda i: (i, 0)), row, row, r4, r16, row, r4, r16],
        out_shape=[jax.ShapeDtypeStruct((s, abc), F32), jax.ShapeDtypeStruct((s, GROUP_W), F32),
                   jax.ShapeDtypeStruct((s, GROUP_W), BF16),
                   _by_residue(s, PATTERN_DILS[1], GROUP_W, BF16), _by_residue(s, PATTERN_DILS[2], GROUP_W, BF16),
                   jax.ShapeDtypeStruct((s, GROUP_W), F32),
                   _by_residue(s, PATTERN_DILS[1], GROUP_W, F32), _by_residue(s, PATTERN_DILS[2], GROUP_W, F32)],
        scratch_shapes=_stage_scratch(TILE_IN, GROUP_W, 2),
        compiler_params=_params(("parallel",)),
    )(dx, w, z, o)


def in_bwd(dz, w, x, g, dx_next):
    s = x.shape[0]

    def body(dz_ref, w_ref, x_ref, g_ref, dxn_ref, dx_ref, dg_ref):
        @pl.when(pl.program_id(0) == 0)
        def _():
            dg_ref[...] = jnp.zeros_like(dg_ref)

        dh = _dot(dz_ref[...], w_ref[...])
        xv = x_ref[...]
        rs = lax.rsqrt(jnp.mean(xv * xv, axis=-1, keepdims=True) + NORM_EPS)
        xh = xv * rs
        dg_ref[...] += _put_row(dg_ref.shape, 0, jnp.sum(dh * xh, axis=0, keepdims=True))
        dn = dh * g_ref[...]
        dx_ref[...] = dxn_ref[...] + rs * (dn - xh * jnp.mean(dn * xh, axis=-1, keepdims=True))

    return pl.pallas_call(
        body, name="in_bwd", grid=(s // TILE_IN,),
        in_specs=[pl.BlockSpec((TILE_IN, D_IN), lambda i: (i, 0)),
                  pl.BlockSpec((D_IN, D_MODEL), lambda i: (0, 0)),
                  pl.BlockSpec((TILE_IN, D_MODEL), lambda i: (i, 0)),
                  pl.BlockSpec((1, D_MODEL), lambda i: (0, 0)),
                  pl.BlockSpec((TILE_IN, D_MODEL), lambda i: (i, 0))],
        out_specs=[pl.BlockSpec((TILE_IN, D_MODEL), lambda i: (i, 0)),
                   pl.BlockSpec((SUBLANES, D_MODEL), lambda i: (0, 0))],
        out_shape=[jax.ShapeDtypeStruct((s, D_MODEL), F32), jax.ShapeDtypeStruct((SUBLANES, D_MODEL), F32)],
        compiler_params=_params(("arbitrary",)),
    )(dz, w, x, g, dx_next)


def matmul_tn(a, b, n_split):
    s, m = a.shape
    n = b.shape[1]
    tn = n // n_split

    def body(a_ref, b_ref, o_ref):
        @pl.when(pl.program_id(1) == 0)
        def _():
            o_ref[...] = jnp.zeros_like(o_ref)

        o_ref[...] += _dot_tn(a_ref[...], b_ref[...].astype(BF16))

    return pl.pallas_call(
        body, name="matmul_tn", grid=(n_split, s // TILE_DW),
        in_specs=[pl.BlockSpec((TILE_DW, m), lambda j, k: (k, 0)),
                  pl.BlockSpec((TILE_DW, tn), lambda j, k: (k, j))],
        out_specs=pl.BlockSpec((m, tn), lambda j, k: (0, j)),
        out_shape=jax.ShapeDtypeStruct((m, n), F32),
        compiler_params=_params(("parallel", "arbitrary")),
    )(a, b)


def grad_w_in(h, dz, chip):
    s = h.shape[0]
    rows = D_IN // N_CHIPS

    def body(chip_ref, h_ref, dz_ref, staged_ref, own_ref, acc):
        k = pl.program_id(0)

        @pl.when(k == 0)
        def _():
            acc[...] = jnp.zeros_like(acc)

        acc[...] += _dot_tn(dz_ref[...], h_ref[...])

        @pl.when(k == s // TILE_DW - 1)
        def _():
            for j in range(N_CHIPS):
                part = acc[j * rows:(j + 1) * rows, :]
                staged_ref[j] = part.astype(BF16)

                @pl.when(chip_ref[0] == j)
                def _():
                    own_ref[...] = part

    return pl.pallas_call(
        body, name="grad_w_in",
        grid_spec=pltpu.PrefetchScalarGridSpec(
            num_scalar_prefetch=1, grid=(s // TILE_DW,),
            in_specs=[pl.BlockSpec((TILE_DW, D_MODEL), lambda k, c: (k, 0)),
                      pl.BlockSpec((TILE_DW, D_IN), lambda k, c: (k, 0))],
            out_specs=[pl.BlockSpec((N_CHIPS, rows, D_MODEL), lambda k, c: (0, 0, 0)),
                       pl.BlockSpec((rows, D_MODEL), lambda k, c: (0, 0))],
            scratch_shapes=[pltpu.VMEM((D_IN, D_MODEL), F32)]),
        out_shape=[jax.ShapeDtypeStruct((N_CHIPS, rows, D_MODEL), BF16), jax.ShapeDtypeStruct((rows, D_MODEL), F32)],
        compiler_params=_params(("arbitrary",)),
    )(chip, h, dz)


def loss_head(x, g, tgt):
    s = x.shape[0]

    def body(x_ref, g_ref, t_ref, l_ref, dx_ref, dg_ref):
        @pl.when(pl.program_id(0) == 0)
        def _():
            l_ref[...] = jnp.zeros_like(l_ref)
            dg_ref[...] = jnp.zeros_like(dg_ref)

        xv = x_ref[...]
        gv = g_ref[...]
        rs = lax.rsqrt(jnp.mean(xv * xv, axis=-1, keepdims=True) + NORM_EPS)
        xh = xv * rs
        e = xh * gv - t_ref[...]
        part = 0.5 * jnp.sum(jnp.mean(e * e, axis=-1, keepdims=True), axis=0, keepdims=True)
        l_ref[...] += jnp.broadcast_to(part, l_ref.shape)
        dy = e * (1.0 / D_MODEL)
        dg_ref[...] += _put_row(dg_ref.shape, 0, jnp.sum(dy * xh, axis=0, keepdims=True))
        dn = dy * gv
        dx_ref[...] = rs * (dn - xh * jnp.mean(dn * xh, axis=-1, keepdims=True))

    return pl.pallas_call(
        body, name="loss_head", grid=(s // TILE_IN,),
        in_specs=[pl.BlockSpec((TILE_IN, D_MODEL), lambda i: (i, 0)),
                  pl.BlockSpec((1, D_MODEL), lambda i: (0, 0)),
                  pl.BlockSpec((TILE_IN, D_MODEL), lambda i: (i, 0))],
        out_specs=[pl.BlockSpec((SUBLANES, 128), lambda i: (0, 0)),
                   pl.BlockSpec((TILE_IN, D_MODEL), lambda i: (i, 0)),
                   pl.BlockSpec((SUBLANES, D_MODEL), lambda i: (0, 0))],
        out_shape=[jax.ShapeDtypeStruct((SUBLANES, 128), F32), jax.ShapeDtypeStruct((s, D_MODEL), F32),
                   jax.ShapeDtypeStruct((SUBLANES, D_MODEL), F32)],
        compiler_params=_params(("arbitrary",)),
    )(x, g, tgt)


def _attn_bias(dil):
    qi = np.arange(ATTN_BLOCK)[:, None]
    ki = np.arange(2 * ATTN_BLOCK)[None, :]
    delta = qi + ATTN_BLOCK - ki
    band = (delta >= 0) & (delta <= ATTN_BLOCK)
    out = np.empty((2, N_HEADS, ATTN_BLOCK, 2 * ATTN_BLOCK), np.float32)
    for f in range(2):
        ok = band & ((ki >= ATTN_BLOCK) | (f == 0))
        for h in range(N_HEADS):
            out[f, h] = np.where(ok, -ALIBI_SLOPES[h] * dil * delta, NEG_BIG)
    return jnp.asarray(out.reshape(2, N_HEADS * ATTN_BLOCK, 2 * ATTN_BLOCK))


def _stack_heads(a, head):
    return jnp.concatenate([jnp.where(head == h, a, jnp.zeros_like(a)) for h in range(N_HEADS)], axis=0)


def _unstack_heads(a, head):
    out = a[:ATTN_BLOCK]
    for h in range(1, N_HEADS):
        out = jnp.where(head == h, a[h * ATTN_BLOCK:(h + 1) * ATTN_BLOCK], out)
    return out


def _head_column(a):
    return jnp.concatenate([a[:, h * HEAD_DIM:h * HEAD_DIM + 1] for h in range(N_HEADS)], axis=0)


def _attn_specs(n_blocks):
    rows = ATTN_QB * ATTN_BLOCK
    cur = lambda c: pl.BlockSpec((rows, GROUP_W), lambda n, c=c: (n, c))
    prev = lambda c: pl.BlockSpec((ATTN_BLOCK, GROUP_W), lambda n, c=c: (jnp.maximum(n * ATTN_QB - 1, 0), c))
    nxt = lambda c: pl.BlockSpec((ATTN_BLOCK, GROUP_W),
                                 lambda n, c=c: (jnp.minimum(n * ATTN_QB + ATTN_QB, n_blocks - 1), c))
    return cur, prev, nxt


def _keys(kp_ref, k_ref, j):
    prev = kp_ref[...] if j == 0 else k_ref[(j - 1) * ATTN_BLOCK:j * ATTN_BLOCK, :]
    return jnp.concatenate([prev, k_ref[j * ATTN_BLOCK:(j + 1) * ATTN_BLOCK, :]], axis=0)


def attn_fwd(qkv, dil):
    s = qkv.shape[0]
    n_blocks = s // ATTN_BLOCK
    bps = n_blocks // dil
    rows = ATTN_QB * ATTN_BLOCK

    def body(q_ref, kp_ref, k_ref, vp_ref, v_ref, bias_ref, o_ref, lse_ref):
        n = pl.program_id(0)
        head = _head_of_lane((ATTN_BLOCK, GROUP_W))
        for j in range(ATTN_QB):
            sl = slice(j * ATTN_BLOCK, (j + 1) * ATTN_BLOCK)
            first = (((n * ATTN_QB + j) % bps) == 0).astype(jnp.int32)
            qs = _stack_heads(q_ref[sl, :], head)
            sc = _dot_nt(qs, _keys(kp_ref, k_ref, j)) * ATTN_SCALE + bias_ref[first]
            m = jnp.max(sc, axis=-1, keepdims=True)
            pr = jnp.exp(sc - m)
            l = jnp.sum(pr, axis=-1, keepdims=True)
            oh = _dot(pr.astype(BF16), _keys(vp_ref, v_ref, j)) / l
            o_ref[sl, :] = _unstack_heads(oh, head)
            lse_ref[sl, :] = _unstack_heads(jnp.broadcast_to(m + jnp.log(l), oh.shape), head)

    cur, prev, _ = _attn_specs(n_blocks)
    bias = _attn_bias(dil)
    out = jax.ShapeDtypeStruct((s, GROUP_W), F32)
    return pl.pallas_call(
        body, name=f"attn_fwd_d{dil}", grid=(n_blocks // ATTN_QB,),
        in_specs=[cur(0), prev(1), cur(1), prev(2), cur(2), pl.BlockSpec(bias.shape, lambda n: (0, 0, 0))],
        out_specs=[cur(0), cur(0)],
        out_shape=[out, out],
        compiler_params=_params(("parallel",)),
    )(qkv, qkv, qkv, qkv, qkv, bias)


def attn_bwd(qkv, do, lse, dlt, dil):
    s = qkv.shape[0]
    n_blocks = s // ATTN_BLOCK
    bps = n_blocks // dil
    rows = ATTN_QB * ATTN_BLOCK

    def body(q_ref, qn_ref, kp_ref, k_ref, vp_ref, v_ref, do_ref, don_ref, lse_ref, lsen_ref, dl_ref, dln_ref,
             bias_ref, out_ref, dk_acc, dv_acc):
        n = pl.program_id(0)
        head = _head_of_lane((ATTN_BLOCK, GROUP_W))
        dk_acc[...] = jnp.zeros_like(dk_acc)
        dv_acc[...] = jnp.zeros_like(dv_acc)

        def pair(qj, doj, lsej, dlj, kk, vv, bias, keep):
            qs = _stack_heads(qj, head)
            dos = _stack_heads(doj, head)
            sc = _dot_nt(qs, kk) * ATTN_SCALE + bias
            if keep is None:
                pr = jnp.exp(sc - _head_column(lsej))
            else:
                pr = jnp.exp(jnp.minimum(sc - _head_column(lsej), 0.0)) * keep
            dp = _dot_nt(dos, vv)
            ds = (pr * (dp - _head_column(dlj)) * ATTN_SCALE).astype(BF16)
            return ds, _dot_tn(ds, qs), _dot_tn(pr.astype(BF16), dos)

        for j in range(ATTN_QB):
            sl = slice(j * ATTN_BLOCK, (j + 1) * ATTN_BLOCK)
            first = (((n * ATTN_QB + j) % bps) == 0).astype(jnp.int32)
            kk = _keys(kp_ref, k_ref, j)
            ds, dks, dvs = pair(q_ref[sl, :], do_ref[sl, :], lse_ref[sl, :], dl_ref[sl, :],
                                kk, _keys(vp_ref, v_ref, j), bias_ref[first], None)
            out_ref[sl, 0:GROUP_W] = _unstack_heads(_dot(ds, kk), head)
            acc = slice(j * ATTN_BLOCK, (j + 2) * ATTN_BLOCK)
            dk_acc[acc, :] += dks
            dv_acc[acc, :] += dvs

        nxt = n * ATTN_QB + ATTN_QB
        valid = ((nxt < n_blocks) & ((nxt % bps) != 0)).astype(F32)
        last = slice((ATTN_QB - 1) * ATTN_BLOCK, ATTN_QB * ATTN_BLOCK)
        _, dks, dvs = pair(qn_ref[...], don_ref[...], lsen_ref[...], dln_ref[...], k_ref[last, :], v_ref[last, :],
                           bias_ref[0][:, :ATTN_BLOCK], valid)
        acc = slice(ATTN_QB * ATTN_BLOCK, (ATTN_QB + 1) * ATTN_BLOCK)
        dk_acc[acc, :] += dks
        dv_acc[acc, :] += dvs
        out_ref[:, GROUP_W:2 * GROUP_W] = dk_acc[ATTN_BLOCK:, :]
        out_ref[:, 2 * GROUP_W:3 * GROUP_W] = dv_acc[ATTN_BLOCK:, :]

    cur, prev, nxt = _attn_specs(n_blocks)
    bias = _attn_bias(dil)
    return pl.pallas_call(
        body, name=f"attn_bwd_d{dil}", grid=(n_blocks // ATTN_QB,),
        in_specs=[cur(0), nxt(0), prev(1), cur(1), prev(2), cur(2), cur(0), nxt(0), cur(0), nxt(0), cur(0), nxt(0),
                  pl.BlockSpec(bias.shape, lambda n: (0, 0, 0))],
        out_specs=pl.BlockSpec((rows, 3 * GROUP_W), lambda n: (n, 0)),
        out_shape=jax.ShapeDtypeStruct((s, 3 * GROUP_W), F32),
        scratch_shapes=[pltpu.VMEM(((ATTN_QB + 1) * ATTN_BLOCK, GROUP_W), F32),
                        pltpu.VMEM(((ATTN_QB + 1) * ATTN_BLOCK, GROUP_W), F32)],
        compiler_params=_params(("parallel",)),
    )(qkv, qkv, qkv, qkv, qkv, qkv, do, do, lse, lse, dlt, dlt, bias)


def _zcol(c):
    return pl.BlockSpec((TILE_MIX, GROUP_W), lambda i, c=c: (i, c))


def _zhalo(c):
    per = TILE_MIX // SUBLANES
    return pl.BlockSpec((SUBLANES, GROUP_W), lambda i, c=c: (jnp.maximum(i * per - 1, 0), c))


def _full(shape):
    return pl.BlockSpec(shape, lambda i: tuple(0 for _ in shape))


def _of_layer(a, l):
    rest = a.shape[1:]
    return pl.BlockSpec((None,) + rest, lambda i: (l,) + tuple(0 for _ in rest))


def _softplus_neg(lam):
    nl = -lam
    return jnp.maximum(nl, 0.0) + jnp.log1p(jnp.exp(-jnp.abs(nl)))


def _lru_gates(xb, wa_ref, wx_ref, ba, bx, lam):
    xbb = xb.astype(BF16)
    r = jax.nn.sigmoid(_dot(xbb, wa_ref[...]) + ba)
    ig = jax.nn.sigmoid(_dot(xbb, wx_ref[...]) + bx)
    log_a = (-RG_C * r) * _softplus_neg(lam)
    a = jnp.exp(log_a)
    mult = jnp.sqrt(_neg_expm1(2.0 * log_a))
    return r, ig, a, mult


LRU_SAVED = 5


def _gmlp_spatial(ws_ref, vvb, head):
    outs = []
    for j in range(vvb.shape[0] // GMLP_CHUNK):
        blk = vvb[j * GMLP_CHUNK:(j + 1) * GMLP_CHUNK, :]
        acc = jnp.zeros((GMLP_CHUNK, GROUP_W), F32)
        for h in range(N_HEADS):
            acc = jnp.where(head[:GMLP_CHUNK] == h, _dot(ws_ref[h], blk), acc)
        outs.append(acc)
    return jnp.concatenate(outs, axis=0)


def mix_fwd(z, attn, wts, l):
    s = z.shape[0]
    d4, d16 = PATTERN_DILS[1], PATTERN_DILS[2]

    def body(ax_ref, ab_ref, ac_ref, ag_ref, rx_ref, rg_ref, cu_ref, cv_ref, cg_ref, dg_ref,
             axh_ref, ach_ref, rxh_ref, o1_ref, l1_ref, o4_ref, l4_ref, o16_ref, l16_ref,
             caw_ref, crw_ref, crb_ref, wa_ref, wx_ref, ba_ref, bx_ref, lam_ref, gng_ref, ws_ref, bs_ref,
             y_ref, hl_ref, o_ref, lse_ref, lse4_ref, lse16_ref, lru_ref, carry, *stage):
        st_a, st_b, st_c, st_d, st_e = (stage[2 * k:2 * k + 2] for k in range(5))
        i = pl.program_id(0)

        @pl.when(i == 0)
        def _():
            carry[...] = jnp.zeros_like(carry)

        nz = (i > 0).astype(F32)
        head = _head_of_lane((TILE_MIX, GROUP_W))

        pa = ac_ref[...] * ax_ref[...]
        pah = ach_ref[...] * axh_ref[...] * nz
        cv = caw_ref[2:3, :] * pa + caw_ref[1:2, :] * _shift_down(pa, pah, 1) + caw_ref[0:1, :] * _shift_down(pa, pah, 2)
        y_ref[:, 0:GROUP_W] = (ab_ref[...] * cv * _silu(ag_ref[...])).astype(BF16)

        rx = rx_ref[...]
        rxh = rxh_ref[...] * nz
        xb = (crw_ref[3:4, :] * rx + crw_ref[2:3, :] * _shift_down(rx, rxh, 1) + crw_ref[1:2, :] * _shift_down(rx, rxh, 2)
              + crw_ref[0:1, :] * _shift_down(rx, rxh, 3) + crb_ref[...])
        r, ig, a, mult = _lru_gates(xb, wa_ref, wx_ref, ba_ref[...], bx_ref[...], lam_ref[...])
        for k, val in enumerate((xb, r, ig, a, mult)):
            lru_ref[:, k * GROUP_W:(k + 1) * GROUP_W] = val
        hl = _scan_causal(a, mult * (ig * xb), carry[SUBLANES - 1:SUBLANES, :])
        hl_ref[...] = hl
        carry[...] = hl[TILE_MIX - SUBLANES:, :]
        y_ref[:, GROUP_W:2 * GROUP_W] = (hl * _silu(rg_ref[...])).astype(BF16)

        u = _gelu(cu_ref[...])
        gv = _gelu(cv_ref[...])
        rs = lax.rsqrt(jnp.mean(gv * gv, axis=-1, keepdims=True) + NORM_EPS)
        vvb = (gv * rs * gng_ref[...]).astype(BF16)
        sp = _gmlp_spatial(ws_ref, vvb, head) + jnp.concatenate([bs_ref[...]] * (TILE_MIX // GMLP_CHUNK), axis=0)
        y_ref[:, 2 * GROUP_W:3 * GROUP_W] = (u * sp * _silu(cg_ref[...])).astype(BF16)

        ops = (o1_ref[...], _interleave_load(o4_ref, d4, st_a), _interleave_load(o16_ref, d16, st_b))
        lps = (l1_ref[...], _interleave_load(l4_ref, d4, st_c), _interleave_load(l16_ref, d16, st_d))
        m = jnp.maximum(jnp.maximum(lps[0], lps[1]), lps[2])
        zsum = jnp.zeros_like(m)
        o = jnp.zeros_like(m)
        for op, lp in zip(ops, lps):
            w = jnp.exp(lp - m)
            zsum = zsum + w
            o = o + w * op
        o = o / zsum
        lse = m + jnp.log(zsum)
        o_ref[...] = o
        lse_ref[...] = lse
        _deinterleave_store(lse, st_e, ((d4, lse4_ref), (d16, lse16_ref)))
        y_ref[:, 3 * GROUP_W:4 * GROUP_W] = (o * _silu(dg_ref[...])).astype(BF16)

    row = pl.BlockSpec((TILE_MIX, GROUP_W), lambda i: (i, 0))
    r4 = _residue_block(d4, TILE_MIX, GROUP_W)
    r16 = _residue_block(d16, TILE_MIX, GROUP_W)
    names = ("caw", "crw", "crb", "wa", "wx", "ba", "bx", "lam", "gng", "ws", "bs")
    in_specs = ([_zcol(c) for c in (C_AX, C_AB, C_AC, C_AG, C_RX, C_RG, C_CU, C_CV, C_CG, C_DG)]
                + [_zhalo(C_AX), _zhalo(C_AC), _zhalo(C_RX), row, row, r4, r4, r16, r16]
                + [_of_layer(wts[k], l) for k in names])
    return pl.pallas_call(
        body, name="mix_fwd", grid=(s // TILE_MIX,),
        in_specs=in_specs,
        out_specs=[pl.BlockSpec((TILE_MIX, D_MIX), lambda i: (i, 0)), row, row, row, r4, r16,
                   pl.BlockSpec((TILE_MIX, LRU_SAVED * GROUP_W), lambda i: (i, 0))],
        out_shape=([jax.ShapeDtypeStruct((s, D_MIX), BF16)] + [jax.ShapeDtypeStruct((s, GROUP_W), F32)] * 3
                   + [_by_residue(s, d4, GROUP_W, F32), _by_residue(s, d16, GROUP_W, F32),
                      jax.ShapeDtypeStruct((s, LRU_SAVED * GROUP_W), F32)]),
        scratch_shapes=[pltpu.VMEM((SUBLANES, GROUP_W), F32)] + _stage_scratch(TILE_MIX, GROUP_W, 5),
        compiler_params=_params(("arbitrary",)),
    )(*([z] * 13), *[a for pair in attn for a in pair], *[wts[k] for k in names])


def mix_bwd(dy, z, hl, lru, dqkv, ddg, wts, l):
    s = z.shape[0]
    d4, d16 = PATTERN_DILS[1], PATTERN_DILS[2]
    n_tiles = s // TILE_MIX

    def body(dya_ref, dyb_ref, dyc_ref, ax_ref, ab_ref, ac_ref, ag_ref, rx_ref, rg_ref, cu_ref, cv_ref, cg_ref,
             axh_ref, ach_ref, rxh_ref, hl_ref, hlh_ref, lru_ref, dqkv1_ref, dqkv4_ref, dqkv16_ref, ddg_ref,
             caw_ref, crw_ref, crb_ref, wa_ref, wx_ref, ba_ref, bx_ref, lam_ref, gng_ref, ws_ref, wst_ref, bs_ref,
             dz_ref, ga_ref, gr_ref, gn_ref, gwa_ref, gwx_ref, gws_ref, gbs_ref,
             c_dcv, c_g, c_a, c_dxb, *stage):
        st_a, st_b = stage[:len(stage) // 2], stage[len(stage) // 2:]
        step = pl.program_id(0)
        i = n_tiles - 1 - step

        @pl.when(step == 0)
        def _():
            for r in (c_dcv, c_g, c_a, c_dxb, ga_ref, gr_ref, gn_ref, gwa_ref, gwx_ref, gws_ref, gbs_ref):
                r[...] = jnp.zeros_like(r)

        nz = (i > 0).astype(F32)
        head = _head_of_lane((TILE_MIX, GROUP_W))
        shp8 = (SUBLANES, GROUP_W)
        colsum = lambda v: jnp.sum(v, axis=0, keepdims=True)

        ax, ab, ac, ag = ax_ref[...], ab_ref[...], ac_ref[...], ag_ref[...]
        dya = dya_ref[...]
        pa = ac * ax
        pah = ach_ref[...] * axh_ref[...] * nz
        pa1 = _shift_down(pa, pah, 1)
        pa2 = _shift_down(pa, pah, 2)
        cv = caw_ref[2:3, :] * pa + caw_ref[1:2, :] * pa1 + caw_ref[0:1, :] * pa2
        sg = _silu(ag)
        dz_ref[:, C_AB * GROUP_W:(C_AB + 1) * GROUP_W] = (dya * cv * sg).astype(BF16)
        dz_ref[:, C_AG * GROUP_W:(C_AG + 1) * GROUP_W] = (dya * ab * cv * _dsilu(ag)).astype(BF16)
        dcv = dya * ab * sg
        nxt = c_dcv[...]
        dpa = caw_ref[2:3, :] * dcv + caw_ref[1:2, :] * _shift_up(dcv, nxt, 1) + caw_ref[0:1, :] * _shift_up(dcv, nxt, 2)
        c_dcv[...] = dcv[:SUBLANES, :]
        dz_ref[:, C_AC * GROUP_W:(C_AC + 1) * GROUP_W] = (dpa * ax).astype(BF16)
        dz_ref[:, C_AX * GROUP_W:(C_AX + 1) * GROUP_W] = (dpa * ac).astype(BF16)
        ga_ref[...] += (_put_row(shp8, 2, colsum(dcv * pa)) + _put_row(shp8, 1, colsum(dcv * pa1))
                        + _put_row(shp8, 0, colsum(dcv * pa2)))

        rx, rg = rx_ref[...], rg_ref[...]
        dyb = dyb_ref[...]
        rxh = rxh_ref[...] * nz
        rx1, rx2, rx3 = _shift_down(rx, rxh, 1), _shift_down(rx, rxh, 2), _shift_down(rx, rxh, 3)
        xb, r, ig, a, mult = (lru_ref[:, k * GROUP_W:(k + 1) * GROUP_W] for k in range(LRU_SAVED))
        lam = lam_ref[...]
        sp = _softplus_neg(lam)
        hl = hl_ref[...]
        hprev = _shift_down(hl, hlh_ref[...] * nz, 1)
        dz_ref[:, C_RG * GROUP_W:(C_RG + 1) * GROUP_W] = (dyb * hl * _dsilu(rg)).astype(BF16)
        dh = dyb * _silu(rg)
        a_next = _shift_up(a, c_a[...], 1)
        g = _scan_anticausal(a_next, dh, c_g[0:1, :])
        c_g[...] = g[:SUBLANES, :]
        c_a[...] = a[:SUBLANES, :]
        u = ig * xb
        da = g * hprev
        dmult = g * u
        du = g * mult
        dlog_a = da * a - dmult * (a * a) / mult
        dr = dlog_a * (-RG_C * sp)
        dga = dr * r * (1.0 - r)
        dgx = (du * xb) * ig * (1.0 - ig)
        dgab, dgxb = dga.astype(BF16), dgx.astype(BF16)
        dxb = du * ig + _dot_nt(dgab, wa_ref[...]) + _dot_nt(dgxb, wx_ref[...])
        xbb = xb.astype(BF16)
        gwa_ref[...] += _dot_tn(xbb, dgab)
        gwx_ref[...] += _dot_tn(xbb, dgxb)
        nxt = c_dxb[...]
        drx = (crw_ref[3:4, :] * dxb + crw_ref[2:3, :] * _shift_up(dxb, nxt, 1) + crw_ref[1:2, :] * _shift_up(dxb, nxt, 2)
               + crw_ref[0:1, :] * _shift_up(dxb, nxt, 3))
        c_dxb[...] = dxb[:SUBLANES, :]
        dz_ref[:, C_RX * GROUP_W:(C_RX + 1) * GROUP_W] = drx.astype(BF16)
        dlam = colsum(dlog_a * (-RG_C * r)) * (-jax.nn.sigmoid(-lam))
        gr_ref[...] += (_put_row(shp8, 3, colsum(dxb * rx)) + _put_row(shp8, 2, colsum(dxb * rx1))
                        + _put_row(shp8, 1, colsum(dxb * rx2)) + _put_row(shp8, 0, colsum(dxb * rx3))
                        + _put_row(shp8, 4, colsum(dxb)) + _put_row(shp8, 5, colsum(dga))
                        + _put_row(shp8, 6, colsum(dgx)) + _put_row(shp8, 7, dlam))

        cu, cvv, cg = cu_ref[...], cv_ref[...], cg_ref[...]
        dyc = dyc_ref[...]
        u_c, du_c = _gelu_and_grad(cu)
        gv, dgv_c = _gelu_and_grad(cvv)
        rs = lax.rsqrt(jnp.mean(gv * gv, axis=-1, keepdims=True) + NORM_EPS)
        vh = gv * rs
        gng = gng_ref[...]
        vvb = (vh * gng).astype(BF16)
        spat = _gmlp_spatial(ws_ref, vvb, head) + jnp.concatenate([bs_ref[...]] * (TILE_MIX // GMLP_CHUNK), axis=0)
        sgc = _silu(cg)
        dz_ref[:, C_CU * GROUP_W:(C_CU + 1) * GROUP_W] = (dyc * spat * sgc * du_c).astype(BF16)
        dz_ref[:, C_CG * GROUP_W:(C_CG + 1) * GROUP_W] = (dyc * u_c * spat * _dsilu(cg)).astype(BF16)
        dsp = dyc * u_c * sgc
        dspb = dsp.astype(BF16)
        tril = (lax.broadcasted_iota(jnp.int32, (GMLP_CHUNK, GMLP_CHUNK), 0)
                >= lax.broadcasted_iota(jnp.int32, (GMLP_CHUNK, GMLP_CHUNK), 1))
        head_c = head[:GMLP_CHUNK]
        dvv_parts = []
        gbs = jnp.zeros((GMLP_CHUNK, GROUP_W), F32)
        for j in range(TILE_MIX // GMLP_CHUNK):
            sl = slice(j * GMLP_CHUNK, (j + 1) * GMLP_CHUNK)
            dblk = dspb[sl, :]
            vblk = vvb[sl, :]
            gbs = gbs + dsp[sl, :]
            acc = jnp.zeros((GMLP_CHUNK, GROUP_W), F32)
            for h in range(N_HEADS):
                acc = jnp.where(head_c == h, _dot(wst_ref[h], dblk), acc)
                dm = jnp.where(head_c == h, dblk, jnp.zeros_like(dblk))
                gws_ref[h] += jnp.where(tril, _dot_nt(dm, vblk), 0.0)
            dvv_parts.append(acc)
        gbs_ref[...] += gbs
        dvv = jnp.concatenate(dvv_parts, axis=0)
        gn_ref[...] += _put_row(shp8, 0, colsum(dvv * vh))
        dvh = dvv * gng
        dgv = rs * (dvh - vh * jnp.mean(dvh * vh, axis=-1, keepdims=True))
        dz_ref[:, C_CV * GROUP_W:(C_CV + 1) * GROUP_W] = (dgv * dgv_c).astype(BF16)

        dsum = dqkv1_ref[...] + _interleave_load(dqkv4_ref, d4, st_a) + _interleave_load(dqkv16_ref, d16, st_b)
        dz_ref[:, C_DQ * GROUP_W:(C_DV + 1) * GROUP_W] = dsum.astype(BF16)
        dz_ref[:, C_DG * GROUP_W:(C_DG + 1) * GROUP_W] = ddg_ref[...].astype(BF16)

    per = TILE_MIX // SUBLANES
    qkv_w = 3 * GROUP_W
    rev = lambda c: pl.BlockSpec((TILE_MIX, GROUP_W), lambda t, c=c: (n_tiles - 1 - t, c))
    revh = lambda c: pl.BlockSpec((SUBLANES, GROUP_W),
                                  lambda t, c=c: (jnp.maximum((n_tiles - 1 - t) * per - 1, 0), c))
    revr = lambda dil: pl.BlockSpec((dil, TILE_MIX // dil, qkv_w), lambda t: (0, n_tiles - 1 - t, 0))
    names = ("caw", "crw", "crb", "wa", "wx", "ba", "bx", "lam", "gng", "ws", "wst", "bs")
    in_specs = ([rev(0), rev(1), rev(2)]
                + [rev(c) for c in (C_AX, C_AB, C_AC, C_AG, C_RX, C_RG, C_CU, C_CV, C_CG)]
                + [revh(C_AX), revh(C_AC), revh(C_RX), rev(0), revh(0),
                   pl.BlockSpec((TILE_MIX, LRU_SAVED * GROUP_W), lambda t: (n_tiles - 1 - t, 0)),
                   pl.BlockSpec((TILE_MIX, qkv_w), lambda t: (n_tiles - 1 - t, 0)), revr(d4), revr(d16), rev(0)]
                + [_of_layer(wts[k], l) for k in names])
    small = jax.ShapeDtypeStruct((SUBLANES, GROUP_W), F32)
    sq = jax.ShapeDtypeStruct((GROUP_W, GROUP_W), F32)
    out_shape = [jax.ShapeDtypeStruct((s, D_IN), BF16), small, small, small, sq, sq,
                 jax.ShapeDtypeStruct((N_HEADS, GMLP_CHUNK, GMLP_CHUNK), F32),
                 jax.ShapeDtypeStruct((GMLP_CHUNK, GROUP_W), F32)]
    out_specs = ([pl.BlockSpec((TILE_MIX, D_IN), lambda t: (n_tiles - 1 - t, 0))]
                 + [_full(o.shape) for o in out_shape[1:]])
    return pl.pallas_call(
        body, name="mix_bwd", grid=(n_tiles,),
        in_specs=in_specs, out_specs=out_specs, out_shape=out_shape,
        scratch_shapes=[pltpu.VMEM((SUBLANES, GROUP_W), F32)] * 4 + _stage_scratch(TILE_MIX, qkv_w, 2),
        compiler_params=_params(("arbitrary",)),
    )(dy, dy, dy, *([z] * 12), hl, hl, lru, *dqkv, ddg, *[wts[k] for k in names])


def prepare_small_weights(p):
    tril = jnp.tril(jnp.ones((GMLP_CHUNK, GMLP_CHUNK), dtype=bool))
    ws = jnp.where(tril, p["gmlp_ws"], 0.0).astype(BF16)
    row = lambda a: a[:, None, :]
    eye = jnp.eye(N_HEADS, dtype=F32)
    bd = lambda w: (w[:, :, :, None, :] * eye[None, :, None, :, None]).reshape(-1, GROUP_W, GROUP_W).astype(BF16)
    return dict(
        caw=p["conv_a_w"], crw=p["conv_r_w"], crb=row(p["conv_r_b"]),
        wa=bd(p["lru_wa"]), wx=bd(p["lru_wx"]),
        ba=row(p["lru_ba"]), bx=row(p["lru_bx"]), lam=row(p["lru_lambda"]), gng=row(p["gmlp_norm_g"]),
        ws=ws, wst=jnp.swapaxes(ws, 2, 3),
        bs=jnp.repeat(jnp.swapaxes(p["gmlp_bs"], 1, 2), HEAD_DIM, axis=2))


def _flat(a):
    return a.reshape(a.shape[0] * a.shape[1], a.shape[2])


def _split(a, dil):
    return a.reshape(dil, a.shape[0] // dil, a.shape[1])


def local_step(x, tgt, final_g, depth, chip, layer_weights, projections_done):
    saved = []
    for l in range(depth):
        gain, w_in_l, w_out_l, wts = layer_weights(l, x)
        z, h, *qkvs = in_fwd(x, gain, w_in_l)
        qkvs = [_flat(q) if q.ndim == 3 else q for q in qkvs]
        attn = []
        for q, d in zip(qkvs, PATTERN_DILS):
            o_p, lse_p = attn_fwd(q, d)
            attn.append((o_p, lse_p) if d == 1 else (_split(o_p, d), _split(lse_p, d)))
        y, hl, o, lse, lse4, lse16, lru = mix_fwd(z, attn, wts, l)
        saved.append(dict(x=x, z=z, h=h, y=y, hl=hl, o=o, qkvs=qkvs, lses=(lse, _flat(lse4), _flat(lse16)), wts=wts, lru=lru,
                          gain=gain, w_in=w_in_l, w_out=w_out_l))
        x = out_fwd(y, w_out_l, x)

    loss, dx, dfg = loss_head(x, final_g[None, :], tgt)
    raw = {k: [None] * depth for k in ("gain", "a", "r", "n", "wa", "wx", "ws", "bs")}
    zero = None
    for l in reversed(range(depth)):
        sv = saved[l]
        dy, ddg, do1, do4, do16, dl1, dl4, dl16 = out_bwd(dx, sv["w_out"], sv["z"], sv["o"])
        g_w_out = matmul_tn(sv["y"], dx, 1)
        dqkv = []
        for q, do, lse, dl, d in zip(sv["qkvs"], (do1, _flat(do4), _flat(do16)), sv["lses"],
                                     (dl1, _flat(dl4), _flat(dl16)), PATTERN_DILS):
            g = attn_bwd(q, do, lse, dl, d)
            dqkv.append(g if d == 1 else _split(g, d))
        dz, ga, gr, gn, gwa, gwx, gws, gbs = mix_bwd(dy, sv["z"], sv["hl"], sv["lru"], dqkv, ddg, sv["wts"], l)
        gain = sv["gain"] if zero is None else sv["gain"] + zero
        zero = projections_done(l, *grad_w_in(sv["h"], dz, chip), g_w_out)
        if l == 0 and zero is not None:
            gain = gain + zero
        dx, dgn = in_bwd(dz, sv["w_in"], sv["x"], gain, dx)
        for k, g in zip(("gain", "a", "r", "n", "wa", "wx", "ws", "bs"), (dgn, ga, gr, gn, gwa, gwx, gws, gbs)):
            raw[k][l] = g
    st = {k: jnp.stack(v) for k, v in raw.items()}
    eye = jnp.eye(N_HEADS, dtype=F32)[None, :, None, :, None]
    diag = lambda g: (g.reshape(depth, N_HEADS, HEAD_DIM, N_HEADS, HEAD_DIM) * eye).sum(axis=3)
    grads = dict(
        norm_g=st["gain"][:, 0], conv_a_w=st["a"][:, :3], conv_r_w=st["r"][:, :4], conv_r_b=st["r"][:, 4],
        lru_ba=st["r"][:, 5], lru_bx=st["r"][:, 6], lru_lambda=st["r"][:, 7], gmlp_norm_g=st["n"][:, 0],
        lru_wa=diag(st["wa"]), lru_wx=diag(st["wx"]), gmlp_ws=st["ws"],
        gmlp_bs=jnp.swapaxes(st["bs"].reshape(depth, GMLP_CHUNK, N_HEADS, HEAD_DIM).sum(-1), 1, 2),
        final_g=dfg[0])
    return loss, dx, grads


MESH = pl.DeviceIdType.MESH
N_CHIPS = 4
N_DEV = 8
ANY = pl.BlockSpec(memory_space=pl.ANY)


def _place():
    x, y, c = lax.axis_index("x"), lax.axis_index("y"), lax.axis_index("c")
    chips = [(1 - x, y), (x, 1 - y), (1 - x, 1 - y)]
    return x, y, c, chips


def _remote(src, dst, ssem, rsem, to):
    return pltpu.make_async_remote_copy(src_ref=src, dst_ref=dst, send_sem=ssem, recv_sem=rsem,
                                        device_id=to, device_id_type=MESH)


HBM = pl.BlockSpec(memory_space=pltpu.HBM)
SEM = pl.BlockSpec(memory_space=pltpu.SEMAPHORE)
DATAFLOW = pltpu.SideEffectType.DATAFLOW_SIDE_EFFECTING
GATHER, SCATTER = "gather", "scatter"


def _chip_copies(mode, src_refs, land_refs, ssem, rsem):
    x, y, c, chips = _place()
    me = 2 * x + y
    n = len(src_refs)
    copies = []
    for k, (cx, cy) in enumerate(chips):
        for a in range(n):
            if mode == GATHER:
                src, dst = src_refs[a], land_refs[a].at[me]
            else:
                src, dst = src_refs[a].at[2 * cx + cy], land_refs[a].at[k]
            copies.append(_remote(src, dst, ssem.at[n * k + a], rsem.at[n * k + a], (cx, cy, c)))
    return copies


def exchange_start(mode, srcs, after, name):
    n = len(srcs)
    if mode == GATHER:
        lands = [lax.empty((N_CHIPS,) + s.shape, s.dtype) for s in srcs]
    else:
        lands = [lax.empty((N_CHIPS - 1,) + s.shape[1:], s.dtype) for s in srcs]
    extra = [] if after is None else [after]

    def body(*refs):
        src_refs, land_refs = refs[:n], refs[n:2 * n]
        ssem, rsem = refs[2 * n + len(extra)], refs[2 * n + len(extra) + 1]
        token = refs[-1]
        for cp in _chip_copies(mode, src_refs, land_refs, ssem, rsem):
            cp.start()
        token[...] = jnp.zeros_like(token)

    arrays = list(srcs) + lands
    return pl.pallas_call(
        body, name=name,
        out_shape=(pltpu.SemaphoreType.DMA((3 * n,)), pltpu.SemaphoreType.DMA((3 * n,)),
                   *[pltpu.HBM(a.shape, a.dtype) for a in arrays], jax.ShapeDtypeStruct((SUBLANES, LANES), F32)),
        in_specs=[HBM] * (2 * n) + [ANY] * len(extra),
        out_specs=(SEM, SEM, *[HBM] * (2 * n), pl.BlockSpec(memory_space=pltpu.VMEM)),
        input_output_aliases={i: 2 + i for i in range(2 * n)},
        compiler_params=pltpu.CompilerParams(has_side_effects=DATAFLOW),
    )(*[pltpu.with_memory_space_constraint(a, pltpu.HBM) for a in arrays], *extra)


def exchange_wait(mode, started, after, name):
    ssem, rsem, *thru, _ = started
    n = len(thru) // 2

    def body(*refs):
        src_refs, land_refs = refs[:n], refs[n:2 * n]
        ssem_ref, rsem_ref = refs[2 * n], refs[2 * n + 1]
        for cp in _chip_copies(mode, src_refs, land_refs, ssem_ref, rsem_ref):
            cp.wait_send()
            cp.wait_recv()

    outs = pl.pallas_call(
        body, name=name,
        out_shape=[pltpu.HBM(a.shape, a.dtype) for a in thru],
        in_specs=[HBM] * (2 * n) + [SEM, SEM, ANY],
        out_specs=[HBM] * (2 * n),
        input_output_aliases={i: i for i in range(2 * n)},
        compiler_params=pltpu.CompilerParams(has_side_effects=DATAFLOW),
    )(*thru, ssem, rsem, after)
    return outs[n:]


def sibling_exchange(p1, p2):
    def body(p1_ref, p2_ref, q1_ref, q2_ref, ssem, rsem):
        x, y, c, _ = _place()
        copies = [_remote(p_ref, q_ref, ssem.at[a], rsem.at[a], (x, y, 1 - c))
                  for a, (p_ref, q_ref) in enumerate(((p1_ref, q1_ref), (p2_ref, q2_ref)))]
        for cp in copies:
            cp.start()
        for cp in copies:
            cp.wait()

    return pl.pallas_call(
        body, name="sibling_exchange",
        in_specs=[ANY, ANY], out_specs=[ANY, ANY],
        out_shape=[jax.ShapeDtypeStruct(p.shape, p.dtype) for p in (p1, p2)],
        scratch_shapes=[pltpu.SemaphoreType.DMA((2,)), pltpu.SemaphoreType.DMA((2,))],
    )(p1, p2)


def gather_all(v):
    m_per, n = v.shape

    def body(x_ref, out_ref, send_sems, recv_sems, local_sem):
        x, y, c, chips = _place()
        me, sibling = (x, y, c), (x, y, 1 - c)

        def rows(px, py, pc):
            return out_ref.at[pl.ds((4 * px + 2 * py + pc) * m_per, m_per), :]

        def copy(k, block, to, src=None):
            return _remote(rows(*block) if src is None else src, rows(*block), send_sems.at[k], recv_sems.at[k], to)

        mine = pltpu.make_async_copy(x_ref, rows(*me), local_sem)
        mine.start()
        first = [copy(0, me, sibling, src=x_ref)]
        first += [copy(1 + j, me, (*chip, c), src=x_ref) for j, chip in enumerate(chips)]
        for cp in first:
            cp.start()
        passed = [copy(4 + j, (*chip, c), sibling) for j, chip in enumerate(chips)]
        for j, chip in enumerate(chips):
            copy(1 + j, (*chip, c), me).wait_recv()
            passed[j].start()
        copy(0, sibling, me).wait_recv()
        for j, chip in enumerate(chips):
            copy(4 + j, (*chip, 1 - c), me).wait_recv()
        for cp in first + passed:
            cp.wait_send()
        mine.wait()

    return pl.pallas_call(
        body, name="gather_all",
        out_shape=jax.ShapeDtypeStruct((N_DEV * m_per, n), v.dtype),
        in_specs=[pl.BlockSpec(memory_space=pltpu.VMEM)],
        out_specs=pl.BlockSpec(memory_space=pltpu.VMEM),
        scratch_shapes=[pltpu.SemaphoreType.DMA((7,)), pltpu.SemaphoreType.DMA((7,)), pltpu.SemaphoreType.DMA],
        compiler_params=pltpu.CompilerParams(vmem_limit_bytes=VMEM_LIMIT),
    )(v)


TILE_ROWS = 256


def sum_partials(own, parts):
    k, r, c = parts.shape
    first = [] if own is None else [own]

    def body(*refs):
        p_ref, out_ref = refs[-2], refs[-1]
        acc = p_ref[0].astype(F32) if own is None else refs[0][...]
        for i in range(1 if own is None else 0, k):
            acc = acc + p_ref[i].astype(F32)
        out_ref[...] = acc

    row = pl.BlockSpec((TILE_ROWS, c), lambda i: (i, 0))
    return pl.pallas_call(
        body, name="sum_partials", grid=(r // TILE_ROWS,),
        in_specs=[row] * len(first) + [pl.BlockSpec((k, TILE_ROWS, c), lambda i: (0, i, 0))],
        out_specs=row,
        out_shape=jax.ShapeDtypeStruct((r, c), F32),
        compiler_params=_params(("parallel",)),
    )(*first, parts)


def _adamw_update(w, g, m, v):
    m2 = ADAM_B1 * m + (1.0 - ADAM_B1) * g
    v2 = ADAM_B2 * v + (1.0 - ADAM_B2) * (g * g)
    m_hat = m2 / (1.0 - ADAM_B1 ** ADAM_STEP)
    v_hat = v2 / (1.0 - ADAM_B2 ** ADAM_STEP)
    return -ADAM_LR * (m_hat / (jnp.sqrt(v_hat) + ADAM_EPS) + ADAM_WD * w), m2, v2


def adamw_small(ws, gs, ms, vs):
    n = len(ws)

    def body(*refs):
        ins, outs = refs[:4 * n], refs[4 * n:]
        for i in range(n):
            d, m2, v2 = _adamw_update(ins[i][...], ins[n + i][...], ins[2 * n + i][...], ins[3 * n + i][...])
            outs[3 * i][...] = d
            outs[3 * i + 1][...] = m2
            outs[3 * i + 2][...] = v2

    outs = pl.pallas_call(
        body, name="adamw_small",
        out_shape=[jax.ShapeDtypeStruct(w.shape, F32) for w in ws for _ in range(3)],
    )(*ws, *gs, *ms, *vs)
    return [tuple(outs[3 * i:3 * i + 3]) for i in range(n)]


def adamw(w, ga, gb, m, v):
    n, r, c = w.shape
    tile = max(t for t in range(SUBLANES, TILE_ROWS + 1, SUBLANES) if r % t == 0)

    def body(w_ref, ga_ref, gb_ref, m_ref, v_ref, g_ref, d_ref, m2_ref, v2_ref):
        g = ga_ref[...] + gb_ref[...]
        g_ref[...] = g
        d_ref[...], m2_ref[...], v2_ref[...] = _adamw_update(w_ref[...], g, m_ref[...], v_ref[...])

    spec = pl.BlockSpec((1, tile, c), lambda j, i: (j, i, 0))
    return pl.pallas_call(
        body, name="adamw", grid=(n, r // tile),
        in_specs=[spec] * 5, out_specs=[spec] * 4,
        out_shape=[jax.ShapeDtypeStruct((n, r, c), F32)] * 4,
        compiler_params=_params(("parallel", "parallel")),
    )(w, ga, gb, m, v)


REPLICATED = ("norm_g", "conv_r_b", "lru_wa", "lru_ba", "lru_wx", "lru_bx", "lru_lambda", "gmlp_norm_g",
              "gmlp_ws", "gmlp_bs", "final_g")
CHIP_SHARDED_SMALL = ("conv_a_w", "conv_r_w")
PACK_LANES = 128


def _pack(arrays):
    flat = jnp.concatenate([a.reshape(-1) for a in arrays])
    pad = (-flat.shape[0]) % (TILE_ROWS * PACK_LANES)
    return jnp.pad(flat, (0, pad)).reshape(-1, PACK_LANES)


def _unpack(packed, shapes):
    flat = packed.reshape(-1)
    out, off = [], 0
    for shp in shapes:
        n = math.prod(shp)
        out.append(flat[off:off + n].reshape(shp))
        off += n
    return out


def kernel(x, norm_g, w_in, conv_a_w, conv_r_w, conv_r_b, lru_wa, lru_ba, lru_wx, lru_bx, lru_lambda, gmlp_norm_g, gmlp_ws, gmlp_bs, w_out, final_g, loss_target, m_norm_g, m_w_in, m_conv_a_w, m_conv_r_w, m_conv_r_b, m_lru_wa, m_lru_ba, m_lru_wx, m_lru_bx, m_lru_lambda, m_gmlp_norm_g, m_gmlp_ws, m_gmlp_bs, m_w_out, m_final_g, v_norm_g, v_w_in, v_conv_a_w, v_conv_r_w, v_conv_r_b, v_lru_wa, v_lru_ba, v_lru_wx, v_lru_bx, v_lru_lambda, v_gmlp_norm_g, v_gmlp_ws, v_gmlp_bs, v_w_out, v_final_g):
    names = ("norm_g", "w_in", "conv_a_w", "conv_r_w", "conv_r_b", "lru_wa", "lru_ba", "lru_wx", "lru_bx",
             "lru_lambda", "gmlp_norm_g", "gmlp_ws", "gmlp_bs", "w_out", "final_g")
    w = dict(zip(names, (norm_g, w_in, conv_a_w, conv_r_w, conv_r_b, lru_wa, lru_ba, lru_wx, lru_bx, lru_lambda,
                         gmlp_norm_g, gmlp_ws, gmlp_bs, w_out, final_g)))
    m = dict(zip(names, (m_norm_g, m_w_in, m_conv_a_w, m_conv_r_w, m_conv_r_b, m_lru_wa, m_lru_ba, m_lru_wx, m_lru_bx,
                         m_lru_lambda, m_gmlp_norm_g, m_gmlp_ws, m_gmlp_bs, m_w_out, m_final_g)))
    v = dict(zip(names, (v_norm_g, v_w_in, v_conv_a_w, v_conv_r_w, v_conv_r_b, v_lru_wa, v_lru_ba, v_lru_wx, v_lru_bx,
                         v_lru_lambda, v_gmlp_norm_g, v_gmlp_ws, v_gmlp_bs, v_w_out, v_final_g)))
    depth, _, in_cols = w_in.shape
    out_rows = w_out.shape[1]
    conv_ch = conv_a_w.shape[2]
    chip = 2 * lax.axis_index("x") + lax.axis_index("y")

    taps = conv_a_w.shape[1] + conv_r_w.shape[1]
    w_in_t, m_w_in_t, v_w_in_t = (jnp.swapaxes(a, 1, 2) for a in (w_in, m_w_in, v_w_in))
    w_in_h, w_out_h = w_in_t.astype(BF16), w_out.astype(BF16)
    conv_own = jnp.concatenate([conv_a_w, conv_r_w], axis=1).reshape(depth * taps, conv_ch)
    gathers, token = [], None
    for l in range(depth):
        srcs = [w_in_h[l], w_out_h[l]] + ([conv_own] if l == 0 else [])
        gathers.append(exchange_start(GATHER, srcs, token, f"gather_start_{l}"))
        token = gathers[-1][-1]
    p = dict(w)

    def with_own(land, own):
        return lax.dynamic_update_slice(land, own[None], (chip,) + (0,) * own.ndim)

    def layer_weights(l, x_l):
        lands = exchange_wait(GATHER, gathers[l], x_l, f"gather_wait_{l}")
        w_in_l = with_own(lands[0], w_in_h[l]).reshape(D_IN, D_MODEL)
        w_out_l = with_own(lands[1], w_out_h[l]).reshape(D_MIX, D_MODEL)
        gain = norm_g[l][None, :]
        if l == 0:
            conv = with_own(lands[2], conv_own).reshape(N_CHIPS, depth, taps, conv_ch)
            conv = conv.transpose(1, 2, 0, 3).reshape(depth, taps, GROUP_W)
            p["conv_a_w"] = conv[:, :conv_a_w.shape[1]]
            p["conv_r_w"] = conv[:, conv_a_w.shape[1]:]
            p["prepared"] = prepare_small_weights(p)
            gain = gain + token[0, 0]
        return gain, w_in_l, w_out_l, p["prepared"]

    scatters, owns = [None] * depth, [None] * depth

    def projections_done(l, g_w_in_by_chip, g_w_in_own, g_w_out):
        go = g_w_out.reshape(N_CHIPS, out_rows, D_MODEL)
        owns[l] = (g_w_in_own, lax.dynamic_index_in_dim(go, chip, axis=0, keepdims=False))
        scatters[l] = exchange_start(SCATTER, [g_w_in_by_chip, go.astype(BF16)], None, f"scatter_start_{l}")
        return scatters[l][-1][0, 0]

    loss8, dx, grads = local_step(x[0], loss_target[0], final_g, depth, chip.reshape(1), layer_weights,
                                  projections_done)
    loss = lax.psum(loss8[0, 0], ("x", "y", "c"))

    parts = [exchange_wait(SCATTER, scatters[l], dx, f"scatter_wait_{l}") for l in range(depth)]
    r1 = jnp.concatenate([parts[l][0] for l in range(depth)], axis=1)
    r2 = jnp.concatenate([parts[l][1] for l in range(depth)], axis=1)
    own1 = jnp.concatenate([owns[l][0] for l in range(depth)], axis=0)
    own2 = jnp.concatenate([owns[l][1] for l in range(depth)], axis=0)
    p1 = sum_partials(own1, r1)
    p2 = sum_partials(own2, r2)
    q1, q2 = sibling_exchange(p1, p2)
    res = {}
    res["w_in"] = [jnp.swapaxes(t, 1, 2) for t in
                   adamw(w_in_t, p1.reshape(w_in_t.shape), q1.reshape(w_in_t.shape), m_w_in_t, v_w_in_t)]
    res["w_out"] = adamw(w_out, p2.reshape(w_out.shape), q2.reshape(w_out.shape), m_w_out, v_w_out)

    small = REPLICATED + CHIP_SHARDED_SMALL
    packed = _pack([grads[k] for k in small])
    rows = packed.shape[0]
    allp = gather_all(packed).reshape(N_DEV, rows, PACK_LANES)
    total = sum_partials(None, allp)
    gs = dict(zip(small, _unpack(total, [grads[k].shape for k in small])))
    for k in CHIP_SHARDED_SMALL:
        gs[k] = lax.dynamic_slice_in_dim(gs[k], chip * conv_ch, conv_ch, axis=2)
    as2d = lambda a: a[None] if a.ndim == 1 else a
    outs = adamw_small(*[[as2d(d[k]) for k in small] for d in (w, gs, m, v)])
    for k, (delta, m2, v2) in zip(small, outs):
        res[k] = [t.reshape(w[k].shape) for t in (gs[k], delta, m2, v2)]

    return (loss, dx[None], *[res[k][0] for k in names], *[res[k][1] for k in names],
            *[res[k][2] for k in names], *[res[k][3] for k in names])
```

```python
import functools
import math

import jax
import jax.numpy as jnp
import numpy as np
from jax import lax
from jax.experimental import pallas as pl
from jax.experimental.pallas import tpu as pltpu

F32 = jnp.float32
BF16 = jnp.bfloat16

D_MODEL = 1024
GROUP_W = 256
N_HEADS = 4
HEAD_DIM = 64
N_CHUNKS = 13
D_IN = N_CHUNKS * GROUP_W
D_MIX = 4 * GROUP_W
NORM_EPS = 1e-6
RG_C = 8.0
GMLP_CHUNK = 128
ATTN_BLOCK = 128
PATTERN_DILS = (1, 4, 16)
N_PATTERNS = len(PATTERN_DILS)
ALIBI_SLOPES = tuple(2.0 ** (-8.0 * (h + 1) / N_HEADS) for h in range(N_HEADS))
ATTN_SCALE = 1.0 / math.sqrt(HEAD_DIM)
NEG_BIG = -1e30

ADAM_LR = 0.001
ADAM_B1 = 0.9
ADAM_B2 = 0.999
ADAM_EPS = 1e-08
ADAM_WD = 0.01
ADAM_STEP = 10

C_AX, C_AB, C_AC, C_AG, C_RX, C_RG, C_CU, C_CV, C_CG, C_DQ, C_DK, C_DV, C_DG = range(13)

SUBLANES = 8
LANES = 128
VMEM_LIMIT = 56 * 1024 * 1024
TILE_IN = 512
TILE_MIX = 512
TILE_DW = 512
ATTN_QB = 4
GELU_K0 = math.sqrt(2.0 / math.pi)
GELU_K1 = 0.044715


def _params(sem):
    return pltpu.CompilerParams(dimension_semantics=sem, vmem_limit_bytes=VMEM_LIMIT)


def _silu(x):
    return x * jax.nn.sigmoid(x)


def _dsilu(x):
    s = jax.nn.sigmoid(x)
    return s * (1.0 + x * (1.0 - s))


def _gelu(x):
    return 0.5 * x * (1.0 + jnp.tanh(GELU_K0 * (x + GELU_K1 * x * x * x)))


def _gelu_and_grad(x):
    t = jnp.tanh(GELU_K0 * (x + GELU_K1 * x * x * x))
    g = 0.5 * x * (1.0 + t)
    dg = 0.5 * (1.0 + t) + 0.5 * x * (1.0 - t * t) * GELU_K0 * (1.0 + 3.0 * GELU_K1 * x * x)
    return g, dg


def _neg_expm1(x):
    series = x * (1.0 + x * (0.5 + x * (1.0 / 6.0 + x * (1.0 / 24.0 + x * (1.0 / 120.0)))))
    return -jnp.where(x > -0.05, series, jnp.exp(x) - 1.0)


def _shift_down(v, halo, k):
    r = pltpu.roll(v, k, 0)
    rh = pltpu.roll(halo, k, 0)
    row = lax.broadcasted_iota(jnp.int32, halo.shape, 0)
    top = jnp.where(row < k, rh, r[:SUBLANES])
    return jnp.concatenate([top, r[SUBLANES:]], axis=0)


def _shift_up(v, halo, k):
    t = v.shape[0]
    r = pltpu.roll(v, t - k, 0)
    rh = pltpu.roll(halo, SUBLANES - k, 0)
    row = lax.broadcasted_iota(jnp.int32, halo.shape, 0)
    bot = jnp.where(row >= SUBLANES - k, rh, r[t - SUBLANES:])
    return jnp.concatenate([r[:t - SUBLANES], bot], axis=0)


def _scan_causal(a, b, h_in):
    t = a.shape[0]
    row8 = lax.broadcasted_iota(jnp.int32, a.shape, 0) % SUBLANES
    d = 1
    while d < SUBLANES:
        m = row8 >= d
        a_s = jnp.where(m, pltpu.roll(a, d, 0), 1.0)
        b_s = jnp.where(m, pltpu.roll(b, d, 0), 0.0)
        b = a * b_s + b
        a = a * a_s
        d *= 2
    out, carry = [], h_in
    for g in range(t // SUBLANES):
        sl = slice(g * SUBLANES, (g + 1) * SUBLANES)
        hg = b[sl] + a[sl] * carry
        out.append(hg)
        carry = hg[SUBLANES - 1:SUBLANES]
    return jnp.concatenate(out, axis=0)


def _scan_anticausal(a, b, g_in):
    t = a.shape[0]
    row8 = lax.broadcasted_iota(jnp.int32, a.shape, 0) % SUBLANES
    d = 1
    while d < SUBLANES:
        m = row8 < SUBLANES - d
        a_s = jnp.where(m, pltpu.roll(a, t - d, 0), 1.0)
        b_s = jnp.where(m, pltpu.roll(b, t - d, 0), 0.0)
        b = a * b_s + b
        a = a * a_s
        d *= 2
    out, carry = [], g_in
    for g in reversed(range(t // SUBLANES)):
        sl = slice(g * SUBLANES, (g + 1) * SUBLANES)
        gg = b[sl] + a[sl] * carry
        out.append(gg)
        carry = gg[0:1]
    return jnp.concatenate(out[::-1], axis=0)


def _head_of_lane(shape):
    return lax.broadcasted_iota(jnp.int32, shape, len(shape) - 1) // HEAD_DIM


def _put_row(acc_shape, k, row_vec):
    row = lax.broadcasted_iota(jnp.int32, acc_shape, 0)
    return jnp.where(row == k, jnp.broadcast_to(row_vec, acc_shape), 0.0)


def _dot(a, b):
    return jnp.dot(a, b, preferred_element_type=F32)


def _dot_nt(a, b):
    return lax.dot_general(a, b, (((1,), (1,)), ((), ())), preferred_element_type=F32)


def _dot_tn(a, b):
    return lax.dot_general(a, b, (((0,), (0,)), ((), ())), preferred_element_type=F32)


def _deinterleave_store(val, stage, outs):
    t, c = val.shape
    for hh in range(c // LANES):
        stage[hh][...] = val[:, hh * LANES:(hh + 1) * LANES].astype(F32)
    for dil, ref in outs:
        for r in range(dil):
            for hh in range(c // LANES):
                ref[r, :, hh * LANES:(hh + 1) * LANES] = stage[hh][pl.ds(r, t // dil, stride=dil), :].astype(ref.dtype)


def _interleave_load(ref, dil, stage):
    _, n, c = ref.shape
    for r in range(dil):
        for hh in range(c // LANES):
            stage[hh][pl.ds(r, n, stride=dil), :] = ref[r, :, hh * LANES:(hh + 1) * LANES].astype(F32)
    return jnp.concatenate([stage[hh][...] for hh in range(c // LANES)], axis=1)


def _stage_scratch(tile, cols, copies):
    return [pltpu.VMEM((tile, LANES), F32)] * (copies * (cols // LANES))


def _by_residue(s, dil, cols, dtype):
    return jax.ShapeDtypeStruct((dil, s // dil, cols), dtype)


def _residue_block(dil, tile, cols):
    return pl.BlockSpec((dil, tile // dil, cols), lambda i: (0, i, 0))


def in_fwd(x, g, w):
    s = x.shape[0]
    qkv_w = 3 * GROUP_W

    def body(x_ref, g_ref, w_ref, z_ref, h_ref, qkv1_ref, qkv4_ref, qkv16_ref, *stage):
        xv = x_ref[...]
        rs = lax.rsqrt(jnp.mean(xv * xv, axis=-1, keepdims=True) + NORM_EPS)
        h = (xv * rs * g_ref[...]).astype(BF16)
        h_ref[...] = h
        z = _dot_nt(h, w_ref[...])
        z_ref[...] = z
        qkv = z[:, C_DQ * GROUP_W:(C_DV + 1) * GROUP_W]
        qkv1_ref[...] = qkv.astype(BF16)
        _deinterleave_store(qkv, stage, ((PATTERN_DILS[1], qkv4_ref), (PATTERN_DILS[2], qkv16_ref)))

    return pl.pallas_call(
        body, name="in_fwd", grid=(s // TILE_IN,),
        in_specs=[pl.BlockSpec((TILE_IN, D_MODEL), lambda i: (i, 0)),
                  pl.BlockSpec((1, D_MODEL), lambda i: (0, 0)),
                  pl.BlockSpec((D_IN, D_MODEL), lambda i: (0, 0))],
        out_specs=[pl.BlockSpec((TILE_IN, D_IN), lambda i: (i, 0)),
                   pl.BlockSpec((TILE_IN, D_MODEL), lambda i: (i, 0)),
                   pl.BlockSpec((TILE_IN, qkv_w), lambda i: (i, 0)),
                   _residue_block(PATTERN_DILS[1], TILE_IN, qkv_w),
                   _residue_block(PATTERN_DILS[2], TILE_IN, qkv_w)],
        out_shape=[jax.ShapeDtypeStruct((s, D_IN), F32), jax.ShapeDtypeStruct((s, D_MODEL), BF16),
                   jax.ShapeDtypeStruct((s, qkv_w), BF16),
                   _by_residue(s, PATTERN_DILS[1], qkv_w, BF16), _by_residue(s, PATTERN_DILS[2], qkv_w, BF16)],
        scratch_shapes=_stage_scratch(TILE_IN, qkv_w, 1),
        compiler_params=_params(("parallel",)),
    )(x, g, w)


def out_fwd(y, w, x):
    s = x.shape[0]

    def body(y_ref, w_ref, x_ref, o_ref):
        o_ref[...] = x_ref[...] + _dot(y_ref[...], w_ref[...])

    return pl.pallas_call(
        body, name="out_fwd", grid=(s // TILE_IN,),
        in_specs=[pl.BlockSpec((TILE_IN, D_MIX), lambda i: (i, 0)),
                  pl.BlockSpec((D_MIX, D_MODEL), lambda i: (0, 0)),
                  pl.BlockSpec((TILE_IN, D_MODEL), lambda i: (i, 0))],
        out_specs=pl.BlockSpec((TILE_IN, D_MODEL), lambda i: (i, 0)),
        out_shape=jax.ShapeDtypeStruct((s, D_MODEL), F32),
        compiler_params=_params(("parallel",)),
    )(y, w, x)


def out_bwd(dx, w, z, o):
    s = dx.shape[0]
    abc = 3 * GROUP_W

    def body(dx_ref, w_ref, dg_ref, o_ref, dy_ref, ddg_ref, do1_ref, do4_ref, do16_ref, dl1_ref, dl4_ref, dl16_ref,
             *stage):
        stage_a, stage_b = stage[:2], stage[2:]
        dy = _dot_nt(dx_ref[...].astype(BF16), w_ref[...])
        dy_ref[...] = dy[:, :abc]
        dyd = dy[:, abc:]
        head = _head_of_lane((TILE_IN, GROUP_W))
        dg = dg_ref[...]
        o = o_ref[...]
        do = dyd * _silu(dg)
        ddg_ref[...] = dyd * o * _dsilu(dg)
        prod = do * o
        dl = jnp.zeros_like(prod)
        for h in range(N_HEADS):
            sm = jnp.sum(jnp.where(head == h, prod, 0.0), axis=-1, keepdims=True)
            dl = jnp.where(head == h, sm, dl)
        do1_ref[...] = do.astype(BF16)
        dl1_ref[...] = dl
        _deinterleave_store(do, stage_a, ((PATTERN_DILS[1], do4_ref), (PATTERN_DILS[2], do16_ref)))
        _deinterleave_store(dl, stage_b, ((PATTERN_DILS[1], dl4_ref), (PATTERN_DILS[2], dl16_ref)))

    row = pl.BlockSpec((TILE_IN, GROUP_W), lambda i: (i, 0))
    r4 = _residue_block(PATTERN_DILS[1], TILE_IN, GROUP_W)
    r16 = _residue_block(PATTERN_DILS[2], TILE_IN, GROUP_W)
    return pl.pallas_call(
        body, name="out_bwd", grid=(s // TILE_IN,),
        in_specs=[pl.BlockSpec((TILE_IN, D_MODEL), lambda i: (i, 0)),
                  pl.BlockSpec((D_MIX, D_MODEL), lambda i: (0, 0)),
                  pl.BlockSpec((TILE_IN, GROUP_W), lambda i: (i, C_DG)), row],
        out_specs=[pl.BlockSpec((TILE_IN, abc), lambda i: (i, 0)), row, row, r4, r16, row, r4, r16],
        out_shape=[jax.ShapeDtypeStruct((s, abc), F32), jax.ShapeDtypeStruct((s, GROUP_W), F32),
                   jax.ShapeDtypeStruct((s, GROUP_W), BF16),
                   _by_residue(s, PATTERN_DILS[1], GROUP_W, BF16), _by_residue(s, PATTERN_DILS[2], GROUP_W, BF16),
                   jax.ShapeDtypeStruct((s, GROUP_W), F32),
                   _by_residue(s, PATTERN_DILS[1], GROUP_W, F32), _by_residue(s, PATTERN_DILS[2], GROUP_W, F32)],
        scratch_shapes=_stage_scratch(TILE_IN, GROUP_W, 2),
        compiler_params=_params(("parallel",)),
    )(dx, w, z, o)


def in_bwd(dz, w, x, g, dx_next):
    s = x.shape[0]

    def body(dz_ref, w_ref, x_ref, g_ref, dxn_ref, dx_ref, dg_ref):
        @pl.when(pl.program_id(0) == 0)
        def _():
            dg_ref[...] = jnp.zeros_like(dg_ref)

        dh = _dot(dz_ref[...], w_ref[...])
        xv = x_ref[...]
        rs = lax.rsqrt(jnp.mean(xv * xv, axis=-1, keepdims=True) + NORM_EPS)
        xh = xv * rs
        dg_ref[...] += _put_row(dg_ref.shape, 0, jnp.sum(dh * xh, axis=0, keepdims=True))
        dn = dh * g_ref[...]
        dx_ref[...] = dxn_ref[...] + rs * (dn - xh * jnp.mean(dn * xh, axis=-1, keepdims=True))

    return pl.pallas_call(
        body, name="in_bwd", grid=(s // TILE_IN,),
        in_specs=[pl.BlockSpec((TILE_IN, D_IN), lambda i: (i, 0)),
                  pl.BlockSpec((D_IN, D_MODEL), lambda i: (0, 0)),
                  pl.BlockSpec((TILE_IN, D_MODEL), lambda i: (i, 0)),
                  pl.BlockSpec((1, D_MODEL), lambda i: (0, 0)),
                  pl.BlockSpec((TILE_IN, D_MODEL), lambda i: (i, 0))],
        out_specs=[pl.BlockSpec((TILE_IN, D_MODEL), lambda i: (i, 0)),
                   pl.BlockSpec((SUBLANES, D_MODEL), lambda i: (0, 0))],
        out_shape=[jax.ShapeDtypeStruct((s, D_MODEL), F32), jax.ShapeDtypeStruct((SUBLANES, D_MODEL), F32)],
        compiler_params=_params(("arbitrary",)),
    )(dz, w, x, g, dx_next)


def matmul_tn(a, b, n_split):
    s, m = a.shape
    n = b.shape[1]
    tn = n // n_split

    def body(a_ref, b_ref, o_ref):
        @pl.when(pl.program_id(1) == 0)
        def _():
            o_ref[...] = jnp.zeros_like(o_ref)

        o_ref[...] += _dot_tn(a_ref[...], b_ref[...].astype(BF16))

    return pl.pallas_call(
        body, name="matmul_tn", grid=(n_split, s // TILE_DW),
        in_specs=[pl.BlockSpec((TILE_DW, m), lambda j, k: (k, 0)),
                  pl.BlockSpec((TILE_DW, tn), lambda j, k: (k, j))],
        out_specs=pl.BlockSpec((m, tn), lambda j, k: (0, j)),
        out_shape=jax.ShapeDtypeStruct((m, n), F32),
        compiler_params=_params(("parallel", "arbitrary")),
    )(a, b)


def grad_w_in(h, dz, chip):
    s = h.shape[0]
    rows = D_IN // N_CHIPS

    def body(chip_ref, h_ref, dz_ref, staged_ref, own_ref, acc):
        k = pl.program_id(0)

        @pl.when(k == 0)
        def _():
            acc[...] = jnp.zeros_like(acc)

        acc[...] += _dot_tn(dz_ref[...], h_ref[...])

        @pl.when(k == s // TILE_DW - 1)
        def _():
            for j in range(N_CHIPS):
                part = acc[j * rows:(j + 1) * rows, :]
                staged_ref[j] = part.astype(BF16)

                @pl.when(chip_ref[0] == j)
                def _():
                    own_ref[...] = part

    return pl.pallas_call(
        body, name="grad_w_in",
        grid_spec=pltpu.PrefetchScalarGridSpec(
            num_scalar_prefetch=1, grid=(s // TILE_DW,),
            in_specs=[pl.BlockSpec((TILE_DW, D_MODEL), lambda k, c: (k, 0)),
                      pl.BlockSpec((TILE_DW, D_IN), lambda k, c: (k, 0))],
            out_specs=[pl.BlockSpec((N_CHIPS, rows, D_MODEL), lambda k, c: (0, 0, 0)),
                       pl.BlockSpec((rows, D_MODEL), lambda k, c: (0, 0))],
            scratch_shapes=[pltpu.VMEM((D_IN, D_MODEL), F32)]),
        out_shape=[jax.ShapeDtypeStruct((N_CHIPS, rows, D_MODEL), BF16), jax.ShapeDtypeStruct((rows, D_MODEL), F32)],
        compiler_params=_params(("arbitrary",)),
    )(chip, h, dz)


def loss_head(x, g, tgt):
    s = x.shape[0]

    def body(x_ref, g_ref, t_ref, l_ref, dx_ref, dg_ref):
        @pl.when(pl.program_id(0) == 0)
        def _():
            l_ref[...] = jnp.zeros_like(l_ref)
            dg_ref[...] = jnp.zeros_like(dg_ref)

        xv = x_ref[...]
        gv = g_ref[...]
        rs = lax.rsqrt(jnp.mean(xv * xv, axis=-1, keepdims=True) + NORM_EPS)
        xh = xv * rs
        e = xh * gv - t_ref[...]
        part = 0.5 * jnp.sum(jnp.mean(e * e, axis=-1, keepdims=True), axis=0, keepdims=True)
        l_ref[...] += jnp.broadcast_to(part, l_ref.shape)
        dy = e * (1.0 / D_MODEL)
        dg_ref[...] += _put_row(dg_ref.shape, 0, jnp.sum(dy * xh, axis=0, keepdims=True))
        dn = dy * gv
        dx_ref[...] = rs * (dn - xh * jnp.mean(dn * xh, axis=-1, keepdims=True))

    return pl.pallas_call(
        body, name="loss_head", grid=(s // TILE_IN,),
        in_specs=[pl.BlockSpec((TILE_IN, D_MODEL), lambda i: (i, 0)),
                  pl.BlockSpec((1, D_MODEL), lambda i: (0, 0)),
                  pl.BlockSpec((TILE_IN, D_MODEL), lambda i: (i, 0))],
        out_specs=[pl.BlockSpec((SUBLANES, 128), lambda i: (0, 0)),
                   pl.BlockSpec((TILE_IN, D_MODEL), lambda i: (i, 0)),
                   pl.BlockSpec((SUBLANES, D_MODEL), lambda i: (0, 0))],
        out_shape=[jax.ShapeDtypeStruct((SUBLANES, 128), F32), jax.ShapeDtypeStruct((s, D_MODEL), F32),
                   jax.ShapeDtypeStruct((SUBLANES, D_MODEL), F32)],
        compiler_params=_params(("arbitrary",)),
    )(x, g, tgt)


def _attn_bias(dil):
    qi = np.arange(ATTN_BLOCK)[:, None]
    ki = np.arange(2 * ATTN_BLOCK)[None, :]
    delta = qi + ATTN_BLOCK - ki
    band = (delta >= 0) & (delta <= ATTN_BLOCK)
    out = np.empty((2, N_HEADS, ATTN_BLOCK, 2 * ATTN_BLOCK), np.float32)
    for f in range(2):
        ok = band & ((ki >= ATTN_BLOCK) | (f == 0))
        for h in range(N_HEADS):
            out[f, h] = np.where(ok, -ALIBI_SLOPES[h] * dil * delta, NEG_BIG)
    return jnp.asarray(out.reshape(2, N_HEADS * ATTN_BLOCK, 2 * ATTN_BLOCK))


def _stack_heads(a, head):
    return jnp.concatenate([jnp.where(head == h, a, jnp.zeros_like(a)) for h in range(N_HEADS)], axis=0)


def _unstack_heads(a, head):
    out = a[:ATTN_BLOCK]
    for h in range(1, N_HEADS):
        out = jnp.where(head == h, a[h * ATTN_BLOCK:(h + 1) * ATTN_BLOCK], out)
    return out


def _head_column(a):
    return jnp.concatenate([a[:, h * HEAD_DIM:h * HEAD_DIM + 1] for h in range(N_HEADS)], axis=0)


def _attn_specs(n_blocks):
    rows = ATTN_QB * ATTN_BLOCK
    cur = lambda c: pl.BlockSpec((rows, GROUP_W), lambda n, c=c: (n, c))
    prev = lambda c: pl.BlockSpec((ATTN_BLOCK, GROUP_W), lambda n, c=c: (jnp.maximum(n * ATTN_QB - 1, 0), c))
    nxt = lambda c: pl.BlockSpec((ATTN_BLOCK, GROUP_W),
                                 lambda n, c=c: (jnp.minimum(n * ATTN_QB + ATTN_QB, n_blocks - 1), c))
    return cur, prev, nxt


def _keys(kp_ref, k_ref, j):
    prev = kp_ref[...] if j == 0 else k_ref[(j - 1) * ATTN_BLOCK:j * ATTN_BLOCK, :]
    return jnp.concatenate([prev, k_ref[j * ATTN_BLOCK:(j + 1) * ATTN_BLOCK, :]], axis=0)


def attn_fwd(qkv, dil):
    s = qkv.shape[0]
    n_blocks = s // ATTN_BLOCK
    bps = n_blocks // dil
    rows = ATTN_QB * ATTN_BLOCK

    def body(q_ref, kp_ref, k_ref, vp_ref, v_ref, bias_ref, o_ref, lse_ref):
        n = pl.program_id(0)
        head = _head_of_lane((ATTN_BLOCK, GROUP_W))
        for j in range(ATTN_QB):
            sl = slice(j * ATTN_BLOCK, (j + 1) * ATTN_BLOCK)
            first = (((n * ATTN_QB + j) % bps) == 0).astype(jnp.int32)
            qs = _stack_heads(q_ref[sl, :], head)
            sc = _dot_nt(qs, _keys(kp_ref, k_ref, j)) * ATTN_SCALE + bias_ref[first]
            m = jnp.max(sc, axis=-1, keepdims=True)
            pr = jnp.exp(sc - m)
            l = jnp.sum(pr, axis=-1, keepdims=True)
            oh = _dot(pr.astype(BF16), _keys(vp_ref, v_ref, j)) / l
            o_ref[sl, :] = _unstack_heads(oh, head)
            lse_ref[sl, :] = _unstack_heads(jnp.broadcast_to(m + jnp.log(l), oh.shape), head)

    cur, prev, _ = _attn_specs(n_blocks)
    bias = _attn_bias(dil)
    out = jax.ShapeDtypeStruct((s, GROUP_W), F32)
    return pl.pallas_call(
        body, name=f"attn_fwd_d{dil}", grid=(n_blocks // ATTN_QB,),
        in_specs=[cur(0), prev(1), cur(1), prev(2), cur(2), pl.BlockSpec(bias.shape, lambda n: (0, 0, 0))],
        out_specs=[cur(0), cur(0)],
        out_shape=[out, out],
        compiler_params=_params(("parallel",)),
    )(qkv, qkv, qkv, qkv, qkv, bias)


def attn_bwd(qkv, do, lse, dlt, dil):
    s = qkv.shape[0]
    n_blocks = s // ATTN_BLOCK
    bps = n_blocks // dil
    rows = ATTN_QB * ATTN_BLOCK

    def body(q_ref, qn_ref, kp_ref, k_ref, vp_ref, v_ref, do_ref, don_ref, lse_ref, lsen_ref, dl_ref, dln_ref,
             bias_ref, out_ref, dk_acc, dv_acc):
        n = pl.program_id(0)
        head = _head_of_lane((ATTN_BLOCK, GROUP_W))
        dk_acc[...] = jnp.zeros_like(dk_acc)
        dv_acc[...] = jnp.zeros_like(dv_acc)

        def pair(qj, doj, lsej, dlj, kk, vv, bias, keep):
            qs = _stack_heads(qj, head)
            dos = _stack_heads(doj, head)
            sc = _dot_nt(qs, kk) * ATTN_SCALE + bias
            if keep is None:
                pr = jnp.exp(sc - _head_column(lsej))
            else:
                pr = jnp.exp(jnp.minimum(sc - _head_column(lsej), 0.0)) * keep
            dp = _dot_nt(dos, vv)
            ds = (pr * (dp - _head_column(dlj)) * ATTN_SCALE).astype(BF16)
            return ds, _dot_tn(ds, qs), _dot_tn(pr.astype(BF16), dos)

        for j in range(ATTN_QB):
            sl = slice(j * ATTN_BLOCK, (j + 1) * ATTN_BLOCK)
            first = (((n * ATTN_QB + j) % bps) == 0).astype(jnp.int32)
            kk = _keys(kp_ref, k_ref, j)
            ds, dks, dvs = pair(q_ref[sl, :], do_ref[sl, :], lse_ref[sl, :], dl_ref[sl, :],
                                kk, _keys(vp_ref, v_ref, j), bias_ref[first], None)
            out_ref[sl, 0:GROUP_W] = _unstack_heads(_dot(ds, kk), head)
            acc = slice(j * ATTN_BLOCK, (j + 2) * ATTN_BLOCK)
            dk_acc[acc, :] += dks
            dv_acc[acc, :] += dvs

        nxt = n * ATTN_QB + ATTN_QB
        valid = ((nxt < n_blocks) & ((nxt % bps) != 0)).astype(F32)
        last = slice((ATTN_QB - 1) * ATTN_BLOCK, ATTN_QB * ATTN_BLOCK)
        _, dks, dvs = pair(qn_ref[...], don_ref[...], lsen_ref[...], dln_ref[...], k_ref[last, :], v_ref[last, :],
                           bias_ref[0][:, :ATTN_BLOCK], valid)
        acc = slice(ATTN_QB * ATTN_BLOCK, (ATTN_QB + 1) * ATTN_BLOCK)
        dk_acc[acc, :] += dks
        dv_acc[acc, :] += dvs
        out_ref[:, GROUP_W:2 * GROUP_W] = dk_acc[ATTN_BLOCK:, :]
        out_ref[:, 2 * GROUP_W:3 * GROUP_W] = dv_acc[ATTN_BLOCK:, :]

    cur, prev, nxt = _attn_specs(n_blocks)
    bias = _attn_bias(dil)
    return pl.pallas_call(
        body, name=f"attn_bwd_d{dil}", grid=(n_blocks // ATTN_QB,),
        in_specs=[cur(0), nxt(0), prev(1), cur(1), prev(2), cur(2), cur(0), nxt(0), cur(0), nxt(0), cur(0), nxt(0),
                  pl.BlockSpec(bias.shape, lambda n: (0, 0, 0))],
        out_specs=pl.BlockSpec((rows, 3 * GROUP_W), lambda n: (n, 0)),
        out_shape=jax.ShapeDtypeStruct((s, 3 * GROUP_W), F32),
        scratch_shapes=[pltpu.VMEM(((ATTN_QB + 1) * ATTN_BLOCK, GROUP_W), F32),
                        pltpu.VMEM(((ATTN_QB + 1) * ATTN_BLOCK, GROUP_W), F32)],
        compiler_params=_params(("parallel",)),
    )(qkv, qkv, qkv, qkv, qkv, qkv, do, do, lse, lse, dlt, dlt, bias)


def _zcol(c):
    return pl.BlockSpec((TILE_MIX, GROUP_W), lambda i, c=c: (i, c))


def _zhalo(c):
    per = TILE_MIX // SUBLANES
    return pl.BlockSpec((SUBLANES, GROUP_W), lambda i, c=c: (jnp.maximum(i * per - 1, 0), c))


def _full(shape):
    return pl.BlockSpec(shape, lambda i: tuple(0 for _ in shape))


def _of_layer(a, l):
    rest = a.shape[1:]
    return pl.BlockSpec((None,) + rest, lambda i: (l,) + tuple(0 for _ in rest))


def _softplus_neg(lam):
    nl = -lam
    return jnp.maximum(nl, 0.0) + jnp.log1p(jnp.exp(-jnp.abs(nl)))


def _lru_gates(xb, wa_ref, wx_ref, ba, bx, lam):
    xbb = xb.astype(BF16)
    r = jax.nn.sigmoid(_dot(xbb, wa_ref[...]) + ba)
    ig = jax.nn.sigmoid(_dot(xbb, wx_ref[...]) + bx)
    log_a = (-RG_C * r) * _softplus_neg(lam)
    a = jnp.exp(log_a)
    mult = jnp.sqrt(_neg_expm1(2.0 * log_a))
    return r, ig, a, mult


LRU_SAVED = 5


def _gmlp_spatial(ws_ref, vvb, head):
    outs = []
    for j in range(vvb.shape[0] // GMLP_CHUNK):
        blk = vvb[j * GMLP_CHUNK:(j + 1) * GMLP_CHUNK, :]
        acc = jnp.zeros((GMLP_CHUNK, GROUP_W), F32)
        for h in range(N_HEADS):
            acc = jnp.where(head[:GMLP_CHUNK] == h, _dot(ws_ref[h], blk), acc)
        outs.append(acc)
    return jnp.concatenate(outs, axis=0)


def mix_fwd(z, attn, wts, l):
    s = z.shape[0]
    d4, d16 = PATTERN_DILS[1], PATTERN_DILS[2]

    def body(ax_ref, ab_ref, ac_ref, ag_ref, rx_ref, rg_ref, cu_ref, cv_ref, cg_ref, dg_ref,
             axh_ref, ach_ref, rxh_ref, o1_ref, l1_ref, o4_ref, l4_ref, o16_ref, l16_ref,
             caw_ref, crw_ref, crb_ref, wa_ref, wx_ref, ba_ref, bx_ref, lam_ref, gng_ref, ws_ref, bs_ref,
             y_ref, hl_ref, o_ref, lse_ref, lse4_ref, lse16_ref, lru_ref, carry, *stage):
        st_a, st_b, st_c, st_d, st_e = (stage[2 * k:2 * k + 2] for k in range(5))
        i = pl.program_id(0)

        @pl.when(i == 0)
        def _():
            carry[...] = jnp.zeros_like(carry)

        nz = (i > 0).astype(F32)
        head = _head_of_lane((TILE_MIX, GROUP_W))

        pa = ac_ref[...] * ax_ref[...]
        pah = ach_ref[...] * axh_ref[...] * nz
        cv = caw_ref[2:3, :] * pa + caw_ref[1:2, :] * _shift_down(pa, pah, 1) + caw_ref[0:1, :] * _shift_down(pa, pah, 2)
        y_ref[:, 0:GROUP_W] = (ab_ref[...] * cv * _silu(ag_ref[...])).astype(BF16)

        rx = rx_ref[...]
        rxh = rxh_ref[...] * nz
        xb = (crw_ref[3:4, :] * rx + crw_ref[2:3, :] * _shift_down(rx, rxh, 1) + crw_ref[1:2, :] * _shift_down(rx, rxh, 2)
              + crw_ref[0:1, :] * _shift_down(rx, rxh, 3) + crb_ref[...])
        r, ig, a, mult = _lru_gates(xb, wa_ref, wx_ref, ba_ref[...], bx_ref[...], lam_ref[...])
        for k, val in enumerate((xb, r, ig, a, mult)):
            lru_ref[:, k * GROUP_W:(k + 1) * GROUP_W] = val
        hl = _scan_causal(a, mult * (ig * xb), carry[SUBLANES - 1:SUBLANES, :])
        hl_ref[...] = hl
        carry[...] = hl[TILE_MIX - SUBLANES:, :]
        y_ref[:, GROUP_W:2 * GROUP_W] = (hl * _silu(rg_ref[...])).astype(BF16)

        u = _gelu(cu_ref[...])
        gv = _gelu(cv_ref[...])
        rs = lax.rsqrt(jnp.mean(gv * gv, axis=-1, keepdims=True) + NORM_EPS)
        vvb = (gv * rs * gng_ref[...]).astype(BF16)
        sp = _gmlp_spatial(ws_ref, vvb, head) + jnp.concatenate([bs_ref[...]] * (TILE_MIX // GMLP_CHUNK), axis=0)
        y_ref[:, 2 * GROUP_W:3 * GROUP_W] = (u * sp * _silu(cg_ref[...])).astype(BF16)

        ops = (o1_ref[...], _interleave_load(o4_ref, d4, st_a), _interleave_load(o16_ref, d16, st_b))
        lps = (l1_ref[...], _interleave_load(l4_ref, d4, st_c), _interleave_load(l16_ref, d16, st_d))
        m = jnp.maximum(jnp.maximum(lps[0], lps[1]), lps[2])
        zsum = jnp.zeros_like(m)
        o = jnp.zeros_like(m)
        for op, lp in zip(ops, lps):
            w = jnp.exp(lp - m)
            zsum = zsum + w
            o = o + w * op
        o = o / zsum
        lse = m + jnp.log(zsum)
        o_ref[...] = o
        lse_ref[...] = lse
        _deinterleave_store(lse, st_e, ((d4, lse4_ref), (d16, lse16_ref)))
        y_ref[:, 3 * GROUP_W:4 * GROUP_W] = (o * _silu(dg_ref[...])).astype(BF16)

    row = pl.BlockSpec((TILE_MIX, GROUP_W), lambda i: (i, 0))
    r4 = _residue_block(d4, TILE_MIX, GROUP_W)
    r16 = _residue_block(d16, TILE_MIX, GROUP_W)
    names = ("caw", "crw", "crb", "wa", "wx", "ba", "bx", "lam", "gng", "ws", "bs")
    in_specs = ([_zcol(c) for c in (C_AX, C_AB, C_AC, C_AG, C_RX, C_RG, C_CU, C_CV, C_CG, C_DG)]
                + [_zhalo(C_AX), _zhalo(C_AC), _zhalo(C_RX), row, row, r4, r4, r16, r16]
                + [_of_layer(wts[k], l) for k in names])
    return pl.pallas_call(
        body, name="mix_fwd", grid=(s // TILE_MIX,),
        in_specs=in_specs,
        out_specs=[pl.BlockSpec((TILE_MIX, D_MIX), lambda i: (i, 0)), row, row, row, r4, r16,
                   pl.BlockSpec((TILE_MIX, LRU_SAVED * GROUP_W), lambda i: (i, 0))],
        out_shape=([jax.ShapeDtypeStruct((s, D_MIX), BF16)] + [jax.ShapeDtypeStruct((s, GROUP_W), F32)] * 3
                   + [_by_residue(s, d4, GROUP_W, F32), _by_residue(s, d16, GROUP_W, F32),
                      jax.ShapeDtypeStruct((s, LRU_SAVED * GROUP_W), F32)]),
        scratch_shapes=[pltpu.VMEM((SUBLANES, GROUP_W), F32)] + _stage_scratch(TILE_MIX, GROUP_W, 5),
        compiler_params=_params(("arbitrary",)),
    )(*([z] * 13), *[a for pair in attn for a in pair], *[wts[k] for k in names])


def mix_bwd(dy, z, hl, lru, dqkv, ddg, wts, l):
    s = z.shape[0]
    d4, d16 = PATTERN_DILS[1], PATTERN_DILS[2]
    n_tiles = s // TILE_MIX

    def body(dya_ref, dyb_ref, dyc_ref, ax_ref, ab_ref, ac_ref, ag_ref, rx_ref, rg_ref, cu_ref, cv_ref, cg_ref,
             axh_ref, ach_ref, rxh_ref, hl_ref, hlh_ref, lru_ref, dqkv1_ref, dqkv4_ref, dqkv16_ref, ddg_ref,
             caw_ref, crw_ref, crb_ref, wa_ref, wx_ref, ba_ref, bx_ref, lam_ref, gng_ref, ws_ref, wst_ref, bs_ref,
             dz_ref, ga_ref, gr_ref, gn_ref, gwa_ref, gwx_ref, gws_ref, gbs_ref,
             c_dcv, c_g, c_a, c_dxb, *stage):
        st_a, st_b = stage[:len(stage) // 2], stage[len(stage) // 2:]
        step = pl.program_id(0)
        i = n_tiles - 1 - step

        @pl.when(step == 0)
        def _():
            for r in (c_dcv, c_g, c_a, c_dxb, ga_ref, gr_ref, gn_ref, gwa_ref, gwx_ref, gws_ref, gbs_ref):
                r[...] = jnp.zeros_like(r)

        nz = (i > 0).astype(F32)
        head = _head_of_lane((TILE_MIX, GROUP_W))
        shp8 = (SUBLANES, GROUP_W)
        colsum = lambda v: jnp.sum(v, axis=0, keepdims=True)

        ax, ab, ac, ag = ax_ref[...], ab_ref[...], ac_ref[...], ag_ref[...]
        dya = dya_ref[...]
        pa = ac * ax
        pah = ach_ref[...] * axh_ref[...] * nz
        pa1 = _shift_down(pa, pah, 1)
        pa2 = _shift_down(pa, pah, 2)
        cv = caw_ref[2:3, :] * pa + caw_ref[1:2, :] * pa1 + caw_ref[0:1, :] * pa2
        sg = _silu(ag)
        dz_ref[:, C_AB * GROUP_W:(C_AB + 1) * GROUP_W] = (dya * cv * sg).astype(BF16)
        dz_ref[:, C_AG * GROUP_W:(C_AG + 1) * GROUP_W] = (dya * ab * cv * _dsilu(ag)).astype(BF16)
        dcv = dya * ab * sg
        nxt = c_dcv[...]
        dpa = caw_ref[2:3, :] * dcv + caw_ref[1:2, :] * _shift_up(dcv, nxt, 1) + caw_ref[0:1, :] * _shift_up(dcv, nxt, 2)
        c_dcv[...] = dcv[:SUBLANES, :]
        dz_ref[:, C_AC * GROUP_W:(C_AC + 1) * GROUP_W] = (dpa * ax).astype(BF16)
        dz_ref[:, C_AX * GROUP_W:(C_AX + 1) * GROUP_W] = (dpa * ac).astype(BF16)
        ga_ref[...] += (_put_row(shp8, 2, colsum(dcv * pa)) + _put_row(shp8, 1, colsum(dcv * pa1))
                        + _put_row(shp8, 0, colsum(dcv * pa2)))

        rx, rg = rx_ref[...], rg_ref[...]
        dyb = dyb_ref[...]
        rxh = rxh_ref[...] * nz
        rx1, rx2, rx3 = _shift_down(rx, rxh, 1), _shift_down(rx, rxh, 2), _shift_down(rx, rxh, 3)
        xb, r, ig, a, mult = (lru_ref[:, k * GROUP_W:(k + 1) * GROUP_W] for k in range(LRU_SAVED))
        lam = lam_ref[...]
        sp = _softplus_neg(lam)
        hl = hl_ref[...]
        hprev = _shift_down(hl, hlh_ref[...] * nz, 1)
        dz_ref[:, C_RG * GROUP_W:(C_RG + 1) * GROUP_W] = (dyb * hl * _dsilu(rg)).astype(BF16)
        dh = dyb * _silu(rg)
        a_next = _shift_up(a, c_a[...], 1)
        g = _scan_anticausal(a_next, dh, c_g[0:1, :])
        c_g[...] = g[:SUBLANES, :]
        c_a[...] = a[:SUBLANES, :]
        u = ig * xb
        da = g * hprev
        dmult = g * u
        du = g * mult
        dlog_a = da * a - dmult * (a * a) / mult
        dr = dlog_a * (-RG_C * sp)
        dga = dr * r * (1.0 - r)
        dgx = (du * xb) * ig * (1.0 - ig)
        dgab, dgxb = dga.astype(BF16), dgx.astype(BF16)
        dxb = du * ig + _dot_nt(dgab, wa_ref[...]) + _dot_nt(dgxb, wx_ref[...])
        xbb = xb.astype(BF16)
        gwa_ref[...] += _dot_tn(xbb, dgab)
        gwx_ref[...] += _dot_tn(xbb, dgxb)
        nxt = c_dxb[...]
        drx = (crw_ref[3:4, :] * dxb + crw_ref[2:3, :] * _shift_up(dxb, nxt, 1) + crw_ref[1:2, :] * _shift_up(dxb, nxt, 2)
               + crw_ref[0:1, :] * _shift_up(dxb, nxt, 3))
        c_dxb[...] = dxb[:SUBLANES, :]
        dz_ref[:, C_RX * GROUP_W:(C_RX + 1) * GROUP_W] = drx.astype(BF16)
        dlam = colsum(dlog_a * (-RG_C * r)) * (-jax.nn.sigmoid(-lam))
        gr_ref[...] += (_put_row(shp8, 3, colsum(dxb * rx)) + _put_row(shp8, 2, colsum(dxb * rx1))
                        + _put_row(shp8, 1, colsum(dxb * rx2)) + _put_row(shp8, 0, colsum(dxb * rx3))
                        + _put_row(shp8, 4, colsum(dxb)) + _put_row(shp8, 5, colsum(dga))
                        + _put_row(shp8, 6, colsum(dgx)) + _put_row(shp8, 7, dlam))

        cu, cvv, cg = cu_ref[...], cv_ref[...], cg_ref[...]
        dyc = dyc_ref[...]
        u_c, du_c = _gelu_and_grad(cu)
        gv, dgv_c = _gelu_and_grad(cvv)
        rs = lax.rsqrt(jnp.mean(gv * gv, axis=-1, keepdims=True) + NORM_EPS)
        vh = gv * rs
        gng = gng_ref[...]
        vvb = (vh * gng).astype(BF16)
        spat = _gmlp_spatial(ws_ref, vvb, head) + jnp.concatenate([bs_ref[...]] * (TILE_MIX // GMLP_CHUNK), axis=0)
        sgc = _silu(cg)
        dz_ref[:, C_CU * GROUP_W:(C_CU + 1) * GROUP_W] = (dyc * spat * sgc * du_c).astype(BF16)
        dz_ref[:, C_CG * GROUP_W:(C_CG + 1) * GROUP_W] = (dyc * u_c * spat * _dsilu(cg)).astype(BF16)
        dsp = dyc * u_c * sgc
        dspb = dsp.astype(BF16)
        tril = (lax.broadcasted_iota(jnp.int32, (GMLP_CHUNK, GMLP_CHUNK), 0)
                >= lax.broadcasted_iota(jnp.int32, (GMLP_CHUNK, GMLP_CHUNK), 1))
        head_c = head[:GMLP_CHUNK]
        dvv_parts = []
        gbs = jnp.zeros((GMLP_CHUNK, GROUP_W), F32)
        for j in range(TILE_MIX // GMLP_CHUNK):
            sl = slice(j * GMLP_CHUNK, (j + 1) * GMLP_CHUNK)
            dblk = dspb[sl, :]
            vblk = vvb[sl, :]
            gbs = gbs + dsp[sl, :]
            acc = jnp.zeros((GMLP_CHUNK, GROUP_W), F32)
            for h in range(N_HEADS):
                acc = jnp.where(head_c == h, _dot(wst_ref[h], dblk), acc)
                dm = jnp.where(head_c == h, dblk, jnp.zeros_like(dblk))
                gws_ref[h] += jnp.where(tril, _dot_nt(dm, vblk), 0.0)
            dvv_parts.append(acc)
        gbs_ref[...] += gbs
        dvv = jnp.concatenate(dvv_parts, axis=0)
        gn_ref[...] += _put_row(shp8, 0, colsum(dvv * vh))
        dvh = dvv * gng
        dgv = rs * (dvh - vh * jnp.mean(dvh * vh, axis=-1, keepdims=True))
        dz_ref[:, C_CV * GROUP_W:(C_CV + 1) * GROUP_W] = (dgv * dgv_c).astype(BF16)

        dsum = dqkv1_ref[...] + _interleave_load(dqkv4_ref, d4, st_a) + _interleave_load(dqkv16_ref, d16, st_b)
        dz_ref[:, C_DQ * GROUP_W:(C_DV + 1) * GROUP_W] = dsum.astype(BF16)
        dz_ref[:, C_DG * GROUP_W:(C_DG + 1) * GROUP_W] = ddg_ref[...].astype(BF16)

    per = TILE_MIX // SUBLANES
    qkv_w = 3 * GROUP_W
    rev = lambda c: pl.BlockSpec((TILE_MIX, GROUP_W), lambda t, c=c: (n_tiles - 1 - t, c))
    revh = lambda c: pl.BlockSpec((SUBLANES, GROUP_W),
                                  lambda t, c=c: (jnp.maximum((n_tiles - 1 - t) * per - 1, 0), c))
    revr = lambda dil: pl.BlockSpec((dil, TILE_MIX // dil, qkv_w), lambda t: (0, n_tiles - 1 - t, 0))
    names = ("caw", "crw", "crb", "wa", "wx", "ba", "bx", "lam", "gng", "ws", "wst", "bs")
    in_specs = ([rev(0), rev(1), rev(2)]
                + [rev(c) for c in (C_AX, C_AB, C_AC, C_AG, C_RX, C_RG, C_CU, C_CV, C_CG)]
                + [revh(C_AX), revh(C_AC), revh(C_RX), rev(0), revh(0),
                   pl.BlockSpec((TILE_MIX, LRU_SAVED * GROUP_W), lambda t: (n_tiles - 1 - t, 0)),
                   pl.BlockSpec((TILE_MIX, qkv_w), lambda t: (n_tiles - 1 - t, 0)), revr(d4), revr(d16), rev(0)]
                + [_of_layer(wts[k], l) for k in names])
    small = jax.ShapeDtypeStruct((SUBLANES, GROUP_W), F32)
    sq = jax.ShapeDtypeStruct((GROUP_W, GROUP_W), F32)
    out_shape = [jax.ShapeDtypeStruct((s, D_IN), BF16), small, small, small, sq, sq,
                 jax.ShapeDtypeStruct((N_HEADS, GMLP_CHUNK, GMLP_CHUNK), F32),
                 jax.ShapeDtypeStruct((GMLP_CHUNK, GROUP_W), F32)]
    out_specs = ([pl.BlockSpec((TILE_MIX, D_IN), lambda t: (n_tiles - 1 - t, 0))]
                 + [_full(o.shape) for o in out_shape[1:]])
    return pl.pallas_call(
        body, name="mix_bwd", grid=(n_tiles,),
        in_specs=in_specs, out_specs=out_specs, out_shape=out_shape,
        scratch_shapes=[pltpu.VMEM((SUBLANES, GROUP_W), F32)] * 4 + _stage_scratch(TILE_MIX, qkv_w, 2),
        compiler_params=_params(("arbitrary",)),
    )(dy, dy, dy, *([z] * 12), hl, hl, lru, *dqkv, ddg, *[wts[k] for k in names])


def prepare_small_weights(p):
    tril = jnp.tril(jnp.ones((GMLP_CHUNK, GMLP_CHUNK), dtype=bool))
    ws = jnp.where(tril, p["gmlp_ws"], 0.0).astype(BF16)
    row = lambda a: a[:, None, :]
    eye = jnp.eye(N_HEADS, dtype=F32)
    bd = lambda w: (w[:, :, :, None, :] * eye[None, :, None, :, None]).reshape(-1, GROUP_W, GROUP_W).astype(BF16)
    return dict(
        caw=p["conv_a_w"], crw=p["conv_r_w"], crb=row(p["conv_r_b"]),
        wa=bd(p["lru_wa"]), wx=bd(p["lru_wx"]),
        ba=row(p["lru_ba"]), bx=row(p["lru_bx"]), lam=row(p["lru_lambda"]), gng=row(p["gmlp_norm_g"]),
        ws=ws, wst=jnp.swapaxes(ws, 2, 3),
        bs=jnp.repeat(jnp.swapaxes(p["gmlp_bs"], 1, 2), HEAD_DIM, axis=2))


def _flat(a):
    return a.reshape(a.shape[0] * a.shape[1], a.shape[2])


def _split(a, dil):
    return a.reshape(dil, a.shape[0] // dil, a.shape[1])


def local_step(x, tgt, final_g, depth, chip, layer_weights, projections_done):
    saved = []
    for l in range(depth):
        gain, w_in_l, rest = layer_weights(l, x)
        z, h, *qkvs = in_fwd(x, gain, w_in_l)
        w_out_l, wts = rest(z)
        qkvs = [_flat(q) if q.ndim == 3 else q for q in qkvs]
        attn = []
        for q, d in zip(qkvs, PATTERN_DILS):
            o_p, lse_p = attn_fwd(q, d)
            attn.append((o_p, lse_p) if d == 1 else (_split(o_p, d), _split(lse_p, d)))
        y, hl, o, lse, lse4, lse16, lru = mix_fwd(z, attn, wts, l)
        saved.append(dict(x=x, z=z, h=h, y=y, hl=hl, o=o, qkvs=qkvs, lses=(lse, _flat(lse4), _flat(lse16)), wts=wts, lru=lru,
                          gain=gain, w_in=w_in_l, w_out=w_out_l))
        x = out_fwd(y, w_out_l, x)

    loss, dx, dfg = loss_head(x, final_g[None, :], tgt)
    raw = {k: [None] * depth for k in ("gain", "a", "r", "n", "wa", "wx", "ws", "bs")}
    zero = None
    for l in reversed(range(depth)):
        sv = saved[l]
        dy, ddg, do1, do4, do16, dl1, dl4, dl16 = out_bwd(dx, sv["w_out"], sv["z"], sv["o"])
        g_w_out = matmul_tn(sv["y"], dx, 1)
        dqkv = []
        for q, do, lse, dl, d in zip(sv["qkvs"], (do1, _flat(do4), _flat(do16)), sv["lses"],
                                     (dl1, _flat(dl4), _flat(dl16)), PATTERN_DILS):
            g = attn_bwd(q, do, lse, dl, d)
            dqkv.append(g if d == 1 else _split(g, d))
        dz, ga, gr, gn, gwa, gwx, gws, gbs = mix_bwd(dy, sv["z"], sv["hl"], sv["lru"], dqkv, ddg, sv["wts"], l)
        gain = sv["gain"] if zero is None else sv["gain"] + zero
        zero = projections_done(l, *grad_w_in(sv["h"], dz, chip), g_w_out)
        if l == 0 and zero is not None:
            gain = gain + zero
        dx, dgn = in_bwd(dz, sv["w_in"], sv["x"], gain, dx)
        for k, g in zip(("gain", "a", "r", "n", "wa", "wx", "ws", "bs"), (dgn, ga, gr, gn, gwa, gwx, gws, gbs)):
            raw[k][l] = g
    st = {k: jnp.stack(v) for k, v in raw.items()}
    eye = jnp.eye(N_HEADS, dtype=F32)[None, :, None, :, None]
    diag = lambda g: (g.reshape(depth, N_HEADS, HEAD_DIM, N_HEADS, HEAD_DIM) * eye).sum(axis=3)
    grads = dict(
        norm_g=st["gain"][:, 0], conv_a_w=st["a"][:, :3], conv_r_w=st["r"][:, :4], conv_r_b=st["r"][:, 4],
        lru_ba=st["r"][:, 5], lru_bx=st["r"][:, 6], lru_lambda=st["r"][:, 7], gmlp_norm_g=st["n"][:, 0],
        lru_wa=diag(st["wa"]), lru_wx=diag(st["wx"]), gmlp_ws=st["ws"],
        gmlp_bs=jnp.swapaxes(st["bs"].reshape(depth, GMLP_CHUNK, N_HEADS, HEAD_DIM).sum(-1), 1, 2),
        final_g=dfg[0])
    return loss, dx, grads


MESH = pl.DeviceIdType.MESH
N_CHIPS = 4
N_DEV = 8
ANY = pl.BlockSpec(memory_space=pl.ANY)


def _place():
    x, y, c = lax.axis_index("x"), lax.axis_index("y"), lax.axis_index("c")
    chips = [(1 - x, y), (x, 1 - y), (1 - x, 1 - y)]
    return x, y, c, chips


def _remote(src, dst, ssem, rsem, to):
    return pltpu.make_async_remote_copy(src_ref=src, dst_ref=dst, send_sem=ssem, recv_sem=rsem,
                                        device_id=to, device_id_type=MESH)


HBM = pl.BlockSpec(memory_space=pltpu.HBM)
SEM = pl.BlockSpec(memory_space=pltpu.SEMAPHORE)
DATAFLOW = pltpu.SideEffectType.DATAFLOW_SIDE_EFFECTING
GATHER, SCATTER = "gather", "scatter"


def _chip_copies(mode, src_refs, land_refs, ssem, rsem):
    x, y, c, chips = _place()
    me = 2 * x + y
    n = len(src_refs)
    copies = []
    for k, (cx, cy) in enumerate(chips):
        for a in range(n):
            if mode == GATHER:
                src, dst = src_refs[a], land_refs[a].at[me]
            else:
                src, dst = src_refs[a].at[2 * cx + cy], land_refs[a].at[k]
            copies.append(_remote(src, dst, ssem.at[n * k + a], rsem.at[n * k + a], (cx, cy, c)))
    return copies


def exchange_start(mode, srcs, after, name):
    n = len(srcs)
    if mode == GATHER:
        lands = [lax.empty((N_CHIPS,) + s.shape, s.dtype) for s in srcs]
    else:
        lands = [lax.empty((N_CHIPS - 1,) + s.shape[1:], s.dtype) for s in srcs]
    extra = [] if after is None else [after]

    def body(*refs):
        src_refs, land_refs = refs[:n], refs[n:2 * n]
        ssem, rsem = refs[2 * n + len(extra)], refs[2 * n + len(extra) + 1]
        token = refs[-1]
        for cp in _chip_copies(mode, src_refs, land_refs, ssem, rsem):
            cp.start()
        token[...] = jnp.zeros_like(token)

    arrays = list(srcs) + lands
    return pl.pallas_call(
        body, name=name,
        out_shape=(pltpu.SemaphoreType.DMA((3 * n,)), pltpu.SemaphoreType.DMA((3 * n,)),
                   *[pltpu.HBM(a.shape, a.dtype) for a in arrays], jax.ShapeDtypeStruct((SUBLANES, LANES), F32)),
        in_specs=[HBM] * (2 * n) + [ANY] * len(extra),
        out_specs=(SEM, SEM, *[HBM] * (2 * n), pl.BlockSpec(memory_space=pltpu.VMEM)),
        input_output_aliases={i: 2 + i for i in range(2 * n)},
        compiler_params=pltpu.CompilerParams(has_side_effects=DATAFLOW),
    )(*[pltpu.with_memory_space_constraint(a, pltpu.HBM) for a in arrays], *extra)


def exchange_wait(mode, started, after, name):
    ssem, rsem, *thru, _ = started
    n = len(thru) // 2

    def body(*refs):
        src_refs, land_refs = refs[:n], refs[n:2 * n]
        ssem_ref, rsem_ref = refs[2 * n], refs[2 * n + 1]
        for cp in _chip_copies(mode, src_refs, land_refs, ssem_ref, rsem_ref):
            cp.wait_send()
            cp.wait_recv()

    outs = pl.pallas_call(
        body, name=name,
        out_shape=[pltpu.HBM(a.shape, a.dtype) for a in thru],
        in_specs=[HBM] * (2 * n) + [SEM, SEM, ANY],
        out_specs=[HBM] * (2 * n),
        input_output_aliases={i: i for i in range(2 * n)},
        compiler_params=pltpu.CompilerParams(has_side_effects=DATAFLOW),
    )(*thru, ssem, rsem, after)
    return outs[n:]


def sibling_exchange(p1, p2):
    def body(p1_ref, p2_ref, q1_ref, q2_ref, ssem, rsem):
        x, y, c, _ = _place()
        copies = [_remote(p_ref, q_ref, ssem.at[a], rsem.at[a], (x, y, 1 - c))
                  for a, (p_ref, q_ref) in enumerate(((p1_ref, q1_ref), (p2_ref, q2_ref)))]
        for cp in copies:
            cp.start()
        for cp in copies:
            cp.wait()

    return pl.pallas_call(
        body, name="sibling_exchange",
        in_specs=[ANY, ANY], out_specs=[ANY, ANY],
        out_shape=[jax.ShapeDtypeStruct(p.shape, p.dtype) for p in (p1, p2)],
        scratch_shapes=[pltpu.SemaphoreType.DMA((2,)), pltpu.SemaphoreType.DMA((2,))],
    )(p1, p2)


def all_reduce_small(v):
    r, n = v.shape
    piece = r // N_DEV

    def body(x_ref, out_ref, recv, ssem1, rsem1, ssem2, rsem2):
        x, y, c, _ = _place()
        me = 4 * x + 2 * y + c

        def peer(k):
            px = 1 - x if (k >> 2) & 1 else x
            py = 1 - y if (k >> 1) & 1 else y
            pc = 1 - c if k & 1 else c
            return (px, py, pc), 4 * px + 2 * py + pc

        def rows(ref, d):
            return ref.at[pl.ds(d * piece, piece), :]

        scatter = []
        for k in range(1, N_DEV):
            to, idx = peer(k)
            scatter.append(_remote(rows(x_ref, idx), recv.at[k], ssem1.at[k - 1], rsem1.at[k - 1], to))
            scatter[-1].start()
        acc = rows(x_ref, me)[...]
        for k in range(1, N_DEV):
            scatter[k - 1].wait_recv()
            acc = acc + recv[k]
        rows(out_ref, me)[...] = acc

        gather = []
        for k in range(1, N_DEV):
            to, _ = peer(k)
            gather.append(_remote(rows(out_ref, me), rows(out_ref, me), ssem2.at[k - 1], rsem2.at[k - 1], to))
            gather[-1].start()
        for k in range(1, N_DEV):
            to, idx = peer(k)
            _remote(rows(out_ref, idx), rows(out_ref, idx), ssem2.at[k - 1], rsem2.at[k - 1], to).wait_recv()
        for cp in scatter + gather:
            cp.wait_send()

    return pl.pallas_call(
        body, name="all_reduce_small",
        out_shape=jax.ShapeDtypeStruct((r, n), v.dtype),
        in_specs=[pl.BlockSpec(memory_space=pltpu.VMEM)],
        out_specs=pl.BlockSpec(memory_space=pltpu.VMEM),
        scratch_shapes=[pltpu.VMEM((N_DEV, piece, n), v.dtype)] + [pltpu.SemaphoreType.DMA((N_DEV - 1,))] * 4,
        compiler_params=pltpu.CompilerParams(vmem_limit_bytes=VMEM_LIMIT),
    )(v)


TILE_ROWS = 256


def sum_partials(own, parts):
    k, r, c = parts.shape

    def body(o_ref, p_ref, out_ref):
        acc = o_ref[...]
        for i in range(k):
            acc = acc + p_ref[i].astype(F32)
        out_ref[...] = acc

    row = pl.BlockSpec((TILE_ROWS, c), lambda i: (i, 0))
    return pl.pallas_call(
        body, name="sum_partials", grid=(r // TILE_ROWS,),
        in_specs=[row, pl.BlockSpec((k, TILE_ROWS, c), lambda i: (0, i, 0))],
        out_specs=row,
        out_shape=jax.ShapeDtypeStruct((r, c), F32),
        compiler_params=_params(("parallel",)),
    )(own, parts)


def _adamw_update(w, g, m, v):
    m2 = ADAM_B1 * m + (1.0 - ADAM_B1) * g
    v2 = ADAM_B2 * v + (1.0 - ADAM_B2) * (g * g)
    m_hat = m2 / (1.0 - ADAM_B1 ** ADAM_STEP)
    v_hat = v2 / (1.0 - ADAM_B2 ** ADAM_STEP)
    return -ADAM_LR * (m_hat / (jnp.sqrt(v_hat) + ADAM_EPS) + ADAM_WD * w), m2, v2


def adamw_small(ws, gs, ms, vs):
    n = len(ws)

    def body(*refs):
        ins, outs = refs[:4 * n], refs[4 * n:]
        for i in range(n):
            d, m2, v2 = _adamw_update(ins[i][...], ins[n + i][...], ins[2 * n + i][...], ins[3 * n + i][...])
            outs[3 * i][...] = d
            outs[3 * i + 1][...] = m2
            outs[3 * i + 2][...] = v2

    outs = pl.pallas_call(
        body, name="adamw_small",
        out_shape=[jax.ShapeDtypeStruct(w.shape, F32) for w in ws for _ in range(3)],
    )(*ws, *gs, *ms, *vs)
    return [tuple(outs[3 * i:3 * i + 3]) for i in range(n)]


def adamw(w, ga, gb, m, v):
    n, r, c = w.shape
    tile = max(t for t in range(SUBLANES, TILE_ROWS + 1, SUBLANES) if r % t == 0)

    def body(w_ref, ga_ref, gb_ref, m_ref, v_ref, g_ref, d_ref, m2_ref, v2_ref):
        g = ga_ref[...] + gb_ref[...]
        g_ref[...] = g
        d_ref[...], m2_ref[...], v2_ref[...] = _adamw_update(w_ref[...], g, m_ref[...], v_ref[...])

    spec = pl.BlockSpec((1, tile, c), lambda j, i: (j, i, 0))
    return pl.pallas_call(
        body, name="adamw", grid=(n, r // tile),
        in_specs=[spec] * 5, out_specs=[spec] * 4,
        out_shape=[jax.ShapeDtypeStruct((n, r, c), F32)] * 4,
        compiler_params=_params(("parallel", "parallel")),
    )(w, ga, gb, m, v)


REPLICATED = ("norm_g", "conv_r_b", "lru_wa", "lru_ba", "lru_wx", "lru_bx", "lru_lambda", "gmlp_norm_g",
              "gmlp_ws", "gmlp_bs", "final_g")
CHIP_SHARDED_SMALL = ("conv_a_w", "conv_r_w")
PACK_LANES = 128


def _pack(arrays):
    flat = jnp.concatenate([a.reshape(-1) for a in arrays])
    pad = (-flat.shape[0]) % (TILE_ROWS * PACK_LANES)
    return jnp.pad(flat, (0, pad)).reshape(-1, PACK_LANES)


def _unpack(packed, shapes):
    flat = packed.reshape(-1)
    out, off = [], 0
    for shp in shapes:
        n = math.prod(shp)
        out.append(flat[off:off + n].reshape(shp))
        off += n
    return out


def kernel(x, norm_g, w_in, conv_a_w, conv_r_w, conv_r_b, lru_wa, lru_ba, lru_wx, lru_bx, lru_lambda, gmlp_norm_g, gmlp_ws, gmlp_bs, w_out, final_g, loss_target, m_norm_g, m_w_in, m_conv_a_w, m_conv_r_w, m_conv_r_b, m_lru_wa, m_lru_ba, m_lru_wx, m_lru_bx, m_lru_lambda, m_gmlp_norm_g, m_gmlp_ws, m_gmlp_bs, m_w_out, m_final_g, v_norm_g, v_w_in, v_conv_a_w, v_conv_r_w, v_conv_r_b, v_lru_wa, v_lru_ba, v_lru_wx, v_lru_bx, v_lru_lambda, v_gmlp_norm_g, v_gmlp_ws, v_gmlp_bs, v_w_out, v_final_g):
    names = ("norm_g", "w_in", "conv_a_w", "conv_r_w", "conv_r_b", "lru_wa", "lru_ba", "lru_wx", "lru_bx",
             "lru_lambda", "gmlp_norm_g", "gmlp_ws", "gmlp_bs", "w_out", "final_g")
    w = dict(zip(names, (norm_g, w_in, conv_a_w, conv_r_w, conv_r_b, lru_wa, lru_ba, lru_wx, lru_bx, lru_lambda,
                         gmlp_norm_g, gmlp_ws, gmlp_bs, w_out, final_g)))
    m = dict(zip(names, (m_norm_g, m_w_in, m_conv_a_w, m_conv_r_w, m_conv_r_b, m_lru_wa, m_lru_ba, m_lru_wx, m_lru_bx,
                         m_lru_lambda, m_gmlp_norm_g, m_gmlp_ws, m_gmlp_bs, m_w_out, m_final_g)))
    v = dict(zip(names, (v_norm_g, v_w_in, v_conv_a_w, v_conv_r_w, v_conv_r_b, v_lru_wa, v_lru_ba, v_lru_wx, v_lru_bx,
                         v_lru_lambda, v_gmlp_norm_g, v_gmlp_ws, v_gmlp_bs, v_w_out, v_final_g)))
    depth, _, in_cols = w_in.shape
    out_rows = w_out.shape[1]
    conv_ch = conv_a_w.shape[2]
    chip = 2 * lax.axis_index("x") + lax.axis_index("y")

    taps = conv_a_w.shape[1] + conv_r_w.shape[1]
    w_in_t, m_w_in_t, v_w_in_t = (jnp.swapaxes(a, 1, 2) for a in (w_in, m_w_in, v_w_in))
    w_in_h, w_out_h = w_in_t.astype(BF16), w_out.astype(BF16)
    conv_own = jnp.concatenate([conv_a_w, conv_r_w], axis=1).reshape(depth * taps, conv_ch)
    gathers, token = [], None
    for l in range(depth):
        groups = [[w_in_h[l]], [w_out_h[l], conv_own]] if l == 0 else [[w_in_h[l], w_out_h[l]]]
        gathers.append([])
        for i, srcs in enumerate(groups):
            gathers[l].append(exchange_start(GATHER, srcs, token, f"gather_start_{l}_{i}"))
            token = gathers[l][-1][-1]
    p = dict(w)

    def with_own(land, own):
        return lax.dynamic_update_slice(land, own[None], (chip,) + (0,) * own.ndim)

    def layer_weights(l, x_l):
        lands = list(exchange_wait(GATHER, gathers[l][0], x_l, f"gather_wait_{l}_0"))
        w_in_l = with_own(lands[0], w_in_h[l]).reshape(D_IN, D_MODEL)
        gain = norm_g[l][None, :]
        if l == 0:
            gain = gain + token[0, 0]

        def rest(z_l):
            if l == 0:
                lands.extend(exchange_wait(GATHER, gathers[l][1], z_l, f"gather_wait_{l}_1"))
                conv = with_own(lands[2], conv_own).reshape(N_CHIPS, depth, taps, conv_ch)
                conv = conv.transpose(1, 2, 0, 3).reshape(depth, taps, GROUP_W)
                p["conv_a_w"] = conv[:, :conv_a_w.shape[1]]
                p["conv_r_w"] = conv[:, conv_a_w.shape[1]:]
                p["prepared"] = prepare_small_weights(p)
            return with_own(lands[1], w_out_h[l]).reshape(D_MIX, D_MODEL), p["prepared"]

        return gain, w_in_l, rest

    scatters, owns = [None] * depth, [None] * depth

    def projections_done(l, g_w_in_by_chip, g_w_in_own, g_w_out):
        go = g_w_out.reshape(N_CHIPS, out_rows, D_MODEL)
        owns[l] = (g_w_in_own, lax.dynamic_index_in_dim(go, chip, axis=0, keepdims=False))
        scatters[l] = exchange_start(SCATTER, [g_w_in_by_chip, go.astype(BF16)], None, f"scatter_start_{l}")
        return scatters[l][-1][0, 0]

    loss8, dx, grads = local_step(x[0], loss_target[0], final_g, depth, chip.reshape(1), layer_weights,
                                  projections_done)
    loss = lax.psum(loss8[0, 0], ("x", "y", "c"))

    parts = [exchange_wait(SCATTER, scatters[l], dx, f"scatter_wait_{l}") for l in range(depth)]
    r1 = jnp.concatenate([parts[l][0] for l in range(depth)], axis=1)
    r2 = jnp.concatenate([parts[l][1] for l in range(depth)], axis=1)
    own1 = jnp.concatenate([owns[l][0] for l in range(depth)], axis=0)
    own2 = jnp.concatenate([owns[l][1] for l in range(depth)], axis=0)
    p1 = sum_partials(own1, r1)
    p2 = sum_partials(own2, r2)
    q1, q2 = sibling_exchange(p1, p2)
    res = {}
    res["w_in"] = [jnp.swapaxes(t, 1, 2) for t in
                   adamw(w_in_t, p1.reshape(w_in_t.shape), q1.reshape(w_in_t.shape), m_w_in_t, v_w_in_t)]
    res["w_out"] = adamw(w_out, p2.reshape(w_out.shape), q2.reshape(w_out.shape), m_w_out, v_w_out)

    small = REPLICATED + CHIP_SHARDED_SMALL
    packed = _pack([grads[k] for k in small])
    total = all_reduce_small(packed)
    gs = dict(zip(small, _unpack(total, [grads[k].shape for k in small])))
    for k in CHIP_SHARDED_SMALL:
        gs[k] = lax.dynamic_slice_in_dim(gs[k], chip * conv_ch, conv_ch, axis=2)
    as2d = lambda a: a[None] if a.ndim == 1 else a
    outs = adamw_small(*[[as2d(d[k]) for k in small] for d in (w, gs, m, v)])
    for k, (delta, m2, v2) in zip(small, outs):
        res[k] = [t.reshape(w[k].shape) for t in (gs[k], delta, m2, v2)]

    return (loss, dx[None], *[res[k][0] for k in names], *[res[k][1] for k in names],
            *[res[k][2] for k in names], *[res[k][3] for k in names])
```

```python
import functools
import math

import jax
import jax.numpy as jnp
import numpy as np
from jax import lax
from jax.experimental import pallas as pl
from jax.experimental.pallas import tpu as pltpu

F32 = jnp.float32
BF16 = jnp.bfloat16

D_MODEL = 1024
GROUP_W = 256
N_HEADS = 4
HEAD_DIM = 64
N_CHUNKS = 13
D_IN = N_CHUNKS * GROUP_W
D_MIX = 4 * GROUP_W
NORM_EPS = 1e-6
RG_C = 8.0
GMLP_CHUNK = 128
ATTN_BLOCK = 128
PATTERN_DILS = (1, 4, 16)
N_PATTERNS = len(PATTERN_DILS)
ALIBI_SLOPES = tuple(2.0 ** (-8.0 * (h + 1) / N_HEADS) for h in range(N_HEADS))
ATTN_SCALE = 1.0 / math.sqrt(HEAD_DIM)
NEG_BIG = -1e30

ADAM_LR = 0.001
ADAM_B1 = 0.9
ADAM_B2 = 0.999
ADAM_EPS = 1e-08
ADAM_WD = 0.01
ADAM_STEP = 10

C_AX, C_AB, C_AC, C_AG, C_RX, C_RG, C_CU, C_CV, C_CG, C_DQ, C_DK, C_DV, C_DG = range(13)

SUBLANES = 8
LANES = 128
VMEM_LIMIT = 56 * 1024 * 1024
TILE_IN = 512
TILE_MIX = 512
TILE_DW = 512
ATTN_QB = 4
GELU_K0 = math.sqrt(2.0 / math.pi)
GELU_K1 = 0.044715


def _params(sem):
    return pltpu.CompilerParams(dimension_semantics=sem, vmem_limit_bytes=VMEM_LIMIT)


def _silu(x):
    return x * jax.nn.sigmoid(x)


def _dsilu(x):
    s = jax.nn.sigmoid(x)
    return s * (1.0 + x * (1.0 - s))


def _gelu(x):
    return 0.5 * x * (1.0 + jnp.tanh(GELU_K0 * (x + GELU_K1 * x * x * x)))


def _gelu_and_grad(x):
    t = jnp.tanh(GELU_K0 * (x + GELU_K1 * x * x * x))
    g = 0.5 * x * (1.0 + t)
    dg = 0.5 * (1.0 + t) + 0.5 * x * (1.0 - t * t) * GELU_K0 * (1.0 + 3.0 * GELU_K1 * x * x)
    return g, dg


def _neg_expm1(x):
    series = x * (1.0 + x * (0.5 + x * (1.0 / 6.0 + x * (1.0 / 24.0 + x * (1.0 / 120.0)))))
    return -jnp.where(x > -0.05, series, jnp.exp(x) - 1.0)


def _shift_down(v, halo, k):
    r = pltpu.roll(v, k, 0)
    rh = pltpu.roll(halo, k, 0)
    row = lax.broadcasted_iota(jnp.int32, halo.shape, 0)
    top = jnp.where(row < k, rh, r[:SUBLANES])
    return jnp.concatenate([top, r[SUBLANES:]], axis=0)


def _shift_up(v, halo, k):
    t = v.shape[0]
    r = pltpu.roll(v, t - k, 0)
    rh = pltpu.roll(halo, SUBLANES - k, 0)
    row = lax.broadcasted_iota(jnp.int32, halo.shape, 0)
    bot = jnp.where(row >= SUBLANES - k, rh, r[t - SUBLANES:])
    return jnp.concatenate([r[:t - SUBLANES], bot], axis=0)


def _scan_causal(a, b, h_in):
    t = a.shape[0]
    row8 = lax.broadcasted_iota(jnp.int32, a.shape, 0) % SUBLANES
    d = 1
    while d < SUBLANES:
        m = row8 >= d
        a_s = jnp.where(m, pltpu.roll(a, d, 0), 1.0)
        b_s = jnp.where(m, pltpu.roll(b, d, 0), 0.0)
        b = a * b_s + b
        a = a * a_s
        d *= 2
    out, carry = [], h_in
    for g in range(t // SUBLANES):
        sl = slice(g * SUBLANES, (g + 1) * SUBLANES)
        hg = b[sl] + a[sl] * carry
        out.append(hg)
        carry = hg[SUBLANES - 1:SUBLANES]
    return jnp.concatenate(out, axis=0)


def _scan_anticausal(a, b, g_in):
    t = a.shape[0]
    row8 = lax.broadcasted_iota(jnp.int32, a.shape, 0) % SUBLANES
    d = 1
    while d < SUBLANES:
        m = row8 < SUBLANES - d
        a_s = jnp.where(m, pltpu.roll(a, t - d, 0), 1.0)
        b_s = jnp.where(m, pltpu.roll(b, t - d, 0), 0.0)
        b = a * b_s + b
        a = a * a_s
        d *= 2
    out, carry = [], g_in
    for g in reversed(range(t // SUBLANES)):
        sl = slice(g * SUBLANES, (g + 1) * SUBLANES)
        gg = b[sl] + a[sl] * carry
        out.append(gg)
        carry = gg[0:1]
    return jnp.concatenate(out[::-1], axis=0)


def _head_of_lane(shape):
    return lax.broadcasted_iota(jnp.int32, shape, len(shape) - 1) // HEAD_DIM


def _put_row(acc_shape, k, row_vec):
    row = lax.broadcasted_iota(jnp.int32, acc_shape, 0)
    return jnp.where(row == k, jnp.broadcast_to(row_vec, acc_shape), 0.0)


def _dot(a, b):
    return jnp.dot(a, b, preferred_element_type=F32)


def _dot_nt(a, b):
    return lax.dot_general(a, b, (((1,), (1,)), ((), ())), preferred_element_type=F32)


def _dot_tn(a, b):
    return lax.dot_general(a, b, (((0,), (0,)), ((), ())), preferred_element_type=F32)


def _deinterleave_store(val, stage, outs):
    t, c = val.shape
    for hh in range(c // LANES):
        stage[hh][...] = val[:, hh * LANES:(hh + 1) * LANES].astype(F32)
    for dil, ref in outs:
        for r in range(dil):
            for hh in range(c // LANES):
                ref[r, :, hh * LANES:(hh + 1) * LANES] = stage[hh][pl.ds(r, t // dil, stride=dil), :].astype(ref.dtype)


def _interleave_load(ref, dil, stage):
    _, n, c = ref.shape
    for r in range(dil):
        for hh in range(c // LANES):
            stage[hh][pl.ds(r, n, stride=dil), :] = ref[r, :, hh * LANES:(hh + 1) * LANES].astype(F32)
    return jnp.concatenate([stage[hh][...] for hh in range(c // LANES)], axis=1)


def _stage_scratch(tile, cols, copies):
    return [pltpu.VMEM((tile, LANES), F32)] * (copies * (cols // LANES))


def _by_residue(s, dil, cols, dtype):
    return jax.ShapeDtypeStruct((dil, s // dil, cols), dtype)


def _residue_block(dil, tile, cols):
    return pl.BlockSpec((dil, tile // dil, cols), lambda i: (0, i, 0))


def in_fwd(x, g, w):
    s = x.shape[0]
    qkv_w = 3 * GROUP_W

    def body(x_ref, g_ref, w_ref, z_ref, h_ref, qkv1_ref, qkv4_ref, qkv16_ref, *stage):
        xv = x_ref[...]
        rs = lax.rsqrt(jnp.mean(xv * xv, axis=-1, keepdims=True) + NORM_EPS)
        h = (xv * rs * g_ref[...]).astype(BF16)
        h_ref[...] = h
        z = _dot_nt(h, w_ref[...])
        z_ref[...] = z
        qkv = z[:, C_DQ * GROUP_W:(C_DV + 1) * GROUP_W]
        qkv1_ref[...] = qkv.astype(BF16)
        _deinterleave_store(qkv, stage, ((PATTERN_DILS[1], qkv4_ref), (PATTERN_DILS[2], qkv16_ref)))

    return pl.pallas_call(
        body, name="in_fwd", grid=(s // TILE_IN,),
        in_specs=[pl.BlockSpec((TILE_IN, D_MODEL), lambda i: (i, 0)),
                  pl.BlockSpec((1, D_MODEL), lambda i: (0, 0)),
                  pl.BlockSpec((D_IN, D_MODEL), lambda i: (0, 0))],
        out_specs=[pl.BlockSpec((TILE_IN, D_IN), lambda i: (i, 0)),
                   pl.BlockSpec((TILE_IN, D_MODEL), lambda i: (i, 0)),
                   pl.BlockSpec((TILE_IN, qkv_w), lambda i: (i, 0)),
                   _residue_block(PATTERN_DILS[1], TILE_IN, qkv_w),
                   _residue_block(PATTERN_DILS[2], TILE_IN, qkv_w)],
        out_shape=[jax.ShapeDtypeStruct((s, D_IN), F32), jax.ShapeDtypeStruct((s, D_MODEL), BF16),
                   jax.ShapeDtypeStruct((s, qkv_w), BF16),
                   _by_residue(s, PATTERN_DILS[1], qkv_w, BF16), _by_residue(s, PATTERN_DILS[2], qkv_w, BF16)],
        scratch_shapes=_stage_scratch(TILE_IN, qkv_w, 1),
        compiler_params=_params(("parallel",)),
    )(x, g, w)


def out_fwd(y, w, x):
    s = x.shape[0]

    def body(y_ref, w_ref, x_ref, o_ref):
        o_ref[...] = x_ref[...] + _dot(y_ref[...], w_ref[...])

    return pl.pallas_call(
        body, name="out_fwd", grid=(s // TILE_IN,),
        in_specs=[pl.BlockSpec((TILE_IN, D_MIX), lambda i: (i, 0)),
                  pl.BlockSpec((D_MIX, D_MODEL), lambda i: (0, 0)),
                  pl.BlockSpec((TILE_IN, D_MODEL), lambda i: (i, 0))],
        out_specs=pl.BlockSpec((TILE_IN, D_MODEL), lambda i: (i, 0)),
        out_shape=jax.ShapeDtypeStruct((s, D_MODEL), F32),
        compiler_params=_params(("parallel",)),
    )(y, w, x)


def out_bwd(dx, w, z, o):
    s = dx.shape[0]
    abc = 3 * GROUP_W

    def body(dx_ref, w_ref, dg_ref, o_ref, dy_ref, ddg_ref, do1_ref, do4_ref, do16_ref, dl1_ref, dl4_ref, dl16_ref,
             *stage):
        stage_a, stage_b = stage[:2], stage[2:]
        dy = _dot_nt(dx_ref[...].astype(BF16), w_ref[...])
        dy_ref[...] = dy[:, :abc]
        dyd = dy[:, abc:]
        head = _head_of_lane((TILE_IN, GROUP_W))
        dg = dg_ref[...]
        o = o_ref[...]
        do = dyd * _silu(dg)
        ddg_ref[...] = dyd * o * _dsilu(dg)
        prod = do * o
        dl = jnp.zeros_like(prod)
        for h in range(N_HEADS):
            sm = jnp.sum(jnp.where(head == h, prod, 0.0), axis=-1, keepdims=True)
            dl = jnp.where(head == h, sm, dl)
        do1_ref[...] = do.astype(BF16)
        dl1_ref[...] = dl
        _deinterleave_store(do, stage_a, ((PATTERN_DILS[1], do4_ref), (PATTERN_DILS[2], do16_ref)))
        _deinterleave_store(dl, stage_b, ((PATTERN_DILS[1], dl4_ref), (PATTERN_DILS[2], dl16_ref)))

    row = pl.BlockSpec((TILE_IN, GROUP_W), lambda i: (i, 0))
    r4 = _residue_block(PATTERN_DILS[1], TILE_IN, GROUP_W)
    r16 = _residue_block(PATTERN_DILS[2], TILE_IN, GROUP_W)
    return pl.pallas_call(
        body, name="out_bwd", grid=(s // TILE_IN,),
        in_specs=[pl.BlockSpec((TILE_IN, D_MODEL), lambda i: (i, 0)),
                  pl.BlockSpec((D_MIX, D_MODEL), lambda i: (0, 0)),
                  pl.BlockSpec((TILE_IN, GROUP_W), lambda i: (i, C_DG)), row],
        out_specs=[pl.BlockSpec((TILE_IN, abc), lambda i: (i, 0)), row, row, r4, r16, row, r4, r16],
        out_shape=[jax.ShapeDtypeStruct((s, abc), F32), jax.ShapeDtypeStruct((s, GROUP_W), F32),
                   jax.ShapeDtypeStruct((s, GROUP_W), BF16),
                   _by_residue(s, PATTERN_DILS[1], GROUP_W, BF16), _by_residue(s, PATTERN_DILS[2], GROUP_W, BF16),
                   jax.ShapeDtypeStruct((s, GROUP_W), F32),
                   _by_residue(s, PATTERN_DILS[1], GROUP_W, F32), _by_residue(s, PATTERN_DILS[2], GROUP_W, F32)],
        scratch_shapes=_stage_scratch(TILE_IN, GROUP_W, 2),
        compiler_params=_params(("parallel",)),
    )(dx, w, z, o)


def in_bwd(dz, w, x, g, dx_next):
    s = x.shape[0]

    def body(dz_ref, w_ref, x_ref, g_ref, dxn_ref, dx_ref, dg_ref):
        @pl.when(pl.program_id(0) == 0)
        def _():
            dg_ref[...] = jnp.zeros_like(dg_ref)

        dh = _dot(dz_ref[...], w_ref[...])
        xv = x_ref[...]
        rs = lax.rsqrt(jnp.mean(xv * xv, axis=-1, keepdims=True) + NORM_EPS)
        xh = xv * rs
        dg_ref[...] += _put_row(dg_ref.shape, 0, jnp.sum(dh * xh, axis=0, keepdims=True))
        dn = dh * g_ref[...]
        dx_ref[...] = dxn_ref[...] + rs * (dn - xh * jnp.mean(dn * xh, axis=-1, keepdims=True))

    return pl.pallas_call(
        body, name="in_bwd", grid=(s // TILE_IN,),
        in_specs=[pl.BlockSpec((TILE_IN, D_IN), lambda i: (i, 0)),
                  pl.BlockSpec((D_IN, D_MODEL), lambda i: (0, 0)),
                  pl.BlockSpec((TILE_IN, D_MODEL), lambda i: (i, 0)),
                  pl.BlockSpec((1, D_MODEL), lambda i: (0, 0)),
                  pl.BlockSpec((TILE_IN, D_MODEL), lambda i: (i, 0))],
        out_specs=[pl.BlockSpec((TILE_IN, D_MODEL), lambda i: (i, 0)),
                   pl.BlockSpec((SUBLANES, D_MODEL), lambda i: (0, 0))],
        out_shape=[jax.ShapeDtypeStruct((s, D_MODEL), F32), jax.ShapeDtypeStruct((SUBLANES, D_MODEL), F32)],
        compiler_params=_params(("arbitrary",)),
    )(dz, w, x, g, dx_next)


def matmul_tn(a, b, n_split):
    s, m = a.shape
    n = b.shape[1]
    tn = n // n_split

    def body(a_ref, b_ref, o_ref):
        @pl.when(pl.program_id(1) == 0)
        def _():
            o_ref[...] = jnp.zeros_like(o_ref)

        o_ref[...] += _dot_tn(a_ref[...], b_ref[...].astype(BF16))

    return pl.pallas_call(
        body, name="matmul_tn", grid=(n_split, s // TILE_DW),
        in_specs=[pl.BlockSpec((TILE_DW, m), lambda j, k: (k, 0)),
                  pl.BlockSpec((TILE_DW, tn), lambda j, k: (k, j))],
        out_specs=pl.BlockSpec((m, tn), lambda j, k: (0, j)),
        out_shape=jax.ShapeDtypeStruct((m, n), F32),
        compiler_params=_params(("parallel", "arbitrary")),
    )(a, b)


def grad_w_in(h, dz, chip):
    s = h.shape[0]
    rows = D_IN // N_CHIPS

    def body(chip_ref, h_ref, dz_ref, staged_ref, own_ref, acc):
        k = pl.program_id(0)

        @pl.when(k == 0)
        def _():
            acc[...] = jnp.zeros_like(acc)

        acc[...] += _dot_tn(dz_ref[...], h_ref[...])

        @pl.when(k == s // TILE_DW - 1)
        def _():
            for j in range(N_CHIPS):
                part = acc[j * rows:(j + 1) * rows, :]
                staged_ref[j] = part.astype(BF16)

                @pl.when(chip_ref[0] == j)
                def _():
                    own_ref[...] = part

    return pl.pallas_call(
        body, name="grad_w_in",
        grid_spec=pltpu.PrefetchScalarGridSpec(
            num_scalar_prefetch=1, grid=(s // TILE_DW,),
            in_specs=[pl.BlockSpec((TILE_DW, D_MODEL), lambda k, c: (k, 0)),
                      pl.BlockSpec((TILE_DW, D_IN), lambda k, c: (k, 0))],
            out_specs=[pl.BlockSpec((N_CHIPS, rows, D_MODEL), lambda k, c: (0, 0, 0)),
                       pl.BlockSpec((rows, D_MODEL), lambda k, c: (0, 0))],
            scratch_shapes=[pltpu.VMEM((D_IN, D_MODEL), F32)]),
        out_shape=[jax.ShapeDtypeStruct((N_CHIPS, rows, D_MODEL), BF16), jax.ShapeDtypeStruct((rows, D_MODEL), F32)],
        compiler_params=_params(("arbitrary",)),
    )(chip, h, dz)


def loss_head(x, g, tgt):
    s = x.shape[0]

    def body(x_ref, g_ref, t_ref, l_ref, dx_ref, dg_ref):
        @pl.when(pl.program_id(0) == 0)
        def _():
            l_ref[...] = jnp.zeros_like(l_ref)
            dg_ref[...] = jnp.zeros_like(dg_ref)

        xv = x_ref[...]
        gv = g_ref[...]
        rs = lax.rsqrt(jnp.mean(xv * xv, axis=-1, keepdims=True) + NORM_EPS)
        xh = xv * rs
        e = xh * gv - t_ref[...]
        part = 0.5 * jnp.sum(jnp.mean(e * e, axis=-1, keepdims=True), axis=0, keepdims=True)
        l_ref[...] += jnp.broadcast_to(part, l_ref.shape)
        dy = e * (1.0 / D_MODEL)
        dg_ref[...] += _put_row(dg_ref.shape, 0, jnp.sum(dy * xh, axis=0, keepdims=True))
        dn = dy * gv
        dx_ref[...] = rs * (dn - xh * jnp.mean(dn * xh, axis=-1, keepdims=True))

    return pl.pallas_call(
        body, name="loss_head", grid=(s // TILE_IN,),
        in_specs=[pl.BlockSpec((TILE_IN, D_MODEL), lambda i: (i, 0)),
                  pl.BlockSpec((1, D_MODEL), lambda i: (0, 0)),
                  pl.BlockSpec((TILE_IN, D_MODEL), lambda i: (i, 0))],
        out_specs=[pl.BlockSpec((SUBLANES, 128), lambda i: (0, 0)),
                   pl.BlockSpec((TILE_IN, D_MODEL), lambda i: (i, 0)),
                   pl.BlockSpec((SUBLANES, D_MODEL), lambda i: (0, 0))],
        out_shape=[jax.ShapeDtypeStruct((SUBLANES, 128), F32), jax.ShapeDtypeStruct((s, D_MODEL), F32),
                   jax.ShapeDtypeStruct((SUBLANES, D_MODEL), F32)],
        compiler_params=_params(("arbitrary",)),
    )(x, g, tgt)


def _attn_bias(dil):
    qi = np.arange(ATTN_BLOCK)[:, None]
    ki = np.arange(2 * ATTN_BLOCK)[None, :]
    delta = qi + ATTN_BLOCK - ki
    band = (delta >= 0) & (delta <= ATTN_BLOCK)
    out = np.empty((2, N_HEADS, ATTN_BLOCK, 2 * ATTN_BLOCK), np.float32)
    for f in range(2):
        ok = band & ((ki >= ATTN_BLOCK) | (f == 0))
        for h in range(N_HEADS):
            out[f, h] = np.where(ok, -ALIBI_SLOPES[h] * dil * delta, NEG_BIG)
    return jnp.asarray(out.reshape(2, N_HEADS * ATTN_BLOCK, 2 * ATTN_BLOCK))


def _stack_heads(a, head):
    return jnp.concatenate([jnp.where(head == h, a, jnp.zeros_like(a)) for h in range(N_HEADS)], axis=0)


def _unstack_heads(a, head):
    out = a[:ATTN_BLOCK]
    for h in range(1, N_HEADS):
        out = jnp.where(head == h, a[h * ATTN_BLOCK:(h + 1) * ATTN_BLOCK], out)
    return out


def _head_column(a):
    return jnp.concatenate([a[:, h * HEAD_DIM:h * HEAD_DIM + 1] for h in range(N_HEADS)], axis=0)


def _attn_specs(n_blocks, phase):
    rows = ATTN_QB * ATTN_BLOCK
    ns = n_blocks // ATTN_QB
    loc = lambda n: jnp.clip(n - phase * ns, 0, ns - 1)
    cur = lambda c: pl.BlockSpec((rows, GROUP_W), lambda n, c=c: (loc(n), c))
    prev = lambda c: pl.BlockSpec((ATTN_BLOCK, GROUP_W), lambda n, c=c: (jnp.maximum(loc(n) * ATTN_QB - 1, 0), c))
    nxt = lambda c: pl.BlockSpec((ATTN_BLOCK, GROUP_W),
                                 lambda n, c=c: (jnp.minimum(loc(n) * ATTN_QB + ATTN_QB, n_blocks - 1), c))
    return cur, prev, nxt


def _keys(kp_ref, k_ref, j):
    prev = kp_ref[...] if j == 0 else k_ref[(j - 1) * ATTN_BLOCK:j * ATTN_BLOCK, :]
    return jnp.concatenate([prev, k_ref[j * ATTN_BLOCK:(j + 1) * ATTN_BLOCK, :]], axis=0)


def attn_fwd(qkvs):
    s = qkvs[0].shape[0]
    n_blocks = s // ATTN_BLOCK
    ns = n_blocks // ATTN_QB
    n_in = 6

    def pattern(n, bps, q_ref, kp_ref, k_ref, vp_ref, v_ref, bias_ref, o_ref, lse_ref):
        head = _head_of_lane((ATTN_BLOCK, GROUP_W))
        for j in range(ATTN_QB):
            sl = slice(j * ATTN_BLOCK, (j + 1) * ATTN_BLOCK)
            first = (((n * ATTN_QB + j) % bps) == 0).astype(jnp.int32)
            qs = _stack_heads(q_ref[sl, :], head)
            sc = _dot_nt(qs, _keys(kp_ref, k_ref, j)) * ATTN_SCALE + bias_ref[first]
            m = jnp.max(sc, axis=-1, keepdims=True)
            pr = jnp.exp(sc - m)
            l = jnp.sum(pr, axis=-1, keepdims=True)
            oh = _dot(pr.astype(BF16), _keys(vp_ref, v_ref, j)) / l
            o_ref[sl, :] = _unstack_heads(oh, head)
            lse_ref[sl, :] = _unstack_heads(jnp.broadcast_to(m + jnp.log(l), oh.shape), head)

    def body(*refs):
        n = pl.program_id(0)
        for p, dil in enumerate(PATTERN_DILS):
            mine = refs[n_in * p:n_in * (p + 1)] + refs[n_in * N_PATTERNS + 2 * p:n_in * N_PATTERNS + 2 * p + 2]

            @pl.when(n // ns == p)
            def _(mine=mine, p=p, dil=dil):
                pattern(n - p * ns, n_blocks // dil, *mine)

    in_specs, operands, out_specs = [], [], []
    for p, (qkv, dil) in enumerate(zip(qkvs, PATTERN_DILS)):
        cur, prev, _ = _attn_specs(n_blocks, p)
        bias = _attn_bias(dil)
        in_specs += [cur(0), prev(1), cur(1), prev(2), cur(2), pl.BlockSpec(bias.shape, lambda n: (0, 0, 0))]
        operands += [qkv, qkv, qkv, qkv, qkv, bias]
        out_specs += [cur(0), cur(0)]
    out = jax.ShapeDtypeStruct((s, GROUP_W), F32)
    res = pl.pallas_call(
        body, name="attn_fwd", grid=(N_PATTERNS * ns,),
        in_specs=in_specs, out_specs=out_specs, out_shape=[out] * (2 * N_PATTERNS),
        compiler_params=_params(("arbitrary",)),
    )(*operands)
    return [(res[2 * p], res[2 * p + 1]) for p in range(N_PATTERNS)]


def attn_bwd(qkvs, dos, lses, dlts):
    s = qkvs[0].shape[0]
    n_blocks = s // ATTN_BLOCK
    ns = n_blocks // ATTN_QB
    rows = ATTN_QB * ATTN_BLOCK
    n_in = 13

    def pattern(n, bps, q_ref, qn_ref, kp_ref, k_ref, vp_ref, v_ref, do_ref, don_ref, lse_ref, lsen_ref, dl_ref,
                dln_ref, bias_ref, out_ref, dk_acc, dv_acc):
        head = _head_of_lane((ATTN_BLOCK, GROUP_W))
        dk_acc[...] = jnp.zeros_like(dk_acc)
        dv_acc[...] = jnp.zeros_like(dv_acc)

        def pair(qj, doj, lsej, dlj, kk, vv, bias, keep):
            qs = _stack_heads(qj, head)
            dos = _stack_heads(doj, head)
            sc = _dot_nt(qs, kk) * ATTN_SCALE + bias
            if keep is None:
                pr = jnp.exp(sc - _head_column(lsej))
            else:
                pr = jnp.exp(jnp.minimum(sc - _head_column(lsej), 0.0)) * keep
            dp = _dot_nt(dos, vv)
            ds = (pr * (dp - _head_column(dlj)) * ATTN_SCALE).astype(BF16)
            return ds, _dot_tn(ds, qs), _dot_tn(pr.astype(BF16), dos)

        for j in range(ATTN_QB):
            sl = slice(j * ATTN_BLOCK, (j + 1) * ATTN_BLOCK)
            first = (((n * ATTN_QB + j) % bps) == 0).astype(jnp.int32)
            kk = _keys(kp_ref, k_ref, j)
            ds, dks, dvs = pair(q_ref[sl, :], do_ref[sl, :], lse_ref[sl, :], dl_ref[sl, :],
                                kk, _keys(vp_ref, v_ref, j), bias_ref[first], None)
            out_ref[sl, 0:GROUP_W] = _unstack_heads(_dot(ds, kk), head)
            acc = slice(j * ATTN_BLOCK, (j + 2) * ATTN_BLOCK)
            dk_acc[acc, :] += dks
            dv_acc[acc, :] += dvs

        nxt = n * ATTN_QB + ATTN_QB
        valid = ((nxt < n_blocks) & ((nxt % bps) != 0)).astype(F32)
        last = slice((ATTN_QB - 1) * ATTN_BLOCK, ATTN_QB * ATTN_BLOCK)
        _, dks, dvs = pair(qn_ref[...], don_ref[...], lsen_ref[...], dln_ref[...], k_ref[last, :], v_ref[last, :],
                           bias_ref[0][:, :ATTN_BLOCK], valid)
        acc = slice(ATTN_QB * ATTN_BLOCK, (ATTN_QB + 1) * ATTN_BLOCK)
        dk_acc[acc, :] += dks
        dv_acc[acc, :] += dvs
        out_ref[:, GROUP_W:2 * GROUP_W] = dk_acc[ATTN_BLOCK:, :]
        out_ref[:, 2 * GROUP_W:3 * GROUP_W] = dv_acc[ATTN_BLOCK:, :]

    def body(*refs):
        n = pl.program_id(0)
        scratch = refs[-2:]
        for p, dil in enumerate(PATTERN_DILS):
            mine = refs[n_in * p:n_in * (p + 1)] + (refs[n_in * N_PATTERNS + p],) + scratch

            @pl.when(n // ns == p)
            def _(mine=mine, p=p, dil=dil):
                pattern(n - p * ns, n_blocks // dil, *mine)

    in_specs, operands, out_specs = [], [], []
    for p, (qkv, do, lse, dlt, dil) in enumerate(zip(qkvs, dos, lses, dlts, PATTERN_DILS)):
        cur, prev, nxt = _attn_specs(n_blocks, p)
        bias = _attn_bias(dil)
        in_specs += [cur(0), nxt(0), prev(1), cur(1), prev(2), cur(2), cur(0), nxt(0), cur(0), nxt(0), cur(0), nxt(0),
                     pl.BlockSpec(bias.shape, lambda n: (0, 0, 0))]
        operands += [qkv, qkv, qkv, qkv, qkv, qkv, do, do, lse, lse, dlt, dlt, bias]
        loc = lambda n, p=p: jnp.clip(n - p * ns, 0, ns - 1)
        out_specs.append(pl.BlockSpec((rows, 3 * GROUP_W), lambda n, loc=loc: (loc(n), 0)))
    return pl.pallas_call(
        body, name="attn_bwd", grid=(N_PATTERNS * ns,),
        in_specs=in_specs, out_specs=out_specs,
        out_shape=[jax.ShapeDtypeStruct((s, 3 * GROUP_W), F32)] * N_PATTERNS,
        scratch_shapes=[pltpu.VMEM(((ATTN_QB + 1) * ATTN_BLOCK, GROUP_W), F32),
                        pltpu.VMEM(((ATTN_QB + 1) * ATTN_BLOCK, GROUP_W), F32)],
        compiler_params=_params(("arbitrary",)),
    )(*operands)


def _zcol(c):
    return pl.BlockSpec((TILE_MIX, GROUP_W), lambda i, c=c: (i, c))


def _zhalo(c):
    per = TILE_MIX // SUBLANES
    return pl.BlockSpec((SUBLANES, GROUP_W), lambda i, c=c: (jnp.maximum(i * per - 1, 0), c))


def _full(shape):
    return pl.BlockSpec(shape, lambda i: tuple(0 for _ in shape))


def _of_layer(a, l):
    rest = a.shape[1:]
    return pl.BlockSpec((None,) + rest, lambda i: (l,) + tuple(0 for _ in rest))


def _softplus_neg(lam):
    nl = -lam
    return jnp.maximum(nl, 0.0) + jnp.log1p(jnp.exp(-jnp.abs(nl)))


def _lru_gates(xb, wa_ref, wx_ref, ba, bx, lam):
    xbb = xb.astype(BF16)
    r = jax.nn.sigmoid(_dot(xbb, wa_ref[...]) + ba)
    ig = jax.nn.sigmoid(_dot(xbb, wx_ref[...]) + bx)
    log_a = (-RG_C * r) * _softplus_neg(lam)
    a = jnp.exp(log_a)
    mult = jnp.sqrt(_neg_expm1(2.0 * log_a))
    return r, ig, a, mult


LRU_SAVED = 5


def _gmlp_spatial(ws_ref, vvb, head):
    outs = []
    for j in range(vvb.shape[0] // GMLP_CHUNK):
        blk = vvb[j * GMLP_CHUNK:(j + 1) * GMLP_CHUNK, :]
        acc = jnp.zeros((GMLP_CHUNK, GROUP_W), F32)
        for h in range(N_HEADS):
            acc = jnp.where(head[:GMLP_CHUNK] == h, _dot(ws_ref[h], blk), acc)
        outs.append(acc)
    return jnp.concatenate(outs, axis=0)


def mix_fwd(z, attn, wts, l):
    s = z.shape[0]
    d4, d16 = PATTERN_DILS[1], PATTERN_DILS[2]

    def body(ax_ref, ab_ref, ac_ref, ag_ref, rx_ref, rg_ref, cu_ref, cv_ref, cg_ref, dg_ref,
             axh_ref, ach_ref, rxh_ref, o1_ref, l1_ref, o4_ref, l4_ref, o16_ref, l16_ref,
             caw_ref, crw_ref, crb_ref, wa_ref, wx_ref, ba_ref, bx_ref, lam_ref, gng_ref, ws_ref, bs_ref,
             y_ref, hl_ref, o_ref, lse_ref, lse4_ref, lse16_ref, lru_ref, carry, *stage):
        st_a, st_b, st_c, st_d, st_e = (stage[2 * k:2 * k + 2] for k in range(5))
        i = pl.program_id(0)

        @pl.when(i == 0)
        def _():
            carry[...] = jnp.zeros_like(carry)

        nz = (i > 0).astype(F32)
        head = _head_of_lane((TILE_MIX, GROUP_W))

        pa = ac_ref[...] * ax_ref[...]
        pah = ach_ref[...] * axh_ref[...] * nz
        cv = caw_ref[2:3, :] * pa + caw_ref[1:2, :] * _shift_down(pa, pah, 1) + caw_ref[0:1, :] * _shift_down(pa, pah, 2)
        y_ref[:, 0:GROUP_W] = (ab_ref[...] * cv * _silu(ag_ref[...])).astype(BF16)

        rx = rx_ref[...]
        rxh = rxh_ref[...] * nz
        xb = (crw_ref[3:4, :] * rx + crw_ref[2:3, :] * _shift_down(rx, rxh, 1) + crw_ref[1:2, :] * _shift_down(rx, rxh, 2)
              + crw_ref[0:1, :] * _shift_down(rx, rxh, 3) + crb_ref[...])
        r, ig, a, mult = _lru_gates(xb, wa_ref, wx_ref, ba_ref[...], bx_ref[...], lam_ref[...])
        for k, val in enumerate((xb, r, ig, a, mult)):
            lru_ref[:, k * GROUP_W:(k + 1) * GROUP_W] = val
        hl = _scan_causal(a, mult * (ig * xb), carry[SUBLANES - 1:SUBLANES, :])
        hl_ref[...] = hl
        carry[...] = hl[TILE_MIX - SUBLANES:, :]
        y_ref[:, GROUP_W:2 * GROUP_W] = (hl * _silu(rg_ref[...])).astype(BF16)

        u = _gelu(cu_ref[...])
        gv = _gelu(cv_ref[...])
        rs = lax.rsqrt(jnp.mean(gv * gv, axis=-1, keepdims=True) + NORM_EPS)
        vvb = (gv * rs * gng_ref[...]).astype(BF16)
        sp = _gmlp_spatial(ws_ref, vvb, head) + jnp.concatenate([bs_ref[...]] * (TILE_MIX // GMLP_CHUNK), axis=0)
        y_ref[:, 2 * GROUP_W:3 * GROUP_W] = (u * sp * _silu(cg_ref[...])).astype(BF16)

        ops = (o1_ref[...], _interleave_load(o4_ref, d4, st_a), _interleave_load(o16_ref, d16, st_b))
        lps = (l1_ref[...], _interleave_load(l4_ref, d4, st_c), _interleave_load(l16_ref, d16, st_d))
        m = jnp.maximum(jnp.maximum(lps[0], lps[1]), lps[2])
        zsum = jnp.zeros_like(m)
        o = jnp.zeros_like(m)
        for op, lp in zip(ops, lps):
            w = jnp.exp(lp - m)
            zsum = zsum + w
            o = o + w * op
        o = o / zsum
        lse = m + jnp.log(zsum)
        o_ref[...] = o
        lse_ref[...] = lse
        _deinterleave_store(lse, st_e, ((d4, lse4_ref), (d16, lse16_ref)))
        y_ref[:, 3 * GROUP_W:4 * GROUP_W] = (o * _silu(dg_ref[...])).astype(BF16)

    row = pl.BlockSpec((TILE_MIX, GROUP_W), lambda i: (i, 0))
    r4 = _residue_block(d4, TILE_MIX, GROUP_W)
    r16 = _residue_block(d16, TILE_MIX, GROUP_W)
    names = ("caw", "crw", "crb", "wa", "wx", "ba", "bx", "lam", "gng", "ws", "bs")
    in_specs = ([_zcol(c) for c in (C_AX, C_AB, C_AC, C_AG, C_RX, C_RG, C_CU, C_CV, C_CG, C_DG)]
                + [_zhalo(C_AX), _zhalo(C_AC), _zhalo(C_RX), row, row, r4, r4, r16, r16]
                + [_of_layer(wts[k], l) for k in names])
    return pl.pallas_call(
        body, name="mix_fwd", grid=(s // TILE_MIX,),
        in_specs=in_specs,
        out_specs=[pl.BlockSpec((TILE_MIX, D_MIX), lambda i: (i, 0)), row, row, row, r4, r16,
                   pl.BlockSpec((TILE_MIX, LRU_SAVED * GROUP_W), lambda i: (i, 0))],
        out_shape=([jax.ShapeDtypeStruct((s, D_MIX), BF16)] + [jax.ShapeDtypeStruct((s, GROUP_W), F32)] * 3
                   + [_by_residue(s, d4, GROUP_W, F32), _by_residue(s, d16, GROUP_W, F32),
                      jax.ShapeDtypeStruct((s, LRU_SAVED * GROUP_W), F32)]),
        scratch_shapes=[pltpu.VMEM((SUBLANES, GROUP_W), F32)] + _stage_scratch(TILE_MIX, GROUP_W, 5),
        compiler_params=_params(("arbitrary",)),
    )(*([z] * 13), *[a for pair in attn for a in pair], *[wts[k] for k in names])


def mix_bwd(dy, z, hl, lru, dqkv, ddg, wts, l):
    s = z.shape[0]
    d4, d16 = PATTERN_DILS[1], PATTERN_DILS[2]
    n_tiles = s // TILE_MIX

    def body(dya_ref, dyb_ref, dyc_ref, ax_ref, ab_ref, ac_ref, ag_ref, rx_ref, rg_ref, cu_ref, cv_ref, cg_ref,
             axh_ref, ach_ref, rxh_ref, hl_ref, hlh_ref, lru_ref, dqkv1_ref, dqkv4_ref, dqkv16_ref, ddg_ref,
             caw_ref, crw_ref, crb_ref, wa_ref, wx_ref, ba_ref, bx_ref, lam_ref, gng_ref, ws_ref, wst_ref, bs_ref,
             dz_ref, ga_ref, gr_ref, gn_ref, gwa_ref, gwx_ref, gws_ref, gbs_ref,
             c_dcv, c_g, c_a, c_dxb, *stage):
        st_a, st_b = stage[:len(stage) // 2], stage[len(stage) // 2:]
        step = pl.program_id(0)
        i = n_tiles - 1 - step

        @pl.when(step == 0)
        def _():
            for r in (c_dcv, c_g, c_a, c_dxb, ga_ref, gr_ref, gn_ref, gwa_ref, gwx_ref, gws_ref, gbs_ref):
                r[...] = jnp.zeros_like(r)

        nz = (i > 0).astype(F32)
        head = _head_of_lane((TILE_MIX, GROUP_W))
        shp8 = (SUBLANES, GROUP_W)
        colsum = lambda v: jnp.sum(v, axis=0, keepdims=True)

        ax, ab, ac, ag = ax_ref[...], ab_ref[...], ac_ref[...], ag_ref[...]
        dya = dya_ref[...]
        pa = ac * ax
        pah = ach_ref[...] * axh_ref[...] * nz
        pa1 = _shift_down(pa, pah, 1)
        pa2 = _shift_down(pa, pah, 2)
        cv = caw_ref[2:3, :] * pa + caw_ref[1:2, :] * pa1 + caw_ref[0:1, :] * pa2
        sg = _silu(ag)
        dz_ref[:, C_AB * GROUP_W:(C_AB + 1) * GROUP_W] = (dya * cv * sg).astype(BF16)
        dz_ref[:, C_AG * GROUP_W:(C_AG + 1) * GROUP_W] = (dya * ab * cv * _dsilu(ag)).astype(BF16)
        dcv = dya * ab * sg
        nxt = c_dcv[...]
        dpa = caw_ref[2:3, :] * dcv + caw_ref[1:2, :] * _shift_up(dcv, nxt, 1) + caw_ref[0:1, :] * _shift_up(dcv, nxt, 2)
        c_dcv[...] = dcv[:SUBLANES, :]
        dz_ref[:, C_AC * GROUP_W:(C_AC + 1) * GROUP_W] = (dpa * ax).astype(BF16)
        dz_ref[:, C_AX * GROUP_W:(C_AX + 1) * GROUP_W] = (dpa * ac).astype(BF16)
        ga_ref[...] += (_put_row(shp8, 2, colsum(dcv * pa)) + _put_row(shp8, 1, colsum(dcv * pa1))
                        + _put_row(shp8, 0, colsum(dcv * pa2)))

        rx, rg = rx_ref[...], rg_ref[...]
        dyb = dyb_ref[...]
        rxh = rxh_ref[...] * nz
        rx1, rx2, rx3 = _shift_down(rx, rxh, 1), _shift_down(rx, rxh, 2), _shift_down(rx, rxh, 3)
        xb, r, ig, a, mult = (lru_ref[:, k * GROUP_W:(k + 1) * GROUP_W] for k in range(LRU_SAVED))
        lam = lam_ref[...]
        sp = _softplus_neg(lam)
        hl = hl_ref[...]
        hprev = _shift_down(hl, hlh_ref[...] * nz, 1)
        dz_ref[:, C_RG * GROUP_W:(C_RG + 1) * GROUP_W] = (dyb * hl * _dsilu(rg)).astype(BF16)
        dh = dyb * _silu(rg)
        a_next = _shift_up(a, c_a[...], 1)
        g = _scan_anticausal(a_next, dh, c_g[0:1, :])
        c_g[...] = g[:SUBLANES, :]
        c_a[...] = a[:SUBLANES, :]
        u = ig * xb
        da = g * hprev
        dmult = g * u
        du = g * mult
        dlog_a = da * a - dmult * (a * a) / mult
        dr = dlog_a * (-RG_C * sp)
        dga = dr * r * (1.0 - r)
        dgx = (du * xb) * ig * (1.0 - ig)
        dgab, dgxb = dga.astype(BF16), dgx.astype(BF16)
        dxb = du * ig + _dot_nt(dgab, wa_ref[...]) + _dot_nt(dgxb, wx_ref[...])
        xbb = xb.astype(BF16)
        gwa_ref[...] += _dot_tn(xbb, dgab)
        gwx_ref[...] += _dot_tn(xbb, dgxb)
        nxt = c_dxb[...]
        drx = (crw_ref[3:4, :] * dxb + crw_ref[2:3, :] * _shift_up(dxb, nxt, 1) + crw_ref[1:2, :] * _shift_up(dxb, nxt, 2)
               + crw_ref[0:1, :] * _shift_up(dxb, nxt, 3))
        c_dxb[...] = dxb[:SUBLANES, :]
        dz_ref[:, C_RX * GROUP_W:(C_RX + 1) * GROUP_W] = drx.astype(BF16)
        dlam = colsum(dlog_a * (-RG_C * r)) * (-jax.nn.sigmoid(-lam))
        gr_ref[...] += (_put_row(shp8, 3, colsum(dxb * rx)) + _put_row(shp8, 2, colsum(dxb * rx1))
                        + _put_row(shp8, 1, colsum(dxb * rx2)) + _put_row(shp8, 0, colsum(dxb * rx3))
                        + _put_row(shp8, 4, colsum(dxb)) + _put_row(shp8, 5, colsum(dga))
                        + _put_row(shp8, 6, colsum(dgx)) + _put_row(shp8, 7, dlam))

        cu, cvv, cg = cu_ref[...], cv_ref[...], cg_ref[...]
        dyc = dyc_ref[...]
        u_c, du_c = _gelu_and_grad(cu)
        gv, dgv_c = _gelu_and_grad(cvv)
        rs = lax.rsqrt(jnp.mean(gv * gv, axis=-1, keepdims=True) + NORM_EPS)
        vh = gv * rs
        gng = gng_ref[...]
        vvb = (vh * gng).astype(BF16)
        spat = _gmlp_spatial(ws_ref, vvb, head) + jnp.concatenate([bs_ref[...]] * (TILE_MIX // GMLP_CHUNK), axis=0)
        sgc = _silu(cg)
        dz_ref[:, C_CU * GROUP_W:(C_CU + 1) * GROUP_W] = (dyc * spat * sgc * du_c).astype(BF16)
        dz_ref[:, C_CG * GROUP_W:(C_CG + 1) * GROUP_W] = (dyc * u_c * spat * _dsilu(cg)).astype(BF16)
        dsp = dyc * u_c * sgc
        dspb = dsp.astype(BF16)
        tril = (lax.broadcasted_iota(jnp.int32, (GMLP_CHUNK, GMLP_CHUNK), 0)
                >= lax.broadcasted_iota(jnp.int32, (GMLP_CHUNK, GMLP_CHUNK), 1))
        head_c = head[:GMLP_CHUNK]
        dvv_parts = []
        gbs = jnp.zeros((GMLP_CHUNK, GROUP_W), F32)
        for j in range(TILE_MIX // GMLP_CHUNK):
            sl = slice(j * GMLP_CHUNK, (j + 1) * GMLP_CHUNK)
            dblk = dspb[sl, :]
            vblk = vvb[sl, :]
            gbs = gbs + dsp[sl, :]
            acc = jnp.zeros((GMLP_CHUNK, GROUP_W), F32)
            for h in range(N_HEADS):
                acc = jnp.where(head_c == h, _dot(wst_ref[h], dblk), acc)
                dm = jnp.where(head_c == h, dblk, jnp.zeros_like(dblk))
                gws_ref[h] += jnp.where(tril, _dot_nt(dm, vblk), 0.0)
            dvv_parts.append(acc)
        gbs_ref[...] += gbs
        dvv = jnp.concatenate(dvv_parts, axis=0)
        gn_ref[...] += _put_row(shp8, 0, colsum(dvv * vh))
        dvh = dvv * gng
        dgv = rs * (dvh - vh * jnp.mean(dvh * vh, axis=-1, keepdims=True))
        dz_ref[:, C_CV * GROUP_W:(C_CV + 1) * GROUP_W] = (dgv * dgv_c).astype(BF16)

        dsum = dqkv1_ref[...] + _interleave_load(dqkv4_ref, d4, st_a) + _interleave_load(dqkv16_ref, d16, st_b)
        dz_ref[:, C_DQ * GROUP_W:(C_DV + 1) * GROUP_W] = dsum.astype(BF16)
        dz_ref[:, C_DG * GROUP_W:(C_DG + 1) * GROUP_W] = ddg_ref[...].astype(BF16)

    per = TILE_MIX // SUBLANES
    qkv_w = 3 * GROUP_W
    rev = lambda c: pl.BlockSpec((TILE_MIX, GROUP_W), lambda t, c=c: (n_tiles - 1 - t, c))
    revh = lambda c: pl.BlockSpec((SUBLANES, GROUP_W),
                                  lambda t, c=c: (jnp.maximum((n_tiles - 1 - t) * per - 1, 0), c))
    revr = lambda dil: pl.BlockSpec((dil, TILE_MIX // dil, qkv_w), lambda t: (0, n_tiles - 1 - t, 0))
    names = ("caw", "crw", "crb", "wa", "wx", "ba", "bx", "lam", "gng", "ws", "wst", "bs")
    in_specs = ([rev(0), rev(1), rev(2)]
                + [rev(c) for c in (C_AX, C_AB, C_AC, C_AG, C_RX, C_RG, C_CU, C_CV, C_CG)]
                + [revh(C_AX), revh(C_AC), revh(C_RX), rev(0), revh(0),
                   pl.BlockSpec((TILE_MIX, LRU_SAVED * GROUP_W), lambda t: (n_tiles - 1 - t, 0)),
                   pl.BlockSpec((TILE_MIX, qkv_w), lambda t: (n_tiles - 1 - t, 0)), revr(d4), revr(d16), rev(0)]
                + [_of_layer(wts[k], l) for k in names])
    small = jax.ShapeDtypeStruct((SUBLANES, GROUP_W), F32)
    sq = jax.ShapeDtypeStruct((GROUP_W, GROUP_W), F32)
    out_shape = [jax.ShapeDtypeStruct((s, D_IN), BF16), small, small, small, sq, sq,
                 jax.ShapeDtypeStruct((N_HEADS, GMLP_CHUNK, GMLP_CHUNK), F32),
                 jax.ShapeDtypeStruct((GMLP_CHUNK, GROUP_W), F32)]
    out_specs = ([pl.BlockSpec((TILE_MIX, D_IN), lambda t: (n_tiles - 1 - t, 0))]
                 + [_full(o.shape) for o in out_shape[1:]])
    return pl.pallas_call(
        body, name="mix_bwd", grid=(n_tiles,),
        in_specs=in_specs, out_specs=out_specs, out_shape=out_shape,
        scratch_shapes=[pltpu.VMEM((SUBLANES, GROUP_W), F32)] * 4 + _stage_scratch(TILE_MIX, qkv_w, 2),
        compiler_params=_params(("arbitrary",)),
    )(dy, dy, dy, *([z] * 12), hl, hl, lru, *dqkv, ddg, *[wts[k] for k in names])


def prepare_small_weights(p):
    tril = jnp.tril(jnp.ones((GMLP_CHUNK, GMLP_CHUNK), dtype=bool))
    ws = jnp.where(tril, p["gmlp_ws"], 0.0).astype(BF16)
    row = lambda a: a[:, None, :]
    eye = jnp.eye(N_HEADS, dtype=F32)
    bd = lambda w: (w[:, :, :, None, :] * eye[None, :, None, :, None]).reshape(-1, GROUP_W, GROUP_W).astype(BF16)
    return dict(
        caw=p["conv_a_w"], crw=p["conv_r_w"], crb=row(p["conv_r_b"]),
        wa=bd(p["lru_wa"]), wx=bd(p["lru_wx"]),
        ba=row(p["lru_ba"]), bx=row(p["lru_bx"]), lam=row(p["lru_lambda"]), gng=row(p["gmlp_norm_g"]),
        ws=ws, wst=jnp.swapaxes(ws, 2, 3),
        bs=jnp.repeat(jnp.swapaxes(p["gmlp_bs"], 1, 2), HEAD_DIM, axis=2))


def _flat(a):
    return a.reshape(a.shape[0] * a.shape[1], a.shape[2])


def _split(a, dil):
    return a.reshape(dil, a.shape[0] // dil, a.shape[1])


def local_step(x, tgt, final_g, depth, chip, layer_weights, projections_done):
    saved = []
    for l in range(depth):
        gain, w_in_l, rest = layer_weights(l, x)
        z, h, *qkvs = in_fwd(x, gain, w_in_l)
        w_out_l, wts = rest(z)
        qkvs = [_flat(q) if q.ndim == 3 else q for q in qkvs]
        attn = [(o_p, lse_p) if d == 1 else (_split(o_p, d), _split(lse_p, d))
                for (o_p, lse_p), d in zip(attn_fwd(qkvs), PATTERN_DILS)]
        y, hl, o, lse, lse4, lse16, lru = mix_fwd(z, attn, wts, l)
        saved.append(dict(x=x, z=z, h=h, y=y, hl=hl, o=o, qkvs=qkvs, lses=(lse, _flat(lse4), _flat(lse16)), wts=wts, lru=lru,
                          gain=gain, w_in=w_in_l, w_out=w_out_l))
        x = out_fwd(y, w_out_l, x)

    loss, dx, dfg = loss_head(x, final_g[None, :], tgt)
    raw = {k: [None] * depth for k in ("gain", "a", "r", "n", "wa", "wx", "ws", "bs")}
    zero = None
    for l in reversed(range(depth)):
        sv = saved[l]
        dy, ddg, do1, do4, do16, dl1, dl4, dl16 = out_bwd(dx, sv["w_out"], sv["z"], sv["o"])
        g_w_out = matmul_tn(sv["y"], dx, 1)
        dqkv = attn_bwd(sv["qkvs"], (do1, _flat(do4), _flat(do16)), sv["lses"], (dl1, _flat(dl4), _flat(dl16)))
        dqkv = [g if d == 1 else _split(g, d) for g, d in zip(dqkv, PATTERN_DILS)]
        dz, ga, gr, gn, gwa, gwx, gws, gbs = mix_bwd(dy, sv["z"], sv["hl"], sv["lru"], dqkv, ddg, sv["wts"], l)
        gain = sv["gain"] if zero is None else sv["gain"] + zero
        zero = projections_done(l, *grad_w_in(sv["h"], dz, chip), g_w_out)
        if l == 0 and zero is not None:
            gain = gain + zero
        dx, dgn = in_bwd(dz, sv["w_in"], sv["x"], gain, dx)
        for k, g in zip(("gain", "a", "r", "n", "wa", "wx", "ws", "bs"), (dgn, ga, gr, gn, gwa, gwx, gws, gbs)):
            raw[k][l] = g
    st = {k: jnp.stack(v) for k, v in raw.items()}
    eye = jnp.eye(N_HEADS, dtype=F32)[None, :, None, :, None]
    diag = lambda g: (g.reshape(depth, N_HEADS, HEAD_DIM, N_HEADS, HEAD_DIM) * eye).sum(axis=3)
    grads = dict(
        norm_g=st["gain"][:, 0], conv_a_w=st["a"][:, :3], conv_r_w=st["r"][:, :4], conv_r_b=st["r"][:, 4],
        lru_ba=st["r"][:, 5], lru_bx=st["r"][:, 6], lru_lambda=st["r"][:, 7], gmlp_norm_g=st["n"][:, 0],
        lru_wa=diag(st["wa"]), lru_wx=diag(st["wx"]), gmlp_ws=st["ws"],
        gmlp_bs=jnp.swapaxes(st["bs"].reshape(depth, GMLP_CHUNK, N_HEADS, HEAD_DIM).sum(-1), 1, 2),
        final_g=dfg[0])
    return loss, dx, grads


MESH = pl.DeviceIdType.MESH
N_CHIPS = 4
N_DEV = 8
ANY = pl.BlockSpec(memory_space=pl.ANY)


def _place():
    x, y, c = lax.axis_index("x"), lax.axis_index("y"), lax.axis_index("c")
    chips = [(1 - x, y), (x, 1 - y), (1 - x, 1 - y)]
    return x, y, c, chips


def _remote(src, dst, ssem, rsem, to):
    return pltpu.make_async_remote_copy(src_ref=src, dst_ref=dst, send_sem=ssem, recv_sem=rsem,
                                        device_id=to, device_id_type=MESH)


HBM = pl.BlockSpec(memory_space=pltpu.HBM)
SEM = pl.BlockSpec(memory_space=pltpu.SEMAPHORE)
DATAFLOW = pltpu.SideEffectType.DATAFLOW_SIDE_EFFECTING
GATHER, SCATTER = "gather", "scatter"


def _chip_copies(mode, src_refs, land_refs, ssem, rsem):
    x, y, c, chips = _place()
    me = 2 * x + y
    n = len(src_refs)
    copies = []
    for k, (cx, cy) in enumerate(chips):
        for a in range(n):
            if mode == GATHER:
                src, dst = src_refs[a], land_refs[a].at[me]
            else:
                src, dst = src_refs[a].at[2 * cx + cy], land_refs[a].at[k]
            copies.append(_remote(src, dst, ssem.at[n * k + a], rsem.at[n * k + a], (cx, cy, c)))
    return copies


def exchange_start(mode, srcs, after, name):
    n = len(srcs)
    if mode == GATHER:
        lands = [lax.empty((N_CHIPS,) + s.shape, s.dtype) for s in srcs]
    else:
        lands = [lax.empty((N_CHIPS - 1,) + s.shape[1:], s.dtype) for s in srcs]
    extra = [] if after is None else [after]

    def body(*refs):
        src_refs, land_refs = refs[:n], refs[n:2 * n]
        ssem, rsem = refs[2 * n + len(extra)], refs[2 * n + len(extra) + 1]
        token = refs[-1]
        for cp in _chip_copies(mode, src_refs, land_refs, ssem, rsem):
            cp.start()
        token[...] = jnp.zeros_like(token)

    arrays = list(srcs) + lands
    return pl.pallas_call(
        body, name=name,
        out_shape=(pltpu.SemaphoreType.DMA((3 * n,)), pltpu.SemaphoreType.DMA((3 * n,)),
                   *[pltpu.HBM(a.shape, a.dtype) for a in arrays], jax.ShapeDtypeStruct((SUBLANES, LANES), F32)),
        in_specs=[HBM] * (2 * n) + [ANY] * len(extra),
        out_specs=(SEM, SEM, *[HBM] * (2 * n), pl.BlockSpec(memory_space=pltpu.VMEM)),
        input_output_aliases={i: 2 + i for i in range(2 * n)},
        compiler_params=pltpu.CompilerParams(has_side_effects=DATAFLOW),
    )(*[pltpu.with_memory_space_constraint(a, pltpu.HBM) for a in arrays], *extra)


def exchange_wait(mode, started, after, name):
    ssem, rsem, *thru, _ = started
    n = len(thru) // 2

    def body(*refs):
        src_refs, land_refs = refs[:n], refs[n:2 * n]
        ssem_ref, rsem_ref = refs[2 * n], refs[2 * n + 1]
        for cp in _chip_copies(mode, src_refs, land_refs, ssem_ref, rsem_ref):
            cp.wait_send()
            cp.wait_recv()

    outs = pl.pallas_call(
        body, name=name,
        out_shape=[pltpu.HBM(a.shape, a.dtype) for a in thru],
        in_specs=[HBM] * (2 * n) + [SEM, SEM, ANY],
        out_specs=[HBM] * (2 * n),
        input_output_aliases={i: i for i in range(2 * n)},
        compiler_params=pltpu.CompilerParams(has_side_effects=DATAFLOW),
    )(*thru, ssem, rsem, after)
    return outs[n:]


def sibling_exchange(p1, p2):
    def body(p1_ref, p2_ref, q1_ref, q2_ref, ssem, rsem):
        x, y, c, _ = _place()
        copies = [_remote(p_ref, q_ref, ssem.at[a], rsem.at[a], (x, y, 1 - c))
                  for a, (p_ref, q_ref) in enumerate(((p1_ref, q1_ref), (p2_ref, q2_ref)))]
        for cp in copies:
            cp.start()
        for cp in copies:
            cp.wait()

    return pl.pallas_call(
        body, name="sibling_exchange",
        in_specs=[ANY, ANY], out_specs=[ANY, ANY],
        out_shape=[jax.ShapeDtypeStruct(p.shape, p.dtype) for p in (p1, p2)],
        scratch_shapes=[pltpu.SemaphoreType.DMA((2,)), pltpu.SemaphoreType.DMA((2,))],
    )(p1, p2)


def all_reduce_small(v):
    r, n = v.shape
    piece = r // N_DEV

    def body(x_ref, out_ref, recv, ssem1, rsem1, ssem2, rsem2):
        x, y, c, _ = _place()
        me = 4 * x + 2 * y + c

        def peer(k):
            px = 1 - x if (k >> 2) & 1 else x
            py = 1 - y if (k >> 1) & 1 else y
            pc = 1 - c if k & 1 else c
            return (px, py, pc), 4 * px + 2 * py + pc

        def rows(ref, d):
            return ref.at[pl.ds(d * piece, piece), :]

        scatter = []
        for k in range(1, N_DEV):
            to, idx = peer(k)
            scatter.append(_remote(rows(x_ref, idx), recv.at[k], ssem1.at[k - 1], rsem1.at[k - 1], to))
            scatter[-1].start()
        acc = rows(x_ref, me)[...]
        for k in range(1, N_DEV):
            scatter[k - 1].wait_recv()
            acc = acc + recv[k]
        rows(out_ref, me)[...] = acc

        gather = []
        for k in range(1, N_DEV):
            to, _ = peer(k)
            gather.append(_remote(rows(out_ref, me), rows(out_ref, me), ssem2.at[k - 1], rsem2.at[k - 1], to))
            gather[-1].start()
        for k in range(1, N_DEV):
            to, idx = peer(k)
            _remote(rows(out_ref, idx), rows(out_ref, idx), ssem2.at[k - 1], rsem2.at[k - 1], to).wait_recv()
        for cp in scatter + gather:
            cp.wait_send()

    return pl.pallas_call(
        body, name="all_reduce_small",
        out_shape=jax.ShapeDtypeStruct((r, n), v.dtype),
        in_specs=[pl.BlockSpec(memory_space=pltpu.VMEM)],
        out_specs=pl.BlockSpec(memory_space=pltpu.VMEM),
        scratch_shapes=[pltpu.VMEM((N_DEV, piece, n), v.dtype)] + [pltpu.SemaphoreType.DMA((N_DEV - 1,))] * 4,
        compiler_params=pltpu.CompilerParams(vmem_limit_bytes=VMEM_LIMIT),
    )(v)


TILE_ROWS = 256


def sum_partials(own, parts):
    k, r, c = parts.shape

    def body(o_ref, p_ref, out_ref):
        acc = o_ref[...]
        for i in range(k):
            acc = acc + p_ref[i].astype(F32)
        out_ref[...] = acc

    row = pl.BlockSpec((TILE_ROWS, c), lambda i: (i, 0))
    return pl.pallas_call(
        body, name="sum_partials", grid=(r // TILE_ROWS,),
        in_specs=[row, pl.BlockSpec((k, TILE_ROWS, c), lambda i: (0, i, 0))],
        out_specs=row,
        out_shape=jax.ShapeDtypeStruct((r, c), F32),
        compiler_params=_params(("parallel",)),
    )(own, parts)


def _adamw_update(w, g, m, v):
    m2 = ADAM_B1 * m + (1.0 - ADAM_B1) * g
    v2 = ADAM_B2 * v + (1.0 - ADAM_B2) * (g * g)
    m_hat = m2 / (1.0 - ADAM_B1 ** ADAM_STEP)
    v_hat = v2 / (1.0 - ADAM_B2 ** ADAM_STEP)
    return -ADAM_LR * (m_hat / (jnp.sqrt(v_hat) + ADAM_EPS) + ADAM_WD * w), m2, v2


def adamw_small(ws, gs, ms, vs):
    n = len(ws)

    def body(*refs):
        ins, outs = refs[:4 * n], refs[4 * n:]
        for i in range(n):
            d, m2, v2 = _adamw_update(ins[i][...], ins[n + i][...], ins[2 * n + i][...], ins[3 * n + i][...])
            outs[3 * i][...] = d
            outs[3 * i + 1][...] = m2
            outs[3 * i + 2][...] = v2

    outs = pl.pallas_call(
        body, name="adamw_small",
        out_shape=[jax.ShapeDtypeStruct(w.shape, F32) for w in ws for _ in range(3)],
    )(*ws, *gs, *ms, *vs)
    return [tuple(outs[3 * i:3 * i + 3]) for i in range(n)]


def adamw(w, ga, gb, m, v):
    n, r, c = w.shape
    tile = max(t for t in range(SUBLANES, TILE_ROWS + 1, SUBLANES) if r % t == 0)

    def body(w_ref, ga_ref, gb_ref, m_ref, v_ref, g_ref, d_ref, m2_ref, v2_ref):
        g = ga_ref[...] + gb_ref[...]
        g_ref[...] = g
        d_ref[...], m2_ref[...], v2_ref[...] = _adamw_update(w_ref[...], g, m_ref[...], v_ref[...])

    spec = pl.BlockSpec((1, tile, c), lambda j, i: (j, i, 0))
    return pl.pallas_call(
        body, name="adamw", grid=(n, r // tile),
        in_specs=[spec] * 5, out_specs=[spec] * 4,
        out_shape=[jax.ShapeDtypeStruct((n, r, c), F32)] * 4,
        compiler_params=_params(("parallel", "parallel")),
    )(w, ga, gb, m, v)


REPLICATED = ("norm_g", "conv_r_b", "lru_wa", "lru_ba", "lru_wx", "lru_bx", "lru_lambda", "gmlp_norm_g",
              "gmlp_ws", "gmlp_bs", "final_g")
CHIP_SHARDED_SMALL = ("conv_a_w", "conv_r_w")
PACK_LANES = 128


def _pack(arrays):
    flat = jnp.concatenate([a.reshape(-1) for a in arrays])
    pad = (-flat.shape[0]) % (TILE_ROWS * PACK_LANES)
    return jnp.pad(flat, (0, pad)).reshape(-1, PACK_LANES)


def _unpack(packed, shapes):
    flat = packed.reshape(-1)
    out, off = [], 0
    for shp in shapes:
        n = math.prod(shp)
        out.append(flat[off:off + n].reshape(shp))
        off += n
    return out


def kernel(x, norm_g, w_in, conv_a_w, conv_r_w, conv_r_b, lru_wa, lru_ba, lru_wx, lru_bx, lru_lambda, gmlp_norm_g, gmlp_ws, gmlp_bs, w_out, final_g, loss_target, m_norm_g, m_w_in, m_conv_a_w, m_conv_r_w, m_conv_r_b, m_lru_wa, m_lru_ba, m_lru_wx, m_lru_bx, m_lru_lambda, m_gmlp_norm_g, m_gmlp_ws, m_gmlp_bs, m_w_out, m_final_g, v_norm_g, v_w_in, v_conv_a_w, v_conv_r_w, v_conv_r_b, v_lru_wa, v_lru_ba, v_lru_wx, v_lru_bx, v_lru_lambda, v_gmlp_norm_g, v_gmlp_ws, v_gmlp_bs, v_w_out, v_final_g):
    names = ("norm_g", "w_in", "conv_a_w", "conv_r_w", "conv_r_b", "lru_wa", "lru_ba", "lru_wx", "lru_bx",
             "lru_lambda", "gmlp_norm_g", "gmlp_ws", "gmlp_bs", "w_out", "final_g")
    w = dict(zip(names, (norm_g, w_in, conv_a_w, conv_r_w, conv_r_b, lru_wa, lru_ba, lru_wx, lru_bx, lru_lambda,
                         gmlp_norm_g, gmlp_ws, gmlp_bs, w_out, final_g)))
    m = dict(zip(names, (m_norm_g, m_w_in, m_conv_a_w, m_conv_r_w, m_conv_r_b, m_lru_wa, m_lru_ba, m_lru_wx, m_lru_bx,
                         m_lru_lambda, m_gmlp_norm_g, m_gmlp_ws, m_gmlp_bs, m_w_out, m_final_g)))
    v = dict(zip(names, (v_norm_g, v_w_in, v_conv_a_w, v_conv_r_w, v_conv_r_b, v_lru_wa, v_lru_ba, v_lru_wx, v_lru_bx,
                         v_lru_lambda, v_gmlp_norm_g, v_gmlp_ws, v_gmlp_bs, v_w_out, v_final_g)))
    depth, _, in_cols = w_in.shape
    out_rows = w_out.shape[1]
    conv_ch = conv_a_w.shape[2]
    chip = 2 * lax.axis_index("x") + lax.axis_index("y")

    taps = conv_a_w.shape[1] + conv_r_w.shape[1]
    w_in_t, m_w_in_t, v_w_in_t = (jnp.swapaxes(a, 1, 2) for a in (w_in, m_w_in, v_w_in))
    w_in_h, w_out_h = w_in_t.astype(BF16), w_out.astype(BF16)
    conv_own = jnp.concatenate([conv_a_w, conv_r_w], axis=1).reshape(depth * taps, conv_ch)
    gathers, token = [], None
    for l in range(depth):
        groups = [[w_in_h[l]], [w_out_h[l], conv_own]] if l == 0 else [[w_in_h[l], w_out_h[l]]]
        gathers.append([])
        for i, srcs in enumerate(groups):
            gathers[l].append(exchange_start(GATHER, srcs, token, f"gather_start_{l}_{i}"))
            token = gathers[l][-1][-1]
    p = dict(w)

    def with_own(land, own):
        return lax.dynamic_update_slice(land, own[None], (chip,) + (0,) * own.ndim)

    def layer_weights(l, x_l):
        lands = list(exchange_wait(GATHER, gathers[l][0], x_l, f"gather_wait_{l}_0"))
        w_in_l = with_own(lands[0], w_in_h[l]).reshape(D_IN, D_MODEL)
        gain = norm_g[l][None, :]
        if l == 0:
            gain = gain + token[0, 0]

        def rest(z_l):
            if l == 0:
                lands.extend(exchange_wait(GATHER, gathers[l][1], z_l, f"gather_wait_{l}_1"))
                conv = with_own(lands[2], conv_own).reshape(N_CHIPS, depth, taps, conv_ch)
                conv = conv.transpose(1, 2, 0, 3).reshape(depth, taps, GROUP_W)
                p["conv_a_w"] = conv[:, :conv_a_w.shape[1]]
                p["conv_r_w"] = conv[:, conv_a_w.shape[1]:]
                p["prepared"] = prepare_small_weights(p)
            return with_own(lands[1], w_out_h[l]).reshape(D_MIX, D_MODEL), p["prepared"]

        return gain, w_in_l, rest

    scatters, owns = [None] * depth, [None] * depth

    def projections_done(l, g_w_in_by_chip, g_w_in_own, g_w_out):
        go = g_w_out.reshape(N_CHIPS, out_rows, D_MODEL)
        owns[l] = (g_w_in_own, lax.dynamic_index_in_dim(go, chip, axis=0, keepdims=False))
        scatters[l] = exchange_start(SCATTER, [g_w_in_by_chip, go.astype(BF16)], None, f"scatter_start_{l}")
        return scatters[l][-1][0, 0]

    loss8, dx, grads = local_step(x[0], loss_target[0], final_g, depth, chip.reshape(1), layer_weights,
                                  projections_done)
    loss = lax.psum(loss8[0, 0], ("x", "y", "c"))

    parts = [exchange_wait(SCATTER, scatters[l], dx, f"scatter_wait_{l}") for l in range(depth)]
    r1 = jnp.concatenate([parts[l][0] for l in range(depth)], axis=1)
    r2 = jnp.concatenate([parts[l][1] for l in range(depth)], axis=1)
    own1 = jnp.concatenate([owns[l][0] for l in range(depth)], axis=0)
    own2 = jnp.concatenate([owns[l][1] for l in range(depth)], axis=0)
    p1 = sum_partials(own1, r1)
    p2 = sum_partials(own2, r2)
    q1, q2 = sibling_exchange(p1, p2)
    res = {}
    res["w_in"] = [jnp.swapaxes(t, 1, 2) for t in
                   adamw(w_in_t, p1.reshape(w_in_t.shape), q1.reshape(w_in_t.shape), m_w_in_t, v_w_in_t)]
    res["w_out"] = adamw(w_out, p2.reshape(w_out.shape), q2.reshape(w_out.shape), m_w_out, v_w_out)

    small = REPLICATED + CHIP_SHARDED_SMALL
    packed = _pack([grads[k] for k in small])
    total = all_reduce_small(packed)
    gs = dict(zip(small, _unpack(total, [grads[k].shape for k in small])))
    for k in CHIP_SHARDED_SMALL:
        gs[k] = lax.dynamic_slice_in_dim(gs[k], chip * conv_ch, conv_ch, axis=2)
    as2d = lambda a: a[None] if a.ndim == 1 else a
    outs = adamw_small(*[[as2d(d[k]) for k in small] for d in (w, gs, m, v)])
    for k, (delta, m2, v2) in zip(small, outs):
        res[k] = [t.reshape(w[k].shape) for t in (gs[k], delta, m2, v2)]

    return (loss, dx[None], *[res[k][0] for k in names], *[res[k][1] for k in names],
            *[res[k][2] for k in names], *[res[k][3] for k in names])
```

```python
import functools
import math

import jax
import jax.numpy as jnp
import numpy as np
from jax import lax
from jax.experimental import pallas as pl
from jax.experimental.pallas import tpu as pltpu

F32 = jnp.float32
BF16 = jnp.bfloat16

D_MODEL = 1024
GROUP_W = 256
N_HEADS = 4
HEAD_DIM = 64
N_CHUNKS = 13
D_IN = N_CHUNKS * GROUP_W
D_MIX = 4 * GROUP_W
NORM_EPS = 1e-6
RG_C = 8.0
GMLP_CHUNK = 128
ATTN_BLOCK = 128
PATTERN_DILS = (1, 4, 16)
N_PATTERNS = len(PATTERN_DILS)
ALIBI_SLOPES = tuple(2.0 ** (-8.0 * (h + 1) / N_HEADS) for h in range(N_HEADS))
ATTN_SCALE = 1.0 / math.sqrt(HEAD_DIM)
NEG_BIG = -1e30

ADAM_LR = 0.001
ADAM_B1 = 0.9
ADAM_B2 = 0.999
ADAM_EPS = 1e-08
ADAM_WD = 0.01
ADAM_STEP = 10

C_AX, C_AB, C_AC, C_AG, C_RX, C_RG, C_CU, C_CV, C_CG, C_DQ, C_DK, C_DV, C_DG = range(13)

SUBLANES = 8
LANES = 128
VMEM_LIMIT = 56 * 1024 * 1024
TILE_IN = 512
TILE_MIX = 512
TILE_DW = 1024
TILE_DW_OUT = 2048
ATTN_QB = 4
GELU_K0 = math.sqrt(2.0 / math.pi)
GELU_K1 = 0.044715


def _params(sem):
    return pltpu.CompilerParams(dimension_semantics=sem, vmem_limit_bytes=VMEM_LIMIT)


def _silu(x):
    return x * jax.nn.sigmoid(x)


def _dsilu(x):
    s = jax.nn.sigmoid(x)
    return s * (1.0 + x * (1.0 - s))


def _gelu(x):
    return 0.5 * x * (1.0 + jnp.tanh(GELU_K0 * (x + GELU_K1 * x * x * x)))


def _gelu_and_grad(x):
    t = jnp.tanh(GELU_K0 * (x + GELU_K1 * x * x * x))
    g = 0.5 * x * (1.0 + t)
    dg = 0.5 * (1.0 + t) + 0.5 * x * (1.0 - t * t) * GELU_K0 * (1.0 + 3.0 * GELU_K1 * x * x)
    return g, dg


def _neg_expm1(x):
    series = x * (1.0 + x * (0.5 + x * (1.0 / 6.0 + x * (1.0 / 24.0 + x * (1.0 / 120.0)))))
    return -jnp.where(x > -0.05, series, jnp.exp(x) - 1.0)


def _shift_down(v, halo, k):
    r = pltpu.roll(v, k, 0)
    rh = pltpu.roll(halo, k, 0)
    row = lax.broadcasted_iota(jnp.int32, halo.shape, 0)
    top = jnp.where(row < k, rh, r[:SUBLANES])
    return jnp.concatenate([top, r[SUBLANES:]], axis=0)


def _shift_up(v, halo, k):
    t = v.shape[0]
    r = pltpu.roll(v, t - k, 0)
    rh = pltpu.roll(halo, SUBLANES - k, 0)
    row = lax.broadcasted_iota(jnp.int32, halo.shape, 0)
    bot = jnp.where(row >= SUBLANES - k, rh, r[t - SUBLANES:])
    return jnp.concatenate([r[:t - SUBLANES], bot], axis=0)


def _scan_causal(a, b, h_in):
    t = a.shape[0]
    row8 = lax.broadcasted_iota(jnp.int32, a.shape, 0) % SUBLANES
    d = 1
    while d < SUBLANES:
        m = row8 >= d
        a_s = jnp.where(m, pltpu.roll(a, d, 0), 1.0)
        b_s = jnp.where(m, pltpu.roll(b, d, 0), 0.0)
        b = a * b_s + b
        a = a * a_s
        d *= 2
    out, carry = [], h_in
    for g in range(t // SUBLANES):
        sl = slice(g * SUBLANES, (g + 1) * SUBLANES)
        hg = b[sl] + a[sl] * carry
        out.append(hg)
        carry = hg[SUBLANES - 1:SUBLANES]
    return jnp.concatenate(out, axis=0)


def _scan_anticausal(a, b, g_in):
    t = a.shape[0]
    row8 = lax.broadcasted_iota(jnp.int32, a.shape, 0) % SUBLANES
    d = 1
    while d < SUBLANES:
        m = row8 < SUBLANES - d
        a_s = jnp.where(m, pltpu.roll(a, t - d, 0), 1.0)
        b_s = jnp.where(m, pltpu.roll(b, t - d, 0), 0.0)
        b = a * b_s + b
        a = a * a_s
        d *= 2
    out, carry = [], g_in
    for g in reversed(range(t // SUBLANES)):
        sl = slice(g * SUBLANES, (g + 1) * SUBLANES)
        gg = b[sl] + a[sl] * carry
        out.append(gg)
        carry = gg[0:1]
    return jnp.concatenate(out[::-1], axis=0)


def _head_of_lane(shape):
    return lax.broadcasted_iota(jnp.int32, shape, len(shape) - 1) // HEAD_DIM


def _put_row(acc_shape, k, row_vec):
    row = lax.broadcasted_iota(jnp.int32, acc_shape, 0)
    return jnp.where(row == k, jnp.broadcast_to(row_vec, acc_shape), 0.0)


def _dot(a, b):
    return jnp.dot(a, b, preferred_element_type=F32)


def _dot_nt(a, b):
    return lax.dot_general(a, b, (((1,), (1,)), ((), ())), preferred_element_type=F32)


def _dot_tn(a, b):
    return lax.dot_general(a, b, (((0,), (0,)), ((), ())), preferred_element_type=F32)


def _deinterleave_store(val, stage, outs):
    t, c = val.shape
    for hh in range(c // LANES):
        stage[hh][...] = val[:, hh * LANES:(hh + 1) * LANES].astype(F32)
    for dil, ref in outs:
        for r in range(dil):
            for hh in range(c // LANES):
                ref[r, :, hh * LANES:(hh + 1) * LANES] = stage[hh][pl.ds(r, t // dil, stride=dil), :].astype(ref.dtype)


def _interleave_load(ref, dil, stage):
    _, n, c = ref.shape
    for r in range(dil):
        for hh in range(c // LANES):
            stage[hh][pl.ds(r, n, stride=dil), :] = ref[r, :, hh * LANES:(hh + 1) * LANES].astype(F32)
    return jnp.concatenate([stage[hh][...] for hh in range(c // LANES)], axis=1)


def _stage_scratch(tile, cols, copies):
    return [pltpu.VMEM((tile, LANES), F32)] * (copies * (cols // LANES))


def _by_residue(s, dil, cols, dtype):
    return jax.ShapeDtypeStruct((dil, s // dil, cols), dtype)


def _residue_block(dil, tile, cols):
    return pl.BlockSpec((dil, tile // dil, cols), lambda i: (0, i, 0))


def in_fwd(x, g, w):
    s = x.shape[0]
    qkv_w = 3 * GROUP_W

    def body(x_ref, g_ref, w_ref, z_ref, h_ref, qkv1_ref, qkv4_ref, qkv16_ref, *stage):
        xv = x_ref[...]
        rs = lax.rsqrt(jnp.mean(xv * xv, axis=-1, keepdims=True) + NORM_EPS)
        h = (xv * rs * g_ref[...]).astype(BF16)
        h_ref[...] = h
        z = _dot_nt(h, w_ref[...])
        z_ref[...] = z
        qkv = z[:, C_DQ * GROUP_W:(C_DV + 1) * GROUP_W]
        qkv1_ref[...] = qkv.astype(BF16)
        _deinterleave_store(qkv, stage, ((PATTERN_DILS[1], qkv4_ref), (PATTERN_DILS[2], qkv16_ref)))

    return pl.pallas_call(
        body, name="in_fwd", grid=(s // TILE_IN,),
        in_specs=[pl.BlockSpec((TILE_IN, D_MODEL), lambda i: (i, 0)),
                  pl.BlockSpec((1, D_MODEL), lambda i: (0, 0)),
                  pl.BlockSpec((D_IN, D_MODEL), lambda i: (0, 0))],
        out_specs=[pl.BlockSpec((TILE_IN, D_IN), lambda i: (i, 0)),
                   pl.BlockSpec((TILE_IN, D_MODEL), lambda i: (i, 0)),
                   pl.BlockSpec((TILE_IN, qkv_w), lambda i: (i, 0)),
                   _residue_block(PATTERN_DILS[1], TILE_IN, qkv_w),
                   _residue_block(PATTERN_DILS[2], TILE_IN, qkv_w)],
        out_shape=[jax.ShapeDtypeStruct((s, D_IN), F32), jax.ShapeDtypeStruct((s, D_MODEL), BF16),
                   jax.ShapeDtypeStruct((s, qkv_w), BF16),
                   _by_residue(s, PATTERN_DILS[1], qkv_w, BF16), _by_residue(s, PATTERN_DILS[2], qkv_w, BF16)],
        scratch_shapes=_stage_scratch(TILE_IN, qkv_w, 1),
        compiler_params=_params(("parallel",)),
    )(x, g, w)


def out_fwd(y, w, x):
    s = x.shape[0]

    def body(y_ref, w_ref, x_ref, o_ref):
        o_ref[...] = x_ref[...] + _dot(y_ref[...], w_ref[...])

    return pl.pallas_call(
        body, name="out_fwd", grid=(s // TILE_IN,),
        in_specs=[pl.BlockSpec((TILE_IN, D_MIX), lambda i: (i, 0)),
                  pl.BlockSpec((D_MIX, D_MODEL), lambda i: (0, 0)),
                  pl.BlockSpec((TILE_IN, D_MODEL), lambda i: (i, 0))],
        out_specs=pl.BlockSpec((TILE_IN, D_MODEL), lambda i: (i, 0)),
        out_shape=jax.ShapeDtypeStruct((s, D_MODEL), F32),
        compiler_params=_params(("parallel",)),
    )(y, w, x)


def out_bwd(dx, w, z, o):
    s = dx.shape[0]
    abc = 3 * GROUP_W

    def body(dx_ref, w_ref, dg_ref, o_ref, dy_ref, ddg_ref, do1_ref, do4_ref, do16_ref, dl1_ref, dl4_ref, dl16_ref,
             *stage):
        stage_a, stage_b = stage[:2], stage[2:]
        dy = _dot_nt(dx_ref[...].astype(BF16), w_ref[...])
        dy_ref[...] = dy[:, :abc]
        dyd = dy[:, abc:]
        head = _head_of_lane((TILE_IN, GROUP_W))
        dg = dg_ref[...]
        o = o_ref[...]
        do = dyd * _silu(dg)
        ddg_ref[...] = dyd * o * _dsilu(dg)
        prod = do * o
        dl = jnp.zeros_like(prod)
        for h in range(N_HEADS):
            sm = jnp.sum(jnp.where(head == h, prod, 0.0), axis=-1, keepdims=True)
            dl = jnp.where(head == h, sm, dl)
        do1_ref[...] = do.astype(BF16)
        dl1_ref[...] = dl
        _deinterleave_store(do, stage_a, ((PATTERN_DILS[1], do4_ref), (PATTERN_DILS[2], do16_ref)))
        _deinterleave_store(dl, stage_b, ((PATTERN_DILS[1], dl4_ref), (PATTERN_DILS[2], dl16_ref)))

    row = pl.BlockSpec((TILE_IN, GROUP_W), lambda i: (i, 0))
    r4 = _residue_block(PATTERN_DILS[1], TILE_IN, GROUP_W)
    r16 = _residue_block(PATTERN_DILS[2], TILE_IN, GROUP_W)
    return pl.pallas_call(
        body, name="out_bwd", grid=(s // TILE_IN,),
        in_specs=[pl.BlockSpec((TILE_IN, D_MODEL), lambda i: (i, 0)),
                  pl.BlockSpec((D_MIX, D_MODEL), lambda i: (0, 0)),
                  pl.BlockSpec((TILE_IN, GROUP_W), lambda i: (i, C_DG)), row],
        out_specs=[pl.BlockSpec((TILE_IN, abc), lambda i: (i, 0)), row, row, r4, r16, row, r4, r16],
        out_shape=[jax.ShapeDtypeStruct((s, abc), F32), jax.ShapeDtypeStruct((s, GROUP_W), F32),
                   jax.ShapeDtypeStruct((s, GROUP_W), BF16),
                   _by_residue(s, PATTERN_DILS[1], GROUP_W, BF16), _by_residue(s, PATTERN_DILS[2], GROUP_W, BF16),
                   jax.ShapeDtypeStruct((s, GROUP_W), F32),
                   _by_residue(s, PATTERN_DILS[1], GROUP_W, F32), _by_residue(s, PATTERN_DILS[2], GROUP_W, F32)],
        scratch_shapes=_stage_scratch(TILE_IN, GROUP_W, 2),
        compiler_params=_params(("parallel",)),
    )(dx, w, z, o)


def in_bwd(dz, w, x, g, dx_next):
    s = x.shape[0]

    def body(dz_ref, w_ref, x_ref, g_ref, dxn_ref, dx_ref, dg_ref):
        @pl.when(pl.program_id(0) == 0)
        def _():
            dg_ref[...] = jnp.zeros_like(dg_ref)

        dh = _dot(dz_ref[...], w_ref[...])
        xv = x_ref[...]
        rs = lax.rsqrt(jnp.mean(xv * xv, axis=-1, keepdims=True) + NORM_EPS)
        xh = xv * rs
        dg_ref[...] += _put_row(dg_ref.shape, 0, jnp.sum(dh * xh, axis=0, keepdims=True))
        dn = dh * g_ref[...]
        dx_ref[...] = dxn_ref[...] + rs * (dn - xh * jnp.mean(dn * xh, axis=-1, keepdims=True))

    return pl.pallas_call(
        body, name="in_bwd", grid=(s // TILE_IN,),
        in_specs=[pl.BlockSpec((TILE_IN, D_IN), lambda i: (i, 0)),
                  pl.BlockSpec((D_IN, D_MODEL), lambda i: (0, 0)),
                  pl.BlockSpec((TILE_IN, D_MODEL), lambda i: (i, 0)),
                  pl.BlockSpec((1, D_MODEL), lambda i: (0, 0)),
                  pl.BlockSpec((TILE_IN, D_MODEL), lambda i: (i, 0))],
        out_specs=[pl.BlockSpec((TILE_IN, D_MODEL), lambda i: (i, 0)),
                   pl.BlockSpec((SUBLANES, D_MODEL), lambda i: (0, 0))],
        out_shape=[jax.ShapeDtypeStruct((s, D_MODEL), F32), jax.ShapeDtypeStruct((SUBLANES, D_MODEL), F32)],
        compiler_params=_params(("arbitrary",)),
    )(dz, w, x, g, dx_next)


def grad_w_out(y, dx):
    s = y.shape[0]

    def body(y_ref, dx_ref, o_ref):
        @pl.when(pl.program_id(0) == 0)
        def _():
            o_ref[...] = jnp.zeros_like(o_ref)

        o_ref[...] += _dot_tn(y_ref[...], dx_ref[...].astype(BF16))

    return pl.pallas_call(
        body, name="grad_w_out", grid=(s // TILE_DW_OUT,),
        in_specs=[pl.BlockSpec((TILE_DW_OUT, D_MIX), lambda k: (k, 0)),
                  pl.BlockSpec((TILE_DW_OUT, D_MODEL), lambda k: (k, 0))],
        out_specs=pl.BlockSpec((D_MIX, D_MODEL), lambda k: (0, 0)),
        out_shape=jax.ShapeDtypeStruct((D_MIX, D_MODEL), F32),
        compiler_params=_params(("arbitrary",)),
    )(y, dx)


def grad_w_in(h, dz, chip):
    s = h.shape[0]
    rows = D_IN // N_CHIPS

    def body(chip_ref, h_ref, dz_ref, staged_ref, own_ref, acc):
        k = pl.program_id(0)

        @pl.when(k == 0)
        def _():
            acc[...] = jnp.zeros_like(acc)

        acc[...] += _dot_tn(dz_ref[...], h_ref[...])

        @pl.when(k == s // TILE_DW - 1)
        def _():
            for j in range(N_CHIPS):
                part = acc[j * rows:(j + 1) * rows, :]
                staged_ref[j] = part.astype(BF16)

                @pl.when(chip_ref[0] == j)
                def _():
                    own_ref[...] = part

    return pl.pallas_call(
        body, name="grad_w_in",
        grid_spec=pltpu.PrefetchScalarGridSpec(
            num_scalar_prefetch=1, grid=(s // TILE_DW,),
            in_specs=[pl.BlockSpec((TILE_DW, D_MODEL), lambda k, c: (k, 0)),
                      pl.BlockSpec((TILE_DW, D_IN), lambda k, c: (k, 0))],
            out_specs=[pl.BlockSpec((N_CHIPS, rows, D_MODEL), lambda k, c: (0, 0, 0)),
                       pl.BlockSpec((rows, D_MODEL), lambda k, c: (0, 0))],
            scratch_shapes=[pltpu.VMEM((D_IN, D_MODEL), F32)]),
        out_shape=[jax.ShapeDtypeStruct((N_CHIPS, rows, D_MODEL), BF16), jax.ShapeDtypeStruct((rows, D_MODEL), F32)],
        compiler_params=_params(("arbitrary",)),
    )(chip, h, dz)


def out_fwd_loss(y, w, x, g, tgt):
    s = x.shape[0]

    def body(y_ref, w_ref, x_ref, g_ref, t_ref, l_ref, dx_ref, dg_ref):
        @pl.when(pl.program_id(0) == 0)
        def _():
            l_ref[...] = jnp.zeros_like(l_ref)
            dg_ref[...] = jnp.zeros_like(dg_ref)

        xv = x_ref[...] + _dot(y_ref[...], w_ref[...])
        gv = g_ref[...]
        rs = lax.rsqrt(jnp.mean(xv * xv, axis=-1, keepdims=True) + NORM_EPS)
        xh = xv * rs
        e = xh * gv - t_ref[...]
        part = 0.5 * jnp.sum(jnp.mean(e * e, axis=-1, keepdims=True), axis=0, keepdims=True)
        l_ref[...] += jnp.broadcast_to(part, l_ref.shape)
        dy = e * (1.0 / D_MODEL)
        dg_ref[...] += _put_row(dg_ref.shape, 0, jnp.sum(dy * xh, axis=0, keepdims=True))
        dn = dy * gv
        dx_ref[...] = rs * (dn - xh * jnp.mean(dn * xh, axis=-1, keepdims=True))

    return pl.pallas_call(
        body, name="out_fwd_loss", grid=(s // TILE_IN,),
        in_specs=[pl.BlockSpec((TILE_IN, D_MIX), lambda i: (i, 0)),
                  pl.BlockSpec((D_MIX, D_MODEL), lambda i: (0, 0)),
                  pl.BlockSpec((TILE_IN, D_MODEL), lambda i: (i, 0)),
                  pl.BlockSpec((1, D_MODEL), lambda i: (0, 0)),
                  pl.BlockSpec((TILE_IN, D_MODEL), lambda i: (i, 0))],
        out_specs=[pl.BlockSpec((SUBLANES, 128), lambda i: (0, 0)),
                   pl.BlockSpec((TILE_IN, D_MODEL), lambda i: (i, 0)),
                   pl.BlockSpec((SUBLANES, D_MODEL), lambda i: (0, 0))],
        out_shape=[jax.ShapeDtypeStruct((SUBLANES, 128), F32), jax.ShapeDtypeStruct((s, D_MODEL), F32),
                   jax.ShapeDtypeStruct((SUBLANES, D_MODEL), F32)],
        compiler_params=_params(("arbitrary",)),
    )(y, w, x, g, tgt)


def _attn_bias(dil):
    qi = np.arange(ATTN_BLOCK)[:, None]
    ki = np.arange(2 * ATTN_BLOCK)[None, :]
    delta = qi + ATTN_BLOCK - ki
    band = (delta >= 0) & (delta <= ATTN_BLOCK)
    out = np.empty((2, N_HEADS, ATTN_BLOCK, 2 * ATTN_BLOCK), np.float32)
    for f in range(2):
        ok = band & ((ki >= ATTN_BLOCK) | (f == 0))
        for h in range(N_HEADS):
            out[f, h] = np.where(ok, -ALIBI_SLOPES[h] * dil * delta, NEG_BIG)
    return jnp.asarray(out.reshape(2, N_HEADS * ATTN_BLOCK, 2 * ATTN_BLOCK))


def _stack_heads(a, head):
    return jnp.concatenate([jnp.where(head == h, a, jnp.zeros_like(a)) for h in range(N_HEADS)], axis=0)


def _unstack_heads(a, head):
    out = a[:ATTN_BLOCK]
    for h in range(1, N_HEADS):
        out = jnp.where(head == h, a[h * ATTN_BLOCK:(h + 1) * ATTN_BLOCK], out)
    return out


def _head_column(a):
    return jnp.concatenate([a[:, h * HEAD_DIM:h * HEAD_DIM + 1] for h in range(N_HEADS)], axis=0)


def _attn_specs(n_blocks):
    rows = ATTN_QB * ATTN_BLOCK
    cur = lambda c: pl.BlockSpec((rows, GROUP_W), lambda n, c=c: (n, c))
    prev = lambda c: pl.BlockSpec((ATTN_BLOCK, GROUP_W), lambda n, c=c: (jnp.maximum(n * ATTN_QB - 1, 0), c))
    nxt = lambda c: pl.BlockSpec((ATTN_BLOCK, GROUP_W),
                                 lambda n, c=c: (jnp.minimum(n * ATTN_QB + ATTN_QB, n_blocks - 1), c))
    return cur, prev, nxt


def _keys(kp_ref, k_ref, j):
    prev = kp_ref[...] if j == 0 else k_ref[(j - 1) * ATTN_BLOCK:j * ATTN_BLOCK, :]
    return jnp.concatenate([prev, k_ref[j * ATTN_BLOCK:(j + 1) * ATTN_BLOCK, :]], axis=0)


def attn_fwd(qkv, dil):
    s = qkv.shape[0]
    n_blocks = s // ATTN_BLOCK
    bps = n_blocks // dil
    rows = ATTN_QB * ATTN_BLOCK

    def body(q_ref, kp_ref, k_ref, vp_ref, v_ref, bias_ref, o_ref, lse_ref):
        n = pl.program_id(0)
        head = _head_of_lane((ATTN_BLOCK, GROUP_W))
        for j in range(ATTN_QB):
            sl = slice(j * ATTN_BLOCK, (j + 1) * ATTN_BLOCK)
            first = (((n * ATTN_QB + j) % bps) == 0).astype(jnp.int32)
            qs = _stack_heads(q_ref[sl, :], head)
            sc = _dot_nt(qs, _keys(kp_ref, k_ref, j)) * ATTN_SCALE + bias_ref[first]
            m = jnp.max(sc, axis=-1, keepdims=True)
            pr = jnp.exp(sc - m)
            l = jnp.sum(pr, axis=-1, keepdims=True)
            oh = _dot(pr.astype(BF16), _keys(vp_ref, v_ref, j)) / l
            o_ref[sl, :] = _unstack_heads(oh, head)
            lse_ref[sl, :] = _unstack_heads(jnp.broadcast_to(m + jnp.log(l), oh.shape), head)

    cur, prev, _ = _attn_specs(n_blocks)
    bias = _attn_bias(dil)
    out = jax.ShapeDtypeStruct((s, GROUP_W), F32)
    return pl.pallas_call(
        body, name=f"attn_fwd_d{dil}", grid=(n_blocks // ATTN_QB,),
        in_specs=[cur(0), prev(1), cur(1), prev(2), cur(2), pl.BlockSpec(bias.shape, lambda n: (0, 0, 0))],
        out_specs=[cur(0), cur(0)],
        out_shape=[out, out],
        compiler_params=_params(("parallel",)),
    )(qkv, qkv, qkv, qkv, qkv, bias)


def attn_bwd(qkv, do, lse, dlt, dil):
    s = qkv.shape[0]
    n_blocks = s // ATTN_BLOCK
    bps = n_blocks // dil
    rows = ATTN_QB * ATTN_BLOCK

    def body(q_ref, qn_ref, kp_ref, k_ref, vp_ref, v_ref, do_ref, don_ref, lse_ref, lsen_ref, dl_ref, dln_ref,
             bias_ref, out_ref, dk_acc, dv_acc):
        n = pl.program_id(0)
        head = _head_of_lane((ATTN_BLOCK, GROUP_W))
        dk_acc[...] = jnp.zeros_like(dk_acc)
        dv_acc[...] = jnp.zeros_like(dv_acc)

        def pair(qj, doj, lsej, dlj, kk, vv, bias, keep):
            qs = _stack_heads(qj, head)
            dos = _stack_heads(doj, head)
            sc = _dot_nt(qs, kk) * ATTN_SCALE + bias
            if keep is None:
                pr = jnp.exp(sc - _head_column(lsej))
            else:
                pr = jnp.exp(jnp.minimum(sc - _head_column(lsej), 0.0)) * keep
            dp = _dot_nt(dos, vv)
            ds = (pr * (dp - _head_column(dlj)) * ATTN_SCALE).astype(BF16)
            return ds, _dot_tn(ds, qs), _dot_tn(pr.astype(BF16), dos)

        for j in range(ATTN_QB):
            sl = slice(j * ATTN_BLOCK, (j + 1) * ATTN_BLOCK)
            first = (((n * ATTN_QB + j) % bps) == 0).astype(jnp.int32)
            kk = _keys(kp_ref, k_ref, j)
            ds, dks, dvs = pair(q_ref[sl, :], do_ref[sl, :], lse_ref[sl, :], dl_ref[sl, :],
                                kk, _keys(vp_ref, v_ref, j), bias_ref[first], None)
            out_ref[sl, 0:GROUP_W] = _unstack_heads(_dot(ds, kk), head)
            acc = slice(j * ATTN_BLOCK, (j + 2) * ATTN_BLOCK)
            dk_acc[acc, :] += dks
            dv_acc[acc, :] += dvs

        nxt = n * ATTN_QB + ATTN_QB
        valid = ((nxt < n_blocks) & ((nxt % bps) != 0)).astype(F32)
        last = slice((ATTN_QB - 1) * ATTN_BLOCK, ATTN_QB * ATTN_BLOCK)
        _, dks, dvs = pair(qn_ref[...], don_ref[...], lsen_ref[...], dln_ref[...], k_ref[last, :], v_ref[last, :],
                           bias_ref[0][:, :ATTN_BLOCK], valid)
        acc = slice(ATTN_QB * ATTN_BLOCK, (ATTN_QB + 1) * ATTN_BLOCK)
        dk_acc[acc, :] += dks
        dv_acc[acc, :] += dvs
        out_ref[:, GROUP_W:2 * GROUP_W] = dk_acc[ATTN_BLOCK:, :]
        out_ref[:, 2 * GROUP_W:3 * GROUP_W] = dv_acc[ATTN_BLOCK:, :]

    cur, prev, nxt = _attn_specs(n_blocks)
    bias = _attn_bias(dil)
    return pl.pallas_call(
        body, name=f"attn_bwd_d{dil}", grid=(n_blocks // ATTN_QB,),
        in_specs=[cur(0), nxt(0), prev(1), cur(1), prev(2), cur(2), cur(0), nxt(0), cur(0), nxt(0), cur(0), nxt(0),
                  pl.BlockSpec(bias.shape, lambda n: (0, 0, 0))],
        out_specs=pl.BlockSpec((rows, 3 * GROUP_W), lambda n: (n, 0)),
        out_shape=jax.ShapeDtypeStruct((s, 3 * GROUP_W), F32),
        scratch_shapes=[pltpu.VMEM(((ATTN_QB + 1) * ATTN_BLOCK, GROUP_W), F32),
                        pltpu.VMEM(((ATTN_QB + 1) * ATTN_BLOCK, GROUP_W), F32)],
        compiler_params=_params(("parallel",)),
    )(qkv, qkv, qkv, qkv, qkv, qkv, do, do, lse, lse, dlt, dlt, bias)


def _zcol(c):
    return pl.BlockSpec((TILE_MIX, GROUP_W), lambda i, c=c: (i, c))


def _zhalo(c):
    per = TILE_MIX // SUBLANES
    return pl.BlockSpec((SUBLANES, GROUP_W), lambda i, c=c: (jnp.maximum(i * per - 1, 0), c))


def _full(shape):
    return pl.BlockSpec(shape, lambda i: tuple(0 for _ in shape))


def _of_layer(a, l):
    rest = a.shape[1:]
    return pl.BlockSpec((None,) + rest, lambda i: (l,) + tuple(0 for _ in rest))


def _softplus_neg(lam):
    nl = -lam
    return jnp.maximum(nl, 0.0) + jnp.log1p(jnp.exp(-jnp.abs(nl)))


def _lru_gates(xb, wa_ref, wx_ref, ba, bx, lam):
    xbb = xb.astype(BF16)
    r = jax.nn.sigmoid(_dot(xbb, wa_ref[...]) + ba)
    ig = jax.nn.sigmoid(_dot(xbb, wx_ref[...]) + bx)
    log_a = (-RG_C * r) * _softplus_neg(lam)
    a = jnp.exp(log_a)
    mult = jnp.sqrt(_neg_expm1(2.0 * log_a))
    return r, ig, a, mult


LRU_SAVED = 5


def _gmlp_spatial(ws_ref, vvb, head):
    outs = []
    for j in range(vvb.shape[0] // GMLP_CHUNK):
        blk = vvb[j * GMLP_CHUNK:(j + 1) * GMLP_CHUNK, :]
        acc = jnp.zeros((GMLP_CHUNK, GROUP_W), F32)
        for h in range(N_HEADS):
            acc = jnp.where(head[:GMLP_CHUNK] == h, _dot(ws_ref[h], blk), acc)
        outs.append(acc)
    return jnp.concatenate(outs, axis=0)


def mix_fwd(z, attn, wts, l):
    s = z.shape[0]
    d4, d16 = PATTERN_DILS[1], PATTERN_DILS[2]

    def body(ax_ref, ab_ref, ac_ref, ag_ref, rx_ref, rg_ref, cu_ref, cv_ref, cg_ref, dg_ref,
             axh_ref, ach_ref, rxh_ref, o1_ref, l1_ref, o4_ref, l4_ref, o16_ref, l16_ref,
             caw_ref, crw_ref, crb_ref, wa_ref, wx_ref, ba_ref, bx_ref, lam_ref, gng_ref, ws_ref, bs_ref,
             y_ref, hl_ref, o_ref, lse_ref, lse4_ref, lse16_ref, lru_ref, carry, *stage):
        st_a, st_b, st_c, st_d, st_e = (stage[2 * k:2 * k + 2] for k in range(5))
        i = pl.program_id(0)

        @pl.when(i == 0)
        def _():
            carry[...] = jnp.zeros_like(carry)

        nz = (i > 0).astype(F32)
        head = _head_of_lane((TILE_MIX, GROUP_W))

        pa = ac_ref[...] * ax_ref[...]
        pah = ach_ref[...] * axh_ref[...] * nz
        cv = caw_ref[2:3, :] * pa + caw_ref[1:2, :] * _shift_down(pa, pah, 1) + caw_ref[0:1, :] * _shift_down(pa, pah, 2)
        y_ref[:, 0:GROUP_W] = (ab_ref[...] * cv * _silu(ag_ref[...])).astype(BF16)

        rx = rx_ref[...]
        rxh = rxh_ref[...] * nz
        xb = (crw_ref[3:4, :] * rx + crw_ref[2:3, :] * _shift_down(rx, rxh, 1) + crw_ref[1:2, :] * _shift_down(rx, rxh, 2)
              + crw_ref[0:1, :] * _shift_down(rx, rxh, 3) + crb_ref[...])
        r, ig, a, mult = _lru_gates(xb, wa_ref, wx_ref, ba_ref[...], bx_ref[...], lam_ref[...])
        for k, val in enumerate((xb, r, ig, a, mult)):
            lru_ref[:, k * GROUP_W:(k + 1) * GROUP_W] = val
        hl = _scan_causal(a, mult * (ig * xb), carry[SUBLANES - 1:SUBLANES, :])
        hl_ref[...] = hl
        carry[...] = hl[TILE_MIX - SUBLANES:, :]
        y_ref[:, GROUP_W:2 * GROUP_W] = (hl * _silu(rg_ref[...])).astype(BF16)

        u = _gelu(cu_ref[...])
        gv = _gelu(cv_ref[...])
        rs = lax.rsqrt(jnp.mean(gv * gv, axis=-1, keepdims=True) + NORM_EPS)
        vvb = (gv * rs * gng_ref[...]).astype(BF16)
        sp = _gmlp_spatial(ws_ref, vvb, head) + jnp.concatenate([bs_ref[...]] * (TILE_MIX // GMLP_CHUNK), axis=0)
        y_ref[:, 2 * GROUP_W:3 * GROUP_W] = (u * sp * _silu(cg_ref[...])).astype(BF16)

        ops = (o1_ref[...], _interleave_load(o4_ref, d4, st_a), _interleave_load(o16_ref, d16, st_b))
        lps = (l1_ref[...], _interleave_load(l4_ref, d4, st_c), _interleave_load(l16_ref, d16, st_d))
        m = jnp.maximum(jnp.maximum(lps[0], lps[1]), lps[2])
        zsum = jnp.zeros_like(m)
        o = jnp.zeros_like(m)
        for op, lp in zip(ops, lps):
            w = jnp.exp(lp - m)
            zsum = zsum + w
            o = o + w * op
        o = o / zsum
        lse = m + jnp.log(zsum)
        o_ref[...] = o
        lse_ref[...] = lse
        _deinterleave_store(lse, st_e, ((d4, lse4_ref), (d16, lse16_ref)))
        y_ref[:, 3 * GROUP_W:4 * GROUP_W] = (o * _silu(dg_ref[...])).astype(BF16)

    row = pl.BlockSpec((TILE_MIX, GROUP_W), lambda i: (i, 0))
    r4 = _residue_block(d4, TILE_MIX, GROUP_W)
    r16 = _residue_block(d16, TILE_MIX, GROUP_W)
    names = ("caw", "crw", "crb", "wa", "wx", "ba", "bx", "lam", "gng", "ws", "bs")
    in_specs = ([_zcol(c) for c in (C_AX, C_AB, C_AC, C_AG, C_RX, C_RG, C_CU, C_CV, C_CG, C_DG)]
                + [_zhalo(C_AX), _zhalo(C_AC), _zhalo(C_RX), row, row, r4, r4, r16, r16]
                + [_of_layer(wts[k], l) for k in names])
    return pl.pallas_call(
        body, name="mix_fwd", grid=(s // TILE_MIX,),
        in_specs=in_specs,
        out_specs=[pl.BlockSpec((TILE_MIX, D_MIX), lambda i: (i, 0)), row, row, row, r4, r16,
                   pl.BlockSpec((TILE_MIX, LRU_SAVED * GROUP_W), lambda i: (i, 0))],
        out_shape=([jax.ShapeDtypeStruct((s, D_MIX), BF16)] + [jax.ShapeDtypeStruct((s, GROUP_W), F32)] * 3
                   + [_by_residue(s, d4, GROUP_W, F32), _by_residue(s, d16, GROUP_W, F32),
                      jax.ShapeDtypeStruct((s, LRU_SAVED * GROUP_W), F32)]),
        scratch_shapes=[pltpu.VMEM((SUBLANES, GROUP_W), F32)] + _stage_scratch(TILE_MIX, GROUP_W, 5),
        compiler_params=_params(("arbitrary",)),
    )(*([z] * 13), *[a for pair in attn for a in pair], *[wts[k] for k in names])


def mix_bwd(dy, z, hl, lru, dqkv, ddg, wts, l):
    s = z.shape[0]
    d4, d16 = PATTERN_DILS[1], PATTERN_DILS[2]
    n_tiles = s // TILE_MIX

    def body(dya_ref, dyb_ref, dyc_ref, ax_ref, ab_ref, ac_ref, ag_ref, rx_ref, rg_ref, cu_ref, cv_ref, cg_ref,
             axh_ref, ach_ref, rxh_ref, hl_ref, hlh_ref, lru_ref, dqkv1_ref, dqkv4_ref, dqkv16_ref, ddg_ref,
             caw_ref, crw_ref, crb_ref, wa_ref, wx_ref, ba_ref, bx_ref, lam_ref, gng_ref, ws_ref, wst_ref, bs_ref,
             dz_ref, ga_ref, gr_ref, gn_ref, gwa_ref, gwx_ref, gws_ref, gbs_ref,
             c_dcv, c_g, c_a, c_dxb, *stage):
        st_a, st_b = stage[:len(stage) // 2], stage[len(stage) // 2:]
        step = pl.program_id(0)
        i = n_tiles - 1 - step

        @pl.when(step == 0)
        def _():
            for r in (c_dcv, c_g, c_a, c_dxb, ga_ref, gr_ref, gn_ref, gwa_ref, gwx_ref, gws_ref, gbs_ref):
                r[...] = jnp.zeros_like(r)

        nz = (i > 0).astype(F32)
        head = _head_of_lane((TILE_MIX, GROUP_W))
        shp8 = (SUBLANES, GROUP_W)
        colsum = lambda v: jnp.sum(v, axis=0, keepdims=True)

        ax, ab, ac, ag = ax_ref[...], ab_ref[...], ac_ref[...], ag_ref[...]
        dya = dya_ref[...]
        pa = ac * ax
        pah = ach_ref[...] * axh_ref[...] * nz
        pa1 = _shift_down(pa, pah, 1)
        pa2 = _shift_down(pa, pah, 2)
        cv = caw_ref[2:3, :] * pa + caw_ref[1:2, :] * pa1 + caw_ref[0:1, :] * pa2
        sg = _silu(ag)
        dz_ref[:, C_AB * GROUP_W:(C_AB + 1) * GROUP_W] = (dya * cv * sg).astype(BF16)
        dz_ref[:, C_AG * GROUP_W:(C_AG + 1) * GROUP_W] = (dya * ab * cv * _dsilu(ag)).astype(BF16)
        dcv = dya * ab * sg
        nxt = c_dcv[...]
        dpa = caw_ref[2:3, :] * dcv + caw_ref[1:2, :] * _shift_up(dcv, nxt, 1) + caw_ref[0:1, :] * _shift_up(dcv, nxt, 2)
        c_dcv[...] = dcv[:SUBLANES, :]
        dz_ref[:, C_AC * GROUP_W:(C_AC + 1) * GROUP_W] = (dpa * ax).astype(BF16)
        dz_ref[:, C_AX * GROUP_W:(C_AX + 1) * GROUP_W] = (dpa * ac).astype(BF16)
        ga_ref[...] += (_put_row(shp8, 2, colsum(dcv * pa)) + _put_row(shp8, 1, colsum(dcv * pa1))
                        + _put_row(shp8, 0, colsum(dcv * pa2)))

        rx, rg = rx_ref[...], rg_ref[...]
        dyb = dyb_ref[...]
        rxh = rxh_ref[...] * nz
        rx1, rx2, rx3 = _shift_down(rx, rxh, 1), _shift_down(rx, rxh, 2), _shift_down(rx, rxh, 3)
        xb, r, ig, a, mult = (lru_ref[:, k * GROUP_W:(k + 1) * GROUP_W] for k in range(LRU_SAVED))
        lam = lam_ref[...]
        sp = _softplus_neg(lam)
        hl = hl_ref[...]
        hprev = _shift_down(hl, hlh_ref[...] * nz, 1)
        dz_ref[:, C_RG * GROUP_W:(C_RG + 1) * GROUP_W] = (dyb * hl * _dsilu(rg)).astype(BF16)
        dh = dyb * _silu(rg)
        a_next = _shift_up(a, c_a[...], 1)
        g = _scan_anticausal(a_next, dh, c_g[0:1, :])
        c_g[...] = g[:SUBLANES, :]
        c_a[...] = a[:SUBLANES, :]
        u = ig * xb
        da = g * hprev
        dmult = g * u
        du = g * mult
        dlog_a = da * a - dmult * (a * a) / mult
        dr = dlog_a * (-RG_C * sp)
        dga = dr * r * (1.0 - r)
        dgx = (du * xb) * ig * (1.0 - ig)
        dgab, dgxb = dga.astype(BF16), dgx.astype(BF16)
        dxb = du * ig + _dot_nt(dgab, wa_ref[...]) + _dot_nt(dgxb, wx_ref[...])
        xbb = xb.astype(BF16)
        gwa_ref[...] += _dot_tn(xbb, dgab)
        gwx_ref[...] += _dot_tn(xbb, dgxb)
        nxt = c_dxb[...]
        drx = (crw_ref[3:4, :] * dxb + crw_ref[2:3, :] * _shift_up(dxb, nxt, 1) + crw_ref[1:2, :] * _shift_up(dxb, nxt, 2)
               + crw_ref[0:1, :] * _shift_up(dxb, nxt, 3))
        c_dxb[...] = dxb[:SUBLANES, :]
        dz_ref[:, C_RX * GROUP_W:(C_RX + 1) * GROUP_W] = drx.astype(BF16)
        dlam = colsum(dlog_a * (-RG_C * r)) * (-jax.nn.sigmoid(-lam))
        gr_ref[...] += (_put_row(shp8, 3, colsum(dxb * rx)) + _put_row(shp8, 2, colsum(dxb * rx1))
                        + _put_row(shp8, 1, colsum(dxb * rx2)) + _put_row(shp8, 0, colsum(dxb * rx3))
                        + _put_row(shp8, 4, colsum(dxb)) + _put_row(shp8, 5, colsum(dga))
                        + _put_row(shp8, 6, colsum(dgx)) + _put_row(shp8, 7, dlam))

        cu, cvv, cg = cu_ref[...], cv_ref[...], cg_ref[...]
        dyc = dyc_ref[...]
        u_c, du_c = _gelu_and_grad(cu)
        gv, dgv_c = _gelu_and_grad(cvv)
        rs = lax.rsqrt(jnp.mean(gv * gv, axis=-1, keepdims=True) + NORM_EPS)
        vh = gv * rs
        gng = gng_ref[...]
        vvb = (vh * gng).astype(BF16)
        spat = _gmlp_spatial(ws_ref, vvb, head) + jnp.concatenate([bs_ref[...]] * (TILE_MIX // GMLP_CHUNK), axis=0)
        sgc = _silu(cg)
        dz_ref[:, C_CU * GROUP_W:(C_CU + 1) * GROUP_W] = (dyc * spat * sgc * du_c).astype(BF16)
        dz_ref[:, C_CG * GROUP_W:(C_CG + 1) * GROUP_W] = (dyc * u_c * spat * _dsilu(cg)).astype(BF16)
        dsp = dyc * u_c * sgc
        dspb = dsp.astype(BF16)
        tril = (lax.broadcasted_iota(jnp.int32, (GMLP_CHUNK, GMLP_CHUNK), 0)
                >= lax.broadcasted_iota(jnp.int32, (GMLP_CHUNK, GMLP_CHUNK), 1))
        head_c = head[:GMLP_CHUNK]
        dvv_parts = []
        gbs = jnp.zeros((GMLP_CHUNK, GROUP_W), F32)
        for j in range(TILE_MIX // GMLP_CHUNK):
            sl = slice(j * GMLP_CHUNK, (j + 1) * GMLP_CHUNK)
            dblk = dspb[sl, :]
            vblk = vvb[sl, :]
            gbs = gbs + dsp[sl, :]
            acc = jnp.zeros((GMLP_CHUNK, GROUP_W), F32)
            for h in range(N_HEADS):
                acc = jnp.where(head_c == h, _dot(wst_ref[h], dblk), acc)
                dm = jnp.where(head_c == h, dblk, jnp.zeros_like(dblk))
                gws_ref[h] += jnp.where(tril, _dot_nt(dm, vblk), 0.0)
            dvv_parts.append(acc)
        gbs_ref[...] += gbs
        dvv = jnp.concatenate(dvv_parts, axis=0)
        gn_ref[...] += _put_row(shp8, 0, colsum(dvv * vh))
        dvh = dvv * gng
        dgv = rs * (dvh - vh * jnp.mean(dvh * vh, axis=-1, keepdims=True))
        dz_ref[:, C_CV * GROUP_W:(C_CV + 1) * GROUP_W] = (dgv * dgv_c).astype(BF16)

        dsum = dqkv1_ref[...] + _interleave_load(dqkv4_ref, d4, st_a) + _interleave_load(dqkv16_ref, d16, st_b)
        dz_ref[:, C_DQ * GROUP_W:(C_DV + 1) * GROUP_W] = dsum.astype(BF16)
        dz_ref[:, C_DG * GROUP_W:(C_DG + 1) * GROUP_W] = ddg_ref[...].astype(BF16)

    per = TILE_MIX // SUBLANES
    qkv_w = 3 * GROUP_W
    rev = lambda c: pl.BlockSpec((TILE_MIX, GROUP_W), lambda t, c=c: (n_tiles - 1 - t, c))
    revh = lambda c: pl.BlockSpec((SUBLANES, GROUP_W),
                                  lambda t, c=c: (jnp.maximum((n_tiles - 1 - t) * per - 1, 0), c))
    revr = lambda dil: pl.BlockSpec((dil, TILE_MIX // dil, qkv_w), lambda t: (0, n_tiles - 1 - t, 0))
    names = ("caw", "crw", "crb", "wa", "wx", "ba", "bx", "lam", "gng", "ws", "wst", "bs")
    in_specs = ([rev(0), rev(1), rev(2)]
                + [rev(c) for c in (C_AX, C_AB, C_AC, C_AG, C_RX, C_RG, C_CU, C_CV, C_CG)]
                + [revh(C_AX), revh(C_AC), revh(C_RX), rev(0), revh(0),
                   pl.BlockSpec((TILE_MIX, LRU_SAVED * GROUP_W), lambda t: (n_tiles - 1 - t, 0)),
                   pl.BlockSpec((TILE_MIX, qkv_w), lambda t: (n_tiles - 1 - t, 0)), revr(d4), revr(d16), rev(0)]
                + [_of_layer(wts[k], l) for k in names])
    small = jax.ShapeDtypeStruct((SUBLANES, GROUP_W), F32)
    sq = jax.ShapeDtypeStruct((GROUP_W, GROUP_W), F32)
    out_shape = [jax.ShapeDtypeStruct((s, D_IN), BF16), small, small, small, sq, sq,
                 jax.ShapeDtypeStruct((N_HEADS, GMLP_CHUNK, GMLP_CHUNK), F32),
                 jax.ShapeDtypeStruct((GMLP_CHUNK, GROUP_W), F32)]
    out_specs = ([pl.BlockSpec((TILE_MIX, D_IN), lambda t: (n_tiles - 1 - t, 0))]
                 + [_full(o.shape) for o in out_shape[1:]])
    return pl.pallas_call(
        body, name="mix_bwd", grid=(n_tiles,),
        in_specs=in_specs, out_specs=out_specs, out_shape=out_shape,
        scratch_shapes=[pltpu.VMEM((SUBLANES, GROUP_W), F32)] * 4 + _stage_scratch(TILE_MIX, qkv_w, 2),
        compiler_params=_params(("arbitrary",)),
    )(dy, dy, dy, *([z] * 12), hl, hl, lru, *dqkv, ddg, *[wts[k] for k in names])


def prepare_small_weights(p):
    tril = jnp.tril(jnp.ones((GMLP_CHUNK, GMLP_CHUNK), dtype=bool))
    ws = jnp.where(tril, p["gmlp_ws"], 0.0).astype(BF16)
    row = lambda a: a[:, None, :]
    eye = jnp.eye(N_HEADS, dtype=F32)
    bd = lambda w: (w[:, :, :, None, :] * eye[None, :, None, :, None]).reshape(-1, GROUP_W, GROUP_W).astype(BF16)
    return dict(
        caw=p["conv_a_w"], crw=p["conv_r_w"], crb=row(p["conv_r_b"]),
        wa=bd(p["lru_wa"]), wx=bd(p["lru_wx"]),
        ba=row(p["lru_ba"]), bx=row(p["lru_bx"]), lam=row(p["lru_lambda"]), gng=row(p["gmlp_norm_g"]),
        ws=ws, wst=jnp.swapaxes(ws, 2, 3),
        bs=jnp.repeat(jnp.swapaxes(p["gmlp_bs"], 1, 2), HEAD_DIM, axis=2))


def _flat(a):
    return a.reshape(a.shape[0] * a.shape[1], a.shape[2])


def _split(a, dil):
    return a.reshape(dil, a.shape[0] // dil, a.shape[1])


def local_step(x, tgt, final_g, depth, chip, layer_weights, projections_done):
    saved = []
    for l in range(depth):
        gain, w_in_l, rest = layer_weights(l, x)
        z, h, *qkvs = in_fwd(x, gain, w_in_l)
        w_out_l, wts = rest(z)
        qkvs = [_flat(q) if q.ndim == 3 else q for q in qkvs]
        attn = []
        for q, d in zip(qkvs, PATTERN_DILS):
            o_p, lse_p = attn_fwd(q, d)
            attn.append((o_p, lse_p) if d == 1 else (_split(o_p, d), _split(lse_p, d)))
        y, hl, o, lse, lse4, lse16, lru = mix_fwd(z, attn, wts, l)
        saved.append(dict(x=x, z=z, h=h, y=y, hl=hl, o=o, qkvs=qkvs, lses=(lse, _flat(lse4), _flat(lse16)), wts=wts, lru=lru,
                          gain=gain, w_in=w_in_l, w_out=w_out_l))
        if l < depth - 1:
            x = out_fwd(y, w_out_l, x)
        else:
            loss, dx, dfg = out_fwd_loss(y, w_out_l, x, final_g[None, :], tgt)
    raw = {k: [None] * depth for k in ("gain", "a", "r", "n", "wa", "wx", "ws", "bs")}
    zero = None
    for l in reversed(range(depth)):
        sv = saved[l]
        dy, ddg, do1, do4, do16, dl1, dl4, dl16 = out_bwd(dx, sv["w_out"], sv["z"], sv["o"])
        g_w_out = grad_w_out(sv["y"], dx)
        dqkv = []
        for q, do, lse, dl, d in zip(sv["qkvs"], (do1, _flat(do4), _flat(do16)), sv["lses"],
                                     (dl1, _flat(dl4), _flat(dl16)), PATTERN_DILS):
            g = attn_bwd(q, do, lse, dl, d)
            dqkv.append(g if d == 1 else _split(g, d))
        dz, ga, gr, gn, gwa, gwx, gws, gbs = mix_bwd(dy, sv["z"], sv["hl"], sv["lru"], dqkv, ddg, sv["wts"], l)
        gain = sv["gain"] if zero is None else sv["gain"] + zero
        zero = projections_done(l, *grad_w_in(sv["h"], dz, chip), g_w_out)
        if l == 0 and zero is not None:
            gain = gain + zero
        dx, dgn = in_bwd(dz, sv["w_in"], sv["x"], gain, dx)
        for k, g in zip(("gain", "a", "r", "n", "wa", "wx", "ws", "bs"), (dgn, ga, gr, gn, gwa, gwx, gws, gbs)):
            raw[k][l] = g
    st = {k: jnp.stack(v) for k, v in raw.items()}
    eye = jnp.eye(N_HEADS, dtype=F32)[None, :, None, :, None]
    diag = lambda g: (g.reshape(depth, N_HEADS, HEAD_DIM, N_HEADS, HEAD_DIM) * eye).sum(axis=3)
    grads = dict(
        norm_g=st["gain"][:, 0], conv_a_w=st["a"][:, :3], conv_r_w=st["r"][:, :4], conv_r_b=st["r"][:, 4],
        lru_ba=st["r"][:, 5], lru_bx=st["r"][:, 6], lru_lambda=st["r"][:, 7], gmlp_norm_g=st["n"][:, 0],
        lru_wa=diag(st["wa"]), lru_wx=diag(st["wx"]), gmlp_ws=st["ws"],
        gmlp_bs=jnp.swapaxes(st["bs"].reshape(depth, GMLP_CHUNK, N_HEADS, HEAD_DIM).sum(-1), 1, 2),
        final_g=dfg[0])
    return loss, dx, grads


MESH = pl.DeviceIdType.MESH
N_CHIPS = 4
N_DEV = 8
ANY = pl.BlockSpec(memory_space=pl.ANY)


def _place():
    x, y, c = lax.axis_index("x"), lax.axis_index("y"), lax.axis_index("c")
    chips = [(1 - x, y), (x, 1 - y), (1 - x, 1 - y)]
    return x, y, c, chips


def _remote(src, dst, ssem, rsem, to):
    return pltpu.make_async_remote_copy(src_ref=src, dst_ref=dst, send_sem=ssem, recv_sem=rsem,
                                        device_id=to, device_id_type=MESH)


HBM = pl.BlockSpec(memory_space=pltpu.HBM)
SEM = pl.BlockSpec(memory_space=pltpu.SEMAPHORE)
DATAFLOW = pltpu.SideEffectType.DATAFLOW_SIDE_EFFECTING
GATHER, SCATTER = "gather", "scatter"


def _chip_copies(mode, src_refs, land_refs, ssem, rsem):
    x, y, c, chips = _place()
    me = 2 * x + y
    n = len(src_refs)
    copies = []
    for k, (cx, cy) in enumerate(chips):
        for a in range(n):
            if mode == GATHER:
                src, dst = src_refs[a], land_refs[a].at[me]
            else:
                src, dst = src_refs[a].at[2 * cx + cy], land_refs[a].at[k]
            copies.append(_remote(src, dst, ssem.at[n * k + a], rsem.at[n * k + a], (cx, cy, c)))
    return copies


def exchange_start(mode, srcs, after, name):
    n = len(srcs)
    if mode == GATHER:
        lands = [lax.empty((N_CHIPS,) + s.shape, s.dtype) for s in srcs]
    else:
        lands = [lax.empty((N_CHIPS - 1,) + s.shape[1:], s.dtype) for s in srcs]
    extra = [] if after is None else [after]

    def body(*refs):
        src_refs, land_refs = refs[:n], refs[n:2 * n]
        ssem, rsem = refs[2 * n + len(extra)], refs[2 * n + len(extra) + 1]
        token = refs[-1]
        for cp in _chip_copies(mode, src_refs, land_refs, ssem, rsem):
            cp.start()
        token[...] = jnp.zeros_like(token)

    arrays = list(srcs) + lands
    return pl.pallas_call(
        body, name=name,
        out_shape=(pltpu.SemaphoreType.DMA((3 * n,)), pltpu.SemaphoreType.DMA((3 * n,)),
                   *[pltpu.HBM(a.shape, a.dtype) for a in arrays], jax.ShapeDtypeStruct((SUBLANES, LANES), F32)),
        in_specs=[HBM] * (2 * n) + [ANY] * len(extra),
        out_specs=(SEM, SEM, *[HBM] * (2 * n), pl.BlockSpec(memory_space=pltpu.VMEM)),
        input_output_aliases={i: 2 + i for i in range(2 * n)},
        compiler_params=pltpu.CompilerParams(has_side_effects=DATAFLOW),
    )(*[pltpu.with_memory_space_constraint(a, pltpu.HBM) for a in arrays], *extra)


def exchange_wait(mode, started, after, name):
    ssem, rsem, *thru, _ = started
    n = len(thru) // 2

    def body(*refs):
        src_refs, land_refs = refs[:n], refs[n:2 * n]
        ssem_ref, rsem_ref = refs[2 * n], refs[2 * n + 1]
        for cp in _chip_copies(mode, src_refs, land_refs, ssem_ref, rsem_ref):
            cp.wait_send()
            cp.wait_recv()

    outs = pl.pallas_call(
        body, name=name,
        out_shape=[pltpu.HBM(a.shape, a.dtype) for a in thru],
        in_specs=[HBM] * (2 * n) + [SEM, SEM, ANY],
        out_specs=[HBM] * (2 * n),
        input_output_aliases={i: i for i in range(2 * n)},
        compiler_params=pltpu.CompilerParams(has_side_effects=DATAFLOW),
    )(*thru, ssem, rsem, after)
    return outs[n:]


def sibling_exchange(p1, p2):
    def body(p1_ref, p2_ref, q1_ref, q2_ref, ssem, rsem):
        x, y, c, _ = _place()
        copies = [_remote(p_ref, q_ref, ssem.at[a], rsem.at[a], (x, y, 1 - c))
                  for a, (p_ref, q_ref) in enumerate(((p1_ref, q1_ref), (p2_ref, q2_ref)))]
        for cp in copies:
            cp.start()
        for cp in copies:
            cp.wait()

    return pl.pallas_call(
        body, name="sibling_exchange",
        in_specs=[ANY, ANY], out_specs=[ANY, ANY],
        out_shape=[jax.ShapeDtypeStruct(p.shape, p.dtype) for p in (p1, p2)],
        scratch_shapes=[pltpu.SemaphoreType.DMA((2,)), pltpu.SemaphoreType.DMA((2,))],
    )(p1, p2)


def all_reduce_small(v):
    r, n = v.shape
    piece = r // N_DEV

    def body(x_ref, out_ref, recv, ssem1, rsem1, ssem2, rsem2):
        x, y, c, _ = _place()
        me = 4 * x + 2 * y + c

        def peer(k):
            px = 1 - x if (k >> 2) & 1 else x
            py = 1 - y if (k >> 1) & 1 else y
            pc = 1 - c if k & 1 else c
            return (px, py, pc), 4 * px + 2 * py + pc

        def rows(ref, d):
            return ref.at[pl.ds(d * piece, piece), :]

        scatter = []
        for k in range(1, N_DEV):
            to, idx = peer(k)
            scatter.append(_remote(rows(x_ref, idx), recv.at[k], ssem1.at[k - 1], rsem1.at[k - 1], to))
            scatter[-1].start()
        acc = rows(x_ref, me)[...]
        for k in range(1, N_DEV):
            scatter[k - 1].wait_recv()
            acc = acc + recv[k]
        rows(out_ref, me)[...] = acc

        gather = []
        for k in range(1, N_DEV):
            to, _ = peer(k)
            gather.append(_remote(rows(out_ref, me), rows(out_ref, me), ssem2.at[k - 1], rsem2.at[k - 1], to))
            gather[-1].start()
        for k in range(1, N_DEV):
            to, idx = peer(k)
            _remote(rows(out_ref, idx), rows(out_ref, idx), ssem2.at[k - 1], rsem2.at[k - 1], to).wait_recv()
        for cp in scatter + gather:
            cp.wait_send()

    return pl.pallas_call(
        body, name="all_reduce_small",
        out_shape=jax.ShapeDtypeStruct((r, n), v.dtype),
        in_specs=[pl.BlockSpec(memory_space=pltpu.VMEM)],
        out_specs=pl.BlockSpec(memory_space=pltpu.VMEM),
        scratch_shapes=[pltpu.VMEM((N_DEV, piece, n), v.dtype)] + [pltpu.SemaphoreType.DMA((N_DEV - 1,))] * 4,
        compiler_params=pltpu.CompilerParams(vmem_limit_bytes=VMEM_LIMIT),
    )(v)


TILE_ROWS = 256


def sum_partials(own, parts):
    k, r, c = parts.shape

    def body(o_ref, p_ref, out_ref):
        acc = o_ref[...]
        for i in range(k):
            acc = acc + p_ref[i].astype(F32)
        out_ref[...] = acc

    row = pl.BlockSpec((TILE_ROWS, c), lambda i: (i, 0))
    return pl.pallas_call(
        body, name="sum_partials", grid=(r // TILE_ROWS,),
        in_specs=[row, pl.BlockSpec((k, TILE_ROWS, c), lambda i: (0, i, 0))],
        out_specs=row,
        out_shape=jax.ShapeDtypeStruct((r, c), F32),
        compiler_params=_params(("parallel",)),
    )(own, parts)


def _adamw_update(w, g, m, v):
    m2 = ADAM_B1 * m + (1.0 - ADAM_B1) * g
    v2 = ADAM_B2 * v + (1.0 - ADAM_B2) * (g * g)
    m_hat = m2 / (1.0 - ADAM_B1 ** ADAM_STEP)
    v_hat = v2 / (1.0 - ADAM_B2 ** ADAM_STEP)
    return -ADAM_LR * (m_hat / (jnp.sqrt(v_hat) + ADAM_EPS) + ADAM_WD * w), m2, v2


def adamw_small(ws, gs, ms, vs):
    n = len(ws)

    def body(*refs):
        ins, outs = refs[:4 * n], refs[4 * n:]
        for i in range(n):
            d, m2, v2 = _adamw_update(ins[i][...], ins[n + i][...], ins[2 * n + i][...], ins[3 * n + i][...])
            outs[3 * i][...] = d
            outs[3 * i + 1][...] = m2
            outs[3 * i + 2][...] = v2

    outs = pl.pallas_call(
        body, name="adamw_small",
        out_shape=[jax.ShapeDtypeStruct(w.shape, F32) for w in ws for _ in range(3)],
    )(*ws, *gs, *ms, *vs)
    return [tuple(outs[3 * i:3 * i + 3]) for i in range(n)]


def adamw(w, ga, gb, m, v):
    n, r, c = w.shape
    tile = max(t for t in range(SUBLANES, TILE_ROWS + 1, SUBLANES) if r % t == 0)

    def body(w_ref, ga_ref, gb_ref, m_ref, v_ref, g_ref, d_ref, m2_ref, v2_ref):
        g = ga_ref[...] + gb_ref[...]
        g_ref[...] = g
        d_ref[...], m2_ref[...], v2_ref[...] = _adamw_update(w_ref[...], g, m_ref[...], v_ref[...])

    spec = pl.BlockSpec((1, tile, c), lambda j, i: (j, i, 0))
    return pl.pallas_call(
        body, name="adamw", grid=(n, r // tile),
        in_specs=[spec] * 5, out_specs=[spec] * 4,
        out_shape=[jax.ShapeDtypeStruct((n, r, c), F32)] * 4,
        compiler_params=_params(("parallel", "parallel")),
    )(w, ga, gb, m, v)


REPLICATED = ("norm_g", "conv_r_b", "lru_wa", "lru_ba", "lru_wx", "lru_bx", "lru_lambda", "gmlp_norm_g",
              "gmlp_ws", "gmlp_bs", "final_g")
CHIP_SHARDED_SMALL = ("conv_a_w", "conv_r_w")
PACK_LANES = 128


def _pack(arrays):
    flat = jnp.concatenate([a.reshape(-1) for a in arrays])
    pad = (-flat.shape[0]) % (TILE_ROWS * PACK_LANES)
    return jnp.pad(flat, (0, pad)).reshape(-1, PACK_LANES)


def _unpack(packed, shapes):
    flat = packed.reshape(-1)
    out, off = [], 0
    for shp in shapes:
        n = math.prod(shp)
        out.append(flat[off:off + n].reshape(shp))
        off += n
    return out


def kernel(x, norm_g, w_in, conv_a_w, conv_r_w, conv_r_b, lru_wa, lru_ba, lru_wx, lru_bx, lru_lambda, gmlp_norm_g, gmlp_ws, gmlp_bs, w_out, final_g, loss_target, m_norm_g, m_w_in, m_conv_a_w, m_conv_r_w, m_conv_r_b, m_lru_wa, m_lru_ba, m_lru_wx, m_lru_bx, m_lru_lambda, m_gmlp_norm_g, m_gmlp_ws, m_gmlp_bs, m_w_out, m_final_g, v_norm_g, v_w_in, v_conv_a_w, v_conv_r_w, v_conv_r_b, v_lru_wa, v_lru_ba, v_lru_wx, v_lru_bx, v_lru_lambda, v_gmlp_norm_g, v_gmlp_ws, v_gmlp_bs, v_w_out, v_final_g):
    names = ("norm_g", "w_in", "conv_a_w", "conv_r_w", "conv_r_b", "lru_wa", "lru_ba", "lru_wx", "lru_bx",
             "lru_lambda", "gmlp_norm_g", "gmlp_ws", "gmlp_bs", "w_out", "final_g")
    w = dict(zip(names, (norm_g, w_in, conv_a_w, conv_r_w, conv_r_b, lru_wa, lru_ba, lru_wx, lru_bx, lru_lambda,
                         gmlp_norm_g, gmlp_ws, gmlp_bs, w_out, final_g)))
    m = dict(zip(names, (m_norm_g, m_w_in, m_conv_a_w, m_conv_r_w, m_conv_r_b, m_lru_wa, m_lru_ba, m_lru_wx, m_lru_bx,
                         m_lru_lambda, m_gmlp_norm_g, m_gmlp_ws, m_gmlp_bs, m_w_out, m_final_g)))
    v = dict(zip(names, (v_norm_g, v_w_in, v_conv_a_w, v_conv_r_w, v_conv_r_b, v_lru_wa, v_lru_ba, v_lru_wx, v_lru_bx,
                         v_lru_lambda, v_gmlp_norm_g, v_gmlp_ws, v_gmlp_bs, v_w_out, v_final_g)))
    depth, _, in_cols = w_in.shape
    out_rows = w_out.shape[1]
    conv_ch = conv_a_w.shape[2]
    chip = 2 * lax.axis_index("x") + lax.axis_index("y")

    taps = conv_a_w.shape[1] + conv_r_w.shape[1]
    w_in_t, m_w_in_t, v_w_in_t = (jnp.swapaxes(a, 1, 2) for a in (w_in, m_w_in, v_w_in))
    w_in_h, w_out_h = w_in_t.astype(BF16), w_out.astype(BF16)
    conv_own = jnp.concatenate([conv_a_w, conv_r_w], axis=1).reshape(depth * taps, conv_ch)
    gathers, token = [], None
    for l in range(depth):
        groups = [[w_in_h[l]], [w_out_h[l], conv_own]] if l == 0 else [[w_in_h[l], w_out_h[l]]]
        gathers.append([])
        for i, srcs in enumerate(groups):
            gathers[l].append(exchange_start(GATHER, srcs, token, f"gather_start_{l}_{i}"))
            token = gathers[l][-1][-1]
    p = dict(w)

    def with_own(land, own):
        return lax.dynamic_update_slice(land, own[None], (chip,) + (0,) * own.ndim)

    def layer_weights(l, x_l):
        lands = list(exchange_wait(GATHER, gathers[l][0], x_l, f"gather_wait_{l}_0"))
        w_in_l = with_own(lands[0], w_in_h[l]).reshape(D_IN, D_MODEL)
        gain = norm_g[l][None, :]
        if l == 0:
            gain = gain + token[0, 0]

        def rest(z_l):
            if l == 0:
                lands.extend(exchange_wait(GATHER, gathers[l][1], z_l, f"gather_wait_{l}_1"))
                conv = with_own(lands[2], conv_own).reshape(N_CHIPS, depth, taps, conv_ch)
                conv = conv.transpose(1, 2, 0, 3).reshape(depth, taps, GROUP_W)
                p["conv_a_w"] = conv[:, :conv_a_w.shape[1]]
                p["conv_r_w"] = conv[:, conv_a_w.shape[1]:]
                p["prepared"] = prepare_small_weights(p)
            return with_own(lands[1], w_out_h[l]).reshape(D_MIX, D_MODEL), p["prepared"]

        return gain, w_in_l, rest

    scatters, owns = [None] * depth, [None] * depth

    def projections_done(l, g_w_in_by_chip, g_w_in_own, g_w_out):
        go = g_w_out.reshape(N_CHIPS, out_rows, D_MODEL)
        owns[l] = (g_w_in_own, lax.dynamic_index_in_dim(go, chip, axis=0, keepdims=False))
        scatters[l] = exchange_start(SCATTER, [g_w_in_by_chip, go.astype(BF16)], None, f"scatter_start_{l}")
        return scatters[l][-1][0, 0]

    loss8, dx, grads = local_step(x[0], loss_target[0], final_g, depth, chip.reshape(1), layer_weights,
                                  projections_done)
    loss = lax.psum(loss8[0, 0], ("x", "y", "c"))

    parts = [exchange_wait(SCATTER, scatters[l], dx, f"scatter_wait_{l}") for l in range(depth)]
    r1 = jnp.concatenate([parts[l][0] for l in range(depth)], axis=1)
    r2 = jnp.concatenate([parts[l][1] for l in range(depth)], axis=1)
    own1 = jnp.concatenate([owns[l][0] for l in range(depth)], axis=0)
    own2 = jnp.concatenate([owns[l][1] for l in range(depth)], axis=0)
    p1 = sum_partials(own1, r1)
    p2 = sum_partials(own2, r2)
    q1, q2 = sibling_exchange(p1, p2)
    res = {}
    res["w_in"] = [jnp.swapaxes(t, 1, 2) for t in
                   adamw(w_in_t, p1.reshape(w_in_t.shape), q1.reshape(w_in_t.shape), m_w_in_t, v_w_in_t)]
    res["w_out"] = adamw(w_out, p2.reshape(w_out.shape), q2.reshape(w_out.shape), m_w_out, v_w_out)

    small = REPLICATED + CHIP_SHARDED_SMALL
    packed = _pack([grads[k] for k in small])
    total = all_reduce_small(packed)
    gs = dict(zip(small, _unpack(total, [grads[k].shape for k in small])))
    for k in CHIP_SHARDED_SMALL:
        gs[k] = lax.dynamic_slice_in_dim(gs[k], chip * conv_ch, conv_ch, axis=2)
    as2d = lambda a: a[None] if a.ndim == 1 else a
    outs = adamw_small(*[[as2d(d[k]) for k in small] for d in (w, gs, m, v)])
    for k, (delta, m2, v2) in zip(small, outs):
        res[k] = [t.reshape(w[k].shape) for t in (gs[k], delta, m2, v2)]

    return (loss, dx[None], *[res[k][0] for k in names], *[res[k][1] for k in names],
            *[res[k][2] for k in names], *[res[k][3] for k in names])
```

```python
import functools
import math

import jax
import jax.numpy as jnp
import numpy as np
from jax import lax
from jax.experimental import pallas as pl
from jax.experimental.pallas import tpu as pltpu

F32 = jnp.float32
BF16 = jnp.bfloat16

D_MODEL = 1024
GROUP_W = 256
N_HEADS = 4
HEAD_DIM = 64
N_CHUNKS = 13
D_IN = N_CHUNKS * GROUP_W
D_MIX = 4 * GROUP_W
NORM_EPS = 1e-6
RG_C = 8.0
GMLP_CHUNK = 128
ATTN_BLOCK = 128
PATTERN_DILS = (1, 4, 16)
N_PATTERNS = len(PATTERN_DILS)
ALIBI_SLOPES = tuple(2.0 ** (-8.0 * (h + 1) / N_HEADS) for h in range(N_HEADS))
ATTN_SCALE = 1.0 / math.sqrt(HEAD_DIM)
NEG_BIG = -1e30

ADAM_LR = 0.001
ADAM_B1 = 0.9
ADAM_B2 = 0.999
ADAM_EPS = 1e-08
ADAM_WD = 0.01
ADAM_STEP = 10

C_AX, C_AB, C_AC, C_AG, C_RX, C_RG, C_CU, C_CV, C_CG, C_DQ, C_DK, C_DV, C_DG = range(13)

SUBLANES = 8
LANES = 128
VMEM_LIMIT = 56 * 1024 * 1024
TILE_IN = 512
TILE_MIX = 512
TILE_DW = 1024
TILE_DW_OUT = 2048
ATTN_QB = 4
GELU_K0 = math.sqrt(2.0 / math.pi)
GELU_K1 = 0.044715


def _params(sem):
    return pltpu.CompilerParams(dimension_semantics=sem, vmem_limit_bytes=VMEM_LIMIT)


def _silu(x):
    return x * jax.nn.sigmoid(x)


def _dsilu(x):
    s = jax.nn.sigmoid(x)
    return s * (1.0 + x * (1.0 - s))


def _gelu(x):
    return 0.5 * x * (1.0 + jnp.tanh(GELU_K0 * (x + GELU_K1 * x * x * x)))


def _gelu_and_grad(x):
    t = jnp.tanh(GELU_K0 * (x + GELU_K1 * x * x * x))
    g = 0.5 * x * (1.0 + t)
    dg = 0.5 * (1.0 + t) + 0.5 * x * (1.0 - t * t) * GELU_K0 * (1.0 + 3.0 * GELU_K1 * x * x)
    return g, dg


def _neg_expm1(x):
    series = x * (1.0 + x * (0.5 + x * (1.0 / 6.0 + x * (1.0 / 24.0 + x * (1.0 / 120.0)))))
    return -jnp.where(x > -0.05, series, jnp.exp(x) - 1.0)


def _shift_down(v, halo, k):
    r = pltpu.roll(v, k, 0)
    rh = pltpu.roll(halo, k, 0)
    row = lax.broadcasted_iota(jnp.int32, halo.shape, 0)
    top = jnp.where(row < k, rh, r[:SUBLANES])
    return jnp.concatenate([top, r[SUBLANES:]], axis=0)


def _shift_up(v, halo, k):
    t = v.shape[0]
    r = pltpu.roll(v, t - k, 0)
    rh = pltpu.roll(halo, SUBLANES - k, 0)
    row = lax.broadcasted_iota(jnp.int32, halo.shape, 0)
    bot = jnp.where(row >= SUBLANES - k, rh, r[t - SUBLANES:])
    return jnp.concatenate([r[:t - SUBLANES], bot], axis=0)


def _scan_causal(a, b, h_in):
    t = a.shape[0]
    row8 = lax.broadcasted_iota(jnp.int32, a.shape, 0) % SUBLANES
    d = 1
    while d < SUBLANES:
        m = row8 >= d
        a_s = jnp.where(m, pltpu.roll(a, d, 0), 1.0)
        b_s = jnp.where(m, pltpu.roll(b, d, 0), 0.0)
        b = a * b_s + b
        a = a * a_s
        d *= 2
    out, carry = [], h_in
    for g in range(t // SUBLANES):
        sl = slice(g * SUBLANES, (g + 1) * SUBLANES)
        hg = b[sl] + a[sl] * carry
        out.append(hg)
        carry = hg[SUBLANES - 1:SUBLANES]
    return jnp.concatenate(out, axis=0)


def _scan_anticausal(a, b, g_in):
    t = a.shape[0]
    row8 = lax.broadcasted_iota(jnp.int32, a.shape, 0) % SUBLANES
    d = 1
    while d < SUBLANES:
        m = row8 < SUBLANES - d
        a_s = jnp.where(m, pltpu.roll(a, t - d, 0), 1.0)
        b_s = jnp.where(m, pltpu.roll(b, t - d, 0), 0.0)
        b = a * b_s + b
        a = a * a_s
        d *= 2
    out, carry = [], g_in
    for g in reversed(range(t // SUBLANES)):
        sl = slice(g * SUBLANES, (g + 1) * SUBLANES)
        gg = b[sl] + a[sl] * carry
        out.append(gg)
        carry = gg[0:1]
    return jnp.concatenate(out[::-1], axis=0)


def _head_of_lane(shape):
    return lax.broadcasted_iota(jnp.int32, shape, len(shape) - 1) // HEAD_DIM


def _per_head_lanes(cols):
    t = cols[0].shape[0]
    lane = lax.broadcasted_iota(jnp.int32, (t, LANES), 1)
    out = jnp.zeros((t, LANES), F32)
    for h, col in enumerate(cols):
        out = jnp.where(lane == h, col, out)
    return out


def _put_row(acc_shape, k, row_vec):
    row = lax.broadcasted_iota(jnp.int32, acc_shape, 0)
    return jnp.where(row == k, jnp.broadcast_to(row_vec, acc_shape), 0.0)


def _dot(a, b):
    return jnp.dot(a, b, preferred_element_type=F32)


def _dot_nt(a, b):
    return lax.dot_general(a, b, (((1,), (1,)), ((), ())), preferred_element_type=F32)


def _dot_tn(a, b):
    return lax.dot_general(a, b, (((0,), (0,)), ((), ())), preferred_element_type=F32)


def _deinterleave_store(val, stage, outs):
    t, c = val.shape
    for hh in range(c // LANES):
        stage[hh][...] = val[:, hh * LANES:(hh + 1) * LANES].astype(F32)
    for dil, ref in outs:
        for r in range(dil):
            for hh in range(c // LANES):
                ref[r, :, hh * LANES:(hh + 1) * LANES] = stage[hh][pl.ds(r, t // dil, stride=dil), :].astype(ref.dtype)


def _interleave_load(ref, dil, stage):
    _, n, c = ref.shape
    for r in range(dil):
        for hh in range(c // LANES):
            stage[hh][pl.ds(r, n, stride=dil), :] = ref[r, :, hh * LANES:(hh + 1) * LANES].astype(F32)
    return jnp.concatenate([stage[hh][...] for hh in range(c // LANES)], axis=1)


def _stage_scratch(tile, cols, copies):
    return [pltpu.VMEM((tile, LANES), F32)] * (copies * (cols // LANES))


def _by_residue(s, dil, cols, dtype):
    return jax.ShapeDtypeStruct((dil, s // dil, cols), dtype)


def _residue_block(dil, tile, cols):
    return pl.BlockSpec((dil, tile // dil, cols), lambda i: (0, i, 0))


def in_fwd(x, g, w):
    s = x.shape[0]
    qkv_w = 3 * GROUP_W

    def body(x_ref, g_ref, w_ref, z_ref, h_ref, qkv1_ref, qkv4_ref, qkv16_ref, *stage):
        xv = x_ref[...]
        rs = lax.rsqrt(jnp.mean(xv * xv, axis=-1, keepdims=True) + NORM_EPS)
        h = (xv * rs * g_ref[...]).astype(BF16)
        h_ref[...] = h
        z = _dot_nt(h, w_ref[...])
        z_ref[...] = z
        qkv = z[:, C_DQ * GROUP_W:(C_DV + 1) * GROUP_W]
        qkv1_ref[...] = qkv.astype(BF16)
        _deinterleave_store(qkv, stage, ((PATTERN_DILS[1], qkv4_ref), (PATTERN_DILS[2], qkv16_ref)))

    return pl.pallas_call(
        body, name="in_fwd", grid=(s // TILE_IN,),
        in_specs=[pl.BlockSpec((TILE_IN, D_MODEL), lambda i: (i, 0)),
                  pl.BlockSpec((1, D_MODEL), lambda i: (0, 0)),
                  pl.BlockSpec((D_IN, D_MODEL), lambda i: (0, 0))],
        out_specs=[pl.BlockSpec((TILE_IN, D_IN), lambda i: (i, 0)),
                   pl.BlockSpec((TILE_IN, D_MODEL), lambda i: (i, 0)),
                   pl.BlockSpec((TILE_IN, qkv_w), lambda i: (i, 0)),
                   _residue_block(PATTERN_DILS[1], TILE_IN, qkv_w),
                   _residue_block(PATTERN_DILS[2], TILE_IN, qkv_w)],
        out_shape=[jax.ShapeDtypeStruct((s, D_IN), F32), jax.ShapeDtypeStruct((s, D_MODEL), BF16),
                   jax.ShapeDtypeStruct((s, qkv_w), BF16),
                   _by_residue(s, PATTERN_DILS[1], qkv_w, BF16), _by_residue(s, PATTERN_DILS[2], qkv_w, BF16)],
        scratch_shapes=_stage_scratch(TILE_IN, qkv_w, 1),
        compiler_params=_params(("parallel",)),
    )(x, g, w)


def out_fwd(y, w, x):
    s = x.shape[0]

    def body(y_ref, w_ref, x_ref, o_ref):
        o_ref[...] = x_ref[...] + _dot(y_ref[...], w_ref[...])

    return pl.pallas_call(
        body, name="out_fwd", grid=(s // TILE_IN,),
        in_specs=[pl.BlockSpec((TILE_IN, D_MIX), lambda i: (i, 0)),
                  pl.BlockSpec((D_MIX, D_MODEL), lambda i: (0, 0)),
                  pl.BlockSpec((TILE_IN, D_MODEL), lambda i: (i, 0))],
        out_specs=pl.BlockSpec((TILE_IN, D_MODEL), lambda i: (i, 0)),
        out_shape=jax.ShapeDtypeStruct((s, D_MODEL), F32),
        compiler_params=_params(("parallel",)),
    )(y, w, x)


def out_bwd(dx, w, z, o):
    s = dx.shape[0]
    abc = 3 * GROUP_W

    def body(dx_ref, w_ref, dg_ref, o_ref, dy_ref, ddg_ref, do1_ref, do4_ref, do16_ref, dl1_ref, dl4_ref, dl16_ref,
             *stage):
        stage_a, stage_b = stage[:2], stage[2:]
        dy = _dot_nt(dx_ref[...].astype(BF16), w_ref[...])
        dy_ref[...] = dy[:, :abc]
        dyd = dy[:, abc:]
        head = _head_of_lane((TILE_IN, GROUP_W))
        dg = dg_ref[...]
        o = o_ref[...]
        do = dyd * _silu(dg)
        ddg_ref[...] = dyd * o * _dsilu(dg)
        prod = do * o
        dl = _per_head_lanes([jnp.sum(jnp.where(head == h, prod, 0.0), axis=-1, keepdims=True)
                              for h in range(N_HEADS)])
        do1_ref[...] = do.astype(BF16)
        dl1_ref[...] = dl
        _deinterleave_store(do, stage_a, ((PATTERN_DILS[1], do4_ref), (PATTERN_DILS[2], do16_ref)))
        _deinterleave_store(dl, stage_b, ((PATTERN_DILS[1], dl4_ref), (PATTERN_DILS[2], dl16_ref)))

    row = pl.BlockSpec((TILE_IN, GROUP_W), lambda i: (i, 0))
    r4 = _residue_block(PATTERN_DILS[1], TILE_IN, GROUP_W)
    r16 = _residue_block(PATTERN_DILS[2], TILE_IN, GROUP_W)
    crow = pl.BlockSpec((TILE_IN, LANES), lambda i: (i, 0))
    c4 = _residue_block(PATTERN_DILS[1], TILE_IN, LANES)
    c16 = _residue_block(PATTERN_DILS[2], TILE_IN, LANES)
    return pl.pallas_call(
        body, name="out_bwd", grid=(s // TILE_IN,),
        in_specs=[pl.BlockSpec((TILE_IN, D_MODEL), lambda i: (i, 0)),
                  pl.BlockSpec((D_MIX, D_MODEL), lambda i: (0, 0)),
                  pl.BlockSpec((TILE_IN, GROUP_W), lambda i: (i, C_DG)), row],
        out_specs=[pl.BlockSpec((TILE_IN, abc), lambda i: (i, 0)), row, row, r4, r16, crow, c4, c16],
        out_shape=[jax.ShapeDtypeStruct((s, abc), F32), jax.ShapeDtypeStruct((s, GROUP_W), F32),
                   jax.ShapeDtypeStruct((s, GROUP_W), BF16),
                   _by_residue(s, PATTERN_DILS[1], GROUP_W, BF16), _by_residue(s, PATTERN_DILS[2], GROUP_W, BF16),
                   jax.ShapeDtypeStruct((s, LANES), F32),
                   _by_residue(s, PATTERN_DILS[1], LANES, F32), _by_residue(s, PATTERN_DILS[2], LANES, F32)],
        scratch_shapes=_stage_scratch(TILE_IN, GROUP_W, 1) + _stage_scratch(TILE_IN, LANES, 1),
        compiler_params=_params(("parallel",)),
    )(dx, w, z, o)


def in_bwd(dz, w, x, g, dx_next):
    s = x.shape[0]

    def body(dz_ref, w_ref, x_ref, g_ref, dxn_ref, dx_ref, dg_ref):
        @pl.when(pl.program_id(0) == 0)
        def _():
            dg_ref[...] = jnp.zeros_like(dg_ref)

        dh = _dot(dz_ref[...], w_ref[...])
        xv = x_ref[...]
        rs = lax.rsqrt(jnp.mean(xv * xv, axis=-1, keepdims=True) + NORM_EPS)
        xh = xv * rs
        dg_ref[...] += _put_row(dg_ref.shape, 0, jnp.sum(dh * xh, axis=0, keepdims=True))
        dn = dh * g_ref[...]
        dx_ref[...] = dxn_ref[...] + rs * (dn - xh * jnp.mean(dn * xh, axis=-1, keepdims=True))

    return pl.pallas_call(
        body, name="in_bwd", grid=(s // TILE_IN,),
        in_specs=[pl.BlockSpec((TILE_IN, D_IN), lambda i: (i, 0)),
                  pl.BlockSpec((D_IN, D_MODEL), lambda i: (0, 0)),
                  pl.BlockSpec((TILE_IN, D_MODEL), lambda i: (i, 0)),
                  pl.BlockSpec((1, D_MODEL), lambda i: (0, 0)),
                  pl.BlockSpec((TILE_IN, D_MODEL), lambda i: (i, 0))],
        out_specs=[pl.BlockSpec((TILE_IN, D_MODEL), lambda i: (i, 0)),
                   pl.BlockSpec((SUBLANES, D_MODEL), lambda i: (0, 0))],
        out_shape=[jax.ShapeDtypeStruct((s, D_MODEL), F32), jax.ShapeDtypeStruct((SUBLANES, D_MODEL), F32)],
        compiler_params=_params(("arbitrary",)),
    )(dz, w, x, g, dx_next)


def grad_w_out(y, dx):
    s = y.shape[0]

    def body(y_ref, dx_ref, o_ref):
        @pl.when(pl.program_id(0) == 0)
        def _():
            o_ref[...] = jnp.zeros_like(o_ref)

        o_ref[...] += _dot_tn(y_ref[...], dx_ref[...].astype(BF16))

    return pl.pallas_call(
        body, name="grad_w_out", grid=(s // TILE_DW_OUT,),
        in_specs=[pl.BlockSpec((TILE_DW_OUT, D_MIX), lambda k: (k, 0)),
                  pl.BlockSpec((TILE_DW_OUT, D_MODEL), lambda k: (k, 0))],
        out_specs=pl.BlockSpec((D_MIX, D_MODEL), lambda k: (0, 0)),
        out_shape=jax.ShapeDtypeStruct((D_MIX, D_MODEL), F32),
        compiler_params=_params(("arbitrary",)),
    )(y, dx)


def grad_w_in(h, dz, chip):
    s = h.shape[0]
    rows = D_IN // N_CHIPS

    def body(chip_ref, h_ref, dz_ref, staged_ref, own_ref, acc):
        k = pl.program_id(0)

        @pl.when(k == 0)
        def _():
            acc[...] = jnp.zeros_like(acc)

        acc[...] += _dot_tn(dz_ref[...], h_ref[...])

        @pl.when(k == s // TILE_DW - 1)
        def _():
            for j in range(N_CHIPS):
                part = acc[j * rows:(j + 1) * rows, :]
                staged_ref[j] = part.astype(BF16)

                @pl.when(chip_ref[0] == j)
                def _():
                    own_ref[...] = part

    return pl.pallas_call(
        body, name="grad_w_in",
        grid_spec=pltpu.PrefetchScalarGridSpec(
            num_scalar_prefetch=1, grid=(s // TILE_DW,),
            in_specs=[pl.BlockSpec((TILE_DW, D_MODEL), lambda k, c: (k, 0)),
                      pl.BlockSpec((TILE_DW, D_IN), lambda k, c: (k, 0))],
            out_specs=[pl.BlockSpec((N_CHIPS, rows, D_MODEL), lambda k, c: (0, 0, 0)),
                       pl.BlockSpec((rows, D_MODEL), lambda k, c: (0, 0))],
            scratch_shapes=[pltpu.VMEM((D_IN, D_MODEL), F32)]),
        out_shape=[jax.ShapeDtypeStruct((N_CHIPS, rows, D_MODEL), BF16), jax.ShapeDtypeStruct((rows, D_MODEL), F32)],
        compiler_params=_params(("arbitrary",)),
    )(chip, h, dz)


def out_fwd_loss(y, w, x, g, tgt):
    s = x.shape[0]

    def body(y_ref, w_ref, x_ref, g_ref, t_ref, l_ref, dx_ref, dg_ref):
        @pl.when(pl.program_id(0) == 0)
        def _():
            l_ref[...] = jnp.zeros_like(l_ref)
            dg_ref[...] = jnp.zeros_like(dg_ref)

        xv = x_ref[...] + _dot(y_ref[...], w_ref[...])
        gv = g_ref[...]
        rs = lax.rsqrt(jnp.mean(xv * xv, axis=-1, keepdims=True) + NORM_EPS)
        xh = xv * rs
        e = xh * gv - t_ref[...]
        part = 0.5 * jnp.sum(jnp.mean(e * e, axis=-1, keepdims=True), axis=0, keepdims=True)
        l_ref[...] += jnp.broadcast_to(part, l_ref.shape)
        dy = e * (1.0 / D_MODEL)
        dg_ref[...] += _put_row(dg_ref.shape, 0, jnp.sum(dy * xh, axis=0, keepdims=True))
        dn = dy * gv
        dx_ref[...] = rs * (dn - xh * jnp.mean(dn * xh, axis=-1, keepdims=True))

    return pl.pallas_call(
        body, name="out_fwd_loss", grid=(s // TILE_IN,),
        in_specs=[pl.BlockSpec((TILE_IN, D_MIX), lambda i: (i, 0)),
                  pl.BlockSpec((D_MIX, D_MODEL), lambda i: (0, 0)),
                  pl.BlockSpec((TILE_IN, D_MODEL), lambda i: (i, 0)),
                  pl.BlockSpec((1, D_MODEL), lambda i: (0, 0)),
                  pl.BlockSpec((TILE_IN, D_MODEL), lambda i: (i, 0))],
        out_specs=[pl.BlockSpec((SUBLANES, 128), lambda i: (0, 0)),
                   pl.BlockSpec((TILE_IN, D_MODEL), lambda i: (i, 0)),
                   pl.BlockSpec((SUBLANES, D_MODEL), lambda i: (0, 0))],
        out_shape=[jax.ShapeDtypeStruct((SUBLANES, 128), F32), jax.ShapeDtypeStruct((s, D_MODEL), F32),
                   jax.ShapeDtypeStruct((SUBLANES, D_MODEL), F32)],
        compiler_params=_params(("arbitrary",)),
    )(y, w, x, g, tgt)


def _attn_bias(dil):
    qi = np.arange(ATTN_BLOCK)[:, None]
    ki = np.arange(2 * ATTN_BLOCK)[None, :]
    delta = qi + ATTN_BLOCK - ki
    band = (delta >= 0) & (delta <= ATTN_BLOCK)
    out = np.empty((2, N_HEADS, ATTN_BLOCK, 2 * ATTN_BLOCK), np.float32)
    for f in range(2):
        ok = band & ((ki >= ATTN_BLOCK) | (f == 0))
        for h in range(N_HEADS):
            out[f, h] = np.where(ok, -ALIBI_SLOPES[h] * dil * delta, NEG_BIG)
    return jnp.asarray(out.reshape(2, N_HEADS * ATTN_BLOCK, 2 * ATTN_BLOCK))


def _stack_heads(a, head):
    return jnp.concatenate([jnp.where(head == h, a, jnp.zeros_like(a)) for h in range(N_HEADS)], axis=0)


def _unstack_heads(a, head):
    out = a[:ATTN_BLOCK]
    for h in range(1, N_HEADS):
        out = jnp.where(head == h, a[h * ATTN_BLOCK:(h + 1) * ATTN_BLOCK], out)
    return out


def _head_column(a):
    return jnp.concatenate([a[:, h:h + 1] for h in range(N_HEADS)], axis=0)


def _attn_specs(n_blocks):
    rows = ATTN_QB * ATTN_BLOCK
    cur = lambda c, w=GROUP_W: pl.BlockSpec((rows, w), lambda n, c=c: (n, c))
    prev = lambda c: pl.BlockSpec((ATTN_BLOCK, GROUP_W), lambda n, c=c: (jnp.maximum(n * ATTN_QB - 1, 0), c))
    nxt = lambda c, w=GROUP_W: pl.BlockSpec((ATTN_BLOCK, w),
                                            lambda n, c=c: (jnp.minimum(n * ATTN_QB + ATTN_QB, n_blocks - 1), c))
    return cur, prev, nxt


def _keys(kp_ref, k_ref, j):
    prev = kp_ref[...] if j == 0 else k_ref[(j - 1) * ATTN_BLOCK:j * ATTN_BLOCK, :]
    return jnp.concatenate([prev, k_ref[j * ATTN_BLOCK:(j + 1) * ATTN_BLOCK, :]], axis=0)


def attn_fwd(qkv, dil):
    s = qkv.shape[0]
    n_blocks = s // ATTN_BLOCK
    bps = n_blocks // dil
    rows = ATTN_QB * ATTN_BLOCK

    def body(q_ref, kp_ref, k_ref, vp_ref, v_ref, bias_ref, o_ref, lse_ref):
        n = pl.program_id(0)
        head = _head_of_lane((ATTN_BLOCK, GROUP_W))
        for j in range(ATTN_QB):
            sl = slice(j * ATTN_BLOCK, (j + 1) * ATTN_BLOCK)
            first = (((n * ATTN_QB + j) % bps) == 0).astype(jnp.int32)
            qs = _stack_heads(q_ref[sl, :], head)
            sc = _dot_nt(qs, _keys(kp_ref, k_ref, j)) * ATTN_SCALE + bias_ref[first]
            m = jnp.max(sc, axis=-1, keepdims=True)
            pr = jnp.exp(sc - m)
            l = jnp.sum(pr, axis=-1, keepdims=True)
            oh = _dot(pr.astype(BF16), _keys(vp_ref, v_ref, j)) / l
            o_ref[sl, :] = _unstack_heads(oh, head)
            lse_ref[sl, :] = _unstack_heads(jnp.broadcast_to(m + jnp.log(l), oh.shape), head)

    cur, prev, _ = _attn_specs(n_blocks)
    bias = _attn_bias(dil)
    out = jax.ShapeDtypeStruct((s, GROUP_W), F32)
    return pl.pallas_call(
        body, name=f"attn_fwd_d{dil}", grid=(n_blocks // ATTN_QB,),
        in_specs=[cur(0), prev(1), cur(1), prev(2), cur(2), pl.BlockSpec(bias.shape, lambda n: (0, 0, 0))],
        out_specs=[cur(0), cur(0)],
        out_shape=[out, out],
        compiler_params=_params(("parallel",)),
    )(qkv, qkv, qkv, qkv, qkv, bias)


def attn_bwd(qkv, do, lse, dlt, dil):
    s = qkv.shape[0]
    n_blocks = s // ATTN_BLOCK
    bps = n_blocks // dil
    rows = ATTN_QB * ATTN_BLOCK

    def body(q_ref, qn_ref, kp_ref, k_ref, vp_ref, v_ref, do_ref, don_ref, lse_ref, lsen_ref, dl_ref, dln_ref,
             bias_ref, out_ref, dk_acc, dv_acc):
        n = pl.program_id(0)
        head = _head_of_lane((ATTN_BLOCK, GROUP_W))
        dk_acc[...] = jnp.zeros_like(dk_acc)
        dv_acc[...] = jnp.zeros_like(dv_acc)

        def pair(qj, doj, lsej, dlj, kk, vv, bias, keep):
            qs = _stack_heads(qj, head)
            dos = _stack_heads(doj, head)
            sc = _dot_nt(qs, kk) * ATTN_SCALE + bias
            if keep is None:
                pr = jnp.exp(sc - _head_column(lsej))
            else:
                pr = jnp.exp(jnp.minimum(sc - _head_column(lsej), 0.0)) * keep
            dp = _dot_nt(dos, vv)
            ds = (pr * (dp - _head_column(dlj)) * ATTN_SCALE).astype(BF16)
            return ds, _dot_tn(ds, qs), _dot_tn(pr.astype(BF16), dos)

        for j in range(ATTN_QB):
            sl = slice(j * ATTN_BLOCK, (j + 1) * ATTN_BLOCK)
            first = (((n * ATTN_QB + j) % bps) == 0).astype(jnp.int32)
            kk = _keys(kp_ref, k_ref, j)
            ds, dks, dvs = pair(q_ref[sl, :], do_ref[sl, :], lse_ref[sl, :], dl_ref[sl, :],
                                kk, _keys(vp_ref, v_ref, j), bias_ref[first], None)
            out_ref[sl, 0:GROUP_W] = _unstack_heads(_dot(ds, kk), head)
            acc = slice(j * ATTN_BLOCK, (j + 2) * ATTN_BLOCK)
            dk_acc[acc, :] += dks
            dv_acc[acc, :] += dvs

        nxt = n * ATTN_QB + ATTN_QB
        valid = ((nxt < n_blocks) & ((nxt % bps) != 0)).astype(F32)
        last = slice((ATTN_QB - 1) * ATTN_BLOCK, ATTN_QB * ATTN_BLOCK)
        _, dks, dvs = pair(qn_ref[...], don_ref[...], lsen_ref[...], dln_ref[...], k_ref[last, :], v_ref[last, :],
                           bias_ref[0][:, :ATTN_BLOCK], valid)
        acc = slice(ATTN_QB * ATTN_BLOCK, (ATTN_QB + 1) * ATTN_BLOCK)
        dk_acc[acc, :] += dks
        dv_acc[acc, :] += dvs
        out_ref[:, GROUP_W:2 * GROUP_W] = dk_acc[ATTN_BLOCK:, :]
        out_ref[:, 2 * GROUP_W:3 * GROUP_W] = dv_acc[ATTN_BLOCK:, :]

    cur, prev, nxt = _attn_specs(n_blocks)
    bias = _attn_bias(dil)
    return pl.pallas_call(
        body, name=f"attn_bwd_d{dil}", grid=(n_blocks // ATTN_QB,),
        in_specs=[cur(0), nxt(0), prev(1), cur(1), prev(2), cur(2), cur(0), nxt(0),
                  cur(0, LANES), nxt(0, LANES), cur(0, LANES), nxt(0, LANES),
                  pl.BlockSpec(bias.shape, lambda n: (0, 0, 0))],
        out_specs=pl.BlockSpec((rows, 3 * GROUP_W), lambda n: (n, 0)),
        out_shape=jax.ShapeDtypeStruct((s, 3 * GROUP_W), F32),
        scratch_shapes=[pltpu.VMEM(((ATTN_QB + 1) * ATTN_BLOCK, GROUP_W), F32),
                        pltpu.VMEM(((ATTN_QB + 1) * ATTN_BLOCK, GROUP_W), F32)],
        compiler_params=_params(("parallel",)),
    )(qkv, qkv, qkv, qkv, qkv, qkv, do, do, lse, lse, dlt, dlt, bias)


def _zcol(c):
    return pl.BlockSpec((TILE_MIX, GROUP_W), lambda i, c=c: (i, c))


def _zhalo(c):
    per = TILE_MIX // SUBLANES
    return pl.BlockSpec((SUBLANES, GROUP_W), lambda i, c=c: (jnp.maximum(i * per - 1, 0), c))


def _full(shape):
    return pl.BlockSpec(shape, lambda i: tuple(0 for _ in shape))


def _of_layer(a, l):
    rest = a.shape[1:]
    return pl.BlockSpec((None,) + rest, lambda i: (l,) + tuple(0 for _ in rest))


def _softplus_neg(lam):
    nl = -lam
    return jnp.maximum(nl, 0.0) + jnp.log1p(jnp.exp(-jnp.abs(nl)))


def _lru_gates(xb, wa_ref, wx_ref, ba, bx, lam):
    xbb = xb.astype(BF16)
    r = jax.nn.sigmoid(_dot(xbb, wa_ref[...]) + ba)
    ig = jax.nn.sigmoid(_dot(xbb, wx_ref[...]) + bx)
    log_a = (-RG_C * r) * _softplus_neg(lam)
    a = jnp.exp(log_a)
    mult = jnp.sqrt(_neg_expm1(2.0 * log_a))
    return r, ig, a, mult


LRU_SAVED = 5


def _gmlp_spatial(ws_ref, vvb, head):
    outs = []
    for j in range(vvb.shape[0] // GMLP_CHUNK):
        blk = vvb[j * GMLP_CHUNK:(j + 1) * GMLP_CHUNK, :]
        acc = jnp.zeros((GMLP_CHUNK, GROUP_W), F32)
        for h in range(N_HEADS):
            acc = jnp.where(head[:GMLP_CHUNK] == h, _dot(ws_ref[h], blk), acc)
        outs.append(acc)
    return jnp.concatenate(outs, axis=0)


def mix_fwd(z, attn, wts, l):
    s = z.shape[0]
    d4, d16 = PATTERN_DILS[1], PATTERN_DILS[2]

    def body(ax_ref, ab_ref, ac_ref, ag_ref, rx_ref, rg_ref, cu_ref, cv_ref, cg_ref, dg_ref,
             axh_ref, ach_ref, rxh_ref, o1_ref, l1_ref, o4_ref, l4_ref, o16_ref, l16_ref,
             caw_ref, crw_ref, crb_ref, wa_ref, wx_ref, ba_ref, bx_ref, lam_ref, gng_ref, ws_ref, bs_ref,
             y_ref, hl_ref, o_ref, lse_ref, lse4_ref, lse16_ref, lru_ref, carry, *stage):
        st_a, st_b, st_c, st_d, st_e = (stage[2 * k:2 * k + 2] for k in range(5))
        i = pl.program_id(0)

        @pl.when(i == 0)
        def _():
            carry[...] = jnp.zeros_like(carry)

        nz = (i > 0).astype(F32)
        head = _head_of_lane((TILE_MIX, GROUP_W))

        pa = ac_ref[...] * ax_ref[...]
        pah = ach_ref[...] * axh_ref[...] * nz
        cv = caw_ref[2:3, :] * pa + caw_ref[1:2, :] * _shift_down(pa, pah, 1) + caw_ref[0:1, :] * _shift_down(pa, pah, 2)
        y_ref[:, 0:GROUP_W] = (ab_ref[...] * cv * _silu(ag_ref[...])).astype(BF16)

        rx = rx_ref[...]
        rxh = rxh_ref[...] * nz
        xb = (crw_ref[3:4, :] * rx + crw_ref[2:3, :] * _shift_down(rx, rxh, 1) + crw_ref[1:2, :] * _shift_down(rx, rxh, 2)
              + crw_ref[0:1, :] * _shift_down(rx, rxh, 3) + crb_ref[...])
        r, ig, a, mult = _lru_gates(xb, wa_ref, wx_ref, ba_ref[...], bx_ref[...], lam_ref[...])
        for k, val in enumerate((xb, r, ig, a, mult)):
            lru_ref[:, k * GROUP_W:(k + 1) * GROUP_W] = val
        hl = _scan_causal(a, mult * (ig * xb), carry[SUBLANES - 1:SUBLANES, :])
        hl_ref[...] = hl
        carry[...] = hl[TILE_MIX - SUBLANES:, :]
        y_ref[:, GROUP_W:2 * GROUP_W] = (hl * _silu(rg_ref[...])).astype(BF16)

        u = _gelu(cu_ref[...])
        gv = _gelu(cv_ref[...])
        rs = lax.rsqrt(jnp.mean(gv * gv, axis=-1, keepdims=True) + NORM_EPS)
        vvb = (gv * rs * gng_ref[...]).astype(BF16)
        sp = _gmlp_spatial(ws_ref, vvb, head) + jnp.concatenate([bs_ref[...]] * (TILE_MIX // GMLP_CHUNK), axis=0)
        y_ref[:, 2 * GROUP_W:3 * GROUP_W] = (u * sp * _silu(cg_ref[...])).astype(BF16)

        ops = (o1_ref[...], _interleave_load(o4_ref, d4, st_a), _interleave_load(o16_ref, d16, st_b))
        lps = (l1_ref[...], _interleave_load(l4_ref, d4, st_c), _interleave_load(l16_ref, d16, st_d))
        m = jnp.maximum(jnp.maximum(lps[0], lps[1]), lps[2])
        zsum = jnp.zeros_like(m)
        o = jnp.zeros_like(m)
        for op, lp in zip(ops, lps):
            w = jnp.exp(lp - m)
            zsum = zsum + w
            o = o + w * op
        o = o / zsum
        lse = m + jnp.log(zsum)
        lse = _per_head_lanes([lse[:, h * HEAD_DIM:h * HEAD_DIM + 1] for h in range(N_HEADS)])
        o_ref[...] = o
        lse_ref[...] = lse
        _deinterleave_store(lse, st_e, ((d4, lse4_ref), (d16, lse16_ref)))
        y_ref[:, 3 * GROUP_W:4 * GROUP_W] = (o * _silu(dg_ref[...])).astype(BF16)

    row = pl.BlockSpec((TILE_MIX, GROUP_W), lambda i: (i, 0))
    r4 = _residue_block(d4, TILE_MIX, GROUP_W)
    r16 = _residue_block(d16, TILE_MIX, GROUP_W)
    crow = pl.BlockSpec((TILE_MIX, LANES), lambda i: (i, 0))
    c4 = _residue_block(d4, TILE_MIX, LANES)
    c16 = _residue_block(d16, TILE_MIX, LANES)
    names = ("caw", "crw", "crb", "wa", "wx", "ba", "bx", "lam", "gng", "ws", "bs")
    in_specs = ([_zcol(c) for c in (C_AX, C_AB, C_AC, C_AG, C_RX, C_RG, C_CU, C_CV, C_CG, C_DG)]
                + [_zhalo(C_AX), _zhalo(C_AC), _zhalo(C_RX), row, row, r4, r4, r16, r16]
                + [_of_layer(wts[k], l) for k in names])
    return pl.pallas_call(
        body, name="mix_fwd", grid=(s // TILE_MIX,),
        in_specs=in_specs,
        out_specs=[pl.BlockSpec((TILE_MIX, D_MIX), lambda i: (i, 0)), row, row, crow, c4, c16,
                   pl.BlockSpec((TILE_MIX, LRU_SAVED * GROUP_W), lambda i: (i, 0))],
        out_shape=([jax.ShapeDtypeStruct((s, D_MIX), BF16)] + [jax.ShapeDtypeStruct((s, GROUP_W), F32)] * 2
                   + [jax.ShapeDtypeStruct((s, LANES), F32), _by_residue(s, d4, LANES, F32),
                      _by_residue(s, d16, LANES, F32), jax.ShapeDtypeStruct((s, LRU_SAVED * GROUP_W), F32)]),
        scratch_shapes=([pltpu.VMEM((SUBLANES, GROUP_W), F32)] + _stage_scratch(TILE_MIX, GROUP_W, 4)
                        + _stage_scratch(TILE_MIX, LANES, 1)),
        compiler_params=_params(("arbitrary",)),
    )(*([z] * 13), *[a for pair in attn for a in pair], *[wts[k] for k in names])


def mix_bwd(dy, z, hl, lru, dqkv, ddg, wts, l):
    s = z.shape[0]
    d4, d16 = PATTERN_DILS[1], PATTERN_DILS[2]
    n_tiles = s // TILE_MIX

    def body(dya_ref, dyb_ref, dyc_ref, ax_ref, ab_ref, ac_ref, ag_ref, rx_ref, rg_ref, cu_ref, cv_ref, cg_ref,
             axh_ref, ach_ref, rxh_ref, hl_ref, hlh_ref, lru_ref, dqkv1_ref, dqkv4_ref, dqkv16_ref, ddg_ref,
             caw_ref, crw_ref, crb_ref, wa_ref, wx_ref, ba_ref, bx_ref, lam_ref, gng_ref, ws_ref, wst_ref, bs_ref,
             dz_ref, ga_ref, gr_ref, gn_ref, gwa_ref, gwx_ref, gws_ref, gbs_ref,
             c_dcv, c_g, c_a, c_dxb, *stage):
        st_a, st_b = stage[:len(stage) // 2], stage[len(stage) // 2:]
        step = pl.program_id(0)
        i = n_tiles - 1 - step

        @pl.when(step == 0)
        def _():
            for r in (c_dcv, c_g, c_a, c_dxb, ga_ref, gr_ref, gn_ref, gwa_ref, gwx_ref, gws_ref, gbs_ref):
                r[...] = jnp.zeros_like(r)

        nz = (i > 0).astype(F32)
        head = _head_of_lane((TILE_MIX, GROUP_W))
        shp8 = (SUBLANES, GROUP_W)
        colsum = lambda v: jnp.sum(v, axis=0, keepdims=True)

        ax, ab, ac, ag = ax_ref[...], ab_ref[...], ac_ref[...], ag_ref[...]
        dya = dya_ref[...]
        pa = ac * ax
        pah = ach_ref[...] * axh_ref[...] * nz
        pa1 = _shift_down(pa, pah, 1)
        pa2 = _shift_down(pa, pah, 2)
        cv = caw_ref[2:3, :] * pa + caw_ref[1:2, :] * pa1 + caw_ref[0:1, :] * pa2
        sg = _silu(ag)
        dz_ref[:, C_AB * GROUP_W:(C_AB + 1) * GROUP_W] = (dya * cv * sg).astype(BF16)
        dz_ref[:, C_AG * GROUP_W:(C_AG + 1) * GROUP_W] = (dya * ab * cv * _dsilu(ag)).astype(BF16)
        dcv = dya * ab * sg
        nxt = c_dcv[...]
        dpa = caw_ref[2:3, :] * dcv + caw_ref[1:2, :] * _shift_up(dcv, nxt, 1) + caw_ref[0:1, :] * _shift_up(dcv, nxt, 2)
        c_dcv[...] = dcv[:SUBLANES, :]
        dz_ref[:, C_AC * GROUP_W:(C_AC + 1) * GROUP_W] = (dpa * ax).astype(BF16)
        dz_ref[:, C_AX * GROUP_W:(C_AX + 1) * GROUP_W] = (dpa * ac).astype(BF16)
        ga_ref[...] += (_put_row(shp8, 2, colsum(dcv * pa)) + _put_row(shp8, 1, colsum(dcv * pa1))
                        + _put_row(shp8, 0, colsum(dcv * pa2)))

        rx, rg = rx_ref[...], rg_ref[...]
        dyb = dyb_ref[...]
        rxh = rxh_ref[...] * nz
        rx1, rx2, rx3 = _shift_down(rx, rxh, 1), _shift_down(rx, rxh, 2), _shift_down(rx, rxh, 3)
        xb, r, ig, a, mult = (lru_ref[:, k * GROUP_W:(k + 1) * GROUP_W] for k in range(LRU_SAVED))
        lam = lam_ref[...]
        sp = _softplus_neg(lam)
        hl = hl_ref[...]
        hprev = _shift_down(hl, hlh_ref[...] * nz, 1)
        dz_ref[:, C_RG * GROUP_W:(C_RG + 1) * GROUP_W] = (dyb * hl * _dsilu(rg)).astype(BF16)
        dh = dyb * _silu(rg)
        a_next = _shift_up(a, c_a[...], 1)
        g = _scan_anticausal(a_next, dh, c_g[0:1, :])
        c_g[...] = g[:SUBLANES, :]
        c_a[...] = a[:SUBLANES, :]
        u = ig * xb
        da = g * hprev
        dmult = g * u
        du = g * mult
        dlog_a = da * a - dmult * (a * a) / mult
        dr = dlog_a * (-RG_C * sp)
        dga = dr * r * (1.0 - r)
        dgx = (du * xb) * ig * (1.0 - ig)
        dgab, dgxb = dga.astype(BF16), dgx.astype(BF16)
        dxb = du * ig + _dot_nt(dgab, wa_ref[...]) + _dot_nt(dgxb, wx_ref[...])
        xbb = xb.astype(BF16)
        gwa_ref[...] += _dot_tn(xbb, dgab)
        gwx_ref[...] += _dot_tn(xbb, dgxb)
        nxt = c_dxb[...]
        drx = (crw_ref[3:4, :] * dxb + crw_ref[2:3, :] * _shift_up(dxb, nxt, 1) + crw_ref[1:2, :] * _shift_up(dxb, nxt, 2)
               + crw_ref[0:1, :] * _shift_up(dxb, nxt, 3))
        c_dxb[...] = dxb[:SUBLANES, :]
        dz_ref[:, C_RX * GROUP_W:(C_RX + 1) * GROUP_W] = drx.astype(BF16)
        dlam = colsum(dlog_a * (-RG_C * r)) * (-jax.nn.sigmoid(-lam))
        gr_ref[...] += (_put_row(shp8, 3, colsum(dxb * rx)) + _put_row(shp8, 2, colsum(dxb * rx1))
                        + _put_row(shp8, 1, colsum(dxb * rx2)) + _put_row(shp8, 0, colsum(dxb * rx3))
                        + _put_row(shp8, 4, colsum(dxb)) + _put_row(shp8, 5, colsum(dga))
                        + _put_row(shp8, 6, colsum(dgx)) + _put_row(shp8, 7, dlam))

        cu, cvv, cg = cu_ref[...], cv_ref[...], cg_ref[...]
        dyc = dyc_ref[...]
        u_c, du_c = _gelu_and_grad(cu)
        gv, dgv_c = _gelu_and_grad(cvv)
        rs = lax.rsqrt(jnp.mean(gv * gv, axis=-1, keepdims=True) + NORM_EPS)
        vh = gv * rs
        gng = gng_ref[...]
        vvb = (vh * gng).astype(BF16)
        spat = _gmlp_spatial(ws_ref, vvb, head) + jnp.concatenate([bs_ref[...]] * (TILE_MIX // GMLP_CHUNK), axis=0)
        sgc = _silu(cg)
        dz_ref[:, C_CU * GROUP_W:(C_CU + 1) * GROUP_W] = (dyc * spat * sgc * du_c).astype(BF16)
        dz_ref[:, C_CG * GROUP_W:(C_CG + 1) * GROUP_W] = (dyc * u_c * spat * _dsilu(cg)).astype(BF16)
        dsp = dyc * u_c * sgc
        dspb = dsp.astype(BF16)
        tril = (lax.broadcasted_iota(jnp.int32, (GMLP_CHUNK, GMLP_CHUNK), 0)
                >= lax.broadcasted_iota(jnp.int32, (GMLP_CHUNK, GMLP_CHUNK), 1))
        head_c = head[:GMLP_CHUNK]
        dvv_parts = []
        gbs = jnp.zeros((GMLP_CHUNK, GROUP_W), F32)
        for j in range(TILE_MIX // GMLP_CHUNK):
            sl = slice(j * GMLP_CHUNK, (j + 1) * GMLP_CHUNK)
            dblk = dspb[sl, :]
            vblk = vvb[sl, :]
            gbs = gbs + dsp[sl, :]
            acc = jnp.zeros((GMLP_CHUNK, GROUP_W), F32)
            for h in range(N_HEADS):
                acc = jnp.where(head_c == h, _dot(wst_ref[h], dblk), acc)
                dm = jnp.where(head_c == h, dblk, jnp.zeros_like(dblk))
                gws_ref[h] += jnp.where(tril, _dot_nt(dm, vblk), 0.0)
            dvv_parts.append(acc)
        gbs_ref[...] += gbs
        dvv = jnp.concatenate(dvv_parts, axis=0)
        gn_ref[...] += _put_row(shp8, 0, colsum(dvv * vh))
        dvh = dvv * gng
        dgv = rs * (dvh - vh * jnp.mean(dvh * vh, axis=-1, keepdims=True))
        dz_ref[:, C_CV * GROUP_W:(C_CV + 1) * GROUP_W] = (dgv * dgv_c).astype(BF16)

        dsum = dqkv1_ref[...] + _interleave_load(dqkv4_ref, d4, st_a) + _interleave_load(dqkv16_ref, d16, st_b)
        dz_ref[:, C_DQ * GROUP_W:(C_DV + 1) * GROUP_W] = dsum.astype(BF16)
        dz_ref[:, C_DG * GROUP_W:(C_DG + 1) * GROUP_W] = ddg_ref[...].astype(BF16)

    per = TILE_MIX // SUBLANES
    qkv_w = 3 * GROUP_W
    rev = lambda c: pl.BlockSpec((TILE_MIX, GROUP_W), lambda t, c=c: (n_tiles - 1 - t, c))
    revh = lambda c: pl.BlockSpec((SUBLANES, GROUP_W),
                                  lambda t, c=c: (jnp.maximum((n_tiles - 1 - t) * per - 1, 0), c))
    revr = lambda dil: pl.BlockSpec((dil, TILE_MIX // dil, qkv_w), lambda t: (0, n_tiles - 1 - t, 0))
    names = ("caw", "crw", "crb", "wa", "wx", "ba", "bx", "lam", "gng", "ws", "wst", "bs")
    in_specs = ([rev(0), rev(1), rev(2)]
                + [rev(c) for c in (C_AX, C_AB, C_AC, C_AG, C_RX, C_RG, C_CU, C_CV, C_CG)]
                + [revh(C_AX), revh(C_AC), revh(C_RX), rev(0), revh(0),
                   pl.BlockSpec((TILE_MIX, LRU_SAVED * GROUP_W), lambda t: (n_tiles - 1 - t, 0)),
                   pl.BlockSpec((TILE_MIX, qkv_w), lambda t: (n_tiles - 1 - t, 0)), revr(d4), revr(d16), rev(0)]
                + [_of_layer(wts[k], l) for k in names])
    small = jax.ShapeDtypeStruct((SUBLANES, GROUP_W), F32)
    sq = jax.ShapeDtypeStruct((GROUP_W, GROUP_W), F32)
    out_shape = [jax.ShapeDtypeStruct((s, D_IN), BF16), small, small, small, sq, sq,
                 jax.ShapeDtypeStruct((N_HEADS, GMLP_CHUNK, GMLP_CHUNK), F32),
                 jax.ShapeDtypeStruct((GMLP_CHUNK, GROUP_W), F32)]
    out_specs = ([pl.BlockSpec((TILE_MIX, D_IN), lambda t: (n_tiles - 1 - t, 0))]
                 + [_full(o.shape) for o in out_shape[1:]])
    return pl.pallas_call(
        body, name="mix_bwd", grid=(n_tiles,),
        in_specs=in_specs, out_specs=out_specs, out_shape=out_shape,
        scratch_shapes=[pltpu.VMEM((SUBLANES, GROUP_W), F32)] * 4 + _stage_scratch(TILE_MIX, qkv_w, 2),
        compiler_params=_params(("arbitrary",)),
    )(dy, dy, dy, *([z] * 12), hl, hl, lru, *dqkv, ddg, *[wts[k] for k in names])


def prepare_small_weights(p):
    tril = jnp.tril(jnp.ones((GMLP_CHUNK, GMLP_CHUNK), dtype=bool))
    ws = jnp.where(tril, p["gmlp_ws"], 0.0).astype(BF16)
    row = lambda a: a[:, None, :]
    eye = jnp.eye(N_HEADS, dtype=F32)
    bd = lambda w: (w[:, :, :, None, :] * eye[None, :, None, :, None]).reshape(-1, GROUP_W, GROUP_W).astype(BF16)
    return dict(
        caw=p["conv_a_w"], crw=p["conv_r_w"], crb=row(p["conv_r_b"]),
        wa=bd(p["lru_wa"]), wx=bd(p["lru_wx"]),
        ba=row(p["lru_ba"]), bx=row(p["lru_bx"]), lam=row(p["lru_lambda"]), gng=row(p["gmlp_norm_g"]),
        ws=ws, wst=jnp.swapaxes(ws, 2, 3),
        bs=jnp.repeat(jnp.swapaxes(p["gmlp_bs"], 1, 2), HEAD_DIM, axis=2))


def _flat(a):
    return a.reshape(a.shape[0] * a.shape[1], a.shape[2])


def _split(a, dil):
    return a.reshape(dil, a.shape[0] // dil, a.shape[1])


def local_step(x, tgt, final_g, depth, chip, layer_weights, projections_done):
    saved = []
    for l in range(depth):
        gain, w_in_l, rest = layer_weights(l, x)
        z, h, *qkvs = in_fwd(x, gain, w_in_l)
        w_out_l, wts = rest(z)
        qkvs = [_flat(q) if q.ndim == 3 else q for q in qkvs]
        attn = []
        for q, d in zip(qkvs, PATTERN_DILS):
            o_p, lse_p = attn_fwd(q, d)
            attn.append((o_p, lse_p) if d == 1 else (_split(o_p, d), _split(lse_p, d)))
        y, hl, o, lse, lse4, lse16, lru = mix_fwd(z, attn, wts, l)
        saved.append(dict(x=x, z=z, h=h, y=y, hl=hl, o=o, qkvs=qkvs, lses=(lse, _flat(lse4), _flat(lse16)), wts=wts, lru=lru,
                          gain=gain, w_in=w_in_l, w_out=w_out_l))
        if l < depth - 1:
            x = out_fwd(y, w_out_l, x)
        else:
            loss, dx, dfg = out_fwd_loss(y, w_out_l, x, final_g[None, :], tgt)
    raw = {k: [None] * depth for k in ("gain", "a", "r", "n", "wa", "wx", "ws", "bs")}
    zero = None
    for l in reversed(range(depth)):
        sv = saved[l]
        dy, ddg, do1, do4, do16, dl1, dl4, dl16 = out_bwd(dx, sv["w_out"], sv["z"], sv["o"])
        g_w_out = grad_w_out(sv["y"], dx)
        dqkv = []
        for q, do, lse, dl, d in zip(sv["qkvs"], (do1, _flat(do4), _flat(do16)), sv["lses"],
                                     (dl1, _flat(dl4), _flat(dl16)), PATTERN_DILS):
            g = attn_bwd(q, do, lse, dl, d)
            dqkv.append(g if d == 1 else _split(g, d))
        dz, ga, gr, gn, gwa, gwx, gws, gbs = mix_bwd(dy, sv["z"], sv["hl"], sv["lru"], dqkv, ddg, sv["wts"], l)
        gain = sv["gain"] if zero is None else sv["gain"] + zero
        zero = projections_done(l, *grad_w_in(sv["h"], dz, chip), g_w_out)
        if l == 0 and zero is not None:
            gain = gain + zero
        dx, dgn = in_bwd(dz, sv["w_in"], sv["x"], gain, dx)
        for k, g in zip(("gain", "a", "r", "n", "wa", "wx", "ws", "bs"), (dgn, ga, gr, gn, gwa, gwx, gws, gbs)):
            raw[k][l] = g
    st = {k: jnp.stack(v) for k, v in raw.items()}
    eye = jnp.eye(N_HEADS, dtype=F32)[None, :, None, :, None]
    diag = lambda g: (g.reshape(depth, N_HEADS, HEAD_DIM, N_HEADS, HEAD_DIM) * eye).sum(axis=3)
    grads = dict(
        norm_g=st["gain"][:, 0], conv_a_w=st["a"][:, :3], conv_r_w=st["r"][:, :4], conv_r_b=st["r"][:, 4],
        lru_ba=st["r"][:, 5], lru_bx=st["r"][:, 6], lru_lambda=st["r"][:, 7], gmlp_norm_g=st["n"][:, 0],
        lru_wa=diag(st["wa"]), lru_wx=diag(st["wx"]), gmlp_ws=st["ws"],
        gmlp_bs=jnp.swapaxes(st["bs"].reshape(depth, GMLP_CHUNK, N_HEADS, HEAD_DIM).sum(-1), 1, 2),
        final_g=dfg[0])
    return loss, dx, grads


MESH = pl.DeviceIdType.MESH
N_CHIPS = 4
N_DEV = 8
ANY = pl.BlockSpec(memory_space=pl.ANY)


def _place():
    x, y, c = lax.axis_index("x"), lax.axis_index("y"), lax.axis_index("c")
    chips = [(1 - x, y), (x, 1 - y), (1 - x, 1 - y)]
    return x, y, c, chips


def _remote(src, dst, ssem, rsem, to):
    return pltpu.make_async_remote_copy(src_ref=src, dst_ref=dst, send_sem=ssem, recv_sem=rsem,
                                        device_id=to, device_id_type=MESH)


HBM = pl.BlockSpec(memory_space=pltpu.HBM)
SEM = pl.BlockSpec(memory_space=pltpu.SEMAPHORE)
DATAFLOW = pltpu.SideEffectType.DATAFLOW_SIDE_EFFECTING
GATHER, SCATTER = "gather", "scatter"


def _chip_copies(mode, src_refs, land_refs, ssem, rsem):
    x, y, c, chips = _place()
    me = 2 * x + y
    n = len(src_refs)
    copies = []
    for k, (cx, cy) in enumerate(chips):
        for a in range(n):
            if mode == GATHER:
                src, dst = src_refs[a], land_refs[a].at[me]
            else:
                src, dst = src_refs[a].at[2 * cx + cy], land_refs[a].at[k]
            copies.append(_remote(src, dst, ssem.at[n * k + a], rsem.at[n * k + a], (cx, cy, c)))
    return copies


def exchange_start(mode, srcs, after, name):
    n = len(srcs)
    if mode == GATHER:
        lands = [lax.empty((N_CHIPS,) + s.shape, s.dtype) for s in srcs]
    else:
        lands = [lax.empty((N_CHIPS - 1,) + s.shape[1:], s.dtype) for s in srcs]
    extra = [] if after is None else [after]

    def body(*refs):
        src_refs, land_refs = refs[:n], refs[n:2 * n]
        ssem, rsem = refs[2 * n + len(extra)], refs[2 * n + len(extra) + 1]
        token = refs[-1]
        for cp in _chip_copies(mode, src_refs, land_refs, ssem, rsem):
            cp.start()
        token[...] = jnp.zeros_like(token)

    arrays = list(srcs) + lands
    return pl.pallas_call(
        body, name=name,
        out_shape=(pltpu.SemaphoreType.DMA((3 * n,)), pltpu.SemaphoreType.DMA((3 * n,)),
                   *[pltpu.HBM(a.shape, a.dtype) for a in arrays], jax.ShapeDtypeStruct((SUBLANES, LANES), F32)),
        in_specs=[HBM] * (2 * n) + [ANY] * len(extra),
        out_specs=(SEM, SEM, *[HBM] * (2 * n), pl.BlockSpec(memory_space=pltpu.VMEM)),
        input_output_aliases={i: 2 + i for i in range(2 * n)},
        compiler_params=pltpu.CompilerParams(has_side_effects=DATAFLOW),
    )(*[pltpu.with_memory_space_constraint(a, pltpu.HBM) for a in arrays], *extra)


def exchange_wait(mode, started, after, name):
    ssem, rsem, *thru, _ = started
    n = len(thru) // 2

    def body(*refs):
        src_refs, land_refs = refs[:n], refs[n:2 * n]
        ssem_ref, rsem_ref = refs[2 * n], refs[2 * n + 1]
        for cp in _chip_copies(mode, src_refs, land_refs, ssem_ref, rsem_ref):
            cp.wait_send()
            cp.wait_recv()

    outs = pl.pallas_call(
        body, name=name,
        out_shape=[pltpu.HBM(a.shape, a.dtype) for a in thru],
        in_specs=[HBM] * (2 * n) + [SEM, SEM, ANY],
        out_specs=[HBM] * (2 * n),
        input_output_aliases={i: i for i in range(2 * n)},
        compiler_params=pltpu.CompilerParams(has_side_effects=DATAFLOW),
    )(*thru, ssem, rsem, after)
    return outs[n:]


def sibling_exchange(p1, p2):
    def body(p1_ref, p2_ref, q1_ref, q2_ref, ssem, rsem):
        x, y, c, _ = _place()
        copies = [_remote(p_ref, q_ref, ssem.at[a], rsem.at[a], (x, y, 1 - c))
                  for a, (p_ref, q_ref) in enumerate(((p1_ref, q1_ref), (p2_ref, q2_ref)))]
        for cp in copies:
            cp.start()
        for cp in copies:
            cp.wait()

    return pl.pallas_call(
        body, name="sibling_exchange",
        in_specs=[ANY, ANY], out_specs=[ANY, ANY],
        out_shape=[jax.ShapeDtypeStruct(p.shape, p.dtype) for p in (p1, p2)],
        scratch_shapes=[pltpu.SemaphoreType.DMA((2,)), pltpu.SemaphoreType.DMA((2,))],
    )(p1, p2)


def all_reduce_small(v):
    r, n = v.shape
    piece = r // N_DEV

    def body(x_ref, out_ref, recv, ssem1, rsem1, ssem2, rsem2):
        x, y, c, _ = _place()
        me = 4 * x + 2 * y + c

        def peer(k):
            px = 1 - x if (k >> 2) & 1 else x
            py = 1 - y if (k >> 1) & 1 else y
            pc = 1 - c if k & 1 else c
            return (px, py, pc), 4 * px + 2 * py + pc

        def rows(ref, d):
            return ref.at[pl.ds(d * piece, piece), :]

        scatter = []
        for k in range(1, N_DEV):
            to, idx = peer(k)
            scatter.append(_remote(rows(x_ref, idx), recv.at[k], ssem1.at[k - 1], rsem1.at[k - 1], to))
            scatter[-1].start()
        acc = rows(x_ref, me)[...]
        for k in range(1, N_DEV):
            scatter[k - 1].wait_recv()
            acc = acc + recv[k]
        rows(out_ref, me)[...] = acc

        gather = []
        for k in range(1, N_DEV):
            to, _ = peer(k)
            gather.append(_remote(rows(out_ref, me), rows(out_ref, me), ssem2.at[k - 1], rsem2.at[k - 1], to))
            gather[-1].start()
        for k in range(1, N_DEV):
            to, idx = peer(k)
            _remote(rows(out_ref, idx), rows(out_ref, idx), ssem2.at[k - 1], rsem2.at[k - 1], to).wait_recv()
        for cp in scatter + gather:
            cp.wait_send()

    return pl.pallas_call(
        body, name="all_reduce_small",
        out_shape=jax.ShapeDtypeStruct((r, n), v.dtype),
        in_specs=[pl.BlockSpec(memory_space=pltpu.VMEM)],
        out_specs=pl.BlockSpec(memory_space=pltpu.VMEM),
        scratch_shapes=[pltpu.VMEM((N_DEV, piece, n), v.dtype)] + [pltpu.SemaphoreType.DMA((N_DEV - 1,))] * 4,
        compiler_params=pltpu.CompilerParams(vmem_limit_bytes=VMEM_LIMIT),
    )(v)


TILE_ROWS = 256


def sum_partials(own, parts):
    k, r, c = parts.shape

    def body(o_ref, p_ref, out_ref):
        acc = o_ref[...]
        for i in range(k):
            acc = acc + p_ref[i].astype(F32)
        out_ref[...] = acc

    row = pl.BlockSpec((TILE_ROWS, c), lambda i: (i, 0))
    return pl.pallas_call(
        body, name="sum_partials", grid=(r // TILE_ROWS,),
        in_specs=[row, pl.BlockSpec((k, TILE_ROWS, c), lambda i: (0, i, 0))],
        out_specs=row,
        out_shape=jax.ShapeDtypeStruct((r, c), F32),
        compiler_params=_params(("parallel",)),
    )(own, parts)


def _adamw_update(w, g, m, v):
    m2 = ADAM_B1 * m + (1.0 - ADAM_B1) * g
    v2 = ADAM_B2 * v + (1.0 - ADAM_B2) * (g * g)
    m_hat = m2 / (1.0 - ADAM_B1 ** ADAM_STEP)
    v_hat = v2 / (1.0 - ADAM_B2 ** ADAM_STEP)
    return -ADAM_LR * (m_hat / (jnp.sqrt(v_hat) + ADAM_EPS) + ADAM_WD * w), m2, v2


def adamw_small(ws, gs, ms, vs):
    n = len(ws)

    def body(*refs):
        ins, outs = refs[:4 * n], refs[4 * n:]
        for i in range(n):
            d, m2, v2 = _adamw_update(ins[i][...], ins[n + i][...], ins[2 * n + i][...], ins[3 * n + i][...])
            outs[3 * i][...] = d
            outs[3 * i + 1][...] = m2
            outs[3 * i + 2][...] = v2

    outs = pl.pallas_call(
        body, name="adamw_small",
        out_shape=[jax.ShapeDtypeStruct(w.shape, F32) for w in ws for _ in range(3)],
    )(*ws, *gs, *ms, *vs)
    return [tuple(outs[3 * i:3 * i + 3]) for i in range(n)]


def adamw(w, ga, gb, m, v):
    n, r, c = w.shape
    tile = max(t for t in range(SUBLANES, TILE_ROWS + 1, SUBLANES) if r % t == 0)

    def body(w_ref, ga_ref, gb_ref, m_ref, v_ref, g_ref, d_ref, m2_ref, v2_ref):
        g = ga_ref[...] + gb_ref[...]
        g_ref[...] = g
        d_ref[...], m2_ref[...], v2_ref[...] = _adamw_update(w_ref[...], g, m_ref[...], v_ref[...])

    spec = pl.BlockSpec((1, tile, c), lambda j, i: (j, i, 0))
    return pl.pallas_call(
        body, name="adamw", grid=(n, r // tile),
        in_specs=[spec] * 5, out_specs=[spec] * 4,
        out_shape=[jax.ShapeDtypeStruct((n, r, c), F32)] * 4,
        compiler_params=_params(("parallel", "parallel")),
    )(w, ga, gb, m, v)


REPLICATED = ("norm_g", "conv_r_b", "lru_wa", "lru_ba", "lru_wx", "lru_bx", "lru_lambda", "gmlp_norm_g",
              "gmlp_ws", "gmlp_bs", "final_g")
CHIP_SHARDED_SMALL = ("conv_a_w", "conv_r_w")
PACK_LANES = 128


def _pack(arrays):
    flat = jnp.concatenate([a.reshape(-1) for a in arrays])
    pad = (-flat.shape[0]) % (TILE_ROWS * PACK_LANES)
    return jnp.pad(flat, (0, pad)).reshape(-1, PACK_LANES)


def _unpack(packed, shapes):
    flat = packed.reshape(-1)
    out, off = [], 0
    for shp in shapes:
        n = math.prod(shp)
        out.append(flat[off:off + n].reshape(shp))
        off += n
    return out


def kernel(x, norm_g, w_in, conv_a_w, conv_r_w, conv_r_b, lru_wa, lru_ba, lru_wx, lru_bx, lru_lambda, gmlp_norm_g, gmlp_ws, gmlp_bs, w_out, final_g, loss_target, m_norm_g, m_w_in, m_conv_a_w, m_conv_r_w, m_conv_r_b, m_lru_wa, m_lru_ba, m_lru_wx, m_lru_bx, m_lru_lambda, m_gmlp_norm_g, m_gmlp_ws, m_gmlp_bs, m_w_out, m_final_g, v_norm_g, v_w_in, v_conv_a_w, v_conv_r_w, v_conv_r_b, v_lru_wa, v_lru_ba, v_lru_wx, v_lru_bx, v_lru_lambda, v_gmlp_norm_g, v_gmlp_ws, v_gmlp_bs, v_w_out, v_final_g):
    names = ("norm_g", "w_in", "conv_a_w", "conv_r_w", "conv_r_b", "lru_wa", "lru_ba", "lru_wx", "lru_bx",
             "lru_lambda", "gmlp_norm_g", "gmlp_ws", "gmlp_bs", "w_out", "final_g")
    w = dict(zip(names, (norm_g, w_in, conv_a_w, conv_r_w, conv_r_b, lru_wa, lru_ba, lru_wx, lru_bx, lru_lambda,
                         gmlp_norm_g, gmlp_ws, gmlp_bs, w_out, final_g)))
    m = dict(zip(names, (m_norm_g, m_w_in, m_conv_a_w, m_conv_r_w, m_conv_r_b, m_lru_wa, m_lru_ba, m_lru_wx, m_lru_bx,
                         m_lru_lambda, m_gmlp_norm_g, m_gmlp_ws, m_gmlp_bs, m_w_out, m_final_g)))
    v = dict(zip(names, (v_norm_g, v_w_in, v_conv_a_w, v_conv_r_w, v_conv_r_b, v_lru_wa, v_lru_ba, v_lru_wx, v_lru_bx,
                         v_lru_lambda, v_gmlp_norm_g, v_gmlp_ws, v_gmlp_bs, v_w_out, v_final_g)))
    depth, _, in_cols = w_in.shape
    out_rows = w_out.shape[1]
    conv_ch = conv_a_w.shape[2]
    chip = 2 * lax.axis_index("x") + lax.axis_index("y")

    taps = conv_a_w.shape[1] + conv_r_w.shape[1]
    w_in_t, m_w_in_t, v_w_in_t = (jnp.swapaxes(a, 1, 2) for a in (w_in, m_w_in, v_w_in))
    w_in_h, w_out_h = w_in_t.astype(BF16), w_out.astype(BF16)
    conv_own = jnp.concatenate([conv_a_w, conv_r_w], axis=1).reshape(depth * taps, conv_ch)
    gathers, token = [], None
    for l in range(depth):
        groups = [[w_in_h[l]], [w_out_h[l], conv_own]] if l == 0 else [[w_in_h[l], w_out_h[l]]]
        gathers.append([])
        for i, srcs in enumerate(groups):
            gathers[l].append(exchange_start(GATHER, srcs, token, f"gather_start_{l}_{i}"))
            token = gathers[l][-1][-1]
    p = dict(w)

    def with_own(land, own):
        return lax.dynamic_update_slice(land, own[None], (chip,) + (0,) * own.ndim)

    def layer_weights(l, x_l):
        lands = list(exchange_wait(GATHER, gathers[l][0], x_l, f"gather_wait_{l}_0"))
        w_in_l = with_own(lands[0], w_in_h[l]).reshape(D_IN, D_MODEL)
        gain = norm_g[l][None, :]
        if l == 0:
            gain = gain + token[0, 0]

        def rest(z_l):
            if l == 0:
                lands.extend(exchange_wait(GATHER, gathers[l][1], z_l, f"gather_wait_{l}_1"))
                conv = with_own(lands[2], conv_own).reshape(N_CHIPS, depth, taps, conv_ch)
                conv = conv.transpose(1, 2, 0, 3).reshape(depth, taps, GROUP_W)
                p["conv_a_w"] = conv[:, :conv_a_w.shape[1]]
                p["conv_r_w"] = conv[:, conv_a_w.shape[1]:]
                p["prepared"] = prepare_small_weights(p)
            return with_own(lands[1], w_out_h[l]).reshape(D_MIX, D_MODEL), p["prepared"]

        return gain, w_in_l, rest

    scatters, owns = [None] * depth, [None] * depth

    def projections_done(l, g_w_in_by_chip, g_w_in_own, g_w_out):
        go = g_w_out.reshape(N_CHIPS, out_rows, D_MODEL)
        owns[l] = (g_w_in_own, lax.dynamic_index_in_dim(go, chip, axis=0, keepdims=False))
        scatters[l] = exchange_start(SCATTER, [g_w_in_by_chip, go.astype(BF16)], None, f"scatter_start_{l}")
        return scatters[l][-1][0, 0]

    loss8, dx, grads = local_step(x[0], loss_target[0], final_g, depth, chip.reshape(1), layer_weights,
                                  projections_done)
    loss = lax.psum(loss8[0, 0], ("x", "y", "c"))

    res = {}
    small = REPLICATED + CHIP_SHARDED_SMALL
    packed = _pack([grads[k] for k in small])
    total = all_reduce_small(packed)
    gs = dict(zip(small, _unpack(total, [grads[k].shape for k in small])))
    for k in CHIP_SHARDED_SMALL:
        gs[k] = lax.dynamic_slice_in_dim(gs[k], chip * conv_ch, conv_ch, axis=2)
    as2d = lambda a: a[None] if a.ndim == 1 else a
    outs = adamw_small(*[[as2d(d[k]) for k in small] for d in (w, gs, m, v)])
    for k, (delta, m2, v2) in zip(small, outs):
        res[k] = [t.reshape(w[k].shape) for t in (gs[k], delta, m2, v2)]

    parts = [exchange_wait(SCATTER, scatters[l], total, f"scatter_wait_{l}") for l in range(depth)]
    r1 = jnp.concatenate([parts[l][0] for l in range(depth)], axis=1)
    r2 = jnp.concatenate([parts[l][1] for l in range(depth)], axis=1)
    own1 = jnp.concatenate([owns[l][0] for l in range(depth)], axis=0)
    own2 = jnp.concatenate([owns[l][1] for l in range(depth)], axis=0)
    p1 = sum_partials(own1, r1)
    p2 = sum_partials(own2, r2)
    q1, q2 = sibling_exchange(p1, p2)
    res["w_in"] = [jnp.swapaxes(t, 1, 2) for t in
                   adamw(w_in_t, p1.reshape(w_in_t.shape), q1.reshape(w_in_t.shape), m_w_in_t, v_w_in_t)]
    res["w_out"] = adamw(w_out, p2.reshape(w_out.shape), q2.reshape(w_out.shape), m_w_out, v_w_out)

    return (loss, dx[None], *[res[k][0] for k in names], *[res[k][1] for k in names],
            *[res[k][2] for k in names], *[res[k][3] for k in names])
```

```python
import functools
import math

import jax
import jax.numpy as jnp
import numpy as np
from jax import lax
from jax.experimental import pallas as pl
from jax.experimental.pallas import tpu as pltpu

F32 = jnp.float32
BF16 = jnp.bfloat16

D_MODEL = 1024
GROUP_W = 256
N_HEADS = 4
HEAD_DIM = 64
N_CHUNKS = 13
D_IN = N_CHUNKS * GROUP_W
D_MIX = 4 * GROUP_W
NORM_EPS = 1e-6
RG_C = 8.0
GMLP_CHUNK = 128
ATTN_BLOCK = 128
PATTERN_DILS = (1, 4, 16)
N_PATTERNS = len(PATTERN_DILS)
ALIBI_SLOPES = tuple(2.0 ** (-8.0 * (h + 1) / N_HEADS) for h in range(N_HEADS))
ATTN_SCALE = 1.0 / math.sqrt(HEAD_DIM)
NEG_BIG = -1e30

ADAM_LR = 0.001
ADAM_B1 = 0.9
ADAM_B2 = 0.999
ADAM_EPS = 1e-08
ADAM_WD = 0.01
ADAM_STEP = 10

C_AX, C_AB, C_AC, C_AG, C_RX, C_RG, C_CU, C_CV, C_CG, C_DQ, C_DK, C_DV, C_DG = range(13)

SUBLANES = 8
LANES = 128
VMEM_LIMIT = 56 * 1024 * 1024
TILE_IN = 512
TILE_MIX = 512
TILE_DW = 1024
TILE_DW_OUT = 2048
ATTN_QB = 4
GELU_K0 = math.sqrt(2.0 / math.pi)
GELU_K1 = 0.044715


def _params(sem):
    return pltpu.CompilerParams(dimension_semantics=sem, vmem_limit_bytes=VMEM_LIMIT)


def _silu(x):
    return x * jax.nn.sigmoid(x)


def _dsilu(x):
    s = jax.nn.sigmoid(x)
    return s * (1.0 + x * (1.0 - s))


def _gelu(x):
    return 0.5 * x * (1.0 + jnp.tanh(GELU_K0 * (x + GELU_K1 * x * x * x)))


def _gelu_and_grad(x):
    t = jnp.tanh(GELU_K0 * (x + GELU_K1 * x * x * x))
    g = 0.5 * x * (1.0 + t)
    dg = 0.5 * (1.0 + t) + 0.5 * x * (1.0 - t * t) * GELU_K0 * (1.0 + 3.0 * GELU_K1 * x * x)
    return g, dg


def _neg_expm1(x):
    series = x * (1.0 + x * (0.5 + x * (1.0 / 6.0 + x * (1.0 / 24.0 + x * (1.0 / 120.0)))))
    return -jnp.where(x > -0.05, series, jnp.exp(x) - 1.0)


def _shift_down(v, halo, k):
    r = pltpu.roll(v, k, 0)
    rh = pltpu.roll(halo, k, 0)
    row = lax.broadcasted_iota(jnp.int32, halo.shape, 0)
    top = jnp.where(row < k, rh, r[:SUBLANES])
    return jnp.concatenate([top, r[SUBLANES:]], axis=0)


def _shift_up(v, halo, k):
    t = v.shape[0]
    r = pltpu.roll(v, t - k, 0)
    rh = pltpu.roll(halo, SUBLANES - k, 0)
    row = lax.broadcasted_iota(jnp.int32, halo.shape, 0)
    bot = jnp.where(row >= SUBLANES - k, rh, r[t - SUBLANES:])
    return jnp.concatenate([r[:t - SUBLANES], bot], axis=0)


def _scan_causal(a, b, h_in):
    t = a.shape[0]
    row8 = lax.broadcasted_iota(jnp.int32, a.shape, 0) % SUBLANES
    d = 1
    while d < SUBLANES:
        m = row8 >= d
        a_s = jnp.where(m, pltpu.roll(a, d, 0), 1.0)
        b_s = jnp.where(m, pltpu.roll(b, d, 0), 0.0)
        b = a * b_s + b
        a = a * a_s
        d *= 2
    out, carry = [], h_in
    for g in range(t // SUBLANES):
        sl = slice(g * SUBLANES, (g + 1) * SUBLANES)
        hg = b[sl] + a[sl] * carry
        out.append(hg)
        carry = hg[SUBLANES - 1:SUBLANES]
    return jnp.concatenate(out, axis=0)


def _scan_anticausal(a, b, g_in):
    t = a.shape[0]
    row8 = lax.broadcasted_iota(jnp.int32, a.shape, 0) % SUBLANES
    d = 1
    while d < SUBLANES:
        m = row8 < SUBLANES - d
        a_s = jnp.where(m, pltpu.roll(a, t - d, 0), 1.0)
        b_s = jnp.where(m, pltpu.roll(b, t - d, 0), 0.0)
        b = a * b_s + b
        a = a * a_s
        d *= 2
    out, carry = [], g_in
    for g in reversed(range(t // SUBLANES)):
        sl = slice(g * SUBLANES, (g + 1) * SUBLANES)
        gg = b[sl] + a[sl] * carry
        out.append(gg)
        carry = gg[0:1]
    return jnp.concatenate(out[::-1], axis=0)


def _head_of_lane(shape):
    return lax.broadcasted_iota(jnp.int32, shape, len(shape) - 1) // HEAD_DIM


def _per_head_lanes(cols):
    t = cols[0].shape[0]
    lane = lax.broadcasted_iota(jnp.int32, (t, LANES), 1)
    out = jnp.zeros((t, LANES), F32)
    for h, col in enumerate(cols):
        out = jnp.where(lane == h, col, out)
    return out


def _put_row(acc_shape, k, row_vec):
    row = lax.broadcasted_iota(jnp.int32, acc_shape, 0)
    return jnp.where(row == k, jnp.broadcast_to(row_vec, acc_shape), 0.0)


def _dot(a, b):
    return jnp.dot(a, b, preferred_element_type=F32)


def _dot_nt(a, b):
    return lax.dot_general(a, b, (((1,), (1,)), ((), ())), preferred_element_type=F32)


def _dot_tn(a, b):
    return lax.dot_general(a, b, (((0,), (0,)), ((), ())), preferred_element_type=F32)


def _deinterleave_store(val, stage, outs):
    t, c = val.shape
    for hh in range(c // LANES):
        stage[hh][...] = val[:, hh * LANES:(hh + 1) * LANES].astype(F32)
    for dil, ref in outs:
        for r in range(dil):
            for hh in range(c // LANES):
                ref[r, :, hh * LANES:(hh + 1) * LANES] = stage[hh][pl.ds(r, t // dil, stride=dil), :].astype(ref.dtype)


def _interleave_load(ref, dil, stage):
    _, n, c = ref.shape
    for r in range(dil):
        for hh in range(c // LANES):
            stage[hh][pl.ds(r, n, stride=dil), :] = ref[r, :, hh * LANES:(hh + 1) * LANES].astype(F32)
    return jnp.concatenate([stage[hh][...] for hh in range(c // LANES)], axis=1)


def _stage_scratch(tile, cols, copies):
    return [pltpu.VMEM((tile, LANES), F32)] * (copies * (cols // LANES))


def _by_residue(s, dil, cols, dtype):
    return jax.ShapeDtypeStruct((dil, s // dil, cols), dtype)


def _residue_block(dil, tile, cols):
    return pl.BlockSpec((dil, tile // dil, cols), lambda i: (0, i, 0))


def in_fwd(x, g, w):
    s = x.shape[0]
    qkv_w = 3 * GROUP_W

    def body(x_ref, g_ref, w_ref, z_ref, h_ref, qkv1_ref, qkv4_ref, qkv16_ref, *stage):
        xv = x_ref[...]
        rs = lax.rsqrt(jnp.mean(xv * xv, axis=-1, keepdims=True) + NORM_EPS)
        h = (xv * rs * g_ref[...]).astype(BF16)
        h_ref[...] = h
        z = _dot_nt(h, w_ref[...])
        z_ref[...] = z
        qkv = z[:, C_DQ * GROUP_W:(C_DV + 1) * GROUP_W]
        qkv1_ref[...] = qkv.astype(BF16)
        _deinterleave_store(qkv, stage, ((PATTERN_DILS[1], qkv4_ref), (PATTERN_DILS[2], qkv16_ref)))

    return pl.pallas_call(
        body, name="in_fwd", grid=(s // TILE_IN,),
        in_specs=[pl.BlockSpec((TILE_IN, D_MODEL), lambda i: (i, 0)),
                  pl.BlockSpec((1, D_MODEL), lambda i: (0, 0)),
                  pl.BlockSpec((D_IN, D_MODEL), lambda i: (0, 0))],
        out_specs=[pl.BlockSpec((TILE_IN, D_IN), lambda i: (i, 0)),
                   pl.BlockSpec((TILE_IN, D_MODEL), lambda i: (i, 0)),
                   pl.BlockSpec((TILE_IN, qkv_w), lambda i: (i, 0)),
                   _residue_block(PATTERN_DILS[1], TILE_IN, qkv_w),
                   _residue_block(PATTERN_DILS[2], TILE_IN, qkv_w)],
        out_shape=[jax.ShapeDtypeStruct((s, D_IN), F32), jax.ShapeDtypeStruct((s, D_MODEL), BF16),
                   jax.ShapeDtypeStruct((s, qkv_w), BF16),
                   _by_residue(s, PATTERN_DILS[1], qkv_w, BF16), _by_residue(s, PATTERN_DILS[2], qkv_w, BF16)],
        scratch_shapes=_stage_scratch(TILE_IN, qkv_w, 1),
        compiler_params=_params(("parallel",)),
    )(x, g, w)


def out_fwd(y, w, x):
    s = x.shape[0]

    def body(y_ref, w_ref, x_ref, o_ref):
        o_ref[...] = x_ref[...] + _dot(y_ref[...], w_ref[...])

    return pl.pallas_call(
        body, name="out_fwd", grid=(s // TILE_IN,),
        in_specs=[pl.BlockSpec((TILE_IN, D_MIX), lambda i: (i, 0)),
                  pl.BlockSpec((D_MIX, D_MODEL), lambda i: (0, 0)),
                  pl.BlockSpec((TILE_IN, D_MODEL), lambda i: (i, 0))],
        out_specs=pl.BlockSpec((TILE_IN, D_MODEL), lambda i: (i, 0)),
        out_shape=jax.ShapeDtypeStruct((s, D_MODEL), F32),
        compiler_params=_params(("parallel",)),
    )(y, w, x)


def out_bwd(dx, w, z, o):
    s = dx.shape[0]
    abc = 3 * GROUP_W

    def body(dx_ref, w_ref, dg_ref, o_ref, dy_ref, ddg_ref, do1_ref, do4_ref, do16_ref, dl1_ref, dl4_ref, dl16_ref,
             *stage):
        stage_a, stage_b = stage[:2], stage[2:]
        dy = _dot_nt(dx_ref[...].astype(BF16), w_ref[...])
        dy_ref[...] = dy[:, :abc]
        dyd = dy[:, abc:]
        head = _head_of_lane((TILE_IN, GROUP_W))
        dg = dg_ref[...]
        o = o_ref[...]
        do = dyd * _silu(dg)
        ddg_ref[...] = dyd * o * _dsilu(dg)
        prod = do * o
        dl = _per_head_lanes([jnp.sum(jnp.where(head == h, prod, 0.0), axis=-1, keepdims=True)
                              for h in range(N_HEADS)])
        do1_ref[...] = do.astype(BF16)
        dl1_ref[...] = dl
        _deinterleave_store(do, stage_a, ((PATTERN_DILS[1], do4_ref), (PATTERN_DILS[2], do16_ref)))
        _deinterleave_store(dl, stage_b, ((PATTERN_DILS[1], dl4_ref), (PATTERN_DILS[2], dl16_ref)))

    row = pl.BlockSpec((TILE_IN, GROUP_W), lambda i: (i, 0))
    r4 = _residue_block(PATTERN_DILS[1], TILE_IN, GROUP_W)
    r16 = _residue_block(PATTERN_DILS[2], TILE_IN, GROUP_W)
    crow = pl.BlockSpec((TILE_IN, LANES), lambda i: (i, 0))
    c4 = _residue_block(PATTERN_DILS[1], TILE_IN, LANES)
    c16 = _residue_block(PATTERN_DILS[2], TILE_IN, LANES)
    return pl.pallas_call(
        body, name="out_bwd", grid=(s // TILE_IN,),
        in_specs=[pl.BlockSpec((TILE_IN, D_MODEL), lambda i: (i, 0)),
                  pl.BlockSpec((D_MIX, D_MODEL), lambda i: (0, 0)),
                  pl.BlockSpec((TILE_IN, GROUP_W), lambda i: (i, C_DG)), row],
        out_specs=[pl.BlockSpec((TILE_IN, abc), lambda i: (i, 0)), row, row, r4, r16, crow, c4, c16],
        out_shape=[jax.ShapeDtypeStruct((s, abc), F32), jax.ShapeDtypeStruct((s, GROUP_W), F32),
                   jax.ShapeDtypeStruct((s, GROUP_W), BF16),
                   _by_residue(s, PATTERN_DILS[1], GROUP_W, BF16), _by_residue(s, PATTERN_DILS[2], GROUP_W, BF16),
                   jax.ShapeDtypeStruct((s, LANES), F32),
                   _by_residue(s, PATTERN_DILS[1], LANES, F32), _by_residue(s, PATTERN_DILS[2], LANES, F32)],
        scratch_shapes=_stage_scratch(TILE_IN, GROUP_W, 1) + _stage_scratch(TILE_IN, LANES, 1),
        compiler_params=_params(("parallel",)),
    )(dx, w, z, o)


def in_bwd(dz, w, x, g, dx_next):
    s = x.shape[0]

    def body(dz_ref, w_ref, x_ref, g_ref, dxn_ref, dx_ref, dg_ref):
        @pl.when(pl.program_id(0) == 0)
        def _():
            dg_ref[...] = jnp.zeros_like(dg_ref)

        dh = _dot(dz_ref[...], w_ref[...])
        xv = x_ref[...]
        rs = lax.rsqrt(jnp.mean(xv * xv, axis=-1, keepdims=True) + NORM_EPS)
        xh = xv * rs
        dg_ref[...] += _put_row(dg_ref.shape, 0, jnp.sum(dh * xh, axis=0, keepdims=True))
        dn = dh * g_ref[...]
        dx_ref[...] = dxn_ref[...] + rs * (dn - xh * jnp.mean(dn * xh, axis=-1, keepdims=True))

    return pl.pallas_call(
        body, name="in_bwd", grid=(s // TILE_IN,),
        in_specs=[pl.BlockSpec((TILE_IN, D_IN), lambda i: (i, 0)),
                  pl.BlockSpec((D_IN, D_MODEL), lambda i: (0, 0)),
                  pl.BlockSpec((TILE_IN, D_MODEL), lambda i: (i, 0)),
                  pl.BlockSpec((1, D_MODEL), lambda i: (0, 0)),
                  pl.BlockSpec((TILE_IN, D_MODEL), lambda i: (i, 0))],
        out_specs=[pl.BlockSpec((TILE_IN, D_MODEL), lambda i: (i, 0)),
                   pl.BlockSpec((SUBLANES, D_MODEL), lambda i: (0, 0))],
        out_shape=[jax.ShapeDtypeStruct((s, D_MODEL), F32), jax.ShapeDtypeStruct((SUBLANES, D_MODEL), F32)],
        compiler_params=_params(("arbitrary",)),
    )(dz, w, x, g, dx_next)


def grad_w_out(y, dx):
    s = y.shape[0]

    def body(y_ref, dx_ref, o_ref):
        @pl.when(pl.program_id(0) == 0)
        def _():
            o_ref[...] = jnp.zeros_like(o_ref)

        o_ref[...] += _dot_tn(y_ref[...], dx_ref[...].astype(BF16))

    return pl.pallas_call(
        body, name="grad_w_out", grid=(s // TILE_DW_OUT,),
        in_specs=[pl.BlockSpec((TILE_DW_OUT, D_MIX), lambda k: (k, 0)),
                  pl.BlockSpec((TILE_DW_OUT, D_MODEL), lambda k: (k, 0))],
        out_specs=pl.BlockSpec((D_MIX, D_MODEL), lambda k: (0, 0)),
        out_shape=jax.ShapeDtypeStruct((D_MIX, D_MODEL), F32),
        compiler_params=_params(("arbitrary",)),
    )(y, dx)


def grad_w_in(h, dz, chip):
    s = h.shape[0]
    rows = D_IN // N_CHIPS

    def body(chip_ref, h_ref, dz_ref, staged_ref, own_ref, acc):
        k = pl.program_id(0)

        @pl.when(k == 0)
        def _():
            acc[...] = jnp.zeros_like(acc)

        acc[...] += _dot_tn(dz_ref[...], h_ref[...])

        @pl.when(k == s // TILE_DW - 1)
        def _():
            for j in range(N_CHIPS):
                part = acc[j * rows:(j + 1) * rows, :]
                staged_ref[j] = part.astype(BF16)

                @pl.when(chip_ref[0] == j)
                def _():
                    own_ref[...] = part

    return pl.pallas_call(
        body, name="grad_w_in",
        grid_spec=pltpu.PrefetchScalarGridSpec(
            num_scalar_prefetch=1, grid=(s // TILE_DW,),
            in_specs=[pl.BlockSpec((TILE_DW, D_MODEL), lambda k, c: (k, 0)),
                      pl.BlockSpec((TILE_DW, D_IN), lambda k, c: (k, 0))],
            out_specs=[pl.BlockSpec((N_CHIPS, rows, D_MODEL), lambda k, c: (0, 0, 0)),
                       pl.BlockSpec((rows, D_MODEL), lambda k, c: (0, 0))],
            scratch_shapes=[pltpu.VMEM((D_IN, D_MODEL), F32)]),
        out_shape=[jax.ShapeDtypeStruct((N_CHIPS, rows, D_MODEL), BF16), jax.ShapeDtypeStruct((rows, D_MODEL), F32)],
        compiler_params=_params(("arbitrary",)),
    )(chip, h, dz)


def out_fwd_loss(y, w, x, g, tgt):
    s = x.shape[0]

    def body(y_ref, w_ref, x_ref, g_ref, t_ref, l_ref, dx_ref, dg_ref):
        @pl.when(pl.program_id(0) == 0)
        def _():
            l_ref[...] = jnp.zeros_like(l_ref)
            dg_ref[...] = jnp.zeros_like(dg_ref)

        xv = x_ref[...] + _dot(y_ref[...], w_ref[...])
        gv = g_ref[...]
        rs = lax.rsqrt(jnp.mean(xv * xv, axis=-1, keepdims=True) + NORM_EPS)
        xh = xv * rs
        e = xh * gv - t_ref[...]
        part = 0.5 * jnp.sum(jnp.mean(e * e, axis=-1, keepdims=True), axis=0, keepdims=True)
        l_ref[...] += jnp.broadcast_to(part, l_ref.shape)
        dy = e * (1.0 / D_MODEL)
        dg_ref[...] += _put_row(dg_ref.shape, 0, jnp.sum(dy * xh, axis=0, keepdims=True))
        dn = dy * gv
        dx_ref[...] = rs * (dn - xh * jnp.mean(dn * xh, axis=-1, keepdims=True))

    return pl.pallas_call(
        body, name="out_fwd_loss", grid=(s // TILE_IN,),
        in_specs=[pl.BlockSpec((TILE_IN, D_MIX), lambda i: (i, 0)),
                  pl.BlockSpec((D_MIX, D_MODEL), lambda i: (0, 0)),
                  pl.BlockSpec((TILE_IN, D_MODEL), lambda i: (i, 0)),
                  pl.BlockSpec((1, D_MODEL), lambda i: (0, 0)),
                  pl.BlockSpec((TILE_IN, D_MODEL), lambda i: (i, 0))],
        out_specs=[pl.BlockSpec((SUBLANES, 128), lambda i: (0, 0)),
                   pl.BlockSpec((TILE_IN, D_MODEL), lambda i: (i, 0)),
                   pl.BlockSpec((SUBLANES, D_MODEL), lambda i: (0, 0))],
        out_shape=[jax.ShapeDtypeStruct((SUBLANES, 128), F32), jax.ShapeDtypeStruct((s, D_MODEL), F32),
                   jax.ShapeDtypeStruct((SUBLANES, D_MODEL), F32)],
        compiler_params=_params(("arbitrary",)),
    )(y, w, x, g, tgt)


def _attn_bias(dil):
    qi = np.arange(ATTN_BLOCK)[:, None]
    ki = np.arange(2 * ATTN_BLOCK)[None, :]
    delta = qi + ATTN_BLOCK - ki
    band = (delta >= 0) & (delta <= ATTN_BLOCK)
    out = np.empty((2, N_HEADS, ATTN_BLOCK, 2 * ATTN_BLOCK), np.float32)
    for f in range(2):
        ok = band & ((ki >= ATTN_BLOCK) | (f == 0))
        for h in range(N_HEADS):
            out[f, h] = np.where(ok, -ALIBI_SLOPES[h] * dil * delta, NEG_BIG)
    return jnp.asarray(out.reshape(2, N_HEADS * ATTN_BLOCK, 2 * ATTN_BLOCK))


def _stack_heads(a, head):
    return jnp.concatenate([jnp.where(head == h, a, jnp.zeros_like(a)) for h in range(N_HEADS)], axis=0)


def _unstack_heads(a, head):
    out = a[:ATTN_BLOCK]
    for h in range(1, N_HEADS):
        out = jnp.where(head == h, a[h * ATTN_BLOCK:(h + 1) * ATTN_BLOCK], out)
    return out


def _head_column(a):
    return jnp.concatenate([a[:, h:h + 1] for h in range(N_HEADS)], axis=0)


def _attn_specs(n_blocks):
    rows = ATTN_QB * ATTN_BLOCK
    cur = lambda c, w=GROUP_W: pl.BlockSpec((rows, w), lambda n, c=c: (n, c))
    prev = lambda c: pl.BlockSpec((ATTN_BLOCK, GROUP_W), lambda n, c=c: (jnp.maximum(n * ATTN_QB - 1, 0), c))
    nxt = lambda c, w=GROUP_W: pl.BlockSpec((ATTN_BLOCK, w),
                                            lambda n, c=c: (jnp.minimum(n * ATTN_QB + ATTN_QB, n_blocks - 1), c))
    return cur, prev, nxt


def _keys(kp_ref, k_ref, j):
    prev = kp_ref[...] if j == 0 else k_ref[(j - 1) * ATTN_BLOCK:j * ATTN_BLOCK, :]
    return jnp.concatenate([prev, k_ref[j * ATTN_BLOCK:(j + 1) * ATTN_BLOCK, :]], axis=0)


def attn_fwd(qkv, dil):
    s = qkv.shape[0]
    n_blocks = s // ATTN_BLOCK
    bps = n_blocks // dil
    rows = ATTN_QB * ATTN_BLOCK

    def body(q_ref, kp_ref, k_ref, vp_ref, v_ref, bias_ref, o_ref, lse_ref):
        n = pl.program_id(0)
        head = _head_of_lane((ATTN_BLOCK, GROUP_W))
        for j in range(ATTN_QB):
            sl = slice(j * ATTN_BLOCK, (j + 1) * ATTN_BLOCK)
            first = (((n * ATTN_QB + j) % bps) == 0).astype(jnp.int32)
            qs = _stack_heads(q_ref[sl, :], head)
            sc = _dot_nt(qs, _keys(kp_ref, k_ref, j)) * ATTN_SCALE + bias_ref[first]
            m = jnp.max(sc, axis=-1, keepdims=True)
            pr = jnp.exp(sc - m)
            l = jnp.sum(pr, axis=-1, keepdims=True)
            oh = _dot(pr.astype(BF16), _keys(vp_ref, v_ref, j)) / l
            o_ref[sl, :] = _unstack_heads(oh, head).astype(BF16)
            lse_ref[sl, :] = _unstack_heads(jnp.broadcast_to(m + jnp.log(l), oh.shape), head)

    cur, prev, _ = _attn_specs(n_blocks)
    bias = _attn_bias(dil)
    return pl.pallas_call(
        body, name=f"attn_fwd_d{dil}", grid=(n_blocks // ATTN_QB,),
        in_specs=[cur(0), prev(1), cur(1), prev(2), cur(2), pl.BlockSpec(bias.shape, lambda n: (0, 0, 0))],
        out_specs=[cur(0), cur(0)],
        out_shape=[jax.ShapeDtypeStruct((s, GROUP_W), BF16), jax.ShapeDtypeStruct((s, GROUP_W), F32)],
        compiler_params=_params(("parallel",)),
    )(qkv, qkv, qkv, qkv, qkv, bias)


def attn_bwd(qkv, do, lse, dlt, dil):
    s = qkv.shape[0]
    n_blocks = s // ATTN_BLOCK
    bps = n_blocks // dil
    rows = ATTN_QB * ATTN_BLOCK

    def body(q_ref, qn_ref, kp_ref, k_ref, vp_ref, v_ref, do_ref, don_ref, lse_ref, lsen_ref, dl_ref, dln_ref,
             bias_ref, out_ref, dk_acc, dv_acc):
        n = pl.program_id(0)
        head = _head_of_lane((ATTN_BLOCK, GROUP_W))
        dk_acc[...] = jnp.zeros_like(dk_acc)
        dv_acc[...] = jnp.zeros_like(dv_acc)

        def pair(qj, doj, lsej, dlj, kk, vv, bias, keep):
            qs = _stack_heads(qj, head)
            dos = _stack_heads(doj, head)
            sc = _dot_nt(qs, kk) * ATTN_SCALE + bias
            if keep is None:
                pr = jnp.exp(sc - _head_column(lsej))
            else:
                pr = jnp.exp(jnp.minimum(sc - _head_column(lsej), 0.0)) * keep
            dp = _dot_nt(dos, vv)
            ds = (pr * (dp - _head_column(dlj)) * ATTN_SCALE).astype(BF16)
            return ds, _dot_tn(ds, qs), _dot_tn(pr.astype(BF16), dos)

        for j in range(ATTN_QB):
            sl = slice(j * ATTN_BLOCK, (j + 1) * ATTN_BLOCK)
            first = (((n * ATTN_QB + j) % bps) == 0).astype(jnp.int32)
            kk = _keys(kp_ref, k_ref, j)
            ds, dks, dvs = pair(q_ref[sl, :], do_ref[sl, :], lse_ref[sl, :], dl_ref[sl, :],
                                kk, _keys(vp_ref, v_ref, j), bias_ref[first], None)
            out_ref[sl, 0:GROUP_W] = _unstack_heads(_dot(ds, kk), head)
            acc = slice(j * ATTN_BLOCK, (j + 2) * ATTN_BLOCK)
            dk_acc[acc, :] += dks
            dv_acc[acc, :] += dvs

        nxt = n * ATTN_QB + ATTN_QB
        valid = ((nxt < n_blocks) & ((nxt % bps) != 0)).astype(F32)
        last = slice((ATTN_QB - 1) * ATTN_BLOCK, ATTN_QB * ATTN_BLOCK)
        _, dks, dvs = pair(qn_ref[...], don_ref[...], lsen_ref[...], dln_ref[...], k_ref[last, :], v_ref[last, :],
                           bias_ref[0][:, :ATTN_BLOCK], valid)
        acc = slice(ATTN_QB * ATTN_BLOCK, (ATTN_QB + 1) * ATTN_BLOCK)
        dk_acc[acc, :] += dks
        dv_acc[acc, :] += dvs
        out_ref[:, GROUP_W:2 * GROUP_W] = dk_acc[ATTN_BLOCK:, :]
        out_ref[:, 2 * GROUP_W:3 * GROUP_W] = dv_acc[ATTN_BLOCK:, :]

    cur, prev, nxt = _attn_specs(n_blocks)
    bias = _attn_bias(dil)
    return pl.pallas_call(
        body, name=f"attn_bwd_d{dil}", grid=(n_blocks // ATTN_QB,),
        in_specs=[cur(0), nxt(0), prev(1), cur(1), prev(2), cur(2), cur(0), nxt(0),
                  cur(0, LANES), nxt(0, LANES), cur(0, LANES), nxt(0, LANES),
                  pl.BlockSpec(bias.shape, lambda n: (0, 0, 0))],
        out_specs=pl.BlockSpec((rows, 3 * GROUP_W), lambda n: (n, 0)),
        out_shape=jax.ShapeDtypeStruct((s, 3 * GROUP_W), F32),
        scratch_shapes=[pltpu.VMEM(((ATTN_QB + 1) * ATTN_BLOCK, GROUP_W), F32),
                        pltpu.VMEM(((ATTN_QB + 1) * ATTN_BLOCK, GROUP_W), F32)],
        compiler_params=_params(("parallel",)),
    )(qkv, qkv, qkv, qkv, qkv, qkv, do, do, lse, lse, dlt, dlt, bias)


def _zcol(c):
    return pl.BlockSpec((TILE_MIX, GROUP_W), lambda i, c=c: (i, c))


def _zhalo(c):
    per = TILE_MIX // SUBLANES
    return pl.BlockSpec((SUBLANES, GROUP_W), lambda i, c=c: (jnp.maximum(i * per - 1, 0), c))


def _full(shape):
    return pl.BlockSpec(shape, lambda i: tuple(0 for _ in shape))


def _of_layer(a, l):
    rest = a.shape[1:]
    return pl.BlockSpec((None,) + rest, lambda i: (l,) + tuple(0 for _ in rest))


def _softplus_neg(lam):
    nl = -lam
    return jnp.maximum(nl, 0.0) + jnp.log1p(jnp.exp(-jnp.abs(nl)))


def _lru_gates(xb, wa_ref, wx_ref, ba, bx, lam):
    xbb = xb.astype(BF16)
    r = jax.nn.sigmoid(_dot(xbb, wa_ref[...]) + ba)
    ig = jax.nn.sigmoid(_dot(xbb, wx_ref[...]) + bx)
    log_a = (-RG_C * r) * _softplus_neg(lam)
    a = jnp.exp(log_a)
    mult = jnp.sqrt(_neg_expm1(2.0 * log_a))
    return r, ig, a, mult


LRU_SAVED = 5


def _gmlp_spatial(ws_ref, vvb, head):
    outs = []
    for j in range(vvb.shape[0] // GMLP_CHUNK):
        blk = vvb[j * GMLP_CHUNK:(j + 1) * GMLP_CHUNK, :]
        acc = jnp.zeros((GMLP_CHUNK, GROUP_W), F32)
        for h in range(N_HEADS):
            acc = jnp.where(head[:GMLP_CHUNK] == h, _dot(ws_ref[h], blk), acc)
        outs.append(acc)
    return jnp.concatenate(outs, axis=0)


def mix_fwd(z, attn, wts, l):
    s = z.shape[0]
    d4, d16 = PATTERN_DILS[1], PATTERN_DILS[2]

    def body(ax_ref, ab_ref, ac_ref, ag_ref, rx_ref, rg_ref, cu_ref, cv_ref, cg_ref, dg_ref,
             axh_ref, ach_ref, rxh_ref, o1_ref, l1_ref, o4_ref, l4_ref, o16_ref, l16_ref,
             caw_ref, crw_ref, crb_ref, wa_ref, wx_ref, ba_ref, bx_ref, lam_ref, gng_ref, ws_ref, bs_ref,
             y_ref, hl_ref, o_ref, lse_ref, lse4_ref, lse16_ref, lru_ref, carry, *stage):
        st_a, st_b, st_c, st_d, st_e = (stage[2 * k:2 * k + 2] for k in range(5))
        i = pl.program_id(0)

        @pl.when(i == 0)
        def _():
            carry[...] = jnp.zeros_like(carry)

        nz = (i > 0).astype(F32)
        head = _head_of_lane((TILE_MIX, GROUP_W))

        pa = ac_ref[...] * ax_ref[...]
        pah = ach_ref[...] * axh_ref[...] * nz
        cv = caw_ref[2:3, :] * pa + caw_ref[1:2, :] * _shift_down(pa, pah, 1) + caw_ref[0:1, :] * _shift_down(pa, pah, 2)
        y_ref[:, 0:GROUP_W] = (ab_ref[...] * cv * _silu(ag_ref[...])).astype(BF16)

        rx = rx_ref[...]
        rxh = rxh_ref[...] * nz
        xb = (crw_ref[3:4, :] * rx + crw_ref[2:3, :] * _shift_down(rx, rxh, 1) + crw_ref[1:2, :] * _shift_down(rx, rxh, 2)
              + crw_ref[0:1, :] * _shift_down(rx, rxh, 3) + crb_ref[...])
        r, ig, a, mult = _lru_gates(xb, wa_ref, wx_ref, ba_ref[...], bx_ref[...], lam_ref[...])
        for k, val in enumerate((xb, r, ig, a, mult)):
            lru_ref[:, k * GROUP_W:(k + 1) * GROUP_W] = val
        hl = _scan_causal(a, mult * (ig * xb), carry[SUBLANES - 1:SUBLANES, :])
        hl_ref[...] = hl
        carry[...] = hl[TILE_MIX - SUBLANES:, :]
        y_ref[:, GROUP_W:2 * GROUP_W] = (hl * _silu(rg_ref[...])).astype(BF16)

        u = _gelu(cu_ref[...])
        gv = _gelu(cv_ref[...])
        rs = lax.rsqrt(jnp.mean(gv * gv, axis=-1, keepdims=True) + NORM_EPS)
        vvb = (gv * rs * gng_ref[...]).astype(BF16)
        sp = _gmlp_spatial(ws_ref, vvb, head) + jnp.concatenate([bs_ref[...]] * (TILE_MIX // GMLP_CHUNK), axis=0)
        y_ref[:, 2 * GROUP_W:3 * GROUP_W] = (u * sp * _silu(cg_ref[...])).astype(BF16)

        ops = (o1_ref[...].astype(F32), _interleave_load(o4_ref, d4, st_a), _interleave_load(o16_ref, d16, st_b))
        lps = (l1_ref[...], _interleave_load(l4_ref, d4, st_c), _interleave_load(l16_ref, d16, st_d))
        m = jnp.maximum(jnp.maximum(lps[0], lps[1]), lps[2])
        zsum = jnp.zeros_like(m)
        o = jnp.zeros_like(m)
        for op, lp in zip(ops, lps):
            w = jnp.exp(lp - m)
            zsum = zsum + w
            o = o + w * op
        o = o / zsum
        lse = m + jnp.log(zsum)
        lse = _per_head_lanes([lse[:, h * HEAD_DIM:h * HEAD_DIM + 1] for h in range(N_HEADS)])
        o_ref[...] = o
        lse_ref[...] = lse
        _deinterleave_store(lse, st_e, ((d4, lse4_ref), (d16, lse16_ref)))
        y_ref[:, 3 * GROUP_W:4 * GROUP_W] = (o * _silu(dg_ref[...])).astype(BF16)

    row = pl.BlockSpec((TILE_MIX, GROUP_W), lambda i: (i, 0))
    r4 = _residue_block(d4, TILE_MIX, GROUP_W)
    r16 = _residue_block(d16, TILE_MIX, GROUP_W)
    crow = pl.BlockSpec((TILE_MIX, LANES), lambda i: (i, 0))
    c4 = _residue_block(d4, TILE_MIX, LANES)
    c16 = _residue_block(d16, TILE_MIX, LANES)
    names = ("caw", "crw", "crb", "wa", "wx", "ba", "bx", "lam", "gng", "ws", "bs")
    in_specs = ([_zcol(c) for c in (C_AX, C_AB, C_AC, C_AG, C_RX, C_RG, C_CU, C_CV, C_CG, C_DG)]
                + [_zhalo(C_AX), _zhalo(C_AC), _zhalo(C_RX), row, row, r4, r4, r16, r16]
                + [_of_layer(wts[k], l) for k in names])
    return pl.pallas_call(
        body, name="mix_fwd", grid=(s // TILE_MIX,),
        in_specs=in_specs,
        out_specs=[pl.BlockSpec((TILE_MIX, D_MIX), lambda i: (i, 0)), row, row, crow, c4, c16,
                   pl.BlockSpec((TILE_MIX, LRU_SAVED * GROUP_W), lambda i: (i, 0))],
        out_shape=([jax.ShapeDtypeStruct((s, D_MIX), BF16)] + [jax.ShapeDtypeStruct((s, GROUP_W), F32)] * 2
                   + [jax.ShapeDtypeStruct((s, LANES), F32), _by_residue(s, d4, LANES, F32),
                      _by_residue(s, d16, LANES, F32), jax.ShapeDtypeStruct((s, LRU_SAVED * GROUP_W), F32)]),
        scratch_shapes=([pltpu.VMEM((SUBLANES, GROUP_W), F32)] + _stage_scratch(TILE_MIX, GROUP_W, 4)
                        + _stage_scratch(TILE_MIX, LANES, 1)),
        compiler_params=_params(("arbitrary",)),
    )(*([z] * 13), *[a for pair in attn for a in pair], *[wts[k] for k in names])


def mix_bwd(dy, z, hl, lru, dqkv, ddg, wts, l):
    s = z.shape[0]
    d4, d16 = PATTERN_DILS[1], PATTERN_DILS[2]
    n_tiles = s // TILE_MIX

    def body(dya_ref, dyb_ref, dyc_ref, ax_ref, ab_ref, ac_ref, ag_ref, rx_ref, rg_ref, cu_ref, cv_ref, cg_ref,
             axh_ref, ach_ref, rxh_ref, hl_ref, hlh_ref, lru_ref, dqkv1_ref, dqkv4_ref, dqkv16_ref, ddg_ref,
             caw_ref, crw_ref, crb_ref, wa_ref, wx_ref, ba_ref, bx_ref, lam_ref, gng_ref, ws_ref, wst_ref, bs_ref,
             dz_ref, ga_ref, gr_ref, gn_ref, gwa_ref, gwx_ref, gws_ref, gbs_ref,
             c_dcv, c_g, c_a, c_dxb, *stage):
        st_a, st_b = stage[:len(stage) // 2], stage[len(stage) // 2:]
        step = pl.program_id(0)
        i = n_tiles - 1 - step

        @pl.when(step == 0)
        def _():
            for r in (c_dcv, c_g, c_a, c_dxb, ga_ref, gr_ref, gn_ref, gwa_ref, gwx_ref, gws_ref, gbs_ref):
                r[...] = jnp.zeros_like(r)

        nz = (i > 0).astype(F32)
        head = _head_of_lane((TILE_MIX, GROUP_W))
        shp8 = (SUBLANES, GROUP_W)
        colsum = lambda v: jnp.sum(v, axis=0, keepdims=True)

        ax, ab, ac, ag = ax_ref[...], ab_ref[...], ac_ref[...], ag_ref[...]
        dya = dya_ref[...]
        pa = ac * ax
        pah = ach_ref[...] * axh_ref[...] * nz
        pa1 = _shift_down(pa, pah, 1)
        pa2 = _shift_down(pa, pah, 2)
        cv = caw_ref[2:3, :] * pa + caw_ref[1:2, :] * pa1 + caw_ref[0:1, :] * pa2
        sg = _silu(ag)
        dz_ref[:, C_AB * GROUP_W:(C_AB + 1) * GROUP_W] = (dya * cv * sg).astype(BF16)
        dz_ref[:, C_AG * GROUP_W:(C_AG + 1) * GROUP_W] = (dya * ab * cv * _dsilu(ag)).astype(BF16)
        dcv = dya * ab * sg
        nxt = c_dcv[...]
        dpa = caw_ref[2:3, :] * dcv + caw_ref[1:2, :] * _shift_up(dcv, nxt, 1) + caw_ref[0:1, :] * _shift_up(dcv, nxt, 2)
        c_dcv[...] = dcv[:SUBLANES, :]
        dz_ref[:, C_AC * GROUP_W:(C_AC + 1) * GROUP_W] = (dpa * ax).astype(BF16)
        dz_ref[:, C_AX * GROUP_W:(C_AX + 1) * GROUP_W] = (dpa * ac).astype(BF16)
        ga_ref[...] += (_put_row(shp8, 2, colsum(dcv * pa)) + _put_row(shp8, 1, colsum(dcv * pa1))
                        + _put_row(shp8, 0, colsum(dcv * pa2)))

        rx, rg = rx_ref[...], rg_ref[...]
        dyb = dyb_ref[...]
        rxh = rxh_ref[...] * nz
        rx1, rx2, rx3 = _shift_down(rx, rxh, 1), _shift_down(rx, rxh, 2), _shift_down(rx, rxh, 3)
        xb, r, ig, a, mult = (lru_ref[:, k * GROUP_W:(k + 1) * GROUP_W] for k in range(LRU_SAVED))
        lam = lam_ref[...]
        sp = _softplus_neg(lam)
        hl = hl_ref[...]
        hprev = _shift_down(hl, hlh_ref[...] * nz, 1)
        dz_ref[:, C_RG * GROUP_W:(C_RG + 1) * GROUP_W] = (dyb * hl * _dsilu(rg)).astype(BF16)
        dh = dyb * _silu(rg)
        a_next = _shift_up(a, c_a[...], 1)
        g = _scan_anticausal(a_next, dh, c_g[0:1, :])
        c_g[...] = g[:SUBLANES, :]
        c_a[...] = a[:SUBLANES, :]
        u = ig * xb
        da = g * hprev
        dmult = g * u
        du = g * mult
        dlog_a = da * a - dmult * (a * a) / mult
        dr = dlog_a * (-RG_C * sp)
        dga = dr * r * (1.0 - r)
        dgx = (du * xb) * ig * (1.0 - ig)
        dgab, dgxb = dga.astype(BF16), dgx.astype(BF16)
        dxb = du * ig + _dot_nt(dgab, wa_ref[...]) + _dot_nt(dgxb, wx_ref[...])
        xbb = xb.astype(BF16)
        gwa_ref[...] += _dot_tn(xbb, dgab)
        gwx_ref[...] += _dot_tn(xbb, dgxb)
        nxt = c_dxb[...]
        drx = (crw_ref[3:4, :] * dxb + crw_ref[2:3, :] * _shift_up(dxb, nxt, 1) + crw_ref[1:2, :] * _shift_up(dxb, nxt, 2)
               + crw_ref[0:1, :] * _shift_up(dxb, nxt, 3))
        c_dxb[...] = dxb[:SUBLANES, :]
        dz_ref[:, C_RX * GROUP_W:(C_RX + 1) * GROUP_W] = drx.astype(BF16)
        dlam = colsum(dlog_a * (-RG_C * r)) * (-jax.nn.sigmoid(-lam))
        gr_ref[...] += (_put_row(shp8, 3, colsum(dxb * rx)) + _put_row(shp8, 2, colsum(dxb * rx1))
                        + _put_row(shp8, 1, colsum(dxb * rx2)) + _put_row(shp8, 0, colsum(dxb * rx3))
                        + _put_row(shp8, 4, colsum(dxb)) + _put_row(shp8, 5, colsum(dga))
                        + _put_row(shp8, 6, colsum(dgx)) + _put_row(shp8, 7, dlam))

        cu, cvv, cg = cu_ref[...], cv_ref[...], cg_ref[...]
        dyc = dyc_ref[...]
        u_c, du_c = _gelu_and_grad(cu)
        gv, dgv_c = _gelu_and_grad(cvv)
        rs = lax.rsqrt(jnp.mean(gv * gv, axis=-1, keepdims=True) + NORM_EPS)
        vh = gv * rs
        gng = gng_ref[...]
        vvb = (vh * gng).astype(BF16)
        spat = _gmlp_spatial(ws_ref, vvb, head) + jnp.concatenate([bs_ref[...]] * (TILE_MIX // GMLP_CHUNK), axis=0)
        sgc = _silu(cg)
        dz_ref[:, C_CU * GROUP_W:(C_CU + 1) * GROUP_W] = (dyc * spat * sgc * du_c).astype(BF16)
        dz_ref[:, C_CG * GROUP_W:(C_CG + 1) * GROUP_W] = (dyc * u_c * spat * _dsilu(cg)).astype(BF16)
        dsp = dyc * u_c * sgc
        dspb = dsp.astype(BF16)
        tril = (lax.broadcasted_iota(jnp.int32, (GMLP_CHUNK, GMLP_CHUNK), 0)
                >= lax.broadcasted_iota(jnp.int32, (GMLP_CHUNK, GMLP_CHUNK), 1))
        head_c = head[:GMLP_CHUNK]
        dvv_parts = []
        gbs = jnp.zeros((GMLP_CHUNK, GROUP_W), F32)
        for j in range(TILE_MIX // GMLP_CHUNK):
            sl = slice(j * GMLP_CHUNK, (j + 1) * GMLP_CHUNK)
            dblk = dspb[sl, :]
            vblk = vvb[sl, :]
            gbs = gbs + dsp[sl, :]
            acc = jnp.zeros((GMLP_CHUNK, GROUP_W), F32)
            for h in range(N_HEADS):
                acc = jnp.where(head_c == h, _dot(wst_ref[h], dblk), acc)
                dm = jnp.where(head_c == h, dblk, jnp.zeros_like(dblk))
                gws_ref[h] += jnp.where(tril, _dot_nt(dm, vblk), 0.0)
            dvv_parts.append(acc)
        gbs_ref[...] += gbs
        dvv = jnp.concatenate(dvv_parts, axis=0)
        gn_ref[...] += _put_row(shp8, 0, colsum(dvv * vh))
        dvh = dvv * gng
        dgv = rs * (dvh - vh * jnp.mean(dvh * vh, axis=-1, keepdims=True))
        dz_ref[:, C_CV * GROUP_W:(C_CV + 1) * GROUP_W] = (dgv * dgv_c).astype(BF16)

        dsum = dqkv1_ref[...] + _interleave_load(dqkv4_ref, d4, st_a) + _interleave_load(dqkv16_ref, d16, st_b)
        dz_ref[:, C_DQ * GROUP_W:(C_DV + 1) * GROUP_W] = dsum.astype(BF16)
        dz_ref[:, C_DG * GROUP_W:(C_DG + 1) * GROUP_W] = ddg_ref[...].astype(BF16)

    per = TILE_MIX // SUBLANES
    qkv_w = 3 * GROUP_W
    rev = lambda c: pl.BlockSpec((TILE_MIX, GROUP_W), lambda t, c=c: (n_tiles - 1 - t, c))
    revh = lambda c: pl.BlockSpec((SUBLANES, GROUP_W),
                                  lambda t, c=c: (jnp.maximum((n_tiles - 1 - t) * per - 1, 0), c))
    revr = lambda dil: pl.BlockSpec((dil, TILE_MIX // dil, qkv_w), lambda t: (0, n_tiles - 1 - t, 0))
    names = ("caw", "crw", "crb", "wa", "wx", "ba", "bx", "lam", "gng", "ws", "wst", "bs")
    in_specs = ([rev(0), rev(1), rev(2)]
                + [rev(c) for c in (C_AX, C_AB, C_AC, C_AG, C_RX, C_RG, C_CU, C_CV, C_CG)]
                + [revh(C_AX), revh(C_AC), revh(C_RX), rev(0), revh(0),
                   pl.BlockSpec((TILE_MIX, LRU_SAVED * GROUP_W), lambda t: (n_tiles - 1 - t, 0)),
                   pl.BlockSpec((TILE_MIX, qkv_w), lambda t: (n_tiles - 1 - t, 0)), revr(d4), revr(d16), rev(0)]
                + [_of_layer(wts[k], l) for k in names])
    small = jax.ShapeDtypeStruct((SUBLANES, GROUP_W), F32)
    sq = jax.ShapeDtypeStruct((GROUP_W, GROUP_W), F32)
    out_shape = [jax.ShapeDtypeStruct((s, D_IN), BF16), small, small, small, sq, sq,
                 jax.ShapeDtypeStruct((N_HEADS, GMLP_CHUNK, GMLP_CHUNK), F32),
                 jax.ShapeDtypeStruct((GMLP_CHUNK, GROUP_W), F32)]
    out_specs = ([pl.BlockSpec((TILE_MIX, D_IN), lambda t: (n_tiles - 1 - t, 0))]
                 + [_full(o.shape) for o in out_shape[1:]])
    return pl.pallas_call(
        body, name="mix_bwd", grid=(n_tiles,),
        in_specs=in_specs, out_specs=out_specs, out_shape=out_shape,
        scratch_shapes=[pltpu.VMEM((SUBLANES, GROUP_W), F32)] * 4 + _stage_scratch(TILE_MIX, qkv_w, 2),
        compiler_params=_params(("arbitrary",)),
    )(dy, dy, dy, *([z] * 12), hl, hl, lru, *dqkv, ddg, *[wts[k] for k in names])


def prepare_small_weights(p):
    tril = jnp.tril(jnp.ones((GMLP_CHUNK, GMLP_CHUNK), dtype=bool))
    ws = jnp.where(tril, p["gmlp_ws"], 0.0).astype(BF16)
    row = lambda a: a[:, None, :]
    eye = jnp.eye(N_HEADS, dtype=F32)
    bd = lambda w: (w[:, :, :, None, :] * eye[None, :, None, :, None]).reshape(-1, GROUP_W, GROUP_W).astype(BF16)
    return dict(
        caw=p["conv_a_w"], crw=p["conv_r_w"], crb=row(p["conv_r_b"]),
        wa=bd(p["lru_wa"]), wx=bd(p["lru_wx"]),
        ba=row(p["lru_ba"]), bx=row(p["lru_bx"]), lam=row(p["lru_lambda"]), gng=row(p["gmlp_norm_g"]),
        ws=ws, wst=jnp.swapaxes(ws, 2, 3),
        bs=jnp.repeat(jnp.swapaxes(p["gmlp_bs"], 1, 2), HEAD_DIM, axis=2))


def _flat(a):
    return a.reshape(a.shape[0] * a.shape[1], a.shape[2])


def _split(a, dil):
    return a.reshape(dil, a.shape[0] // dil, a.shape[1])


def local_step(x, tgt, final_g, depth, chip, layer_weights, projections_done):
    saved = []
    for l in range(depth):
        gain, w_in_l, rest = layer_weights(l, x)
        z, h, *qkvs = in_fwd(x, gain, w_in_l)
        w_out_l, wts = rest(z)
        qkvs = [_flat(q) if q.ndim == 3 else q for q in qkvs]
        attn = []
        for q, d in zip(qkvs, PATTERN_DILS):
            o_p, lse_p = attn_fwd(q, d)
            attn.append((o_p, lse_p) if d == 1 else (_split(o_p, d), _split(lse_p, d)))
        y, hl, o, lse, lse4, lse16, lru = mix_fwd(z, attn, wts, l)
        saved.append(dict(x=x, z=z, h=h, y=y, hl=hl, o=o, qkvs=qkvs, lses=(lse, _flat(lse4), _flat(lse16)), wts=wts, lru=lru,
                          gain=gain, w_in=w_in_l, w_out=w_out_l))
        if l < depth - 1:
            x = out_fwd(y, w_out_l, x)
        else:
            loss, dx, dfg = out_fwd_loss(y, w_out_l, x, final_g[None, :], tgt)
    raw = {k: [None] * depth for k in ("gain", "a", "r", "n", "wa", "wx", "ws", "bs")}
    zero = None
    for l in reversed(range(depth)):
        sv = saved[l]
        dy, ddg, do1, do4, do16, dl1, dl4, dl16 = out_bwd(dx, sv["w_out"], sv["z"], sv["o"])
        g_w_out = grad_w_out(sv["y"], dx)
        dqkv = []
        for q, do, lse, dl, d in zip(sv["qkvs"], (do1, _flat(do4), _flat(do16)), sv["lses"],
                                     (dl1, _flat(dl4), _flat(dl16)), PATTERN_DILS):
            g = attn_bwd(q, do, lse, dl, d)
            dqkv.append(g if d == 1 else _split(g, d))
        dz, ga, gr, gn, gwa, gwx, gws, gbs = mix_bwd(dy, sv["z"], sv["hl"], sv["lru"], dqkv, ddg, sv["wts"], l)
        gain = sv["gain"] if zero is None else sv["gain"] + zero
        zero = projections_done(l, *grad_w_in(sv["h"], dz, chip), g_w_out)
        if l == 0 and zero is not None:
            gain = gain + zero
        dx, dgn = in_bwd(dz, sv["w_in"], sv["x"], gain, dx)
        for k, g in zip(("gain", "a", "r", "n", "wa", "wx", "ws", "bs"), (dgn, ga, gr, gn, gwa, gwx, gws, gbs)):
            raw[k][l] = g
    st = {k: jnp.stack(v) for k, v in raw.items()}
    eye = jnp.eye(N_HEADS, dtype=F32)[None, :, None, :, None]
    diag = lambda g: (g.reshape(depth, N_HEADS, HEAD_DIM, N_HEADS, HEAD_DIM) * eye).sum(axis=3)
    grads = dict(
        norm_g=st["gain"][:, 0], conv_a_w=st["a"][:, :3], conv_r_w=st["r"][:, :4], conv_r_b=st["r"][:, 4],
        lru_ba=st["r"][:, 5], lru_bx=st["r"][:, 6], lru_lambda=st["r"][:, 7], gmlp_norm_g=st["n"][:, 0],
        lru_wa=diag(st["wa"]), lru_wx=diag(st["wx"]), gmlp_ws=st["ws"],
        gmlp_bs=jnp.swapaxes(st["bs"].reshape(depth, GMLP_CHUNK, N_HEADS, HEAD_DIM).sum(-1), 1, 2),
        final_g=dfg[0])
    return loss, dx, grads


MESH = pl.DeviceIdType.MESH
N_CHIPS = 4
N_DEV = 8
ANY = pl.BlockSpec(memory_space=pl.ANY)


def _place():
    x, y, c = lax.axis_index("x"), lax.axis_index("y"), lax.axis_index("c")
    chips = [(1 - x, y), (x, 1 - y), (1 - x, 1 - y)]
    return x, y, c, chips


def _remote(src, dst, ssem, rsem, to):
    return pltpu.make_async_remote_copy(src_ref=src, dst_ref=dst, send_sem=ssem, recv_sem=rsem,
                                        device_id=to, device_id_type=MESH)


HBM = pl.BlockSpec(memory_space=pltpu.HBM)
SEM = pl.BlockSpec(memory_space=pltpu.SEMAPHORE)
DATAFLOW = pltpu.SideEffectType.DATAFLOW_SIDE_EFFECTING
GATHER, SCATTER = "gather", "scatter"


def _chip_copies(mode, src_refs, land_refs, ssem, rsem):
    x, y, c, chips = _place()
    me = 2 * x + y
    n = len(src_refs)
    copies = []
    for k, (cx, cy) in enumerate(chips):
        for a in range(n):
            if mode == GATHER:
                src, dst = src_refs[a], land_refs[a].at[me]
            else:
                src, dst = src_refs[a].at[2 * cx + cy], land_refs[a].at[k]
            copies.append(_remote(src, dst, ssem.at[n * k + a], rsem.at[n * k + a], (cx, cy, c)))
    return copies


def exchange_start(mode, srcs, after, name):
    n = len(srcs)
    if mode == GATHER:
        lands = [lax.empty((N_CHIPS,) + s.shape, s.dtype) for s in srcs]
    else:
        lands = [lax.empty((N_CHIPS - 1,) + s.shape[1:], s.dtype) for s in srcs]
    extra = [] if after is None else [after]

    def body(*refs):
        src_refs, land_refs = refs[:n], refs[n:2 * n]
        ssem, rsem = refs[2 * n + len(extra)], refs[2 * n + len(extra) + 1]
        token = refs[-1]
        for cp in _chip_copies(mode, src_refs, land_refs, ssem, rsem):
            cp.start()
        token[...] = jnp.zeros_like(token)

    arrays = list(srcs) + lands
    return pl.pallas_call(
        body, name=name,
        out_shape=(pltpu.SemaphoreType.DMA((3 * n,)), pltpu.SemaphoreType.DMA((3 * n,)),
                   *[pltpu.HBM(a.shape, a.dtype) for a in arrays], jax.ShapeDtypeStruct((SUBLANES, LANES), F32)),
        in_specs=[HBM] * (2 * n) + [ANY] * len(extra),
        out_specs=(SEM, SEM, *[HBM] * (2 * n), pl.BlockSpec(memory_space=pltpu.VMEM)),
        input_output_aliases={i: 2 + i for i in range(2 * n)},
        compiler_params=pltpu.CompilerParams(has_side_effects=DATAFLOW),
    )(*[pltpu.with_memory_space_constraint(a, pltpu.HBM) for a in arrays], *extra)


def exchange_wait(mode, started, after, name):
    ssem, rsem, *thru, _ = started
    n = len(thru) // 2

    def body(*refs):
        src_refs, land_refs = refs[:n], refs[n:2 * n]
        ssem_ref, rsem_ref = refs[2 * n], refs[2 * n + 1]
        for cp in _chip_copies(mode, src_refs, land_refs, ssem_ref, rsem_ref):
            cp.wait_send()
            cp.wait_recv()

    outs = pl.pallas_call(
        body, name=name,
        out_shape=[pltpu.HBM(a.shape, a.dtype) for a in thru],
        in_specs=[HBM] * (2 * n) + [SEM, SEM, ANY],
        out_specs=[HBM] * (2 * n),
        input_output_aliases={i: i for i in range(2 * n)},
        compiler_params=pltpu.CompilerParams(has_side_effects=DATAFLOW),
    )(*thru, ssem, rsem, after)
    return outs[n:]


def sibling_exchange(p1, p2):
    def body(p1_ref, p2_ref, q1_ref, q2_ref, ssem, rsem):
        x, y, c, _ = _place()
        copies = [_remote(p_ref, q_ref, ssem.at[a], rsem.at[a], (x, y, 1 - c))
                  for a, (p_ref, q_ref) in enumerate(((p1_ref, q1_ref), (p2_ref, q2_ref)))]
        for cp in copies:
            cp.start()
        for cp in copies:
            cp.wait()

    return pl.pallas_call(
        body, name="sibling_exchange",
        in_specs=[ANY, ANY], out_specs=[ANY, ANY],
        out_shape=[jax.ShapeDtypeStruct(p.shape, p.dtype) for p in (p1, p2)],
        scratch_shapes=[pltpu.SemaphoreType.DMA((2,)), pltpu.SemaphoreType.DMA((2,))],
    )(p1, p2)


def all_reduce_small(v):
    r, n = v.shape
    piece = r // N_DEV

    def body(x_ref, out_ref, recv, ssem1, rsem1, ssem2, rsem2):
        x, y, c, _ = _place()
        me = 4 * x + 2 * y + c

        def peer(k):
            px = 1 - x if (k >> 2) & 1 else x
            py = 1 - y if (k >> 1) & 1 else y
            pc = 1 - c if k & 1 else c
            return (px, py, pc), 4 * px + 2 * py + pc

        def rows(ref, d):
            return ref.at[pl.ds(d * piece, piece), :]

        scatter = []
        for k in range(1, N_DEV):
            to, idx = peer(k)
            scatter.append(_remote(rows(x_ref, idx), recv.at[k], ssem1.at[k - 1], rsem1.at[k - 1], to))
            scatter[-1].start()
        acc = rows(x_ref, me)[...]
        for k in range(1, N_DEV):
            scatter[k - 1].wait_recv()
            acc = acc + recv[k]
        rows(out_ref, me)[...] = acc

        gather = []
        for k in range(1, N_DEV):
            to, _ = peer(k)
            gather.append(_remote(rows(out_ref, me), rows(out_ref, me), ssem2.at[k - 1], rsem2.at[k - 1], to))
            gather[-1].start()
        for k in range(1, N_DEV):
            to, idx = peer(k)
            _remote(rows(out_ref, idx), rows(out_ref, idx), ssem2.at[k - 1], rsem2.at[k - 1], to).wait_recv()
        for cp in scatter + gather:
            cp.wait_send()

    return pl.pallas_call(
        body, name="all_reduce_small",
        out_shape=jax.ShapeDtypeStruct((r, n), v.dtype),
        in_specs=[pl.BlockSpec(memory_space=pltpu.VMEM)],
        out_specs=pl.BlockSpec(memory_space=pltpu.VMEM),
        scratch_shapes=[pltpu.VMEM((N_DEV, piece, n), v.dtype)] + [pltpu.SemaphoreType.DMA((N_DEV - 1,))] * 4,
        compiler_params=pltpu.CompilerParams(vmem_limit_bytes=VMEM_LIMIT),
    )(v)


TILE_ROWS = 256


def _row_tile(r):
    return max(t for t in range(SUBLANES, TILE_ROWS + 1, SUBLANES) if r % t == 0)


def sum_partials(owns, parts):
    k, r, c = parts[0].shape
    depth = len(owns)
    tile = _row_tile(r)

    def body(buf_ref, o_ref, p_ref, out_ref):
        acc = o_ref[...]
        for i in range(k):
            acc = acc + p_ref[i].astype(F32)
        out_ref[0] = acc

    out = lax.empty((depth, r, c), F32)
    for l in range(depth):
        out = pl.pallas_call(
            functools.partial(body), name="sum_partials", grid=(r // tile,),
            in_specs=[ANY, pl.BlockSpec((tile, c), lambda i: (i, 0)), pl.BlockSpec((k, tile, c), lambda i: (0, i, 0))],
            out_specs=pl.BlockSpec((1, tile, c), lambda i, l=l: (l, i, 0)),
            out_shape=jax.ShapeDtypeStruct((depth, r, c), F32),
            input_output_aliases={0: 0},
            compiler_params=_params(("parallel",)),
        )(out, owns[l], parts[l])
    return out


def _adamw_update(w, g, m, v):
    m2 = ADAM_B1 * m + (1.0 - ADAM_B1) * g
    v2 = ADAM_B2 * v + (1.0 - ADAM_B2) * (g * g)
    m_hat = m2 / (1.0 - ADAM_B1 ** ADAM_STEP)
    v_hat = v2 / (1.0 - ADAM_B2 ** ADAM_STEP)
    return -ADAM_LR * (m_hat / (jnp.sqrt(v_hat) + ADAM_EPS) + ADAM_WD * w), m2, v2


def adamw_small(ws, gs, ms, vs):
    n = len(ws)

    def body(*refs):
        ins, outs = refs[:4 * n], refs[4 * n:]
        for i in range(n):
            d, m2, v2 = _adamw_update(ins[i][...], ins[n + i][...], ins[2 * n + i][...], ins[3 * n + i][...])
            outs[3 * i][...] = d
            outs[3 * i + 1][...] = m2
            outs[3 * i + 2][...] = v2

    outs = pl.pallas_call(
        body, name="adamw_small",
        out_shape=[jax.ShapeDtypeStruct(w.shape, F32) for w in ws for _ in range(3)],
    )(*ws, *gs, *ms, *vs)
    return [tuple(outs[3 * i:3 * i + 3]) for i in range(n)]


def adamw(w, ga, gb, m, v):
    n, r, c = w.shape
    tile = _row_tile(r)

    def body(w_ref, ga_ref, gb_ref, m_ref, v_ref, g_ref, d_ref, m2_ref, v2_ref):
        g = ga_ref[...] + gb_ref[...]
        g_ref[...] = g
        d_ref[...], m2_ref[...], v2_ref[...] = _adamw_update(w_ref[...], g, m_ref[...], v_ref[...])

    spec = pl.BlockSpec((1, tile, c), lambda j, i: (j, i, 0))
    return pl.pallas_call(
        body, name="adamw", grid=(n, r // tile),
        in_specs=[spec] * 5, out_specs=[spec] * 4,
        out_shape=[jax.ShapeDtypeStruct((n, r, c), F32)] * 4,
        compiler_params=_params(("parallel", "parallel")),
    )(w, ga, gb, m, v)


REPLICATED = ("norm_g", "conv_r_b", "lru_wa", "lru_ba", "lru_wx", "lru_bx", "lru_lambda", "gmlp_norm_g",
              "gmlp_ws", "gmlp_bs", "final_g")
CHIP_SHARDED_SMALL = ("conv_a_w", "conv_r_w")
PACK_LANES = 128


def _pack(arrays):
    flat = jnp.concatenate([a.reshape(-1) for a in arrays])
    pad = (-flat.shape[0]) % (TILE_ROWS * PACK_LANES)
    return jnp.pad(flat, (0, pad)).reshape(-1, PACK_LANES)


def _unpack(packed, shapes):
    flat = packed.reshape(-1)
    out, off = [], 0
    for shp in shapes:
        n = math.prod(shp)
        out.append(flat[off:off + n].reshape(shp))
        off += n
    return out


def kernel(x, norm_g, w_in, conv_a_w, conv_r_w, conv_r_b, lru_wa, lru_ba, lru_wx, lru_bx, lru_lambda, gmlp_norm_g, gmlp_ws, gmlp_bs, w_out, final_g, loss_target, m_norm_g, m_w_in, m_conv_a_w, m_conv_r_w, m_conv_r_b, m_lru_wa, m_lru_ba, m_lru_wx, m_lru_bx, m_lru_lambda, m_gmlp_norm_g, m_gmlp_ws, m_gmlp_bs, m_w_out, m_final_g, v_norm_g, v_w_in, v_conv_a_w, v_conv_r_w, v_conv_r_b, v_lru_wa, v_lru_ba, v_lru_wx, v_lru_bx, v_lru_lambda, v_gmlp_norm_g, v_gmlp_ws, v_gmlp_bs, v_w_out, v_final_g):
    names = ("norm_g", "w_in", "conv_a_w", "conv_r_w", "conv_r_b", "lru_wa", "lru_ba", "lru_wx", "lru_bx",
             "lru_lambda", "gmlp_norm_g", "gmlp_ws", "gmlp_bs", "w_out", "final_g")
    w = dict(zip(names, (norm_g, w_in, conv_a_w, conv_r_w, conv_r_b, lru_wa, lru_ba, lru_wx, lru_bx, lru_lambda,
                         gmlp_norm_g, gmlp_ws, gmlp_bs, w_out, final_g)))
    m = dict(zip(names, (m_norm_g, m_w_in, m_conv_a_w, m_conv_r_w, m_conv_r_b, m_lru_wa, m_lru_ba, m_lru_wx, m_lru_bx,
                         m_lru_lambda, m_gmlp_norm_g, m_gmlp_ws, m_gmlp_bs, m_w_out, m_final_g)))
    v = dict(zip(names, (v_norm_g, v_w_in, v_conv_a_w, v_conv_r_w, v_conv_r_b, v_lru_wa, v_lru_ba, v_lru_wx, v_lru_bx,
                         v_lru_lambda, v_gmlp_norm_g, v_gmlp_ws, v_gmlp_bs, v_w_out, v_final_g)))
    depth, _, in_cols = w_in.shape
    out_rows = w_out.shape[1]
    conv_ch = conv_a_w.shape[2]
    chip = 2 * lax.axis_index("x") + lax.axis_index("y")

    taps = conv_a_w.shape[1] + conv_r_w.shape[1]
    w_in_t, m_w_in_t, v_w_in_t = (jnp.swapaxes(a, 1, 2) for a in (w_in, m_w_in, v_w_in))
    w_in_h, w_out_h = w_in_t.astype(BF16), w_out.astype(BF16)
    conv_own = jnp.concatenate([conv_a_w, conv_r_w], axis=1).reshape(depth * taps, conv_ch)
    gathers, token = [], None
    for l in range(depth):
        groups = [[w_in_h[l]], [w_out_h[l], conv_own]] if l == 0 else [[w_in_h[l], w_out_h[l]]]
        gathers.append([])
        for i, srcs in enumerate(groups):
            gathers[l].append(exchange_start(GATHER, srcs, token, f"gather_start_{l}_{i}"))
            token = gathers[l][-1][-1]
    p = dict(w)

    def with_own(land, own):
        return lax.dynamic_update_slice(land, own[None], (chip,) + (0,) * own.ndim)

    def layer_weights(l, x_l):
        lands = list(exchange_wait(GATHER, gathers[l][0], x_l, f"gather_wait_{l}_0"))
        w_in_l = with_own(lands[0], w_in_h[l]).reshape(D_IN, D_MODEL)
        gain = norm_g[l][None, :]
        if l == 0:
            gain = gain + token[0, 0]

        def rest(z_l):
            if l == 0:
                lands.extend(exchange_wait(GATHER, gathers[l][1], z_l, f"gather_wait_{l}_1"))
                conv = with_own(lands[2], conv_own).reshape(N_CHIPS, depth, taps, conv_ch)
                conv = conv.transpose(1, 2, 0, 3).reshape(depth, taps, GROUP_W)
                p["conv_a_w"] = conv[:, :conv_a_w.shape[1]]
                p["conv_r_w"] = conv[:, conv_a_w.shape[1]:]
                p["prepared"] = prepare_small_weights(p)
            return with_own(lands[1], w_out_h[l]).reshape(D_MIX, D_MODEL), p["prepared"]

        return gain, w_in_l, rest

    scatters, owns = [None] * depth, [None] * depth

    def projections_done(l, g_w_in_by_chip, g_w_in_own, g_w_out):
        go = g_w_out.reshape(N_CHIPS, out_rows, D_MODEL)
        owns[l] = (g_w_in_own, lax.dynamic_index_in_dim(go, chip, axis=0, keepdims=False))
        scatters[l] = exchange_start(SCATTER, [g_w_in_by_chip, go.astype(BF16)], None, f"scatter_start_{l}")
        return scatters[l][-1][0, 0]

    loss8, dx, grads = local_step(x[0], loss_target[0], final_g, depth, chip.reshape(1), layer_weights,
                                  projections_done)

    res = {}
    small = REPLICATED + CHIP_SHARDED_SMALL
    packed = _pack([grads[k] for k in small] + [loss8[0, :1]])
    total = all_reduce_small(packed)
    *sums, loss = _unpack(total, [grads[k].shape for k in small] + [()])
    gs = dict(zip(small, sums))
    for k in CHIP_SHARDED_SMALL:
        gs[k] = lax.dynamic_slice_in_dim(gs[k], chip * conv_ch, conv_ch, axis=2)
    as2d = lambda a: a[None] if a.ndim == 1 else a
    outs = adamw_small(*[[as2d(d[k]) for k in small] for d in (w, gs, m, v)])
    for k, (delta, m2, v2) in zip(small, outs):
        res[k] = [t.reshape(w[k].shape) for t in (gs[k], delta, m2, v2)]

    parts = [exchange_wait(SCATTER, scatters[l], total, f"scatter_wait_{l}") for l in range(depth)]
    p1 = sum_partials([owns[l][0] for l in range(depth)], [parts[l][0] for l in range(depth)])
    p2 = sum_partials([owns[l][1] for l in range(depth)], [parts[l][1] for l in range(depth)])
    q1, q2 = sibling_exchange(p1, p2)
    res["w_in"] = [jnp.swapaxes(t, 1, 2) for t in adamw(w_in_t, p1, q1, m_w_in_t, v_w_in_t)]
    res["w_out"] = adamw(w_out, p2, q2, m_w_out, v_w_out)

    return (loss, dx[None], *[res[k][0] for k in names], *[res[k][1] for k in names],
            *[res[k][2] for k in names], *[res[k][3] for k in names])
```

```python
import functools
import math

import jax
import jax.numpy as jnp
import numpy as np
from jax import lax
from jax.experimental import pallas as pl
from jax.experimental.pallas import tpu as pltpu

F32 = jnp.float32
BF16 = jnp.bfloat16

D_MODEL = 1024
GROUP_W = 256
N_HEADS = 4
HEAD_DIM = 64
N_CHUNKS = 13
D_IN = N_CHUNKS * GROUP_W
D_MIX = 4 * GROUP_W
NORM_EPS = 1e-6
RG_C = 8.0
GMLP_CHUNK = 128
ATTN_BLOCK = 128
PATTERN_DILS = (1, 4, 16)
N_PATTERNS = len(PATTERN_DILS)
ALIBI_SLOPES = tuple(2.0 ** (-8.0 * (h + 1) / N_HEADS) for h in range(N_HEADS))
ATTN_SCALE = 1.0 / math.sqrt(HEAD_DIM)
NEG_BIG = -1e30

ADAM_LR = 0.001
ADAM_B1 = 0.9
ADAM_B2 = 0.999
ADAM_EPS = 1e-08
ADAM_WD = 0.01
ADAM_STEP = 10

C_AX, C_AB, C_AC, C_AG, C_RX, C_RG, C_CU, C_CV, C_CG, C_DQ, C_DK, C_DV, C_DG = range(13)

SUBLANES = 8
LANES = 128
VMEM_LIMIT = 56 * 1024 * 1024
TILE_IN = 512
TILE_OUT = 1024
TILE_MIX = 512
TILE_DW = 1024
TILE_DW_OUT = 2048
ATTN_QB = 4
GELU_K0 = math.sqrt(2.0 / math.pi)
GELU_K1 = 0.044715


def _params(sem):
    return pltpu.CompilerParams(dimension_semantics=sem, vmem_limit_bytes=VMEM_LIMIT)


def _silu(x):
    return x * jax.nn.sigmoid(x)


def _dsilu(x):
    s = jax.nn.sigmoid(x)
    return s * (1.0 + x * (1.0 - s))


def _gelu(x):
    return 0.5 * x * (1.0 + jnp.tanh(GELU_K0 * (x + GELU_K1 * x * x * x)))


def _gelu_and_grad(x):
    t = jnp.tanh(GELU_K0 * (x + GELU_K1 * x * x * x))
    g = 0.5 * x * (1.0 + t)
    dg = 0.5 * (1.0 + t) + 0.5 * x * (1.0 - t * t) * GELU_K0 * (1.0 + 3.0 * GELU_K1 * x * x)
    return g, dg


def _neg_expm1(x):
    series = x * (1.0 + x * (0.5 + x * (1.0 / 6.0 + x * (1.0 / 24.0 + x * (1.0 / 120.0)))))
    return -jnp.where(x > -0.05, series, jnp.exp(x) - 1.0)


def _shift_down(v, halo, k):
    r = pltpu.roll(v, k, 0)
    rh = pltpu.roll(halo, k, 0)
    row = lax.broadcasted_iota(jnp.int32, halo.shape, 0)
    top = jnp.where(row < k, rh, r[:SUBLANES])
    return jnp.concatenate([top, r[SUBLANES:]], axis=0)


def _shift_up(v, halo, k):
    t = v.shape[0]
    r = pltpu.roll(v, t - k, 0)
    rh = pltpu.roll(halo, SUBLANES - k, 0)
    row = lax.broadcasted_iota(jnp.int32, halo.shape, 0)
    bot = jnp.where(row >= SUBLANES - k, rh, r[t - SUBLANES:])
    return jnp.concatenate([r[:t - SUBLANES], bot], axis=0)


def _scan_causal(a, b, h_in):
    t = a.shape[0]
    row8 = lax.broadcasted_iota(jnp.int32, a.shape, 0) % SUBLANES
    d = 1
    while d < SUBLANES:
        m = row8 >= d
        a_s = jnp.where(m, pltpu.roll(a, d, 0), 1.0)
        b_s = jnp.where(m, pltpu.roll(b, d, 0), 0.0)
        b = a * b_s + b
        a = a * a_s
        d *= 2
    out, carry = [], h_in
    for g in range(t // SUBLANES):
        sl = slice(g * SUBLANES, (g + 1) * SUBLANES)
        hg = b[sl] + a[sl] * carry
        out.append(hg)
        carry = hg[SUBLANES - 1:SUBLANES]
    return jnp.concatenate(out, axis=0)


def _scan_anticausal(a, b, g_in):
    t = a.shape[0]
    row8 = lax.broadcasted_iota(jnp.int32, a.shape, 0) % SUBLANES
    d = 1
    while d < SUBLANES:
        m = row8 < SUBLANES - d
        a_s = jnp.where(m, pltpu.roll(a, t - d, 0), 1.0)
        b_s = jnp.where(m, pltpu.roll(b, t - d, 0), 0.0)
        b = a * b_s + b
        a = a * a_s
        d *= 2
    out, carry = [], g_in
    for g in reversed(range(t // SUBLANES)):
        sl = slice(g * SUBLANES, (g + 1) * SUBLANES)
        gg = b[sl] + a[sl] * carry
        out.append(gg)
        carry = gg[0:1]
    return jnp.concatenate(out[::-1], axis=0)


def _head_of_lane(shape):
    return lax.broadcasted_iota(jnp.int32, shape, len(shape) - 1) // HEAD_DIM


def _per_head_lanes(cols):
    t = cols[0].shape[0]
    lane = lax.broadcasted_iota(jnp.int32, (t, LANES), 1)
    out = jnp.zeros((t, LANES), F32)
    for h, col in enumerate(cols):
        out = jnp.where(lane == h, col, out)
    return out


def _put_row(acc_shape, k, row_vec):
    row = lax.broadcasted_iota(jnp.int32, acc_shape, 0)
    return jnp.where(row == k, jnp.broadcast_to(row_vec, acc_shape), 0.0)


def _dot(a, b):
    return jnp.dot(a, b, preferred_element_type=F32)


def _dot_nt(a, b):
    return lax.dot_general(a, b, (((1,), (1,)), ((), ())), preferred_element_type=F32)


def _dot_tn(a, b):
    return lax.dot_general(a, b, (((0,), (0,)), ((), ())), preferred_element_type=F32)


def _deinterleave_store(val, stage, outs):
    t, c = val.shape
    for hh in range(c // LANES):
        stage[hh][...] = val[:, hh * LANES:(hh + 1) * LANES].astype(F32)
    for dil, ref in outs:
        for r in range(dil):
            for hh in range(c // LANES):
                ref[r, :, hh * LANES:(hh + 1) * LANES] = stage[hh][pl.ds(r, t // dil, stride=dil), :].astype(ref.dtype)


def _interleave_load(ref, dil, stage):
    _, n, c = ref.shape
    for r in range(dil):
        for hh in range(c // LANES):
            stage[hh][pl.ds(r, n, stride=dil), :] = ref[r, :, hh * LANES:(hh + 1) * LANES].astype(F32)
    return jnp.concatenate([stage[hh][...] for hh in range(c // LANES)], axis=1)


def _stage_scratch(tile, cols, copies):
    return [pltpu.VMEM((tile, LANES), F32)] * (copies * (cols // LANES))


def _by_residue(s, dil, cols, dtype):
    return jax.ShapeDtypeStruct((dil, s // dil, cols), dtype)


def _residue_block(dil, tile, cols):
    return pl.BlockSpec((dil, tile // dil, cols), lambda i: (0, i, 0))


def in_fwd(x, g, w):
    s = x.shape[0]
    qkv_w = 3 * GROUP_W

    def body(x_ref, g_ref, w_ref, z_ref, h_ref, qkv1_ref, qkv4_ref, qkv16_ref, *stage):
        xv = x_ref[...]
        rs = lax.rsqrt(jnp.mean(xv * xv, axis=-1, keepdims=True) + NORM_EPS)
        h = (xv * rs * g_ref[...]).astype(BF16)
        h_ref[...] = h
        z = _dot_nt(h, w_ref[...])
        z_ref[...] = z
        qkv = z[:, C_DQ * GROUP_W:(C_DV + 1) * GROUP_W]
        qkv1_ref[...] = qkv.astype(BF16)
        _deinterleave_store(qkv, stage, ((PATTERN_DILS[1], qkv4_ref), (PATTERN_DILS[2], qkv16_ref)))

    return pl.pallas_call(
        body, name="in_fwd", grid=(s // TILE_IN,),
        in_specs=[pl.BlockSpec((TILE_IN, D_MODEL), lambda i: (i, 0)),
                  pl.BlockSpec((1, D_MODEL), lambda i: (0, 0)),
                  pl.BlockSpec((D_IN, D_MODEL), lambda i: (0, 0))],
        out_specs=[pl.BlockSpec((TILE_IN, D_IN), lambda i: (i, 0)),
                   pl.BlockSpec((TILE_IN, D_MODEL), lambda i: (i, 0)),
                   pl.BlockSpec((TILE_IN, qkv_w), lambda i: (i, 0)),
                   _residue_block(PATTERN_DILS[1], TILE_IN, qkv_w),
                   _residue_block(PATTERN_DILS[2], TILE_IN, qkv_w)],
        out_shape=[jax.ShapeDtypeStruct((s, D_IN), F32), jax.ShapeDtypeStruct((s, D_MODEL), BF16),
                   jax.ShapeDtypeStruct((s, qkv_w), BF16),
                   _by_residue(s, PATTERN_DILS[1], qkv_w, BF16), _by_residue(s, PATTERN_DILS[2], qkv_w, BF16)],
        scratch_shapes=_stage_scratch(TILE_IN, qkv_w, 1),
        compiler_params=_params(("parallel",)),
    )(x, g, w)


def out_fwd(y, w, x):
    s = x.shape[0]

    def body(y_ref, w_ref, x_ref, o_ref):
        o_ref[...] = x_ref[...] + _dot(y_ref[...], w_ref[...])

    return pl.pallas_call(
        body, name="out_fwd", grid=(s // TILE_OUT,),
        in_specs=[pl.BlockSpec((TILE_OUT, D_MIX), lambda i: (i, 0)),
                  pl.BlockSpec((D_MIX, D_MODEL), lambda i: (0, 0)),
                  pl.BlockSpec((TILE_OUT, D_MODEL), lambda i: (i, 0))],
        out_specs=pl.BlockSpec((TILE_OUT, D_MODEL), lambda i: (i, 0)),
        out_shape=jax.ShapeDtypeStruct((s, D_MODEL), F32),
        compiler_params=_params(("parallel",)),
    )(y, w, x)


def out_bwd(dx, w, z, o):
    s = dx.shape[0]
    abc = 3 * GROUP_W

    def body(dx_ref, w_ref, dg_ref, o_ref, dy_ref, ddg_ref, do1_ref, do4_ref, do16_ref, dl1_ref, dl4_ref, dl16_ref,
             *stage):
        stage_a, stage_b = stage[:2], stage[2:]
        dy = _dot_nt(dx_ref[...].astype(BF16), w_ref[...])
        dy_ref[...] = dy[:, :abc]
        dyd = dy[:, abc:]
        head = _head_of_lane((TILE_OUT, GROUP_W))
        dg = dg_ref[...]
        o = o_ref[...]
        do = dyd * _silu(dg)
        ddg_ref[...] = dyd * o * _dsilu(dg)
        prod = do * o
        dl = _per_head_lanes([jnp.sum(jnp.where(head == h, prod, 0.0), axis=-1, keepdims=True)
                              for h in range(N_HEADS)])
        do1_ref[...] = do.astype(BF16)
        dl1_ref[...] = dl
        _deinterleave_store(do, stage_a, ((PATTERN_DILS[1], do4_ref), (PATTERN_DILS[2], do16_ref)))
        _deinterleave_store(dl, stage_b, ((PATTERN_DILS[1], dl4_ref), (PATTERN_DILS[2], dl16_ref)))

    row = pl.BlockSpec((TILE_OUT, GROUP_W), lambda i: (i, 0))
    r4 = _residue_block(PATTERN_DILS[1], TILE_OUT, GROUP_W)
    r16 = _residue_block(PATTERN_DILS[2], TILE_OUT, GROUP_W)
    crow = pl.BlockSpec((TILE_OUT, LANES), lambda i: (i, 0))
    c4 = _residue_block(PATTERN_DILS[1], TILE_OUT, LANES)
    c16 = _residue_block(PATTERN_DILS[2], TILE_OUT, LANES)
    return pl.pallas_call(
        body, name="out_bwd", grid=(s // TILE_OUT,),
        in_specs=[pl.BlockSpec((TILE_OUT, D_MODEL), lambda i: (i, 0)),
                  pl.BlockSpec((D_MIX, D_MODEL), lambda i: (0, 0)),
                  pl.BlockSpec((TILE_OUT, GROUP_W), lambda i: (i, C_DG)), row],
        out_specs=[pl.BlockSpec((TILE_OUT, abc), lambda i: (i, 0)), row, row, r4, r16, crow, c4, c16],
        out_shape=[jax.ShapeDtypeStruct((s, abc), F32), jax.ShapeDtypeStruct((s, GROUP_W), F32),
                   jax.ShapeDtypeStruct((s, GROUP_W), BF16),
                   _by_residue(s, PATTERN_DILS[1], GROUP_W, BF16), _by_residue(s, PATTERN_DILS[2], GROUP_W, BF16),
                   jax.ShapeDtypeStruct((s, LANES), F32),
                   _by_residue(s, PATTERN_DILS[1], LANES, F32), _by_residue(s, PATTERN_DILS[2], LANES, F32)],
        scratch_shapes=_stage_scratch(TILE_OUT, GROUP_W, 1) + _stage_scratch(TILE_OUT, LANES, 1),
        compiler_params=_params(("parallel",)),
    )(dx, w, z, o)


def in_bwd(dz, w, x, g, dx_next):
    s = x.shape[0]

    def body(dz_ref, w_ref, x_ref, g_ref, dxn_ref, dx_ref, dg_ref):
        @pl.when(pl.program_id(0) == 0)
        def _():
            dg_ref[...] = jnp.zeros_like(dg_ref)

        dh = _dot(dz_ref[...], w_ref[...])
        xv = x_ref[...]
        rs = lax.rsqrt(jnp.mean(xv * xv, axis=-1, keepdims=True) + NORM_EPS)
        xh = xv * rs
        dg_ref[...] += _put_row(dg_ref.shape, 0, jnp.sum(dh * xh, axis=0, keepdims=True))
        dn = dh * g_ref[...]
        dx_ref[...] = dxn_ref[...] + rs * (dn - xh * jnp.mean(dn * xh, axis=-1, keepdims=True))

    return pl.pallas_call(
        body, name="in_bwd", grid=(s // TILE_IN,),
        in_specs=[pl.BlockSpec((TILE_IN, D_IN), lambda i: (i, 0)),
                  pl.BlockSpec((D_IN, D_MODEL), lambda i: (0, 0)),
                  pl.BlockSpec((TILE_IN, D_MODEL), lambda i: (i, 0)),
                  pl.BlockSpec((1, D_MODEL), lambda i: (0, 0)),
                  pl.BlockSpec((TILE_IN, D_MODEL), lambda i: (i, 0))],
        out_specs=[pl.BlockSpec((TILE_IN, D_MODEL), lambda i: (i, 0)),
                   pl.BlockSpec((SUBLANES, D_MODEL), lambda i: (0, 0))],
        out_shape=[jax.ShapeDtypeStruct((s, D_MODEL), F32), jax.ShapeDtypeStruct((SUBLANES, D_MODEL), F32)],
        compiler_params=_params(("arbitrary",)),
    )(dz, w, x, g, dx_next)


def grad_w_out(y, dx):
    s = y.shape[0]

    def body(y_ref, dx_ref, o_ref):
        @pl.when(pl.program_id(0) == 0)
        def _():
            o_ref[...] = jnp.zeros_like(o_ref)

        o_ref[...] += _dot_tn(y_ref[...], dx_ref[...].astype(BF16))

    return pl.pallas_call(
        body, name="grad_w_out", grid=(s // TILE_DW_OUT,),
        in_specs=[pl.BlockSpec((TILE_DW_OUT, D_MIX), lambda k: (k, 0)),
                  pl.BlockSpec((TILE_DW_OUT, D_MODEL), lambda k: (k, 0))],
        out_specs=pl.BlockSpec((D_MIX, D_MODEL), lambda k: (0, 0)),
        out_shape=jax.ShapeDtypeStruct((D_MIX, D_MODEL), F32),
        compiler_params=_params(("arbitrary",)),
    )(y, dx)


def grad_w_in(h, dz, chip):
    s = h.shape[0]
    rows = D_IN // N_CHIPS

    def body(chip_ref, h_ref, dz_ref, staged_ref, own_ref, acc):
        k = pl.program_id(0)

        @pl.when(k == 0)
        def _():
            acc[...] = jnp.zeros_like(acc)

        acc[...] += _dot_tn(dz_ref[...], h_ref[...])

        @pl.when(k == s // TILE_DW - 1)
        def _():
            for j in range(N_CHIPS):
                part = acc[j * rows:(j + 1) * rows, :]
                staged_ref[j] = part.astype(BF16)

                @pl.when(chip_ref[0] == j)
                def _():
                    own_ref[...] = part

    return pl.pallas_call(
        body, name="grad_w_in",
        grid_spec=pltpu.PrefetchScalarGridSpec(
            num_scalar_prefetch=1, grid=(s // TILE_DW,),
            in_specs=[pl.BlockSpec((TILE_DW, D_MODEL), lambda k, c: (k, 0)),
                      pl.BlockSpec((TILE_DW, D_IN), lambda k, c: (k, 0))],
            out_specs=[pl.BlockSpec((N_CHIPS, rows, D_MODEL), lambda k, c: (0, 0, 0)),
                       pl.BlockSpec((rows, D_MODEL), lambda k, c: (0, 0))],
            scratch_shapes=[pltpu.VMEM((D_IN, D_MODEL), F32)]),
        out_shape=[jax.ShapeDtypeStruct((N_CHIPS, rows, D_MODEL), BF16), jax.ShapeDtypeStruct((rows, D_MODEL), F32)],
        compiler_params=_params(("arbitrary",)),
    )(chip, h, dz)


def out_fwd_loss(y, w, x, g, tgt):
    s = x.shape[0]

    def body(y_ref, w_ref, x_ref, g_ref, t_ref, l_ref, dx_ref, dg_ref):
        @pl.when(pl.program_id(0) == 0)
        def _():
            l_ref[...] = jnp.zeros_like(l_ref)
            dg_ref[...] = jnp.zeros_like(dg_ref)

        xv = x_ref[...] + _dot(y_ref[...], w_ref[...])
        gv = g_ref[...]
        rs = lax.rsqrt(jnp.mean(xv * xv, axis=-1, keepdims=True) + NORM_EPS)
        xh = xv * rs
        e = xh * gv - t_ref[...]
        part = 0.5 * jnp.sum(jnp.mean(e * e, axis=-1, keepdims=True), axis=0, keepdims=True)
        l_ref[...] += jnp.broadcast_to(part, l_ref.shape)
        dy = e * (1.0 / D_MODEL)
        dg_ref[...] += _put_row(dg_ref.shape, 0, jnp.sum(dy * xh, axis=0, keepdims=True))
        dn = dy * gv
        dx_ref[...] = rs * (dn - xh * jnp.mean(dn * xh, axis=-1, keepdims=True))

    return pl.pallas_call(
        body, name="out_fwd_loss", grid=(s // TILE_OUT,),
        in_specs=[pl.BlockSpec((TILE_OUT, D_MIX), lambda i: (i, 0)),
                  pl.BlockSpec((D_MIX, D_MODEL), lambda i: (0, 0)),
                  pl.BlockSpec((TILE_OUT, D_MODEL), lambda i: (i, 0)),
                  pl.BlockSpec((1, D_MODEL), lambda i: (0, 0)),
                  pl.BlockSpec((TILE_OUT, D_MODEL), lambda i: (i, 0))],
        out_specs=[pl.BlockSpec((SUBLANES, 128), lambda i: (0, 0)),
                   pl.BlockSpec((TILE_OUT, D_MODEL), lambda i: (i, 0)),
                   pl.BlockSpec((SUBLANES, D_MODEL), lambda i: (0, 0))],
        out_shape=[jax.ShapeDtypeStruct((SUBLANES, 128), F32), jax.ShapeDtypeStruct((s, D_MODEL), F32),
                   jax.ShapeDtypeStruct((SUBLANES, D_MODEL), F32)],
        compiler_params=_params(("arbitrary",)),
    )(y, w, x, g, tgt)


def _attn_bias(dil):
    qi = np.arange(ATTN_BLOCK)[:, None]
    ki = np.arange(2 * ATTN_BLOCK)[None, :]
    delta = qi + ATTN_BLOCK - ki
    band = (delta >= 0) & (delta <= ATTN_BLOCK)
    out = np.empty((2, N_HEADS, ATTN_BLOCK, 2 * ATTN_BLOCK), np.float32)
    for f in range(2):
        ok = band & ((ki >= ATTN_BLOCK) | (f == 0))
        for h in range(N_HEADS):
            out[f, h] = np.where(ok, -ALIBI_SLOPES[h] * dil * delta, NEG_BIG)
    return jnp.asarray(out.reshape(2, N_HEADS * ATTN_BLOCK, 2 * ATTN_BLOCK))


def _stack_heads(a, head):
    return jnp.concatenate([jnp.where(head == h, a, jnp.zeros_like(a)) for h in range(N_HEADS)], axis=0)


def _unstack_heads(a, head):
    out = a[:ATTN_BLOCK]
    for h in range(1, N_HEADS):
        out = jnp.where(head == h, a[h * ATTN_BLOCK:(h + 1) * ATTN_BLOCK], out)
    return out


def _head_column(a):
    return jnp.concatenate([a[:, h:h + 1] for h in range(N_HEADS)], axis=0)


def _attn_specs(n_blocks):
    rows = ATTN_QB * ATTN_BLOCK
    cur = lambda c, w=GROUP_W: pl.BlockSpec((rows, w), lambda n, c=c: (n, c))
    prev = lambda c: pl.BlockSpec((ATTN_BLOCK, GROUP_W), lambda n, c=c: (jnp.maximum(n * ATTN_QB - 1, 0), c))
    nxt = lambda c, w=GROUP_W: pl.BlockSpec((ATTN_BLOCK, w),
                                            lambda n, c=c: (jnp.minimum(n * ATTN_QB + ATTN_QB, n_blocks - 1), c))
    return cur, prev, nxt


def _keys(kp_ref, k_ref, j):
    prev = kp_ref[...] if j == 0 else k_ref[(j - 1) * ATTN_BLOCK:j * ATTN_BLOCK, :]
    return jnp.concatenate([prev, k_ref[j * ATTN_BLOCK:(j + 1) * ATTN_BLOCK, :]], axis=0)


def attn_fwd(qkv, dil):
    s = qkv.shape[0]
    n_blocks = s // ATTN_BLOCK
    bps = n_blocks // dil
    rows = ATTN_QB * ATTN_BLOCK

    def body(q_ref, kp_ref, k_ref, vp_ref, v_ref, bias_ref, o_ref, lse_ref):
        n = pl.program_id(0)
        head = _head_of_lane((ATTN_BLOCK, GROUP_W))
        for j in range(ATTN_QB):
            sl = slice(j * ATTN_BLOCK, (j + 1) * ATTN_BLOCK)
            first = (((n * ATTN_QB + j) % bps) == 0).astype(jnp.int32)
            qs = _stack_heads(q_ref[sl, :], head)
            sc = _dot_nt(qs, _keys(kp_ref, k_ref, j)) * ATTN_SCALE + bias_ref[first]
            m = jnp.max(sc, axis=-1, keepdims=True)
            pr = jnp.exp(sc - m)
            l = jnp.sum(pr, axis=-1, keepdims=True)
            oh = _dot(pr.astype(BF16), _keys(vp_ref, v_ref, j)) / l
            o_ref[sl, :] = _unstack_heads(oh, head).astype(BF16)
            lse_ref[sl, :] = _unstack_heads(jnp.broadcast_to(m + jnp.log(l), oh.shape), head)

    cur, prev, _ = _attn_specs(n_blocks)
    bias = _attn_bias(dil)
    return pl.pallas_call(
        body, name=f"attn_fwd_d{dil}", grid=(n_blocks // ATTN_QB,),
        in_specs=[cur(0), prev(1), cur(1), prev(2), cur(2), pl.BlockSpec(bias.shape, lambda n: (0, 0, 0))],
        out_specs=[cur(0), cur(0)],
        out_shape=[jax.ShapeDtypeStruct((s, GROUP_W), BF16), jax.ShapeDtypeStruct((s, GROUP_W), F32)],
        compiler_params=_params(("parallel",)),
    )(qkv, qkv, qkv, qkv, qkv, bias)


def attn_bwd(qkv, do, lse, dlt, dil):
    s = qkv.shape[0]
    n_blocks = s // ATTN_BLOCK
    bps = n_blocks // dil
    rows = ATTN_QB * ATTN_BLOCK

    def body(q_ref, qn_ref, kp_ref, k_ref, vp_ref, v_ref, do_ref, don_ref, lse_ref, lsen_ref, dl_ref, dln_ref,
             bias_ref, out_ref, dk_acc, dv_acc):
        n = pl.program_id(0)
        head = _head_of_lane((ATTN_BLOCK, GROUP_W))
        dk_acc[...] = jnp.zeros_like(dk_acc)
        dv_acc[...] = jnp.zeros_like(dv_acc)

        def pair(qj, doj, lsej, dlj, kk, vv, bias, keep):
            qs = _stack_heads(qj, head)
            dos = _stack_heads(doj, head)
            sc = _dot_nt(qs, kk) * ATTN_SCALE + bias
            if keep is None:
                pr = jnp.exp(sc - _head_column(lsej))
            else:
                pr = jnp.exp(jnp.minimum(sc - _head_column(lsej), 0.0)) * keep
            dp = _dot_nt(dos, vv)
            ds = (pr * (dp - _head_column(dlj)) * ATTN_SCALE).astype(BF16)
            return ds, _dot_tn(ds, qs), _dot_tn(pr.astype(BF16), dos)

        for j in range(ATTN_QB):
            sl = slice(j * ATTN_BLOCK, (j + 1) * ATTN_BLOCK)
            first = (((n * ATTN_QB + j) % bps) == 0).astype(jnp.int32)
            kk = _keys(kp_ref, k_ref, j)
            ds, dks, dvs = pair(q_ref[sl, :], do_ref[sl, :], lse_ref[sl, :], dl_ref[sl, :],
                                kk, _keys(vp_ref, v_ref, j), bias_ref[first], None)
            out_ref[sl, 0:GROUP_W] = _unstack_heads(_dot(ds, kk), head)
            acc = slice(j * ATTN_BLOCK, (j + 2) * ATTN_BLOCK)
            dk_acc[acc, :] += dks
            dv_acc[acc, :] += dvs

        nxt = n * ATTN_QB + ATTN_QB
        valid = ((nxt < n_blocks) & ((nxt % bps) != 0)).astype(F32)
        last = slice((ATTN_QB - 1) * ATTN_BLOCK, ATTN_QB * ATTN_BLOCK)
        _, dks, dvs = pair(qn_ref[...], don_ref[...], lsen_ref[...], dln_ref[...], k_ref[last, :], v_ref[last, :],
                           bias_ref[0][:, :ATTN_BLOCK], valid)
        acc = slice(ATTN_QB * ATTN_BLOCK, (ATTN_QB + 1) * ATTN_BLOCK)
        dk_acc[acc, :] += dks
        dv_acc[acc, :] += dvs
        out_ref[:, GROUP_W:2 * GROUP_W] = dk_acc[ATTN_BLOCK:, :]
        out_ref[:, 2 * GROUP_W:3 * GROUP_W] = dv_acc[ATTN_BLOCK:, :]

    cur, prev, nxt = _attn_specs(n_blocks)
    bias = _attn_bias(dil)
    return pl.pallas_call(
        body, name=f"attn_bwd_d{dil}", grid=(n_blocks // ATTN_QB,),
        in_specs=[cur(0), nxt(0), prev(1), cur(1), prev(2), cur(2), cur(0), nxt(0),
                  cur(0, LANES), nxt(0, LANES), cur(0, LANES), nxt(0, LANES),
                  pl.BlockSpec(bias.shape, lambda n: (0, 0, 0))],
        out_specs=pl.BlockSpec((rows, 3 * GROUP_W), lambda n: (n, 0)),
        out_shape=jax.ShapeDtypeStruct((s, 3 * GROUP_W), F32),
        scratch_shapes=[pltpu.VMEM(((ATTN_QB + 1) * ATTN_BLOCK, GROUP_W), F32),
                        pltpu.VMEM(((ATTN_QB + 1) * ATTN_BLOCK, GROUP_W), F32)],
        compiler_params=_params(("parallel",)),
    )(qkv, qkv, qkv, qkv, qkv, qkv, do, do, lse, lse, dlt, dlt, bias)


def _zcol(c):
    return pl.BlockSpec((TILE_MIX, GROUP_W), lambda i, c=c: (i, c))


def _zhalo(c):
    per = TILE_MIX // SUBLANES
    return pl.BlockSpec((SUBLANES, GROUP_W), lambda i, c=c: (jnp.maximum(i * per - 1, 0), c))


def _full(shape):
    return pl.BlockSpec(shape, lambda i: tuple(0 for _ in shape))


def _of_layer(a, l):
    rest = a.shape[1:]
    return pl.BlockSpec((None,) + rest, lambda i: (l,) + tuple(0 for _ in rest))


def _softplus_neg(lam):
    nl = -lam
    return jnp.maximum(nl, 0.0) + jnp.log1p(jnp.exp(-jnp.abs(nl)))


def _lru_gates(xb, wa_ref, wx_ref, ba, bx, lam):
    xbb = xb.astype(BF16)
    r = jax.nn.sigmoid(_dot(xbb, wa_ref[...]) + ba)
    ig = jax.nn.sigmoid(_dot(xbb, wx_ref[...]) + bx)
    log_a = (-RG_C * r) * _softplus_neg(lam)
    a = jnp.exp(log_a)
    mult = jnp.sqrt(_neg_expm1(2.0 * log_a))
    return r, ig, a, mult


LRU_SAVED = 5


def _gmlp_spatial(ws_ref, vvb, head):
    outs = []
    for j in range(vvb.shape[0] // GMLP_CHUNK):
        blk = vvb[j * GMLP_CHUNK:(j + 1) * GMLP_CHUNK, :]
        acc = jnp.zeros((GMLP_CHUNK, GROUP_W), F32)
        for h in range(N_HEADS):
            acc = jnp.where(head[:GMLP_CHUNK] == h, _dot(ws_ref[h], blk), acc)
        outs.append(acc)
    return jnp.concatenate(outs, axis=0)


def mix_fwd(z, attn, wts, l):
    s = z.shape[0]
    d4, d16 = PATTERN_DILS[1], PATTERN_DILS[2]

    def body(ax_ref, ab_ref, ac_ref, ag_ref, rx_ref, rg_ref, cu_ref, cv_ref, cg_ref, dg_ref,
             axh_ref, ach_ref, rxh_ref, o1_ref, l1_ref, o4_ref, l4_ref, o16_ref, l16_ref,
             caw_ref, crw_ref, crb_ref, wa_ref, wx_ref, ba_ref, bx_ref, lam_ref, gng_ref, ws_ref, bs_ref,
             y_ref, hl_ref, o_ref, lse_ref, lse4_ref, lse16_ref, lru_ref, carry, *stage):
        st_a, st_b, st_c, st_d, st_e = (stage[2 * k:2 * k + 2] for k in range(5))
        i = pl.program_id(0)

        @pl.when(i == 0)
        def _():
            carry[...] = jnp.zeros_like(carry)

        nz = (i > 0).astype(F32)
        head = _head_of_lane((TILE_MIX, GROUP_W))

        pa = ac_ref[...] * ax_ref[...]
        pah = ach_ref[...] * axh_ref[...] * nz
        cv = caw_ref[2:3, :] * pa + caw_ref[1:2, :] * _shift_down(pa, pah, 1) + caw_ref[0:1, :] * _shift_down(pa, pah, 2)
        y_ref[:, 0:GROUP_W] = (ab_ref[...] * cv * _silu(ag_ref[...])).astype(BF16)

        rx = rx_ref[...]
        rxh = rxh_ref[...] * nz
        xb = (crw_ref[3:4, :] * rx + crw_ref[2:3, :] * _shift_down(rx, rxh, 1) + crw_ref[1:2, :] * _shift_down(rx, rxh, 2)
              + crw_ref[0:1, :] * _shift_down(rx, rxh, 3) + crb_ref[...])
        r, ig, a, mult = _lru_gates(xb, wa_ref, wx_ref, ba_ref[...], bx_ref[...], lam_ref[...])
        for k, val in enumerate((xb, r, ig, a, mult)):
            lru_ref[:, k * GROUP_W:(k + 1) * GROUP_W] = val
        hl = _scan_causal(a, mult * (ig * xb), carry[SUBLANES - 1:SUBLANES, :])
        hl_ref[...] = hl
        carry[...] = hl[TILE_MIX - SUBLANES:, :]
        y_ref[:, GROUP_W:2 * GROUP_W] = (hl * _silu(rg_ref[...])).astype(BF16)

        u = _gelu(cu_ref[...])
        gv = _gelu(cv_ref[...])
        rs = lax.rsqrt(jnp.mean(gv * gv, axis=-1, keepdims=True) + NORM_EPS)
        vvb = (gv * rs * gng_ref[...]).astype(BF16)
        sp = _gmlp_spatial(ws_ref, vvb, head) + jnp.concatenate([bs_ref[...]] * (TILE_MIX // GMLP_CHUNK), axis=0)
        y_ref[:, 2 * GROUP_W:3 * GROUP_W] = (u * sp * _silu(cg_ref[...])).astype(BF16)

        ops = (o1_ref[...].astype(F32), _interleave_load(o4_ref, d4, st_a), _interleave_load(o16_ref, d16, st_b))
        lps = (l1_ref[...], _interleave_load(l4_ref, d4, st_c), _interleave_load(l16_ref, d16, st_d))
        m = jnp.maximum(jnp.maximum(lps[0], lps[1]), lps[2])
        zsum = jnp.zeros_like(m)
        o = jnp.zeros_like(m)
        for op, lp in zip(ops, lps):
            w = jnp.exp(lp - m)
            zsum = zsum + w
            o = o + w * op
        o = o / zsum
        lse = m + jnp.log(zsum)
        lse = _per_head_lanes([lse[:, h * HEAD_DIM:h * HEAD_DIM + 1] for h in range(N_HEADS)])
        o_ref[...] = o
        lse_ref[...] = lse
        _deinterleave_store(lse, st_e, ((d4, lse4_ref), (d16, lse16_ref)))
        y_ref[:, 3 * GROUP_W:4 * GROUP_W] = (o * _silu(dg_ref[...])).astype(BF16)

    row = pl.BlockSpec((TILE_MIX, GROUP_W), lambda i: (i, 0))
    r4 = _residue_block(d4, TILE_MIX, GROUP_W)
    r16 = _residue_block(d16, TILE_MIX, GROUP_W)
    crow = pl.BlockSpec((TILE_MIX, LANES), lambda i: (i, 0))
    c4 = _residue_block(d4, TILE_MIX, LANES)
    c16 = _residue_block(d16, TILE_MIX, LANES)
    names = ("caw", "crw", "crb", "wa", "wx", "ba", "bx", "lam", "gng", "ws", "bs")
    in_specs = ([_zcol(c) for c in (C_AX, C_AB, C_AC, C_AG, C_RX, C_RG, C_CU, C_CV, C_CG, C_DG)]
                + [_zhalo(C_AX), _zhalo(C_AC), _zhalo(C_RX), row, row, r4, r4, r16, r16]
                + [_of_layer(wts[k], l) for k in names])
    return pl.pallas_call(
        body, name="mix_fwd", grid=(s // TILE_MIX,),
        in_specs=in_specs,
        out_specs=[pl.BlockSpec((TILE_MIX, D_MIX), lambda i: (i, 0)), row, row, crow, c4, c16,
                   pl.BlockSpec((TILE_MIX, LRU_SAVED * GROUP_W), lambda i: (i, 0))],
        out_shape=([jax.ShapeDtypeStruct((s, D_MIX), BF16)] + [jax.ShapeDtypeStruct((s, GROUP_W), F32)] * 2
                   + [jax.ShapeDtypeStruct((s, LANES), F32), _by_residue(s, d4, LANES, F32),
                      _by_residue(s, d16, LANES, F32), jax.ShapeDtypeStruct((s, LRU_SAVED * GROUP_W), F32)]),
        scratch_shapes=([pltpu.VMEM((SUBLANES, GROUP_W), F32)] + _stage_scratch(TILE_MIX, GROUP_W, 4)
                        + _stage_scratch(TILE_MIX, LANES, 1)),
        compiler_params=_params(("arbitrary",)),
    )(*([z] * 13), *[a for pair in attn for a in pair], *[wts[k] for k in names])


def mix_bwd(dy, z, hl, lru, dqkv, ddg, wts, l):
    s = z.shape[0]
    d4, d16 = PATTERN_DILS[1], PATTERN_DILS[2]
    n_tiles = s // TILE_MIX

    def body(dya_ref, dyb_ref, dyc_ref, ax_ref, ab_ref, ac_ref, ag_ref, rx_ref, rg_ref, cu_ref, cv_ref, cg_ref,
             axh_ref, ach_ref, rxh_ref, hl_ref, hlh_ref, lru_ref, dqkv1_ref, dqkv4_ref, dqkv16_ref, ddg_ref,
             caw_ref, crw_ref, crb_ref, wa_ref, wx_ref, ba_ref, bx_ref, lam_ref, gng_ref, ws_ref, wst_ref, bs_ref,
             dz_ref, ga_ref, gr_ref, gn_ref, gwa_ref, gwx_ref, gws_ref, gbs_ref,
             c_dcv, c_g, c_a, c_dxb, *stage):
        st_a, st_b = stage[:len(stage) // 2], stage[len(stage) // 2:]
        step = pl.program_id(0)
        i = n_tiles - 1 - step

        @pl.when(step == 0)
        def _():
            for r in (c_dcv, c_g, c_a, c_dxb, ga_ref, gr_ref, gn_ref, gwa_ref, gwx_ref, gws_ref, gbs_ref):
                r[...] = jnp.zeros_like(r)

        nz = (i > 0).astype(F32)
        head = _head_of_lane((TILE_MIX, GROUP_W))
        shp8 = (SUBLANES, GROUP_W)
        colsum = lambda v: jnp.sum(v, axis=0, keepdims=True)

        ax, ab, ac, ag = ax_ref[...], ab_ref[...], ac_ref[...], ag_ref[...]
        dya = dya_ref[...]
        pa = ac * ax
        pah = ach_ref[...] * axh_ref[...] * nz
        pa1 = _shift_down(pa, pah, 1)
        pa2 = _shift_down(pa, pah, 2)
        cv = caw_ref[2:3, :] * pa + caw_ref[1:2, :] * pa1 + caw_ref[0:1, :] * pa2
        sg = _silu(ag)
        dz_ref[:, C_AB * GROUP_W:(C_AB + 1) * GROUP_W] = (dya * cv * sg).astype(BF16)
        dz_ref[:, C_AG * GROUP_W:(C_AG + 1) * GROUP_W] = (dya * ab * cv * _dsilu(ag)).astype(BF16)
        dcv = dya * ab * sg
        nxt = c_dcv[...]
        dpa = caw_ref[2:3, :] * dcv + caw_ref[1:2, :] * _shift_up(dcv, nxt, 1) + caw_ref[0:1, :] * _shift_up(dcv, nxt, 2)
        c_dcv[...] = dcv[:SUBLANES, :]
        dz_ref[:, C_AC * GROUP_W:(C_AC + 1) * GROUP_W] = (dpa * ax).astype(BF16)
        dz_ref[:, C_AX * GROUP_W:(C_AX + 1) * GROUP_W] = (dpa * ac).astype(BF16)
        ga_ref[...] += (_put_row(shp8, 2, colsum(dcv * pa)) + _put_row(shp8, 1, colsum(dcv * pa1))
                        + _put_row(shp8, 0, colsum(dcv * pa2)))

        rx, rg = rx_ref[...], rg_ref[...]
        dyb = dyb_ref[...]
        rxh = rxh_ref[...] * nz
        rx1, rx2, rx3 = _shift_down(rx, rxh, 1), _shift_down(rx, rxh, 2), _shift_down(rx, rxh, 3)
        xb, r, ig, a, mult = (lru_ref[:, k * GROUP_W:(k + 1) * GROUP_W] for k in range(LRU_SAVED))
        lam = lam_ref[...]
        sp = _softplus_neg(lam)
        hl = hl_ref[...]
        hprev = _shift_down(hl, hlh_ref[...] * nz, 1)
        dz_ref[:, C_RG * GROUP_W:(C_RG + 1) * GROUP_W] = (dyb * hl * _dsilu(rg)).astype(BF16)
        dh = dyb * _silu(rg)
        a_next = _shift_up(a, c_a[...], 1)
        g = _scan_anticausal(a_next, dh, c_g[0:1, :])
        c_g[...] = g[:SUBLANES, :]
        c_a[...] = a[:SUBLANES, :]
        u = ig * xb
        da = g * hprev
        dmult = g * u
        du = g * mult
        dlog_a = da * a - dmult * (a * a) / mult
        dr = dlog_a * (-RG_C * sp)
        dga = dr * r * (1.0 - r)
        dgx = (du * xb) * ig * (1.0 - ig)
        dgab, dgxb = dga.astype(BF16), dgx.astype(BF16)
        dxb = du * ig + _dot_nt(dgab, wa_ref[...]) + _dot_nt(dgxb, wx_ref[...])
        xbb = xb.astype(BF16)
        gwa_ref[...] += _dot_tn(xbb, dgab)
        gwx_ref[...] += _dot_tn(xbb, dgxb)
        nxt = c_dxb[...]
        drx = (crw_ref[3:4, :] * dxb + crw_ref[2:3, :] * _shift_up(dxb, nxt, 1) + crw_ref[1:2, :] * _shift_up(dxb, nxt, 2)
               + crw_ref[0:1, :] * _shift_up(dxb, nxt, 3))
        c_dxb[...] = dxb[:SUBLANES, :]
        dz_ref[:, C_RX * GROUP_W:(C_RX + 1) * GROUP_W] = drx.astype(BF16)
        dlam = colsum(dlog_a * (-RG_C * r)) * (-jax.nn.sigmoid(-lam))
        gr_ref[...] += (_put_row(shp8, 3, colsum(dxb * rx)) + _put_row(shp8, 2, colsum(dxb * rx1))
                        + _put_row(shp8, 1, colsum(dxb * rx2)) + _put_row(shp8, 0, colsum(dxb * rx3))
                        + _put_row(shp8, 4, colsum(dxb)) + _put_row(shp8, 5, colsum(dga))
                        + _put_row(shp8, 6, colsum(dgx)) + _put_row(shp8, 7, dlam))

        cu, cvv, cg = cu_ref[...], cv_ref[...], cg_ref[...]
        dyc = dyc_ref[...]
        u_c, du_c = _gelu_and_grad(cu)
        gv, dgv_c = _gelu_and_grad(cvv)
        rs = lax.rsqrt(jnp.mean(gv * gv, axis=-1, keepdims=True) + NORM_EPS)
        vh = gv * rs
        gng = gng_ref[...]
        vvb = (vh * gng).astype(BF16)
        spat = _gmlp_spatial(ws_ref, vvb, head) + jnp.concatenate([bs_ref[...]] * (TILE_MIX // GMLP_CHUNK), axis=0)
        sgc = _silu(cg)
        dz_ref[:, C_CU * GROUP_W:(C_CU + 1) * GROUP_W] = (dyc * spat * sgc * du_c).astype(BF16)
        dz_ref[:, C_CG * GROUP_W:(C_CG + 1) * GROUP_W] = (dyc * u_c * spat * _dsilu(cg)).astype(BF16)
        dsp = dyc * u_c * sgc
        dspb = dsp.astype(BF16)
        tril = (lax.broadcasted_iota(jnp.int32, (GMLP_CHUNK, GMLP_CHUNK), 0)
                >= lax.broadcasted_iota(jnp.int32, (GMLP_CHUNK, GMLP_CHUNK), 1))
        head_c = head[:GMLP_CHUNK]
        dvv_parts = []
        gbs = jnp.zeros((GMLP_CHUNK, GROUP_W), F32)
        for j in range(TILE_MIX // GMLP_CHUNK):
            sl = slice(j * GMLP_CHUNK, (j + 1) * GMLP_CHUNK)
            dblk = dspb[sl, :]
            vblk = vvb[sl, :]
            gbs = gbs + dsp[sl, :]
            acc = jnp.zeros((GMLP_CHUNK, GROUP_W), F32)
            for h in range(N_HEADS):
                acc = jnp.where(head_c == h, _dot(wst_ref[h], dblk), acc)
                dm = jnp.where(head_c == h, dblk, jnp.zeros_like(dblk))
                gws_ref[h] += jnp.where(tril, _dot_nt(dm, vblk), 0.0)
            dvv_parts.append(acc)
        gbs_ref[...] += gbs
        dvv = jnp.concatenate(dvv_parts, axis=0)
        gn_ref[...] += _put_row(shp8, 0, colsum(dvv * vh))
        dvh = dvv * gng
        dgv = rs * (dvh - vh * jnp.mean(dvh * vh, axis=-1, keepdims=True))
        dz_ref[:, C_CV * GROUP_W:(C_CV + 1) * GROUP_W] = (dgv * dgv_c).astype(BF16)

        dsum = dqkv1_ref[...] + _interleave_load(dqkv4_ref, d4, st_a) + _interleave_load(dqkv16_ref, d16, st_b)
        dz_ref[:, C_DQ * GROUP_W:(C_DV + 1) * GROUP_W] = dsum.astype(BF16)
        dz_ref[:, C_DG * GROUP_W:(C_DG + 1) * GROUP_W] = ddg_ref[...].astype(BF16)

    per = TILE_MIX // SUBLANES
    qkv_w = 3 * GROUP_W
    rev = lambda c: pl.BlockSpec((TILE_MIX, GROUP_W), lambda t, c=c: (n_tiles - 1 - t, c))
    revh = lambda c: pl.BlockSpec((SUBLANES, GROUP_W),
                                  lambda t, c=c: (jnp.maximum((n_tiles - 1 - t) * per - 1, 0), c))
    revr = lambda dil: pl.BlockSpec((dil, TILE_MIX // dil, qkv_w), lambda t: (0, n_tiles - 1 - t, 0))
    names = ("caw", "crw", "crb", "wa", "wx", "ba", "bx", "lam", "gng", "ws", "wst", "bs")
    in_specs = ([rev(0), rev(1), rev(2)]
                + [rev(c) for c in (C_AX, C_AB, C_AC, C_AG, C_RX, C_RG, C_CU, C_CV, C_CG)]
                + [revh(C_AX), revh(C_AC), revh(C_RX), rev(0), revh(0),
                   pl.BlockSpec((TILE_MIX, LRU_SAVED * GROUP_W), lambda t: (n_tiles - 1 - t, 0)),
                   pl.BlockSpec((TILE_MIX, qkv_w), lambda t: (n_tiles - 1 - t, 0)), revr(d4), revr(d16), rev(0)]
                + [_of_layer(wts[k], l) for k in names])
    small = jax.ShapeDtypeStruct((SUBLANES, GROUP_W), F32)
    sq = jax.ShapeDtypeStruct((GROUP_W, GROUP_W), F32)
    out_shape = [jax.ShapeDtypeStruct((s, D_IN), BF16), small, small, small, sq, sq,
                 jax.ShapeDtypeStruct((N_HEADS, GMLP_CHUNK, GMLP_CHUNK), F32),
                 jax.ShapeDtypeStruct((GMLP_CHUNK, GROUP_W), F32)]
    out_specs = ([pl.BlockSpec((TILE_MIX, D_IN), lambda t: (n_tiles - 1 - t, 0))]
                 + [_full(o.shape) for o in out_shape[1:]])
    return pl.pallas_call(
        body, name="mix_bwd", grid=(n_tiles,),
        in_specs=in_specs, out_specs=out_specs, out_shape=out_shape,
        scratch_shapes=[pltpu.VMEM((SUBLANES, GROUP_W), F32)] * 4 + _stage_scratch(TILE_MIX, qkv_w, 2),
        compiler_params=_params(("arbitrary",)),
    )(dy, dy, dy, *([z] * 12), hl, hl, lru, *dqkv, ddg, *[wts[k] for k in names])


def prepare_small_weights(p):
    tril = jnp.tril(jnp.ones((GMLP_CHUNK, GMLP_CHUNK), dtype=bool))
    ws = jnp.where(tril, p["gmlp_ws"], 0.0).astype(BF16)
    row = lambda a: a[:, None, :]
    eye = jnp.eye(N_HEADS, dtype=F32)
    bd = lambda w: (w[:, :, :, None, :] * eye[None, :, None, :, None]).reshape(-1, GROUP_W, GROUP_W).astype(BF16)
    return dict(
        caw=p["conv_a_w"], crw=p["conv_r_w"], crb=row(p["conv_r_b"]),
        wa=bd(p["lru_wa"]), wx=bd(p["lru_wx"]),
        ba=row(p["lru_ba"]), bx=row(p["lru_bx"]), lam=row(p["lru_lambda"]), gng=row(p["gmlp_norm_g"]),
        ws=ws, wst=jnp.swapaxes(ws, 2, 3),
        bs=jnp.repeat(jnp.swapaxes(p["gmlp_bs"], 1, 2), HEAD_DIM, axis=2))


def _flat(a):
    return a.reshape(a.shape[0] * a.shape[1], a.shape[2])


def _split(a, dil):
    return a.reshape(dil, a.shape[0] // dil, a.shape[1])


def local_step(x, tgt, final_g, depth, chip, layer_weights, projections_done):
    saved = []
    for l in range(depth):
        gain, w_in_l, rest = layer_weights(l, x)
        z, h, *qkvs = in_fwd(x, gain, w_in_l)
        w_out_l, wts = rest(z)
        qkvs = [_flat(q) if q.ndim == 3 else q for q in qkvs]
        attn = []
        for q, d in zip(qkvs, PATTERN_DILS):
            o_p, lse_p = attn_fwd(q, d)
            attn.append((o_p, lse_p) if d == 1 else (_split(o_p, d), _split(lse_p, d)))
        y, hl, o, lse, lse4, lse16, lru = mix_fwd(z, attn, wts, l)
        saved.append(dict(x=x, z=z, h=h, y=y, hl=hl, o=o, qkvs=qkvs, lses=(lse, _flat(lse4), _flat(lse16)), wts=wts, lru=lru,
                          gain=gain, w_in=w_in_l, w_out=w_out_l))
        if l < depth - 1:
            x = out_fwd(y, w_out_l, x)
        else:
            loss, dx, dfg = out_fwd_loss(y, w_out_l, x, final_g[None, :], tgt)
    raw = {k: [None] * depth for k in ("gain", "a", "r", "n", "wa", "wx", "ws", "bs")}
    zero = None
    for l in reversed(range(depth)):
        sv = saved[l]
        dy, ddg, do1, do4, do16, dl1, dl4, dl16 = out_bwd(dx, sv["w_out"], sv["z"], sv["o"])
        g_w_out = grad_w_out(sv["y"], dx)
        dqkv = []
        for q, do, lse, dl, d in zip(sv["qkvs"], (do1, _flat(do4), _flat(do16)), sv["lses"],
                                     (dl1, _flat(dl4), _flat(dl16)), PATTERN_DILS):
            g = attn_bwd(q, do, lse, dl, d)
            dqkv.append(g if d == 1 else _split(g, d))
        dz, ga, gr, gn, gwa, gwx, gws, gbs = mix_bwd(dy, sv["z"], sv["hl"], sv["lru"], dqkv, ddg, sv["wts"], l)
        gain = sv["gain"] if zero is None else sv["gain"] + zero
        zero = projections_done(l, *grad_w_in(sv["h"], dz, chip), g_w_out)
        if l == 0 and zero is not None:
            gain = gain + zero
        dx, dgn = in_bwd(dz, sv["w_in"], sv["x"], gain, dx)
        for k, g in zip(("gain", "a", "r", "n", "wa", "wx", "ws", "bs"), (dgn, ga, gr, gn, gwa, gwx, gws, gbs)):
            raw[k][l] = g
    st = {k: jnp.stack(v) for k, v in raw.items()}
    eye = jnp.eye(N_HEADS, dtype=F32)[None, :, None, :, None]
    diag = lambda g: (g.reshape(depth, N_HEADS, HEAD_DIM, N_HEADS, HEAD_DIM) * eye).sum(axis=3)
    grads = dict(
        norm_g=st["gain"][:, 0], conv_a_w=st["a"][:, :3], conv_r_w=st["r"][:, :4], conv_r_b=st["r"][:, 4],
        lru_ba=st["r"][:, 5], lru_bx=st["r"][:, 6], lru_lambda=st["r"][:, 7], gmlp_norm_g=st["n"][:, 0],
        lru_wa=diag(st["wa"]), lru_wx=diag(st["wx"]), gmlp_ws=st["ws"],
        gmlp_bs=jnp.swapaxes(st["bs"].reshape(depth, GMLP_CHUNK, N_HEADS, HEAD_DIM).sum(-1), 1, 2),
        final_g=dfg[0])
    return loss, dx, grads


MESH = pl.DeviceIdType.MESH
N_CHIPS = 4
N_DEV = 8
ANY = pl.BlockSpec(memory_space=pl.ANY)


def _place():
    x, y, c = lax.axis_index("x"), lax.axis_index("y"), lax.axis_index("c")
    chips = [(1 - x, y), (x, 1 - y), (1 - x, 1 - y)]
    return x, y, c, chips


def _remote(src, dst, ssem, rsem, to):
    return pltpu.make_async_remote_copy(src_ref=src, dst_ref=dst, send_sem=ssem, recv_sem=rsem,
                                        device_id=to, device_id_type=MESH)


HBM = pl.BlockSpec(memory_space=pltpu.HBM)
SEM = pl.BlockSpec(memory_space=pltpu.SEMAPHORE)
DATAFLOW = pltpu.SideEffectType.DATAFLOW_SIDE_EFFECTING
GATHER, SCATTER = "gather", "scatter"


def _chip_copies(mode, src_refs, land_refs, ssem, rsem):
    x, y, c, chips = _place()
    me = 2 * x + y
    n = len(src_refs)
    copies = []
    for k, (cx, cy) in enumerate(chips):
        for a in range(n):
            if mode == GATHER:
                src, dst = src_refs[a], land_refs[a].at[me]
            else:
                src, dst = src_refs[a].at[2 * cx + cy], land_refs[a].at[k]
            copies.append(_remote(src, dst, ssem.at[n * k + a], rsem.at[n * k + a], (cx, cy, c)))
    return copies


def exchange_start(mode, srcs, after, name):
    n = len(srcs)
    if mode == GATHER:
        lands = [lax.empty((N_CHIPS,) + s.shape, s.dtype) for s in srcs]
    else:
        lands = [lax.empty((N_CHIPS - 1,) + s.shape[1:], s.dtype) for s in srcs]
    extra = [] if after is None else [after]

    def body(*refs):
        src_refs, land_refs = refs[:n], refs[n:2 * n]
        ssem, rsem = refs[2 * n + len(extra)], refs[2 * n + len(extra) + 1]
        token = refs[-1]
        for cp in _chip_copies(mode, src_refs, land_refs, ssem, rsem):
            cp.start()
        token[...] = jnp.zeros_like(token)

    arrays = list(srcs) + lands
    return pl.pallas_call(
        body, name=name,
        out_shape=(pltpu.SemaphoreType.DMA((3 * n,)), pltpu.SemaphoreType.DMA((3 * n,)),
                   *[pltpu.HBM(a.shape, a.dtype) for a in arrays], jax.ShapeDtypeStruct((SUBLANES, LANES), F32)),
        in_specs=[HBM] * (2 * n) + [ANY] * len(extra),
        out_specs=(SEM, SEM, *[HBM] * (2 * n), pl.BlockSpec(memory_space=pltpu.VMEM)),
        input_output_aliases={i: 2 + i for i in range(2 * n)},
        compiler_params=pltpu.CompilerParams(has_side_effects=DATAFLOW),
    )(*[pltpu.with_memory_space_constraint(a, pltpu.HBM) for a in arrays], *extra)


def exchange_wait(mode, started, after, name):
    ssem, rsem, *thru, _ = started
    n = len(thru) // 2

    def body(*refs):
        src_refs, land_refs = refs[:n], refs[n:2 * n]
        ssem_ref, rsem_ref = refs[2 * n], refs[2 * n + 1]
        for cp in _chip_copies(mode, src_refs, land_refs, ssem_ref, rsem_ref):
            cp.wait_send()
            cp.wait_recv()

    outs = pl.pallas_call(
        body, name=name,
        out_shape=[pltpu.HBM(a.shape, a.dtype) for a in thru],
        in_specs=[HBM] * (2 * n) + [SEM, SEM, ANY],
        out_specs=[HBM] * (2 * n),
        input_output_aliases={i: i for i in range(2 * n)},
        compiler_params=pltpu.CompilerParams(has_side_effects=DATAFLOW),
    )(*thru, ssem, rsem, after)
    return outs[n:]


def sibling_exchange(p1, p2):
    def body(p1_ref, p2_ref, q1_ref, q2_ref, ssem, rsem):
        x, y, c, _ = _place()
        copies = [_remote(p_ref, q_ref, ssem.at[a], rsem.at[a], (x, y, 1 - c))
                  for a, (p_ref, q_ref) in enumerate(((p1_ref, q1_ref), (p2_ref, q2_ref)))]
        for cp in copies:
            cp.start()
        for cp in copies:
            cp.wait()

    return pl.pallas_call(
        body, name="sibling_exchange",
        in_specs=[ANY, ANY], out_specs=[ANY, ANY],
        out_shape=[jax.ShapeDtypeStruct(p.shape, p.dtype) for p in (p1, p2)],
        scratch_shapes=[pltpu.SemaphoreType.DMA((2,)), pltpu.SemaphoreType.DMA((2,))],
    )(p1, p2)


def all_reduce_small(v):
    r, n = v.shape
    piece = r // N_DEV

    def body(x_ref, out_ref, recv, ssem1, rsem1, ssem2, rsem2):
        x, y, c, _ = _place()
        me = 4 * x + 2 * y + c

        def peer(k):
            px = 1 - x if (k >> 2) & 1 else x
            py = 1 - y if (k >> 1) & 1 else y
            pc = 1 - c if k & 1 else c
            return (px, py, pc), 4 * px + 2 * py + pc

        def rows(ref, d):
            return ref.at[pl.ds(d * piece, piece), :]

        scatter = []
        for k in range(1, N_DEV):
            to, idx = peer(k)
            scatter.append(_remote(rows(x_ref, idx), recv.at[k], ssem1.at[k - 1], rsem1.at[k - 1], to))
            scatter[-1].start()
        acc = rows(x_ref, me)[...]
        for k in range(1, N_DEV):
            scatter[k - 1].wait_recv()
            acc = acc + recv[k]
        rows(out_ref, me)[...] = acc

        gather = []
        for k in range(1, N_DEV):
            to, _ = peer(k)
            gather.append(_remote(rows(out_ref, me), rows(out_ref, me), ssem2.at[k - 1], rsem2.at[k - 1], to))
            gather[-1].start()
        for k in range(1, N_DEV):
            to, idx = peer(k)
            _remote(rows(out_ref, idx), rows(out_ref, idx), ssem2.at[k - 1], rsem2.at[k - 1], to).wait_recv()
        for cp in scatter + gather:
            cp.wait_send()

    return pl.pallas_call(
        body, name="all_reduce_small",
        out_shape=jax.ShapeDtypeStruct((r, n), v.dtype),
        in_specs=[pl.BlockSpec(memory_space=pltpu.VMEM)],
        out_specs=pl.BlockSpec(memory_space=pltpu.VMEM),
        scratch_shapes=[pltpu.VMEM((N_DEV, piece, n), v.dtype)] + [pltpu.SemaphoreType.DMA((N_DEV - 1,))] * 4,
        compiler_params=pltpu.CompilerParams(vmem_limit_bytes=VMEM_LIMIT),
    )(v)


TILE_ROWS = 256


def _row_tile(r):
    return max(t for t in range(SUBLANES, TILE_ROWS + 1, SUBLANES) if r % t == 0)


def sum_partials(owns, parts):
    k, r, c = parts[0].shape
    depth = len(owns)
    tile = _row_tile(r)

    def body(buf_ref, o_ref, p_ref, out_ref):
        acc = o_ref[...]
        for i in range(k):
            acc = acc + p_ref[i].astype(F32)
        out_ref[0] = acc

    out = lax.empty((depth, r, c), F32)
    for l in range(depth):
        out = pl.pallas_call(
            functools.partial(body), name="sum_partials", grid=(r // tile,),
            in_specs=[ANY, pl.BlockSpec((tile, c), lambda i: (i, 0)), pl.BlockSpec((k, tile, c), lambda i: (0, i, 0))],
            out_specs=pl.BlockSpec((1, tile, c), lambda i, l=l: (l, i, 0)),
            out_shape=jax.ShapeDtypeStruct((depth, r, c), F32),
            input_output_aliases={0: 0},
            compiler_params=_params(("parallel",)),
        )(out, owns[l], parts[l])
    return out


def _adamw_update(w, g, m, v):
    m2 = ADAM_B1 * m + (1.0 - ADAM_B1) * g
    v2 = ADAM_B2 * v + (1.0 - ADAM_B2) * (g * g)
    m_hat = m2 / (1.0 - ADAM_B1 ** ADAM_STEP)
    v_hat = v2 / (1.0 - ADAM_B2 ** ADAM_STEP)
    return -ADAM_LR * (m_hat / (jnp.sqrt(v_hat) + ADAM_EPS) + ADAM_WD * w), m2, v2


def adamw_small(ws, gs, ms, vs):
    n = len(ws)

    def body(*refs):
        ins, outs = refs[:4 * n], refs[4 * n:]
        for i in range(n):
            d, m2, v2 = _adamw_update(ins[i][...], ins[n + i][...], ins[2 * n + i][...], ins[3 * n + i][...])
            outs[3 * i][...] = d
            outs[3 * i + 1][...] = m2
            outs[3 * i + 2][...] = v2

    outs = pl.pallas_call(
        body, name="adamw_small",
        out_shape=[jax.ShapeDtypeStruct(w.shape, F32) for w in ws for _ in range(3)],
    )(*ws, *gs, *ms, *vs)
    return [tuple(outs[3 * i:3 * i + 3]) for i in range(n)]


def adamw(w, ga, gb, m, v):
    n, r, c = w.shape
    tile = _row_tile(r)

    def body(w_ref, ga_ref, gb_ref, m_ref, v_ref, g_ref, d_ref, m2_ref, v2_ref):
        g = ga_ref[...] + gb_ref[...]
        g_ref[...] = g
        d_ref[...], m2_ref[...], v2_ref[...] = _adamw_update(w_ref[...], g, m_ref[...], v_ref[...])

    spec = pl.BlockSpec((1, tile, c), lambda j, i: (j, i, 0))
    return pl.pallas_call(
        body, name="adamw", grid=(n, r // tile),
        in_specs=[spec] * 5, out_specs=[spec] * 4,
        out_shape=[jax.ShapeDtypeStruct((n, r, c), F32)] * 4,
        compiler_params=_params(("parallel", "parallel")),
    )(w, ga, gb, m, v)


REPLICATED = ("norm_g", "conv_r_b", "lru_wa", "lru_ba", "lru_wx", "lru_bx", "lru_lambda", "gmlp_norm_g",
              "gmlp_ws", "gmlp_bs", "final_g")
CHIP_SHARDED_SMALL = ("conv_a_w", "conv_r_w")
PACK_LANES = 128


def _pack(arrays):
    flat = jnp.concatenate([a.reshape(-1) for a in arrays])
    pad = (-flat.shape[0]) % (TILE_ROWS * PACK_LANES)
    return jnp.pad(flat, (0, pad)).reshape(-1, PACK_LANES)


def _unpack(packed, shapes):
    flat = packed.reshape(-1)
    out, off = [], 0
    for shp in shapes:
        n = math.prod(shp)
        out.append(flat[off:off + n].reshape(shp))
        off += n
    return out


def kernel(x, norm_g, w_in, conv_a_w, conv_r_w, conv_r_b, lru_wa, lru_ba, lru_wx, lru_bx, lru_lambda, gmlp_norm_g, gmlp_ws, gmlp_bs, w_out, final_g, loss_target, m_norm_g, m_w_in, m_conv_a_w, m_conv_r_w, m_conv_r_b, m_lru_wa, m_lru_ba, m_lru_wx, m_lru_bx, m_lru_lambda, m_gmlp_norm_g, m_gmlp_ws, m_gmlp_bs, m_w_out, m_final_g, v_norm_g, v_w_in, v_conv_a_w, v_conv_r_w, v_conv_r_b, v_lru_wa, v_lru_ba, v_lru_wx, v_lru_bx, v_lru_lambda, v_gmlp_norm_g, v_gmlp_ws, v_gmlp_bs, v_w_out, v_final_g):
    names = ("norm_g", "w_in", "conv_a_w", "conv_r_w", "conv_r_b", "lru_wa", "lru_ba", "lru_wx", "lru_bx",
             "lru_lambda", "gmlp_norm_g", "gmlp_ws", "gmlp_bs", "w_out", "final_g")
    w = dict(zip(names, (norm_g, w_in, conv_a_w, conv_r_w, conv_r_b, lru_wa, lru_ba, lru_wx, lru_bx, lru_lambda,
                         gmlp_norm_g, gmlp_ws, gmlp_bs, w_out, final_g)))
    m = dict(zip(names, (m_norm_g, m_w_in, m_conv_a_w, m_conv_r_w, m_conv_r_b, m_lru_wa, m_lru_ba, m_lru_wx, m_lru_bx,
                         m_lru_lambda, m_gmlp_norm_g, m_gmlp_ws, m_gmlp_bs, m_w_out, m_final_g)))
    v = dict(zip(names, (v_norm_g, v_w_in, v_conv_a_w, v_conv_r_w, v_conv_r_b, v_lru_wa, v_lru_ba, v_lru_wx, v_lru_bx,
                         v_lru_lambda, v_gmlp_norm_g, v_gmlp_ws, v_gmlp_bs, v_w_out, v_final_g)))
    depth, _, in_cols = w_in.shape
    out_rows = w_out.shape[1]
    conv_ch = conv_a_w.shape[2]
    chip = 2 * lax.axis_index("x") + lax.axis_index("y")

    taps = conv_a_w.shape[1] + conv_r_w.shape[1]
    w_in_t, m_w_in_t, v_w_in_t = (jnp.swapaxes(a, 1, 2) for a in (w_in, m_w_in, v_w_in))
    w_in_h, w_out_h = w_in_t.astype(BF16), w_out.astype(BF16)
    conv_own = jnp.concatenate([conv_a_w, conv_r_w], axis=1).reshape(depth * taps, conv_ch)
    gathers, token = [], None
    for l in range(depth):
        groups = [[w_in_h[l]], [w_out_h[l], conv_own]] if l == 0 else [[w_in_h[l], w_out_h[l]]]
        gathers.append([])
        for i, srcs in enumerate(groups):
            gathers[l].append(exchange_start(GATHER, srcs, token, f"gather_start_{l}_{i}"))
            token = gathers[l][-1][-1]
    p = dict(w)

    def with_own(land, own):
        return lax.dynamic_update_slice(land, own[None], (chip,) + (0,) * own.ndim)

    def layer_weights(l, x_l):
        lands = list(exchange_wait(GATHER, gathers[l][0], x_l, f"gather_wait_{l}_0"))
        w_in_l = with_own(lands[0], w_in_h[l]).reshape(D_IN, D_MODEL)
        gain = norm_g[l][None, :]
        if l == 0:
            gain = gain + token[0, 0]

        def rest(z_l):
            if l == 0:
                lands.extend(exchange_wait(GATHER, gathers[l][1], z_l, f"gather_wait_{l}_1"))
                conv = with_own(lands[2], conv_own).reshape(N_CHIPS, depth, taps, conv_ch)
                conv = conv.transpose(1, 2, 0, 3).reshape(depth, taps, GROUP_W)
                p["conv_a_w"] = conv[:, :conv_a_w.shape[1]]
                p["conv_r_w"] = conv[:, conv_a_w.shape[1]:]
                p["prepared"] = prepare_small_weights(p)
            return with_own(lands[1], w_out_h[l]).reshape(D_MIX, D_MODEL), p["prepared"]

        return gain, w_in_l, rest

    scatters, owns = [None] * depth, [None] * depth

    def projections_done(l, g_w_in_by_chip, g_w_in_own, g_w_out):
        go = g_w_out.reshape(N_CHIPS, out_rows, D_MODEL)
        owns[l] = (g_w_in_own, lax.dynamic_index_in_dim(go, chip, axis=0, keepdims=False))
        scatters[l] = exchange_start(SCATTER, [g_w_in_by_chip, go.astype(BF16)], None, f"scatter_start_{l}")
        return scatters[l][-1][0, 0]

    loss8, dx, grads = local_step(x[0], loss_target[0], final_g, depth, chip.reshape(1), layer_weights,
                                  projections_done)

    res = {}
    small = REPLICATED + CHIP_SHARDED_SMALL
    packed = _pack([grads[k] for k in small] + [loss8[0, :1]])
    total = all_reduce_small(packed)
    *sums, loss = _unpack(total, [grads[k].shape for k in small] + [()])
    gs = dict(zip(small, sums))
    for k in CHIP_SHARDED_SMALL:
        gs[k] = lax.dynamic_slice_in_dim(gs[k], chip * conv_ch, conv_ch, axis=2)
    as2d = lambda a: a[None] if a.ndim == 1 else a
    outs = adamw_small(*[[as2d(d[k]) for k in small] for d in (w, gs, m, v)])
    for k, (delta, m2, v2) in zip(small, outs):
        res[k] = [t.reshape(w[k].shape) for t in (gs[k], delta, m2, v2)]

    parts = [exchange_wait(SCATTER, scatters[l], total, f"scatter_wait_{l}") for l in range(depth)]
    p1 = sum_partials([owns[l][0] for l in range(depth)], [parts[l][0] for l in range(depth)])
    p2 = sum_partials([owns[l][1] for l in range(depth)], [parts[l][1] for l in range(depth)])
    q1, q2 = sibling_exchange(p1, p2)
    res["w_in"] = [jnp.swapaxes(t, 1, 2) for t in adamw(w_in_t, p1, q1, m_w_in_t, v_w_in_t)]
    res["w_out"] = adamw(w_out, p2, q2, m_w_out, v_w_out)

    return (loss, dx[None], *[res[k][0] for k in names], *[res[k][1] for k in names],
            *[res[k][2] for k in names], *[res[k][3] for k in names])
```

```python
import functools
import math

import jax
import jax.numpy as jnp
import numpy as np
from jax import lax
from jax.experimental import pallas as pl
from jax.experimental.pallas import tpu as pltpu

F32 = jnp.float32
BF16 = jnp.bfloat16

D_MODEL = 1024
GROUP_W = 256
N_HEADS = 4
HEAD_DIM = 64
N_CHUNKS = 13
D_IN = N_CHUNKS * GROUP_W
D_MIX = 4 * GROUP_W
NORM_EPS = 1e-6
RG_C = 8.0
GMLP_CHUNK = 128
ATTN_BLOCK = 128
PATTERN_DILS = (1, 4, 16)
ALIBI_SLOPES = tuple(2.0 ** (-8.0 * (h + 1) / N_HEADS) for h in range(N_HEADS))
ATTN_SCALE = 1.0 / math.sqrt(HEAD_DIM)
NEG_BIG = -1e30

ADAM_LR = 0.001
ADAM_B1 = 0.9
ADAM_B2 = 0.999
ADAM_EPS = 1e-08
ADAM_WD = 0.01
ADAM_STEP = 10

C_AX, C_AB, C_AC, C_AG, C_RX, C_RG, C_CU, C_CV, C_CG, C_DQ, C_DK, C_DV, C_DG = range(13)

SUBLANES = 8
LANES = 128
VMEM_LIMIT = 56 * 1024 * 1024
TILE_IN = 512
TILE_OUT = 1024
TILE_MIX = 512
TILE_DW = 1024
TILE_DW_OUT = 2048
ATTN_QB = 8
GELU_K0 = math.sqrt(2.0 / math.pi)
GELU_K1 = 0.044715


def _params(sem):
    return pltpu.CompilerParams(dimension_semantics=sem, vmem_limit_bytes=VMEM_LIMIT)


def _silu(x):
    return x * jax.nn.sigmoid(x)


def _dsilu(x):
    s = jax.nn.sigmoid(x)
    return s * (1.0 + x * (1.0 - s))


def _gelu(x):
    return 0.5 * x * (1.0 + jnp.tanh(GELU_K0 * (x + GELU_K1 * x * x * x)))


def _gelu_and_grad(x):
    t = jnp.tanh(GELU_K0 * (x + GELU_K1 * x * x * x))
    g = 0.5 * x * (1.0 + t)
    dg = 0.5 * (1.0 + t) + 0.5 * x * (1.0 - t * t) * GELU_K0 * (1.0 + 3.0 * GELU_K1 * x * x)
    return g, dg


def _neg_expm1(x):
    series = x * (1.0 + x * (0.5 + x * (1.0 / 6.0 + x * (1.0 / 24.0 + x * (1.0 / 120.0)))))
    return -jnp.where(x > -0.05, series, jnp.exp(x) - 1.0)


def _shift_down(v, halo, k):
    r = pltpu.roll(v, k, 0)
    rh = pltpu.roll(halo, k, 0)
    row = lax.broadcasted_iota(jnp.int32, halo.shape, 0)
    top = jnp.where(row < k, rh, r[:SUBLANES])
    return jnp.concatenate([top, r[SUBLANES:]], axis=0)


def _shift_up(v, halo, k):
    t = v.shape[0]
    r = pltpu.roll(v, t - k, 0)
    rh = pltpu.roll(halo, SUBLANES - k, 0)
    row = lax.broadcasted_iota(jnp.int32, halo.shape, 0)
    bot = jnp.where(row >= SUBLANES - k, rh, r[t - SUBLANES:])
    return jnp.concatenate([r[:t - SUBLANES], bot], axis=0)


def _scan_causal(a, b, h_in):
    t = a.shape[0]
    row8 = lax.broadcasted_iota(jnp.int32, a.shape, 0) % SUBLANES
    d = 1
    while d < SUBLANES:
        m = row8 >= d
        a_s = jnp.where(m, pltpu.roll(a, d, 0), 1.0)
        b_s = jnp.where(m, pltpu.roll(b, d, 0), 0.0)
        b = a * b_s + b
        a = a * a_s
        d *= 2
    out, carry = [], h_in
    for g in range(t // SUBLANES):
        sl = slice(g * SUBLANES, (g + 1) * SUBLANES)
        hg = b[sl] + a[sl] * carry
        out.append(hg)
        carry = hg[SUBLANES - 1:SUBLANES]
    return jnp.concatenate(out, axis=0)


def _scan_anticausal(a, b, g_in):
    t = a.shape[0]
    row8 = lax.broadcasted_iota(jnp.int32, a.shape, 0) % SUBLANES
    d = 1
    while d < SUBLANES:
        m = row8 < SUBLANES - d
        a_s = jnp.where(m, pltpu.roll(a, t - d, 0), 1.0)
        b_s = jnp.where(m, pltpu.roll(b, t - d, 0), 0.0)
        b = a * b_s + b
        a = a * a_s
        d *= 2
    out, carry = [], g_in
    for g in reversed(range(t // SUBLANES)):
        sl = slice(g * SUBLANES, (g + 1) * SUBLANES)
        gg = b[sl] + a[sl] * carry
        out.append(gg)
        carry = gg[0:1]
    return jnp.concatenate(out[::-1], axis=0)


def _head_of_lane(shape):
    return lax.broadcasted_iota(jnp.int32, shape, len(shape) - 1) // HEAD_DIM


def _per_head_lanes(cols):
    t = cols[0].shape[0]
    lane = lax.broadcasted_iota(jnp.int32, (t, LANES), 1)
    out = jnp.zeros((t, LANES), F32)
    for h, col in enumerate(cols):
        out = jnp.where(lane == h, col, out)
    return out


def _put_row(acc_shape, k, row_vec):
    row = lax.broadcasted_iota(jnp.int32, acc_shape, 0)
    return jnp.where(row == k, jnp.broadcast_to(row_vec, acc_shape), 0.0)


def _dot(a, b):
    return jnp.dot(a, b, preferred_element_type=F32)


def _dot_nt(a, b):
    return lax.dot_general(a, b, (((1,), (1,)), ((), ())), preferred_element_type=F32)


def _dot_tn(a, b):
    return lax.dot_general(a, b, (((0,), (0,)), ((), ())), preferred_element_type=F32)


def _deinterleave_store(val, stage, outs):
    t, c = val.shape
    for hh in range(c // LANES):
        stage[hh][...] = val[:, hh * LANES:(hh + 1) * LANES].astype(F32)
    for dil, ref in outs:
        for r in range(dil):
            for hh in range(c // LANES):
                ref[r, :, hh * LANES:(hh + 1) * LANES] = stage[hh][pl.ds(r, t // dil, stride=dil), :].astype(ref.dtype)


def _interleave_load(ref, dil, stage):
    _, n, c = ref.shape
    for r in range(dil):
        for hh in range(c // LANES):
            stage[hh][pl.ds(r, n, stride=dil), :] = ref[r, :, hh * LANES:(hh + 1) * LANES].astype(F32)
    return jnp.concatenate([stage[hh][...] for hh in range(c // LANES)], axis=1)


def _stage_scratch(tile, cols, copies):
    return [pltpu.VMEM((tile, LANES), F32)] * (copies * (cols // LANES))


def _by_residue(s, dil, cols, dtype):
    return jax.ShapeDtypeStruct((dil, s // dil, cols), dtype)


def _residue_block(dil, tile, cols):
    return pl.BlockSpec((dil, tile // dil, cols), lambda i: (0, i, 0))


def in_fwd(x, g, w):
    s = x.shape[0]
    qkv_w = 3 * GROUP_W

    def body(x_ref, g_ref, w_ref, z_ref, h_ref, qkv1_ref, qkv4_ref, qkv16_ref, *stage):
        xv = x_ref[...]
        rs = lax.rsqrt(jnp.mean(xv * xv, axis=-1, keepdims=True) + NORM_EPS)
        h = (xv * rs * g_ref[...]).astype(BF16)
        h_ref[...] = h
        z = _dot_nt(h, w_ref[...])
        z_ref[...] = z
        qkv = z[:, C_DQ * GROUP_W:(C_DV + 1) * GROUP_W]
        qkv1_ref[...] = qkv.astype(BF16)
        _deinterleave_store(qkv, stage, ((PATTERN_DILS[1], qkv4_ref), (PATTERN_DILS[2], qkv16_ref)))

    return pl.pallas_call(
        body, name="in_fwd", grid=(s // TILE_IN,),
        in_specs=[pl.BlockSpec((TILE_IN, D_MODEL), lambda i: (i, 0)),
                  pl.BlockSpec((1, D_MODEL), lambda i: (0, 0)),
                  pl.BlockSpec((D_IN, D_MODEL), lambda i: (0, 0))],
        out_specs=[pl.BlockSpec((TILE_IN, D_IN), lambda i: (i, 0)),
                   pl.BlockSpec((TILE_IN, D_MODEL), lambda i: (i, 0)),
                   pl.BlockSpec((TILE_IN, qkv_w), lambda i: (i, 0)),
                   _residue_block(PATTERN_DILS[1], TILE_IN, qkv_w),
                   _residue_block(PATTERN_DILS[2], TILE_IN, qkv_w)],
        out_shape=[jax.ShapeDtypeStruct((s, D_IN), F32), jax.ShapeDtypeStruct((s, D_MODEL), BF16),
                   jax.ShapeDtypeStruct((s, qkv_w), BF16),
                   _by_residue(s, PATTERN_DILS[1], qkv_w, BF16), _by_residue(s, PATTERN_DILS[2], qkv_w, BF16)],
        scratch_shapes=_stage_scratch(TILE_IN, qkv_w, 1),
        compiler_params=_params(("parallel",)),
    )(x, g, w)


def out_fwd(y, w, x):
    s = x.shape[0]

    def body(y_ref, w_ref, x_ref, o_ref):
        o_ref[...] = x_ref[...] + _dot(y_ref[...], w_ref[...])

    return pl.pallas_call(
        body, name="out_fwd", grid=(s // TILE_OUT,),
        in_specs=[pl.BlockSpec((TILE_OUT, D_MIX), lambda i: (i, 0)),
                  pl.BlockSpec((D_MIX, D_MODEL), lambda i: (0, 0)),
                  pl.BlockSpec((TILE_OUT, D_MODEL), lambda i: (i, 0))],
        out_specs=pl.BlockSpec((TILE_OUT, D_MODEL), lambda i: (i, 0)),
        out_shape=jax.ShapeDtypeStruct((s, D_MODEL), F32),
        compiler_params=_params(("parallel",)),
    )(y, w, x)


def out_bwd(dx, w, z, o):
    s = dx.shape[0]
    abc = 3 * GROUP_W

    def body(dx_ref, w_ref, dg_ref, o_ref, dy_ref, ddg_ref, do1_ref, do4_ref, do16_ref, dl1_ref, dl4_ref, dl16_ref,
             *stage):
        stage_a, stage_b = stage[:2], stage[2:]
        dy = _dot_nt(dx_ref[...].astype(BF16), w_ref[...])
        dy_ref[...] = dy[:, :abc]
        dyd = dy[:, abc:]
        head = _head_of_lane((TILE_OUT, GROUP_W))
        dg = dg_ref[...]
        o = o_ref[...]
        do = dyd * _silu(dg)
        ddg_ref[...] = dyd * o * _dsilu(dg)
        prod = do * o
        dl = _per_head_lanes([jnp.sum(jnp.where(head == h, prod, 0.0), axis=-1, keepdims=True)
                              for h in range(N_HEADS)])
        do1_ref[...] = do.astype(BF16)
        dl1_ref[...] = dl
        _deinterleave_store(do, stage_a, ((PATTERN_DILS[1], do4_ref), (PATTERN_DILS[2], do16_ref)))
        _deinterleave_store(dl, stage_b, ((PATTERN_DILS[1], dl4_ref), (PATTERN_DILS[2], dl16_ref)))

    row = pl.BlockSpec((TILE_OUT, GROUP_W), lambda i: (i, 0))
    r4 = _residue_block(PATTERN_DILS[1], TILE_OUT, GROUP_W)
    r16 = _residue_block(PATTERN_DILS[2], TILE_OUT, GROUP_W)
    crow = pl.BlockSpec((TILE_OUT, LANES), lambda i: (i, 0))
    c4 = _residue_block(PATTERN_DILS[1], TILE_OUT, LANES)
    c16 = _residue_block(PATTERN_DILS[2], TILE_OUT, LANES)
    return pl.pallas_call(
        body, name="out_bwd", grid=(s // TILE_OUT,),
        in_specs=[pl.BlockSpec((TILE_OUT, D_MODEL), lambda i: (i, 0)),
                  pl.BlockSpec((D_MIX, D_MODEL), lambda i: (0, 0)),
                  pl.BlockSpec((TILE_OUT, GROUP_W), lambda i: (i, C_DG)), row],
        out_specs=[pl.BlockSpec((TILE_OUT, abc), lambda i: (i, 0)), row, row, r4, r16, crow, c4, c16],
        out_shape=[jax.ShapeDtypeStruct((s, abc), F32), jax.ShapeDtypeStruct((s, GROUP_W), F32),
                   jax.ShapeDtypeStruct((s, GROUP_W), BF16),
                   _by_residue(s, PATTERN_DILS[1], GROUP_W, BF16), _by_residue(s, PATTERN_DILS[2], GROUP_W, BF16),
                   jax.ShapeDtypeStruct((s, LANES), F32),
                   _by_residue(s, PATTERN_DILS[1], LANES, F32), _by_residue(s, PATTERN_DILS[2], LANES, F32)],
        scratch_shapes=_stage_scratch(TILE_OUT, GROUP_W, 1) + _stage_scratch(TILE_OUT, LANES, 1),
        compiler_params=_params(("parallel",)),
    )(dx, w, z, o)


def in_bwd(dz, w, x, g, dx_next):
    s = x.shape[0]

    def body(dz_ref, w_ref, x_ref, g_ref, dxn_ref, dx_ref, dg_ref):
        @pl.when(pl.program_id(0) == 0)
        def _():
            dg_ref[...] = jnp.zeros_like(dg_ref)

        dh = _dot(dz_ref[...], w_ref[...])
        xv = x_ref[...]
        rs = lax.rsqrt(jnp.mean(xv * xv, axis=-1, keepdims=True) + NORM_EPS)
        xh = xv * rs
        dg_ref[...] += _put_row(dg_ref.shape, 0, jnp.sum(dh * xh, axis=0, keepdims=True))
        dn = dh * g_ref[...]
        dx_ref[...] = dxn_ref[...] + rs * (dn - xh * jnp.mean(dn * xh, axis=-1, keepdims=True))

    return pl.pallas_call(
        body, name="in_bwd", grid=(s // TILE_IN,),
        in_specs=[pl.BlockSpec((TILE_IN, D_IN), lambda i: (i, 0)),
                  pl.BlockSpec((D_IN, D_MODEL), lambda i: (0, 0)),
                  pl.BlockSpec((TILE_IN, D_MODEL), lambda i: (i, 0)),
                  pl.BlockSpec((1, D_MODEL), lambda i: (0, 0)),
                  pl.BlockSpec((TILE_IN, D_MODEL), lambda i: (i, 0))],
        out_specs=[pl.BlockSpec((TILE_IN, D_MODEL), lambda i: (i, 0)),
                   pl.BlockSpec((SUBLANES, D_MODEL), lambda i: (0, 0))],
        out_shape=[jax.ShapeDtypeStruct((s, D_MODEL), F32), jax.ShapeDtypeStruct((SUBLANES, D_MODEL), F32)],
        compiler_params=_params(("arbitrary",)),
    )(dz, w, x, g, dx_next)


def grad_w_out(y, dx):
    s = y.shape[0]

    def body(y_ref, dx_ref, o_ref):
        @pl.when(pl.program_id(0) == 0)
        def _():
            o_ref[...] = jnp.zeros_like(o_ref)

        o_ref[...] += _dot_tn(y_ref[...], dx_ref[...].astype(BF16))

    return pl.pallas_call(
        body, name="grad_w_out", grid=(s // TILE_DW_OUT,),
        in_specs=[pl.BlockSpec((TILE_DW_OUT, D_MIX), lambda k: (k, 0)),
                  pl.BlockSpec((TILE_DW_OUT, D_MODEL), lambda k: (k, 0))],
        out_specs=pl.BlockSpec((D_MIX, D_MODEL), lambda k: (0, 0)),
        out_shape=jax.ShapeDtypeStruct((D_MIX, D_MODEL), F32),
        compiler_params=_params(("arbitrary",)),
    )(y, dx)


def grad_w_in(h, dz, chip):
    s = h.shape[0]
    rows = D_IN // N_CHIPS

    def body(chip_ref, h_ref, dz_ref, staged_ref, own_ref, acc):
        k = pl.program_id(0)

        @pl.when(k == 0)
        def _():
            acc[...] = jnp.zeros_like(acc)

        acc[...] += _dot_tn(dz_ref[...], h_ref[...])

        @pl.when(k == s // TILE_DW - 1)
        def _():
            for j in range(N_CHIPS):
                part = acc[j * rows:(j + 1) * rows, :]
                staged_ref[j] = part.astype(BF16)

                @pl.when(chip_ref[0] == j)
                def _():
                    own_ref[...] = part

    return pl.pallas_call(
        body, name="grad_w_in",
        grid_spec=pltpu.PrefetchScalarGridSpec(
            num_scalar_prefetch=1, grid=(s // TILE_DW,),
            in_specs=[pl.BlockSpec((TILE_DW, D_MODEL), lambda k, c: (k, 0)),
                      pl.BlockSpec((TILE_DW, D_IN), lambda k, c: (k, 0))],
            out_specs=[pl.BlockSpec((N_CHIPS, rows, D_MODEL), lambda k, c: (0, 0, 0)),
                       pl.BlockSpec((rows, D_MODEL), lambda k, c: (0, 0))],
            scratch_shapes=[pltpu.VMEM((D_IN, D_MODEL), F32)]),
        out_shape=[jax.ShapeDtypeStruct((N_CHIPS, rows, D_MODEL), BF16), jax.ShapeDtypeStruct((rows, D_MODEL), F32)],
        compiler_params=_params(("arbitrary",)),
    )(chip, h, dz)


def out_fwd_loss(y, w, x, g, tgt):
    s = x.shape[0]

    def body(y_ref, w_ref, x_ref, g_ref, t_ref, l_ref, dx_ref, dg_ref):
        @pl.when(pl.program_id(0) == 0)
        def _():
            l_ref[...] = jnp.zeros_like(l_ref)
            dg_ref[...] = jnp.zeros_like(dg_ref)

        xv = x_ref[...] + _dot(y_ref[...], w_ref[...])
        gv = g_ref[...]
        rs = lax.rsqrt(jnp.mean(xv * xv, axis=-1, keepdims=True) + NORM_EPS)
        xh = xv * rs
        e = xh * gv - t_ref[...]
        part = 0.5 * jnp.sum(jnp.mean(e * e, axis=-1, keepdims=True), axis=0, keepdims=True)
        l_ref[...] += jnp.broadcast_to(part, l_ref.shape)
        dy = e * (1.0 / D_MODEL)
        dg_ref[...] += _put_row(dg_ref.shape, 0, jnp.sum(dy * xh, axis=0, keepdims=True))
        dn = dy * gv
        dx_ref[...] = rs * (dn - xh * jnp.mean(dn * xh, axis=-1, keepdims=True))

    return pl.pallas_call(
        body, name="out_fwd_loss", grid=(s // TILE_OUT,),
        in_specs=[pl.BlockSpec((TILE_OUT, D_MIX), lambda i: (i, 0)),
                  pl.BlockSpec((D_MIX, D_MODEL), lambda i: (0, 0)),
                  pl.BlockSpec((TILE_OUT, D_MODEL), lambda i: (i, 0)),
                  pl.BlockSpec((1, D_MODEL), lambda i: (0, 0)),
                  pl.BlockSpec((TILE_OUT, D_MODEL), lambda i: (i, 0))],
        out_specs=[pl.BlockSpec((SUBLANES, LANES), lambda i: (0, 0)),
                   pl.BlockSpec((TILE_OUT, D_MODEL), lambda i: (i, 0)),
                   pl.BlockSpec((SUBLANES, D_MODEL), lambda i: (0, 0))],
        out_shape=[jax.ShapeDtypeStruct((SUBLANES, LANES), F32), jax.ShapeDtypeStruct((s, D_MODEL), F32),
                   jax.ShapeDtypeStruct((SUBLANES, D_MODEL), F32)],
        compiler_params=_params(("arbitrary",)),
    )(y, w, x, g, tgt)


def _attn_bias(dil):
    qi = np.arange(ATTN_BLOCK)[:, None]
    ki = np.arange(2 * ATTN_BLOCK)[None, :]
    delta = qi + ATTN_BLOCK - ki
    band = (delta >= 0) & (delta <= ATTN_BLOCK)
    out = np.empty((2, N_HEADS, ATTN_BLOCK, 2 * ATTN_BLOCK), np.float32)
    for f in range(2):
        ok = band & ((ki >= ATTN_BLOCK) | (f == 0))
        for h in range(N_HEADS):
            out[f, h] = np.where(ok, -ALIBI_SLOPES[h] * dil * delta, NEG_BIG)
    return jnp.asarray(out.reshape(2, N_HEADS * ATTN_BLOCK, 2 * ATTN_BLOCK))


def _stack_heads(a, head):
    return jnp.concatenate([jnp.where(head == h, a, jnp.zeros_like(a)) for h in range(N_HEADS)], axis=0)


def _unstack_heads(a, head):
    out = a[:ATTN_BLOCK]
    for h in range(1, N_HEADS):
        out = jnp.where(head == h, a[h * ATTN_BLOCK:(h + 1) * ATTN_BLOCK], out)
    return out


def _head_column(a):
    return jnp.concatenate([a[:, h:h + 1] for h in range(N_HEADS)], axis=0)


def _attn_specs(n_blocks):
    rows = ATTN_QB * ATTN_BLOCK
    cur = lambda c, w=GROUP_W: pl.BlockSpec((rows, w), lambda n, c=c: (n, c))
    prev = lambda c: pl.BlockSpec((ATTN_BLOCK, GROUP_W), lambda n, c=c: (jnp.maximum(n * ATTN_QB - 1, 0), c))
    nxt = lambda c, w=GROUP_W: pl.BlockSpec((ATTN_BLOCK, w),
                                            lambda n, c=c: (jnp.minimum(n * ATTN_QB + ATTN_QB, n_blocks - 1), c))
    return cur, prev, nxt


def _keys(kp_ref, k_ref, j):
    prev = kp_ref[...] if j == 0 else k_ref[(j - 1) * ATTN_BLOCK:j * ATTN_BLOCK, :]
    return jnp.concatenate([prev, k_ref[j * ATTN_BLOCK:(j + 1) * ATTN_BLOCK, :]], axis=0)


def attn_fwd(qkv, dil):
    s = qkv.shape[0]
    n_blocks = s // ATTN_BLOCK
    bps = n_blocks // dil
    rows = ATTN_QB * ATTN_BLOCK

    def body(q_ref, kp_ref, k_ref, vp_ref, v_ref, bias_ref, o_ref, lse_ref):
        n = pl.program_id(0)
        head = _head_of_lane((ATTN_BLOCK, GROUP_W))
        for j in range(ATTN_QB):
            sl = slice(j * ATTN_BLOCK, (j + 1) * ATTN_BLOCK)
            first = (((n * ATTN_QB + j) % bps) == 0).astype(jnp.int32)
            qs = _stack_heads(q_ref[sl, :], head)
            sc = _dot_nt(qs, _keys(kp_ref, k_ref, j)) * ATTN_SCALE + bias_ref[first]
            m = jnp.max(sc, axis=-1, keepdims=True)
            pr = jnp.exp(sc - m)
            l = jnp.sum(pr, axis=-1, keepdims=True)
            oh = _dot(pr.astype(BF16), _keys(vp_ref, v_ref, j)) / l
            o_ref[sl, :] = _unstack_heads(oh, head).astype(BF16)
            lse_ref[sl, :] = _unstack_heads(jnp.broadcast_to(m + jnp.log(l), oh.shape), head)

    cur, prev, _ = _attn_specs(n_blocks)
    bias = _attn_bias(dil)
    return pl.pallas_call(
        body, name=f"attn_fwd_d{dil}", grid=(n_blocks // ATTN_QB,),
        in_specs=[cur(0), prev(1), cur(1), prev(2), cur(2), pl.BlockSpec(bias.shape, lambda n: (0, 0, 0))],
        out_specs=[cur(0), cur(0)],
        out_shape=[jax.ShapeDtypeStruct((s, GROUP_W), BF16), jax.ShapeDtypeStruct((s, GROUP_W), F32)],
        compiler_params=_params(("parallel",)),
    )(qkv, qkv, qkv, qkv, qkv, bias)


def attn_bwd(qkv, do, lse, dlt, dil):
    s = qkv.shape[0]
    n_blocks = s // ATTN_BLOCK
    bps = n_blocks // dil
    rows = ATTN_QB * ATTN_BLOCK

    def body(q_ref, qn_ref, kp_ref, k_ref, vp_ref, v_ref, do_ref, don_ref, lse_ref, lsen_ref, dl_ref, dln_ref,
             bias_ref, out_ref, dk_acc, dv_acc):
        n = pl.program_id(0)
        head = _head_of_lane((ATTN_BLOCK, GROUP_W))
        dk_acc[...] = jnp.zeros_like(dk_acc)
        dv_acc[...] = jnp.zeros_like(dv_acc)

        def pair(qj, doj, lsej, dlj, kk, vv, bias, keep):
            qs = _stack_heads(qj, head)
            dos = _stack_heads(doj, head)
            sc = _dot_nt(qs, kk) * ATTN_SCALE + bias
            if keep is None:
                pr = jnp.exp(sc - _head_column(lsej))
            else:
                pr = jnp.exp(jnp.minimum(sc - _head_column(lsej), 0.0)) * keep
            dp = _dot_nt(dos, vv)
            ds = (pr * (dp - _head_column(dlj)) * ATTN_SCALE).astype(BF16)
            return ds, _dot_tn(ds, qs), _dot_tn(pr.astype(BF16), dos)

        for j in range(ATTN_QB):
            sl = slice(j * ATTN_BLOCK, (j + 1) * ATTN_BLOCK)
            first = (((n * ATTN_QB + j) % bps) == 0).astype(jnp.int32)
            kk = _keys(kp_ref, k_ref, j)
            ds, dks, dvs = pair(q_ref[sl, :], do_ref[sl, :], lse_ref[sl, :], dl_ref[sl, :],
                                kk, _keys(vp_ref, v_ref, j), bias_ref[first], None)
            out_ref[sl, 0:GROUP_W] = _unstack_heads(_dot(ds, kk), head)
            acc = slice(j * ATTN_BLOCK, (j + 2) * ATTN_BLOCK)
            dk_acc[acc, :] += dks
            dv_acc[acc, :] += dvs

        nxt = n * ATTN_QB + ATTN_QB
        valid = ((nxt < n_blocks) & ((nxt % bps) != 0)).astype(F32)
        last = slice((ATTN_QB - 1) * ATTN_BLOCK, ATTN_QB * ATTN_BLOCK)
        _, dks, dvs = pair(qn_ref[...], don_ref[...], lsen_ref[...], dln_ref[...], k_ref[last, :], v_ref[last, :],
                           bias_ref[0][:, :ATTN_BLOCK], valid)
        acc = slice(ATTN_QB * ATTN_BLOCK, (ATTN_QB + 1) * ATTN_BLOCK)
        dk_acc[acc, :] += dks
        dv_acc[acc, :] += dvs
        out_ref[:, GROUP_W:2 * GROUP_W] = dk_acc[ATTN_BLOCK:, :]
        out_ref[:, 2 * GROUP_W:3 * GROUP_W] = dv_acc[ATTN_BLOCK:, :]

    cur, prev, nxt = _attn_specs(n_blocks)
    bias = _attn_bias(dil)
    return pl.pallas_call(
        body, name=f"attn_bwd_d{dil}", grid=(n_blocks // ATTN_QB,),
        in_specs=[cur(0), nxt(0), prev(1), cur(1), prev(2), cur(2), cur(0), nxt(0),
                  cur(0, LANES), nxt(0, LANES), cur(0, LANES), nxt(0, LANES),
                  pl.BlockSpec(bias.shape, lambda n: (0, 0, 0))],
        out_specs=pl.BlockSpec((rows, 3 * GROUP_W), lambda n: (n, 0)),
        out_shape=jax.ShapeDtypeStruct((s, 3 * GROUP_W), F32),
        scratch_shapes=[pltpu.VMEM(((ATTN_QB + 1) * ATTN_BLOCK, GROUP_W), F32),
                        pltpu.VMEM(((ATTN_QB + 1) * ATTN_BLOCK, GROUP_W), F32)],
        compiler_params=_params(("parallel",)),
    )(qkv, qkv, qkv, qkv, qkv, qkv, do, do, lse, lse, dlt, dlt, bias)


def _zcol(c):
    return pl.BlockSpec((TILE_MIX, GROUP_W), lambda i, c=c: (i, c))


def _zhalo(c):
    per = TILE_MIX // SUBLANES
    return pl.BlockSpec((SUBLANES, GROUP_W), lambda i, c=c: (jnp.maximum(i * per - 1, 0), c))


def _full(shape):
    return pl.BlockSpec(shape, lambda i: tuple(0 for _ in shape))


def _of_layer(a, l):
    rest = a.shape[1:]
    return pl.BlockSpec((None,) + rest, lambda i: (l,) + tuple(0 for _ in rest))


def _softplus_neg(lam):
    nl = -lam
    return jnp.maximum(nl, 0.0) + jnp.log1p(jnp.exp(-jnp.abs(nl)))


def _lru_gates(xb, wa_ref, wx_ref, ba, bx, lam):
    xbb = xb.astype(BF16)
    r = jax.nn.sigmoid(_dot(xbb, wa_ref[...]) + ba)
    ig = jax.nn.sigmoid(_dot(xbb, wx_ref[...]) + bx)
    log_a = (-RG_C * r) * _softplus_neg(lam)
    a = jnp.exp(log_a)
    mult = jnp.sqrt(_neg_expm1(2.0 * log_a))
    return r, ig, a, mult


LRU_SAVED = 5


def _gmlp_spatial(ws_ref, vvb, head):
    outs = []
    for j in range(vvb.shape[0] // GMLP_CHUNK):
        blk = vvb[j * GMLP_CHUNK:(j + 1) * GMLP_CHUNK, :]
        acc = jnp.zeros((GMLP_CHUNK, GROUP_W), F32)
        for h in range(N_HEADS):
            acc = jnp.where(head[:GMLP_CHUNK] == h, _dot(ws_ref[h], blk), acc)
        outs.append(acc)
    return jnp.concatenate(outs, axis=0)


def mix_fwd(z, attn, wts, l):
    s = z.shape[0]
    d4, d16 = PATTERN_DILS[1], PATTERN_DILS[2]

    def body(ax_ref, ab_ref, ac_ref, ag_ref, rx_ref, rg_ref, cu_ref, cv_ref, cg_ref, dg_ref,
             axh_ref, ach_ref, rxh_ref, o1_ref, l1_ref, o4_ref, l4_ref, o16_ref, l16_ref,
             caw_ref, crw_ref, crb_ref, wa_ref, wx_ref, ba_ref, bx_ref, lam_ref, gng_ref, ws_ref, bs_ref,
             y_ref, hl_ref, o_ref, lse_ref, lse4_ref, lse16_ref, lru_ref, carry, *stage):
        st_a, st_b, st_c, st_d, st_e = (stage[2 * k:2 * k + 2] for k in range(5))
        i = pl.program_id(0)

        @pl.when(i == 0)
        def _():
            carry[...] = jnp.zeros_like(carry)

        nz = (i > 0).astype(F32)
        head = _head_of_lane((TILE_MIX, GROUP_W))

        pa = ac_ref[...] * ax_ref[...]
        pah = ach_ref[...] * axh_ref[...] * nz
        cv = caw_ref[2:3, :] * pa + caw_ref[1:2, :] * _shift_down(pa, pah, 1) + caw_ref[0:1, :] * _shift_down(pa, pah, 2)
        y_ref[:, 0:GROUP_W] = (ab_ref[...] * cv * _silu(ag_ref[...])).astype(BF16)

        rx = rx_ref[...]
        rxh = rxh_ref[...] * nz
        xb = (crw_ref[3:4, :] * rx + crw_ref[2:3, :] * _shift_down(rx, rxh, 1) + crw_ref[1:2, :] * _shift_down(rx, rxh, 2)
              + crw_ref[0:1, :] * _shift_down(rx, rxh, 3) + crb_ref[...])
        r, ig, a, mult = _lru_gates(xb, wa_ref, wx_ref, ba_ref[...], bx_ref[...], lam_ref[...])
        for k, val in enumerate((xb, r, ig, a, mult)):
            lru_ref[:, k * GROUP_W:(k + 1) * GROUP_W] = val
        hl = _scan_causal(a, mult * (ig * xb), carry[SUBLANES - 1:SUBLANES, :])
        hl_ref[...] = hl
        carry[...] = hl[TILE_MIX - SUBLANES:, :]
        y_ref[:, GROUP_W:2 * GROUP_W] = (hl * _silu(rg_ref[...])).astype(BF16)

        u = _gelu(cu_ref[...])
        gv = _gelu(cv_ref[...])
        rs = lax.rsqrt(jnp.mean(gv * gv, axis=-1, keepdims=True) + NORM_EPS)
        vvb = (gv * rs * gng_ref[...]).astype(BF16)
        sp = _gmlp_spatial(ws_ref, vvb, head) + jnp.concatenate([bs_ref[...]] * (TILE_MIX // GMLP_CHUNK), axis=0)
        y_ref[:, 2 * GROUP_W:3 * GROUP_W] = (u * sp * _silu(cg_ref[...])).astype(BF16)

        ops = (o1_ref[...].astype(F32), _interleave_load(o4_ref, d4, st_a), _interleave_load(o16_ref, d16, st_b))
        lps = (l1_ref[...], _interleave_load(l4_ref, d4, st_c), _interleave_load(l16_ref, d16, st_d))
        m = jnp.maximum(jnp.maximum(lps[0], lps[1]), lps[2])
        zsum = jnp.zeros_like(m)
        o = jnp.zeros_like(m)
        for op, lp in zip(ops, lps):
            w = jnp.exp(lp - m)
            zsum = zsum + w
            o = o + w * op
        o = o / zsum
        lse = m + jnp.log(zsum)
        lse = _per_head_lanes([lse[:, h * HEAD_DIM:h * HEAD_DIM + 1] for h in range(N_HEADS)])
        o_ref[...] = o
        lse_ref[...] = lse
        _deinterleave_store(lse, st_e, ((d4, lse4_ref), (d16, lse16_ref)))
        y_ref[:, 3 * GROUP_W:4 * GROUP_W] = (o * _silu(dg_ref[...])).astype(BF16)

    row = pl.BlockSpec((TILE_MIX, GROUP_W), lambda i: (i, 0))
    r4 = _residue_block(d4, TILE_MIX, GROUP_W)
    r16 = _residue_block(d16, TILE_MIX, GROUP_W)
    crow = pl.BlockSpec((TILE_MIX, LANES), lambda i: (i, 0))
    c4 = _residue_block(d4, TILE_MIX, LANES)
    c16 = _residue_block(d16, TILE_MIX, LANES)
    names = ("caw", "crw", "crb", "wa", "wx", "ba", "bx", "lam", "gng", "ws", "bs")
    in_specs = ([_zcol(c) for c in (C_AX, C_AB, C_AC, C_AG, C_RX, C_RG, C_CU, C_CV, C_CG, C_DG)]
                + [_zhalo(C_AX), _zhalo(C_AC), _zhalo(C_RX), row, row, r4, r4, r16, r16]
                + [_of_layer(wts[k], l) for k in names])
    return pl.pallas_call(
        body, name="mix_fwd", grid=(s // TILE_MIX,),
        in_specs=in_specs,
        out_specs=[pl.BlockSpec((TILE_MIX, D_MIX), lambda i: (i, 0)), row, row, crow, c4, c16,
                   pl.BlockSpec((TILE_MIX, LRU_SAVED * GROUP_W), lambda i: (i, 0))],
        out_shape=([jax.ShapeDtypeStruct((s, D_MIX), BF16)] + [jax.ShapeDtypeStruct((s, GROUP_W), F32)] * 2
                   + [jax.ShapeDtypeStruct((s, LANES), F32), _by_residue(s, d4, LANES, F32),
                      _by_residue(s, d16, LANES, F32), jax.ShapeDtypeStruct((s, LRU_SAVED * GROUP_W), F32)]),
        scratch_shapes=([pltpu.VMEM((SUBLANES, GROUP_W), F32)] + _stage_scratch(TILE_MIX, GROUP_W, 4)
                        + _stage_scratch(TILE_MIX, LANES, 1)),
        compiler_params=_params(("arbitrary",)),
    )(*([z] * 13), *[a for pair in attn for a in pair], *[wts[k] for k in names])


def mix_bwd(dy, z, hl, lru, dqkv, ddg, wts, l):
    s = z.shape[0]
    d4, d16 = PATTERN_DILS[1], PATTERN_DILS[2]
    n_tiles = s // TILE_MIX

    def body(dya_ref, dyb_ref, dyc_ref, ax_ref, ab_ref, ac_ref, ag_ref, rx_ref, rg_ref, cu_ref, cv_ref, cg_ref,
             axh_ref, ach_ref, rxh_ref, hl_ref, hlh_ref, lru_ref, dqkv1_ref, dqkv4_ref, dqkv16_ref, ddg_ref,
             caw_ref, crw_ref, crb_ref, wa_ref, wx_ref, ba_ref, bx_ref, lam_ref, gng_ref, ws_ref, wst_ref, bs_ref,
             dz_ref, ga_ref, gr_ref, gn_ref, gwa_ref, gwx_ref, gws_ref, gbs_ref,
             c_dcv, c_g, c_a, c_dxb, *stage):
        st_a, st_b = stage[:len(stage) // 2], stage[len(stage) // 2:]
        step = pl.program_id(0)
        i = n_tiles - 1 - step

        @pl.when(step == 0)
        def _():
            for r in (c_dcv, c_g, c_a, c_dxb, ga_ref, gr_ref, gn_ref, gwa_ref, gwx_ref, gws_ref, gbs_ref):
                r[...] = jnp.zeros_like(r)

        nz = (i > 0).astype(F32)
        head = _head_of_lane((TILE_MIX, GROUP_W))
        shp8 = (SUBLANES, GROUP_W)
        colsum = lambda v: jnp.sum(v, axis=0, keepdims=True)

        ax, ab, ac, ag = ax_ref[...], ab_ref[...], ac_ref[...], ag_ref[...]
        dya = dya_ref[...]
        pa = ac * ax
        pah = ach_ref[...] * axh_ref[...] * nz
        pa1 = _shift_down(pa, pah, 1)
        pa2 = _shift_down(pa, pah, 2)
        cv = caw_ref[2:3, :] * pa + caw_ref[1:2, :] * pa1 + caw_ref[0:1, :] * pa2
        sg = _silu(ag)
        dz_ref[:, C_AB * GROUP_W:(C_AB + 1) * GROUP_W] = (dya * cv * sg).astype(BF16)
        dz_ref[:, C_AG * GROUP_W:(C_AG + 1) * GROUP_W] = (dya * ab * cv * _dsilu(ag)).astype(BF16)
        dcv = dya * ab * sg
        nxt = c_dcv[...]
        dpa = caw_ref[2:3, :] * dcv + caw_ref[1:2, :] * _shift_up(dcv, nxt, 1) + caw_ref[0:1, :] * _shift_up(dcv, nxt, 2)
        c_dcv[...] = dcv[:SUBLANES, :]
        dz_ref[:, C_AC * GROUP_W:(C_AC + 1) * GROUP_W] = (dpa * ax).astype(BF16)
        dz_ref[:, C_AX * GROUP_W:(C_AX + 1) * GROUP_W] = (dpa * ac).astype(BF16)
        ga_ref[...] += (_put_row(shp8, 2, colsum(dcv * pa)) + _put_row(shp8, 1, colsum(dcv * pa1))
                        + _put_row(shp8, 0, colsum(dcv * pa2)))

        rx, rg = rx_ref[...], rg_ref[...]
        dyb = dyb_ref[...]
        rxh = rxh_ref[...] * nz
        rx1, rx2, rx3 = _shift_down(rx, rxh, 1), _shift_down(rx, rxh, 2), _shift_down(rx, rxh, 3)
        xb, r, ig, a, mult = (lru_ref[:, k * GROUP_W:(k + 1) * GROUP_W] for k in range(LRU_SAVED))
        lam = lam_ref[...]
        sp = _softplus_neg(lam)
        hl = hl_ref[...]
        hprev = _shift_down(hl, hlh_ref[...] * nz, 1)
        dz_ref[:, C_RG * GROUP_W:(C_RG + 1) * GROUP_W] = (dyb * hl * _dsilu(rg)).astype(BF16)
        dh = dyb * _silu(rg)
        a_next = _shift_up(a, c_a[...], 1)
        g = _scan_anticausal(a_next, dh, c_g[0:1, :])
        c_g[...] = g[:SUBLANES, :]
        c_a[...] = a[:SUBLANES, :]
        u = ig * xb
        da = g * hprev
        dmult = g * u
        du = g * mult
        dlog_a = da * a - dmult * (a * a) / mult
        dr = dlog_a * (-RG_C * sp)
        dga = dr * r * (1.0 - r)
        dgx = (du * xb) * ig * (1.0 - ig)
        dgab, dgxb = dga.astype(BF16), dgx.astype(BF16)
        dxb = du * ig + _dot_nt(dgab, wa_ref[...]) + _dot_nt(dgxb, wx_ref[...])
        xbb = xb.astype(BF16)
        gwa_ref[...] += _dot_tn(xbb, dgab)
        gwx_ref[...] += _dot_tn(xbb, dgxb)
        nxt = c_dxb[...]
        drx = (crw_ref[3:4, :] * dxb + crw_ref[2:3, :] * _shift_up(dxb, nxt, 1) + crw_ref[1:2, :] * _shift_up(dxb, nxt, 2)
               + crw_ref[0:1, :] * _shift_up(dxb, nxt, 3))
        c_dxb[...] = dxb[:SUBLANES, :]
        dz_ref[:, C_RX * GROUP_W:(C_RX + 1) * GROUP_W] = drx.astype(BF16)
        dlam = colsum(dlog_a * (-RG_C * r)) * (-jax.nn.sigmoid(-lam))
        gr_ref[...] += (_put_row(shp8, 3, colsum(dxb * rx)) + _put_row(shp8, 2, colsum(dxb * rx1))
                        + _put_row(shp8, 1, colsum(dxb * rx2)) + _put_row(shp8, 0, colsum(dxb * rx3))
                        + _put_row(shp8, 4, colsum(dxb)) + _put_row(shp8, 5, colsum(dga))
                        + _put_row(shp8, 6, colsum(dgx)) + _put_row(shp8, 7, dlam))

        cu, cvv, cg = cu_ref[...], cv_ref[...], cg_ref[...]
        dyc = dyc_ref[...]
        u_c, du_c = _gelu_and_grad(cu)
        gv, dgv_c = _gelu_and_grad(cvv)
        rs = lax.rsqrt(jnp.mean(gv * gv, axis=-1, keepdims=True) + NORM_EPS)
        vh = gv * rs
        gng = gng_ref[...]
        vvb = (vh * gng).astype(BF16)
        spat = _gmlp_spatial(ws_ref, vvb, head) + jnp.concatenate([bs_ref[...]] * (TILE_MIX // GMLP_CHUNK), axis=0)
        sgc = _silu(cg)
        dz_ref[:, C_CU * GROUP_W:(C_CU + 1) * GROUP_W] = (dyc * spat * sgc * du_c).astype(BF16)
        dz_ref[:, C_CG * GROUP_W:(C_CG + 1) * GROUP_W] = (dyc * u_c * spat * _dsilu(cg)).astype(BF16)
        dsp = dyc * u_c * sgc
        dspb = dsp.astype(BF16)
        tril = (lax.broadcasted_iota(jnp.int32, (GMLP_CHUNK, GMLP_CHUNK), 0)
                >= lax.broadcasted_iota(jnp.int32, (GMLP_CHUNK, GMLP_CHUNK), 1))
        head_c = head[:GMLP_CHUNK]
        dvv_parts = []
        gbs = jnp.zeros((GMLP_CHUNK, GROUP_W), F32)
        for j in range(TILE_MIX // GMLP_CHUNK):
            sl = slice(j * GMLP_CHUNK, (j + 1) * GMLP_CHUNK)
            dblk = dspb[sl, :]
            vblk = vvb[sl, :]
            gbs = gbs + dsp[sl, :]
            acc = jnp.zeros((GMLP_CHUNK, GROUP_W), F32)
            for h in range(N_HEADS):
                acc = jnp.where(head_c == h, _dot(wst_ref[h], dblk), acc)
                dm = jnp.where(head_c == h, dblk, jnp.zeros_like(dblk))
                gws_ref[h] += jnp.where(tril, _dot_nt(dm, vblk), 0.0)
            dvv_parts.append(acc)
        gbs_ref[...] += gbs
        dvv = jnp.concatenate(dvv_parts, axis=0)
        gn_ref[...] += _put_row(shp8, 0, colsum(dvv * vh))
        dvh = dvv * gng
        dgv = rs * (dvh - vh * jnp.mean(dvh * vh, axis=-1, keepdims=True))
        dz_ref[:, C_CV * GROUP_W:(C_CV + 1) * GROUP_W] = (dgv * dgv_c).astype(BF16)

        dsum = dqkv1_ref[...] + _interleave_load(dqkv4_ref, d4, st_a) + _interleave_load(dqkv16_ref, d16, st_b)
        dz_ref[:, C_DQ * GROUP_W:(C_DV + 1) * GROUP_W] = dsum.astype(BF16)
        dz_ref[:, C_DG * GROUP_W:(C_DG + 1) * GROUP_W] = ddg_ref[...].astype(BF16)

    per = TILE_MIX // SUBLANES
    qkv_w = 3 * GROUP_W
    rev = lambda c: pl.BlockSpec((TILE_MIX, GROUP_W), lambda t, c=c: (n_tiles - 1 - t, c))
    revh = lambda c: pl.BlockSpec((SUBLANES, GROUP_W),
                                  lambda t, c=c: (jnp.maximum((n_tiles - 1 - t) * per - 1, 0), c))
    revr = lambda dil: pl.BlockSpec((dil, TILE_MIX // dil, qkv_w), lambda t: (0, n_tiles - 1 - t, 0))
    names = ("caw", "crw", "crb", "wa", "wx", "ba", "bx", "lam", "gng", "ws", "wst", "bs")
    in_specs = ([rev(0), rev(1), rev(2)]
                + [rev(c) for c in (C_AX, C_AB, C_AC, C_AG, C_RX, C_RG, C_CU, C_CV, C_CG)]
                + [revh(C_AX), revh(C_AC), revh(C_RX), rev(0), revh(0),
                   pl.BlockSpec((TILE_MIX, LRU_SAVED * GROUP_W), lambda t: (n_tiles - 1 - t, 0)),
                   pl.BlockSpec((TILE_MIX, qkv_w), lambda t: (n_tiles - 1 - t, 0)), revr(d4), revr(d16), rev(0)]
                + [_of_layer(wts[k], l) for k in names])
    small = jax.ShapeDtypeStruct((SUBLANES, GROUP_W), F32)
    sq = jax.ShapeDtypeStruct((GROUP_W, GROUP_W), F32)
    out_shape = [jax.ShapeDtypeStruct((s, D_IN), BF16), small, small, small, sq, sq,
                 jax.ShapeDtypeStruct((N_HEADS, GMLP_CHUNK, GMLP_CHUNK), F32),
                 jax.ShapeDtypeStruct((GMLP_CHUNK, GROUP_W), F32)]
    out_specs = ([pl.BlockSpec((TILE_MIX, D_IN), lambda t: (n_tiles - 1 - t, 0))]
                 + [_full(o.shape) for o in out_shape[1:]])
    return pl.pallas_call(
        body, name="mix_bwd", grid=(n_tiles,),
        in_specs=in_specs, out_specs=out_specs, out_shape=out_shape,
        scratch_shapes=[pltpu.VMEM((SUBLANES, GROUP_W), F32)] * 4 + _stage_scratch(TILE_MIX, qkv_w, 2),
        compiler_params=_params(("arbitrary",)),
    )(dy, dy, dy, *([z] * 12), hl, hl, lru, *dqkv, ddg, *[wts[k] for k in names])


def prepare_small_weights(p):
    tril = jnp.tril(jnp.ones((GMLP_CHUNK, GMLP_CHUNK), dtype=bool))
    ws = jnp.where(tril, p["gmlp_ws"], 0.0).astype(BF16)
    row = lambda a: a[:, None, :]
    eye = jnp.eye(N_HEADS, dtype=F32)
    bd = lambda w: (w[:, :, :, None, :] * eye[None, :, None, :, None]).reshape(-1, GROUP_W, GROUP_W).astype(BF16)
    return dict(
        caw=p["conv_a_w"], crw=p["conv_r_w"], crb=row(p["conv_r_b"]),
        wa=bd(p["lru_wa"]), wx=bd(p["lru_wx"]),
        ba=row(p["lru_ba"]), bx=row(p["lru_bx"]), lam=row(p["lru_lambda"]), gng=row(p["gmlp_norm_g"]),
        ws=ws, wst=jnp.swapaxes(ws, 2, 3),
        bs=jnp.repeat(jnp.swapaxes(p["gmlp_bs"], 1, 2), HEAD_DIM, axis=2))


def _flat(a):
    return a.reshape(a.shape[0] * a.shape[1], a.shape[2])


def _split(a, dil):
    return a.reshape(dil, a.shape[0] // dil, a.shape[1])


def local_step(x, tgt, final_g, depth, chip, layer_weights, projections_done):
    saved = []
    for l in range(depth):
        gain, w_in_l, rest = layer_weights(l, x)
        z, h, *qkvs = in_fwd(x, gain, w_in_l)
        w_out_l, wts = rest(z)
        qkvs = [_flat(q) if q.ndim == 3 else q for q in qkvs]
        attn = []
        for q, d in zip(qkvs, PATTERN_DILS):
            o_p, lse_p = attn_fwd(q, d)
            attn.append((o_p, lse_p) if d == 1 else (_split(o_p, d), _split(lse_p, d)))
        y, hl, o, lse, lse4, lse16, lru = mix_fwd(z, attn, wts, l)
        saved.append(dict(x=x, z=z, h=h, y=y, hl=hl, o=o, qkvs=qkvs, lses=(lse, _flat(lse4), _flat(lse16)), wts=wts, lru=lru,
                          gain=gain, w_in=w_in_l, w_out=w_out_l))
        if l < depth - 1:
            x = out_fwd(y, w_out_l, x)
        else:
            loss, dx, dfg = out_fwd_loss(y, w_out_l, x, final_g[None, :], tgt)
    raw = {k: [None] * depth for k in ("gain", "a", "r", "n", "wa", "wx", "ws", "bs")}
    zero = None
    for l in reversed(range(depth)):
        sv = saved[l]
        dy, ddg, do1, do4, do16, dl1, dl4, dl16 = out_bwd(dx, sv["w_out"], sv["z"], sv["o"])
        g_w_out = grad_w_out(sv["y"], dx)
        dqkv = []
        for q, do, lse, dl, d in zip(sv["qkvs"], (do1, _flat(do4), _flat(do16)), sv["lses"],
                                     (dl1, _flat(dl4), _flat(dl16)), PATTERN_DILS):
            g = attn_bwd(q, do, lse, dl, d)
            dqkv.append(g if d == 1 else _split(g, d))
        dz, ga, gr, gn, gwa, gwx, gws, gbs = mix_bwd(dy, sv["z"], sv["hl"], sv["lru"], dqkv, ddg, sv["wts"], l)
        gain = sv["gain"] if zero is None else sv["gain"] + zero
        zero = projections_done(l, *grad_w_in(sv["h"], dz, chip), g_w_out)
        if l == 0 and zero is not None:
            gain = gain + zero
        dx, dgn = in_bwd(dz, sv["w_in"], sv["x"], gain, dx)
        for k, g in zip(("gain", "a", "r", "n", "wa", "wx", "ws", "bs"), (dgn, ga, gr, gn, gwa, gwx, gws, gbs)):
            raw[k][l] = g
    st = {k: jnp.stack(v) for k, v in raw.items()}
    eye = jnp.eye(N_HEADS, dtype=F32)[None, :, None, :, None]
    diag = lambda g: (g.reshape(depth, N_HEADS, HEAD_DIM, N_HEADS, HEAD_DIM) * eye).sum(axis=3)
    grads = dict(
        norm_g=st["gain"][:, 0], conv_a_w=st["a"][:, :3], conv_r_w=st["r"][:, :4], conv_r_b=st["r"][:, 4],
        lru_ba=st["r"][:, 5], lru_bx=st["r"][:, 6], lru_lambda=st["r"][:, 7], gmlp_norm_g=st["n"][:, 0],
        lru_wa=diag(st["wa"]), lru_wx=diag(st["wx"]), gmlp_ws=st["ws"],
        gmlp_bs=jnp.swapaxes(st["bs"].reshape(depth, GMLP_CHUNK, N_HEADS, HEAD_DIM).sum(-1), 1, 2),
        final_g=dfg[0])
    return loss, dx, grads


MESH = pl.DeviceIdType.MESH
N_CHIPS = 4
N_DEV = 8
ANY = pl.BlockSpec(memory_space=pl.ANY)


def _place():
    x, y, c = lax.axis_index("x"), lax.axis_index("y"), lax.axis_index("c")
    chips = [(1 - x, y), (x, 1 - y), (1 - x, 1 - y)]
    return x, y, c, chips


def _remote(src, dst, ssem, rsem, to):
    return pltpu.make_async_remote_copy(src_ref=src, dst_ref=dst, send_sem=ssem, recv_sem=rsem,
                                        device_id=to, device_id_type=MESH)


HBM = pl.BlockSpec(memory_space=pltpu.HBM)
SEM = pl.BlockSpec(memory_space=pltpu.SEMAPHORE)
DATAFLOW = pltpu.SideEffectType.DATAFLOW_SIDE_EFFECTING
GATHER, SCATTER = "gather", "scatter"


def _chip_copies(mode, src_refs, land_refs, ssem, rsem):
    x, y, c, chips = _place()
    me = 2 * x + y
    n = len(src_refs)
    copies = []
    for k, (cx, cy) in enumerate(chips):
        for a in range(n):
            if mode == GATHER:
                src, dst = src_refs[a], land_refs[a].at[me]
            else:
                src, dst = src_refs[a].at[2 * cx + cy], land_refs[a].at[k]
            copies.append(_remote(src, dst, ssem.at[n * k + a], rsem.at[n * k + a], (cx, cy, c)))
    return copies


def exchange_start(mode, srcs, after, name):
    n = len(srcs)
    if mode == GATHER:
        lands = [lax.empty((N_CHIPS,) + s.shape, s.dtype) for s in srcs]
    else:
        lands = [lax.empty((N_CHIPS - 1,) + s.shape[1:], s.dtype) for s in srcs]
    extra = [] if after is None else [after]

    def body(*refs):
        src_refs, land_refs = refs[:n], refs[n:2 * n]
        ssem, rsem = refs[2 * n + len(extra)], refs[2 * n + len(extra) + 1]
        token = refs[-1]
        for cp in _chip_copies(mode, src_refs, land_refs, ssem, rsem):
            cp.start()
        token[...] = jnp.zeros_like(token)

    arrays = list(srcs) + lands
    return pl.pallas_call(
        body, name=name,
        out_shape=(pltpu.SemaphoreType.DMA((3 * n,)), pltpu.SemaphoreType.DMA((3 * n,)),
                   *[pltpu.HBM(a.shape, a.dtype) for a in arrays], jax.ShapeDtypeStruct((SUBLANES, LANES), F32)),
        in_specs=[HBM] * (2 * n) + [ANY] * len(extra),
        out_specs=(SEM, SEM, *[HBM] * (2 * n), pl.BlockSpec(memory_space=pltpu.VMEM)),
        input_output_aliases={i: 2 + i for i in range(2 * n)},
        compiler_params=pltpu.CompilerParams(has_side_effects=DATAFLOW),
    )(*[pltpu.with_memory_space_constraint(a, pltpu.HBM) for a in arrays], *extra)


def exchange_wait(mode, started, after, name):
    ssem, rsem, *thru, _ = started
    n = len(thru) // 2

    def body(*refs):
        src_refs, land_refs = refs[:n], refs[n:2 * n]
        ssem_ref, rsem_ref = refs[2 * n], refs[2 * n + 1]
        for cp in _chip_copies(mode, src_refs, land_refs, ssem_ref, rsem_ref):
            cp.wait_send()
            cp.wait_recv()

    outs = pl.pallas_call(
        body, name=name,
        out_shape=[pltpu.HBM(a.shape, a.dtype) for a in thru],
        in_specs=[HBM] * (2 * n) + [SEM, SEM, ANY],
        out_specs=[HBM] * (2 * n),
        input_output_aliases={i: i for i in range(2 * n)},
        compiler_params=pltpu.CompilerParams(has_side_effects=DATAFLOW),
    )(*thru, ssem, rsem, after)
    return outs[n:]


def sibling_exchange(p1, p2):
    def body(p1_ref, p2_ref, q1_ref, q2_ref, ssem, rsem):
        x, y, c, _ = _place()
        copies = [_remote(p_ref, q_ref, ssem.at[a], rsem.at[a], (x, y, 1 - c))
                  for a, (p_ref, q_ref) in enumerate(((p1_ref, q1_ref), (p2_ref, q2_ref)))]
        for cp in copies:
            cp.start()
        for cp in copies:
            cp.wait()

    return pl.pallas_call(
        body, name="sibling_exchange",
        in_specs=[ANY, ANY], out_specs=[ANY, ANY],
        out_shape=[jax.ShapeDtypeStruct(p.shape, p.dtype) for p in (p1, p2)],
        scratch_shapes=[pltpu.SemaphoreType.DMA((2,)), pltpu.SemaphoreType.DMA((2,))],
    )(p1, p2)


def all_reduce_small(v):
    r, n = v.shape
    piece = r // N_DEV

    def body(x_ref, out_ref, recv, ssem1, rsem1, ssem2, rsem2):
        x, y, c, _ = _place()
        me = 4 * x + 2 * y + c

        def peer(k):
            px = 1 - x if (k >> 2) & 1 else x
            py = 1 - y if (k >> 1) & 1 else y
            pc = 1 - c if k & 1 else c
            return (px, py, pc), 4 * px + 2 * py + pc

        def rows(ref, d):
            return ref.at[pl.ds(d * piece, piece), :]

        scatter = []
        for k in range(1, N_DEV):
            to, idx = peer(k)
            scatter.append(_remote(rows(x_ref, idx), recv.at[k], ssem1.at[k - 1], rsem1.at[k - 1], to))
            scatter[-1].start()
        acc = rows(x_ref, me)[...]
        for k in range(1, N_DEV):
            scatter[k - 1].wait_recv()
            acc = acc + recv[k]
        rows(out_ref, me)[...] = acc

        gather = []
        for k in range(1, N_DEV):
            to, _ = peer(k)
            gather.append(_remote(rows(out_ref, me), rows(out_ref, me), ssem2.at[k - 1], rsem2.at[k - 1], to))
            gather[-1].start()
        for k in range(1, N_DEV):
            to, idx = peer(k)
            _remote(rows(out_ref, idx), rows(out_ref, idx), ssem2.at[k - 1], rsem2.at[k - 1], to).wait_recv()
        for cp in scatter + gather:
            cp.wait_send()

    return pl.pallas_call(
        body, name="all_reduce_small",
        out_shape=jax.ShapeDtypeStruct((r, n), v.dtype),
        in_specs=[pl.BlockSpec(memory_space=pltpu.VMEM)],
        out_specs=pl.BlockSpec(memory_space=pltpu.VMEM),
        scratch_shapes=[pltpu.VMEM((N_DEV, piece, n), v.dtype)] + [pltpu.SemaphoreType.DMA((N_DEV - 1,))] * 4,
        compiler_params=pltpu.CompilerParams(vmem_limit_bytes=VMEM_LIMIT),
    )(v)


TILE_ROWS = 256


def _row_tile(r):
    return max(t for t in range(SUBLANES, TILE_ROWS + 1, SUBLANES) if r % t == 0)


def sum_partials(owns, parts):
    k, r, c = parts[0].shape
    depth = len(owns)
    tile = _row_tile(r)

    def body(buf_ref, o_ref, p_ref, out_ref):
        acc = o_ref[...]
        for i in range(k):
            acc = acc + p_ref[i].astype(F32)
        out_ref[0] = acc

    out = lax.empty((depth, r, c), F32)
    for l in range(depth):
        out = pl.pallas_call(
            functools.partial(body), name="sum_partials", grid=(r // tile,),
            in_specs=[ANY, pl.BlockSpec((tile, c), lambda i: (i, 0)), pl.BlockSpec((k, tile, c), lambda i: (0, i, 0))],
            out_specs=pl.BlockSpec((1, tile, c), lambda i, l=l: (l, i, 0)),
            out_shape=jax.ShapeDtypeStruct((depth, r, c), F32),
            input_output_aliases={0: 0},
            compiler_params=_params(("parallel",)),
        )(out, owns[l], parts[l])
    return out


def _adamw_update(w, g, m, v):
    m2 = ADAM_B1 * m + (1.0 - ADAM_B1) * g
    v2 = ADAM_B2 * v + (1.0 - ADAM_B2) * (g * g)
    m_hat = m2 / (1.0 - ADAM_B1 ** ADAM_STEP)
    v_hat = v2 / (1.0 - ADAM_B2 ** ADAM_STEP)
    return -ADAM_LR * (m_hat / (jnp.sqrt(v_hat) + ADAM_EPS) + ADAM_WD * w), m2, v2


def adamw_small(ws, gs, ms, vs):
    n = len(ws)

    def body(*refs):
        ins, outs = refs[:4 * n], refs[4 * n:]
        for i in range(n):
            d, m2, v2 = _adamw_update(ins[i][...], ins[n + i][...], ins[2 * n + i][...], ins[3 * n + i][...])
            outs[3 * i][...] = d
            outs[3 * i + 1][...] = m2
            outs[3 * i + 2][...] = v2

    outs = pl.pallas_call(
        body, name="adamw_small",
        out_shape=[jax.ShapeDtypeStruct(w.shape, F32) for w in ws for _ in range(3)],
    )(*ws, *gs, *ms, *vs)
    return [tuple(outs[3 * i:3 * i + 3]) for i in range(n)]


def adamw(w, ga, gb, m, v):
    n, r, c = w.shape
    tile = _row_tile(r)

    def body(w_ref, ga_ref, gb_ref, m_ref, v_ref, g_ref, d_ref, m2_ref, v2_ref):
        g = ga_ref[...] + gb_ref[...]
        g_ref[...] = g
        d_ref[...], m2_ref[...], v2_ref[...] = _adamw_update(w_ref[...], g, m_ref[...], v_ref[...])

    spec = pl.BlockSpec((1, tile, c), lambda j, i: (j, i, 0))
    return pl.pallas_call(
        body, name="adamw", grid=(n, r // tile),
        in_specs=[spec] * 5, out_specs=[spec] * 4,
        out_shape=[jax.ShapeDtypeStruct((n, r, c), F32)] * 4,
        compiler_params=_params(("parallel", "parallel")),
    )(w, ga, gb, m, v)


REPLICATED = ("norm_g", "conv_r_b", "lru_wa", "lru_ba", "lru_wx", "lru_bx", "lru_lambda", "gmlp_norm_g",
              "gmlp_ws", "gmlp_bs", "final_g")
CHIP_SHARDED_SMALL = ("conv_a_w", "conv_r_w")
PACK_LANES = LANES


def _pack(arrays):
    flat = jnp.concatenate([a.reshape(-1) for a in arrays])
    pad = (-flat.shape[0]) % (TILE_ROWS * PACK_LANES)
    return jnp.pad(flat, (0, pad)).reshape(-1, PACK_LANES)


def _unpack(packed, shapes):
    flat = packed.reshape(-1)
    out, off = [], 0
    for shp in shapes:
        n = math.prod(shp)
        out.append(flat[off:off + n].reshape(shp))
        off += n
    return out


def kernel(x, norm_g, w_in, conv_a_w, conv_r_w, conv_r_b, lru_wa, lru_ba, lru_wx, lru_bx, lru_lambda, gmlp_norm_g, gmlp_ws, gmlp_bs, w_out, final_g, loss_target, m_norm_g, m_w_in, m_conv_a_w, m_conv_r_w, m_conv_r_b, m_lru_wa, m_lru_ba, m_lru_wx, m_lru_bx, m_lru_lambda, m_gmlp_norm_g, m_gmlp_ws, m_gmlp_bs, m_w_out, m_final_g, v_norm_g, v_w_in, v_conv_a_w, v_conv_r_w, v_conv_r_b, v_lru_wa, v_lru_ba, v_lru_wx, v_lru_bx, v_lru_lambda, v_gmlp_norm_g, v_gmlp_ws, v_gmlp_bs, v_w_out, v_final_g):
    names = ("norm_g", "w_in", "conv_a_w", "conv_r_w", "conv_r_b", "lru_wa", "lru_ba", "lru_wx", "lru_bx",
             "lru_lambda", "gmlp_norm_g", "gmlp_ws", "gmlp_bs", "w_out", "final_g")
    w = dict(zip(names, (norm_g, w_in, conv_a_w, conv_r_w, conv_r_b, lru_wa, lru_ba, lru_wx, lru_bx, lru_lambda,
                         gmlp_norm_g, gmlp_ws, gmlp_bs, w_out, final_g)))
    m = dict(zip(names, (m_norm_g, m_w_in, m_conv_a_w, m_conv_r_w, m_conv_r_b, m_lru_wa, m_lru_ba, m_lru_wx, m_lru_bx,
                         m_lru_lambda, m_gmlp_norm_g, m_gmlp_ws, m_gmlp_bs, m_w_out, m_final_g)))
    v = dict(zip(names, (v_norm_g, v_w_in, v_conv_a_w, v_conv_r_w, v_conv_r_b, v_lru_wa, v_lru_ba, v_lru_wx, v_lru_bx,
                         v_lru_lambda, v_gmlp_norm_g, v_gmlp_ws, v_gmlp_bs, v_w_out, v_final_g)))
    depth = w_in.shape[0]
    out_rows = w_out.shape[1]
    conv_ch = conv_a_w.shape[2]
    chip = 2 * lax.axis_index("x") + lax.axis_index("y")

    taps = conv_a_w.shape[1] + conv_r_w.shape[1]
    w_in_t, m_w_in_t, v_w_in_t = (jnp.swapaxes(a, 1, 2) for a in (w_in, m_w_in, v_w_in))
    w_in_h, w_out_h = w_in_t.astype(BF16), w_out.astype(BF16)
    conv_own = jnp.concatenate([conv_a_w, conv_r_w], axis=1).reshape(depth * taps, conv_ch)
    gathers, token = [], None
    for l in range(depth):
        groups = [[w_in_h[l]], [w_out_h[l], conv_own]] if l == 0 else [[w_in_h[l], w_out_h[l]]]
        gathers.append([])
        for i, srcs in enumerate(groups):
            gathers[l].append(exchange_start(GATHER, srcs, token, f"gather_start_{l}_{i}"))
            token = gathers[l][-1][-1]
    p = dict(w)

    def with_own(land, own):
        return lax.dynamic_update_slice(land, own[None], (chip,) + (0,) * own.ndim)

    def layer_weights(l, x_l):
        lands = list(exchange_wait(GATHER, gathers[l][0], x_l, f"gather_wait_{l}_0"))
        w_in_l = with_own(lands[0], w_in_h[l]).reshape(D_IN, D_MODEL)
        gain = norm_g[l][None, :]
        if l == 0:
            gain = gain + token[0, 0]

        def rest(z_l):
            if l == 0:
                lands.extend(exchange_wait(GATHER, gathers[l][1], z_l, f"gather_wait_{l}_1"))
                conv = with_own(lands[2], conv_own).reshape(N_CHIPS, depth, taps, conv_ch)
                conv = conv.transpose(1, 2, 0, 3).reshape(depth, taps, GROUP_W)
                p["conv_a_w"] = conv[:, :conv_a_w.shape[1]]
                p["conv_r_w"] = conv[:, conv_a_w.shape[1]:]
                p["prepared"] = prepare_small_weights(p)
            return with_own(lands[1], w_out_h[l]).reshape(D_MIX, D_MODEL), p["prepared"]

        return gain, w_in_l, rest

    scatters, owns = [None] * depth, [None] * depth

    def projections_done(l, g_w_in_by_chip, g_w_in_own, g_w_out):
        go = g_w_out.reshape(N_CHIPS, out_rows, D_MODEL)
        owns[l] = (g_w_in_own, lax.dynamic_index_in_dim(go, chip, axis=0, keepdims=False))
        scatters[l] = exchange_start(SCATTER, [g_w_in_by_chip, go.astype(BF16)], None, f"scatter_start_{l}")
        return scatters[l][-1][0, 0]

    loss8, dx, grads = local_step(x[0], loss_target[0], final_g, depth, chip.reshape(1), layer_weights,
                                  projections_done)

    res = {}
    small = REPLICATED + CHIP_SHARDED_SMALL
    packed = _pack([grads[k] for k in small] + [loss8[0, :1]])
    total = all_reduce_small(packed)
    *sums, loss = _unpack(total, [grads[k].shape for k in small] + [()])
    gs = dict(zip(small, sums))
    for k in CHIP_SHARDED_SMALL:
        gs[k] = lax.dynamic_slice_in_dim(gs[k], chip * conv_ch, conv_ch, axis=2)
    as2d = lambda a: a[None] if a.ndim == 1 else a
    outs = adamw_small(*[[as2d(d[k]) for k in small] for d in (w, gs, m, v)])
    for k, (delta, m2, v2) in zip(small, outs):
        res[k] = [t.reshape(w[k].shape) for t in (gs[k], delta, m2, v2)]

    parts = [exchange_wait(SCATTER, scatters[l], total, f"scatter_wait_{l}") for l in range(depth)]
    p1 = sum_partials([owns[l][0] for l in range(depth)], [parts[l][0] for l in range(depth)])
    p2 = sum_partials([owns[l][1] for l in range(depth)], [parts[l][1] for l in range(depth)])
    q1, q2 = sibling_exchange(p1, p2)
    res["w_in"] = [jnp.swapaxes(t, 1, 2) for t in adamw(w_in_t, p1, q1, m_w_in_t, v_w_in_t)]
    res["w_out"] = adamw(w_out, p2, q2, m_w_out, v_w_out)

    return (loss, dx[None], *[res[k][0] for k in names], *[res[k][1] for k in names],
            *[res[k][2] for k in names], *[res[k][3] for k in names])
```

```python
import functools
import math

import jax
import jax.numpy as jnp
import numpy as np
from jax import lax
from jax.experimental import pallas as pl
from jax.experimental.pallas import tpu as pltpu

F32 = jnp.float32
BF16 = jnp.bfloat16

D_MODEL = 1024
GROUP_W = 256
N_HEADS = 4
HEAD_DIM = 64
N_CHUNKS = 13
D_IN = N_CHUNKS * GROUP_W
D_MIX = 4 * GROUP_W
NORM_EPS = 1e-6
RG_C = 8.0
GMLP_CHUNK = 128
ATTN_BLOCK = 128
PATTERN_DILS = (1, 4, 16)
ALIBI_SLOPES = tuple(2.0 ** (-8.0 * (h + 1) / N_HEADS) for h in range(N_HEADS))
ATTN_SCALE = 1.0 / math.sqrt(HEAD_DIM)
NEG_BIG = -1e30

ADAM_LR = 0.001
ADAM_B1 = 0.9
ADAM_B2 = 0.999
ADAM_EPS = 1e-08
ADAM_WD = 0.01
ADAM_STEP = 10

C_AX, C_AB, C_AC, C_AG, C_RX, C_RG, C_CU, C_CV, C_CG, C_DQ, C_DK, C_DV, C_DG = range(13)

SUBLANES = 8
LANES = 128
VMEM_LIMIT = 56 * 1024 * 1024
TILE_IN = 512
TILE_OUT = 1024
TILE_MIX = 512
TILE_DW = 1024
TILE_DW_OUT = 2048
ATTN_QB = 16
GELU_K0 = math.sqrt(2.0 / math.pi)
GELU_K1 = 0.044715


def _params(sem):
    return pltpu.CompilerParams(dimension_semantics=sem, vmem_limit_bytes=VMEM_LIMIT)


def _silu(x):
    return x * jax.nn.sigmoid(x)


def _dsilu(x):
    s = jax.nn.sigmoid(x)
    return s * (1.0 + x * (1.0 - s))


def _gelu(x):
    return 0.5 * x * (1.0 + jnp.tanh(GELU_K0 * (x + GELU_K1 * x * x * x)))


def _gelu_and_grad(x):
    t = jnp.tanh(GELU_K0 * (x + GELU_K1 * x * x * x))
    g = 0.5 * x * (1.0 + t)
    dg = 0.5 * (1.0 + t) + 0.5 * x * (1.0 - t * t) * GELU_K0 * (1.0 + 3.0 * GELU_K1 * x * x)
    return g, dg


def _neg_expm1(x):
    series = x * (1.0 + x * (0.5 + x * (1.0 / 6.0 + x * (1.0 / 24.0 + x * (1.0 / 120.0)))))
    return -jnp.where(x > -0.05, series, jnp.exp(x) - 1.0)


def _shift_down(v, halo, k):
    r = pltpu.roll(v, k, 0)
    rh = pltpu.roll(halo, k, 0)
    row = lax.broadcasted_iota(jnp.int32, halo.shape, 0)
    top = jnp.where(row < k, rh, r[:SUBLANES])
    return jnp.concatenate([top, r[SUBLANES:]], axis=0)


def _shift_up(v, halo, k):
    t = v.shape[0]
    r = pltpu.roll(v, t - k, 0)
    rh = pltpu.roll(halo, SUBLANES - k, 0)
    row = lax.broadcasted_iota(jnp.int32, halo.shape, 0)
    bot = jnp.where(row >= SUBLANES - k, rh, r[t - SUBLANES:])
    return jnp.concatenate([r[:t - SUBLANES], bot], axis=0)


def _scan_causal(a, b, h_in):
    t = a.shape[0]
    row8 = lax.broadcasted_iota(jnp.int32, a.shape, 0) % SUBLANES
    d = 1
    while d < SUBLANES:
        m = row8 >= d
        a_s = jnp.where(m, pltpu.roll(a, d, 0), 1.0)
        b_s = jnp.where(m, pltpu.roll(b, d, 0), 0.0)
        b = a * b_s + b
        a = a * a_s
        d *= 2
    out, carry = [], h_in
    for g in range(t // SUBLANES):
        sl = slice(g * SUBLANES, (g + 1) * SUBLANES)
        hg = b[sl] + a[sl] * carry
        out.append(hg)
        carry = hg[SUBLANES - 1:SUBLANES]
    return jnp.concatenate(out, axis=0)


def _scan_anticausal(a, b, g_in):
    t = a.shape[0]
    row8 = lax.broadcasted_iota(jnp.int32, a.shape, 0) % SUBLANES
    d = 1
    while d < SUBLANES:
        m = row8 < SUBLANES - d
        a_s = jnp.where(m, pltpu.roll(a, t - d, 0), 1.0)
        b_s = jnp.where(m, pltpu.roll(b, t - d, 0), 0.0)
        b = a * b_s + b
        a = a * a_s
        d *= 2
    out, carry = [], g_in
    for g in reversed(range(t // SUBLANES)):
        sl = slice(g * SUBLANES, (g + 1) * SUBLANES)
        gg = b[sl] + a[sl] * carry
        out.append(gg)
        carry = gg[0:1]
    return jnp.concatenate(out[::-1], axis=0)


def _head_of_lane(shape):
    return lax.broadcasted_iota(jnp.int32, shape, len(shape) - 1) // HEAD_DIM


def _per_head_lanes(cols):
    t = cols[0].shape[0]
    lane = lax.broadcasted_iota(jnp.int32, (t, LANES), 1)
    out = jnp.zeros((t, LANES), F32)
    for h, col in enumerate(cols):
        out = jnp.where(lane == h, col, out)
    return out


def _put_row(acc_shape, k, row_vec):
    row = lax.broadcasted_iota(jnp.int32, acc_shape, 0)
    return jnp.where(row == k, jnp.broadcast_to(row_vec, acc_shape), 0.0)


def _dot(a, b):
    return jnp.dot(a, b, preferred_element_type=F32)


def _dot_nt(a, b):
    return lax.dot_general(a, b, (((1,), (1,)), ((), ())), preferred_element_type=F32)


def _dot_tn(a, b):
    return lax.dot_general(a, b, (((0,), (0,)), ((), ())), preferred_element_type=F32)


def _deinterleave_store(val, stage, outs):
    t, c = val.shape
    for hh in range(c // LANES):
        stage[hh][...] = val[:, hh * LANES:(hh + 1) * LANES].astype(F32)
    for dil, ref in outs:
        for r in range(dil):
            for hh in range(c // LANES):
                ref[r, :, hh * LANES:(hh + 1) * LANES] = stage[hh][pl.ds(r, t // dil, stride=dil), :].astype(ref.dtype)


def _interleave_load(ref, dil, stage):
    _, n, c = ref.shape
    for r in range(dil):
        for hh in range(c // LANES):
            stage[hh][pl.ds(r, n, stride=dil), :] = ref[r, :, hh * LANES:(hh + 1) * LANES].astype(F32)
    return jnp.concatenate([stage[hh][...] for hh in range(c // LANES)], axis=1)


def _stage_scratch(tile, cols, copies):
    return [pltpu.VMEM((tile, LANES), F32)] * (copies * (cols // LANES))


def _by_residue(s, dil, cols, dtype):
    return jax.ShapeDtypeStruct((dil, s // dil, cols), dtype)


def _residue_block(dil, tile, cols):
    return pl.BlockSpec((dil, tile // dil, cols), lambda i: (0, i, 0))


def in_fwd(x, g, w):
    s = x.shape[0]
    qkv_w = 3 * GROUP_W

    def body(x_ref, g_ref, w_ref, z_ref, h_ref, qkv1_ref, qkv4_ref, qkv16_ref, *stage):
        xv = x_ref[...]
        rs = lax.rsqrt(jnp.mean(xv * xv, axis=-1, keepdims=True) + NORM_EPS)
        h = (xv * rs * g_ref[...]).astype(BF16)
        h_ref[...] = h
        z = _dot_nt(h, w_ref[...])
        z_ref[...] = z
        qkv = z[:, C_DQ * GROUP_W:(C_DV + 1) * GROUP_W]
        qkv1_ref[...] = qkv.astype(BF16)
        _deinterleave_store(qkv, stage, ((PATTERN_DILS[1], qkv4_ref), (PATTERN_DILS[2], qkv16_ref)))

    return pl.pallas_call(
        body, name="in_fwd", grid=(s // TILE_IN,),
        in_specs=[pl.BlockSpec((TILE_IN, D_MODEL), lambda i: (i, 0)),
                  pl.BlockSpec((1, D_MODEL), lambda i: (0, 0)),
                  pl.BlockSpec((D_IN, D_MODEL), lambda i: (0, 0))],
        out_specs=[pl.BlockSpec((TILE_IN, D_IN), lambda i: (i, 0)),
                   pl.BlockSpec((TILE_IN, D_MODEL), lambda i: (i, 0)),
                   pl.BlockSpec((TILE_IN, qkv_w), lambda i: (i, 0)),
                   _residue_block(PATTERN_DILS[1], TILE_IN, qkv_w),
                   _residue_block(PATTERN_DILS[2], TILE_IN, qkv_w)],
        out_shape=[jax.ShapeDtypeStruct((s, D_IN), F32), jax.ShapeDtypeStruct((s, D_MODEL), BF16),
                   jax.ShapeDtypeStruct((s, qkv_w), BF16),
                   _by_residue(s, PATTERN_DILS[1], qkv_w, BF16), _by_residue(s, PATTERN_DILS[2], qkv_w, BF16)],
        scratch_shapes=_stage_scratch(TILE_IN, qkv_w, 1),
        compiler_params=_params(("parallel",)),
    )(x, g, w)


def out_fwd(y, w, x):
    s = x.shape[0]

    def body(y_ref, w_ref, x_ref, o_ref):
        o_ref[...] = x_ref[...] + _dot(y_ref[...], w_ref[...])

    return pl.pallas_call(
        body, name="out_fwd", grid=(s // TILE_OUT,),
        in_specs=[pl.BlockSpec((TILE_OUT, D_MIX), lambda i: (i, 0)),
                  pl.BlockSpec((D_MIX, D_MODEL), lambda i: (0, 0)),
                  pl.BlockSpec((TILE_OUT, D_MODEL), lambda i: (i, 0))],
        out_specs=pl.BlockSpec((TILE_OUT, D_MODEL), lambda i: (i, 0)),
        out_shape=jax.ShapeDtypeStruct((s, D_MODEL), F32),
        compiler_params=_params(("parallel",)),
    )(y, w, x)


def out_bwd(dx, w, z, o):
    s = dx.shape[0]
    abc = 3 * GROUP_W

    def body(dx_ref, w_ref, dg_ref, o_ref, dy_ref, ddg_ref, do1_ref, do4_ref, do16_ref, dl1_ref, dl4_ref, dl16_ref,
             *stage):
        stage_a, stage_b = stage[:2], stage[2:]
        dy = _dot_nt(dx_ref[...].astype(BF16), w_ref[...])
        dy_ref[...] = dy[:, :abc]
        dyd = dy[:, abc:]
        head = _head_of_lane((TILE_OUT, GROUP_W))
        dg = dg_ref[...]
        o = o_ref[...]
        do = dyd * _silu(dg)
        ddg_ref[...] = dyd * o * _dsilu(dg)
        prod = do * o
        dl = _per_head_lanes([jnp.sum(jnp.where(head == h, prod, 0.0), axis=-1, keepdims=True)
                              for h in range(N_HEADS)])
        do1_ref[...] = do.astype(BF16)
        dl1_ref[...] = dl
        _deinterleave_store(do, stage_a, ((PATTERN_DILS[1], do4_ref), (PATTERN_DILS[2], do16_ref)))
        _deinterleave_store(dl, stage_b, ((PATTERN_DILS[1], dl4_ref), (PATTERN_DILS[2], dl16_ref)))

    row = pl.BlockSpec((TILE_OUT, GROUP_W), lambda i: (i, 0))
    r4 = _residue_block(PATTERN_DILS[1], TILE_OUT, GROUP_W)
    r16 = _residue_block(PATTERN_DILS[2], TILE_OUT, GROUP_W)
    crow = pl.BlockSpec((TILE_OUT, LANES), lambda i: (i, 0))
    c4 = _residue_block(PATTERN_DILS[1], TILE_OUT, LANES)
    c16 = _residue_block(PATTERN_DILS[2], TILE_OUT, LANES)
    return pl.pallas_call(
        body, name="out_bwd", grid=(s // TILE_OUT,),
        in_specs=[pl.BlockSpec((TILE_OUT, D_MODEL), lambda i: (i, 0)),
                  pl.BlockSpec((D_MIX, D_MODEL), lambda i: (0, 0)),
                  pl.BlockSpec((TILE_OUT, GROUP_W), lambda i: (i, C_DG)), row],
        out_specs=[pl.BlockSpec((TILE_OUT, abc), lambda i: (i, 0)), row, row, r4, r16, crow, c4, c16],
        out_shape=[jax.ShapeDtypeStruct((s, abc), F32), jax.ShapeDtypeStruct((s, GROUP_W), F32),
                   jax.ShapeDtypeStruct((s, GROUP_W), BF16),
                   _by_residue(s, PATTERN_DILS[1], GROUP_W, BF16), _by_residue(s, PATTERN_DILS[2], GROUP_W, BF16),
                   jax.ShapeDtypeStruct((s, LANES), F32),
                   _by_residue(s, PATTERN_DILS[1], LANES, F32), _by_residue(s, PATTERN_DILS[2], LANES, F32)],
        scratch_shapes=_stage_scratch(TILE_OUT, GROUP_W, 1) + _stage_scratch(TILE_OUT, LANES, 1),
        compiler_params=_params(("parallel",)),
    )(dx, w, z, o)


def in_bwd(dz, w, x, g, dx_next):
    s = x.shape[0]

    def body(dz_ref, w_ref, x_ref, g_ref, dxn_ref, dx_ref, dg_ref):
        @pl.when(pl.program_id(0) == 0)
        def _():
            dg_ref[...] = jnp.zeros_like(dg_ref)

        dh = _dot(dz_ref[...], w_ref[...])
        xv = x_ref[...]
        rs = lax.rsqrt(jnp.mean(xv * xv, axis=-1, keepdims=True) + NORM_EPS)
        xh = xv * rs
        dg_ref[...] += _put_row(dg_ref.shape, 0, jnp.sum(dh * xh, axis=0, keepdims=True))
        dn = dh * g_ref[...]
        dx_ref[...] = dxn_ref[...] + rs * (dn - xh * jnp.mean(dn * xh, axis=-1, keepdims=True))

    return pl.pallas_call(
        body, name="in_bwd", grid=(s // TILE_IN,),
        in_specs=[pl.BlockSpec((TILE_IN, D_IN), lambda i: (i, 0)),
                  pl.BlockSpec((D_IN, D_MODEL), lambda i: (0, 0)),
                  pl.BlockSpec((TILE_IN, D_MODEL), lambda i: (i, 0)),
                  pl.BlockSpec((1, D_MODEL), lambda i: (0, 0)),
                  pl.BlockSpec((TILE_IN, D_MODEL), lambda i: (i, 0))],
        out_specs=[pl.BlockSpec((TILE_IN, D_MODEL), lambda i: (i, 0)),
                   pl.BlockSpec((SUBLANES, D_MODEL), lambda i: (0, 0))],
        out_shape=[jax.ShapeDtypeStruct((s, D_MODEL), F32), jax.ShapeDtypeStruct((SUBLANES, D_MODEL), F32)],
        compiler_params=_params(("arbitrary",)),
    )(dz, w, x, g, dx_next)


def grad_w_out(y, dx):
    s = y.shape[0]

    def body(y_ref, dx_ref, o_ref):
        @pl.when(pl.program_id(0) == 0)
        def _():
            o_ref[...] = jnp.zeros_like(o_ref)

        o_ref[...] += _dot_tn(y_ref[...], dx_ref[...].astype(BF16))

    return pl.pallas_call(
        body, name="grad_w_out", grid=(s // TILE_DW_OUT,),
        in_specs=[pl.BlockSpec((TILE_DW_OUT, D_MIX), lambda k: (k, 0)),
                  pl.BlockSpec((TILE_DW_OUT, D_MODEL), lambda k: (k, 0))],
        out_specs=pl.BlockSpec((D_MIX, D_MODEL), lambda k: (0, 0)),
        out_shape=jax.ShapeDtypeStruct((D_MIX, D_MODEL), F32),
        compiler_params=_params(("arbitrary",)),
    )(y, dx)


def grad_w_in(h, dz, chip):
    s = h.shape[0]
    rows = D_IN // N_CHIPS

    def body(chip_ref, h_ref, dz_ref, staged_ref, own_ref, acc):
        k = pl.program_id(0)

        @pl.when(k == 0)
        def _():
            acc[...] = jnp.zeros_like(acc)

        acc[...] += _dot_tn(dz_ref[...], h_ref[...])

        @pl.when(k == s // TILE_DW - 1)
        def _():
            for j in range(N_CHIPS):
                part = acc[j * rows:(j + 1) * rows, :]
                staged_ref[j] = part.astype(BF16)

                @pl.when(chip_ref[0] == j)
                def _():
                    own_ref[...] = part

    return pl.pallas_call(
        body, name="grad_w_in",
        grid_spec=pltpu.PrefetchScalarGridSpec(
            num_scalar_prefetch=1, grid=(s // TILE_DW,),
            in_specs=[pl.BlockSpec((TILE_DW, D_MODEL), lambda k, c: (k, 0)),
                      pl.BlockSpec((TILE_DW, D_IN), lambda k, c: (k, 0))],
            out_specs=[pl.BlockSpec((N_CHIPS, rows, D_MODEL), lambda k, c: (0, 0, 0)),
                       pl.BlockSpec((rows, D_MODEL), lambda k, c: (0, 0))],
            scratch_shapes=[pltpu.VMEM((D_IN, D_MODEL), F32)]),
        out_shape=[jax.ShapeDtypeStruct((N_CHIPS, rows, D_MODEL), BF16), jax.ShapeDtypeStruct((rows, D_MODEL), F32)],
        compiler_params=_params(("arbitrary",)),
    )(chip, h, dz)


def out_fwd_loss(y, w, x, g, tgt):
    s = x.shape[0]

    def body(y_ref, w_ref, x_ref, g_ref, t_ref, l_ref, dx_ref, dg_ref):
        @pl.when(pl.program_id(0) == 0)
        def _():
            l_ref[...] = jnp.zeros_like(l_ref)
            dg_ref[...] = jnp.zeros_like(dg_ref)

        xv = x_ref[...] + _dot(y_ref[...], w_ref[...])
        gv = g_ref[...]
        rs = lax.rsqrt(jnp.mean(xv * xv, axis=-1, keepdims=True) + NORM_EPS)
        xh = xv * rs
        e = xh * gv - t_ref[...]
        part = 0.5 * jnp.sum(jnp.mean(e * e, axis=-1, keepdims=True), axis=0, keepdims=True)
        l_ref[...] += jnp.broadcast_to(part, l_ref.shape)
        dy = e * (1.0 / D_MODEL)
        dg_ref[...] += _put_row(dg_ref.shape, 0, jnp.sum(dy * xh, axis=0, keepdims=True))
        dn = dy * gv
        dx_ref[...] = rs * (dn - xh * jnp.mean(dn * xh, axis=-1, keepdims=True))

    return pl.pallas_call(
        body, name="out_fwd_loss", grid=(s // TILE_OUT,),
        in_specs=[pl.BlockSpec((TILE_OUT, D_MIX), lambda i: (i, 0)),
                  pl.BlockSpec((D_MIX, D_MODEL), lambda i: (0, 0)),
                  pl.BlockSpec((TILE_OUT, D_MODEL), lambda i: (i, 0)),
                  pl.BlockSpec((1, D_MODEL), lambda i: (0, 0)),
                  pl.BlockSpec((TILE_OUT, D_MODEL), lambda i: (i, 0))],
        out_specs=[pl.BlockSpec((SUBLANES, LANES), lambda i: (0, 0)),
                   pl.BlockSpec((TILE_OUT, D_MODEL), lambda i: (i, 0)),
                   pl.BlockSpec((SUBLANES, D_MODEL), lambda i: (0, 0))],
        out_shape=[jax.ShapeDtypeStruct((SUBLANES, LANES), F32), jax.ShapeDtypeStruct((s, D_MODEL), F32),
                   jax.ShapeDtypeStruct((SUBLANES, D_MODEL), F32)],
        compiler_params=_params(("arbitrary",)),
    )(y, w, x, g, tgt)


def _attn_bias(dil):
    qi = np.arange(ATTN_BLOCK)[:, None]
    ki = np.arange(2 * ATTN_BLOCK)[None, :]
    delta = qi + ATTN_BLOCK - ki
    band = (delta >= 0) & (delta <= ATTN_BLOCK)
    out = np.empty((2, N_HEADS, ATTN_BLOCK, 2 * ATTN_BLOCK), np.float32)
    for f in range(2):
        ok = band & ((ki >= ATTN_BLOCK) | (f == 0))
        for h in range(N_HEADS):
            out[f, h] = np.where(ok, -ALIBI_SLOPES[h] * dil * delta, NEG_BIG)
    return jnp.asarray(out.reshape(2, N_HEADS * ATTN_BLOCK, 2 * ATTN_BLOCK))


def _stack_heads(a, head):
    return jnp.concatenate([jnp.where(head == h, a, jnp.zeros_like(a)) for h in range(N_HEADS)], axis=0)


def _unstack_heads(a, head):
    out = a[:ATTN_BLOCK]
    for h in range(1, N_HEADS):
        out = jnp.where(head == h, a[h * ATTN_BLOCK:(h + 1) * ATTN_BLOCK], out)
    return out


def _head_column(a):
    return jnp.concatenate([a[:, h:h + 1] for h in range(N_HEADS)], axis=0)


def _attn_specs(n_blocks):
    rows = ATTN_QB * ATTN_BLOCK
    cur = lambda c, w=GROUP_W: pl.BlockSpec((rows, w), lambda n, c=c: (n, c))
    prev = lambda c: pl.BlockSpec((ATTN_BLOCK, GROUP_W), lambda n, c=c: (jnp.maximum(n * ATTN_QB - 1, 0), c))
    nxt = lambda c, w=GROUP_W: pl.BlockSpec((ATTN_BLOCK, w),
                                            lambda n, c=c: (jnp.minimum(n * ATTN_QB + ATTN_QB, n_blocks - 1), c))
    return cur, prev, nxt


def _keys(kp_ref, k_ref, j):
    prev = kp_ref[...] if j == 0 else k_ref[(j - 1) * ATTN_BLOCK:j * ATTN_BLOCK, :]
    return jnp.concatenate([prev, k_ref[j * ATTN_BLOCK:(j + 1) * ATTN_BLOCK, :]], axis=0)


def attn_fwd(qkv, dil):
    s = qkv.shape[0]
    n_blocks = s // ATTN_BLOCK
    bps = n_blocks // dil
    rows = ATTN_QB * ATTN_BLOCK

    def body(q_ref, kp_ref, k_ref, vp_ref, v_ref, bias_ref, o_ref, lse_ref):
        n = pl.program_id(0)
        head = _head_of_lane((ATTN_BLOCK, GROUP_W))
        for j in range(ATTN_QB):
            sl = slice(j * ATTN_BLOCK, (j + 1) * ATTN_BLOCK)
            first = (((n * ATTN_QB + j) % bps) == 0).astype(jnp.int32)
            qs = _stack_heads(q_ref[sl, :], head)
            sc = _dot_nt(qs, _keys(kp_ref, k_ref, j)) * ATTN_SCALE + bias_ref[first]
            m = jnp.max(sc, axis=-1, keepdims=True)
            pr = jnp.exp(sc - m)
            l = jnp.sum(pr, axis=-1, keepdims=True)
            oh = _dot(pr.astype(BF16), _keys(vp_ref, v_ref, j)) / l
            o_ref[sl, :] = _unstack_heads(oh, head).astype(BF16)
            lse_ref[sl, :] = _unstack_heads(jnp.broadcast_to(m + jnp.log(l), oh.shape), head)

    cur, prev, _ = _attn_specs(n_blocks)
    bias = _attn_bias(dil)
    return pl.pallas_call(
        body, name=f"attn_fwd_d{dil}", grid=(n_blocks // ATTN_QB,),
        in_specs=[cur(0), prev(1), cur(1), prev(2), cur(2), pl.BlockSpec(bias.shape, lambda n: (0, 0, 0))],
        out_specs=[cur(0), cur(0)],
        out_shape=[jax.ShapeDtypeStruct((s, GROUP_W), BF16), jax.ShapeDtypeStruct((s, GROUP_W), F32)],
        compiler_params=_params(("parallel",)),
    )(qkv, qkv, qkv, qkv, qkv, bias)


def attn_bwd(qkv, do, lse, dlt, dil):
    s = qkv.shape[0]
    n_blocks = s // ATTN_BLOCK
    bps = n_blocks // dil
    rows = ATTN_QB * ATTN_BLOCK

    def body(q_ref, qn_ref, kp_ref, k_ref, vp_ref, v_ref, do_ref, don_ref, lse_ref, lsen_ref, dl_ref, dln_ref,
             bias_ref, out_ref, dk_acc, dv_acc):
        n = pl.program_id(0)
        head = _head_of_lane((ATTN_BLOCK, GROUP_W))
        dk_acc[...] = jnp.zeros_like(dk_acc)
        dv_acc[...] = jnp.zeros_like(dv_acc)

        def pair(qj, doj, lsej, dlj, kk, vv, bias, keep):
            qs = _stack_heads(qj, head)
            dos = _stack_heads(doj, head)
            sc = _dot_nt(qs, kk) * ATTN_SCALE + bias
            if keep is None:
                pr = jnp.exp(sc - _head_column(lsej))
            else:
                pr = jnp.exp(jnp.minimum(sc - _head_column(lsej), 0.0)) * keep
            dp = _dot_nt(dos, vv)
            ds = (pr * (dp - _head_column(dlj)) * ATTN_SCALE).astype(BF16)
            return ds, _dot_tn(ds, qs), _dot_tn(pr.astype(BF16), dos)

        for j in range(ATTN_QB):
            sl = slice(j * ATTN_BLOCK, (j + 1) * ATTN_BLOCK)
            first = (((n * ATTN_QB + j) % bps) == 0).astype(jnp.int32)
            kk = _keys(kp_ref, k_ref, j)
            ds, dks, dvs = pair(q_ref[sl, :], do_ref[sl, :], lse_ref[sl, :], dl_ref[sl, :],
                                kk, _keys(vp_ref, v_ref, j), bias_ref[first], None)
            out_ref[sl, 0:GROUP_W] = _unstack_heads(_dot(ds, kk), head)
            acc = slice(j * ATTN_BLOCK, (j + 2) * ATTN_BLOCK)
            dk_acc[acc, :] += dks
            dv_acc[acc, :] += dvs

        nxt = n * ATTN_QB + ATTN_QB
        valid = ((nxt < n_blocks) & ((nxt % bps) != 0)).astype(F32)
        last = slice((ATTN_QB - 1) * ATTN_BLOCK, ATTN_QB * ATTN_BLOCK)
        _, dks, dvs = pair(qn_ref[...], don_ref[...], lsen_ref[...], dln_ref[...], k_ref[last, :], v_ref[last, :],
                           bias_ref[0][:, :ATTN_BLOCK], valid)
        acc = slice(ATTN_QB * ATTN_BLOCK, (ATTN_QB + 1) * ATTN_BLOCK)
        dk_acc[acc, :] += dks
        dv_acc[acc, :] += dvs
        out_ref[:, GROUP_W:2 * GROUP_W] = dk_acc[ATTN_BLOCK:, :]
        out_ref[:, 2 * GROUP_W:3 * GROUP_W] = dv_acc[ATTN_BLOCK:, :]

    cur, prev, nxt = _attn_specs(n_blocks)
    bias = _attn_bias(dil)
    return pl.pallas_call(
        body, name=f"attn_bwd_d{dil}", grid=(n_blocks // ATTN_QB,),
        in_specs=[cur(0), nxt(0), prev(1), cur(1), prev(2), cur(2), cur(0), nxt(0),
                  cur(0, LANES), nxt(0, LANES), cur(0, LANES), nxt(0, LANES),
                  pl.BlockSpec(bias.shape, lambda n: (0, 0, 0))],
        out_specs=pl.BlockSpec((rows, 3 * GROUP_W), lambda n: (n, 0)),
        out_shape=jax.ShapeDtypeStruct((s, 3 * GROUP_W), F32),
        scratch_shapes=[pltpu.VMEM(((ATTN_QB + 1) * ATTN_BLOCK, GROUP_W), F32),
                        pltpu.VMEM(((ATTN_QB + 1) * ATTN_BLOCK, GROUP_W), F32)],
        compiler_params=_params(("parallel",)),
    )(qkv, qkv, qkv, qkv, qkv, qkv, do, do, lse, lse, dlt, dlt, bias)


def _zcol(c):
    return pl.BlockSpec((TILE_MIX, GROUP_W), lambda i, c=c: (i, c))


def _zhalo(c):
    per = TILE_MIX // SUBLANES
    return pl.BlockSpec((SUBLANES, GROUP_W), lambda i, c=c: (jnp.maximum(i * per - 1, 0), c))


def _full(shape):
    return pl.BlockSpec(shape, lambda i: tuple(0 for _ in shape))


def _of_layer(a, l):
    rest = a.shape[1:]
    return pl.BlockSpec((None,) + rest, lambda i: (l,) + tuple(0 for _ in rest))


def _softplus_neg(lam):
    nl = -lam
    return jnp.maximum(nl, 0.0) + jnp.log1p(jnp.exp(-jnp.abs(nl)))


def _lru_gates(xb, wa_ref, wx_ref, ba, bx, lam):
    xbb = xb.astype(BF16)
    r = jax.nn.sigmoid(_dot(xbb, wa_ref[...]) + ba)
    ig = jax.nn.sigmoid(_dot(xbb, wx_ref[...]) + bx)
    log_a = (-RG_C * r) * _softplus_neg(lam)
    a = jnp.exp(log_a)
    mult = jnp.sqrt(_neg_expm1(2.0 * log_a))
    return r, ig, a, mult


LRU_SAVED = 5


def _gmlp_spatial(ws_ref, vvb, head):
    outs = []
    for j in range(vvb.shape[0] // GMLP_CHUNK):
        blk = vvb[j * GMLP_CHUNK:(j + 1) * GMLP_CHUNK, :]
        acc = jnp.zeros((GMLP_CHUNK, GROUP_W), F32)
        for h in range(N_HEADS):
            acc = jnp.where(head[:GMLP_CHUNK] == h, _dot(ws_ref[h], blk), acc)
        outs.append(acc)
    return jnp.concatenate(outs, axis=0)


def mix_fwd(z, attn, wts, l):
    s = z.shape[0]
    d4, d16 = PATTERN_DILS[1], PATTERN_DILS[2]

    def body(ax_ref, ab_ref, ac_ref, ag_ref, rx_ref, rg_ref, cu_ref, cv_ref, cg_ref, dg_ref,
             axh_ref, ach_ref, rxh_ref, o1_ref, l1_ref, o4_ref, l4_ref, o16_ref, l16_ref,
             caw_ref, crw_ref, crb_ref, wa_ref, wx_ref, ba_ref, bx_ref, lam_ref, gng_ref, ws_ref, bs_ref,
             y_ref, hl_ref, o_ref, lse_ref, lse4_ref, lse16_ref, lru_ref, carry, *stage):
        st_a, st_b, st_c, st_d, st_e = (stage[2 * k:2 * k + 2] for k in range(5))
        i = pl.program_id(0)

        @pl.when(i == 0)
        def _():
            carry[...] = jnp.zeros_like(carry)

        nz = (i > 0).astype(F32)
        head = _head_of_lane((TILE_MIX, GROUP_W))

        pa = ac_ref[...] * ax_ref[...]
        pah = ach_ref[...] * axh_ref[...] * nz
        cv = caw_ref[2:3, :] * pa + caw_ref[1:2, :] * _shift_down(pa, pah, 1) + caw_ref[0:1, :] * _shift_down(pa, pah, 2)
        y_ref[:, 0:GROUP_W] = (ab_ref[...] * cv * _silu(ag_ref[...])).astype(BF16)

        rx = rx_ref[...]
        rxh = rxh_ref[...] * nz
        xb = (crw_ref[3:4, :] * rx + crw_ref[2:3, :] * _shift_down(rx, rxh, 1) + crw_ref[1:2, :] * _shift_down(rx, rxh, 2)
              + crw_ref[0:1, :] * _shift_down(rx, rxh, 3) + crb_ref[...])
        r, ig, a, mult = _lru_gates(xb, wa_ref, wx_ref, ba_ref[...], bx_ref[...], lam_ref[...])
        for k, val in enumerate((xb, r, ig, a, mult)):
            lru_ref[:, k * GROUP_W:(k + 1) * GROUP_W] = val
        hl = _scan_causal(a, mult * (ig * xb), carry[SUBLANES - 1:SUBLANES, :])
        hl_ref[...] = hl
        carry[...] = hl[TILE_MIX - SUBLANES:, :]
        y_ref[:, GROUP_W:2 * GROUP_W] = (hl * _silu(rg_ref[...])).astype(BF16)

        u = _gelu(cu_ref[...])
        gv = _gelu(cv_ref[...])
        rs = lax.rsqrt(jnp.mean(gv * gv, axis=-1, keepdims=True) + NORM_EPS)
        vvb = (gv * rs * gng_ref[...]).astype(BF16)
        sp = _gmlp_spatial(ws_ref, vvb, head) + jnp.concatenate([bs_ref[...]] * (TILE_MIX // GMLP_CHUNK), axis=0)
        y_ref[:, 2 * GROUP_W:3 * GROUP_W] = (u * sp * _silu(cg_ref[...])).astype(BF16)

        ops = (o1_ref[...].astype(F32), _interleave_load(o4_ref, d4, st_a), _interleave_load(o16_ref, d16, st_b))
        lps = (l1_ref[...], _interleave_load(l4_ref, d4, st_c), _interleave_load(l16_ref, d16, st_d))
        m = jnp.maximum(jnp.maximum(lps[0], lps[1]), lps[2])
        zsum = jnp.zeros_like(m)
        o = jnp.zeros_like(m)
        for op, lp in zip(ops, lps):
            w = jnp.exp(lp - m)
            zsum = zsum + w
            o = o + w * op
        o = o / zsum
        lse = m + jnp.log(zsum)
        lse = _per_head_lanes([lse[:, h * HEAD_DIM:h * HEAD_DIM + 1] for h in range(N_HEADS)])
        o_ref[...] = o
        lse_ref[...] = lse
        _deinterleave_store(lse, st_e, ((d4, lse4_ref), (d16, lse16_ref)))
        y_ref[:, 3 * GROUP_W:4 * GROUP_W] = (o * _silu(dg_ref[...])).astype(BF16)

    row = pl.BlockSpec((TILE_MIX, GROUP_W), lambda i: (i, 0))
    r4 = _residue_block(d4, TILE_MIX, GROUP_W)
    r16 = _residue_block(d16, TILE_MIX, GROUP_W)
    crow = pl.BlockSpec((TILE_MIX, LANES), lambda i: (i, 0))
    c4 = _residue_block(d4, TILE_MIX, LANES)
    c16 = _residue_block(d16, TILE_MIX, LANES)
    names = ("caw", "crw", "crb", "wa", "wx", "ba", "bx", "lam", "gng", "ws", "bs")
    in_specs = ([_zcol(c) for c in (C_AX, C_AB, C_AC, C_AG, C_RX, C_RG, C_CU, C_CV, C_CG, C_DG)]
                + [_zhalo(C_AX), _zhalo(C_AC), _zhalo(C_RX), row, row, r4, r4, r16, r16]
                + [_of_layer(wts[k], l) for k in names])
    return pl.pallas_call(
        body, name="mix_fwd", grid=(s // TILE_MIX,),
        in_specs=in_specs,
        out_specs=[pl.BlockSpec((TILE_MIX, D_MIX), lambda i: (i, 0)), row, row, crow, c4, c16,
                   pl.BlockSpec((TILE_MIX, LRU_SAVED * GROUP_W), lambda i: (i, 0))],
        out_shape=([jax.ShapeDtypeStruct((s, D_MIX), BF16)] + [jax.ShapeDtypeStruct((s, GROUP_W), F32)] * 2
                   + [jax.ShapeDtypeStruct((s, LANES), F32), _by_residue(s, d4, LANES, F32),
                      _by_residue(s, d16, LANES, F32), jax.ShapeDtypeStruct((s, LRU_SAVED * GROUP_W), F32)]),
        scratch_shapes=([pltpu.VMEM((SUBLANES, GROUP_W), F32)] + _stage_scratch(TILE_MIX, GROUP_W, 4)
                        + _stage_scratch(TILE_MIX, LANES, 1)),
        compiler_params=_params(("arbitrary",)),
    )(*([z] * 13), *[a for pair in attn for a in pair], *[wts[k] for k in names])


def mix_bwd(dy, z, hl, lru, dqkv, ddg, wts, l):
    s = z.shape[0]
    d4, d16 = PATTERN_DILS[1], PATTERN_DILS[2]
    n_tiles = s // TILE_MIX

    def body(dya_ref, dyb_ref, dyc_ref, ax_ref, ab_ref, ac_ref, ag_ref, rx_ref, rg_ref, cu_ref, cv_ref, cg_ref,
             axh_ref, ach_ref, rxh_ref, hl_ref, hlh_ref, lru_ref, dqkv1_ref, dqkv4_ref, dqkv16_ref, ddg_ref,
             caw_ref, crw_ref, crb_ref, wa_ref, wx_ref, ba_ref, bx_ref, lam_ref, gng_ref, ws_ref, wst_ref, bs_ref,
             dz_ref, ga_ref, gr_ref, gn_ref, gwa_ref, gwx_ref, gws_ref, gbs_ref,
             c_dcv, c_g, c_a, c_dxb, *stage):
        st_a, st_b = stage[:len(stage) // 2], stage[len(stage) // 2:]
        step = pl.program_id(0)
        i = n_tiles - 1 - step

        @pl.when(step == 0)
        def _():
            for r in (c_dcv, c_g, c_a, c_dxb, ga_ref, gr_ref, gn_ref, gwa_ref, gwx_ref, gws_ref, gbs_ref):
                r[...] = jnp.zeros_like(r)

        nz = (i > 0).astype(F32)
        head = _head_of_lane((TILE_MIX, GROUP_W))
        shp8 = (SUBLANES, GROUP_W)
        colsum = lambda v: jnp.sum(v, axis=0, keepdims=True)

        ax, ab, ac, ag = ax_ref[...], ab_ref[...], ac_ref[...], ag_ref[...]
        dya = dya_ref[...]
        pa = ac * ax
        pah = ach_ref[...] * axh_ref[...] * nz
        pa1 = _shift_down(pa, pah, 1)
        pa2 = _shift_down(pa, pah, 2)
        cv = caw_ref[2:3, :] * pa + caw_ref[1:2, :] * pa1 + caw_ref[0:1, :] * pa2
        sg = _silu(ag)
        dz_ref[:, C_AB * GROUP_W:(C_AB + 1) * GROUP_W] = (dya * cv * sg).astype(BF16)
        dz_ref[:, C_AG * GROUP_W:(C_AG + 1) * GROUP_W] = (dya * ab * cv * _dsilu(ag)).astype(BF16)
        dcv = dya * ab * sg
        nxt = c_dcv[...]
        dpa = caw_ref[2:3, :] * dcv + caw_ref[1:2, :] * _shift_up(dcv, nxt, 1) + caw_ref[0:1, :] * _shift_up(dcv, nxt, 2)
        c_dcv[...] = dcv[:SUBLANES, :]
        dz_ref[:, C_AC * GROUP_W:(C_AC + 1) * GROUP_W] = (dpa * ax).astype(BF16)
        dz_ref[:, C_AX * GROUP_W:(C_AX + 1) * GROUP_W] = (dpa * ac).astype(BF16)
        ga_ref[...] += (_put_row(shp8, 2, colsum(dcv * pa)) + _put_row(shp8, 1, colsum(dcv * pa1))
                        + _put_row(shp8, 0, colsum(dcv * pa2)))

        rx, rg = rx_ref[...], rg_ref[...]
        dyb = dyb_ref[...]
        rxh = rxh_ref[...] * nz
        rx1, rx2, rx3 = _shift_down(rx, rxh, 1), _shift_down(rx, rxh, 2), _shift_down(rx, rxh, 3)
        xb, r, ig, a, mult = (lru_ref[:, k * GROUP_W:(k + 1) * GROUP_W] for k in range(LRU_SAVED))
        lam = lam_ref[...]
        sp = _softplus_neg(lam)
        hl = hl_ref[...]
        hprev = _shift_down(hl, hlh_ref[...] * nz, 1)
        dz_ref[:, C_RG * GROUP_W:(C_RG + 1) * GROUP_W] = (dyb * hl * _dsilu(rg)).astype(BF16)
        dh = dyb * _silu(rg)
        a_next = _shift_up(a, c_a[...], 1)
        g = _scan_anticausal(a_next, dh, c_g[0:1, :])
        c_g[...] = g[:SUBLANES, :]
        c_a[...] = a[:SUBLANES, :]
        u = ig * xb
        da = g * hprev
        dmult = g * u
        du = g * mult
        dlog_a = da * a - dmult * (a * a) / mult
        dr = dlog_a * (-RG_C * sp)
        dga = dr * r * (1.0 - r)
        dgx = (du * xb) * ig * (1.0 - ig)
        dgab, dgxb = dga.astype(BF16), dgx.astype(BF16)
        dxb = du * ig + _dot_nt(dgab, wa_ref[...]) + _dot_nt(dgxb, wx_ref[...])
        xbb = xb.astype(BF16)
        gwa_ref[...] += _dot_tn(xbb, dgab)
        gwx_ref[...] += _dot_tn(xbb, dgxb)
        nxt = c_dxb[...]
        drx = (crw_ref[3:4, :] * dxb + crw_ref[2:3, :] * _shift_up(dxb, nxt, 1) + crw_ref[1:2, :] * _shift_up(dxb, nxt, 2)
               + crw_ref[0:1, :] * _shift_up(dxb, nxt, 3))
        c_dxb[...] = dxb[:SUBLANES, :]
        dz_ref[:, C_RX * GROUP_W:(C_RX + 1) * GROUP_W] = drx.astype(BF16)
        dlam = colsum(dlog_a * (-RG_C * r)) * (-jax.nn.sigmoid(-lam))
        gr_ref[...] += (_put_row(shp8, 3, colsum(dxb * rx)) + _put_row(shp8, 2, colsum(dxb * rx1))
                        + _put_row(shp8, 1, colsum(dxb * rx2)) + _put_row(shp8, 0, colsum(dxb * rx3))
                        + _put_row(shp8, 4, colsum(dxb)) + _put_row(shp8, 5, colsum(dga))
                        + _put_row(shp8, 6, colsum(dgx)) + _put_row(shp8, 7, dlam))

        cu, cvv, cg = cu_ref[...], cv_ref[...], cg_ref[...]
        dyc = dyc_ref[...]
        u_c, du_c = _gelu_and_grad(cu)
        gv, dgv_c = _gelu_and_grad(cvv)
        rs = lax.rsqrt(jnp.mean(gv * gv, axis=-1, keepdims=True) + NORM_EPS)
        vh = gv * rs
        gng = gng_ref[...]
        vvb = (vh * gng).astype(BF16)
        spat = _gmlp_spatial(ws_ref, vvb, head) + jnp.concatenate([bs_ref[...]] * (TILE_MIX // GMLP_CHUNK), axis=0)
        sgc = _silu(cg)
        dz_ref[:, C_CU * GROUP_W:(C_CU + 1) * GROUP_W] = (dyc * spat * sgc * du_c).astype(BF16)
        dz_ref[:, C_CG * GROUP_W:(C_CG + 1) * GROUP_W] = (dyc * u_c * spat * _dsilu(cg)).astype(BF16)
        dsp = dyc * u_c * sgc
        dspb = dsp.astype(BF16)
        tril = (lax.broadcasted_iota(jnp.int32, (GMLP_CHUNK, GMLP_CHUNK), 0)
                >= lax.broadcasted_iota(jnp.int32, (GMLP_CHUNK, GMLP_CHUNK), 1))
        head_c = head[:GMLP_CHUNK]
        dvv_parts = []
        gbs = jnp.zeros((GMLP_CHUNK, GROUP_W), F32)
        for j in range(TILE_MIX // GMLP_CHUNK):
            sl = slice(j * GMLP_CHUNK, (j + 1) * GMLP_CHUNK)
            dblk = dspb[sl, :]
            vblk = vvb[sl, :]
            gbs = gbs + dsp[sl, :]
            acc = jnp.zeros((GMLP_CHUNK, GROUP_W), F32)
            for h in range(N_HEADS):
                acc = jnp.where(head_c == h, _dot(wst_ref[h], dblk), acc)
                dm = jnp.where(head_c == h, dblk, jnp.zeros_like(dblk))
                gws_ref[h] += jnp.where(tril, _dot_nt(dm, vblk), 0.0)
            dvv_parts.append(acc)
        gbs_ref[...] += gbs
        dvv = jnp.concatenate(dvv_parts, axis=0)
        gn_ref[...] += _put_row(shp8, 0, colsum(dvv * vh))
        dvh = dvv * gng
        dgv = rs * (dvh - vh * jnp.mean(dvh * vh, axis=-1, keepdims=True))
        dz_ref[:, C_CV * GROUP_W:(C_CV + 1) * GROUP_W] = (dgv * dgv_c).astype(BF16)

        dsum = dqkv1_ref[...] + _interleave_load(dqkv4_ref, d4, st_a) + _interleave_load(dqkv16_ref, d16, st_b)
        dz_ref[:, C_DQ * GROUP_W:(C_DV + 1) * GROUP_W] = dsum.astype(BF16)
        dz_ref[:, C_DG * GROUP_W:(C_DG + 1) * GROUP_W] = ddg_ref[...].astype(BF16)

    per = TILE_MIX // SUBLANES
    qkv_w = 3 * GROUP_W
    rev = lambda c: pl.BlockSpec((TILE_MIX, GROUP_W), lambda t, c=c: (n_tiles - 1 - t, c))
    revh = lambda c: pl.BlockSpec((SUBLANES, GROUP_W),
                                  lambda t, c=c: (jnp.maximum((n_tiles - 1 - t) * per - 1, 0), c))
    revr = lambda dil: pl.BlockSpec((dil, TILE_MIX // dil, qkv_w), lambda t: (0, n_tiles - 1 - t, 0))
    names = ("caw", "crw", "crb", "wa", "wx", "ba", "bx", "lam", "gng", "ws", "wst", "bs")
    in_specs = ([rev(0), rev(1), rev(2)]
                + [rev(c) for c in (C_AX, C_AB, C_AC, C_AG, C_RX, C_RG, C_CU, C_CV, C_CG)]
                + [revh(C_AX), revh(C_AC), revh(C_RX), rev(0), revh(0),
                   pl.BlockSpec((TILE_MIX, LRU_SAVED * GROUP_W), lambda t: (n_tiles - 1 - t, 0)),
                   pl.BlockSpec((TILE_MIX, qkv_w), lambda t: (n_tiles - 1 - t, 0)), revr(d4), revr(d16), rev(0)]
                + [_of_layer(wts[k], l) for k in names])
    small = jax.ShapeDtypeStruct((SUBLANES, GROUP_W), F32)
    sq = jax.ShapeDtypeStruct((GROUP_W, GROUP_W), F32)
    out_shape = [jax.ShapeDtypeStruct((s, D_IN), BF16), small, small, small, sq, sq,
                 jax.ShapeDtypeStruct((N_HEADS, GMLP_CHUNK, GMLP_CHUNK), F32),
                 jax.ShapeDtypeStruct((GMLP_CHUNK, GROUP_W), F32)]
    out_specs = ([pl.BlockSpec((TILE_MIX, D_IN), lambda t: (n_tiles - 1 - t, 0))]
                 + [_full(o.shape) for o in out_shape[1:]])
    return pl.pallas_call(
        body, name="mix_bwd", grid=(n_tiles,),
        in_specs=in_specs, out_specs=out_specs, out_shape=out_shape,
        scratch_shapes=[pltpu.VMEM((SUBLANES, GROUP_W), F32)] * 4 + _stage_scratch(TILE_MIX, qkv_w, 2),
        compiler_params=_params(("arbitrary",)),
    )(dy, dy, dy, *([z] * 12), hl, hl, lru, *dqkv, ddg, *[wts[k] for k in names])


def prepare_small_weights(p):
    tril = jnp.tril(jnp.ones((GMLP_CHUNK, GMLP_CHUNK), dtype=bool))
    ws = jnp.where(tril, p["gmlp_ws"], 0.0).astype(BF16)
    row = lambda a: a[:, None, :]
    eye = jnp.eye(N_HEADS, dtype=F32)
    bd = lambda w: (w[:, :, :, None, :] * eye[None, :, None, :, None]).reshape(-1, GROUP_W, GROUP_W).astype(BF16)
    return dict(
        caw=p["conv_a_w"], crw=p["conv_r_w"], crb=row(p["conv_r_b"]),
        wa=bd(p["lru_wa"]), wx=bd(p["lru_wx"]),
        ba=row(p["lru_ba"]), bx=row(p["lru_bx"]), lam=row(p["lru_lambda"]), gng=row(p["gmlp_norm_g"]),
        ws=ws, wst=jnp.swapaxes(ws, 2, 3),
        bs=jnp.repeat(jnp.swapaxes(p["gmlp_bs"], 1, 2), HEAD_DIM, axis=2))


def _flat(a):
    return a.reshape(a.shape[0] * a.shape[1], a.shape[2])


def _split(a, dil):
    return a.reshape(dil, a.shape[0] // dil, a.shape[1])


def local_step(x, tgt, final_g, depth, chip, layer_weights, projections_done):
    saved = []
    for l in range(depth):
        gain, w_in_l, rest = layer_weights(l, x)
        z, h, *qkvs = in_fwd(x, gain, w_in_l)
        w_out_l, wts = rest(z)
        qkvs = [_flat(q) if q.ndim == 3 else q for q in qkvs]
        attn = []
        for q, d in zip(qkvs, PATTERN_DILS):
            o_p, lse_p = attn_fwd(q, d)
            attn.append((o_p, lse_p) if d == 1 else (_split(o_p, d), _split(lse_p, d)))
        y, hl, o, lse, lse4, lse16, lru = mix_fwd(z, attn, wts, l)
        saved.append(dict(x=x, z=z, h=h, y=y, hl=hl, o=o, qkvs=qkvs, lses=(lse, _flat(lse4), _flat(lse16)), wts=wts, lru=lru,
                          gain=gain, w_in=w_in_l, w_out=w_out_l))
        if l < depth - 1:
            x = out_fwd(y, w_out_l, x)
        else:
            loss, dx, dfg = out_fwd_loss(y, w_out_l, x, final_g[None, :], tgt)
    raw = {k: [None] * depth for k in ("gain", "a", "r", "n", "wa", "wx", "ws", "bs")}
    zero = None
    for l in reversed(range(depth)):
        sv = saved[l]
        dy, ddg, do1, do4, do16, dl1, dl4, dl16 = out_bwd(dx, sv["w_out"], sv["z"], sv["o"])
        g_w_out = grad_w_out(sv["y"], dx)
        dqkv = []
        for q, do, lse, dl, d in zip(sv["qkvs"], (do1, _flat(do4), _flat(do16)), sv["lses"],
                                     (dl1, _flat(dl4), _flat(dl16)), PATTERN_DILS):
            g = attn_bwd(q, do, lse, dl, d)
            dqkv.append(g if d == 1 else _split(g, d))
        dz, ga, gr, gn, gwa, gwx, gws, gbs = mix_bwd(dy, sv["z"], sv["hl"], sv["lru"], dqkv, ddg, sv["wts"], l)
        gain = sv["gain"] if zero is None else sv["gain"] + zero
        zero = projections_done(l, *grad_w_in(sv["h"], dz, chip), g_w_out)
        if l == 0 and zero is not None:
            gain = gain + zero
        dx, dgn = in_bwd(dz, sv["w_in"], sv["x"], gain, dx)
        for k, g in zip(("gain", "a", "r", "n", "wa", "wx", "ws", "bs"), (dgn, ga, gr, gn, gwa, gwx, gws, gbs)):
            raw[k][l] = g
    st = {k: jnp.stack(v) for k, v in raw.items()}
    eye = jnp.eye(N_HEADS, dtype=F32)[None, :, None, :, None]
    diag = lambda g: (g.reshape(depth, N_HEADS, HEAD_DIM, N_HEADS, HEAD_DIM) * eye).sum(axis=3)
    grads = dict(
        norm_g=st["gain"][:, 0], conv_a_w=st["a"][:, :3], conv_r_w=st["r"][:, :4], conv_r_b=st["r"][:, 4],
        lru_ba=st["r"][:, 5], lru_bx=st["r"][:, 6], lru_lambda=st["r"][:, 7], gmlp_norm_g=st["n"][:, 0],
        lru_wa=diag(st["wa"]), lru_wx=diag(st["wx"]), gmlp_ws=st["ws"],
        gmlp_bs=jnp.swapaxes(st["bs"].reshape(depth, GMLP_CHUNK, N_HEADS, HEAD_DIM).sum(-1), 1, 2),
        final_g=dfg[0])
    return loss, dx, grads


MESH = pl.DeviceIdType.MESH
N_CHIPS = 4
N_DEV = 8
ANY = pl.BlockSpec(memory_space=pl.ANY)


def _place():
    x, y, c = lax.axis_index("x"), lax.axis_index("y"), lax.axis_index("c")
    chips = [(1 - x, y), (x, 1 - y), (1 - x, 1 - y)]
    return x, y, c, chips


def _remote(src, dst, ssem, rsem, to):
    return pltpu.make_async_remote_copy(src_ref=src, dst_ref=dst, send_sem=ssem, recv_sem=rsem,
                                        device_id=to, device_id_type=MESH)


HBM = pl.BlockSpec(memory_space=pltpu.HBM)
SEM = pl.BlockSpec(memory_space=pltpu.SEMAPHORE)
DATAFLOW = pltpu.SideEffectType.DATAFLOW_SIDE_EFFECTING
GATHER, SCATTER = "gather", "scatter"


def _chip_copies(mode, src_refs, land_refs, ssem, rsem):
    x, y, c, chips = _place()
    me = 2 * x + y
    n = len(src_refs)
    copies = []
    for k, (cx, cy) in enumerate(chips):
        for a in range(n):
            if mode == GATHER:
                src, dst = src_refs[a], land_refs[a].at[me]
            else:
                src, dst = src_refs[a].at[2 * cx + cy], land_refs[a].at[k]
            copies.append(_remote(src, dst, ssem.at[n * k + a], rsem.at[n * k + a], (cx, cy, c)))
    return copies


def exchange_start(mode, srcs, after, name):
    n = len(srcs)
    if mode == GATHER:
        lands = [lax.empty((N_CHIPS,) + s.shape, s.dtype) for s in srcs]
    else:
        lands = [lax.empty((N_CHIPS - 1,) + s.shape[1:], s.dtype) for s in srcs]
    extra = [] if after is None else [after]

    def body(*refs):
        src_refs, land_refs = refs[:n], refs[n:2 * n]
        ssem, rsem = refs[2 * n + len(extra)], refs[2 * n + len(extra) + 1]
        token = refs[-1]
        for cp in _chip_copies(mode, src_refs, land_refs, ssem, rsem):
            cp.start()
        token[...] = jnp.zeros_like(token)

    arrays = list(srcs) + lands
    return pl.pallas_call(
        body, name=name,
        out_shape=(pltpu.SemaphoreType.DMA((3 * n,)), pltpu.SemaphoreType.DMA((3 * n,)),
                   *[pltpu.HBM(a.shape, a.dtype) for a in arrays], jax.ShapeDtypeStruct((SUBLANES, LANES), F32)),
        in_specs=[HBM] * (2 * n) + [ANY] * len(extra),
        out_specs=(SEM, SEM, *[HBM] * (2 * n), pl.BlockSpec(memory_space=pltpu.VMEM)),
        input_output_aliases={i: 2 + i for i in range(2 * n)},
        compiler_params=pltpu.CompilerParams(has_side_effects=DATAFLOW),
    )(*[pltpu.with_memory_space_constraint(a, pltpu.HBM) for a in arrays], *extra)


def exchange_wait(mode, started, after, name):
    ssem, rsem, *thru, _ = started
    n = len(thru) // 2

    def body(*refs):
        src_refs, land_refs = refs[:n], refs[n:2 * n]
        ssem_ref, rsem_ref = refs[2 * n], refs[2 * n + 1]
        for cp in _chip_copies(mode, src_refs, land_refs, ssem_ref, rsem_ref):
            cp.wait_send()
            cp.wait_recv()

    outs = pl.pallas_call(
        body, name=name,
        out_shape=[pltpu.HBM(a.shape, a.dtype) for a in thru],
        in_specs=[HBM] * (2 * n) + [SEM, SEM, ANY],
        out_specs=[HBM] * (2 * n),
        input_output_aliases={i: i for i in range(2 * n)},
        compiler_params=pltpu.CompilerParams(has_side_effects=DATAFLOW),
    )(*thru, ssem, rsem, after)
    return outs[n:]


def sibling_exchange(p1, p2):
    def body(p1_ref, p2_ref, q1_ref, q2_ref, ssem, rsem):
        x, y, c, _ = _place()
        copies = [_remote(p_ref, q_ref, ssem.at[a], rsem.at[a], (x, y, 1 - c))
                  for a, (p_ref, q_ref) in enumerate(((p1_ref, q1_ref), (p2_ref, q2_ref)))]
        for cp in copies:
            cp.start()
        for cp in copies:
            cp.wait()

    return pl.pallas_call(
        body, name="sibling_exchange",
        in_specs=[ANY, ANY], out_specs=[ANY, ANY],
        out_shape=[jax.ShapeDtypeStruct(p.shape, p.dtype) for p in (p1, p2)],
        scratch_shapes=[pltpu.SemaphoreType.DMA((2,)), pltpu.SemaphoreType.DMA((2,))],
    )(p1, p2)


def all_reduce_small(v):
    r, n = v.shape
    piece = r // N_DEV

    def body(x_ref, out_ref, recv, ssem1, rsem1, ssem2, rsem2):
        x, y, c, _ = _place()
        me = 4 * x + 2 * y + c

        def peer(k):
            px = 1 - x if (k >> 2) & 1 else x
            py = 1 - y if (k >> 1) & 1 else y
            pc = 1 - c if k & 1 else c
            return (px, py, pc), 4 * px + 2 * py + pc

        def rows(ref, d):
            return ref.at[pl.ds(d * piece, piece), :]

        scatter = []
        for k in range(1, N_DEV):
            to, idx = peer(k)
            scatter.append(_remote(rows(x_ref, idx), recv.at[k], ssem1.at[k - 1], rsem1.at[k - 1], to))
            scatter[-1].start()
        acc = rows(x_ref, me)[...]
        for k in range(1, N_DEV):
            scatter[k - 1].wait_recv()
            acc = acc + recv[k]
        rows(out_ref, me)[...] = acc

        gather = []
        for k in range(1, N_DEV):
            to, _ = peer(k)
            gather.append(_remote(rows(out_ref, me), rows(out_ref, me), ssem2.at[k - 1], rsem2.at[k - 1], to))
            gather[-1].start()
        for k in range(1, N_DEV):
            to, idx = peer(k)
            _remote(rows(out_ref, idx), rows(out_ref, idx), ssem2.at[k - 1], rsem2.at[k - 1], to).wait_recv()
        for cp in scatter + gather:
            cp.wait_send()

    return pl.pallas_call(
        body, name="all_reduce_small",
        out_shape=jax.ShapeDtypeStruct((r, n), v.dtype),
        in_specs=[pl.BlockSpec(memory_space=pltpu.VMEM)],
        out_specs=pl.BlockSpec(memory_space=pltpu.VMEM),
        scratch_shapes=[pltpu.VMEM((N_DEV, piece, n), v.dtype)] + [pltpu.SemaphoreType.DMA((N_DEV - 1,))] * 4,
        compiler_params=pltpu.CompilerParams(vmem_limit_bytes=VMEM_LIMIT),
    )(v)


TILE_ROWS = 256


def _row_tile(r):
    return max(t for t in range(SUBLANES, TILE_ROWS + 1, SUBLANES) if r % t == 0)


def sum_partials(owns, parts):
    k, r, c = parts[0].shape
    depth = len(owns)
    tile = _row_tile(r)

    def body(buf_ref, o_ref, p_ref, out_ref):
        acc = o_ref[...]
        for i in range(k):
            acc = acc + p_ref[i].astype(F32)
        out_ref[0] = acc

    out = lax.empty((depth, r, c), F32)
    for l in range(depth):
        out = pl.pallas_call(
            functools.partial(body), name="sum_partials", grid=(r // tile,),
            in_specs=[ANY, pl.BlockSpec((tile, c), lambda i: (i, 0)), pl.BlockSpec((k, tile, c), lambda i: (0, i, 0))],
            out_specs=pl.BlockSpec((1, tile, c), lambda i, l=l: (l, i, 0)),
            out_shape=jax.ShapeDtypeStruct((depth, r, c), F32),
            input_output_aliases={0: 0},
            compiler_params=_params(("parallel",)),
        )(out, owns[l], parts[l])
    return out


def _adamw_update(w, g, m, v):
    m2 = ADAM_B1 * m + (1.0 - ADAM_B1) * g
    v2 = ADAM_B2 * v + (1.0 - ADAM_B2) * (g * g)
    m_hat = m2 / (1.0 - ADAM_B1 ** ADAM_STEP)
    v_hat = v2 / (1.0 - ADAM_B2 ** ADAM_STEP)
    return -ADAM_LR * (m_hat / (jnp.sqrt(v_hat) + ADAM_EPS) + ADAM_WD * w), m2, v2


def adamw_small(ws, gs, ms, vs):
    n = len(ws)

    def body(*refs):
        ins, outs = refs[:4 * n], refs[4 * n:]
        for i in range(n):
            d, m2, v2 = _adamw_update(ins[i][...], ins[n + i][...], ins[2 * n + i][...], ins[3 * n + i][...])
            outs[3 * i][...] = d
            outs[3 * i + 1][...] = m2
            outs[3 * i + 2][...] = v2

    outs = pl.pallas_call(
        body, name="adamw_small",
        out_shape=[jax.ShapeDtypeStruct(w.shape, F32) for w in ws for _ in range(3)],
    )(*ws, *gs, *ms, *vs)
    return [tuple(outs[3 * i:3 * i + 3]) for i in range(n)]


def adamw(w, ga, gb, m, v):
    n, r, c = w.shape
    tile = _row_tile(r)

    def body(w_ref, ga_ref, gb_ref, m_ref, v_ref, g_ref, d_ref, m2_ref, v2_ref):
        g = ga_ref[...] + gb_ref[...]
        g_ref[...] = g
        d_ref[...], m2_ref[...], v2_ref[...] = _adamw_update(w_ref[...], g, m_ref[...], v_ref[...])

    spec = pl.BlockSpec((1, tile, c), lambda j, i: (j, i, 0))
    return pl.pallas_call(
        body, name="adamw", grid=(n, r // tile),
        in_specs=[spec] * 5, out_specs=[spec] * 4,
        out_shape=[jax.ShapeDtypeStruct((n, r, c), F32)] * 4,
        compiler_params=_params(("parallel", "parallel")),
    )(w, ga, gb, m, v)


REPLICATED = ("norm_g", "conv_r_b", "lru_wa", "lru_ba", "lru_wx", "lru_bx", "lru_lambda", "gmlp_norm_g",
              "gmlp_ws", "gmlp_bs", "final_g")
CHIP_SHARDED_SMALL = ("conv_a_w", "conv_r_w")
PACK_LANES = LANES


def _pack(arrays):
    flat = jnp.concatenate([a.reshape(-1) for a in arrays])
    pad = (-flat.shape[0]) % (TILE_ROWS * PACK_LANES)
    return jnp.pad(flat, (0, pad)).reshape(-1, PACK_LANES)


def _unpack(packed, shapes):
    flat = packed.reshape(-1)
    out, off = [], 0
    for shp in shapes:
        n = math.prod(shp)
        out.append(flat[off:off + n].reshape(shp))
        off += n
    return out


def kernel(x, norm_g, w_in, conv_a_w, conv_r_w, conv_r_b, lru_wa, lru_ba, lru_wx, lru_bx, lru_lambda, gmlp_norm_g, gmlp_ws, gmlp_bs, w_out, final_g, loss_target, m_norm_g, m_w_in, m_conv_a_w, m_conv_r_w, m_conv_r_b, m_lru_wa, m_lru_ba, m_lru_wx, m_lru_bx, m_lru_lambda, m_gmlp_norm_g, m_gmlp_ws, m_gmlp_bs, m_w_out, m_final_g, v_norm_g, v_w_in, v_conv_a_w, v_conv_r_w, v_conv_r_b, v_lru_wa, v_lru_ba, v_lru_wx, v_lru_bx, v_lru_lambda, v_gmlp_norm_g, v_gmlp_ws, v_gmlp_bs, v_w_out, v_final_g):
    names = ("norm_g", "w_in", "conv_a_w", "conv_r_w", "conv_r_b", "lru_wa", "lru_ba", "lru_wx", "lru_bx",
             "lru_lambda", "gmlp_norm_g", "gmlp_ws", "gmlp_bs", "w_out", "final_g")
    w = dict(zip(names, (norm_g, w_in, conv_a_w, conv_r_w, conv_r_b, lru_wa, lru_ba, lru_wx, lru_bx, lru_lambda,
                         gmlp_norm_g, gmlp_ws, gmlp_bs, w_out, final_g)))
    m = dict(zip(names, (m_norm_g, m_w_in, m_conv_a_w, m_conv_r_w, m_conv_r_b, m_lru_wa, m_lru_ba, m_lru_wx, m_lru_bx,
                         m_lru_lambda, m_gmlp_norm_g, m_gmlp_ws, m_gmlp_bs, m_w_out, m_final_g)))
    v = dict(zip(names, (v_norm_g, v_w_in, v_conv_a_w, v_conv_r_w, v_conv_r_b, v_lru_wa, v_lru_ba, v_lru_wx, v_lru_bx,
                         v_lru_lambda, v_gmlp_norm_g, v_gmlp_ws, v_gmlp_bs, v_w_out, v_final_g)))
    depth = w_in.shape[0]
    out_rows = w_out.shape[1]
    conv_ch = conv_a_w.shape[2]
    chip = 2 * lax.axis_index("x") + lax.axis_index("y")

    taps = conv_a_w.shape[1] + conv_r_w.shape[1]
    w_in_t, m_w_in_t, v_w_in_t = (jnp.swapaxes(a, 1, 2) for a in (w_in, m_w_in, v_w_in))
    w_in_h, w_out_h = w_in_t.astype(BF16), w_out.astype(BF16)
    conv_own = jnp.concatenate([conv_a_w, conv_r_w], axis=1).reshape(depth * taps, conv_ch)
    gathers, token = [], None
    for l in range(depth):
        groups = [[w_in_h[l]], [w_out_h[l], conv_own]] if l == 0 else [[w_in_h[l], w_out_h[l]]]
        gathers.append([])
        for i, srcs in enumerate(groups):
            gathers[l].append(exchange_start(GATHER, srcs, token, f"gather_start_{l}_{i}"))
            token = gathers[l][-1][-1]
    p = dict(w)

    def with_own(land, own):
        return lax.dynamic_update_slice(land, own[None], (chip,) + (0,) * own.ndim)

    def layer_weights(l, x_l):
        lands = list(exchange_wait(GATHER, gathers[l][0], x_l, f"gather_wait_{l}_0"))
        w_in_l = with_own(lands[0], w_in_h[l]).reshape(D_IN, D_MODEL)
        gain = norm_g[l][None, :]
        if l == 0:
            gain = gain + token[0, 0]

        def rest(z_l):
            if l == 0:
                lands.extend(exchange_wait(GATHER, gathers[l][1], z_l, f"gather_wait_{l}_1"))
                conv = with_own(lands[2], conv_own).reshape(N_CHIPS, depth, taps, conv_ch)
                conv = conv.transpose(1, 2, 0, 3).reshape(depth, taps, GROUP_W)
                p["conv_a_w"] = conv[:, :conv_a_w.shape[1]]
                p["conv_r_w"] = conv[:, conv_a_w.shape[1]:]
                p["prepared"] = prepare_small_weights(p)
            return with_own(lands[1], w_out_h[l]).reshape(D_MIX, D_MODEL), p["prepared"]

        return gain, w_in_l, rest

    scatters, owns = [None] * depth, [None] * depth

    def projections_done(l, g_w_in_by_chip, g_w_in_own, g_w_out):
        go = g_w_out.reshape(N_CHIPS, out_rows, D_MODEL)
        owns[l] = (g_w_in_own, lax.dynamic_index_in_dim(go, chip, axis=0, keepdims=False))
        scatters[l] = exchange_start(SCATTER, [g_w_in_by_chip, go.astype(BF16)], None, f"scatter_start_{l}")
        return scatters[l][-1][0, 0]

    loss8, dx, grads = local_step(x[0], loss_target[0], final_g, depth, chip.reshape(1), layer_weights,
                                  projections_done)

    res = {}
    small = REPLICATED + CHIP_SHARDED_SMALL
    packed = _pack([grads[k] for k in small] + [loss8[0, :1]])
    total = all_reduce_small(packed)
    *sums, loss = _unpack(total, [grads[k].shape for k in small] + [()])
    gs = dict(zip(small, sums))
    for k in CHIP_SHARDED_SMALL:
        gs[k] = lax.dynamic_slice_in_dim(gs[k], chip * conv_ch, conv_ch, axis=2)
    as2d = lambda a: a[None] if a.ndim == 1 else a
    outs = adamw_small(*[[as2d(d[k]) for k in small] for d in (w, gs, m, v)])
    for k, (delta, m2, v2) in zip(small, outs):
        res[k] = [t.reshape(w[k].shape) for t in (gs[k], delta, m2, v2)]

    parts = [exchange_wait(SCATTER, scatters[l], total, f"scatter_wait_{l}") for l in range(depth)]
    p1 = sum_partials([owns[l][0] for l in range(depth)], [parts[l][0] for l in range(depth)])
    p2 = sum_partials([owns[l][1] for l in range(depth)], [parts[l][1] for l in range(depth)])
    q1, q2 = sibling_exchange(p1, p2)
    res["w_in"] = [jnp.swapaxes(t, 1, 2) for t in adamw(w_in_t, p1, q1, m_w_in_t, v_w_in_t)]
    res["w_out"] = adamw(w_out, p2, q2, m_w_out, v_w_out)

    return (loss, dx[None], *[res[k][0] for k in names], *[res[k][1] for k in names],
            *[res[k][2] for k in names], *[res[k][3] for k in names])
```

```python
import functools
import math

import jax
import jax.numpy as jnp
import numpy as np
from jax import lax
from jax.experimental import pallas as pl
from jax.experimental.pallas import tpu as pltpu

F32 = jnp.float32
BF16 = jnp.bfloat16

D_MODEL = 1024
GROUP_W = 256
N_HEADS = 4
HEAD_DIM = 64
N_CHUNKS = 13
D_IN = N_CHUNKS * GROUP_W
D_MIX = 4 * GROUP_W
NORM_EPS = 1e-6
RG_C = 8.0
GMLP_CHUNK = 128
ATTN_BLOCK = 128
PATTERN_DILS = (1, 4, 16)
ALIBI_SLOPES = tuple(2.0 ** (-8.0 * (h + 1) / N_HEADS) for h in range(N_HEADS))
ATTN_SCALE = 1.0 / math.sqrt(HEAD_DIM)
NEG_BIG = -1e30

ADAM_LR = 0.001
ADAM_B1 = 0.9
ADAM_B2 = 0.999
ADAM_EPS = 1e-08
ADAM_WD = 0.01
ADAM_STEP = 10

C_AX, C_AB, C_AC, C_AG, C_RX, C_RG, C_CU, C_CV, C_CG, C_DQ, C_DK, C_DV, C_DG = range(13)

SUBLANES = 8
LANES = 128
VMEM_LIMIT = 56 * 1024 * 1024
TILE_IN = 512
TILE_OUT = 1024
TILE_MIX = 512
TILE_DW = 1024
TILE_DW_OUT = 2048
ATTN_QB = 16
GELU_K0 = math.sqrt(2.0 / math.pi)
GELU_K1 = 0.044715


def _params(sem):
    return pltpu.CompilerParams(dimension_semantics=sem, vmem_limit_bytes=VMEM_LIMIT)


def _sigmoid(x):
    return 0.5 * jnp.tanh(0.5 * x) + 0.5


def _silu(x):
    return x * _sigmoid(x)


def _silu_and_grad(x):
    s = _sigmoid(x)
    return x * s, s * (1.0 + x * (1.0 - s))


def _gelu(x):
    return x * (0.5 + 0.5 * jnp.tanh(x * (GELU_K0 + (GELU_K0 * GELU_K1) * (x * x))))


def _gelu_and_grad(x):
    x2 = x * x
    t = jnp.tanh(x * (GELU_K0 + (GELU_K0 * GELU_K1) * x2))
    half = 0.5 + 0.5 * t
    return x * half, half + x * (0.5 - 0.5 * t * t) * (GELU_K0 + (3.0 * GELU_K0 * GELU_K1) * x2)


def _neg_expm1_2x(x):
    t = jnp.tanh(x)
    return -2.0 * t / (1.0 - t)


def _shift_down(v, halo, k):
    r = pltpu.roll(v, k, 0)
    rh = pltpu.roll(halo, k, 0)
    row = lax.broadcasted_iota(jnp.int32, halo.shape, 0)
    top = jnp.where(row < k, rh, r[:SUBLANES])
    return jnp.concatenate([top, r[SUBLANES:]], axis=0)


def _shift_up(v, halo, k):
    t = v.shape[0]
    r = pltpu.roll(v, t - k, 0)
    rh = pltpu.roll(halo, SUBLANES - k, 0)
    row = lax.broadcasted_iota(jnp.int32, halo.shape, 0)
    bot = jnp.where(row >= SUBLANES - k, rh, r[t - SUBLANES:])
    return jnp.concatenate([r[:t - SUBLANES], bot], axis=0)


def _scan_causal(a, b, h_in):
    t = a.shape[0]
    row8 = lax.broadcasted_iota(jnp.int32, a.shape, 0) % SUBLANES
    d = 1
    while d < SUBLANES:
        m = row8 >= d
        a_s = jnp.where(m, pltpu.roll(a, d, 0), 1.0)
        b_s = jnp.where(m, pltpu.roll(b, d, 0), 0.0)
        b = a * b_s + b
        a = a * a_s
        d *= 2
    out, carry = [], h_in
    for g in range(t // SUBLANES):
        sl = slice(g * SUBLANES, (g + 1) * SUBLANES)
        hg = b[sl] + a[sl] * carry
        out.append(hg)
        carry = hg[SUBLANES - 1:SUBLANES]
    return jnp.concatenate(out, axis=0)


def _scan_anticausal(a, b, g_in):
    t = a.shape[0]
    row8 = lax.broadcasted_iota(jnp.int32, a.shape, 0) % SUBLANES
    d = 1
    while d < SUBLANES:
        m = row8 < SUBLANES - d
        a_s = jnp.where(m, pltpu.roll(a, t - d, 0), 1.0)
        b_s = jnp.where(m, pltpu.roll(b, t - d, 0), 0.0)
        b = a * b_s + b
        a = a * a_s
        d *= 2
    out, carry = [], g_in
    for g in reversed(range(t // SUBLANES)):
        sl = slice(g * SUBLANES, (g + 1) * SUBLANES)
        gg = b[sl] + a[sl] * carry
        out.append(gg)
        carry = gg[0:1]
    return jnp.concatenate(out[::-1], axis=0)


def _head_of_lane(shape):
    return lax.broadcasted_iota(jnp.int32, shape, len(shape) - 1) // HEAD_DIM


def _per_head_lanes(cols):
    t = cols[0].shape[0]
    lane = lax.broadcasted_iota(jnp.int32, (t, LANES), 1)
    out = jnp.zeros((t, LANES), F32)
    for h, col in enumerate(cols):
        out = jnp.where(lane == h, col, out)
    return out


def _put_row(acc_shape, k, row_vec):
    row = lax.broadcasted_iota(jnp.int32, acc_shape, 0)
    return jnp.where(row == k, jnp.broadcast_to(row_vec, acc_shape), 0.0)


def _dot(a, b):
    return jnp.dot(a, b, preferred_element_type=F32)


def _dot_nt(a, b):
    return lax.dot_general(a, b, (((1,), (1,)), ((), ())), preferred_element_type=F32)


def _dot_tn(a, b):
    return lax.dot_general(a, b, (((0,), (0,)), ((), ())), preferred_element_type=F32)


def _deinterleave_store(val, stage, outs):
    t, c = val.shape
    for hh in range(c // LANES):
        stage[hh][...] = val[:, hh * LANES:(hh + 1) * LANES].astype(F32)
    for dil, ref in outs:
        for r in range(dil):
            for hh in range(c // LANES):
                ref[r, :, hh * LANES:(hh + 1) * LANES] = stage[hh][pl.ds(r, t // dil, stride=dil), :].astype(ref.dtype)


def _interleave_load(ref, dil, stage):
    _, n, c = ref.shape
    for r in range(dil):
        for hh in range(c // LANES):
            stage[hh][pl.ds(r, n, stride=dil), :] = ref[r, :, hh * LANES:(hh + 1) * LANES].astype(F32)
    return jnp.concatenate([stage[hh][...] for hh in range(c // LANES)], axis=1)


def _stage_scratch(tile, cols, copies):
    return [pltpu.VMEM((tile, LANES), F32)] * (copies * (cols // LANES))


def _by_residue(s, dil, cols, dtype):
    return jax.ShapeDtypeStruct((dil, s // dil, cols), dtype)


def _residue_block(dil, tile, cols):
    return pl.BlockSpec((dil, tile // dil, cols), lambda i: (0, i, 0))


def in_fwd(x, g, w):
    s = x.shape[0]
    qkv_w = 3 * GROUP_W

    def body(x_ref, g_ref, w_ref, z_ref, h_ref, qkv1_ref, qkv4_ref, qkv16_ref, *stage):
        xv = x_ref[...]
        rs = lax.rsqrt(jnp.mean(xv * xv, axis=-1, keepdims=True) + NORM_EPS)
        h = (xv * rs * g_ref[...]).astype(BF16)
        h_ref[...] = h
        z = _dot_nt(h, w_ref[...])
        z_ref[...] = z
        qkv = z[:, C_DQ * GROUP_W:(C_DV + 1) * GROUP_W]
        qkv1_ref[...] = qkv.astype(BF16)
        _deinterleave_store(qkv, stage, ((PATTERN_DILS[1], qkv4_ref), (PATTERN_DILS[2], qkv16_ref)))

    return pl.pallas_call(
        body, name="in_fwd", grid=(s // TILE_IN,),
        in_specs=[pl.BlockSpec((TILE_IN, D_MODEL), lambda i: (i, 0)),
                  pl.BlockSpec((1, D_MODEL), lambda i: (0, 0)),
                  pl.BlockSpec((D_IN, D_MODEL), lambda i: (0, 0))],
        out_specs=[pl.BlockSpec((TILE_IN, D_IN), lambda i: (i, 0)),
                   pl.BlockSpec((TILE_IN, D_MODEL), lambda i: (i, 0)),
                   pl.BlockSpec((TILE_IN, qkv_w), lambda i: (i, 0)),
                   _residue_block(PATTERN_DILS[1], TILE_IN, qkv_w),
                   _residue_block(PATTERN_DILS[2], TILE_IN, qkv_w)],
        out_shape=[jax.ShapeDtypeStruct((s, D_IN), F32), jax.ShapeDtypeStruct((s, D_MODEL), BF16),
                   jax.ShapeDtypeStruct((s, qkv_w), BF16),
                   _by_residue(s, PATTERN_DILS[1], qkv_w, BF16), _by_residue(s, PATTERN_DILS[2], qkv_w, BF16)],
        scratch_shapes=_stage_scratch(TILE_IN, qkv_w, 1),
        compiler_params=_params(("parallel",)),
    )(x, g, w)


def out_fwd(y, w, x):
    s = x.shape[0]

    def body(y_ref, w_ref, x_ref, o_ref):
        o_ref[...] = x_ref[...] + _dot(y_ref[...], w_ref[...])

    return pl.pallas_call(
        body, name="out_fwd", grid=(s // TILE_OUT,),
        in_specs=[pl.BlockSpec((TILE_OUT, D_MIX), lambda i: (i, 0)),
                  pl.BlockSpec((D_MIX, D_MODEL), lambda i: (0, 0)),
                  pl.BlockSpec((TILE_OUT, D_MODEL), lambda i: (i, 0))],
        out_specs=pl.BlockSpec((TILE_OUT, D_MODEL), lambda i: (i, 0)),
        out_shape=jax.ShapeDtypeStruct((s, D_MODEL), F32),
        compiler_params=_params(("parallel",)),
    )(y, w, x)


def out_bwd(dx, w, z, o):
    s = dx.shape[0]
    abc = 3 * GROUP_W

    def body(dx_ref, w_ref, dg_ref, o_ref, dy_ref, ddg_ref, do1_ref, do4_ref, do16_ref, dl1_ref, dl4_ref, dl16_ref,
             *stage):
        stage_a, stage_b = stage[:2], stage[2:]
        dy = _dot_nt(dx_ref[...].astype(BF16), w_ref[...])
        dy_ref[...] = dy[:, :abc]
        dyd = dy[:, abc:]
        head = _head_of_lane((TILE_OUT, GROUP_W))
        dg = dg_ref[...]
        o = o_ref[...]
        sg, dsg = _silu_and_grad(dg)
        do = dyd * sg
        ddg_ref[...] = dyd * o * dsg
        prod = do * o
        dl = _per_head_lanes([jnp.sum(jnp.where(head == h, prod, 0.0), axis=-1, keepdims=True)
                              for h in range(N_HEADS)])
        do1_ref[...] = do.astype(BF16)
        dl1_ref[...] = dl
        _deinterleave_store(do, stage_a, ((PATTERN_DILS[1], do4_ref), (PATTERN_DILS[2], do16_ref)))
        _deinterleave_store(dl, stage_b, ((PATTERN_DILS[1], dl4_ref), (PATTERN_DILS[2], dl16_ref)))

    row = pl.BlockSpec((TILE_OUT, GROUP_W), lambda i: (i, 0))
    r4 = _residue_block(PATTERN_DILS[1], TILE_OUT, GROUP_W)
    r16 = _residue_block(PATTERN_DILS[2], TILE_OUT, GROUP_W)
    crow = pl.BlockSpec((TILE_OUT, LANES), lambda i: (i, 0))
    c4 = _residue_block(PATTERN_DILS[1], TILE_OUT, LANES)
    c16 = _residue_block(PATTERN_DILS[2], TILE_OUT, LANES)
    return pl.pallas_call(
        body, name="out_bwd", grid=(s // TILE_OUT,),
        in_specs=[pl.BlockSpec((TILE_OUT, D_MODEL), lambda i: (i, 0)),
                  pl.BlockSpec((D_MIX, D_MODEL), lambda i: (0, 0)),
                  pl.BlockSpec((TILE_OUT, GROUP_W), lambda i: (i, C_DG)), row],
        out_specs=[pl.BlockSpec((TILE_OUT, abc), lambda i: (i, 0)), row, row, r4, r16, crow, c4, c16],
        out_shape=[jax.ShapeDtypeStruct((s, abc), F32), jax.ShapeDtypeStruct((s, GROUP_W), F32),
                   jax.ShapeDtypeStruct((s, GROUP_W), BF16),
                   _by_residue(s, PATTERN_DILS[1], GROUP_W, BF16), _by_residue(s, PATTERN_DILS[2], GROUP_W, BF16),
                   jax.ShapeDtypeStruct((s, LANES), F32),
                   _by_residue(s, PATTERN_DILS[1], LANES, F32), _by_residue(s, PATTERN_DILS[2], LANES, F32)],
        scratch_shapes=_stage_scratch(TILE_OUT, GROUP_W, 1) + _stage_scratch(TILE_OUT, LANES, 1),
        compiler_params=_params(("parallel",)),
    )(dx, w, z, o)


def in_bwd(dz, w, x, g, dx_next):
    s = x.shape[0]

    def body(dz_ref, w_ref, x_ref, g_ref, dxn_ref, dx_ref, dg_ref):
        @pl.when(pl.program_id(0) == 0)
        def _():
            dg_ref[...] = jnp.zeros_like(dg_ref)

        dh = _dot(dz_ref[...], w_ref[...])
        xv = x_ref[...]
        rs = lax.rsqrt(jnp.mean(xv * xv, axis=-1, keepdims=True) + NORM_EPS)
        xh = xv * rs
        dg_ref[...] += _put_row(dg_ref.shape, 0, jnp.sum(dh * xh, axis=0, keepdims=True))
        dn = dh * g_ref[...]
        dx_ref[...] = dxn_ref[...] + rs * (dn - xh * jnp.mean(dn * xh, axis=-1, keepdims=True))

    return pl.pallas_call(
        body, name="in_bwd", grid=(s // TILE_IN,),
        in_specs=[pl.BlockSpec((TILE_IN, D_IN), lambda i: (i, 0)),
                  pl.BlockSpec((D_IN, D_MODEL), lambda i: (0, 0)),
                  pl.BlockSpec((TILE_IN, D_MODEL), lambda i: (i, 0)),
                  pl.BlockSpec((1, D_MODEL), lambda i: (0, 0)),
                  pl.BlockSpec((TILE_IN, D_MODEL), lambda i: (i, 0))],
        out_specs=[pl.BlockSpec((TILE_IN, D_MODEL), lambda i: (i, 0)),
                   pl.BlockSpec((SUBLANES, D_MODEL), lambda i: (0, 0))],
        out_shape=[jax.ShapeDtypeStruct((s, D_MODEL), F32), jax.ShapeDtypeStruct((SUBLANES, D_MODEL), F32)],
        compiler_params=_params(("arbitrary",)),
    )(dz, w, x, g, dx_next)


def grad_w_out(y, dx):
    s = y.shape[0]

    def body(y_ref, dx_ref, o_ref):
        @pl.when(pl.program_id(0) == 0)
        def _():
            o_ref[...] = jnp.zeros_like(o_ref)

        o_ref[...] += _dot_tn(y_ref[...], dx_ref[...].astype(BF16))

    return pl.pallas_call(
        body, name="grad_w_out", grid=(s // TILE_DW_OUT,),
        in_specs=[pl.BlockSpec((TILE_DW_OUT, D_MIX), lambda k: (k, 0)),
                  pl.BlockSpec((TILE_DW_OUT, D_MODEL), lambda k: (k, 0))],
        out_specs=pl.BlockSpec((D_MIX, D_MODEL), lambda k: (0, 0)),
        out_shape=jax.ShapeDtypeStruct((D_MIX, D_MODEL), F32),
        compiler_params=_params(("arbitrary",)),
    )(y, dx)


def grad_w_in(h, dz, chip):
    s = h.shape[0]
    rows = D_IN // N_CHIPS

    def body(chip_ref, h_ref, dz_ref, staged_ref, own_ref, acc):
        k = pl.program_id(0)

        @pl.when(k == 0)
        def _():
            acc[...] = jnp.zeros_like(acc)

        acc[...] += _dot_tn(dz_ref[...], h_ref[...])

        @pl.when(k == s // TILE_DW - 1)
        def _():
            for j in range(N_CHIPS):
                part = acc[j * rows:(j + 1) * rows, :]
                staged_ref[j] = part.astype(BF16)

                @pl.when(chip_ref[0] == j)
                def _():
                    own_ref[...] = part

    return pl.pallas_call(
        body, name="grad_w_in",
        grid_spec=pltpu.PrefetchScalarGridSpec(
            num_scalar_prefetch=1, grid=(s // TILE_DW,),
            in_specs=[pl.BlockSpec((TILE_DW, D_MODEL), lambda k, c: (k, 0)),
                      pl.BlockSpec((TILE_DW, D_IN), lambda k, c: (k, 0))],
            out_specs=[pl.BlockSpec((N_CHIPS, rows, D_MODEL), lambda k, c: (0, 0, 0)),
                       pl.BlockSpec((rows, D_MODEL), lambda k, c: (0, 0))],
            scratch_shapes=[pltpu.VMEM((D_IN, D_MODEL), F32)]),
        out_shape=[jax.ShapeDtypeStruct((N_CHIPS, rows, D_MODEL), BF16), jax.ShapeDtypeStruct((rows, D_MODEL), F32)],
        compiler_params=_params(("arbitrary",)),
    )(chip, h, dz)


def out_fwd_loss(y, w, x, g, tgt):
    s = x.shape[0]

    def body(y_ref, w_ref, x_ref, g_ref, t_ref, l_ref, dx_ref, dg_ref):
        @pl.when(pl.program_id(0) == 0)
        def _():
            l_ref[...] = jnp.zeros_like(l_ref)
            dg_ref[...] = jnp.zeros_like(dg_ref)

        xv = x_ref[...] + _dot(y_ref[...], w_ref[...])
        gv = g_ref[...]
        rs = lax.rsqrt(jnp.mean(xv * xv, axis=-1, keepdims=True) + NORM_EPS)
        xh = xv * rs
        e = xh * gv - t_ref[...]
        part = 0.5 * jnp.sum(jnp.mean(e * e, axis=-1, keepdims=True), axis=0, keepdims=True)
        l_ref[...] += jnp.broadcast_to(part, l_ref.shape)
        dy = e * (1.0 / D_MODEL)
        dg_ref[...] += _put_row(dg_ref.shape, 0, jnp.sum(dy * xh, axis=0, keepdims=True))
        dn = dy * gv
        dx_ref[...] = rs * (dn - xh * jnp.mean(dn * xh, axis=-1, keepdims=True))

    return pl.pallas_call(
        body, name="out_fwd_loss", grid=(s // TILE_OUT,),
        in_specs=[pl.BlockSpec((TILE_OUT, D_MIX), lambda i: (i, 0)),
                  pl.BlockSpec((D_MIX, D_MODEL), lambda i: (0, 0)),
                  pl.BlockSpec((TILE_OUT, D_MODEL), lambda i: (i, 0)),
                  pl.BlockSpec((1, D_MODEL), lambda i: (0, 0)),
                  pl.BlockSpec((TILE_OUT, D_MODEL), lambda i: (i, 0))],
        out_specs=[pl.BlockSpec((SUBLANES, LANES), lambda i: (0, 0)),
                   pl.BlockSpec((TILE_OUT, D_MODEL), lambda i: (i, 0)),
                   pl.BlockSpec((SUBLANES, D_MODEL), lambda i: (0, 0))],
        out_shape=[jax.ShapeDtypeStruct((SUBLANES, LANES), F32), jax.ShapeDtypeStruct((s, D_MODEL), F32),
                   jax.ShapeDtypeStruct((SUBLANES, D_MODEL), F32)],
        compiler_params=_params(("arbitrary",)),
    )(y, w, x, g, tgt)


def _attn_bias(dil):
    qi = np.arange(ATTN_BLOCK)[:, None]
    ki = np.arange(2 * ATTN_BLOCK)[None, :]
    delta = qi + ATTN_BLOCK - ki
    band = (delta >= 0) & (delta <= ATTN_BLOCK)
    out = np.empty((2, N_HEADS, ATTN_BLOCK, 2 * ATTN_BLOCK), np.float32)
    for f in range(2):
        ok = band & ((ki >= ATTN_BLOCK) | (f == 0))
        for h in range(N_HEADS):
            out[f, h] = np.where(ok, -ALIBI_SLOPES[h] * dil * delta, NEG_BIG)
    return jnp.asarray(out.reshape(2, N_HEADS * ATTN_BLOCK, 2 * ATTN_BLOCK))


def _stack_heads(a, head):
    return jnp.concatenate([jnp.where(head == h, a, jnp.zeros_like(a)) for h in range(N_HEADS)], axis=0)


def _unstack_heads(a, head):
    out = a[:ATTN_BLOCK]
    for h in range(1, N_HEADS):
        out = jnp.where(head == h, a[h * ATTN_BLOCK:(h + 1) * ATTN_BLOCK], out)
    return out


def _head_column(a):
    return jnp.concatenate([a[:, h:h + 1] for h in range(N_HEADS)], axis=0)


def _attn_specs(n_blocks):
    rows = ATTN_QB * ATTN_BLOCK
    cur = lambda c, w=GROUP_W: pl.BlockSpec((rows, w), lambda n, c=c: (n, c))
    prev = lambda c: pl.BlockSpec((ATTN_BLOCK, GROUP_W), lambda n, c=c: (jnp.maximum(n * ATTN_QB - 1, 0), c))
    nxt = lambda c, w=GROUP_W: pl.BlockSpec((ATTN_BLOCK, w),
                                            lambda n, c=c: (jnp.minimum(n * ATTN_QB + ATTN_QB, n_blocks - 1), c))
    return cur, prev, nxt


def _keys(kp_ref, k_ref, j):
    prev = kp_ref[...] if j == 0 else k_ref[(j - 1) * ATTN_BLOCK:j * ATTN_BLOCK, :]
    return jnp.concatenate([prev, k_ref[j * ATTN_BLOCK:(j + 1) * ATTN_BLOCK, :]], axis=0)


def attn_fwd(qkv, dil):
    s = qkv.shape[0]
    n_blocks = s // ATTN_BLOCK
    bps = n_blocks // dil
    rows = ATTN_QB * ATTN_BLOCK

    def body(q_ref, kp_ref, k_ref, vp_ref, v_ref, bias_ref, o_ref, lse_ref):
        n = pl.program_id(0)
        head = _head_of_lane((ATTN_BLOCK, GROUP_W))
        for j in range(ATTN_QB):
            sl = slice(j * ATTN_BLOCK, (j + 1) * ATTN_BLOCK)
            first = (((n * ATTN_QB + j) % bps) == 0).astype(jnp.int32)
            qs = _stack_heads(q_ref[sl, :], head)
            sc = _dot_nt(qs, _keys(kp_ref, k_ref, j)) * ATTN_SCALE + bias_ref[first]
            m = jnp.max(sc, axis=-1, keepdims=True)
            pr = jnp.exp(sc - m)
            l = jnp.sum(pr, axis=-1, keepdims=True)
            oh = _dot(pr.astype(BF16), _keys(vp_ref, v_ref, j)) / l
            o_ref[sl, :] = _unstack_heads(oh, head).astype(BF16)
            lse_ref[sl, :] = _unstack_heads(jnp.broadcast_to(m + jnp.log(l), oh.shape), head)

    cur, prev, _ = _attn_specs(n_blocks)
    bias = _attn_bias(dil)
    return pl.pallas_call(
        body, name=f"attn_fwd_d{dil}", grid=(n_blocks // ATTN_QB,),
        in_specs=[cur(0), prev(1), cur(1), prev(2), cur(2), pl.BlockSpec(bias.shape, lambda n: (0, 0, 0))],
        out_specs=[cur(0), cur(0)],
        out_shape=[jax.ShapeDtypeStruct((s, GROUP_W), BF16), jax.ShapeDtypeStruct((s, GROUP_W), F32)],
        compiler_params=_params(("parallel",)),
    )(qkv, qkv, qkv, qkv, qkv, bias)


def attn_bwd(qkv, do, lse, dlt, dil):
    s = qkv.shape[0]
    n_blocks = s // ATTN_BLOCK
    bps = n_blocks // dil
    rows = ATTN_QB * ATTN_BLOCK

    def body(q_ref, qn_ref, kp_ref, k_ref, vp_ref, v_ref, do_ref, don_ref, lse_ref, lsen_ref, dl_ref, dln_ref,
             bias_ref, out_ref, dk_acc, dv_acc):
        n = pl.program_id(0)
        head = _head_of_lane((ATTN_BLOCK, GROUP_W))
        dk_acc[...] = jnp.zeros_like(dk_acc)
        dv_acc[...] = jnp.zeros_like(dv_acc)

        def pair(qj, doj, lsej, dlj, kk, vv, bias, keep):
            qs = _stack_heads(qj, head)
            dos = _stack_heads(doj, head)
            sc = _dot_nt(qs, kk) * ATTN_SCALE + bias
            if keep is None:
                pr = jnp.exp(sc - _head_column(lsej))
            else:
                pr = jnp.exp(jnp.minimum(sc - _head_column(lsej), 0.0)) * keep
            dp = _dot_nt(dos, vv)
            ds = (pr * (dp - _head_column(dlj)) * ATTN_SCALE).astype(BF16)
            return ds, _dot_tn(ds, qs), _dot_tn(pr.astype(BF16), dos)

        for j in range(ATTN_QB):
            sl = slice(j * ATTN_BLOCK, (j + 1) * ATTN_BLOCK)
            first = (((n * ATTN_QB + j) % bps) == 0).astype(jnp.int32)
            kk = _keys(kp_ref, k_ref, j)
            ds, dks, dvs = pair(q_ref[sl, :], do_ref[sl, :], lse_ref[sl, :], dl_ref[sl, :],
                                kk, _keys(vp_ref, v_ref, j), bias_ref[first], None)
            out_ref[sl, 0:GROUP_W] = _unstack_heads(_dot(ds, kk), head)
            acc = slice(j * ATTN_BLOCK, (j + 2) * ATTN_BLOCK)
            dk_acc[acc, :] += dks
            dv_acc[acc, :] += dvs

        nxt = n * ATTN_QB + ATTN_QB
        valid = ((nxt < n_blocks) & ((nxt % bps) != 0)).astype(F32)
        last = slice((ATTN_QB - 1) * ATTN_BLOCK, ATTN_QB * ATTN_BLOCK)
        _, dks, dvs = pair(qn_ref[...], don_ref[...], lsen_ref[...], dln_ref[...], k_ref[last, :], v_ref[last, :],
                           bias_ref[0][:, :ATTN_BLOCK], valid)
        acc = slice(ATTN_QB * ATTN_BLOCK, (ATTN_QB + 1) * ATTN_BLOCK)
        dk_acc[acc, :] += dks
        dv_acc[acc, :] += dvs
        out_ref[:, GROUP_W:2 * GROUP_W] = dk_acc[ATTN_BLOCK:, :]
        out_ref[:, 2 * GROUP_W:3 * GROUP_W] = dv_acc[ATTN_BLOCK:, :]

    cur, prev, nxt = _attn_specs(n_blocks)
    bias = _attn_bias(dil)
    return pl.pallas_call(
        body, name=f"attn_bwd_d{dil}", grid=(n_blocks // ATTN_QB,),
        in_specs=[cur(0), nxt(0), prev(1), cur(1), prev(2), cur(2), cur(0), nxt(0),
                  cur(0, LANES), nxt(0, LANES), cur(0, LANES), nxt(0, LANES),
                  pl.BlockSpec(bias.shape, lambda n: (0, 0, 0))],
        out_specs=pl.BlockSpec((rows, 3 * GROUP_W), lambda n: (n, 0)),
        out_shape=jax.ShapeDtypeStruct((s, 3 * GROUP_W), F32),
        scratch_shapes=[pltpu.VMEM(((ATTN_QB + 1) * ATTN_BLOCK, GROUP_W), F32),
                        pltpu.VMEM(((ATTN_QB + 1) * ATTN_BLOCK, GROUP_W), F32)],
        compiler_params=_params(("parallel",)),
    )(qkv, qkv, qkv, qkv, qkv, qkv, do, do, lse, lse, dlt, dlt, bias)


def _zcol(c):
    return pl.BlockSpec((TILE_MIX, GROUP_W), lambda i, c=c: (i, c))


def _zhalo(c):
    per = TILE_MIX // SUBLANES
    return pl.BlockSpec((SUBLANES, GROUP_W), lambda i, c=c: (jnp.maximum(i * per - 1, 0), c))


def _full(shape):
    return pl.BlockSpec(shape, lambda i: tuple(0 for _ in shape))


def _of_layer(a, l):
    rest = a.shape[1:]
    return pl.BlockSpec((None,) + rest, lambda i: (l,) + tuple(0 for _ in rest))


def _softplus_neg(lam):
    nl = -lam
    return jnp.maximum(nl, 0.0) + jnp.log1p(jnp.exp(-jnp.abs(nl)))


def _lru_gates(xb, wa_ref, wx_ref, ba, bx, lam):
    xbb = xb.astype(BF16)
    r = _sigmoid(_dot(xbb, wa_ref[...]) + ba)
    ig = _sigmoid(_dot(xbb, wx_ref[...]) + bx)
    log_a = (-RG_C * r) * _softplus_neg(lam)
    a = jnp.exp(log_a)
    mult = jnp.sqrt(_neg_expm1_2x(log_a))
    return r, ig, a, mult


LRU_SAVED = 5


def _gmlp_spatial(ws_ref, vvb, head):
    outs = []
    for j in range(vvb.shape[0] // GMLP_CHUNK):
        blk = vvb[j * GMLP_CHUNK:(j + 1) * GMLP_CHUNK, :]
        acc = jnp.zeros((GMLP_CHUNK, GROUP_W), F32)
        for h in range(N_HEADS):
            acc = jnp.where(head[:GMLP_CHUNK] == h, _dot(ws_ref[h], blk), acc)
        outs.append(acc)
    return jnp.concatenate(outs, axis=0)


def mix_fwd(z, attn, wts, l):
    s = z.shape[0]
    d4, d16 = PATTERN_DILS[1], PATTERN_DILS[2]

    def body(ax_ref, ab_ref, ac_ref, ag_ref, rx_ref, rg_ref, cu_ref, cv_ref, cg_ref, dg_ref,
             axh_ref, ach_ref, rxh_ref, o1_ref, l1_ref, o4_ref, l4_ref, o16_ref, l16_ref,
             caw_ref, crw_ref, crb_ref, wa_ref, wx_ref, ba_ref, bx_ref, lam_ref, gng_ref, ws_ref, bs_ref,
             y_ref, hl_ref, o_ref, lse_ref, lse4_ref, lse16_ref, lru_ref, carry, *stage):
        st_a, st_b, st_c, st_d, st_e = (stage[2 * k:2 * k + 2] for k in range(5))
        i = pl.program_id(0)

        @pl.when(i == 0)
        def _():
            carry[...] = jnp.zeros_like(carry)

        nz = (i > 0).astype(F32)
        head = _head_of_lane((TILE_MIX, GROUP_W))

        pa = ac_ref[...] * ax_ref[...]
        pah = ach_ref[...] * axh_ref[...] * nz
        cv = caw_ref[2:3, :] * pa + caw_ref[1:2, :] * _shift_down(pa, pah, 1) + caw_ref[0:1, :] * _shift_down(pa, pah, 2)
        y_ref[:, 0:GROUP_W] = (ab_ref[...] * cv * _silu(ag_ref[...])).astype(BF16)

        rx = rx_ref[...]
        rxh = rxh_ref[...] * nz
        xb = (crw_ref[3:4, :] * rx + crw_ref[2:3, :] * _shift_down(rx, rxh, 1) + crw_ref[1:2, :] * _shift_down(rx, rxh, 2)
              + crw_ref[0:1, :] * _shift_down(rx, rxh, 3) + crb_ref[...])
        r, ig, a, mult = _lru_gates(xb, wa_ref, wx_ref, ba_ref[...], bx_ref[...], lam_ref[...])
        for k, val in enumerate((xb, r, ig, a, mult)):
            lru_ref[:, k * GROUP_W:(k + 1) * GROUP_W] = val
        hl = _scan_causal(a, mult * (ig * xb), carry[SUBLANES - 1:SUBLANES, :])
        hl_ref[...] = hl
        carry[...] = hl[TILE_MIX - SUBLANES:, :]
        y_ref[:, GROUP_W:2 * GROUP_W] = (hl * _silu(rg_ref[...])).astype(BF16)

        u = _gelu(cu_ref[...])
        gv = _gelu(cv_ref[...])
        rs = lax.rsqrt(jnp.mean(gv * gv, axis=-1, keepdims=True) + NORM_EPS)
        vvb = (gv * rs * gng_ref[...]).astype(BF16)
        sp = _gmlp_spatial(ws_ref, vvb, head) + jnp.concatenate([bs_ref[...]] * (TILE_MIX // GMLP_CHUNK), axis=0)
        y_ref[:, 2 * GROUP_W:3 * GROUP_W] = (u * sp * _silu(cg_ref[...])).astype(BF16)

        ops = (o1_ref[...].astype(F32), _interleave_load(o4_ref, d4, st_a), _interleave_load(o16_ref, d16, st_b))
        lps = (l1_ref[...], _interleave_load(l4_ref, d4, st_c), _interleave_load(l16_ref, d16, st_d))
        m = jnp.maximum(jnp.maximum(lps[0], lps[1]), lps[2])
        zsum = jnp.zeros_like(m)
        o = jnp.zeros_like(m)
        for op, lp in zip(ops, lps):
            w = jnp.exp(lp - m)
            zsum = zsum + w
            o = o + w * op
        o = o / zsum
        lse = m + jnp.log(zsum)
        lse = _per_head_lanes([lse[:, h * HEAD_DIM:h * HEAD_DIM + 1] for h in range(N_HEADS)])
        o_ref[...] = o
        lse_ref[...] = lse
        _deinterleave_store(lse, st_e, ((d4, lse4_ref), (d16, lse16_ref)))
        y_ref[:, 3 * GROUP_W:4 * GROUP_W] = (o * _silu(dg_ref[...])).astype(BF16)

    row = pl.BlockSpec((TILE_MIX, GROUP_W), lambda i: (i, 0))
    r4 = _residue_block(d4, TILE_MIX, GROUP_W)
    r16 = _residue_block(d16, TILE_MIX, GROUP_W)
    crow = pl.BlockSpec((TILE_MIX, LANES), lambda i: (i, 0))
    c4 = _residue_block(d4, TILE_MIX, LANES)
    c16 = _residue_block(d16, TILE_MIX, LANES)
    names = ("caw", "crw", "crb", "wa", "wx", "ba", "bx", "lam", "gng", "ws", "bs")
    in_specs = ([_zcol(c) for c in (C_AX, C_AB, C_AC, C_AG, C_RX, C_RG, C_CU, C_CV, C_CG, C_DG)]
                + [_zhalo(C_AX), _zhalo(C_AC), _zhalo(C_RX), row, row, r4, r4, r16, r16]
                + [_of_layer(wts[k], l) for k in names])
    return pl.pallas_call(
        body, name="mix_fwd", grid=(s // TILE_MIX,),
        in_specs=in_specs,
        out_specs=[pl.BlockSpec((TILE_MIX, D_MIX), lambda i: (i, 0)), row, row, crow, c4, c16,
                   pl.BlockSpec((TILE_MIX, LRU_SAVED * GROUP_W), lambda i: (i, 0))],
        out_shape=([jax.ShapeDtypeStruct((s, D_MIX), BF16)] + [jax.ShapeDtypeStruct((s, GROUP_W), F32)] * 2
                   + [jax.ShapeDtypeStruct((s, LANES), F32), _by_residue(s, d4, LANES, F32),
                      _by_residue(s, d16, LANES, F32), jax.ShapeDtypeStruct((s, LRU_SAVED * GROUP_W), F32)]),
        scratch_shapes=([pltpu.VMEM((SUBLANES, GROUP_W), F32)] + _stage_scratch(TILE_MIX, GROUP_W, 4)
                        + _stage_scratch(TILE_MIX, LANES, 1)),
        compiler_params=_params(("arbitrary",)),
    )(*([z] * 13), *[a for pair in attn for a in pair], *[wts[k] for k in names])


def mix_bwd(dy, z, hl, lru, dqkv, ddg, wts, l):
    s = z.shape[0]
    d4, d16 = PATTERN_DILS[1], PATTERN_DILS[2]
    n_tiles = s // TILE_MIX

    def body(dya_ref, dyb_ref, dyc_ref, ax_ref, ab_ref, ac_ref, ag_ref, rx_ref, rg_ref, cu_ref, cv_ref, cg_ref,
             axh_ref, ach_ref, rxh_ref, hl_ref, hlh_ref, lru_ref, dqkv1_ref, dqkv4_ref, dqkv16_ref, ddg_ref,
             caw_ref, crw_ref, crb_ref, wa_ref, wx_ref, ba_ref, bx_ref, lam_ref, gng_ref, ws_ref, wst_ref, bs_ref,
             dz_ref, ga_ref, gr_ref, gn_ref, gwa_ref, gwx_ref, gws_ref, gbs_ref,
             c_dcv, c_g, c_a, c_dxb, *stage):
        st_a, st_b = stage[:len(stage) // 2], stage[len(stage) // 2:]
        step = pl.program_id(0)
        i = n_tiles - 1 - step

        @pl.when(step == 0)
        def _():
            for r in (c_dcv, c_g, c_a, c_dxb, ga_ref, gr_ref, gn_ref, gwa_ref, gwx_ref, gws_ref, gbs_ref):
                r[...] = jnp.zeros_like(r)

        nz = (i > 0).astype(F32)
        head = _head_of_lane((TILE_MIX, GROUP_W))
        shp8 = (SUBLANES, GROUP_W)
        colsum = lambda v: jnp.sum(v, axis=0, keepdims=True)

        ax, ab, ac, ag = ax_ref[...], ab_ref[...], ac_ref[...], ag_ref[...]
        dya = dya_ref[...]
        pa = ac * ax
        pah = ach_ref[...] * axh_ref[...] * nz
        pa1 = _shift_down(pa, pah, 1)
        pa2 = _shift_down(pa, pah, 2)
        cv = caw_ref[2:3, :] * pa + caw_ref[1:2, :] * pa1 + caw_ref[0:1, :] * pa2
        sg, dsg = _silu_and_grad(ag)
        dz_ref[:, C_AB * GROUP_W:(C_AB + 1) * GROUP_W] = (dya * cv * sg).astype(BF16)
        dz_ref[:, C_AG * GROUP_W:(C_AG + 1) * GROUP_W] = (dya * ab * cv * dsg).astype(BF16)
        dcv = dya * ab * sg
        nxt = c_dcv[...]
        dpa = caw_ref[2:3, :] * dcv + caw_ref[1:2, :] * _shift_up(dcv, nxt, 1) + caw_ref[0:1, :] * _shift_up(dcv, nxt, 2)
        c_dcv[...] = dcv[:SUBLANES, :]
        dz_ref[:, C_AC * GROUP_W:(C_AC + 1) * GROUP_W] = (dpa * ax).astype(BF16)
        dz_ref[:, C_AX * GROUP_W:(C_AX + 1) * GROUP_W] = (dpa * ac).astype(BF16)
        ga_ref[...] += (_put_row(shp8, 2, colsum(dcv * pa)) + _put_row(shp8, 1, colsum(dcv * pa1))
                        + _put_row(shp8, 0, colsum(dcv * pa2)))

        rx, rg = rx_ref[...], rg_ref[...]
        dyb = dyb_ref[...]
        rxh = rxh_ref[...] * nz
        rx1, rx2, rx3 = _shift_down(rx, rxh, 1), _shift_down(rx, rxh, 2), _shift_down(rx, rxh, 3)
        xb, r, ig, a, mult = (lru_ref[:, k * GROUP_W:(k + 1) * GROUP_W] for k in range(LRU_SAVED))
        lam = lam_ref[...]
        sp = _softplus_neg(lam)
        hl = hl_ref[...]
        hprev = _shift_down(hl, hlh_ref[...] * nz, 1)
        sgr, dsgr = _silu_and_grad(rg)
        dz_ref[:, C_RG * GROUP_W:(C_RG + 1) * GROUP_W] = (dyb * hl * dsgr).astype(BF16)
        dh = dyb * sgr
        a_next = _shift_up(a, c_a[...], 1)
        g = _scan_anticausal(a_next, dh, c_g[0:1, :])
        c_g[...] = g[:SUBLANES, :]
        c_a[...] = a[:SUBLANES, :]
        u = ig * xb
        da = g * hprev
        dmult = g * u
        du = g * mult
        dlog_a = da * a - dmult * (a * a) / mult
        dr = dlog_a * (-RG_C * sp)
        dga = dr * r * (1.0 - r)
        dgx = (du * xb) * ig * (1.0 - ig)
        dgab, dgxb = dga.astype(BF16), dgx.astype(BF16)
        dxb = du * ig + _dot_nt(dgab, wa_ref[...]) + _dot_nt(dgxb, wx_ref[...])
        xbb = xb.astype(BF16)
        gwa_ref[...] += _dot_tn(xbb, dgab)
        gwx_ref[...] += _dot_tn(xbb, dgxb)
        nxt = c_dxb[...]
        drx = (crw_ref[3:4, :] * dxb + crw_ref[2:3, :] * _shift_up(dxb, nxt, 1) + crw_ref[1:2, :] * _shift_up(dxb, nxt, 2)
               + crw_ref[0:1, :] * _shift_up(dxb, nxt, 3))
        c_dxb[...] = dxb[:SUBLANES, :]
        dz_ref[:, C_RX * GROUP_W:(C_RX + 1) * GROUP_W] = drx.astype(BF16)
        dlam = colsum(dlog_a * (-RG_C * r)) * (-_sigmoid(-lam))
        gr_ref[...] += (_put_row(shp8, 3, colsum(dxb * rx)) + _put_row(shp8, 2, colsum(dxb * rx1))
                        + _put_row(shp8, 1, colsum(dxb * rx2)) + _put_row(shp8, 0, colsum(dxb * rx3))
                        + _put_row(shp8, 4, colsum(dxb)) + _put_row(shp8, 5, colsum(dga))
                        + _put_row(shp8, 6, colsum(dgx)) + _put_row(shp8, 7, dlam))

        cu, cvv, cg = cu_ref[...], cv_ref[...], cg_ref[...]
        dyc = dyc_ref[...]
        u_c, du_c = _gelu_and_grad(cu)
        gv, dgv_c = _gelu_and_grad(cvv)
        rs = lax.rsqrt(jnp.mean(gv * gv, axis=-1, keepdims=True) + NORM_EPS)
        vh = gv * rs
        gng = gng_ref[...]
        vvb = (vh * gng).astype(BF16)
        spat = _gmlp_spatial(ws_ref, vvb, head) + jnp.concatenate([bs_ref[...]] * (TILE_MIX // GMLP_CHUNK), axis=0)
        sgc, dsgc = _silu_and_grad(cg)
        dz_ref[:, C_CU * GROUP_W:(C_CU + 1) * GROUP_W] = (dyc * spat * sgc * du_c).astype(BF16)
        dz_ref[:, C_CG * GROUP_W:(C_CG + 1) * GROUP_W] = (dyc * u_c * spat * dsgc).astype(BF16)
        dsp = dyc * u_c * sgc
        dspb = dsp.astype(BF16)
        tril = (lax.broadcasted_iota(jnp.int32, (GMLP_CHUNK, GMLP_CHUNK), 0)
                >= lax.broadcasted_iota(jnp.int32, (GMLP_CHUNK, GMLP_CHUNK), 1))
        head_c = head[:GMLP_CHUNK]
        dvv_parts = []
        gbs = jnp.zeros((GMLP_CHUNK, GROUP_W), F32)
        for j in range(TILE_MIX // GMLP_CHUNK):
            sl = slice(j * GMLP_CHUNK, (j + 1) * GMLP_CHUNK)
            dblk = dspb[sl, :]
            vblk = vvb[sl, :]
            gbs = gbs + dsp[sl, :]
            acc = jnp.zeros((GMLP_CHUNK, GROUP_W), F32)
            for h in range(N_HEADS):
                acc = jnp.where(head_c == h, _dot(wst_ref[h], dblk), acc)
                dm = jnp.where(head_c == h, dblk, jnp.zeros_like(dblk))
                gws_ref[h] += jnp.where(tril, _dot_nt(dm, vblk), 0.0)
            dvv_parts.append(acc)
        gbs_ref[...] += gbs
        dvv = jnp.concatenate(dvv_parts, axis=0)
        gn_ref[...] += _put_row(shp8, 0, colsum(dvv * vh))
        dvh = dvv * gng
        dgv = rs * (dvh - vh * jnp.mean(dvh * vh, axis=-1, keepdims=True))
        dz_ref[:, C_CV * GROUP_W:(C_CV + 1) * GROUP_W] = (dgv * dgv_c).astype(BF16)

        dsum = dqkv1_ref[...] + _interleave_load(dqkv4_ref, d4, st_a) + _interleave_load(dqkv16_ref, d16, st_b)
        dz_ref[:, C_DQ * GROUP_W:(C_DV + 1) * GROUP_W] = dsum.astype(BF16)
        dz_ref[:, C_DG * GROUP_W:(C_DG + 1) * GROUP_W] = ddg_ref[...].astype(BF16)

    per = TILE_MIX // SUBLANES
    qkv_w = 3 * GROUP_W
    rev = lambda c: pl.BlockSpec((TILE_MIX, GROUP_W), lambda t, c=c: (n_tiles - 1 - t, c))
    revh = lambda c: pl.BlockSpec((SUBLANES, GROUP_W),
                                  lambda t, c=c: (jnp.maximum((n_tiles - 1 - t) * per - 1, 0), c))
    revr = lambda dil: pl.BlockSpec((dil, TILE_MIX // dil, qkv_w), lambda t: (0, n_tiles - 1 - t, 0))
    names = ("caw", "crw", "crb", "wa", "wx", "ba", "bx", "lam", "gng", "ws", "wst", "bs")
    in_specs = ([rev(0), rev(1), rev(2)]
                + [rev(c) for c in (C_AX, C_AB, C_AC, C_AG, C_RX, C_RG, C_CU, C_CV, C_CG)]
                + [revh(C_AX), revh(C_AC), revh(C_RX), rev(0), revh(0),
                   pl.BlockSpec((TILE_MIX, LRU_SAVED * GROUP_W), lambda t: (n_tiles - 1 - t, 0)),
                   pl.BlockSpec((TILE_MIX, qkv_w), lambda t: (n_tiles - 1 - t, 0)), revr(d4), revr(d16), rev(0)]
                + [_of_layer(wts[k], l) for k in names])
    small = jax.ShapeDtypeStruct((SUBLANES, GROUP_W), F32)
    sq = jax.ShapeDtypeStruct((GROUP_W, GROUP_W), F32)
    out_shape = [jax.ShapeDtypeStruct((s, D_IN), BF16), small, small, small, sq, sq,
                 jax.ShapeDtypeStruct((N_HEADS, GMLP_CHUNK, GMLP_CHUNK), F32),
                 jax.ShapeDtypeStruct((GMLP_CHUNK, GROUP_W), F32)]
    out_specs = ([pl.BlockSpec((TILE_MIX, D_IN), lambda t: (n_tiles - 1 - t, 0))]
                 + [_full(o.shape) for o in out_shape[1:]])
    return pl.pallas_call(
        body, name="mix_bwd", grid=(n_tiles,),
        in_specs=in_specs, out_specs=out_specs, out_shape=out_shape,
        scratch_shapes=[pltpu.VMEM((SUBLANES, GROUP_W), F32)] * 4 + _stage_scratch(TILE_MIX, qkv_w, 2),
        compiler_params=_params(("arbitrary",)),
    )(dy, dy, dy, *([z] * 12), hl, hl, lru, *dqkv, ddg, *[wts[k] for k in names])


def prepare_small_weights(p):
    tril = jnp.tril(jnp.ones((GMLP_CHUNK, GMLP_CHUNK), dtype=bool))
    ws = jnp.where(tril, p["gmlp_ws"], 0.0).astype(BF16)
    row = lambda a: a[:, None, :]
    eye = jnp.eye(N_HEADS, dtype=F32)
    bd = lambda w: (w[:, :, :, None, :] * eye[None, :, None, :, None]).reshape(-1, GROUP_W, GROUP_W).astype(BF16)
    return dict(
        caw=p["conv_a_w"], crw=p["conv_r_w"], crb=row(p["conv_r_b"]),
        wa=bd(p["lru_wa"]), wx=bd(p["lru_wx"]),
        ba=row(p["lru_ba"]), bx=row(p["lru_bx"]), lam=row(p["lru_lambda"]), gng=row(p["gmlp_norm_g"]),
        ws=ws, wst=jnp.swapaxes(ws, 2, 3),
        bs=jnp.repeat(jnp.swapaxes(p["gmlp_bs"], 1, 2), HEAD_DIM, axis=2))


def _flat(a):
    return a.reshape(a.shape[0] * a.shape[1], a.shape[2])


def _split(a, dil):
    return a.reshape(dil, a.shape[0] // dil, a.shape[1])


def local_step(x, tgt, final_g, depth, chip, layer_weights, projections_done):
    saved = []
    for l in range(depth):
        gain, w_in_l, rest = layer_weights(l, x)
        z, h, *qkvs = in_fwd(x, gain, w_in_l)
        w_out_l, wts = rest(z)
        qkvs = [_flat(q) if q.ndim == 3 else q for q in qkvs]
        attn = []
        for q, d in zip(qkvs, PATTERN_DILS):
            o_p, lse_p = attn_fwd(q, d)
            attn.append((o_p, lse_p) if d == 1 else (_split(o_p, d), _split(lse_p, d)))
        y, hl, o, lse, lse4, lse16, lru = mix_fwd(z, attn, wts, l)
        saved.append(dict(x=x, z=z, h=h, y=y, hl=hl, o=o, qkvs=qkvs, lses=(lse, _flat(lse4), _flat(lse16)), wts=wts, lru=lru,
                          gain=gain, w_in=w_in_l, w_out=w_out_l))
        if l < depth - 1:
            x = out_fwd(y, w_out_l, x)
        else:
            loss, dx, dfg = out_fwd_loss(y, w_out_l, x, final_g[None, :], tgt)
    raw = {k: [None] * depth for k in ("gain", "a", "r", "n", "wa", "wx", "ws", "bs")}
    zero = None
    for l in reversed(range(depth)):
        sv = saved[l]
        dy, ddg, do1, do4, do16, dl1, dl4, dl16 = out_bwd(dx, sv["w_out"], sv["z"], sv["o"])
        g_w_out = grad_w_out(sv["y"], dx)
        dqkv = []
        for q, do, lse, dl, d in zip(sv["qkvs"], (do1, _flat(do4), _flat(do16)), sv["lses"],
                                     (dl1, _flat(dl4), _flat(dl16)), PATTERN_DILS):
            g = attn_bwd(q, do, lse, dl, d)
            dqkv.append(g if d == 1 else _split(g, d))
        dz, ga, gr, gn, gwa, gwx, gws, gbs = mix_bwd(dy, sv["z"], sv["hl"], sv["lru"], dqkv, ddg, sv["wts"], l)
        gain = sv["gain"] if zero is None else sv["gain"] + zero
        zero = projections_done(l, *grad_w_in(sv["h"], dz, chip), g_w_out)
        if l == 0 and zero is not None:
            gain = gain + zero
        dx, dgn = in_bwd(dz, sv["w_in"], sv["x"], gain, dx)
        for k, g in zip(("gain", "a", "r", "n", "wa", "wx", "ws", "bs"), (dgn, ga, gr, gn, gwa, gwx, gws, gbs)):
            raw[k][l] = g
    st = {k: jnp.stack(v) for k, v in raw.items()}
    eye = jnp.eye(N_HEADS, dtype=F32)[None, :, None, :, None]
    diag = lambda g: (g.reshape(depth, N_HEADS, HEAD_DIM, N_HEADS, HEAD_DIM) * eye).sum(axis=3)
    grads = dict(
        norm_g=st["gain"][:, 0], conv_a_w=st["a"][:, :3], conv_r_w=st["r"][:, :4], conv_r_b=st["r"][:, 4],
        lru_ba=st["r"][:, 5], lru_bx=st["r"][:, 6], lru_lambda=st["r"][:, 7], gmlp_norm_g=st["n"][:, 0],
        lru_wa=diag(st["wa"]), lru_wx=diag(st["wx"]), gmlp_ws=st["ws"],
        gmlp_bs=jnp.swapaxes(st["bs"].reshape(depth, GMLP_CHUNK, N_HEADS, HEAD_DIM).sum(-1), 1, 2),
        final_g=dfg[0])
    return loss, dx, grads


MESH = pl.DeviceIdType.MESH
N_CHIPS = 4
N_DEV = 8
ANY = pl.BlockSpec(memory_space=pl.ANY)


def _place():
    x, y, c = lax.axis_index("x"), lax.axis_index("y"), lax.axis_index("c")
    chips = [(1 - x, y), (x, 1 - y), (1 - x, 1 - y)]
    return x, y, c, chips


def _remote(src, dst, ssem, rsem, to):
    return pltpu.make_async_remote_copy(src_ref=src, dst_ref=dst, send_sem=ssem, recv_sem=rsem,
                                        device_id=to, device_id_type=MESH)


HBM = pl.BlockSpec(memory_space=pltpu.HBM)
SEM = pl.BlockSpec(memory_space=pltpu.SEMAPHORE)
DATAFLOW = pltpu.SideEffectType.DATAFLOW_SIDE_EFFECTING
GATHER, SCATTER = "gather", "scatter"


def _chip_copies(mode, src_refs, land_refs, ssem, rsem):
    x, y, c, chips = _place()
    me = 2 * x + y
    n = len(src_refs)
    copies = []
    for k, (cx, cy) in enumerate(chips):
        for a in range(n):
            if mode == GATHER:
                src, dst = src_refs[a], land_refs[a].at[me]
            else:
                src, dst = src_refs[a].at[2 * cx + cy], land_refs[a].at[k]
            copies.append(_remote(src, dst, ssem.at[n * k + a], rsem.at[n * k + a], (cx, cy, c)))
    return copies


def exchange_start(mode, srcs, after, name):
    n = len(srcs)
    if mode == GATHER:
        lands = [lax.empty((N_CHIPS,) + s.shape, s.dtype) for s in srcs]
    else:
        lands = [lax.empty((N_CHIPS - 1,) + s.shape[1:], s.dtype) for s in srcs]
    extra = [] if after is None else [after]

    def body(*refs):
        src_refs, land_refs = refs[:n], refs[n:2 * n]
        ssem, rsem = refs[2 * n + len(extra)], refs[2 * n + len(extra) + 1]
        token = refs[-1]
        for cp in _chip_copies(mode, src_refs, land_refs, ssem, rsem):
            cp.start()
        token[...] = jnp.zeros_like(token)

    arrays = list(srcs) + lands
    return pl.pallas_call(
        body, name=name,
        out_shape=(pltpu.SemaphoreType.DMA((3 * n,)), pltpu.SemaphoreType.DMA((3 * n,)),
                   *[pltpu.HBM(a.shape, a.dtype) for a in arrays], jax.ShapeDtypeStruct((SUBLANES, LANES), F32)),
        in_specs=[HBM] * (2 * n) + [ANY] * len(extra),
        out_specs=(SEM, SEM, *[HBM] * (2 * n), pl.BlockSpec(memory_space=pltpu.VMEM)),
        input_output_aliases={i: 2 + i for i in range(2 * n)},
        compiler_params=pltpu.CompilerParams(has_side_effects=DATAFLOW),
    )(*[pltpu.with_memory_space_constraint(a, pltpu.HBM) for a in arrays], *extra)


def exchange_wait(mode, started, after, name):
    ssem, rsem, *thru, _ = started
    n = len(thru) // 2

    def body(*refs):
        src_refs, land_refs = refs[:n], refs[n:2 * n]
        ssem_ref, rsem_ref = refs[2 * n], refs[2 * n + 1]
        for cp in _chip_copies(mode, src_refs, land_refs, ssem_ref, rsem_ref):
            cp.wait_send()
            cp.wait_recv()

    outs = pl.pallas_call(
        body, name=name,
        out_shape=[pltpu.HBM(a.shape, a.dtype) for a in thru],
        in_specs=[HBM] * (2 * n) + [SEM, SEM, ANY],
        out_specs=[HBM] * (2 * n),
        input_output_aliases={i: i for i in range(2 * n)},
        compiler_params=pltpu.CompilerParams(has_side_effects=DATAFLOW),
    )(*thru, ssem, rsem, after)
    return outs[n:]


def sibling_exchange(p1, p2):
    def body(p1_ref, p2_ref, q1_ref, q2_ref, ssem, rsem):
        x, y, c, _ = _place()
        copies = [_remote(p_ref, q_ref, ssem.at[a], rsem.at[a], (x, y, 1 - c))
                  for a, (p_ref, q_ref) in enumerate(((p1_ref, q1_ref), (p2_ref, q2_ref)))]
        for cp in copies:
            cp.start()
        for cp in copies:
            cp.wait()

    return pl.pallas_call(
        body, name="sibling_exchange",
        in_specs=[ANY, ANY], out_specs=[ANY, ANY],
        out_shape=[jax.ShapeDtypeStruct(p.shape, p.dtype) for p in (p1, p2)],
        scratch_shapes=[pltpu.SemaphoreType.DMA((2,)), pltpu.SemaphoreType.DMA((2,))],
    )(p1, p2)


def all_reduce_small(v):
    r, n = v.shape
    piece = r // N_DEV

    def body(x_ref, out_ref, recv, ssem1, rsem1, ssem2, rsem2):
        x, y, c, _ = _place()
        me = 4 * x + 2 * y + c

        def peer(k):
            px = 1 - x if (k >> 2) & 1 else x
            py = 1 - y if (k >> 1) & 1 else y
            pc = 1 - c if k & 1 else c
            return (px, py, pc), 4 * px + 2 * py + pc

        def rows(ref, d):
            return ref.at[pl.ds(d * piece, piece), :]

        scatter = []
        for k in range(1, N_DEV):
            to, idx = peer(k)
            scatter.append(_remote(rows(x_ref, idx), recv.at[k], ssem1.at[k - 1], rsem1.at[k - 1], to))
            scatter[-1].start()
        acc = rows(x_ref, me)[...]
        for k in range(1, N_DEV):
            scatter[k - 1].wait_recv()
            acc = acc + recv[k]
        rows(out_ref, me)[...] = acc

        gather = []
        for k in range(1, N_DEV):
            to, _ = peer(k)
            gather.append(_remote(rows(out_ref, me), rows(out_ref, me), ssem2.at[k - 1], rsem2.at[k - 1], to))
            gather[-1].start()
        for k in range(1, N_DEV):
            to, idx = peer(k)
            _remote(rows(out_ref, idx), rows(out_ref, idx), ssem2.at[k - 1], rsem2.at[k - 1], to).wait_recv()
        for cp in scatter + gather:
            cp.wait_send()

    return pl.pallas_call(
        body, name="all_reduce_small",
        out_shape=jax.ShapeDtypeStruct((r, n), v.dtype),
        in_specs=[pl.BlockSpec(memory_space=pltpu.VMEM)],
        out_specs=pl.BlockSpec(memory_space=pltpu.VMEM),
        scratch_shapes=[pltpu.VMEM((N_DEV, piece, n), v.dtype)] + [pltpu.SemaphoreType.DMA((N_DEV - 1,))] * 4,
        compiler_params=pltpu.CompilerParams(vmem_limit_bytes=VMEM_LIMIT),
    )(v)


TILE_ROWS = 256


def _row_tile(r):
    return max(t for t in range(SUBLANES, TILE_ROWS + 1, SUBLANES) if r % t == 0)


def sum_partials(owns, parts):
    k, r, c = parts[0].shape
    depth = len(owns)
    tile = _row_tile(r)

    def body(buf_ref, o_ref, p_ref, out_ref):
        acc = o_ref[...]
        for i in range(k):
            acc = acc + p_ref[i].astype(F32)
        out_ref[0] = acc

    out = lax.empty((depth, r, c), F32)
    for l in range(depth):
        out = pl.pallas_call(
            functools.partial(body), name="sum_partials", grid=(r // tile,),
            in_specs=[ANY, pl.BlockSpec((tile, c), lambda i: (i, 0)), pl.BlockSpec((k, tile, c), lambda i: (0, i, 0))],
            out_specs=pl.BlockSpec((1, tile, c), lambda i, l=l: (l, i, 0)),
            out_shape=jax.ShapeDtypeStruct((depth, r, c), F32),
            input_output_aliases={0: 0},
            compiler_params=_params(("parallel",)),
        )(out, owns[l], parts[l])
    return out


def _adamw_update(w, g, m, v):
    m2 = ADAM_B1 * m + (1.0 - ADAM_B1) * g
    v2 = ADAM_B2 * v + (1.0 - ADAM_B2) * (g * g)
    m_hat = m2 / (1.0 - ADAM_B1 ** ADAM_STEP)
    v_hat = v2 / (1.0 - ADAM_B2 ** ADAM_STEP)
    return -ADAM_LR * (m_hat / (jnp.sqrt(v_hat) + ADAM_EPS) + ADAM_WD * w), m2, v2


def adamw_small(ws, gs, ms, vs):
    n = len(ws)

    def body(*refs):
        ins, outs = refs[:4 * n], refs[4 * n:]
        for i in range(n):
            d, m2, v2 = _adamw_update(ins[i][...], ins[n + i][...], ins[2 * n + i][...], ins[3 * n + i][...])
            outs[3 * i][...] = d
            outs[3 * i + 1][...] = m2
            outs[3 * i + 2][...] = v2

    outs = pl.pallas_call(
        body, name="adamw_small",
        out_shape=[jax.ShapeDtypeStruct(w.shape, F32) for w in ws for _ in range(3)],
    )(*ws, *gs, *ms, *vs)
    return [tuple(outs[3 * i:3 * i + 3]) for i in range(n)]


def adamw(w, ga, gb, m, v):
    n, r, c = w.shape
    tile = _row_tile(r)

    def body(w_ref, ga_ref, gb_ref, m_ref, v_ref, g_ref, d_ref, m2_ref, v2_ref):
        g = ga_ref[...] + gb_ref[...]
        g_ref[...] = g
        d_ref[...], m2_ref[...], v2_ref[...] = _adamw_update(w_ref[...], g, m_ref[...], v_ref[...])

    spec = pl.BlockSpec((1, tile, c), lambda j, i: (j, i, 0))
    return pl.pallas_call(
        body, name="adamw", grid=(n, r // tile),
        in_specs=[spec] * 5, out_specs=[spec] * 4,
        out_shape=[jax.ShapeDtypeStruct((n, r, c), F32)] * 4,
        compiler_params=_params(("parallel", "parallel")),
    )(w, ga, gb, m, v)


REPLICATED = ("norm_g", "conv_r_b", "lru_wa", "lru_ba", "lru_wx", "lru_bx", "lru_lambda", "gmlp_norm_g",
              "gmlp_ws", "gmlp_bs", "final_g")
CHIP_SHARDED_SMALL = ("conv_a_w", "conv_r_w")
PACK_LANES = LANES


def _pack(arrays):
    flat = jnp.concatenate([a.reshape(-1) for a in arrays])
    pad = (-flat.shape[0]) % (TILE_ROWS * PACK_LANES)
    return jnp.pad(flat, (0, pad)).reshape(-1, PACK_LANES)


def _unpack(packed, shapes):
    flat = packed.reshape(-1)
    out, off = [], 0
    for shp in shapes:
        n = math.prod(shp)
        out.append(flat[off:off + n].reshape(shp))
        off += n
    return out


def kernel(x, norm_g, w_in, conv_a_w, conv_r_w, conv_r_b, lru_wa, lru_ba, lru_wx, lru_bx, lru_lambda, gmlp_norm_g, gmlp_ws, gmlp_bs, w_out, final_g, loss_target, m_norm_g, m_w_in, m_conv_a_w, m_conv_r_w, m_conv_r_b, m_lru_wa, m_lru_ba, m_lru_wx, m_lru_bx, m_lru_lambda, m_gmlp_norm_g, m_gmlp_ws, m_gmlp_bs, m_w_out, m_final_g, v_norm_g, v_w_in, v_conv_a_w, v_conv_r_w, v_conv_r_b, v_lru_wa, v_lru_ba, v_lru_wx, v_lru_bx, v_lru_lambda, v_gmlp_norm_g, v_gmlp_ws, v_gmlp_bs, v_w_out, v_final_g):
    names = ("norm_g", "w_in", "conv_a_w", "conv_r_w", "conv_r_b", "lru_wa", "lru_ba", "lru_wx", "lru_bx",
             "lru_lambda", "gmlp_norm_g", "gmlp_ws", "gmlp_bs", "w_out", "final_g")
    w = dict(zip(names, (norm_g, w_in, conv_a_w, conv_r_w, conv_r_b, lru_wa, lru_ba, lru_wx, lru_bx, lru_lambda,
                         gmlp_norm_g, gmlp_ws, gmlp_bs, w_out, final_g)))
    m = dict(zip(names, (m_norm_g, m_w_in, m_conv_a_w, m_conv_r_w, m_conv_r_b, m_lru_wa, m_lru_ba, m_lru_wx, m_lru_bx,
                         m_lru_lambda, m_gmlp_norm_g, m_gmlp_ws, m_gmlp_bs, m_w_out, m_final_g)))
    v = dict(zip(names, (v_norm_g, v_w_in, v_conv_a_w, v_conv_r_w, v_conv_r_b, v_lru_wa, v_lru_ba, v_lru_wx, v_lru_bx,
                         v_lru_lambda, v_gmlp_norm_g, v_gmlp_ws, v_gmlp_bs, v_w_out, v_final_g)))
    depth = w_in.shape[0]
    out_rows = w_out.shape[1]
    conv_ch = conv_a_w.shape[2]
    chip = 2 * lax.axis_index("x") + lax.axis_index("y")

    taps = conv_a_w.shape[1] + conv_r_w.shape[1]
    w_in_t, m_w_in_t, v_w_in_t = (jnp.swapaxes(a, 1, 2) for a in (w_in, m_w_in, v_w_in))
    w_in_h, w_out_h = w_in_t.astype(BF16), w_out.astype(BF16)
    conv_own = jnp.concatenate([conv_a_w, conv_r_w], axis=1).reshape(depth * taps, conv_ch)
    gathers, token = [], None
    for l in range(depth):
        groups = [[w_in_h[l]], [w_out_h[l], conv_own]] if l == 0 else [[w_in_h[l], w_out_h[l]]]
        gathers.append([])
        for i, srcs in enumerate(groups):
            gathers[l].append(exchange_start(GATHER, srcs, token, f"gather_start_{l}_{i}"))
            token = gathers[l][-1][-1]
    p = dict(w)

    def with_own(land, own):
        return lax.dynamic_update_slice(land, own[None], (chip,) + (0,) * own.ndim)

    def layer_weights(l, x_l):
        lands = list(exchange_wait(GATHER, gathers[l][0], x_l, f"gather_wait_{l}_0"))
        w_in_l = with_own(lands[0], w_in_h[l]).reshape(D_IN, D_MODEL)
        gain = norm_g[l][None, :]
        if l == 0:
            gain = gain + token[0, 0]

        def rest(z_l):
            if l == 0:
                lands.extend(exchange_wait(GATHER, gathers[l][1], z_l, f"gather_wait_{l}_1"))
                conv = with_own(lands[2], conv_own).reshape(N_CHIPS, depth, taps, conv_ch)
                conv = conv.transpose(1, 2, 0, 3).reshape(depth, taps, GROUP_W)
                p["conv_a_w"] = conv[:, :conv_a_w.shape[1]]
                p["conv_r_w"] = conv[:, conv_a_w.shape[1]:]
                p["prepared"] = prepare_small_weights(p)
            return with_own(lands[1], w_out_h[l]).reshape(D_MIX, D_MODEL), p["prepared"]

        return gain, w_in_l, rest

    scatters, owns = [None] * depth, [None] * depth

    def projections_done(l, g_w_in_by_chip, g_w_in_own, g_w_out):
        go = g_w_out.reshape(N_CHIPS, out_rows, D_MODEL)
        owns[l] = (g_w_in_own, lax.dynamic_index_in_dim(go, chip, axis=0, keepdims=False))
        scatters[l] = exchange_start(SCATTER, [g_w_in_by_chip, go.astype(BF16)], None, f"scatter_start_{l}")
        return scatters[l][-1][0, 0]

    loss8, dx, grads = local_step(x[0], loss_target[0], final_g, depth, chip.reshape(1), layer_weights,
                                  projections_done)

    res = {}
    small = REPLICATED + CHIP_SHARDED_SMALL
    packed = _pack([grads[k] for k in small] + [loss8[0, :1]])
    total = all_reduce_small(packed)
    *sums, loss = _unpack(total, [grads[k].shape for k in small] + [()])
    gs = dict(zip(small, sums))
    for k in CHIP_SHARDED_SMALL:
        gs[k] = lax.dynamic_slice_in_dim(gs[k], chip * conv_ch, conv_ch, axis=2)
    as2d = lambda a: a[None] if a.ndim == 1 else a
    outs = adamw_small(*[[as2d(d[k]) for k in small] for d in (w, gs, m, v)])
    for k, (delta, m2, v2) in zip(small, outs):
        res[k] = [t.reshape(w[k].shape) for t in (gs[k], delta, m2, v2)]

    parts = [exchange_wait(SCATTER, scatters[l], total, f"scatter_wait_{l}") for l in range(depth)]
    p1 = sum_partials([owns[l][0] for l in range(depth)], [parts[l][0] for l in range(depth)])
    p2 = sum_partials([owns[l][1] for l in range(depth)], [parts[l][1] for l in range(depth)])
    q1, q2 = sibling_exchange(p1, p2)
    res["w_in"] = [jnp.swapaxes(t, 1, 2) for t in adamw(w_in_t, p1, q1, m_w_in_t, v_w_in_t)]
    res["w_out"] = adamw(w_out, p2, q2, m_w_out, v_w_out)

    return (loss, dx[None], *[res[k][0] for k in names], *[res[k][1] for k in names],
            *[res[k][2] for k in names], *[res[k][3] for k in names])
```

```python
import functools
import math

import jax
import jax.numpy as jnp
import numpy as np
from jax import lax
from jax.experimental import pallas as pl
from jax.experimental.pallas import tpu as pltpu

F32 = jnp.float32
BF16 = jnp.bfloat16

D_MODEL = 1024
GROUP_W = 256
N_HEADS = 4
HEAD_DIM = 64
N_CHUNKS = 13
D_IN = N_CHUNKS * GROUP_W
D_MIX = 4 * GROUP_W
NORM_EPS = 1e-6
RG_C = 8.0
GMLP_CHUNK = 128
ATTN_BLOCK = 128
PATTERN_DILS = (1, 4, 16)
ALIBI_SLOPES = tuple(2.0 ** (-8.0 * (h + 1) / N_HEADS) for h in range(N_HEADS))
ATTN_SCALE = 1.0 / math.sqrt(HEAD_DIM)
NEG_BIG = -1e30

ADAM_LR = 0.001
ADAM_B1 = 0.9
ADAM_B2 = 0.999
ADAM_EPS = 1e-08
ADAM_WD = 0.01
ADAM_STEP = 10

C_AX, C_AB, C_AC, C_AG, C_RX, C_RG, C_CU, C_CV, C_CG, C_DQ, C_DK, C_DV, C_DG = range(13)

SUBLANES = 8
LANES = 128
VMEM_LIMIT = 56 * 1024 * 1024
TILE_IN = 512
TILE_OUT = 1024
TILE_MIX = 512
TILE_DW = 1024
TILE_DW_OUT = 2048
ATTN_QB = 16
GELU_K0 = math.sqrt(2.0 / math.pi)
GELU_K1 = 0.044715


def _params(sem):
    return pltpu.CompilerParams(dimension_semantics=sem, vmem_limit_bytes=VMEM_LIMIT)


def _sigmoid(x):
    return 0.5 * jnp.tanh(0.5 * x) + 0.5


def _silu(x):
    return x * _sigmoid(x)


def _silu_and_grad(x):
    s = _sigmoid(x)
    return x * s, s * (1.0 + x * (1.0 - s))


def _gelu(x):
    return x * (0.5 + 0.5 * jnp.tanh(x * (GELU_K0 + (GELU_K0 * GELU_K1) * (x * x))))


def _gelu_and_grad(x):
    x2 = x * x
    t = jnp.tanh(x * (GELU_K0 + (GELU_K0 * GELU_K1) * x2))
    half = 0.5 + 0.5 * t
    return x * half, half + x * (0.5 - 0.5 * t * t) * (GELU_K0 + (3.0 * GELU_K0 * GELU_K1) * x2)


def _neg_expm1_2x(x):
    t = jnp.tanh(x)
    return -2.0 * t / (1.0 - t)


def _shift_down(v, halo, k):
    r = pltpu.roll(v, k, 0)
    rh = pltpu.roll(halo, k, 0)
    row = lax.broadcasted_iota(jnp.int32, halo.shape, 0)
    top = jnp.where(row < k, rh, r[:SUBLANES])
    return jnp.concatenate([top, r[SUBLANES:]], axis=0)


def _shift_up(v, halo, k):
    t = v.shape[0]
    r = pltpu.roll(v, t - k, 0)
    rh = pltpu.roll(halo, SUBLANES - k, 0)
    row = lax.broadcasted_iota(jnp.int32, halo.shape, 0)
    bot = jnp.where(row >= SUBLANES - k, rh, r[t - SUBLANES:])
    return jnp.concatenate([r[:t - SUBLANES], bot], axis=0)


def _scan_causal(a, b, h_in):
    t = a.shape[0]
    row8 = lax.broadcasted_iota(jnp.int32, a.shape, 0) % SUBLANES
    d = 1
    while d < SUBLANES:
        m = row8 >= d
        a_s = jnp.where(m, pltpu.roll(a, d, 0), 1.0)
        b_s = jnp.where(m, pltpu.roll(b, d, 0), 0.0)
        b = a * b_s + b
        a = a * a_s
        d *= 2
    out, carry = [], h_in
    for g in range(t // SUBLANES):
        sl = slice(g * SUBLANES, (g + 1) * SUBLANES)
        hg = b[sl] + a[sl] * carry
        out.append(hg)
        carry = hg[SUBLANES - 1:SUBLANES]
    return jnp.concatenate(out, axis=0)


def _scan_anticausal(a, b, g_in):
    t = a.shape[0]
    row8 = lax.broadcasted_iota(jnp.int32, a.shape, 0) % SUBLANES
    d = 1
    while d < SUBLANES:
        m = row8 < SUBLANES - d
        a_s = jnp.where(m, pltpu.roll(a, t - d, 0), 1.0)
        b_s = jnp.where(m, pltpu.roll(b, t - d, 0), 0.0)
        b = a * b_s + b
        a = a * a_s
        d *= 2
    out, carry = [], g_in
    for g in reversed(range(t // SUBLANES)):
        sl = slice(g * SUBLANES, (g + 1) * SUBLANES)
        gg = b[sl] + a[sl] * carry
        out.append(gg)
        carry = gg[0:1]
    return jnp.concatenate(out[::-1], axis=0)


def _head_of_lane(shape):
    return lax.broadcasted_iota(jnp.int32, shape, len(shape) - 1) // HEAD_DIM


def _per_head_lanes(cols):
    t = cols[0].shape[0]
    lane = lax.broadcasted_iota(jnp.int32, (t, LANES), 1)
    out = jnp.zeros((t, LANES), F32)
    for h, col in enumerate(cols):
        out = jnp.where(lane == h, col, out)
    return out


def _put_row(acc_shape, k, row_vec):
    row = lax.broadcasted_iota(jnp.int32, acc_shape, 0)
    return jnp.where(row == k, jnp.broadcast_to(row_vec, acc_shape), 0.0)


def _dot(a, b):
    return jnp.dot(a, b, preferred_element_type=F32)


def _dot_nt(a, b):
    return lax.dot_general(a, b, (((1,), (1,)), ((), ())), preferred_element_type=F32)


def _dot_tn(a, b):
    return lax.dot_general(a, b, (((0,), (0,)), ((), ())), preferred_element_type=F32)


def _deinterleave_store(val, stage, outs):
    t, c = val.shape
    for hh in range(c // LANES):
        stage[hh][...] = val[:, hh * LANES:(hh + 1) * LANES].astype(F32)
    for dil, ref in outs:
        for r in range(dil):
            for hh in range(c // LANES):
                ref[r, :, hh * LANES:(hh + 1) * LANES] = stage[hh][pl.ds(r, t // dil, stride=dil), :].astype(ref.dtype)


def _interleave_load(ref, dil, stage):
    _, n, c = ref.shape
    for r in range(dil):
        for hh in range(c // LANES):
            stage[hh][pl.ds(r, n, stride=dil), :] = ref[r, :, hh * LANES:(hh + 1) * LANES].astype(F32)
    return jnp.concatenate([stage[hh][...] for hh in range(c // LANES)], axis=1)


def _stage_scratch(tile, cols, copies):
    return [pltpu.VMEM((tile, LANES), F32)] * (copies * (cols // LANES))


def _by_residue(s, dil, cols, dtype):
    return jax.ShapeDtypeStruct((dil, s // dil, cols), dtype)


def _residue_block(dil, tile, cols):
    return pl.BlockSpec((dil, tile // dil, cols), lambda i: (0, i, 0))


def in_fwd(x, g, w):
    s = x.shape[0]
    qkv_w = 3 * GROUP_W

    def body(x_ref, g_ref, w_ref, z_ref, h_ref, qkv1_ref, qkv4_ref, qkv16_ref, *stage):
        xv = x_ref[...]
        rs = lax.rsqrt(jnp.mean(xv * xv, axis=-1, keepdims=True) + NORM_EPS)
        h = (xv * rs * g_ref[...]).astype(BF16)
        h_ref[...] = h
        z = _dot_nt(h, w_ref[...])
        z_ref[...] = z
        qkv = z[:, C_DQ * GROUP_W:(C_DV + 1) * GROUP_W]
        qkv1_ref[...] = qkv.astype(BF16)
        _deinterleave_store(qkv, stage, ((PATTERN_DILS[1], qkv4_ref), (PATTERN_DILS[2], qkv16_ref)))

    return pl.pallas_call(
        body, name="in_fwd", grid=(s // TILE_IN,),
        in_specs=[pl.BlockSpec((TILE_IN, D_MODEL), lambda i: (i, 0)),
                  pl.BlockSpec((1, D_MODEL), lambda i: (0, 0)),
                  pl.BlockSpec((D_IN, D_MODEL), lambda i: (0, 0))],
        out_specs=[pl.BlockSpec((TILE_IN, D_IN), lambda i: (i, 0)),
                   pl.BlockSpec((TILE_IN, D_MODEL), lambda i: (i, 0)),
                   pl.BlockSpec((TILE_IN, qkv_w), lambda i: (i, 0)),
                   _residue_block(PATTERN_DILS[1], TILE_IN, qkv_w),
                   _residue_block(PATTERN_DILS[2], TILE_IN, qkv_w)],
        out_shape=[jax.ShapeDtypeStruct((s, D_IN), F32), jax.ShapeDtypeStruct((s, D_MODEL), BF16),
                   jax.ShapeDtypeStruct((s, qkv_w), BF16),
                   _by_residue(s, PATTERN_DILS[1], qkv_w, BF16), _by_residue(s, PATTERN_DILS[2], qkv_w, BF16)],
        scratch_shapes=_stage_scratch(TILE_IN, qkv_w, 1),
        compiler_params=_params(("parallel",)),
    )(x, g, w)


def out_fwd(y, w, x):
    s = x.shape[0]

    def body(y_ref, w_ref, x_ref, o_ref):
        o_ref[...] = x_ref[...] + _dot(y_ref[...], w_ref[...])

    return pl.pallas_call(
        body, name="out_fwd", grid=(s // TILE_OUT,),
        in_specs=[pl.BlockSpec((TILE_OUT, D_MIX), lambda i: (i, 0)),
                  pl.BlockSpec((D_MIX, D_MODEL), lambda i: (0, 0)),
                  pl.BlockSpec((TILE_OUT, D_MODEL), lambda i: (i, 0))],
        out_specs=pl.BlockSpec((TILE_OUT, D_MODEL), lambda i: (i, 0)),
        out_shape=jax.ShapeDtypeStruct((s, D_MODEL), F32),
        compiler_params=_params(("parallel",)),
    )(y, w, x)


def out_bwd(dx, w, z, o):
    s = dx.shape[0]
    abc = 3 * GROUP_W

    def body(dx_ref, w_ref, dg_ref, o_ref, dy_ref, ddg_ref, do1_ref, do4_ref, do16_ref, dl1_ref, dl4_ref, dl16_ref,
             *stage):
        stage_a, stage_b = stage[:2], stage[2:]
        dy = _dot_nt(dx_ref[...].astype(BF16), w_ref[...])
        dy_ref[...] = dy[:, :abc]
        dyd = dy[:, abc:]
        head = _head_of_lane((TILE_OUT, GROUP_W))
        dg = dg_ref[...]
        o = o_ref[...]
        sg, dsg = _silu_and_grad(dg)
        do = dyd * sg
        ddg_ref[...] = dyd * o * dsg
        prod = do * o
        dl = _per_head_lanes([jnp.sum(jnp.where(head == h, prod, 0.0), axis=-1, keepdims=True)
                              for h in range(N_HEADS)])
        do1_ref[...] = do.astype(BF16)
        dl1_ref[...] = dl
        _deinterleave_store(do, stage_a, ((PATTERN_DILS[1], do4_ref), (PATTERN_DILS[2], do16_ref)))
        _deinterleave_store(dl, stage_b, ((PATTERN_DILS[1], dl4_ref), (PATTERN_DILS[2], dl16_ref)))

    row = pl.BlockSpec((TILE_OUT, GROUP_W), lambda i: (i, 0))
    r4 = _residue_block(PATTERN_DILS[1], TILE_OUT, GROUP_W)
    r16 = _residue_block(PATTERN_DILS[2], TILE_OUT, GROUP_W)
    crow = pl.BlockSpec((TILE_OUT, LANES), lambda i: (i, 0))
    c4 = _residue_block(PATTERN_DILS[1], TILE_OUT, LANES)
    c16 = _residue_block(PATTERN_DILS[2], TILE_OUT, LANES)
    return pl.pallas_call(
        body, name="out_bwd", grid=(s // TILE_OUT,),
        in_specs=[pl.BlockSpec((TILE_OUT, D_MODEL), lambda i: (i, 0)),
                  pl.BlockSpec((D_MIX, D_MODEL), lambda i: (0, 0)),
                  pl.BlockSpec((TILE_OUT, GROUP_W), lambda i: (i, C_DG)), row],
        out_specs=[pl.BlockSpec((TILE_OUT, abc), lambda i: (i, 0)), row, row, r4, r16, crow, c4, c16],
        out_shape=[jax.ShapeDtypeStruct((s, abc), F32), jax.ShapeDtypeStruct((s, GROUP_W), F32),
                   jax.ShapeDtypeStruct((s, GROUP_W), BF16),
                   _by_residue(s, PATTERN_DILS[1], GROUP_W, BF16), _by_residue(s, PATTERN_DILS[2], GROUP_W, BF16),
                   jax.ShapeDtypeStruct((s, LANES), F32),
                   _by_residue(s, PATTERN_DILS[1], LANES, F32), _by_residue(s, PATTERN_DILS[2], LANES, F32)],
        scratch_shapes=_stage_scratch(TILE_OUT, GROUP_W, 1) + _stage_scratch(TILE_OUT, LANES, 1),
        compiler_params=_params(("parallel",)),
    )(dx, w, z, o)


def in_bwd(dz, w, x, g, dx_next):
    s = x.shape[0]

    def body(dz_ref, w_ref, x_ref, g_ref, dxn_ref, dx_ref, dg_ref):
        @pl.when(pl.program_id(0) == 0)
        def _():
            dg_ref[...] = jnp.zeros_like(dg_ref)

        dh = _dot(dz_ref[...], w_ref[...])
        xv = x_ref[...]
        rs = lax.rsqrt(jnp.mean(xv * xv, axis=-1, keepdims=True) + NORM_EPS)
        xh = xv * rs
        dg_ref[...] += _put_row(dg_ref.shape, 0, jnp.sum(dh * xh, axis=0, keepdims=True))
        dn = dh * g_ref[...]
        dx_ref[...] = dxn_ref[...] + rs * (dn - xh * jnp.mean(dn * xh, axis=-1, keepdims=True))

    return pl.pallas_call(
        body, name="in_bwd", grid=(s // TILE_IN,),
        in_specs=[pl.BlockSpec((TILE_IN, D_IN), lambda i: (i, 0)),
                  pl.BlockSpec((D_IN, D_MODEL), lambda i: (0, 0)),
                  pl.BlockSpec((TILE_IN, D_MODEL), lambda i: (i, 0)),
                  pl.BlockSpec((1, D_MODEL), lambda i: (0, 0)),
                  pl.BlockSpec((TILE_IN, D_MODEL), lambda i: (i, 0))],
        out_specs=[pl.BlockSpec((TILE_IN, D_MODEL), lambda i: (i, 0)),
                   pl.BlockSpec((SUBLANES, D_MODEL), lambda i: (0, 0))],
        out_shape=[jax.ShapeDtypeStruct((s, D_MODEL), F32), jax.ShapeDtypeStruct((SUBLANES, D_MODEL), F32)],
        compiler_params=_params(("arbitrary",)),
    )(dz, w, x, g, dx_next)


def grad_w_out(y, dx):
    s = y.shape[0]

    def body(y_ref, dx_ref, o_ref):
        @pl.when(pl.program_id(0) == 0)
        def _():
            o_ref[...] = jnp.zeros_like(o_ref)

        o_ref[...] += _dot_tn(y_ref[...], dx_ref[...].astype(BF16))

    return pl.pallas_call(
        body, name="grad_w_out", grid=(s // TILE_DW_OUT,),
        in_specs=[pl.BlockSpec((TILE_DW_OUT, D_MIX), lambda k: (k, 0)),
                  pl.BlockSpec((TILE_DW_OUT, D_MODEL), lambda k: (k, 0))],
        out_specs=pl.BlockSpec((D_MIX, D_MODEL), lambda k: (0, 0)),
        out_shape=jax.ShapeDtypeStruct((D_MIX, D_MODEL), F32),
        compiler_params=_params(("arbitrary",)),
    )(y, dx)


def grad_w_in(h, dz, chip):
    s = h.shape[0]
    rows = D_IN // N_CHIPS

    def body(chip_ref, h_ref, dz_ref, staged_ref, own_ref, acc):
        k = pl.program_id(0)

        @pl.when(k == 0)
        def _():
            acc[...] = jnp.zeros_like(acc)

        acc[...] += _dot_tn(dz_ref[...], h_ref[...])

        @pl.when(k == s // TILE_DW - 1)
        def _():
            for j in range(N_CHIPS):
                part = acc[j * rows:(j + 1) * rows, :]
                staged_ref[j] = part.astype(BF16)

                @pl.when(chip_ref[0] == j)
                def _():
                    own_ref[...] = part

    return pl.pallas_call(
        body, name="grad_w_in",
        grid_spec=pltpu.PrefetchScalarGridSpec(
            num_scalar_prefetch=1, grid=(s // TILE_DW,),
            in_specs=[pl.BlockSpec((TILE_DW, D_MODEL), lambda k, c: (k, 0)),
                      pl.BlockSpec((TILE_DW, D_IN), lambda k, c: (k, 0))],
            out_specs=[pl.BlockSpec((N_CHIPS, rows, D_MODEL), lambda k, c: (0, 0, 0)),
                       pl.BlockSpec((rows, D_MODEL), lambda k, c: (0, 0))],
            scratch_shapes=[pltpu.VMEM((D_IN, D_MODEL), F32)]),
        out_shape=[jax.ShapeDtypeStruct((N_CHIPS, rows, D_MODEL), BF16), jax.ShapeDtypeStruct((rows, D_MODEL), F32)],
        compiler_params=_params(("arbitrary",)),
    )(chip, h, dz)


def out_fwd_loss(y, w, x, g, tgt):
    s = x.shape[0]

    def body(y_ref, w_ref, x_ref, g_ref, t_ref, l_ref, dx_ref, dg_ref):
        @pl.when(pl.program_id(0) == 0)
        def _():
            l_ref[...] = jnp.zeros_like(l_ref)
            dg_ref[...] = jnp.zeros_like(dg_ref)

        xv = x_ref[...] + _dot(y_ref[...], w_ref[...])
        gv = g_ref[...]
        rs = lax.rsqrt(jnp.mean(xv * xv, axis=-1, keepdims=True) + NORM_EPS)
        xh = xv * rs
        e = xh * gv - t_ref[...]
        part = 0.5 * jnp.sum(jnp.mean(e * e, axis=-1, keepdims=True), axis=0, keepdims=True)
        l_ref[...] += jnp.broadcast_to(part, l_ref.shape)
        dy = e * (1.0 / D_MODEL)
        dg_ref[...] += _put_row(dg_ref.shape, 0, jnp.sum(dy * xh, axis=0, keepdims=True))
        dn = dy * gv
        dx_ref[...] = rs * (dn - xh * jnp.mean(dn * xh, axis=-1, keepdims=True))

    return pl.pallas_call(
        body, name="out_fwd_loss", grid=(s // TILE_OUT,),
        in_specs=[pl.BlockSpec((TILE_OUT, D_MIX), lambda i: (i, 0)),
                  pl.BlockSpec((D_MIX, D_MODEL), lambda i: (0, 0)),
                  pl.BlockSpec((TILE_OUT, D_MODEL), lambda i: (i, 0)),
                  pl.BlockSpec((1, D_MODEL), lambda i: (0, 0)),
                  pl.BlockSpec((TILE_OUT, D_MODEL), lambda i: (i, 0))],
        out_specs=[pl.BlockSpec((SUBLANES, LANES), lambda i: (0, 0)),
                   pl.BlockSpec((TILE_OUT, D_MODEL), lambda i: (i, 0)),
                   pl.BlockSpec((SUBLANES, D_MODEL), lambda i: (0, 0))],
        out_shape=[jax.ShapeDtypeStruct((SUBLANES, LANES), F32), jax.ShapeDtypeStruct((s, D_MODEL), F32),
                   jax.ShapeDtypeStruct((SUBLANES, D_MODEL), F32)],
        compiler_params=_params(("arbitrary",)),
    )(y, w, x, g, tgt)


def _attn_bias(dil):
    qi = np.arange(ATTN_BLOCK)[:, None]
    ki = np.arange(2 * ATTN_BLOCK)[None, :]
    delta = qi + ATTN_BLOCK - ki
    band = (delta >= 0) & (delta <= ATTN_BLOCK)
    out = np.empty((2, N_HEADS, ATTN_BLOCK, 2 * ATTN_BLOCK), np.float32)
    for f in range(2):
        ok = band & ((ki >= ATTN_BLOCK) | (f == 0))
        for h in range(N_HEADS):
            out[f, h] = np.where(ok, -ALIBI_SLOPES[h] * dil * delta, NEG_BIG)
    return jnp.asarray(out.reshape(2, N_HEADS * ATTN_BLOCK, 2 * ATTN_BLOCK))


def _stack_heads(a, head):
    return jnp.concatenate([jnp.where(head == h, a, jnp.zeros_like(a)) for h in range(N_HEADS)], axis=0)


def _unstack_heads(a, head):
    out = a[:ATTN_BLOCK]
    for h in range(1, N_HEADS):
        out = jnp.where(head == h, a[h * ATTN_BLOCK:(h + 1) * ATTN_BLOCK], out)
    return out


def _head_column(a):
    return jnp.concatenate([a[:, h:h + 1] for h in range(N_HEADS)], axis=0)


def _attn_specs(n_blocks):
    rows = ATTN_QB * ATTN_BLOCK
    cur = lambda c, w=GROUP_W: pl.BlockSpec((rows, w), lambda n, c=c: (n, c))
    prev = lambda c: pl.BlockSpec((ATTN_BLOCK, GROUP_W), lambda n, c=c: (jnp.maximum(n * ATTN_QB - 1, 0), c))
    nxt = lambda c, w=GROUP_W: pl.BlockSpec((ATTN_BLOCK, w),
                                            lambda n, c=c: (jnp.minimum(n * ATTN_QB + ATTN_QB, n_blocks - 1), c))
    return cur, prev, nxt


def _keys(kp_ref, k_ref, j):
    prev = kp_ref[...] if j == 0 else k_ref[(j - 1) * ATTN_BLOCK:j * ATTN_BLOCK, :]
    return jnp.concatenate([prev, k_ref[j * ATTN_BLOCK:(j + 1) * ATTN_BLOCK, :]], axis=0)


def attn_fwd(qkv, dil):
    s = qkv.shape[0]
    n_blocks = s // ATTN_BLOCK
    bps = n_blocks // dil
    rows = ATTN_QB * ATTN_BLOCK

    def body(q_ref, kp_ref, k_ref, vp_ref, v_ref, bias_ref, o_ref, lse_ref):
        n = pl.program_id(0)
        head = _head_of_lane((ATTN_BLOCK, GROUP_W))
        for j in range(ATTN_QB):
            sl = slice(j * ATTN_BLOCK, (j + 1) * ATTN_BLOCK)
            first = (((n * ATTN_QB + j) % bps) == 0).astype(jnp.int32)
            qs = _stack_heads(q_ref[sl, :], head)
            sc = _dot_nt(qs, _keys(kp_ref, k_ref, j)) * ATTN_SCALE + bias_ref[first]
            m = jnp.max(sc, axis=-1, keepdims=True)
            pr = jnp.exp(sc - m)
            l = jnp.sum(pr, axis=-1, keepdims=True)
            oh = _dot(pr.astype(BF16), _keys(vp_ref, v_ref, j)) / l
            o_ref[sl, :] = _unstack_heads(oh, head).astype(BF16)
            ml = m + jnp.log(l)
            lse_ref[sl, :] = _per_head_lanes([ml[h * ATTN_BLOCK:(h + 1) * ATTN_BLOCK] for h in range(N_HEADS)])

    cur, prev, _ = _attn_specs(n_blocks)
    bias = _attn_bias(dil)
    return pl.pallas_call(
        body, name=f"attn_fwd_d{dil}", grid=(n_blocks // ATTN_QB,),
        in_specs=[cur(0), prev(1), cur(1), prev(2), cur(2), pl.BlockSpec(bias.shape, lambda n: (0, 0, 0))],
        out_specs=[cur(0), cur(0, LANES)],
        out_shape=[jax.ShapeDtypeStruct((s, GROUP_W), BF16), jax.ShapeDtypeStruct((s, LANES), F32)],
        compiler_params=_params(("parallel",)),
    )(qkv, qkv, qkv, qkv, qkv, bias)


def attn_bwd(qkv, do, lse, dlt, dil):
    s = qkv.shape[0]
    n_blocks = s // ATTN_BLOCK
    bps = n_blocks // dil
    rows = ATTN_QB * ATTN_BLOCK

    def body(q_ref, qn_ref, kp_ref, k_ref, vp_ref, v_ref, do_ref, don_ref, lse_ref, lsen_ref, dl_ref, dln_ref,
             bias_ref, out_ref, dk_acc, dv_acc):
        n = pl.program_id(0)
        head = _head_of_lane((ATTN_BLOCK, GROUP_W))
        dk_acc[...] = jnp.zeros_like(dk_acc)
        dv_acc[...] = jnp.zeros_like(dv_acc)

        def pair(qj, doj, lsej, dlj, kk, vv, bias, keep):
            qs = _stack_heads(qj, head)
            dos = _stack_heads(doj, head)
            sc = _dot_nt(qs, kk) * ATTN_SCALE + bias
            if keep is None:
                pr = jnp.exp(sc - _head_column(lsej))
            else:
                pr = jnp.exp(jnp.minimum(sc - _head_column(lsej), 0.0)) * keep
            dp = _dot_nt(dos, vv)
            ds = (pr * (dp - _head_column(dlj)) * ATTN_SCALE).astype(BF16)
            return ds, _dot_tn(ds, qs), _dot_tn(pr.astype(BF16), dos)

        for j in range(ATTN_QB):
            sl = slice(j * ATTN_BLOCK, (j + 1) * ATTN_BLOCK)
            first = (((n * ATTN_QB + j) % bps) == 0).astype(jnp.int32)
            kk = _keys(kp_ref, k_ref, j)
            ds, dks, dvs = pair(q_ref[sl, :], do_ref[sl, :], lse_ref[sl, :], dl_ref[sl, :],
                                kk, _keys(vp_ref, v_ref, j), bias_ref[first], None)
            out_ref[sl, 0:GROUP_W] = _unstack_heads(_dot(ds, kk), head)
            acc = slice(j * ATTN_BLOCK, (j + 2) * ATTN_BLOCK)
            dk_acc[acc, :] += dks
            dv_acc[acc, :] += dvs

        nxt = n * ATTN_QB + ATTN_QB
        valid = ((nxt < n_blocks) & ((nxt % bps) != 0)).astype(F32)
        last = slice((ATTN_QB - 1) * ATTN_BLOCK, ATTN_QB * ATTN_BLOCK)
        _, dks, dvs = pair(qn_ref[...], don_ref[...], lsen_ref[...], dln_ref[...], k_ref[last, :], v_ref[last, :],
                           bias_ref[0][:, :ATTN_BLOCK], valid)
        acc = slice(ATTN_QB * ATTN_BLOCK, (ATTN_QB + 1) * ATTN_BLOCK)
        dk_acc[acc, :] += dks
        dv_acc[acc, :] += dvs
        out_ref[:, GROUP_W:2 * GROUP_W] = dk_acc[ATTN_BLOCK:, :]
        out_ref[:, 2 * GROUP_W:3 * GROUP_W] = dv_acc[ATTN_BLOCK:, :]

    cur, prev, nxt = _attn_specs(n_blocks)
    bias = _attn_bias(dil)
    return pl.pallas_call(
        body, name=f"attn_bwd_d{dil}", grid=(n_blocks // ATTN_QB,),
        in_specs=[cur(0), nxt(0), prev(1), cur(1), prev(2), cur(2), cur(0), nxt(0),
                  cur(0, LANES), nxt(0, LANES), cur(0, LANES), nxt(0, LANES),
                  pl.BlockSpec(bias.shape, lambda n: (0, 0, 0))],
        out_specs=pl.BlockSpec((rows, 3 * GROUP_W), lambda n: (n, 0)),
        out_shape=jax.ShapeDtypeStruct((s, 3 * GROUP_W), F32),
        scratch_shapes=[pltpu.VMEM(((ATTN_QB + 1) * ATTN_BLOCK, GROUP_W), F32),
                        pltpu.VMEM(((ATTN_QB + 1) * ATTN_BLOCK, GROUP_W), F32)],
        compiler_params=_params(("parallel",)),
    )(qkv, qkv, qkv, qkv, qkv, qkv, do, do, lse, lse, dlt, dlt, bias)


def _zcol(c):
    return pl.BlockSpec((TILE_MIX, GROUP_W), lambda i, c=c: (i, c))


def _zhalo(c):
    per = TILE_MIX // SUBLANES
    return pl.BlockSpec((SUBLANES, GROUP_W), lambda i, c=c: (jnp.maximum(i * per - 1, 0), c))


def _full(shape):
    return pl.BlockSpec(shape, lambda i: tuple(0 for _ in shape))


def _of_layer(a, l):
    rest = a.shape[1:]
    return pl.BlockSpec((None,) + rest, lambda i: (l,) + tuple(0 for _ in rest))


def _softplus_neg(lam):
    nl = -lam
    return jnp.maximum(nl, 0.0) + jnp.log1p(jnp.exp(-jnp.abs(nl)))


def _lru_gates(xb, wa_ref, wx_ref, ba, bx, lam):
    xbb = xb.astype(BF16)
    r = _sigmoid(_dot(xbb, wa_ref[...]) + ba)
    ig = _sigmoid(_dot(xbb, wx_ref[...]) + bx)
    log_a = (-RG_C * r) * _softplus_neg(lam)
    a = jnp.exp(log_a)
    mult = jnp.sqrt(_neg_expm1_2x(log_a))
    return r, ig, a, mult


LRU_SAVED = 5


def _gmlp_spatial(ws_ref, vvb, head):
    outs = []
    for j in range(vvb.shape[0] // GMLP_CHUNK):
        blk = vvb[j * GMLP_CHUNK:(j + 1) * GMLP_CHUNK, :]
        acc = jnp.zeros((GMLP_CHUNK, GROUP_W), F32)
        for h in range(N_HEADS):
            acc = jnp.where(head[:GMLP_CHUNK] == h, _dot(ws_ref[h], blk), acc)
        outs.append(acc)
    return jnp.concatenate(outs, axis=0)


def mix_fwd(z, attn, wts, l):
    s = z.shape[0]
    d4, d16 = PATTERN_DILS[1], PATTERN_DILS[2]

    def body(ax_ref, ab_ref, ac_ref, ag_ref, rx_ref, rg_ref, cu_ref, cv_ref, cg_ref, dg_ref,
             axh_ref, ach_ref, rxh_ref, o1_ref, l1_ref, o4_ref, l4_ref, o16_ref, l16_ref,
             caw_ref, crw_ref, crb_ref, wa_ref, wx_ref, ba_ref, bx_ref, lam_ref, gng_ref, ws_ref, bs_ref,
             y_ref, hl_ref, o_ref, lse_ref, lse4_ref, lse16_ref, lru_ref, carry, *stage):
        st_a, st_b, st_c, st_d, st_e = (stage[2 * k:2 * k + 2] for k in range(5))
        i = pl.program_id(0)

        @pl.when(i == 0)
        def _():
            carry[...] = jnp.zeros_like(carry)

        nz = (i > 0).astype(F32)
        head = _head_of_lane((TILE_MIX, GROUP_W))

        pa = ac_ref[...] * ax_ref[...]
        pah = ach_ref[...] * axh_ref[...] * nz
        cv = caw_ref[2:3, :] * pa + caw_ref[1:2, :] * _shift_down(pa, pah, 1) + caw_ref[0:1, :] * _shift_down(pa, pah, 2)
        y_ref[:, 0:GROUP_W] = (ab_ref[...] * cv * _silu(ag_ref[...])).astype(BF16)

        rx = rx_ref[...]
        rxh = rxh_ref[...] * nz
        xb = (crw_ref[3:4, :] * rx + crw_ref[2:3, :] * _shift_down(rx, rxh, 1) + crw_ref[1:2, :] * _shift_down(rx, rxh, 2)
              + crw_ref[0:1, :] * _shift_down(rx, rxh, 3) + crb_ref[...])
        r, ig, a, mult = _lru_gates(xb, wa_ref, wx_ref, ba_ref[...], bx_ref[...], lam_ref[...])
        for k, val in enumerate((xb, r, ig, a, mult)):
            lru_ref[:, k * GROUP_W:(k + 1) * GROUP_W] = val
        hl = _scan_causal(a, mult * (ig * xb), carry[SUBLANES - 1:SUBLANES, :])
        hl_ref[...] = hl
        carry[...] = hl[TILE_MIX - SUBLANES:, :]
        y_ref[:, GROUP_W:2 * GROUP_W] = (hl * _silu(rg_ref[...])).astype(BF16)

        u = _gelu(cu_ref[...])
        gv = _gelu(cv_ref[...])
        rs = lax.rsqrt(jnp.mean(gv * gv, axis=-1, keepdims=True) + NORM_EPS)
        vvb = (gv * rs * gng_ref[...]).astype(BF16)
        sp = _gmlp_spatial(ws_ref, vvb, head) + jnp.concatenate([bs_ref[...]] * (TILE_MIX // GMLP_CHUNK), axis=0)
        y_ref[:, 2 * GROUP_W:3 * GROUP_W] = (u * sp * _silu(cg_ref[...])).astype(BF16)

        ops = (o1_ref[...].astype(F32), _interleave_load(o4_ref, d4, st_a), _interleave_load(o16_ref, d16, st_b))
        lps = (l1_ref[...], _interleave_load(l4_ref, d4, st_c), _interleave_load(l16_ref, d16, st_d))
        m = jnp.maximum(jnp.maximum(lps[0], lps[1]), lps[2])
        ws = [jnp.exp(lp - m) for lp in lps]
        zsum = ws[0] + ws[1] + ws[2]
        lse = m + jnp.log(zsum)
        o = jnp.zeros((TILE_MIX, GROUP_W), F32)
        for op, w in zip(ops, ws):
            wn = w / zsum
            wide = jnp.zeros((TILE_MIX, GROUP_W), F32)
            for h in range(N_HEADS):
                wide = jnp.where(head == h, wn[:, h:h + 1], wide)
            o = o + wide * op
        o_ref[...] = o
        lse_ref[...] = lse
        _deinterleave_store(lse, st_e, ((d4, lse4_ref), (d16, lse16_ref)))
        y_ref[:, 3 * GROUP_W:4 * GROUP_W] = (o * _silu(dg_ref[...])).astype(BF16)

    row = pl.BlockSpec((TILE_MIX, GROUP_W), lambda i: (i, 0))
    r4 = _residue_block(d4, TILE_MIX, GROUP_W)
    r16 = _residue_block(d16, TILE_MIX, GROUP_W)
    crow = pl.BlockSpec((TILE_MIX, LANES), lambda i: (i, 0))
    c4 = _residue_block(d4, TILE_MIX, LANES)
    c16 = _residue_block(d16, TILE_MIX, LANES)
    names = ("caw", "crw", "crb", "wa", "wx", "ba", "bx", "lam", "gng", "ws", "bs")
    in_specs = ([_zcol(c) for c in (C_AX, C_AB, C_AC, C_AG, C_RX, C_RG, C_CU, C_CV, C_CG, C_DG)]
                + [_zhalo(C_AX), _zhalo(C_AC), _zhalo(C_RX), row, crow, r4, c4, r16, c16]
                + [_of_layer(wts[k], l) for k in names])
    return pl.pallas_call(
        body, name="mix_fwd", grid=(s // TILE_MIX,),
        in_specs=in_specs,
        out_specs=[pl.BlockSpec((TILE_MIX, D_MIX), lambda i: (i, 0)), row, row, crow, c4, c16,
                   pl.BlockSpec((TILE_MIX, LRU_SAVED * GROUP_W), lambda i: (i, 0))],
        out_shape=([jax.ShapeDtypeStruct((s, D_MIX), BF16)] + [jax.ShapeDtypeStruct((s, GROUP_W), F32)] * 2
                   + [jax.ShapeDtypeStruct((s, LANES), F32), _by_residue(s, d4, LANES, F32),
                      _by_residue(s, d16, LANES, F32), jax.ShapeDtypeStruct((s, LRU_SAVED * GROUP_W), F32)]),
        scratch_shapes=([pltpu.VMEM((SUBLANES, GROUP_W), F32)] + _stage_scratch(TILE_MIX, GROUP_W, 4)
                        + _stage_scratch(TILE_MIX, LANES, 1)),
        compiler_params=_params(("arbitrary",)),
    )(*([z] * 13), *[a for pair in attn for a in pair], *[wts[k] for k in names])


def mix_bwd(dy, z, hl, lru, dqkv, ddg, wts, l):
    s = z.shape[0]
    d4, d16 = PATTERN_DILS[1], PATTERN_DILS[2]
    n_tiles = s // TILE_MIX

    def body(dya_ref, dyb_ref, dyc_ref, ax_ref, ab_ref, ac_ref, ag_ref, rx_ref, rg_ref, cu_ref, cv_ref, cg_ref,
             axh_ref, ach_ref, rxh_ref, hl_ref, hlh_ref, lru_ref, dqkv1_ref, dqkv4_ref, dqkv16_ref, ddg_ref,
             caw_ref, crw_ref, crb_ref, wa_ref, wx_ref, ba_ref, bx_ref, lam_ref, gng_ref, ws_ref, wst_ref, bs_ref,
             dz_ref, ga_ref, gr_ref, gn_ref, gwa_ref, gwx_ref, gws_ref, gbs_ref,
             c_dcv, c_g, c_a, c_dxb, *stage):
        st_a, st_b = stage[:len(stage) // 2], stage[len(stage) // 2:]
        step = pl.program_id(0)
        i = n_tiles - 1 - step

        @pl.when(step == 0)
        def _():
            for r in (c_dcv, c_g, c_a, c_dxb, ga_ref, gr_ref, gn_ref, gwa_ref, gwx_ref, gws_ref, gbs_ref):
                r[...] = jnp.zeros_like(r)

        nz = (i > 0).astype(F32)
        head = _head_of_lane((TILE_MIX, GROUP_W))
        shp8 = (SUBLANES, GROUP_W)
        colsum = lambda v: jnp.sum(v, axis=0, keepdims=True)

        ax, ab, ac, ag = ax_ref[...], ab_ref[...], ac_ref[...], ag_ref[...]
        dya = dya_ref[...]
        pa = ac * ax
        pah = ach_ref[...] * axh_ref[...] * nz
        pa1 = _shift_down(pa, pah, 1)
        pa2 = _shift_down(pa, pah, 2)
        cv = caw_ref[2:3, :] * pa + caw_ref[1:2, :] * pa1 + caw_ref[0:1, :] * pa2
        sg, dsg = _silu_and_grad(ag)
        dz_ref[:, C_AB * GROUP_W:(C_AB + 1) * GROUP_W] = (dya * cv * sg).astype(BF16)
        dz_ref[:, C_AG * GROUP_W:(C_AG + 1) * GROUP_W] = (dya * ab * cv * dsg).astype(BF16)
        dcv = dya * ab * sg
        nxt = c_dcv[...]
        dpa = caw_ref[2:3, :] * dcv + caw_ref[1:2, :] * _shift_up(dcv, nxt, 1) + caw_ref[0:1, :] * _shift_up(dcv, nxt, 2)
        c_dcv[...] = dcv[:SUBLANES, :]
        dz_ref[:, C_AC * GROUP_W:(C_AC + 1) * GROUP_W] = (dpa * ax).astype(BF16)
        dz_ref[:, C_AX * GROUP_W:(C_AX + 1) * GROUP_W] = (dpa * ac).astype(BF16)
        ga_ref[...] += (_put_row(shp8, 2, colsum(dcv * pa)) + _put_row(shp8, 1, colsum(dcv * pa1))
                        + _put_row(shp8, 0, colsum(dcv * pa2)))

        rx, rg = rx_ref[...], rg_ref[...]
        dyb = dyb_ref[...]
        rxh = rxh_ref[...] * nz
        rx1, rx2, rx3 = _shift_down(rx, rxh, 1), _shift_down(rx, rxh, 2), _shift_down(rx, rxh, 3)
        xb, r, ig, a, mult = (lru_ref[:, k * GROUP_W:(k + 1) * GROUP_W] for k in range(LRU_SAVED))
        lam = lam_ref[...]
        sp = _softplus_neg(lam)
        hl = hl_ref[...]
        hprev = _shift_down(hl, hlh_ref[...] * nz, 1)
        sgr, dsgr = _silu_and_grad(rg)
        dz_ref[:, C_RG * GROUP_W:(C_RG + 1) * GROUP_W] = (dyb * hl * dsgr).astype(BF16)
        dh = dyb * sgr
        a_next = _shift_up(a, c_a[...], 1)
        g = _scan_anticausal(a_next, dh, c_g[0:1, :])
        c_g[...] = g[:SUBLANES, :]
        c_a[...] = a[:SUBLANES, :]
        u = ig * xb
        da = g * hprev
        dmult = g * u
        du = g * mult
        dlog_a = da * a - dmult * (a * a) / mult
        dr = dlog_a * (-RG_C * sp)
        dga = dr * r * (1.0 - r)
        dgx = (du * xb) * ig * (1.0 - ig)
        dgab, dgxb = dga.astype(BF16), dgx.astype(BF16)
        dxb = du * ig + _dot_nt(dgab, wa_ref[...]) + _dot_nt(dgxb, wx_ref[...])
        xbb = xb.astype(BF16)
        gwa_ref[...] += _dot_tn(xbb, dgab)
        gwx_ref[...] += _dot_tn(xbb, dgxb)
        nxt = c_dxb[...]
        drx = (crw_ref[3:4, :] * dxb + crw_ref[2:3, :] * _shift_up(dxb, nxt, 1) + crw_ref[1:2, :] * _shift_up(dxb, nxt, 2)
               + crw_ref[0:1, :] * _shift_up(dxb, nxt, 3))
        c_dxb[...] = dxb[:SUBLANES, :]
        dz_ref[:, C_RX * GROUP_W:(C_RX + 1) * GROUP_W] = drx.astype(BF16)
        dlam = colsum(dlog_a * (-RG_C * r)) * (-_sigmoid(-lam))
        gr_ref[...] += (_put_row(shp8, 3, colsum(dxb * rx)) + _put_row(shp8, 2, colsum(dxb * rx1))
                        + _put_row(shp8, 1, colsum(dxb * rx2)) + _put_row(shp8, 0, colsum(dxb * rx3))
                        + _put_row(shp8, 4, colsum(dxb)) + _put_row(shp8, 5, colsum(dga))
                        + _put_row(shp8, 6, colsum(dgx)) + _put_row(shp8, 7, dlam))

        cu, cvv, cg = cu_ref[...], cv_ref[...], cg_ref[...]
        dyc = dyc_ref[...]
        u_c, du_c = _gelu_and_grad(cu)
        gv, dgv_c = _gelu_and_grad(cvv)
        rs = lax.rsqrt(jnp.mean(gv * gv, axis=-1, keepdims=True) + NORM_EPS)
        vh = gv * rs
        gng = gng_ref[...]
        vvb = (vh * gng).astype(BF16)
        spat = _gmlp_spatial(ws_ref, vvb, head) + jnp.concatenate([bs_ref[...]] * (TILE_MIX // GMLP_CHUNK), axis=0)
        sgc, dsgc = _silu_and_grad(cg)
        dz_ref[:, C_CU * GROUP_W:(C_CU + 1) * GROUP_W] = (dyc * spat * sgc * du_c).astype(BF16)
        dz_ref[:, C_CG * GROUP_W:(C_CG + 1) * GROUP_W] = (dyc * u_c * spat * dsgc).astype(BF16)
        dsp = dyc * u_c * sgc
        dspb = dsp.astype(BF16)
        tril = (lax.broadcasted_iota(jnp.int32, (GMLP_CHUNK, GMLP_CHUNK), 0)
                >= lax.broadcasted_iota(jnp.int32, (GMLP_CHUNK, GMLP_CHUNK), 1))
        head_c = head[:GMLP_CHUNK]
        dvv_parts = []
        gbs = jnp.zeros((GMLP_CHUNK, GROUP_W), F32)
        for j in range(TILE_MIX // GMLP_CHUNK):
            sl = slice(j * GMLP_CHUNK, (j + 1) * GMLP_CHUNK)
            dblk = dspb[sl, :]
            vblk = vvb[sl, :]
            gbs = gbs + dsp[sl, :]
            acc = jnp.zeros((GMLP_CHUNK, GROUP_W), F32)
            for h in range(N_HEADS):
                acc = jnp.where(head_c == h, _dot(wst_ref[h], dblk), acc)
                dm = jnp.where(head_c == h, dblk, jnp.zeros_like(dblk))
                gws_ref[h] += jnp.where(tril, _dot_nt(dm, vblk), 0.0)
            dvv_parts.append(acc)
        gbs_ref[...] += gbs
        dvv = jnp.concatenate(dvv_parts, axis=0)
        gn_ref[...] += _put_row(shp8, 0, colsum(dvv * vh))
        dvh = dvv * gng
        dgv = rs * (dvh - vh * jnp.mean(dvh * vh, axis=-1, keepdims=True))
        dz_ref[:, C_CV * GROUP_W:(C_CV + 1) * GROUP_W] = (dgv * dgv_c).astype(BF16)

        dsum = dqkv1_ref[...] + _interleave_load(dqkv4_ref, d4, st_a) + _interleave_load(dqkv16_ref, d16, st_b)
        dz_ref[:, C_DQ * GROUP_W:(C_DV + 1) * GROUP_W] = dsum.astype(BF16)
        dz_ref[:, C_DG * GROUP_W:(C_DG + 1) * GROUP_W] = ddg_ref[...].astype(BF16)

    per = TILE_MIX // SUBLANES
    qkv_w = 3 * GROUP_W
    rev = lambda c: pl.BlockSpec((TILE_MIX, GROUP_W), lambda t, c=c: (n_tiles - 1 - t, c))
    revh = lambda c: pl.BlockSpec((SUBLANES, GROUP_W),
                                  lambda t, c=c: (jnp.maximum((n_tiles - 1 - t) * per - 1, 0), c))
    revr = lambda dil: pl.BlockSpec((dil, TILE_MIX // dil, qkv_w), lambda t: (0, n_tiles - 1 - t, 0))
    names = ("caw", "crw", "crb", "wa", "wx", "ba", "bx", "lam", "gng", "ws", "wst", "bs")
    in_specs = ([rev(0), rev(1), rev(2)]
                + [rev(c) for c in (C_AX, C_AB, C_AC, C_AG, C_RX, C_RG, C_CU, C_CV, C_CG)]
                + [revh(C_AX), revh(C_AC), revh(C_RX), rev(0), revh(0),
                   pl.BlockSpec((TILE_MIX, LRU_SAVED * GROUP_W), lambda t: (n_tiles - 1 - t, 0)),
                   pl.BlockSpec((TILE_MIX, qkv_w), lambda t: (n_tiles - 1 - t, 0)), revr(d4), revr(d16), rev(0)]
                + [_of_layer(wts[k], l) for k in names])
    small = jax.ShapeDtypeStruct((SUBLANES, GROUP_W), F32)
    sq = jax.ShapeDtypeStruct((GROUP_W, GROUP_W), F32)
    out_shape = [jax.ShapeDtypeStruct((s, D_IN), BF16), small, small, small, sq, sq,
                 jax.ShapeDtypeStruct((N_HEADS, GMLP_CHUNK, GMLP_CHUNK), F32),
                 jax.ShapeDtypeStruct((GMLP_CHUNK, GROUP_W), F32)]
    out_specs = ([pl.BlockSpec((TILE_MIX, D_IN), lambda t: (n_tiles - 1 - t, 0))]
                 + [_full(o.shape) for o in out_shape[1:]])
    return pl.pallas_call(
        body, name="mix_bwd", grid=(n_tiles,),
        in_specs=in_specs, out_specs=out_specs, out_shape=out_shape,
        scratch_shapes=[pltpu.VMEM((SUBLANES, GROUP_W), F32)] * 4 + _stage_scratch(TILE_MIX, qkv_w, 2),
        compiler_params=_params(("arbitrary",)),
    )(dy, dy, dy, *([z] * 12), hl, hl, lru, *dqkv, ddg, *[wts[k] for k in names])


def prepare_small_weights(p):
    tril = jnp.tril(jnp.ones((GMLP_CHUNK, GMLP_CHUNK), dtype=bool))
    ws = jnp.where(tril, p["gmlp_ws"], 0.0).astype(BF16)
    row = lambda a: a[:, None, :]
    eye = jnp.eye(N_HEADS, dtype=F32)
    bd = lambda w: (w[:, :, :, None, :] * eye[None, :, None, :, None]).reshape(-1, GROUP_W, GROUP_W).astype(BF16)
    return dict(
        caw=p["conv_a_w"], crw=p["conv_r_w"], crb=row(p["conv_r_b"]),
        wa=bd(p["lru_wa"]), wx=bd(p["lru_wx"]),
        ba=row(p["lru_ba"]), bx=row(p["lru_bx"]), lam=row(p["lru_lambda"]), gng=row(p["gmlp_norm_g"]),
        ws=ws, wst=jnp.swapaxes(ws, 2, 3),
        bs=jnp.repeat(jnp.swapaxes(p["gmlp_bs"], 1, 2), HEAD_DIM, axis=2))


def _flat(a):
    return a.reshape(a.shape[0] * a.shape[1], a.shape[2])


def _split(a, dil):
    return a.reshape(dil, a.shape[0] // dil, a.shape[1])


def local_step(x, tgt, final_g, depth, chip, layer_weights, projections_done):
    saved = []
    for l in range(depth):
        gain, w_in_l, rest = layer_weights(l, x)
        z, h, *qkvs = in_fwd(x, gain, w_in_l)
        w_out_l, wts = rest(z)
        qkvs = [_flat(q) if q.ndim == 3 else q for q in qkvs]
        attn = []
        for q, d in zip(qkvs, PATTERN_DILS):
            o_p, lse_p = attn_fwd(q, d)
            attn.append((o_p, lse_p) if d == 1 else (_split(o_p, d), _split(lse_p, d)))
        y, hl, o, lse, lse4, lse16, lru = mix_fwd(z, attn, wts, l)
        saved.append(dict(x=x, z=z, h=h, y=y, hl=hl, o=o, qkvs=qkvs, lses=(lse, _flat(lse4), _flat(lse16)), wts=wts, lru=lru,
                          gain=gain, w_in=w_in_l, w_out=w_out_l))
        if l < depth - 1:
            x = out_fwd(y, w_out_l, x)
        else:
            loss, dx, dfg = out_fwd_loss(y, w_out_l, x, final_g[None, :], tgt)
    raw = {k: [None] * depth for k in ("gain", "a", "r", "n", "wa", "wx", "ws", "bs")}
    zero = None
    for l in reversed(range(depth)):
        sv = saved[l]
        dy, ddg, do1, do4, do16, dl1, dl4, dl16 = out_bwd(dx, sv["w_out"], sv["z"], sv["o"])
        g_w_out = grad_w_out(sv["y"], dx)
        dqkv = []
        for q, do, lse, dl, d in zip(sv["qkvs"], (do1, _flat(do4), _flat(do16)), sv["lses"],
                                     (dl1, _flat(dl4), _flat(dl16)), PATTERN_DILS):
            g = attn_bwd(q, do, lse, dl, d)
            dqkv.append(g if d == 1 else _split(g, d))
        dz, ga, gr, gn, gwa, gwx, gws, gbs = mix_bwd(dy, sv["z"], sv["hl"], sv["lru"], dqkv, ddg, sv["wts"], l)
        gain = sv["gain"] if zero is None else sv["gain"] + zero
        zero = projections_done(l, *grad_w_in(sv["h"], dz, chip), g_w_out)
        if l == 0 and zero is not None:
            gain = gain + zero
        dx, dgn = in_bwd(dz, sv["w_in"], sv["x"], gain, dx)
        for k, g in zip(("gain", "a", "r", "n", "wa", "wx", "ws", "bs"), (dgn, ga, gr, gn, gwa, gwx, gws, gbs)):
            raw[k][l] = g
    st = {k: jnp.stack(v) for k, v in raw.items()}
    eye = jnp.eye(N_HEADS, dtype=F32)[None, :, None, :, None]
    diag = lambda g: (g.reshape(depth, N_HEADS, HEAD_DIM, N_HEADS, HEAD_DIM) * eye).sum(axis=3)
    grads = dict(
        norm_g=st["gain"][:, 0], conv_a_w=st["a"][:, :3], conv_r_w=st["r"][:, :4], conv_r_b=st["r"][:, 4],
        lru_ba=st["r"][:, 5], lru_bx=st["r"][:, 6], lru_lambda=st["r"][:, 7], gmlp_norm_g=st["n"][:, 0],
        lru_wa=diag(st["wa"]), lru_wx=diag(st["wx"]), gmlp_ws=st["ws"],
        gmlp_bs=jnp.swapaxes(st["bs"].reshape(depth, GMLP_CHUNK, N_HEADS, HEAD_DIM).sum(-1), 1, 2),
        final_g=dfg[0])
    return loss, dx, grads


MESH = pl.DeviceIdType.MESH
N_CHIPS = 4
N_DEV = 8
ANY = pl.BlockSpec(memory_space=pl.ANY)


def _place():
    x, y, c = lax.axis_index("x"), lax.axis_index("y"), lax.axis_index("c")
    chips = [(1 - x, y), (x, 1 - y), (1 - x, 1 - y)]
    return x, y, c, chips


def _remote(src, dst, ssem, rsem, to):
    return pltpu.make_async_remote_copy(src_ref=src, dst_ref=dst, send_sem=ssem, recv_sem=rsem,
                                        device_id=to, device_id_type=MESH)


HBM = pl.BlockSpec(memory_space=pltpu.HBM)
SEM = pl.BlockSpec(memory_space=pltpu.SEMAPHORE)
DATAFLOW = pltpu.SideEffectType.DATAFLOW_SIDE_EFFECTING
GATHER, SCATTER = "gather", "scatter"


def _chip_copies(mode, src_refs, land_refs, ssem, rsem):
    x, y, c, chips = _place()
    me = 2 * x + y
    n = len(src_refs)
    copies = []
    for k, (cx, cy) in enumerate(chips):
        for a in range(n):
            if mode == GATHER:
                src, dst = src_refs[a], land_refs[a].at[me]
            else:
                src, dst = src_refs[a].at[2 * cx + cy], land_refs[a].at[k]
            copies.append(_remote(src, dst, ssem.at[n * k + a], rsem.at[n * k + a], (cx, cy, c)))
    return copies


def exchange_start(mode, srcs, after, name):
    n = len(srcs)
    if mode == GATHER:
        lands = [lax.empty((N_CHIPS,) + s.shape, s.dtype) for s in srcs]
    else:
        lands = [lax.empty((N_CHIPS - 1,) + s.shape[1:], s.dtype) for s in srcs]
    extra = [] if after is None else [after]

    def body(*refs):
        src_refs, land_refs = refs[:n], refs[n:2 * n]
        ssem, rsem = refs[2 * n + len(extra)], refs[2 * n + len(extra) + 1]
        token = refs[-1]
        for cp in _chip_copies(mode, src_refs, land_refs, ssem, rsem):
            cp.start()
        token[...] = jnp.zeros_like(token)

    arrays = list(srcs) + lands
    return pl.pallas_call(
        body, name=name,
        out_shape=(pltpu.SemaphoreType.DMA((3 * n,)), pltpu.SemaphoreType.DMA((3 * n,)),
                   *[pltpu.HBM(a.shape, a.dtype) for a in arrays], jax.ShapeDtypeStruct((SUBLANES, LANES), F32)),
        in_specs=[HBM] * (2 * n) + [ANY] * len(extra),
        out_specs=(SEM, SEM, *[HBM] * (2 * n), pl.BlockSpec(memory_space=pltpu.VMEM)),
        input_output_aliases={i: 2 + i for i in range(2 * n)},
        compiler_params=pltpu.CompilerParams(has_side_effects=DATAFLOW),
    )(*[pltpu.with_memory_space_constraint(a, pltpu.HBM) for a in arrays], *extra)


def exchange_wait(mode, started, after, name):
    ssem, rsem, *thru, _ = started
    n = len(thru) // 2

    def body(*refs):
        src_refs, land_refs = refs[:n], refs[n:2 * n]
        ssem_ref, rsem_ref = refs[2 * n], refs[2 * n + 1]
        for cp in _chip_copies(mode, src_refs, land_refs, ssem_ref, rsem_ref):
            cp.wait_send()
            cp.wait_recv()

    outs = pl.pallas_call(
        body, name=name,
        out_shape=[pltpu.HBM(a.shape, a.dtype) for a in thru],
        in_specs=[HBM] * (2 * n) + [SEM, SEM, ANY],
        out_specs=[HBM] * (2 * n),
        input_output_aliases={i: i for i in range(2 * n)},
        compiler_params=pltpu.CompilerParams(has_side_effects=DATAFLOW),
    )(*thru, ssem, rsem, after)
    return outs[n:]


def sibling_exchange(p1, p2):
    def body(p1_ref, p2_ref, q1_ref, q2_ref, ssem, rsem):
        x, y, c, _ = _place()
        copies = [_remote(p_ref, q_ref, ssem.at[a], rsem.at[a], (x, y, 1 - c))
                  for a, (p_ref, q_ref) in enumerate(((p1_ref, q1_ref), (p2_ref, q2_ref)))]
        for cp in copies:
            cp.start()
        for cp in copies:
            cp.wait()

    return pl.pallas_call(
        body, name="sibling_exchange",
        in_specs=[ANY, ANY], out_specs=[ANY, ANY],
        out_shape=[jax.ShapeDtypeStruct(p.shape, p.dtype) for p in (p1, p2)],
        scratch_shapes=[pltpu.SemaphoreType.DMA((2,)), pltpu.SemaphoreType.DMA((2,))],
    )(p1, p2)


def all_reduce_small(v):
    r, n = v.shape
    piece = r // N_DEV

    def body(x_ref, out_ref, recv, ssem1, rsem1, ssem2, rsem2):
        x, y, c, _ = _place()
        me = 4 * x + 2 * y + c

        def peer(k):
            px = 1 - x if (k >> 2) & 1 else x
            py = 1 - y if (k >> 1) & 1 else y
            pc = 1 - c if k & 1 else c
            return (px, py, pc), 4 * px + 2 * py + pc

        def rows(ref, d):
            return ref.at[pl.ds(d * piece, piece), :]

        scatter = []
        for k in range(1, N_DEV):
            to, idx = peer(k)
            scatter.append(_remote(rows(x_ref, idx), recv.at[k], ssem1.at[k - 1], rsem1.at[k - 1], to))
            scatter[-1].start()
        acc = rows(x_ref, me)[...]
        for k in range(1, N_DEV):
            scatter[k - 1].wait_recv()
            acc = acc + recv[k]
        rows(out_ref, me)[...] = acc

        gather = []
        for k in range(1, N_DEV):
            to, _ = peer(k)
            gather.append(_remote(rows(out_ref, me), rows(out_ref, me), ssem2.at[k - 1], rsem2.at[k - 1], to))
            gather[-1].start()
        for k in range(1, N_DEV):
            to, idx = peer(k)
            _remote(rows(out_ref, idx), rows(out_ref, idx), ssem2.at[k - 1], rsem2.at[k - 1], to).wait_recv()
        for cp in scatter + gather:
            cp.wait_send()

    return pl.pallas_call(
        body, name="all_reduce_small",
        out_shape=jax.ShapeDtypeStruct((r, n), v.dtype),
        in_specs=[pl.BlockSpec(memory_space=pltpu.VMEM)],
        out_specs=pl.BlockSpec(memory_space=pltpu.VMEM),
        scratch_shapes=[pltpu.VMEM((N_DEV, piece, n), v.dtype)] + [pltpu.SemaphoreType.DMA((N_DEV - 1,))] * 4,
        compiler_params=pltpu.CompilerParams(vmem_limit_bytes=VMEM_LIMIT),
    )(v)


TILE_ROWS = 256


def _row_tile(r):
    return max(t for t in range(SUBLANES, TILE_ROWS + 1, SUBLANES) if r % t == 0)


def sum_partials(owns, parts):
    k, r, c = parts[0].shape
    depth = len(owns)
    tile = _row_tile(r)

    def body(buf_ref, o_ref, p_ref, out_ref):
        acc = o_ref[...]
        for i in range(k):
            acc = acc + p_ref[i].astype(F32)
        out_ref[0] = acc

    out = lax.empty((depth, r, c), F32)
    for l in range(depth):
        out = pl.pallas_call(
            functools.partial(body), name="sum_partials", grid=(r // tile,),
            in_specs=[ANY, pl.BlockSpec((tile, c), lambda i: (i, 0)), pl.BlockSpec((k, tile, c), lambda i: (0, i, 0))],
            out_specs=pl.BlockSpec((1, tile, c), lambda i, l=l: (l, i, 0)),
            out_shape=jax.ShapeDtypeStruct((depth, r, c), F32),
            input_output_aliases={0: 0},
            compiler_params=_params(("parallel",)),
        )(out, owns[l], parts[l])
    return out


def _adamw_update(w, g, m, v):
    m2 = ADAM_B1 * m + (1.0 - ADAM_B1) * g
    v2 = ADAM_B2 * v + (1.0 - ADAM_B2) * (g * g)
    m_hat = m2 / (1.0 - ADAM_B1 ** ADAM_STEP)
    v_hat = v2 / (1.0 - ADAM_B2 ** ADAM_STEP)
    return -ADAM_LR * (m_hat / (jnp.sqrt(v_hat) + ADAM_EPS) + ADAM_WD * w), m2, v2


def adamw_small(ws, gs, ms, vs):
    n = len(ws)

    def body(*refs):
        ins, outs = refs[:4 * n], refs[4 * n:]
        for i in range(n):
            d, m2, v2 = _adamw_update(ins[i][...], ins[n + i][...], ins[2 * n + i][...], ins[3 * n + i][...])
            outs[3 * i][...] = d
            outs[3 * i + 1][...] = m2
            outs[3 * i + 2][...] = v2

    outs = pl.pallas_call(
        body, name="adamw_small",
        out_shape=[jax.ShapeDtypeStruct(w.shape, F32) for w in ws for _ in range(3)],
    )(*ws, *gs, *ms, *vs)
    return [tuple(outs[3 * i:3 * i + 3]) for i in range(n)]


def adamw(w, ga, gb, m, v):
    n, r, c = w.shape
    tile = _row_tile(r)

    def body(w_ref, ga_ref, gb_ref, m_ref, v_ref, g_ref, d_ref, m2_ref, v2_ref):
        g = ga_ref[...] + gb_ref[...]
        g_ref[...] = g
        d_ref[...], m2_ref[...], v2_ref[...] = _adamw_update(w_ref[...], g, m_ref[...], v_ref[...])

    spec = pl.BlockSpec((1, tile, c), lambda j, i: (j, i, 0))
    return pl.pallas_call(
        body, name="adamw", grid=(n, r // tile),
        in_specs=[spec] * 5, out_specs=[spec] * 4,
        out_shape=[jax.ShapeDtypeStruct((n, r, c), F32)] * 4,
        compiler_params=_params(("parallel", "parallel")),
    )(w, ga, gb, m, v)


REPLICATED = ("norm_g", "conv_r_b", "lru_wa", "lru_ba", "lru_wx", "lru_bx", "lru_lambda", "gmlp_norm_g",
              "gmlp_ws", "gmlp_bs", "final_g")
CHIP_SHARDED_SMALL = ("conv_a_w", "conv_r_w")
PACK_LANES = LANES


def _pack(arrays):
    flat = jnp.concatenate([a.reshape(-1) for a in arrays])
    pad = (-flat.shape[0]) % (TILE_ROWS * PACK_LANES)
    return jnp.pad(flat, (0, pad)).reshape(-1, PACK_LANES)


def _unpack(packed, shapes):
    flat = packed.reshape(-1)
    out, off = [], 0
    for shp in shapes:
        n = math.prod(shp)
        out.append(flat[off:off + n].reshape(shp))
        off += n
    return out


def kernel(x, norm_g, w_in, conv_a_w, conv_r_w, conv_r_b, lru_wa, lru_ba, lru_wx, lru_bx, lru_lambda, gmlp_norm_g, gmlp_ws, gmlp_bs, w_out, final_g, loss_target, m_norm_g, m_w_in, m_conv_a_w, m_conv_r_w, m_conv_r_b, m_lru_wa, m_lru_ba, m_lru_wx, m_lru_bx, m_lru_lambda, m_gmlp_norm_g, m_gmlp_ws, m_gmlp_bs, m_w_out, m_final_g, v_norm_g, v_w_in, v_conv_a_w, v_conv_r_w, v_conv_r_b, v_lru_wa, v_lru_ba, v_lru_wx, v_lru_bx, v_lru_lambda, v_gmlp_norm_g, v_gmlp_ws, v_gmlp_bs, v_w_out, v_final_g):
    names = ("norm_g", "w_in", "conv_a_w", "conv_r_w", "conv_r_b", "lru_wa", "lru_ba", "lru_wx", "lru_bx",
             "lru_lambda", "gmlp_norm_g", "gmlp_ws", "gmlp_bs", "w_out", "final_g")
    w = dict(zip(names, (norm_g, w_in, conv_a_w, conv_r_w, conv_r_b, lru_wa, lru_ba, lru_wx, lru_bx, lru_lambda,
                         gmlp_norm_g, gmlp_ws, gmlp_bs, w_out, final_g)))
    m = dict(zip(names, (m_norm_g, m_w_in, m_conv_a_w, m_conv_r_w, m_conv_r_b, m_lru_wa, m_lru_ba, m_lru_wx, m_lru_bx,
                         m_lru_lambda, m_gmlp_norm_g, m_gmlp_ws, m_gmlp_bs, m_w_out, m_final_g)))
    v = dict(zip(names, (v_norm_g, v_w_in, v_conv_a_w, v_conv_r_w, v_conv_r_b, v_lru_wa, v_lru_ba, v_lru_wx, v_lru_bx,
                         v_lru_lambda, v_gmlp_norm_g, v_gmlp_ws, v_gmlp_bs, v_w_out, v_final_g)))
    depth = w_in.shape[0]
    out_rows = w_out.shape[1]
    conv_ch = conv_a_w.shape[2]
    chip = 2 * lax.axis_index("x") + lax.axis_index("y")

    taps = conv_a_w.shape[1] + conv_r_w.shape[1]
    w_in_t, m_w_in_t, v_w_in_t = (jnp.swapaxes(a, 1, 2) for a in (w_in, m_w_in, v_w_in))
    w_in_h, w_out_h = w_in_t.astype(BF16), w_out.astype(BF16)
    conv_own = jnp.concatenate([conv_a_w, conv_r_w], axis=1).reshape(depth * taps, conv_ch)
    gathers, token = [], None
    for l in range(depth):
        groups = [[w_in_h[l]], [w_out_h[l], conv_own]] if l == 0 else [[w_in_h[l], w_out_h[l]]]
        gathers.append([])
        for i, srcs in enumerate(groups):
            gathers[l].append(exchange_start(GATHER, srcs, token, f"gather_start_{l}_{i}"))
            token = gathers[l][-1][-1]
    p = dict(w)

    def with_own(land, own):
        return lax.dynamic_update_slice(land, own[None], (chip,) + (0,) * own.ndim)

    def layer_weights(l, x_l):
        lands = list(exchange_wait(GATHER, gathers[l][0], x_l, f"gather_wait_{l}_0"))
        w_in_l = with_own(lands[0], w_in_h[l]).reshape(D_IN, D_MODEL)
        gain = norm_g[l][None, :]
        if l == 0:
            gain = gain + token[0, 0]

        def rest(z_l):
            if l == 0:
                lands.extend(exchange_wait(GATHER, gathers[l][1], z_l, f"gather_wait_{l}_1"))
                conv = with_own(lands[2], conv_own).reshape(N_CHIPS, depth, taps, conv_ch)
                conv = conv.transpose(1, 2, 0, 3).reshape(depth, taps, GROUP_W)
                p["conv_a_w"] = conv[:, :conv_a_w.shape[1]]
                p["conv_r_w"] = conv[:, conv_a_w.shape[1]:]
                p["prepared"] = prepare_small_weights(p)
            return with_own(lands[1], w_out_h[l]).reshape(D_MIX, D_MODEL), p["prepared"]

        return gain, w_in_l, rest

    scatters, owns = [None] * depth, [None] * depth

    def projections_done(l, g_w_in_by_chip, g_w_in_own, g_w_out):
        go = g_w_out.reshape(N_CHIPS, out_rows, D_MODEL)
        owns[l] = (g_w_in_own, lax.dynamic_index_in_dim(go, chip, axis=0, keepdims=False))
        scatters[l] = exchange_start(SCATTER, [g_w_in_by_chip, go.astype(BF16)], None, f"scatter_start_{l}")
        return scatters[l][-1][0, 0]

    loss8, dx, grads = local_step(x[0], loss_target[0], final_g, depth, chip.reshape(1), layer_weights,
                                  projections_done)

    res = {}
    small = REPLICATED + CHIP_SHARDED_SMALL
    packed = _pack([grads[k] for k in small] + [loss8[0, :1]])
    total = all_reduce_small(packed)
    *sums, loss = _unpack(total, [grads[k].shape for k in small] + [()])
    gs = dict(zip(small, sums))
    for k in CHIP_SHARDED_SMALL:
        gs[k] = lax.dynamic_slice_in_dim(gs[k], chip * conv_ch, conv_ch, axis=2)
    as2d = lambda a: a[None] if a.ndim == 1 else a
    outs = adamw_small(*[[as2d(d[k]) for k in small] for d in (w, gs, m, v)])
    for k, (delta, m2, v2) in zip(small, outs):
        res[k] = [t.reshape(w[k].shape) for t in (gs[k], delta, m2, v2)]

    parts = [exchange_wait(SCATTER, scatters[l], total, f"scatter_wait_{l}") for l in range(depth)]
    p1 = sum_partials([owns[l][0] for l in range(depth)], [parts[l][0] for l in range(depth)])
    p2 = sum_partials([owns[l][1] for l in range(depth)], [parts[l][1] for l in range(depth)])
    q1, q2 = sibling_exchange(p1, p2)
    res["w_in"] = [jnp.swapaxes(t, 1, 2) for t in adamw(w_in_t, p1, q1, m_w_in_t, v_w_in_t)]
    res["w_out"] = adamw(w_out, p2, q2, m_w_out, v_w_out)

    return (loss, dx[None], *[res[k][0] for k in names], *[res[k][1] for k in names],
            *[res[k][2] for k in names], *[res[k][3] for k in names])
```

```python
import functools
import math

import jax
import jax.numpy as jnp
import numpy as np
from jax import lax
from jax.experimental import pallas as pl
from jax.experimental.pallas import tpu as pltpu

F32 = jnp.float32
BF16 = jnp.bfloat16

D_MODEL = 1024
GROUP_W = 256
N_HEADS = 4
HEAD_DIM = 64
N_CHUNKS = 13
D_IN = N_CHUNKS * GROUP_W
D_MIX = 4 * GROUP_W
NORM_EPS = 1e-6
RG_C = 8.0
GMLP_CHUNK = 128
ATTN_BLOCK = 128
PATTERN_DILS = (1, 4, 16)
ALIBI_SLOPES = tuple(2.0 ** (-8.0 * (h + 1) / N_HEADS) for h in range(N_HEADS))
ATTN_SCALE = 1.0 / math.sqrt(HEAD_DIM)
NEG_BIG = -1e30

ADAM_LR = 0.001
ADAM_B1 = 0.9
ADAM_B2 = 0.999
ADAM_EPS = 1e-08
ADAM_WD = 0.01
ADAM_STEP = 10

C_AX, C_AB, C_AC, C_AG, C_RX, C_RG, C_CU, C_CV, C_CG, C_DQ, C_DK, C_DV, C_DG = range(13)

SUBLANES = 8
LANES = 128
VMEM_LIMIT = 56 * 1024 * 1024
TILE_IN = 512
TILE_OUT = 1024
TILE_MIX = 512
TILE_DW = 1024
TILE_DW_OUT = 2048
ATTN_QB = 16
GELU_K0 = math.sqrt(2.0 / math.pi)
GELU_K1 = 0.044715


def _params(sem):
    return pltpu.CompilerParams(dimension_semantics=sem, vmem_limit_bytes=VMEM_LIMIT)


def _sigmoid(x):
    return 0.5 * jnp.tanh(0.5 * x) + 0.5


def _silu(x):
    return x * _sigmoid(x)


def _silu_and_grad(x):
    s = _sigmoid(x)
    return x * s, s * (1.0 + x * (1.0 - s))


def _gelu(x):
    return x * (0.5 + 0.5 * jnp.tanh(x * (GELU_K0 + (GELU_K0 * GELU_K1) * (x * x))))


def _gelu_and_grad(x):
    x2 = x * x
    t = jnp.tanh(x * (GELU_K0 + (GELU_K0 * GELU_K1) * x2))
    half = 0.5 + 0.5 * t
    return x * half, half + x * (0.5 - 0.5 * t * t) * (GELU_K0 + (3.0 * GELU_K0 * GELU_K1) * x2)


def _neg_expm1_2x(x):
    t = jnp.tanh(x)
    return -2.0 * t / (1.0 - t)


def _shift_down(v, halo, k):
    r = pltpu.roll(v, k, 0)
    rh = pltpu.roll(halo, k, 0)
    row = lax.broadcasted_iota(jnp.int32, halo.shape, 0)
    top = jnp.where(row < k, rh, r[:SUBLANES])
    return jnp.concatenate([top, r[SUBLANES:]], axis=0)


def _shift_up(v, halo, k):
    t = v.shape[0]
    r = pltpu.roll(v, t - k, 0)
    rh = pltpu.roll(halo, SUBLANES - k, 0)
    row = lax.broadcasted_iota(jnp.int32, halo.shape, 0)
    bot = jnp.where(row >= SUBLANES - k, rh, r[t - SUBLANES:])
    return jnp.concatenate([r[:t - SUBLANES], bot], axis=0)


def _scan_causal(a, b, h_in):
    t = a.shape[0]
    row8 = lax.broadcasted_iota(jnp.int32, a.shape, 0) % SUBLANES
    d = 1
    while d < SUBLANES:
        m = row8 >= d
        a_s = jnp.where(m, pltpu.roll(a, d, 0), 1.0)
        b_s = jnp.where(m, pltpu.roll(b, d, 0), 0.0)
        b = a * b_s + b
        a = a * a_s
        d *= 2
    out, carry = [], h_in
    for g in range(t // SUBLANES):
        sl = slice(g * SUBLANES, (g + 1) * SUBLANES)
        hg = b[sl] + a[sl] * carry
        out.append(hg)
        carry = hg[SUBLANES - 1:SUBLANES]
    return jnp.concatenate(out, axis=0)


def _scan_anticausal(a, b, g_in):
    t = a.shape[0]
    row8 = lax.broadcasted_iota(jnp.int32, a.shape, 0) % SUBLANES
    d = 1
    while d < SUBLANES:
        m = row8 < SUBLANES - d
        a_s = jnp.where(m, pltpu.roll(a, t - d, 0), 1.0)
        b_s = jnp.where(m, pltpu.roll(b, t - d, 0), 0.0)
        b = a * b_s + b
        a = a * a_s
        d *= 2
    out, carry = [], g_in
    for g in reversed(range(t // SUBLANES)):
        sl = slice(g * SUBLANES, (g + 1) * SUBLANES)
        gg = b[sl] + a[sl] * carry
        out.append(gg)
        carry = gg[0:1]
    return jnp.concatenate(out[::-1], axis=0)


def _head_of_lane(shape):
    return lax.broadcasted_iota(jnp.int32, shape, len(shape) - 1) // HEAD_DIM


def _per_head_lanes(cols):
    t = cols[0].shape[0]
    lane = lax.broadcasted_iota(jnp.int32, (t, LANES), 1)
    out = jnp.zeros((t, LANES), F32)
    for h, col in enumerate(cols):
        out = jnp.where(lane == h, col, out)
    return out


def _put_row(acc_shape, k, row_vec):
    row = lax.broadcasted_iota(jnp.int32, acc_shape, 0)
    return jnp.where(row == k, jnp.broadcast_to(row_vec, acc_shape), 0.0)


def _dot(a, b):
    return jnp.dot(a, b, preferred_element_type=F32)


def _dot_nt(a, b):
    return lax.dot_general(a, b, (((1,), (1,)), ((), ())), preferred_element_type=F32)


def _dot_tn(a, b):
    return lax.dot_general(a, b, (((0,), (0,)), ((), ())), preferred_element_type=F32)


def _deinterleave_store(val, stage, outs):
    t, c = val.shape
    for hh in range(c // LANES):
        stage[hh][...] = val[:, hh * LANES:(hh + 1) * LANES].astype(F32)
    for dil, ref in outs:
        for r in range(dil):
            for hh in range(c // LANES):
                ref[r, :, hh * LANES:(hh + 1) * LANES] = stage[hh][pl.ds(r, t // dil, stride=dil), :].astype(ref.dtype)


def _interleave_load(ref, dil, stage):
    _, n, c = ref.shape
    for r in range(dil):
        for hh in range(c // LANES):
            stage[hh][pl.ds(r, n, stride=dil), :] = ref[r, :, hh * LANES:(hh + 1) * LANES].astype(F32)
    return jnp.concatenate([stage[hh][...] for hh in range(c // LANES)], axis=1)


def _stage_scratch(tile, cols, copies):
    return [pltpu.VMEM((tile, LANES), F32)] * (copies * (cols // LANES))


def _by_residue(s, dil, cols, dtype):
    return jax.ShapeDtypeStruct((dil, s // dil, cols), dtype)


def _residue_block(dil, tile, cols):
    return pl.BlockSpec((dil, tile // dil, cols), lambda i: (0, i, 0))


def in_fwd(x, g, w):
    s = x.shape[0]
    qkv_w = 3 * GROUP_W

    def body(x_ref, g_ref, w_ref, z_ref, h_ref, qkv1_ref, qkv4_ref, qkv16_ref, *stage):
        xv = x_ref[...]
        rs = lax.rsqrt(jnp.mean(xv * xv, axis=-1, keepdims=True) + NORM_EPS)
        h = (xv * rs * g_ref[...]).astype(BF16)
        h_ref[...] = h
        z = _dot_nt(h, w_ref[...])
        z_ref[...] = z
        col = lax.broadcasted_iota(jnp.int32, (1, qkv_w), 1)
        qkv = z[:, C_DQ * GROUP_W:(C_DV + 1) * GROUP_W] * jnp.where(col < GROUP_W, ATTN_SCALE, 1.0)
        qkv1_ref[...] = qkv.astype(BF16)
        _deinterleave_store(qkv, stage, ((PATTERN_DILS[1], qkv4_ref), (PATTERN_DILS[2], qkv16_ref)))

    return pl.pallas_call(
        body, name="in_fwd", grid=(s // TILE_IN,),
        in_specs=[pl.BlockSpec((TILE_IN, D_MODEL), lambda i: (i, 0)),
                  pl.BlockSpec((1, D_MODEL), lambda i: (0, 0)),
                  pl.BlockSpec((D_IN, D_MODEL), lambda i: (0, 0))],
        out_specs=[pl.BlockSpec((TILE_IN, D_IN), lambda i: (i, 0)),
                   pl.BlockSpec((TILE_IN, D_MODEL), lambda i: (i, 0)),
                   pl.BlockSpec((TILE_IN, qkv_w), lambda i: (i, 0)),
                   _residue_block(PATTERN_DILS[1], TILE_IN, qkv_w),
                   _residue_block(PATTERN_DILS[2], TILE_IN, qkv_w)],
        out_shape=[jax.ShapeDtypeStruct((s, D_IN), F32), jax.ShapeDtypeStruct((s, D_MODEL), BF16),
                   jax.ShapeDtypeStruct((s, qkv_w), BF16),
                   _by_residue(s, PATTERN_DILS[1], qkv_w, BF16), _by_residue(s, PATTERN_DILS[2], qkv_w, BF16)],
        scratch_shapes=_stage_scratch(TILE_IN, qkv_w, 1),
        compiler_params=_params(("parallel",)),
    )(x, g, w)


def out_fwd(y, w, x):
    s = x.shape[0]

    def body(y_ref, w_ref, x_ref, o_ref):
        o_ref[...] = x_ref[...] + _dot(y_ref[...], w_ref[...])

    return pl.pallas_call(
        body, name="out_fwd", grid=(s // TILE_OUT,),
        in_specs=[pl.BlockSpec((TILE_OUT, D_MIX), lambda i: (i, 0)),
                  pl.BlockSpec((D_MIX, D_MODEL), lambda i: (0, 0)),
                  pl.BlockSpec((TILE_OUT, D_MODEL), lambda i: (i, 0))],
        out_specs=pl.BlockSpec((TILE_OUT, D_MODEL), lambda i: (i, 0)),
        out_shape=jax.ShapeDtypeStruct((s, D_MODEL), F32),
        compiler_params=_params(("parallel",)),
    )(y, w, x)


def out_bwd(dx, w, z, o):
    s = dx.shape[0]
    abc = 3 * GROUP_W

    def body(dx_ref, w_ref, dg_ref, o_ref, dy_ref, ddg_ref, do1_ref, do4_ref, do16_ref, dl1_ref, dl4_ref, dl16_ref,
             *stage):
        stage_a, stage_b = stage[:2], stage[2:]
        dy = _dot_nt(dx_ref[...].astype(BF16), w_ref[...])
        dy_ref[...] = dy[:, :abc]
        dyd = dy[:, abc:]
        head = _head_of_lane((TILE_OUT, GROUP_W))
        dg = dg_ref[...]
        o = o_ref[...]
        sg, dsg = _silu_and_grad(dg)
        do = dyd * sg
        ddg_ref[...] = dyd * o * dsg
        prod = do * o
        dl = _per_head_lanes([jnp.sum(jnp.where(head == h, prod, 0.0), axis=-1, keepdims=True)
                              for h in range(N_HEADS)])
        do1_ref[...] = do.astype(BF16)
        dl1_ref[...] = dl
        _deinterleave_store(do, stage_a, ((PATTERN_DILS[1], do4_ref), (PATTERN_DILS[2], do16_ref)))
        _deinterleave_store(dl, stage_b, ((PATTERN_DILS[1], dl4_ref), (PATTERN_DILS[2], dl16_ref)))

    row = pl.BlockSpec((TILE_OUT, GROUP_W), lambda i: (i, 0))
    r4 = _residue_block(PATTERN_DILS[1], TILE_OUT, GROUP_W)
    r16 = _residue_block(PATTERN_DILS[2], TILE_OUT, GROUP_W)
    crow = pl.BlockSpec((TILE_OUT, LANES), lambda i: (i, 0))
    c4 = _residue_block(PATTERN_DILS[1], TILE_OUT, LANES)
    c16 = _residue_block(PATTERN_DILS[2], TILE_OUT, LANES)
    return pl.pallas_call(
        body, name="out_bwd", grid=(s // TILE_OUT,),
        in_specs=[pl.BlockSpec((TILE_OUT, D_MODEL), lambda i: (i, 0)),
                  pl.BlockSpec((D_MIX, D_MODEL), lambda i: (0, 0)),
                  pl.BlockSpec((TILE_OUT, GROUP_W), lambda i: (i, C_DG)), row],
        out_specs=[pl.BlockSpec((TILE_OUT, abc), lambda i: (i, 0)), row, row, r4, r16, crow, c4, c16],
        out_shape=[jax.ShapeDtypeStruct((s, abc), F32), jax.ShapeDtypeStruct((s, GROUP_W), F32),
                   jax.ShapeDtypeStruct((s, GROUP_W), BF16),
                   _by_residue(s, PATTERN_DILS[1], GROUP_W, BF16), _by_residue(s, PATTERN_DILS[2], GROUP_W, BF16),
                   jax.ShapeDtypeStruct((s, LANES), F32),
                   _by_residue(s, PATTERN_DILS[1], LANES, F32), _by_residue(s, PATTERN_DILS[2], LANES, F32)],
        scratch_shapes=_stage_scratch(TILE_OUT, GROUP_W, 1) + _stage_scratch(TILE_OUT, LANES, 1),
        compiler_params=_params(("parallel",)),
    )(dx, w, z, o)


def in_bwd(dz, w, x, g, dx_next):
    s = x.shape[0]

    def body(dz_ref, w_ref, x_ref, g_ref, dxn_ref, dx_ref, dg_ref):
        @pl.when(pl.program_id(0) == 0)
        def _():
            dg_ref[...] = jnp.zeros_like(dg_ref)

        dh = _dot(dz_ref[...], w_ref[...])
        xv = x_ref[...]
        rs = lax.rsqrt(jnp.mean(xv * xv, axis=-1, keepdims=True) + NORM_EPS)
        xh = xv * rs
        dg_ref[...] += _put_row(dg_ref.shape, 0, jnp.sum(dh * xh, axis=0, keepdims=True))
        dn = dh * g_ref[...]
        dx_ref[...] = dxn_ref[...] + rs * (dn - xh * jnp.mean(dn * xh, axis=-1, keepdims=True))

    return pl.pallas_call(
        body, name="in_bwd", grid=(s // TILE_IN,),
        in_specs=[pl.BlockSpec((TILE_IN, D_IN), lambda i: (i, 0)),
                  pl.BlockSpec((D_IN, D_MODEL), lambda i: (0, 0)),
                  pl.BlockSpec((TILE_IN, D_MODEL), lambda i: (i, 0)),
                  pl.BlockSpec((1, D_MODEL), lambda i: (0, 0)),
                  pl.BlockSpec((TILE_IN, D_MODEL), lambda i: (i, 0))],
        out_specs=[pl.BlockSpec((TILE_IN, D_MODEL), lambda i: (i, 0)),
                   pl.BlockSpec((SUBLANES, D_MODEL), lambda i: (0, 0))],
        out_shape=[jax.ShapeDtypeStruct((s, D_MODEL), F32), jax.ShapeDtypeStruct((SUBLANES, D_MODEL), F32)],
        compiler_params=_params(("arbitrary",)),
    )(dz, w, x, g, dx_next)


def grad_w_out(y, dx):
    s = y.shape[0]

    def body(y_ref, dx_ref, o_ref):
        @pl.when(pl.program_id(0) == 0)
        def _():
            o_ref[...] = jnp.zeros_like(o_ref)

        o_ref[...] += _dot_tn(y_ref[...], dx_ref[...].astype(BF16))

    return pl.pallas_call(
        body, name="grad_w_out", grid=(s // TILE_DW_OUT,),
        in_specs=[pl.BlockSpec((TILE_DW_OUT, D_MIX), lambda k: (k, 0)),
                  pl.BlockSpec((TILE_DW_OUT, D_MODEL), lambda k: (k, 0))],
        out_specs=pl.BlockSpec((D_MIX, D_MODEL), lambda k: (0, 0)),
        out_shape=jax.ShapeDtypeStruct((D_MIX, D_MODEL), F32),
        compiler_params=_params(("arbitrary",)),
    )(y, dx)


def grad_w_in(h, dz, chip):
    s = h.shape[0]
    rows = D_IN // N_CHIPS

    def body(chip_ref, h_ref, dz_ref, staged_ref, own_ref, acc):
        k = pl.program_id(0)

        @pl.when(k == 0)
        def _():
            acc[...] = jnp.zeros_like(acc)

        acc[...] += _dot_tn(dz_ref[...], h_ref[...])

        @pl.when(k == s // TILE_DW - 1)
        def _():
            for j in range(N_CHIPS):
                part = acc[j * rows:(j + 1) * rows, :]
                staged_ref[j] = part.astype(BF16)

                @pl.when(chip_ref[0] == j)
                def _():
                    own_ref[...] = part

    return pl.pallas_call(
        body, name="grad_w_in",
        grid_spec=pltpu.PrefetchScalarGridSpec(
            num_scalar_prefetch=1, grid=(s // TILE_DW,),
            in_specs=[pl.BlockSpec((TILE_DW, D_MODEL), lambda k, c: (k, 0)),
                      pl.BlockSpec((TILE_DW, D_IN), lambda k, c: (k, 0))],
            out_specs=[pl.BlockSpec((N_CHIPS, rows, D_MODEL), lambda k, c: (0, 0, 0)),
                       pl.BlockSpec((rows, D_MODEL), lambda k, c: (0, 0))],
            scratch_shapes=[pltpu.VMEM((D_IN, D_MODEL), F32)]),
        out_shape=[jax.ShapeDtypeStruct((N_CHIPS, rows, D_MODEL), BF16), jax.ShapeDtypeStruct((rows, D_MODEL), F32)],
        compiler_params=_params(("arbitrary",)),
    )(chip, h, dz)


def out_fwd_loss(y, w, x, g, tgt):
    s = x.shape[0]

    def body(y_ref, w_ref, x_ref, g_ref, t_ref, l_ref, dx_ref, dg_ref):
        @pl.when(pl.program_id(0) == 0)
        def _():
            l_ref[...] = jnp.zeros_like(l_ref)
            dg_ref[...] = jnp.zeros_like(dg_ref)

        xv = x_ref[...] + _dot(y_ref[...], w_ref[...])
        gv = g_ref[...]
        rs = lax.rsqrt(jnp.mean(xv * xv, axis=-1, keepdims=True) + NORM_EPS)
        xh = xv * rs
        e = xh * gv - t_ref[...]
        part = 0.5 * jnp.sum(jnp.mean(e * e, axis=-1, keepdims=True), axis=0, keepdims=True)
        l_ref[...] += jnp.broadcast_to(part, l_ref.shape)
        dy = e * (1.0 / D_MODEL)
        dg_ref[...] += _put_row(dg_ref.shape, 0, jnp.sum(dy * xh, axis=0, keepdims=True))
        dn = dy * gv
        dx_ref[...] = rs * (dn - xh * jnp.mean(dn * xh, axis=-1, keepdims=True))

    return pl.pallas_call(
        body, name="out_fwd_loss", grid=(s // TILE_OUT,),
        in_specs=[pl.BlockSpec((TILE_OUT, D_MIX), lambda i: (i, 0)),
                  pl.BlockSpec((D_MIX, D_MODEL), lambda i: (0, 0)),
                  pl.BlockSpec((TILE_OUT, D_MODEL), lambda i: (i, 0)),
                  pl.BlockSpec((1, D_MODEL), lambda i: (0, 0)),
                  pl.BlockSpec((TILE_OUT, D_MODEL), lambda i: (i, 0))],
        out_specs=[pl.BlockSpec((SUBLANES, LANES), lambda i: (0, 0)),
                   pl.BlockSpec((TILE_OUT, D_MODEL), lambda i: (i, 0)),
                   pl.BlockSpec((SUBLANES, D_MODEL), lambda i: (0, 0))],
        out_shape=[jax.ShapeDtypeStruct((SUBLANES, LANES), F32), jax.ShapeDtypeStruct((s, D_MODEL), F32),
                   jax.ShapeDtypeStruct((SUBLANES, D_MODEL), F32)],
        compiler_params=_params(("arbitrary",)),
    )(y, w, x, g, tgt)


def _attn_bias(dil):
    qi = np.arange(ATTN_BLOCK)[:, None]
    ki = np.arange(2 * ATTN_BLOCK)[None, :]
    delta = qi + ATTN_BLOCK - ki
    band = (delta >= 0) & (delta <= ATTN_BLOCK)
    out = np.empty((2, N_HEADS, ATTN_BLOCK, 2 * ATTN_BLOCK), np.float32)
    for f in range(2):
        ok = band & ((ki >= ATTN_BLOCK) | (f == 0))
        for h in range(N_HEADS):
            out[f, h] = np.where(ok, -ALIBI_SLOPES[h] * dil * delta, NEG_BIG)
    return jnp.asarray(out.reshape(2, N_HEADS * ATTN_BLOCK, 2 * ATTN_BLOCK))


def _stack_heads(a, head):
    return jnp.concatenate([jnp.where(head == h, a, jnp.zeros_like(a)) for h in range(N_HEADS)], axis=0)


def _unstack_heads(a, head):
    out = a[:ATTN_BLOCK]
    for h in range(1, N_HEADS):
        out = jnp.where(head == h, a[h * ATTN_BLOCK:(h + 1) * ATTN_BLOCK], out)
    return out


def _head_column(a):
    return jnp.concatenate([a[:, h:h + 1] for h in range(N_HEADS)], axis=0)


def _attn_specs(n_blocks):
    rows = ATTN_QB * ATTN_BLOCK
    cur = lambda c, w=GROUP_W: pl.BlockSpec((rows, w), lambda n, c=c: (n, c))
    prev = lambda c: pl.BlockSpec((ATTN_BLOCK, GROUP_W), lambda n, c=c: (jnp.maximum(n * ATTN_QB - 1, 0), c))
    nxt = lambda c, w=GROUP_W: pl.BlockSpec((ATTN_BLOCK, w),
                                            lambda n, c=c: (jnp.minimum(n * ATTN_QB + ATTN_QB, n_blocks - 1), c))
    return cur, prev, nxt


def _keys(kp_ref, k_ref, j):
    prev = kp_ref[...] if j == 0 else k_ref[(j - 1) * ATTN_BLOCK:j * ATTN_BLOCK, :]
    return jnp.concatenate([prev, k_ref[j * ATTN_BLOCK:(j + 1) * ATTN_BLOCK, :]], axis=0)


def attn_fwd(qkv, dil):
    s = qkv.shape[0]
    n_blocks = s // ATTN_BLOCK
    bps = n_blocks // dil
    rows = ATTN_QB * ATTN_BLOCK

    def body(q_ref, kp_ref, k_ref, vp_ref, v_ref, bias_ref, o_ref, lse_ref):
        n = pl.program_id(0)
        head = _head_of_lane((ATTN_BLOCK, GROUP_W))
        for j in range(ATTN_QB):
            sl = slice(j * ATTN_BLOCK, (j + 1) * ATTN_BLOCK)
            first = (((n * ATTN_QB + j) % bps) == 0).astype(jnp.int32)
            qs = _stack_heads(q_ref[sl, :], head)
            sc = _dot_nt(qs, _keys(kp_ref, k_ref, j)) + bias_ref[first]
            m = jnp.max(sc, axis=-1, keepdims=True)
            pr = jnp.exp(sc - m)
            l = jnp.sum(pr, axis=-1, keepdims=True)
            oh = _dot(pr.astype(BF16), _keys(vp_ref, v_ref, j)) / l
            o_ref[sl, :] = _unstack_heads(oh, head).astype(BF16)
            ml = m + jnp.log(l)
            lse_ref[sl, :] = _per_head_lanes([ml[h * ATTN_BLOCK:(h + 1) * ATTN_BLOCK] for h in range(N_HEADS)])

    cur, prev, _ = _attn_specs(n_blocks)
    bias = _attn_bias(dil)
    return pl.pallas_call(
        body, name=f"attn_fwd_d{dil}", grid=(n_blocks // ATTN_QB,),
        in_specs=[cur(0), prev(1), cur(1), prev(2), cur(2), pl.BlockSpec(bias.shape, lambda n: (0, 0, 0))],
        out_specs=[cur(0), cur(0, LANES)],
        out_shape=[jax.ShapeDtypeStruct((s, GROUP_W), BF16), jax.ShapeDtypeStruct((s, LANES), F32)],
        compiler_params=_params(("parallel",)),
    )(qkv, qkv, qkv, qkv, qkv, bias)


def attn_bwd(qkv, do, lse, dlt, dil):
    s = qkv.shape[0]
    n_blocks = s // ATTN_BLOCK
    bps = n_blocks // dil
    rows = ATTN_QB * ATTN_BLOCK

    def body(q_ref, qn_ref, kp_ref, k_ref, vp_ref, v_ref, do_ref, don_ref, lse_ref, lsen_ref, dl_ref, dln_ref,
             bias_ref, out_ref, dk_acc, dv_acc):
        n = pl.program_id(0)
        head = _head_of_lane((ATTN_BLOCK, GROUP_W))
        dk_acc[...] = jnp.zeros_like(dk_acc)
        dv_acc[...] = jnp.zeros_like(dv_acc)

        def pair(qj, doj, lsej, dlj, kk, vv, bias, keep):
            qs = _stack_heads(qj, head)
            dos = _stack_heads(doj, head)
            sc = _dot_nt(qs, kk) + bias
            if keep is None:
                pr = jnp.exp(sc - _head_column(lsej))
            else:
                pr = jnp.exp(jnp.minimum(sc - _head_column(lsej), 0.0)) * keep
            dp = _dot_nt(dos, vv)
            ds = (pr * (dp - _head_column(dlj))).astype(BF16)
            return ds, _dot_tn(ds, qs), _dot_tn(pr.astype(BF16), dos)

        for j in range(ATTN_QB):
            sl = slice(j * ATTN_BLOCK, (j + 1) * ATTN_BLOCK)
            first = (((n * ATTN_QB + j) % bps) == 0).astype(jnp.int32)
            kk = _keys(kp_ref, k_ref, j)
            ds, dks, dvs = pair(q_ref[sl, :], do_ref[sl, :], lse_ref[sl, :], dl_ref[sl, :],
                                kk, _keys(vp_ref, v_ref, j), bias_ref[first], None)
            out_ref[sl, 0:GROUP_W] = _unstack_heads(_dot(ds, kk), head) * ATTN_SCALE
            acc = slice(j * ATTN_BLOCK, (j + 2) * ATTN_BLOCK)
            dk_acc[acc, :] += dks
            dv_acc[acc, :] += dvs

        nxt = n * ATTN_QB + ATTN_QB
        valid = ((nxt < n_blocks) & ((nxt % bps) != 0)).astype(F32)
        last = slice((ATTN_QB - 1) * ATTN_BLOCK, ATTN_QB * ATTN_BLOCK)
        _, dks, dvs = pair(qn_ref[...], don_ref[...], lsen_ref[...], dln_ref[...], k_ref[last, :], v_ref[last, :],
                           bias_ref[0][:, :ATTN_BLOCK], valid)
        acc = slice(ATTN_QB * ATTN_BLOCK, (ATTN_QB + 1) * ATTN_BLOCK)
        dk_acc[acc, :] += dks
        dv_acc[acc, :] += dvs
        out_ref[:, GROUP_W:2 * GROUP_W] = dk_acc[ATTN_BLOCK:, :]
        out_ref[:, 2 * GROUP_W:3 * GROUP_W] = dv_acc[ATTN_BLOCK:, :]

    cur, prev, nxt = _attn_specs(n_blocks)
    bias = _attn_bias(dil)
    return pl.pallas_call(
        body, name=f"attn_bwd_d{dil}", grid=(n_blocks // ATTN_QB,),
        in_specs=[cur(0), nxt(0), prev(1), cur(1), prev(2), cur(2), cur(0), nxt(0),
                  cur(0, LANES), nxt(0, LANES), cur(0, LANES), nxt(0, LANES),
                  pl.BlockSpec(bias.shape, lambda n: (0, 0, 0))],
        out_specs=pl.BlockSpec((rows, 3 * GROUP_W), lambda n: (n, 0)),
        out_shape=jax.ShapeDtypeStruct((s, 3 * GROUP_W), F32),
        scratch_shapes=[pltpu.VMEM(((ATTN_QB + 1) * ATTN_BLOCK, GROUP_W), F32),
                        pltpu.VMEM(((ATTN_QB + 1) * ATTN_BLOCK, GROUP_W), F32)],
        compiler_params=_params(("parallel",)),
    )(qkv, qkv, qkv, qkv, qkv, qkv, do, do, lse, lse, dlt, dlt, bias)


def _zcol(c):
    return pl.BlockSpec((TILE_MIX, GROUP_W), lambda i, c=c: (i, c))


def _zhalo(c):
    per = TILE_MIX // SUBLANES
    return pl.BlockSpec((SUBLANES, GROUP_W), lambda i, c=c: (jnp.maximum(i * per - 1, 0), c))


def _full(shape):
    return pl.BlockSpec(shape, lambda i: tuple(0 for _ in shape))


def _of_layer(a, l):
    rest = a.shape[1:]
    return pl.BlockSpec((None,) + rest, lambda i: (l,) + tuple(0 for _ in rest))


def _softplus_neg(lam):
    nl = -lam
    return jnp.maximum(nl, 0.0) + jnp.log1p(jnp.exp(-jnp.abs(nl)))


def _lru_gates(xb, wa_ref, wx_ref, ba, bx, lam):
    xbb = xb.astype(BF16)
    r = _sigmoid(_dot(xbb, wa_ref[...]) + ba)
    ig = _sigmoid(_dot(xbb, wx_ref[...]) + bx)
    log_a = (-RG_C * r) * _softplus_neg(lam)
    a = jnp.exp(log_a)
    mult = jnp.sqrt(_neg_expm1_2x(log_a))
    return r, ig, a, mult


LRU_SAVED = 5


def _gmlp_spatial(ws_ref, vvb, head):
    outs = []
    for j in range(vvb.shape[0] // GMLP_CHUNK):
        blk = vvb[j * GMLP_CHUNK:(j + 1) * GMLP_CHUNK, :]
        acc = jnp.zeros((GMLP_CHUNK, GROUP_W), F32)
        for h in range(N_HEADS):
            acc = jnp.where(head[:GMLP_CHUNK] == h, _dot(ws_ref[h], blk), acc)
        outs.append(acc)
    return jnp.concatenate(outs, axis=0)


def mix_fwd(z, attn, wts, l):
    s = z.shape[0]
    d4, d16 = PATTERN_DILS[1], PATTERN_DILS[2]

    def body(ax_ref, ab_ref, ac_ref, ag_ref, rx_ref, rg_ref, cu_ref, cv_ref, cg_ref, dg_ref,
             axh_ref, ach_ref, rxh_ref, o1_ref, l1_ref, o4_ref, l4_ref, o16_ref, l16_ref,
             caw_ref, crw_ref, crb_ref, wa_ref, wx_ref, ba_ref, bx_ref, lam_ref, gng_ref, ws_ref, bs_ref,
             y_ref, hl_ref, o_ref, lse_ref, lse4_ref, lse16_ref, lru_ref, carry, *stage):
        st_a, st_b, st_c, st_d, st_e = (stage[2 * k:2 * k + 2] for k in range(5))
        i = pl.program_id(0)

        @pl.when(i == 0)
        def _():
            carry[...] = jnp.zeros_like(carry)

        nz = (i > 0).astype(F32)
        head = _head_of_lane((TILE_MIX, GROUP_W))

        pa = ac_ref[...] * ax_ref[...]
        pah = ach_ref[...] * axh_ref[...] * nz
        cv = caw_ref[2:3, :] * pa + caw_ref[1:2, :] * _shift_down(pa, pah, 1) + caw_ref[0:1, :] * _shift_down(pa, pah, 2)
        y_ref[:, 0:GROUP_W] = (ab_ref[...] * cv * _silu(ag_ref[...])).astype(BF16)

        rx = rx_ref[...]
        rxh = rxh_ref[...] * nz
        xb = (crw_ref[3:4, :] * rx + crw_ref[2:3, :] * _shift_down(rx, rxh, 1) + crw_ref[1:2, :] * _shift_down(rx, rxh, 2)
              + crw_ref[0:1, :] * _shift_down(rx, rxh, 3) + crb_ref[...])
        r, ig, a, mult = _lru_gates(xb, wa_ref, wx_ref, ba_ref[...], bx_ref[...], lam_ref[...])
        for k, val in enumerate((xb, r, ig, a, mult)):
            lru_ref[:, k * GROUP_W:(k + 1) * GROUP_W] = val
        hl = _scan_causal(a, mult * (ig * xb), carry[SUBLANES - 1:SUBLANES, :])
        hl_ref[...] = hl
        carry[...] = hl[TILE_MIX - SUBLANES:, :]
        y_ref[:, GROUP_W:2 * GROUP_W] = (hl * _silu(rg_ref[...])).astype(BF16)

        u = _gelu(cu_ref[...])
        gv = _gelu(cv_ref[...])
        rs = lax.rsqrt(jnp.mean(gv * gv, axis=-1, keepdims=True) + NORM_EPS)
        vvb = (gv * rs * gng_ref[...]).astype(BF16)
        sp = _gmlp_spatial(ws_ref, vvb, head) + jnp.concatenate([bs_ref[...]] * (TILE_MIX // GMLP_CHUNK), axis=0)
        y_ref[:, 2 * GROUP_W:3 * GROUP_W] = (u * sp * _silu(cg_ref[...])).astype(BF16)

        ops = (o1_ref[...].astype(F32), _interleave_load(o4_ref, d4, st_a), _interleave_load(o16_ref, d16, st_b))
        lps = (l1_ref[...], _interleave_load(l4_ref, d4, st_c), _interleave_load(l16_ref, d16, st_d))
        m = jnp.maximum(jnp.maximum(lps[0], lps[1]), lps[2])
        ws = [jnp.exp(lp - m) for lp in lps]
        zsum = ws[0] + ws[1] + ws[2]
        lse = m + jnp.log(zsum)
        o = jnp.zeros((TILE_MIX, GROUP_W), F32)
        for op, w in zip(ops, ws):
            wn = w / zsum
            wide = jnp.zeros((TILE_MIX, GROUP_W), F32)
            for h in range(N_HEADS):
                wide = jnp.where(head == h, wn[:, h:h + 1], wide)
            o = o + wide * op
        o_ref[...] = o
        lse_ref[...] = lse
        _deinterleave_store(lse, st_e, ((d4, lse4_ref), (d16, lse16_ref)))
        y_ref[:, 3 * GROUP_W:4 * GROUP_W] = (o * _silu(dg_ref[...])).astype(BF16)

    row = pl.BlockSpec((TILE_MIX, GROUP_W), lambda i: (i, 0))
    r4 = _residue_block(d4, TILE_MIX, GROUP_W)
    r16 = _residue_block(d16, TILE_MIX, GROUP_W)
    crow = pl.BlockSpec((TILE_MIX, LANES), lambda i: (i, 0))
    c4 = _residue_block(d4, TILE_MIX, LANES)
    c16 = _residue_block(d16, TILE_MIX, LANES)
    names = ("caw", "crw", "crb", "wa", "wx", "ba", "bx", "lam", "gng", "ws", "bs")
    in_specs = ([_zcol(c) for c in (C_AX, C_AB, C_AC, C_AG, C_RX, C_RG, C_CU, C_CV, C_CG, C_DG)]
                + [_zhalo(C_AX), _zhalo(C_AC), _zhalo(C_RX), row, crow, r4, c4, r16, c16]
                + [_of_layer(wts[k], l) for k in names])
    return pl.pallas_call(
        body, name="mix_fwd", grid=(s // TILE_MIX,),
        in_specs=in_specs,
        out_specs=[pl.BlockSpec((TILE_MIX, D_MIX), lambda i: (i, 0)), row, row, crow, c4, c16,
                   pl.BlockSpec((TILE_MIX, LRU_SAVED * GROUP_W), lambda i: (i, 0))],
        out_shape=([jax.ShapeDtypeStruct((s, D_MIX), BF16)] + [jax.ShapeDtypeStruct((s, GROUP_W), F32)] * 2
                   + [jax.ShapeDtypeStruct((s, LANES), F32), _by_residue(s, d4, LANES, F32),
                      _by_residue(s, d16, LANES, F32), jax.ShapeDtypeStruct((s, LRU_SAVED * GROUP_W), F32)]),
        scratch_shapes=([pltpu.VMEM((SUBLANES, GROUP_W), F32)] + _stage_scratch(TILE_MIX, GROUP_W, 4)
                        + _stage_scratch(TILE_MIX, LANES, 1)),
        compiler_params=_params(("arbitrary",)),
    )(*([z] * 13), *[a for pair in attn for a in pair], *[wts[k] for k in names])


def mix_bwd(dy, z, hl, lru, dqkv, ddg, wts, l):
    s = z.shape[0]
    d4, d16 = PATTERN_DILS[1], PATTERN_DILS[2]
    n_tiles = s // TILE_MIX

    def body(dya_ref, dyb_ref, dyc_ref, ax_ref, ab_ref, ac_ref, ag_ref, rx_ref, rg_ref, cu_ref, cv_ref, cg_ref,
             axh_ref, ach_ref, rxh_ref, hl_ref, hlh_ref, lru_ref, dqkv1_ref, dqkv4_ref, dqkv16_ref, ddg_ref,
             caw_ref, crw_ref, crb_ref, wa_ref, wx_ref, ba_ref, bx_ref, lam_ref, gng_ref, ws_ref, wst_ref, bs_ref,
             dz_ref, ga_ref, gr_ref, gn_ref, gwa_ref, gwx_ref, gws_ref, gbs_ref,
             c_dcv, c_g, c_a, c_dxb, *stage):
        st_a, st_b = stage[:len(stage) // 2], stage[len(stage) // 2:]
        step = pl.program_id(0)
        i = n_tiles - 1 - step

        @pl.when(step == 0)
        def _():
            for r in (c_dcv, c_g, c_a, c_dxb, ga_ref, gr_ref, gn_ref, gwa_ref, gwx_ref, gws_ref, gbs_ref):
                r[...] = jnp.zeros_like(r)

        nz = (i > 0).astype(F32)
        head = _head_of_lane((TILE_MIX, GROUP_W))
        shp8 = (SUBLANES, GROUP_W)
        colsum = lambda v: jnp.sum(v, axis=0, keepdims=True)

        ax, ab, ac, ag = ax_ref[...], ab_ref[...], ac_ref[...], ag_ref[...]
        dya = dya_ref[...]
        pa = ac * ax
        pah = ach_ref[...] * axh_ref[...] * nz
        pa1 = _shift_down(pa, pah, 1)
        pa2 = _shift_down(pa, pah, 2)
        cv = caw_ref[2:3, :] * pa + caw_ref[1:2, :] * pa1 + caw_ref[0:1, :] * pa2
        sg, dsg = _silu_and_grad(ag)
        dz_ref[:, C_AB * GROUP_W:(C_AB + 1) * GROUP_W] = (dya * cv * sg).astype(BF16)
        dz_ref[:, C_AG * GROUP_W:(C_AG + 1) * GROUP_W] = (dya * ab * cv * dsg).astype(BF16)
        dcv = dya * ab * sg
        nxt = c_dcv[...]
        dpa = caw_ref[2:3, :] * dcv + caw_ref[1:2, :] * _shift_up(dcv, nxt, 1) + caw_ref[0:1, :] * _shift_up(dcv, nxt, 2)
        c_dcv[...] = dcv[:SUBLANES, :]
        dz_ref[:, C_AC * GROUP_W:(C_AC + 1) * GROUP_W] = (dpa * ax).astype(BF16)
        dz_ref[:, C_AX * GROUP_W:(C_AX + 1) * GROUP_W] = (dpa * ac).astype(BF16)
        ga_ref[...] += (_put_row(shp8, 2, colsum(dcv * pa)) + _put_row(shp8, 1, colsum(dcv * pa1))
                        + _put_row(shp8, 0, colsum(dcv * pa2)))

        rx, rg = rx_ref[...], rg_ref[...]
        dyb = dyb_ref[...]
        rxh = rxh_ref[...] * nz
        rx1, rx2, rx3 = _shift_down(rx, rxh, 1), _shift_down(rx, rxh, 2), _shift_down(rx, rxh, 3)
        xb, r, ig, a, mult = (lru_ref[:, k * GROUP_W:(k + 1) * GROUP_W] for k in range(LRU_SAVED))
        lam = lam_ref[...]
        sp = _softplus_neg(lam)
        hl = hl_ref[...]
        hprev = _shift_down(hl, hlh_ref[...] * nz, 1)
        sgr, dsgr = _silu_and_grad(rg)
        dz_ref[:, C_RG * GROUP_W:(C_RG + 1) * GROUP_W] = (dyb * hl * dsgr).astype(BF16)
        dh = dyb * sgr
        a_next = _shift_up(a, c_a[...], 1)
        g = _scan_anticausal(a_next, dh, c_g[0:1, :])
        c_g[...] = g[:SUBLANES, :]
        c_a[...] = a[:SUBLANES, :]
        u = ig * xb
        da = g * hprev
        dmult = g * u
        du = g * mult
        dlog_a = da * a - dmult * (a * a) / mult
        dr = dlog_a * (-RG_C * sp)
        dga = dr * r * (1.0 - r)
        dgx = (du * xb) * ig * (1.0 - ig)
        dgab, dgxb = dga.astype(BF16), dgx.astype(BF16)
        dxb = du * ig + _dot_nt(dgab, wa_ref[...]) + _dot_nt(dgxb, wx_ref[...])
        xbb = xb.astype(BF16)
        gwa_ref[...] += _dot_tn(xbb, dgab)
        gwx_ref[...] += _dot_tn(xbb, dgxb)
        nxt = c_dxb[...]
        drx = (crw_ref[3:4, :] * dxb + crw_ref[2:3, :] * _shift_up(dxb, nxt, 1) + crw_ref[1:2, :] * _shift_up(dxb, nxt, 2)
               + crw_ref[0:1, :] * _shift_up(dxb, nxt, 3))
        c_dxb[...] = dxb[:SUBLANES, :]
        dz_ref[:, C_RX * GROUP_W:(C_RX + 1) * GROUP_W] = drx.astype(BF16)
        dlam = colsum(dlog_a * (-RG_C * r)) * (-_sigmoid(-lam))
        gr_ref[...] += (_put_row(shp8, 3, colsum(dxb * rx)) + _put_row(shp8, 2, colsum(dxb * rx1))
                        + _put_row(shp8, 1, colsum(dxb * rx2)) + _put_row(shp8, 0, colsum(dxb * rx3))
                        + _put_row(shp8, 4, colsum(dxb)) + _put_row(shp8, 5, colsum(dga))
                        + _put_row(shp8, 6, colsum(dgx)) + _put_row(shp8, 7, dlam))

        cu, cvv, cg = cu_ref[...], cv_ref[...], cg_ref[...]
        dyc = dyc_ref[...]
        u_c, du_c = _gelu_and_grad(cu)
        gv, dgv_c = _gelu_and_grad(cvv)
        rs = lax.rsqrt(jnp.mean(gv * gv, axis=-1, keepdims=True) + NORM_EPS)
        vh = gv * rs
        gng = gng_ref[...]
        vvb = (vh * gng).astype(BF16)
        spat = _gmlp_spatial(ws_ref, vvb, head) + jnp.concatenate([bs_ref[...]] * (TILE_MIX // GMLP_CHUNK), axis=0)
        sgc, dsgc = _silu_and_grad(cg)
        dz_ref[:, C_CU * GROUP_W:(C_CU + 1) * GROUP_W] = (dyc * spat * sgc * du_c).astype(BF16)
        dz_ref[:, C_CG * GROUP_W:(C_CG + 1) * GROUP_W] = (dyc * u_c * spat * dsgc).astype(BF16)
        dsp = dyc * u_c * sgc
        dspb = dsp.astype(BF16)
        tril = (lax.broadcasted_iota(jnp.int32, (GMLP_CHUNK, GMLP_CHUNK), 0)
                >= lax.broadcasted_iota(jnp.int32, (GMLP_CHUNK, GMLP_CHUNK), 1))
        head_c = head[:GMLP_CHUNK]
        dvv_parts = []
        gbs = jnp.zeros((GMLP_CHUNK, GROUP_W), F32)
        for j in range(TILE_MIX // GMLP_CHUNK):
            sl = slice(j * GMLP_CHUNK, (j + 1) * GMLP_CHUNK)
            dblk = dspb[sl, :]
            vblk = vvb[sl, :]
            gbs = gbs + dsp[sl, :]
            acc = jnp.zeros((GMLP_CHUNK, GROUP_W), F32)
            for h in range(N_HEADS):
                acc = jnp.where(head_c == h, _dot(wst_ref[h], dblk), acc)
                dm = jnp.where(head_c == h, dblk, jnp.zeros_like(dblk))
                gws_ref[h] += jnp.where(tril, _dot_nt(dm, vblk), 0.0)
            dvv_parts.append(acc)
        gbs_ref[...] += gbs
        dvv = jnp.concatenate(dvv_parts, axis=0)
        gn_ref[...] += _put_row(shp8, 0, colsum(dvv * vh))
        dvh = dvv * gng
        dgv = rs * (dvh - vh * jnp.mean(dvh * vh, axis=-1, keepdims=True))
        dz_ref[:, C_CV * GROUP_W:(C_CV + 1) * GROUP_W] = (dgv * dgv_c).astype(BF16)

        dsum = dqkv1_ref[...] + _interleave_load(dqkv4_ref, d4, st_a) + _interleave_load(dqkv16_ref, d16, st_b)
        dz_ref[:, C_DQ * GROUP_W:(C_DV + 1) * GROUP_W] = dsum.astype(BF16)
        dz_ref[:, C_DG * GROUP_W:(C_DG + 1) * GROUP_W] = ddg_ref[...].astype(BF16)

    per = TILE_MIX // SUBLANES
    qkv_w = 3 * GROUP_W
    rev = lambda c: pl.BlockSpec((TILE_MIX, GROUP_W), lambda t, c=c: (n_tiles - 1 - t, c))
    revh = lambda c: pl.BlockSpec((SUBLANES, GROUP_W),
                                  lambda t, c=c: (jnp.maximum((n_tiles - 1 - t) * per - 1, 0), c))
    revr = lambda dil: pl.BlockSpec((dil, TILE_MIX // dil, qkv_w), lambda t: (0, n_tiles - 1 - t, 0))
    names = ("caw", "crw", "crb", "wa", "wx", "ba", "bx", "lam", "gng", "ws", "wst", "bs")
    in_specs = ([rev(0), rev(1), rev(2)]
                + [rev(c) for c in (C_AX, C_AB, C_AC, C_AG, C_RX, C_RG, C_CU, C_CV, C_CG)]
                + [revh(C_AX), revh(C_AC), revh(C_RX), rev(0), revh(0),
                   pl.BlockSpec((TILE_MIX, LRU_SAVED * GROUP_W), lambda t: (n_tiles - 1 - t, 0)),
                   pl.BlockSpec((TILE_MIX, qkv_w), lambda t: (n_tiles - 1 - t, 0)), revr(d4), revr(d16), rev(0)]
                + [_of_layer(wts[k], l) for k in names])
    small = jax.ShapeDtypeStruct((SUBLANES, GROUP_W), F32)
    sq = jax.ShapeDtypeStruct((GROUP_W, GROUP_W), F32)
    out_shape = [jax.ShapeDtypeStruct((s, D_IN), BF16), small, small, small, sq, sq,
                 jax.ShapeDtypeStruct((N_HEADS, GMLP_CHUNK, GMLP_CHUNK), F32),
                 jax.ShapeDtypeStruct((GMLP_CHUNK, GROUP_W), F32)]
    out_specs = ([pl.BlockSpec((TILE_MIX, D_IN), lambda t: (n_tiles - 1 - t, 0))]
                 + [_full(o.shape) for o in out_shape[1:]])
    return pl.pallas_call(
        body, name="mix_bwd", grid=(n_tiles,),
        in_specs=in_specs, out_specs=out_specs, out_shape=out_shape,
        scratch_shapes=[pltpu.VMEM((SUBLANES, GROUP_W), F32)] * 4 + _stage_scratch(TILE_MIX, qkv_w, 2),
        compiler_params=_params(("arbitrary",)),
    )(dy, dy, dy, *([z] * 12), hl, hl, lru, *dqkv, ddg, *[wts[k] for k in names])


def prepare_small_weights(p):
    tril = jnp.tril(jnp.ones((GMLP_CHUNK, GMLP_CHUNK), dtype=bool))
    ws = jnp.where(tril, p["gmlp_ws"], 0.0).astype(BF16)
    row = lambda a: a[:, None, :]
    eye = jnp.eye(N_HEADS, dtype=F32)
    bd = lambda w: (w[:, :, :, None, :] * eye[None, :, None, :, None]).reshape(-1, GROUP_W, GROUP_W).astype(BF16)
    return dict(
        caw=p["conv_a_w"], crw=p["conv_r_w"], crb=row(p["conv_r_b"]),
        wa=bd(p["lru_wa"]), wx=bd(p["lru_wx"]),
        ba=row(p["lru_ba"]), bx=row(p["lru_bx"]), lam=row(p["lru_lambda"]), gng=row(p["gmlp_norm_g"]),
        ws=ws, wst=jnp.swapaxes(ws, 2, 3),
        bs=jnp.repeat(jnp.swapaxes(p["gmlp_bs"], 1, 2), HEAD_DIM, axis=2))


def _flat(a):
    return a.reshape(a.shape[0] * a.shape[1], a.shape[2])


def _split(a, dil):
    return a.reshape(dil, a.shape[0] // dil, a.shape[1])


def local_step(x, tgt, final_g, depth, chip, layer_weights, projections_done):
    saved = []
    for l in range(depth):
        gain, w_in_l, rest = layer_weights(l, x)
        z, h, *qkvs = in_fwd(x, gain, w_in_l)
        w_out_l, wts = rest(z)
        qkvs = [_flat(q) if q.ndim == 3 else q for q in qkvs]
        attn = []
        for q, d in zip(qkvs, PATTERN_DILS):
            o_p, lse_p = attn_fwd(q, d)
            attn.append((o_p, lse_p) if d == 1 else (_split(o_p, d), _split(lse_p, d)))
        y, hl, o, lse, lse4, lse16, lru = mix_fwd(z, attn, wts, l)
        saved.append(dict(x=x, z=z, h=h, y=y, hl=hl, o=o, qkvs=qkvs, lses=(lse, _flat(lse4), _flat(lse16)), wts=wts, lru=lru,
                          gain=gain, w_in=w_in_l, w_out=w_out_l))
        if l < depth - 1:
            x = out_fwd(y, w_out_l, x)
        else:
            loss, dx, dfg = out_fwd_loss(y, w_out_l, x, final_g[None, :], tgt)
    raw = {k: [None] * depth for k in ("gain", "a", "r", "n", "wa", "wx", "ws", "bs")}
    zero = None
    for l in reversed(range(depth)):
        sv = saved[l]
        dy, ddg, do1, do4, do16, dl1, dl4, dl16 = out_bwd(dx, sv["w_out"], sv["z"], sv["o"])
        g_w_out = grad_w_out(sv["y"], dx)
        dqkv = []
        for q, do, lse, dl, d in zip(sv["qkvs"], (do1, _flat(do4), _flat(do16)), sv["lses"],
                                     (dl1, _flat(dl4), _flat(dl16)), PATTERN_DILS):
            g = attn_bwd(q, do, lse, dl, d)
            dqkv.append(g if d == 1 else _split(g, d))
        dz, ga, gr, gn, gwa, gwx, gws, gbs = mix_bwd(dy, sv["z"], sv["hl"], sv["lru"], dqkv, ddg, sv["wts"], l)
        gain = sv["gain"] if zero is None else sv["gain"] + zero
        zero = projections_done(l, *grad_w_in(sv["h"], dz, chip), g_w_out)
        if l == 0 and zero is not None:
            gain = gain + zero
        dx, dgn = in_bwd(dz, sv["w_in"], sv["x"], gain, dx)
        for k, g in zip(("gain", "a", "r", "n", "wa", "wx", "ws", "bs"), (dgn, ga, gr, gn, gwa, gwx, gws, gbs)):
            raw[k][l] = g
    st = {k: jnp.stack(v) for k, v in raw.items()}
    eye = jnp.eye(N_HEADS, dtype=F32)[None, :, None, :, None]
    diag = lambda g: (g.reshape(depth, N_HEADS, HEAD_DIM, N_HEADS, HEAD_DIM) * eye).sum(axis=3)
    grads = dict(
        norm_g=st["gain"][:, 0], conv_a_w=st["a"][:, :3], conv_r_w=st["r"][:, :4], conv_r_b=st["r"][:, 4],
        lru_ba=st["r"][:, 5], lru_bx=st["r"][:, 6], lru_lambda=st["r"][:, 7], gmlp_norm_g=st["n"][:, 0],
        lru_wa=diag(st["wa"]), lru_wx=diag(st["wx"]), gmlp_ws=st["ws"],
        gmlp_bs=jnp.swapaxes(st["bs"].reshape(depth, GMLP_CHUNK, N_HEADS, HEAD_DIM).sum(-1), 1, 2),
        final_g=dfg[0])
    return loss, dx, grads


MESH = pl.DeviceIdType.MESH
N_CHIPS = 4
N_DEV = 8
ANY = pl.BlockSpec(memory_space=pl.ANY)


def _place():
    x, y, c = lax.axis_index("x"), lax.axis_index("y"), lax.axis_index("c")
    chips = [(1 - x, y), (x, 1 - y), (1 - x, 1 - y)]
    return x, y, c, chips


def _remote(src, dst, ssem, rsem, to):
    return pltpu.make_async_remote_copy(src_ref=src, dst_ref=dst, send_sem=ssem, recv_sem=rsem,
                                        device_id=to, device_id_type=MESH)


HBM = pl.BlockSpec(memory_space=pltpu.HBM)
SEM = pl.BlockSpec(memory_space=pltpu.SEMAPHORE)
DATAFLOW = pltpu.SideEffectType.DATAFLOW_SIDE_EFFECTING
GATHER, SCATTER = "gather", "scatter"


def _chip_copies(mode, src_refs, land_refs, ssem, rsem):
    x, y, c, chips = _place()
    me = 2 * x + y
    n = len(src_refs)
    copies = []
    for k, (cx, cy) in enumerate(chips):
        for a in range(n):
            if mode == GATHER:
                src, dst = src_refs[a], land_refs[a].at[me]
            else:
                src, dst = src_refs[a].at[2 * cx + cy], land_refs[a].at[k]
            copies.append(_remote(src, dst, ssem.at[n * k + a], rsem.at[n * k + a], (cx, cy, c)))
    return copies


def exchange_start(mode, srcs, after, name):
    n = len(srcs)
    if mode == GATHER:
        lands = [lax.empty((N_CHIPS,) + s.shape, s.dtype) for s in srcs]
    else:
        lands = [lax.empty((N_CHIPS - 1,) + s.shape[1:], s.dtype) for s in srcs]
    extra = [] if after is None else [after]

    def body(*refs):
        src_refs, land_refs = refs[:n], refs[n:2 * n]
        ssem, rsem = refs[2 * n + len(extra)], refs[2 * n + len(extra) + 1]
        token = refs[-1]
        for cp in _chip_copies(mode, src_refs, land_refs, ssem, rsem):
            cp.start()
        token[...] = jnp.zeros_like(token)

    arrays = list(srcs) + lands
    return pl.pallas_call(
        body, name=name,
        out_shape=(pltpu.SemaphoreType.DMA((3 * n,)), pltpu.SemaphoreType.DMA((3 * n,)),
                   *[pltpu.HBM(a.shape, a.dtype) for a in arrays], jax.ShapeDtypeStruct((SUBLANES, LANES), F32)),
        in_specs=[HBM] * (2 * n) + [ANY] * len(extra),
        out_specs=(SEM, SEM, *[HBM] * (2 * n), pl.BlockSpec(memory_space=pltpu.VMEM)),
        input_output_aliases={i: 2 + i for i in range(2 * n)},
        compiler_params=pltpu.CompilerParams(has_side_effects=DATAFLOW),
    )(*[pltpu.with_memory_space_constraint(a, pltpu.HBM) for a in arrays], *extra)


def exchange_wait(mode, started, after, name):
    ssem, rsem, *thru, _ = started
    n = len(thru) // 2

    def body(*refs):
        src_refs, land_refs = refs[:n], refs[n:2 * n]
        ssem_ref, rsem_ref = refs[2 * n], refs[2 * n + 1]
        for cp in _chip_copies(mode, src_refs, land_refs, ssem_ref, rsem_ref):
            cp.wait_send()
            cp.wait_recv()

    outs = pl.pallas_call(
        body, name=name,
        out_shape=[pltpu.HBM(a.shape, a.dtype) for a in thru],
        in_specs=[HBM] * (2 * n) + [SEM, SEM, ANY],
        out_specs=[HBM] * (2 * n),
        input_output_aliases={i: i for i in range(2 * n)},
        compiler_params=pltpu.CompilerParams(has_side_effects=DATAFLOW),
    )(*thru, ssem, rsem, after)
    return outs[n:]


def sibling_exchange(p1, p2):
    def body(p1_ref, p2_ref, q1_ref, q2_ref, ssem, rsem):
        x, y, c, _ = _place()
        copies = [_remote(p_ref, q_ref, ssem.at[a], rsem.at[a], (x, y, 1 - c))
                  for a, (p_ref, q_ref) in enumerate(((p1_ref, q1_ref), (p2_ref, q2_ref)))]
        for cp in copies:
            cp.start()
        for cp in copies:
            cp.wait()

    return pl.pallas_call(
        body, name="sibling_exchange",
        in_specs=[ANY, ANY], out_specs=[ANY, ANY],
        out_shape=[jax.ShapeDtypeStruct(p.shape, p.dtype) for p in (p1, p2)],
        scratch_shapes=[pltpu.SemaphoreType.DMA((2,)), pltpu.SemaphoreType.DMA((2,))],
    )(p1, p2)


def all_reduce_small(v):
    r, n = v.shape
    piece = r // N_DEV

    def body(x_ref, out_ref, recv, ssem1, rsem1, ssem2, rsem2):
        x, y, c, _ = _place()
        me = 4 * x + 2 * y + c

        def peer(k):
            px = 1 - x if (k >> 2) & 1 else x
            py = 1 - y if (k >> 1) & 1 else y
            pc = 1 - c if k & 1 else c
            return (px, py, pc), 4 * px + 2 * py + pc

        def rows(ref, d):
            return ref.at[pl.ds(d * piece, piece), :]

        scatter = []
        for k in range(1, N_DEV):
            to, idx = peer(k)
            scatter.append(_remote(rows(x_ref, idx), recv.at[k], ssem1.at[k - 1], rsem1.at[k - 1], to))
            scatter[-1].start()
        acc = rows(x_ref, me)[...]
        for k in range(1, N_DEV):
            scatter[k - 1].wait_recv()
            acc = acc + recv[k]
        rows(out_ref, me)[...] = acc

        gather = []
        for k in range(1, N_DEV):
            to, _ = peer(k)
            gather.append(_remote(rows(out_ref, me), rows(out_ref, me), ssem2.at[k - 1], rsem2.at[k - 1], to))
            gather[-1].start()
        for k in range(1, N_DEV):
            to, idx = peer(k)
            _remote(rows(out_ref, idx), rows(out_ref, idx), ssem2.at[k - 1], rsem2.at[k - 1], to).wait_recv()
        for cp in scatter + gather:
            cp.wait_send()

    return pl.pallas_call(
        body, name="all_reduce_small",
        out_shape=jax.ShapeDtypeStruct((r, n), v.dtype),
        in_specs=[pl.BlockSpec(memory_space=pltpu.VMEM)],
        out_specs=pl.BlockSpec(memory_space=pltpu.VMEM),
        scratch_shapes=[pltpu.VMEM((N_DEV, piece, n), v.dtype)] + [pltpu.SemaphoreType.DMA((N_DEV - 1,))] * 4,
        compiler_params=pltpu.CompilerParams(vmem_limit_bytes=VMEM_LIMIT),
    )(v)


TILE_ROWS = 256


def _row_tile(r):
    return max(t for t in range(SUBLANES, TILE_ROWS + 1, SUBLANES) if r % t == 0)


def sum_partials(owns, parts):
    k, r, c = parts[0].shape
    depth = len(owns)
    tile = _row_tile(r)

    def body(buf_ref, o_ref, p_ref, out_ref):
        acc = o_ref[...]
        for i in range(k):
            acc = acc + p_ref[i].astype(F32)
        out_ref[0] = acc

    out = lax.empty((depth, r, c), F32)
    for l in range(depth):
        out = pl.pallas_call(
            functools.partial(body), name="sum_partials", grid=(r // tile,),
            in_specs=[ANY, pl.BlockSpec((tile, c), lambda i: (i, 0)), pl.BlockSpec((k, tile, c), lambda i: (0, i, 0))],
            out_specs=pl.BlockSpec((1, tile, c), lambda i, l=l: (l, i, 0)),
            out_shape=jax.ShapeDtypeStruct((depth, r, c), F32),
            input_output_aliases={0: 0},
            compiler_params=_params(("parallel",)),
        )(out, owns[l], parts[l])
    return out


def _adamw_update(w, g, m, v):
    m2 = ADAM_B1 * m + (1.0 - ADAM_B1) * g
    v2 = ADAM_B2 * v + (1.0 - ADAM_B2) * (g * g)
    m_hat = m2 / (1.0 - ADAM_B1 ** ADAM_STEP)
    v_hat = v2 / (1.0 - ADAM_B2 ** ADAM_STEP)
    return -ADAM_LR * (m_hat / (jnp.sqrt(v_hat) + ADAM_EPS) + ADAM_WD * w), m2, v2


def adamw_small(ws, gs, ms, vs):
    n = len(ws)

    def body(*refs):
        ins, outs = refs[:4 * n], refs[4 * n:]
        for i in range(n):
            d, m2, v2 = _adamw_update(ins[i][...], ins[n + i][...], ins[2 * n + i][...], ins[3 * n + i][...])
            outs[3 * i][...] = d
            outs[3 * i + 1][...] = m2
            outs[3 * i + 2][...] = v2

    outs = pl.pallas_call(
        body, name="adamw_small",
        out_shape=[jax.ShapeDtypeStruct(w.shape, F32) for w in ws for _ in range(3)],
    )(*ws, *gs, *ms, *vs)
    return [tuple(outs[3 * i:3 * i + 3]) for i in range(n)]


def adamw(w, ga, gb, m, v):
    n, r, c = w.shape
    tile = _row_tile(r)

    def body(w_ref, ga_ref, gb_ref, m_ref, v_ref, g_ref, d_ref, m2_ref, v2_ref):
        g = ga_ref[...] + gb_ref[...]
        g_ref[...] = g
        d_ref[...], m2_ref[...], v2_ref[...] = _adamw_update(w_ref[...], g, m_ref[...], v_ref[...])

    spec = pl.BlockSpec((1, tile, c), lambda j, i: (j, i, 0))
    return pl.pallas_call(
        body, name="adamw", grid=(n, r // tile),
        in_specs=[spec] * 5, out_specs=[spec] * 4,
        out_shape=[jax.ShapeDtypeStruct((n, r, c), F32)] * 4,
        compiler_params=_params(("parallel", "parallel")),
    )(w, ga, gb, m, v)


REPLICATED = ("norm_g", "conv_r_b", "lru_wa", "lru_ba", "lru_wx", "lru_bx", "lru_lambda", "gmlp_norm_g",
              "gmlp_ws", "gmlp_bs", "final_g")
CHIP_SHARDED_SMALL = ("conv_a_w", "conv_r_w")
PACK_LANES = LANES


def _pack(arrays):
    flat = jnp.concatenate([a.reshape(-1) for a in arrays])
    pad = (-flat.shape[0]) % (TILE_ROWS * PACK_LANES)
    return jnp.pad(flat, (0, pad)).reshape(-1, PACK_LANES)


def _unpack(packed, shapes):
    flat = packed.reshape(-1)
    out, off = [], 0
    for shp in shapes:
        n = math.prod(shp)
        out.append(flat[off:off + n].reshape(shp))
        off += n
    return out


def kernel(x, norm_g, w_in, conv_a_w, conv_r_w, conv_r_b, lru_wa, lru_ba, lru_wx, lru_bx, lru_lambda, gmlp_norm_g, gmlp_ws, gmlp_bs, w_out, final_g, loss_target, m_norm_g, m_w_in, m_conv_a_w, m_conv_r_w, m_conv_r_b, m_lru_wa, m_lru_ba, m_lru_wx, m_lru_bx, m_lru_lambda, m_gmlp_norm_g, m_gmlp_ws, m_gmlp_bs, m_w_out, m_final_g, v_norm_g, v_w_in, v_conv_a_w, v_conv_r_w, v_conv_r_b, v_lru_wa, v_lru_ba, v_lru_wx, v_lru_bx, v_lru_lambda, v_gmlp_norm_g, v_gmlp_ws, v_gmlp_bs, v_w_out, v_final_g):
    names = ("norm_g", "w_in", "conv_a_w", "conv_r_w", "conv_r_b", "lru_wa", "lru_ba", "lru_wx", "lru_bx",
             "lru_lambda", "gmlp_norm_g", "gmlp_ws", "gmlp_bs", "w_out", "final_g")
    w = dict(zip(names, (norm_g, w_in, conv_a_w, conv_r_w, conv_r_b, lru_wa, lru_ba, lru_wx, lru_bx, lru_lambda,
                         gmlp_norm_g, gmlp_ws, gmlp_bs, w_out, final_g)))
    m = dict(zip(names, (m_norm_g, m_w_in, m_conv_a_w, m_conv_r_w, m_conv_r_b, m_lru_wa, m_lru_ba, m_lru_wx, m_lru_bx,
                         m_lru_lambda, m_gmlp_norm_g, m_gmlp_ws, m_gmlp_bs, m_w_out, m_final_g)))
    v = dict(zip(names, (v_norm_g, v_w_in, v_conv_a_w, v_conv_r_w, v_conv_r_b, v_lru_wa, v_lru_ba, v_lru_wx, v_lru_bx,
                         v_lru_lambda, v_gmlp_norm_g, v_gmlp_ws, v_gmlp_bs, v_w_out, v_final_g)))
    depth = w_in.shape[0]
    out_rows = w_out.shape[1]
    conv_ch = conv_a_w.shape[2]
    chip = 2 * lax.axis_index("x") + lax.axis_index("y")

    taps = conv_a_w.shape[1] + conv_r_w.shape[1]
    w_in_t, m_w_in_t, v_w_in_t = (jnp.swapaxes(a, 1, 2) for a in (w_in, m_w_in, v_w_in))
    w_in_h, w_out_h = w_in_t.astype(BF16), w_out.astype(BF16)
    conv_own = jnp.concatenate([conv_a_w, conv_r_w], axis=1).reshape(depth * taps, conv_ch)
    gathers, token = [], None
    for l in range(depth):
        groups = [[w_in_h[l]], [w_out_h[l], conv_own]] if l == 0 else [[w_in_h[l], w_out_h[l]]]
        gathers.append([])
        for i, srcs in enumerate(groups):
            gathers[l].append(exchange_start(GATHER, srcs, token, f"gather_start_{l}_{i}"))
            token = gathers[l][-1][-1]
    p = dict(w)

    def with_own(land, own):
        return lax.dynamic_update_slice(land, own[None], (chip,) + (0,) * own.ndim)

    def layer_weights(l, x_l):
        lands = list(exchange_wait(GATHER, gathers[l][0], x_l, f"gather_wait_{l}_0"))
        w_in_l = with_own(lands[0], w_in_h[l]).reshape(D_IN, D_MODEL)
        gain = norm_g[l][None, :]
        if l == 0:
            gain = gain + token[0, 0]

        def rest(z_l):
            if l == 0:
                lands.extend(exchange_wait(GATHER, gathers[l][1], z_l, f"gather_wait_{l}_1"))
                conv = with_own(lands[2], conv_own).reshape(N_CHIPS, depth, taps, conv_ch)
                conv = conv.transpose(1, 2, 0, 3).reshape(depth, taps, GROUP_W)
                p["conv_a_w"] = conv[:, :conv_a_w.shape[1]]
                p["conv_r_w"] = conv[:, conv_a_w.shape[1]:]
                p["prepared"] = prepare_small_weights(p)
            return with_own(lands[1], w_out_h[l]).reshape(D_MIX, D_MODEL), p["prepared"]

        return gain, w_in_l, rest

    scatters, owns = [None] * depth, [None] * depth

    def projections_done(l, g_w_in_by_chip, g_w_in_own, g_w_out):
        go = g_w_out.reshape(N_CHIPS, out_rows, D_MODEL)
        owns[l] = (g_w_in_own, lax.dynamic_index_in_dim(go, chip, axis=0, keepdims=False))
        scatters[l] = exchange_start(SCATTER, [g_w_in_by_chip, go.astype(BF16)], None, f"scatter_start_{l}")
        return scatters[l][-1][0, 0]

    loss8, dx, grads = local_step(x[0], loss_target[0], final_g, depth, chip.reshape(1), layer_weights,
                                  projections_done)

    res = {}
    small = REPLICATED + CHIP_SHARDED_SMALL
    packed = _pack([grads[k] for k in small] + [loss8[0, :1]])
    total = all_reduce_small(packed)
    *sums, loss = _unpack(total, [grads[k].shape for k in small] + [()])
    gs = dict(zip(small, sums))
    for k in CHIP_SHARDED_SMALL:
        gs[k] = lax.dynamic_slice_in_dim(gs[k], chip * conv_ch, conv_ch, axis=2)
    as2d = lambda a: a[None] if a.ndim == 1 else a
    outs = adamw_small(*[[as2d(d[k]) for k in small] for d in (w, gs, m, v)])
    for k, (delta, m2, v2) in zip(small, outs):
        res[k] = [t.reshape(w[k].shape) for t in (gs[k], delta, m2, v2)]

    parts = [exchange_wait(SCATTER, scatters[l], total, f"scatter_wait_{l}") for l in range(depth)]
    p1 = sum_partials([owns[l][0] for l in range(depth)], [parts[l][0] for l in range(depth)])
    p2 = sum_partials([owns[l][1] for l in range(depth)], [parts[l][1] for l in range(depth)])
    q1, q2 = sibling_exchange(p1, p2)
    res["w_in"] = [jnp.swapaxes(t, 1, 2) for t in adamw(w_in_t, p1, q1, m_w_in_t, v_w_in_t)]
    res["w_out"] = adamw(w_out, p2, q2, m_w_out, v_w_out)

    return (loss, dx[None], *[res[k][0] for k in names], *[res[k][1] for k in names],
            *[res[k][2] for k in names], *[res[k][3] for k in names])
```

```python
import functools
import math

import jax
import jax.numpy as jnp
import numpy as np
from jax import lax
from jax.experimental import pallas as pl
from jax.experimental.pallas import tpu as pltpu

F32 = jnp.float32
BF16 = jnp.bfloat16

D_MODEL = 1024
GROUP_W = 256
N_HEADS = 4
HEAD_DIM = 64
N_CHUNKS = 13
D_IN = N_CHUNKS * GROUP_W
D_MIX = 4 * GROUP_W
NORM_EPS = 1e-6
RG_C = 8.0
GMLP_CHUNK = 128
ATTN_BLOCK = 128
PATTERN_DILS = (1, 4, 16)
ALIBI_SLOPES = tuple(2.0 ** (-8.0 * (h + 1) / N_HEADS) for h in range(N_HEADS))
ATTN_SCALE = 1.0 / math.sqrt(HEAD_DIM)
NEG_BIG = -1e30

ADAM_LR = 0.001
ADAM_B1 = 0.9
ADAM_B2 = 0.999
ADAM_EPS = 1e-08
ADAM_WD = 0.01
ADAM_STEP = 10

C_AX, C_AB, C_AC, C_AG, C_RX, C_RG, C_CU, C_CV, C_CG, C_DQ, C_DK, C_DV, C_DG = range(13)

SUBLANES = 8
LANES = 128
VMEM_LIMIT = 56 * 1024 * 1024
TILE_IN = 512
TILE_IN_BWD = 1024
TILE_OUT = 1024
TILE_MIX = 512
TILE_DW = 1024
TILE_DW_OUT = 2048
ATTN_QB = 16
GELU_K0 = math.sqrt(2.0 / math.pi)
GELU_K1 = 0.044715


def _resident(shape):
    return pl.BlockSpec(shape, lambda i: tuple(0 for _ in shape), pipeline_mode=pl.Buffered(1))


def _params(sem):
    return pltpu.CompilerParams(dimension_semantics=sem, vmem_limit_bytes=VMEM_LIMIT)


def _sigmoid(x):
    return 0.5 * jnp.tanh(0.5 * x) + 0.5


def _silu(x):
    return x * _sigmoid(x)


def _silu_and_grad(x):
    s = _sigmoid(x)
    return x * s, s * (1.0 + x * (1.0 - s))


def _gelu(x):
    return x * (0.5 + 0.5 * jnp.tanh(x * (GELU_K0 + (GELU_K0 * GELU_K1) * (x * x))))


def _gelu_and_grad(x):
    x2 = x * x
    t = jnp.tanh(x * (GELU_K0 + (GELU_K0 * GELU_K1) * x2))
    half = 0.5 + 0.5 * t
    return x * half, half + x * (0.5 - 0.5 * t * t) * (GELU_K0 + (3.0 * GELU_K0 * GELU_K1) * x2)


def _neg_expm1_2x(x):
    t = jnp.tanh(x)
    return -2.0 * t / (1.0 - t)


def _shift_down(v, halo, k):
    r = pltpu.roll(v, k, 0)
    rh = pltpu.roll(halo, k, 0)
    row = lax.broadcasted_iota(jnp.int32, halo.shape, 0)
    top = jnp.where(row < k, rh, r[:SUBLANES])
    return jnp.concatenate([top, r[SUBLANES:]], axis=0)


def _shift_up(v, halo, k):
    t = v.shape[0]
    r = pltpu.roll(v, t - k, 0)
    rh = pltpu.roll(halo, SUBLANES - k, 0)
    row = lax.broadcasted_iota(jnp.int32, halo.shape, 0)
    bot = jnp.where(row >= SUBLANES - k, rh, r[t - SUBLANES:])
    return jnp.concatenate([r[:t - SUBLANES], bot], axis=0)


def _scan_causal(a, b, h_in):
    t = a.shape[0]
    row8 = lax.broadcasted_iota(jnp.int32, a.shape, 0) % SUBLANES
    d = 1
    while d < SUBLANES:
        m = row8 >= d
        a_s = jnp.where(m, pltpu.roll(a, d, 0), 1.0)
        b_s = jnp.where(m, pltpu.roll(b, d, 0), 0.0)
        b = a * b_s + b
        a = a * a_s
        d *= 2
    out, carry = [], h_in
    for g in range(t // SUBLANES):
        sl = slice(g * SUBLANES, (g + 1) * SUBLANES)
        hg = b[sl] + a[sl] * carry
        out.append(hg)
        carry = hg[SUBLANES - 1:SUBLANES]
    return jnp.concatenate(out, axis=0)


def _scan_anticausal(a, b, g_in):
    t = a.shape[0]
    row8 = lax.broadcasted_iota(jnp.int32, a.shape, 0) % SUBLANES
    d = 1
    while d < SUBLANES:
        m = row8 < SUBLANES - d
        a_s = jnp.where(m, pltpu.roll(a, t - d, 0), 1.0)
        b_s = jnp.where(m, pltpu.roll(b, t - d, 0), 0.0)
        b = a * b_s + b
        a = a * a_s
        d *= 2
    out, carry = [], g_in
    for g in reversed(range(t // SUBLANES)):
        sl = slice(g * SUBLANES, (g + 1) * SUBLANES)
        gg = b[sl] + a[sl] * carry
        out.append(gg)
        carry = gg[0:1]
    return jnp.concatenate(out[::-1], axis=0)


def _head_of_lane(shape):
    return lax.broadcasted_iota(jnp.int32, shape, len(shape) - 1) // HEAD_DIM


def _per_head_lanes(cols):
    t = cols[0].shape[0]
    lane = lax.broadcasted_iota(jnp.int32, (t, LANES), 1)
    out = jnp.zeros((t, LANES), F32)
    for h, col in enumerate(cols):
        out = jnp.where(lane == h, col, out)
    return out


def _put_row(acc_shape, k, row_vec):
    row = lax.broadcasted_iota(jnp.int32, acc_shape, 0)
    return jnp.where(row == k, jnp.broadcast_to(row_vec, acc_shape), 0.0)


def _dot(a, b):
    return jnp.dot(a, b, preferred_element_type=F32)


def _dot_nt(a, b):
    return lax.dot_general(a, b, (((1,), (1,)), ((), ())), preferred_element_type=F32)


def _dot_tn(a, b):
    return lax.dot_general(a, b, (((0,), (0,)), ((), ())), preferred_element_type=F32)


def _deinterleave_store(val, stage, outs):
    t, c = val.shape
    for hh in range(c // LANES):
        stage[hh][...] = val[:, hh * LANES:(hh + 1) * LANES].astype(F32)
    for dil, ref in outs:
        for r in range(dil):
            for hh in range(c // LANES):
                ref[r, :, hh * LANES:(hh + 1) * LANES] = stage[hh][pl.ds(r, t // dil, stride=dil), :].astype(ref.dtype)


def _interleave_load(ref, dil, stage):
    _, n, c = ref.shape
    for r in range(dil):
        for hh in range(c // LANES):
            stage[hh][pl.ds(r, n, stride=dil), :] = ref[r, :, hh * LANES:(hh + 1) * LANES].astype(F32)
    return jnp.concatenate([stage[hh][...] for hh in range(c // LANES)], axis=1)


def _stage_scratch(tile, cols, copies):
    return [pltpu.VMEM((tile, LANES), F32)] * (copies * (cols // LANES))


def _by_residue(s, dil, cols, dtype):
    return jax.ShapeDtypeStruct((dil, s // dil, cols), dtype)


def _residue_block(dil, tile, cols):
    return pl.BlockSpec((dil, tile // dil, cols), lambda i: (0, i, 0))


def in_fwd(x, g, w):
    s = x.shape[0]
    qkv_w = 3 * GROUP_W

    def body(x_ref, g_ref, w_ref, z_ref, h_ref, qkv1_ref, qkv4_ref, qkv16_ref, *stage):
        xv = x_ref[...]
        rs = lax.rsqrt(jnp.mean(xv * xv, axis=-1, keepdims=True) + NORM_EPS)
        h = (xv * rs * g_ref[...]).astype(BF16)
        h_ref[...] = h
        z = _dot_nt(h, w_ref[...])
        z_ref[...] = z
        col = lax.broadcasted_iota(jnp.int32, (1, qkv_w), 1)
        qkv = z[:, C_DQ * GROUP_W:(C_DV + 1) * GROUP_W] * jnp.where(col < GROUP_W, ATTN_SCALE, 1.0)
        qkv1_ref[...] = qkv.astype(BF16)
        _deinterleave_store(qkv, stage, ((PATTERN_DILS[1], qkv4_ref), (PATTERN_DILS[2], qkv16_ref)))

    return pl.pallas_call(
        body, name="in_fwd", grid=(s // TILE_IN,),
        in_specs=[pl.BlockSpec((TILE_IN, D_MODEL), lambda i: (i, 0)),
                  pl.BlockSpec((1, D_MODEL), lambda i: (0, 0)),
                  _resident((D_IN, D_MODEL))],
        out_specs=[pl.BlockSpec((TILE_IN, D_IN), lambda i: (i, 0)),
                   pl.BlockSpec((TILE_IN, D_MODEL), lambda i: (i, 0)),
                   pl.BlockSpec((TILE_IN, qkv_w), lambda i: (i, 0)),
                   _residue_block(PATTERN_DILS[1], TILE_IN, qkv_w),
                   _residue_block(PATTERN_DILS[2], TILE_IN, qkv_w)],
        out_shape=[jax.ShapeDtypeStruct((s, D_IN), F32), jax.ShapeDtypeStruct((s, D_MODEL), BF16),
                   jax.ShapeDtypeStruct((s, qkv_w), BF16),
                   _by_residue(s, PATTERN_DILS[1], qkv_w, BF16), _by_residue(s, PATTERN_DILS[2], qkv_w, BF16)],
        scratch_shapes=_stage_scratch(TILE_IN, qkv_w, 1),
        compiler_params=_params(("parallel",)),
    )(x, g, w)


def out_fwd(y, w, x):
    s = x.shape[0]

    def body(y_ref, w_ref, x_ref, o_ref):
        o_ref[...] = x_ref[...] + _dot(y_ref[...], w_ref[...])

    return pl.pallas_call(
        body, name="out_fwd", grid=(s // TILE_OUT,),
        in_specs=[pl.BlockSpec((TILE_OUT, D_MIX), lambda i: (i, 0)),
                  pl.BlockSpec((D_MIX, D_MODEL), lambda i: (0, 0)),
                  pl.BlockSpec((TILE_OUT, D_MODEL), lambda i: (i, 0))],
        out_specs=pl.BlockSpec((TILE_OUT, D_MODEL), lambda i: (i, 0)),
        out_shape=jax.ShapeDtypeStruct((s, D_MODEL), F32),
        compiler_params=_params(("parallel",)),
    )(y, w, x)


def out_bwd(dx, w, z, o):
    s = dx.shape[0]
    abc = 3 * GROUP_W

    def body(dx_ref, w_ref, dg_ref, o_ref, dy_ref, ddg_ref, do1_ref, do4_ref, do16_ref, dl1_ref, dl4_ref, dl16_ref,
             *stage):
        stage_a, stage_b = stage[:2], stage[2:]
        dy = _dot_nt(dx_ref[...].astype(BF16), w_ref[...])
        dy_ref[...] = dy[:, :abc]
        dyd = dy[:, abc:]
        head = _head_of_lane((TILE_OUT, GROUP_W))
        dg = dg_ref[...]
        o = o_ref[...]
        sg, dsg = _silu_and_grad(dg)
        do = dyd * sg
        ddg_ref[...] = dyd * o * dsg
        prod = do * o
        dl = _per_head_lanes([jnp.sum(jnp.where(head == h, prod, 0.0), axis=-1, keepdims=True)
                              for h in range(N_HEADS)])
        do1_ref[...] = do.astype(BF16)
        dl1_ref[...] = dl
        _deinterleave_store(do, stage_a, ((PATTERN_DILS[1], do4_ref), (PATTERN_DILS[2], do16_ref)))
        _deinterleave_store(dl, stage_b, ((PATTERN_DILS[1], dl4_ref), (PATTERN_DILS[2], dl16_ref)))

    row = pl.BlockSpec((TILE_OUT, GROUP_W), lambda i: (i, 0))
    r4 = _residue_block(PATTERN_DILS[1], TILE_OUT, GROUP_W)
    r16 = _residue_block(PATTERN_DILS[2], TILE_OUT, GROUP_W)
    crow = pl.BlockSpec((TILE_OUT, LANES), lambda i: (i, 0))
    c4 = _residue_block(PATTERN_DILS[1], TILE_OUT, LANES)
    c16 = _residue_block(PATTERN_DILS[2], TILE_OUT, LANES)
    return pl.pallas_call(
        body, name="out_bwd", grid=(s // TILE_OUT,),
        in_specs=[pl.BlockSpec((TILE_OUT, D_MODEL), lambda i: (i, 0)),
                  pl.BlockSpec((D_MIX, D_MODEL), lambda i: (0, 0)),
                  pl.BlockSpec((TILE_OUT, GROUP_W), lambda i: (i, C_DG)), row],
        out_specs=[pl.BlockSpec((TILE_OUT, abc), lambda i: (i, 0)), row, row, r4, r16, crow, c4, c16],
        out_shape=[jax.ShapeDtypeStruct((s, abc), F32), jax.ShapeDtypeStruct((s, GROUP_W), F32),
                   jax.ShapeDtypeStruct((s, GROUP_W), BF16),
                   _by_residue(s, PATTERN_DILS[1], GROUP_W, BF16), _by_residue(s, PATTERN_DILS[2], GROUP_W, BF16),
                   jax.ShapeDtypeStruct((s, LANES), F32),
                   _by_residue(s, PATTERN_DILS[1], LANES, F32), _by_residue(s, PATTERN_DILS[2], LANES, F32)],
        scratch_shapes=_stage_scratch(TILE_OUT, GROUP_W, 1) + _stage_scratch(TILE_OUT, LANES, 1),
        compiler_params=_params(("parallel",)),
    )(dx, w, z, o)


def in_bwd(dz, w, x, g, dx_next):
    s = x.shape[0]

    def body(dz_ref, w_ref, x_ref, g_ref, dxn_ref, dx_ref, dg_ref):
        @pl.when(pl.program_id(0) == 0)
        def _():
            dg_ref[...] = jnp.zeros_like(dg_ref)

        dh = _dot(dz_ref[...], w_ref[...])
        xv = x_ref[...]
        rs = lax.rsqrt(jnp.mean(xv * xv, axis=-1, keepdims=True) + NORM_EPS)
        xh = xv * rs
        dg_ref[...] += _put_row(dg_ref.shape, 0, jnp.sum(dh * xh, axis=0, keepdims=True))
        dn = dh * g_ref[...]
        dx_ref[...] = dxn_ref[...] + rs * (dn - xh * jnp.mean(dn * xh, axis=-1, keepdims=True))

    return pl.pallas_call(
        body, name="in_bwd", grid=(s // TILE_IN_BWD,),
        in_specs=[pl.BlockSpec((TILE_IN_BWD, D_IN), lambda i: (i, 0)),
                  _resident((D_IN, D_MODEL)),
                  pl.BlockSpec((TILE_IN_BWD, D_MODEL), lambda i: (i, 0)),
                  pl.BlockSpec((1, D_MODEL), lambda i: (0, 0)),
                  pl.BlockSpec((TILE_IN_BWD, D_MODEL), lambda i: (i, 0))],
        out_specs=[pl.BlockSpec((TILE_IN_BWD, D_MODEL), lambda i: (i, 0)),
                   pl.BlockSpec((SUBLANES, D_MODEL), lambda i: (0, 0))],
        out_shape=[jax.ShapeDtypeStruct((s, D_MODEL), F32), jax.ShapeDtypeStruct((SUBLANES, D_MODEL), F32)],
        compiler_params=_params(("arbitrary",)),
    )(dz, w, x, g, dx_next)


def grad_w_out(y, dx):
    s = y.shape[0]

    def body(y_ref, dx_ref, o_ref):
        @pl.when(pl.program_id(0) == 0)
        def _():
            o_ref[...] = jnp.zeros_like(o_ref)

        o_ref[...] += _dot_tn(y_ref[...], dx_ref[...].astype(BF16))

    return pl.pallas_call(
        body, name="grad_w_out", grid=(s // TILE_DW_OUT,),
        in_specs=[pl.BlockSpec((TILE_DW_OUT, D_MIX), lambda k: (k, 0)),
                  pl.BlockSpec((TILE_DW_OUT, D_MODEL), lambda k: (k, 0))],
        out_specs=pl.BlockSpec((D_MIX, D_MODEL), lambda k: (0, 0)),
        out_shape=jax.ShapeDtypeStruct((D_MIX, D_MODEL), F32),
        compiler_params=_params(("arbitrary",)),
    )(y, dx)


def grad_w_in(h, dz, chip):
    s = h.shape[0]
    rows = D_IN // N_CHIPS

    def body(chip_ref, h_ref, dz_ref, staged_ref, own_ref, acc):
        k = pl.program_id(0)

        @pl.when(k == 0)
        def _():
            acc[...] = jnp.zeros_like(acc)

        acc[...] += _dot_tn(dz_ref[...], h_ref[...])

        @pl.when(k == s // TILE_DW - 1)
        def _():
            for j in range(N_CHIPS):
                part = acc[j * rows:(j + 1) * rows, :]
                staged_ref[j] = part.astype(BF16)

                @pl.when(chip_ref[0] == j)
                def _():
                    own_ref[...] = part

    return pl.pallas_call(
        body, name="grad_w_in",
        grid_spec=pltpu.PrefetchScalarGridSpec(
            num_scalar_prefetch=1, grid=(s // TILE_DW,),
            in_specs=[pl.BlockSpec((TILE_DW, D_MODEL), lambda k, c: (k, 0)),
                      pl.BlockSpec((TILE_DW, D_IN), lambda k, c: (k, 0))],
            out_specs=[pl.BlockSpec((N_CHIPS, rows, D_MODEL), lambda k, c: (0, 0, 0)),
                       pl.BlockSpec((rows, D_MODEL), lambda k, c: (0, 0))],
            scratch_shapes=[pltpu.VMEM((D_IN, D_MODEL), F32)]),
        out_shape=[jax.ShapeDtypeStruct((N_CHIPS, rows, D_MODEL), BF16), jax.ShapeDtypeStruct((rows, D_MODEL), F32)],
        compiler_params=_params(("arbitrary",)),
    )(chip, h, dz)


def out_fwd_loss(y, w, x, g, tgt):
    s = x.shape[0]

    def body(y_ref, w_ref, x_ref, g_ref, t_ref, l_ref, dx_ref, dg_ref):
        @pl.when(pl.program_id(0) == 0)
        def _():
            l_ref[...] = jnp.zeros_like(l_ref)
            dg_ref[...] = jnp.zeros_like(dg_ref)

        xv = x_ref[...] + _dot(y_ref[...], w_ref[...])
        gv = g_ref[...]
        rs = lax.rsqrt(jnp.mean(xv * xv, axis=-1, keepdims=True) + NORM_EPS)
        xh = xv * rs
        e = xh * gv - t_ref[...]
        part = 0.5 * jnp.sum(jnp.mean(e * e, axis=-1, keepdims=True), axis=0, keepdims=True)
        l_ref[...] += jnp.broadcast_to(part, l_ref.shape)
        dy = e * (1.0 / D_MODEL)
        dg_ref[...] += _put_row(dg_ref.shape, 0, jnp.sum(dy * xh, axis=0, keepdims=True))
        dn = dy * gv
        dx_ref[...] = rs * (dn - xh * jnp.mean(dn * xh, axis=-1, keepdims=True))

    return pl.pallas_call(
        body, name="out_fwd_loss", grid=(s // TILE_OUT,),
        in_specs=[pl.BlockSpec((TILE_OUT, D_MIX), lambda i: (i, 0)),
                  pl.BlockSpec((D_MIX, D_MODEL), lambda i: (0, 0)),
                  pl.BlockSpec((TILE_OUT, D_MODEL), lambda i: (i, 0)),
                  pl.BlockSpec((1, D_MODEL), lambda i: (0, 0)),
                  pl.BlockSpec((TILE_OUT, D_MODEL), lambda i: (i, 0))],
        out_specs=[pl.BlockSpec((SUBLANES, LANES), lambda i: (0, 0)),
                   pl.BlockSpec((TILE_OUT, D_MODEL), lambda i: (i, 0)),
                   pl.BlockSpec((SUBLANES, D_MODEL), lambda i: (0, 0))],
        out_shape=[jax.ShapeDtypeStruct((SUBLANES, LANES), F32), jax.ShapeDtypeStruct((s, D_MODEL), F32),
                   jax.ShapeDtypeStruct((SUBLANES, D_MODEL), F32)],
        compiler_params=_params(("arbitrary",)),
    )(y, w, x, g, tgt)


def _attn_bias(dil):
    qi = np.arange(ATTN_BLOCK)[:, None]
    ki = np.arange(2 * ATTN_BLOCK)[None, :]
    delta = qi + ATTN_BLOCK - ki
    band = (delta >= 0) & (delta <= ATTN_BLOCK)
    out = np.empty((2, N_HEADS, ATTN_BLOCK, 2 * ATTN_BLOCK), np.float32)
    for f in range(2):
        ok = band & ((ki >= ATTN_BLOCK) | (f == 0))
        for h in range(N_HEADS):
            out[f, h] = np.where(ok, -ALIBI_SLOPES[h] * dil * delta, NEG_BIG)
    return jnp.asarray(out.reshape(2, N_HEADS * ATTN_BLOCK, 2 * ATTN_BLOCK))


def _stack_heads(a, head):
    return jnp.concatenate([jnp.where(head == h, a, jnp.zeros_like(a)) for h in range(N_HEADS)], axis=0)


def _unstack_heads(a, head):
    out = a[:ATTN_BLOCK]
    for h in range(1, N_HEADS):
        out = jnp.where(head == h, a[h * ATTN_BLOCK:(h + 1) * ATTN_BLOCK], out)
    return out


def _head_column(a):
    return jnp.concatenate([a[:, h:h + 1] for h in range(N_HEADS)], axis=0)


def _attn_specs(n_blocks):
    rows = ATTN_QB * ATTN_BLOCK
    cur = lambda c, w=GROUP_W: pl.BlockSpec((rows, w), lambda n, c=c: (n, c))
    prev = lambda c: pl.BlockSpec((ATTN_BLOCK, GROUP_W), lambda n, c=c: (jnp.maximum(n * ATTN_QB - 1, 0), c))
    nxt = lambda c, w=GROUP_W: pl.BlockSpec((ATTN_BLOCK, w),
                                            lambda n, c=c: (jnp.minimum(n * ATTN_QB + ATTN_QB, n_blocks - 1), c))
    return cur, prev, nxt


def _keys(kp_ref, k_ref, j):
    prev = kp_ref[...] if j == 0 else k_ref[(j - 1) * ATTN_BLOCK:j * ATTN_BLOCK, :]
    return jnp.concatenate([prev, k_ref[j * ATTN_BLOCK:(j + 1) * ATTN_BLOCK, :]], axis=0)


def attn_fwd(qkv, dil):
    s = qkv.shape[0]
    n_blocks = s // ATTN_BLOCK
    bps = n_blocks // dil
    rows = ATTN_QB * ATTN_BLOCK

    def body(q_ref, kp_ref, k_ref, vp_ref, v_ref, bias_ref, o_ref, lse_ref):
        n = pl.program_id(0)
        head = _head_of_lane((ATTN_BLOCK, GROUP_W))
        for j in range(ATTN_QB):
            sl = slice(j * ATTN_BLOCK, (j + 1) * ATTN_BLOCK)
            first = (((n * ATTN_QB + j) % bps) == 0).astype(jnp.int32)
            qs = _stack_heads(q_ref[sl, :], head)
            sc = _dot_nt(qs, _keys(kp_ref, k_ref, j)) + bias_ref[first]
            m = jnp.max(sc, axis=-1, keepdims=True)
            pr = jnp.exp(sc - m)
            l = jnp.sum(pr, axis=-1, keepdims=True)
            oh = _dot(pr.astype(BF16), _keys(vp_ref, v_ref, j)) / l
            o_ref[sl, :] = _unstack_heads(oh, head).astype(BF16)
            ml = m + jnp.log(l)
            lse_ref[sl, :] = _per_head_lanes([ml[h * ATTN_BLOCK:(h + 1) * ATTN_BLOCK] for h in range(N_HEADS)])

    cur, prev, _ = _attn_specs(n_blocks)
    bias = _attn_bias(dil)
    return pl.pallas_call(
        body, name=f"attn_fwd_d{dil}", grid=(n_blocks // ATTN_QB,),
        in_specs=[cur(0), prev(1), cur(1), prev(2), cur(2), pl.BlockSpec(bias.shape, lambda n: (0, 0, 0))],
        out_specs=[cur(0), cur(0, LANES)],
        out_shape=[jax.ShapeDtypeStruct((s, GROUP_W), BF16), jax.ShapeDtypeStruct((s, LANES), F32)],
        compiler_params=_params(("parallel",)),
    )(qkv, qkv, qkv, qkv, qkv, bias)


def attn_bwd(qkv, do, lse, dlt, dil):
    s = qkv.shape[0]
    n_blocks = s // ATTN_BLOCK
    bps = n_blocks // dil
    rows = ATTN_QB * ATTN_BLOCK

    def body(q_ref, qn_ref, kp_ref, k_ref, vp_ref, v_ref, do_ref, don_ref, lse_ref, lsen_ref, dl_ref, dln_ref,
             bias_ref, out_ref, dk_acc, dv_acc):
        n = pl.program_id(0)
        head = _head_of_lane((ATTN_BLOCK, GROUP_W))
        dk_acc[...] = jnp.zeros_like(dk_acc)
        dv_acc[...] = jnp.zeros_like(dv_acc)

        def pair(qj, doj, lsej, dlj, kk, vv, bias, keep):
            qs = _stack_heads(qj, head)
            dos = _stack_heads(doj, head)
            sc = _dot_nt(qs, kk) + bias
            if keep is None:
                pr = jnp.exp(sc - _head_column(lsej))
            else:
                pr = jnp.exp(jnp.minimum(sc - _head_column(lsej), 0.0)) * keep
            dp = _dot_nt(dos, vv)
            ds = (pr * (dp - _head_column(dlj))).astype(BF16)
            return ds, _dot_tn(ds, qs), _dot_tn(pr.astype(BF16), dos)

        for j in range(ATTN_QB):
            sl = slice(j * ATTN_BLOCK, (j + 1) * ATTN_BLOCK)
            first = (((n * ATTN_QB + j) % bps) == 0).astype(jnp.int32)
            kk = _keys(kp_ref, k_ref, j)
            ds, dks, dvs = pair(q_ref[sl, :], do_ref[sl, :], lse_ref[sl, :], dl_ref[sl, :],
                                kk, _keys(vp_ref, v_ref, j), bias_ref[first], None)
            out_ref[sl, 0:GROUP_W] = _unstack_heads(_dot(ds, kk), head) * ATTN_SCALE
            acc = slice(j * ATTN_BLOCK, (j + 2) * ATTN_BLOCK)
            dk_acc[acc, :] += dks
            dv_acc[acc, :] += dvs

        nxt = n * ATTN_QB + ATTN_QB
        valid = ((nxt < n_blocks) & ((nxt % bps) != 0)).astype(F32)
        last = slice((ATTN_QB - 1) * ATTN_BLOCK, ATTN_QB * ATTN_BLOCK)
        _, dks, dvs = pair(qn_ref[...], don_ref[...], lsen_ref[...], dln_ref[...], k_ref[last, :], v_ref[last, :],
                           bias_ref[0][:, :ATTN_BLOCK], valid)
        acc = slice(ATTN_QB * ATTN_BLOCK, (ATTN_QB + 1) * ATTN_BLOCK)
        dk_acc[acc, :] += dks
        dv_acc[acc, :] += dvs
        out_ref[:, GROUP_W:2 * GROUP_W] = dk_acc[ATTN_BLOCK:, :]
        out_ref[:, 2 * GROUP_W:3 * GROUP_W] = dv_acc[ATTN_BLOCK:, :]

    cur, prev, nxt = _attn_specs(n_blocks)
    bias = _attn_bias(dil)
    return pl.pallas_call(
        body, name=f"attn_bwd_d{dil}", grid=(n_blocks // ATTN_QB,),
        in_specs=[cur(0), nxt(0), prev(1), cur(1), prev(2), cur(2), cur(0), nxt(0),
                  cur(0, LANES), nxt(0, LANES), cur(0, LANES), nxt(0, LANES),
                  pl.BlockSpec(bias.shape, lambda n: (0, 0, 0))],
        out_specs=pl.BlockSpec((rows, 3 * GROUP_W), lambda n: (n, 0)),
        out_shape=jax.ShapeDtypeStruct((s, 3 * GROUP_W), F32),
        scratch_shapes=[pltpu.VMEM(((ATTN_QB + 1) * ATTN_BLOCK, GROUP_W), F32),
                        pltpu.VMEM(((ATTN_QB + 1) * ATTN_BLOCK, GROUP_W), F32)],
        compiler_params=_params(("parallel",)),
    )(qkv, qkv, qkv, qkv, qkv, qkv, do, do, lse, lse, dlt, dlt, bias)


def _zcol(c):
    return pl.BlockSpec((TILE_MIX, GROUP_W), lambda i, c=c: (i, c))


def _zhalo(c):
    per = TILE_MIX // SUBLANES
    return pl.BlockSpec((SUBLANES, GROUP_W), lambda i, c=c: (jnp.maximum(i * per - 1, 0), c))


def _full(shape):
    return pl.BlockSpec(shape, lambda i: tuple(0 for _ in shape))


def _of_layer(a, l):
    rest = a.shape[1:]
    return pl.BlockSpec((None,) + rest, lambda i: (l,) + tuple(0 for _ in rest))


def _softplus_neg(lam):
    nl = -lam
    return jnp.maximum(nl, 0.0) + jnp.log1p(jnp.exp(-jnp.abs(nl)))


def _lru_gates(xb, wa_ref, wx_ref, ba, bx, lam):
    xbb = xb.astype(BF16)
    r = _sigmoid(_dot(xbb, wa_ref[...]) + ba)
    ig = _sigmoid(_dot(xbb, wx_ref[...]) + bx)
    log_a = (-RG_C * r) * _softplus_neg(lam)
    a = jnp.exp(log_a)
    mult = jnp.sqrt(_neg_expm1_2x(log_a))
    return r, ig, a, mult


LRU_SAVED = 5


def _gmlp_spatial(ws_ref, vvb, head):
    outs = []
    for j in range(vvb.shape[0] // GMLP_CHUNK):
        blk = vvb[j * GMLP_CHUNK:(j + 1) * GMLP_CHUNK, :]
        acc = jnp.zeros((GMLP_CHUNK, GROUP_W), F32)
        for h in range(N_HEADS):
            acc = jnp.where(head[:GMLP_CHUNK] == h, _dot(ws_ref[h], blk), acc)
        outs.append(acc)
    return jnp.concatenate(outs, axis=0)


def mix_fwd(z, attn, wts, l):
    s = z.shape[0]
    d4, d16 = PATTERN_DILS[1], PATTERN_DILS[2]

    def body(ax_ref, ab_ref, ac_ref, ag_ref, rx_ref, rg_ref, cu_ref, cv_ref, cg_ref, dg_ref,
             axh_ref, ach_ref, rxh_ref, o1_ref, l1_ref, o4_ref, l4_ref, o16_ref, l16_ref,
             caw_ref, crw_ref, crb_ref, wa_ref, wx_ref, ba_ref, bx_ref, lam_ref, gng_ref, ws_ref, bs_ref,
             y_ref, hl_ref, o_ref, lse_ref, lse4_ref, lse16_ref, lru_ref, carry, *stage):
        st_a, st_b, st_c, st_d, st_e = (stage[2 * k:2 * k + 2] for k in range(5))
        i = pl.program_id(0)

        @pl.when(i == 0)
        def _():
            carry[...] = jnp.zeros_like(carry)

        nz = (i > 0).astype(F32)
        head = _head_of_lane((TILE_MIX, GROUP_W))

        pa = ac_ref[...] * ax_ref[...]
        pah = ach_ref[...] * axh_ref[...] * nz
        cv = caw_ref[2:3, :] * pa + caw_ref[1:2, :] * _shift_down(pa, pah, 1) + caw_ref[0:1, :] * _shift_down(pa, pah, 2)
        y_ref[:, 0:GROUP_W] = (ab_ref[...] * cv * _silu(ag_ref[...])).astype(BF16)

        rx = rx_ref[...]
        rxh = rxh_ref[...] * nz
        xb = (crw_ref[3:4, :] * rx + crw_ref[2:3, :] * _shift_down(rx, rxh, 1) + crw_ref[1:2, :] * _shift_down(rx, rxh, 2)
              + crw_ref[0:1, :] * _shift_down(rx, rxh, 3) + crb_ref[...])
        r, ig, a, mult = _lru_gates(xb, wa_ref, wx_ref, ba_ref[...], bx_ref[...], lam_ref[...])
        for k, val in enumerate((xb, r, ig, a, mult)):
            lru_ref[:, k * GROUP_W:(k + 1) * GROUP_W] = val
        hl = _scan_causal(a, mult * (ig * xb), carry[SUBLANES - 1:SUBLANES, :])
        hl_ref[...] = hl
        carry[...] = hl[TILE_MIX - SUBLANES:, :]
        y_ref[:, GROUP_W:2 * GROUP_W] = (hl * _silu(rg_ref[...])).astype(BF16)

        u = _gelu(cu_ref[...])
        gv = _gelu(cv_ref[...])
        rs = lax.rsqrt(jnp.mean(gv * gv, axis=-1, keepdims=True) + NORM_EPS)
        vvb = (gv * rs * gng_ref[...]).astype(BF16)
        sp = _gmlp_spatial(ws_ref, vvb, head) + jnp.concatenate([bs_ref[...]] * (TILE_MIX // GMLP_CHUNK), axis=0)
        y_ref[:, 2 * GROUP_W:3 * GROUP_W] = (u * sp * _silu(cg_ref[...])).astype(BF16)

        ops = (o1_ref[...].astype(F32), _interleave_load(o4_ref, d4, st_a), _interleave_load(o16_ref, d16, st_b))
        lps = (l1_ref[...], _interleave_load(l4_ref, d4, st_c), _interleave_load(l16_ref, d16, st_d))
        m = jnp.maximum(jnp.maximum(lps[0], lps[1]), lps[2])
        ws = [jnp.exp(lp - m) for lp in lps]
        zsum = ws[0] + ws[1] + ws[2]
        lse = m + jnp.log(zsum)
        o = jnp.zeros((TILE_MIX, GROUP_W), F32)
        for op, w in zip(ops, ws):
            wn = w / zsum
            wide = jnp.zeros((TILE_MIX, GROUP_W), F32)
            for h in range(N_HEADS):
                wide = jnp.where(head == h, wn[:, h:h + 1], wide)
            o = o + wide * op
        o_ref[...] = o
        lse_ref[...] = lse
        _deinterleave_store(lse, st_e, ((d4, lse4_ref), (d16, lse16_ref)))
        y_ref[:, 3 * GROUP_W:4 * GROUP_W] = (o * _silu(dg_ref[...])).astype(BF16)

    row = pl.BlockSpec((TILE_MIX, GROUP_W), lambda i: (i, 0))
    r4 = _residue_block(d4, TILE_MIX, GROUP_W)
    r16 = _residue_block(d16, TILE_MIX, GROUP_W)
    crow = pl.BlockSpec((TILE_MIX, LANES), lambda i: (i, 0))
    c4 = _residue_block(d4, TILE_MIX, LANES)
    c16 = _residue_block(d16, TILE_MIX, LANES)
    names = ("caw", "crw", "crb", "wa", "wx", "ba", "bx", "lam", "gng", "ws", "bs")
    in_specs = ([_zcol(c) for c in (C_AX, C_AB, C_AC, C_AG, C_RX, C_RG, C_CU, C_CV, C_CG, C_DG)]
                + [_zhalo(C_AX), _zhalo(C_AC), _zhalo(C_RX), row, crow, r4, c4, r16, c16]
                + [_of_layer(wts[k], l) for k in names])
    return pl.pallas_call(
        body, name="mix_fwd", grid=(s // TILE_MIX,),
        in_specs=in_specs,
        out_specs=[pl.BlockSpec((TILE_MIX, D_MIX), lambda i: (i, 0)), row, row, crow, c4, c16,
                   pl.BlockSpec((TILE_MIX, LRU_SAVED * GROUP_W), lambda i: (i, 0))],
        out_shape=([jax.ShapeDtypeStruct((s, D_MIX), BF16)] + [jax.ShapeDtypeStruct((s, GROUP_W), F32)] * 2
                   + [jax.ShapeDtypeStruct((s, LANES), F32), _by_residue(s, d4, LANES, F32),
                      _by_residue(s, d16, LANES, F32), jax.ShapeDtypeStruct((s, LRU_SAVED * GROUP_W), F32)]),
        scratch_shapes=([pltpu.VMEM((SUBLANES, GROUP_W), F32)] + _stage_scratch(TILE_MIX, GROUP_W, 4)
                        + _stage_scratch(TILE_MIX, LANES, 1)),
        compiler_params=_params(("arbitrary",)),
    )(*([z] * 13), *[a for pair in attn for a in pair], *[wts[k] for k in names])


def mix_bwd(dy, z, hl, lru, dqkv, ddg, wts, l):
    s = z.shape[0]
    d4, d16 = PATTERN_DILS[1], PATTERN_DILS[2]
    n_tiles = s // TILE_MIX

    def body(dya_ref, dyb_ref, dyc_ref, ax_ref, ab_ref, ac_ref, ag_ref, rx_ref, rg_ref, cu_ref, cv_ref, cg_ref,
             axh_ref, ach_ref, rxh_ref, hl_ref, hlh_ref, lru_ref, dqkv1_ref, dqkv4_ref, dqkv16_ref, ddg_ref,
             caw_ref, crw_ref, crb_ref, wa_ref, wx_ref, ba_ref, bx_ref, lam_ref, gng_ref, ws_ref, wst_ref, bs_ref,
             dz_ref, ga_ref, gr_ref, gn_ref, gwa_ref, gwx_ref, gws_ref, gbs_ref,
             c_dcv, c_g, c_a, c_dxb, *stage):
        st_a, st_b = stage[:len(stage) // 2], stage[len(stage) // 2:]
        step = pl.program_id(0)
        i = n_tiles - 1 - step

        @pl.when(step == 0)
        def _():
            for r in (c_dcv, c_g, c_a, c_dxb, ga_ref, gr_ref, gn_ref, gwa_ref, gwx_ref, gws_ref, gbs_ref):
                r[...] = jnp.zeros_like(r)

        nz = (i > 0).astype(F32)
        head = _head_of_lane((TILE_MIX, GROUP_W))
        shp8 = (SUBLANES, GROUP_W)
        colsum = lambda v: jnp.sum(v, axis=0, keepdims=True)

        ax, ab, ac, ag = ax_ref[...], ab_ref[...], ac_ref[...], ag_ref[...]
        dya = dya_ref[...]
        pa = ac * ax
        pah = ach_ref[...] * axh_ref[...] * nz
        pa1 = _shift_down(pa, pah, 1)
        pa2 = _shift_down(pa, pah, 2)
        cv = caw_ref[2:3, :] * pa + caw_ref[1:2, :] * pa1 + caw_ref[0:1, :] * pa2
        sg, dsg = _silu_and_grad(ag)
        dz_ref[:, C_AB * GROUP_W:(C_AB + 1) * GROUP_W] = (dya * cv * sg).astype(BF16)
        dz_ref[:, C_AG * GROUP_W:(C_AG + 1) * GROUP_W] = (dya * ab * cv * dsg).astype(BF16)
        dcv = dya * ab * sg
        nxt = c_dcv[...]
        dpa = caw_ref[2:3, :] * dcv + caw_ref[1:2, :] * _shift_up(dcv, nxt, 1) + caw_ref[0:1, :] * _shift_up(dcv, nxt, 2)
        c_dcv[...] = dcv[:SUBLANES, :]
        dz_ref[:, C_AC * GROUP_W:(C_AC + 1) * GROUP_W] = (dpa * ax).astype(BF16)
        dz_ref[:, C_AX * GROUP_W:(C_AX + 1) * GROUP_W] = (dpa * ac).astype(BF16)
        ga_ref[...] += (_put_row(shp8, 2, colsum(dcv * pa)) + _put_row(shp8, 1, colsum(dcv * pa1))
                        + _put_row(shp8, 0, colsum(dcv * pa2)))

        rx, rg = rx_ref[...], rg_ref[...]
        dyb = dyb_ref[...]
        rxh = rxh_ref[...] * nz
        rx1, rx2, rx3 = _shift_down(rx, rxh, 1), _shift_down(rx, rxh, 2), _shift_down(rx, rxh, 3)
        xb, r, ig, a, mult = (lru_ref[:, k * GROUP_W:(k + 1) * GROUP_W] for k in range(LRU_SAVED))
        lam = lam_ref[...]
        sp = _softplus_neg(lam)
        hl = hl_ref[...]
        hprev = _shift_down(hl, hlh_ref[...] * nz, 1)
        sgr, dsgr = _silu_and_grad(rg)
        dz_ref[:, C_RG * GROUP_W:(C_RG + 1) * GROUP_W] = (dyb * hl * dsgr).astype(BF16)
        dh = dyb * sgr
        a_next = _shift_up(a, c_a[...], 1)
        g = _scan_anticausal(a_next, dh, c_g[0:1, :])
        c_g[...] = g[:SUBLANES, :]
        c_a[...] = a[:SUBLANES, :]
        u = ig * xb
        da = g * hprev
        dmult = g * u
        du = g * mult
        dlog_a = da * a - dmult * (a * a) / mult
        dr = dlog_a * (-RG_C * sp)
        dga = dr * r * (1.0 - r)
        dgx = (du * xb) * ig * (1.0 - ig)
        dgab, dgxb = dga.astype(BF16), dgx.astype(BF16)
        dxb = du * ig + _dot_nt(dgab, wa_ref[...]) + _dot_nt(dgxb, wx_ref[...])
        xbb = xb.astype(BF16)
        gwa_ref[...] += _dot_tn(xbb, dgab)
        gwx_ref[...] += _dot_tn(xbb, dgxb)
        nxt = c_dxb[...]
        drx = (crw_ref[3:4, :] * dxb + crw_ref[2:3, :] * _shift_up(dxb, nxt, 1) + crw_ref[1:2, :] * _shift_up(dxb, nxt, 2)
               + crw_ref[0:1, :] * _shift_up(dxb, nxt, 3))
        c_dxb[...] = dxb[:SUBLANES, :]
        dz_ref[:, C_RX * GROUP_W:(C_RX + 1) * GROUP_W] = drx.astype(BF16)
        dlam = colsum(dlog_a * (-RG_C * r)) * (-_sigmoid(-lam))
        gr_ref[...] += (_put_row(shp8, 3, colsum(dxb * rx)) + _put_row(shp8, 2, colsum(dxb * rx1))
                        + _put_row(shp8, 1, colsum(dxb * rx2)) + _put_row(shp8, 0, colsum(dxb * rx3))
                        + _put_row(shp8, 4, colsum(dxb)) + _put_row(shp8, 5, colsum(dga))
                        + _put_row(shp8, 6, colsum(dgx)) + _put_row(shp8, 7, dlam))

        cu, cvv, cg = cu_ref[...], cv_ref[...], cg_ref[...]
        dyc = dyc_ref[...]
        u_c, du_c = _gelu_and_grad(cu)
        gv, dgv_c = _gelu_and_grad(cvv)
        rs = lax.rsqrt(jnp.mean(gv * gv, axis=-1, keepdims=True) + NORM_EPS)
        vh = gv * rs
        gng = gng_ref[...]
        vvb = (vh * gng).astype(BF16)
        spat = _gmlp_spatial(ws_ref, vvb, head) + jnp.concatenate([bs_ref[...]] * (TILE_MIX // GMLP_CHUNK), axis=0)
        sgc, dsgc = _silu_and_grad(cg)
        dz_ref[:, C_CU * GROUP_W:(C_CU + 1) * GROUP_W] = (dyc * spat * sgc * du_c).astype(BF16)
        dz_ref[:, C_CG * GROUP_W:(C_CG + 1) * GROUP_W] = (dyc * u_c * spat * dsgc).astype(BF16)
        dsp = dyc * u_c * sgc
        dspb = dsp.astype(BF16)
        tril = (lax.broadcasted_iota(jnp.int32, (GMLP_CHUNK, GMLP_CHUNK), 0)
                >= lax.broadcasted_iota(jnp.int32, (GMLP_CHUNK, GMLP_CHUNK), 1))
        head_c = head[:GMLP_CHUNK]
        dvv_parts = []
        gbs = jnp.zeros((GMLP_CHUNK, GROUP_W), F32)
        for j in range(TILE_MIX // GMLP_CHUNK):
            sl = slice(j * GMLP_CHUNK, (j + 1) * GMLP_CHUNK)
            dblk = dspb[sl, :]
            vblk = vvb[sl, :]
            gbs = gbs + dsp[sl, :]
            acc = jnp.zeros((GMLP_CHUNK, GROUP_W), F32)
            for h in range(N_HEADS):
                acc = jnp.where(head_c == h, _dot(wst_ref[h], dblk), acc)
                dm = jnp.where(head_c == h, dblk, jnp.zeros_like(dblk))
                gws_ref[h] += jnp.where(tril, _dot_nt(dm, vblk), 0.0)
            dvv_parts.append(acc)
        gbs_ref[...] += gbs
        dvv = jnp.concatenate(dvv_parts, axis=0)
        gn_ref[...] += _put_row(shp8, 0, colsum(dvv * vh))
        dvh = dvv * gng
        dgv = rs * (dvh - vh * jnp.mean(dvh * vh, axis=-1, keepdims=True))
        dz_ref[:, C_CV * GROUP_W:(C_CV + 1) * GROUP_W] = (dgv * dgv_c).astype(BF16)

        dsum = dqkv1_ref[...] + _interleave_load(dqkv4_ref, d4, st_a) + _interleave_load(dqkv16_ref, d16, st_b)
        dz_ref[:, C_DQ * GROUP_W:(C_DV + 1) * GROUP_W] = dsum.astype(BF16)
        dz_ref[:, C_DG * GROUP_W:(C_DG + 1) * GROUP_W] = ddg_ref[...].astype(BF16)

    per = TILE_MIX // SUBLANES
    qkv_w = 3 * GROUP_W
    rev = lambda c: pl.BlockSpec((TILE_MIX, GROUP_W), lambda t, c=c: (n_tiles - 1 - t, c))
    revh = lambda c: pl.BlockSpec((SUBLANES, GROUP_W),
                                  lambda t, c=c: (jnp.maximum((n_tiles - 1 - t) * per - 1, 0), c))
    revr = lambda dil: pl.BlockSpec((dil, TILE_MIX // dil, qkv_w), lambda t: (0, n_tiles - 1 - t, 0))
    names = ("caw", "crw", "crb", "wa", "wx", "ba", "bx", "lam", "gng", "ws", "wst", "bs")
    in_specs = ([rev(0), rev(1), rev(2)]
                + [rev(c) for c in (C_AX, C_AB, C_AC, C_AG, C_RX, C_RG, C_CU, C_CV, C_CG)]
                + [revh(C_AX), revh(C_AC), revh(C_RX), rev(0), revh(0),
                   pl.BlockSpec((TILE_MIX, LRU_SAVED * GROUP_W), lambda t: (n_tiles - 1 - t, 0)),
                   pl.BlockSpec((TILE_MIX, qkv_w), lambda t: (n_tiles - 1 - t, 0)), revr(d4), revr(d16), rev(0)]
                + [_of_layer(wts[k], l) for k in names])
    small = jax.ShapeDtypeStruct((SUBLANES, GROUP_W), F32)
    sq = jax.ShapeDtypeStruct((GROUP_W, GROUP_W), F32)
    out_shape = [jax.ShapeDtypeStruct((s, D_IN), BF16), small, small, small, sq, sq,
                 jax.ShapeDtypeStruct((N_HEADS, GMLP_CHUNK, GMLP_CHUNK), F32),
                 jax.ShapeDtypeStruct((GMLP_CHUNK, GROUP_W), F32)]
    out_specs = ([pl.BlockSpec((TILE_MIX, D_IN), lambda t: (n_tiles - 1 - t, 0))]
                 + [_full(o.shape) for o in out_shape[1:]])
    return pl.pallas_call(
        body, name="mix_bwd", grid=(n_tiles,),
        in_specs=in_specs, out_specs=out_specs, out_shape=out_shape,
        scratch_shapes=[pltpu.VMEM((SUBLANES, GROUP_W), F32)] * 4 + _stage_scratch(TILE_MIX, qkv_w, 2),
        compiler_params=_params(("arbitrary",)),
    )(dy, dy, dy, *([z] * 12), hl, hl, lru, *dqkv, ddg, *[wts[k] for k in names])


def prepare_small_weights(p):
    tril = jnp.tril(jnp.ones((GMLP_CHUNK, GMLP_CHUNK), dtype=bool))
    ws = jnp.where(tril, p["gmlp_ws"], 0.0).astype(BF16)
    row = lambda a: a[:, None, :]
    eye = jnp.eye(N_HEADS, dtype=F32)
    bd = lambda w: (w[:, :, :, None, :] * eye[None, :, None, :, None]).reshape(-1, GROUP_W, GROUP_W).astype(BF16)
    return dict(
        caw=p["conv_a_w"], crw=p["conv_r_w"], crb=row(p["conv_r_b"]),
        wa=bd(p["lru_wa"]), wx=bd(p["lru_wx"]),
        ba=row(p["lru_ba"]), bx=row(p["lru_bx"]), lam=row(p["lru_lambda"]), gng=row(p["gmlp_norm_g"]),
        ws=ws, wst=jnp.swapaxes(ws, 2, 3),
        bs=jnp.repeat(jnp.swapaxes(p["gmlp_bs"], 1, 2), HEAD_DIM, axis=2))


def _flat(a):
    return a.reshape(a.shape[0] * a.shape[1], a.shape[2])


def _split(a, dil):
    return a.reshape(dil, a.shape[0] // dil, a.shape[1])


def local_step(x, tgt, final_g, depth, chip, layer_weights, projections_done):
    saved = []
    for l in range(depth):
        gain, w_in_l, rest = layer_weights(l, x)
        z, h, *qkvs = in_fwd(x, gain, w_in_l)
        w_out_l, wts = rest(z)
        qkvs = [_flat(q) if q.ndim == 3 else q for q in qkvs]
        attn = []
        for q, d in zip(qkvs, PATTERN_DILS):
            o_p, lse_p = attn_fwd(q, d)
            attn.append((o_p, lse_p) if d == 1 else (_split(o_p, d), _split(lse_p, d)))
        y, hl, o, lse, lse4, lse16, lru = mix_fwd(z, attn, wts, l)
        saved.append(dict(x=x, z=z, h=h, y=y, hl=hl, o=o, qkvs=qkvs, lses=(lse, _flat(lse4), _flat(lse16)), wts=wts, lru=lru,
                          gain=gain, w_in=w_in_l, w_out=w_out_l))
        if l < depth - 1:
            x = out_fwd(y, w_out_l, x)
        else:
            loss, dx, dfg = out_fwd_loss(y, w_out_l, x, final_g[None, :], tgt)
    raw = {k: [None] * depth for k in ("gain", "a", "r", "n", "wa", "wx", "ws", "bs")}
    zero = None
    for l in reversed(range(depth)):
        sv = saved[l]
        dy, ddg, do1, do4, do16, dl1, dl4, dl16 = out_bwd(dx, sv["w_out"], sv["z"], sv["o"])
        g_w_out = grad_w_out(sv["y"], dx)
        dqkv = []
        for q, do, lse, dl, d in zip(sv["qkvs"], (do1, _flat(do4), _flat(do16)), sv["lses"],
                                     (dl1, _flat(dl4), _flat(dl16)), PATTERN_DILS):
            g = attn_bwd(q, do, lse, dl, d)
            dqkv.append(g if d == 1 else _split(g, d))
        dz, ga, gr, gn, gwa, gwx, gws, gbs = mix_bwd(dy, sv["z"], sv["hl"], sv["lru"], dqkv, ddg, sv["wts"], l)
        gain = sv["gain"] if zero is None else sv["gain"] + zero
        zero = projections_done(l, *grad_w_in(sv["h"], dz, chip), g_w_out)
        if l == 0 and zero is not None:
            gain = gain + zero
        dx, dgn = in_bwd(dz, sv["w_in"], sv["x"], gain, dx)
        for k, g in zip(("gain", "a", "r", "n", "wa", "wx", "ws", "bs"), (dgn, ga, gr, gn, gwa, gwx, gws, gbs)):
            raw[k][l] = g
    st = {k: jnp.stack(v) for k, v in raw.items()}
    eye = jnp.eye(N_HEADS, dtype=F32)[None, :, None, :, None]
    diag = lambda g: (g.reshape(depth, N_HEADS, HEAD_DIM, N_HEADS, HEAD_DIM) * eye).sum(axis=3)
    grads = dict(
        norm_g=st["gain"][:, 0], conv_a_w=st["a"][:, :3], conv_r_w=st["r"][:, :4], conv_r_b=st["r"][:, 4],
        lru_ba=st["r"][:, 5], lru_bx=st["r"][:, 6], lru_lambda=st["r"][:, 7], gmlp_norm_g=st["n"][:, 0],
        lru_wa=diag(st["wa"]), lru_wx=diag(st["wx"]), gmlp_ws=st["ws"],
        gmlp_bs=jnp.swapaxes(st["bs"].reshape(depth, GMLP_CHUNK, N_HEADS, HEAD_DIM).sum(-1), 1, 2),
        final_g=dfg[0])
    return loss, dx, grads


MESH = pl.DeviceIdType.MESH
N_CHIPS = 4
N_DEV = 8
ANY = pl.BlockSpec(memory_space=pl.ANY)


def _place():
    x, y, c = lax.axis_index("x"), lax.axis_index("y"), lax.axis_index("c")
    chips = [(1 - x, y), (x, 1 - y), (1 - x, 1 - y)]
    return x, y, c, chips


def _remote(src, dst, ssem, rsem, to):
    return pltpu.make_async_remote_copy(src_ref=src, dst_ref=dst, send_sem=ssem, recv_sem=rsem,
                                        device_id=to, device_id_type=MESH)


HBM = pl.BlockSpec(memory_space=pltpu.HBM)
SEM = pl.BlockSpec(memory_space=pltpu.SEMAPHORE)
DATAFLOW = pltpu.SideEffectType.DATAFLOW_SIDE_EFFECTING
GATHER, SCATTER = "gather", "scatter"


def _chip_copies(mode, src_refs, land_refs, ssem, rsem):
    x, y, c, chips = _place()
    me = 2 * x + y
    n = len(src_refs)
    copies = []
    for k, (cx, cy) in enumerate(chips):
        for a in range(n):
            if mode == GATHER:
                src, dst = src_refs[a], land_refs[a].at[me]
            else:
                src, dst = src_refs[a].at[2 * cx + cy], land_refs[a].at[k]
            copies.append(_remote(src, dst, ssem.at[n * k + a], rsem.at[n * k + a], (cx, cy, c)))
    return copies


def exchange_start(mode, srcs, after, name):
    n = len(srcs)
    if mode == GATHER:
        lands = [lax.empty((N_CHIPS,) + s.shape, s.dtype) for s in srcs]
    else:
        lands = [lax.empty((N_CHIPS - 1,) + s.shape[1:], s.dtype) for s in srcs]
    extra = [] if after is None else [after]

    def body(*refs):
        src_refs, land_refs = refs[:n], refs[n:2 * n]
        ssem, rsem = refs[2 * n + len(extra)], refs[2 * n + len(extra) + 1]
        token = refs[-1]
        for cp in _chip_copies(mode, src_refs, land_refs, ssem, rsem):
            cp.start()
        token[...] = jnp.zeros_like(token)

    arrays = list(srcs) + lands
    return pl.pallas_call(
        body, name=name,
        out_shape=(pltpu.SemaphoreType.DMA((3 * n,)), pltpu.SemaphoreType.DMA((3 * n,)),
                   *[pltpu.HBM(a.shape, a.dtype) for a in arrays], jax.ShapeDtypeStruct((SUBLANES, LANES), F32)),
        in_specs=[HBM] * (2 * n) + [ANY] * len(extra),
        out_specs=(SEM, SEM, *[HBM] * (2 * n), pl.BlockSpec(memory_space=pltpu.VMEM)),
        input_output_aliases={i: 2 + i for i in range(2 * n)},
        compiler_params=pltpu.CompilerParams(has_side_effects=DATAFLOW),
    )(*[pltpu.with_memory_space_constraint(a, pltpu.HBM) for a in arrays], *extra)


def exchange_wait(mode, started, after, name):
    ssem, rsem, *thru, _ = started
    n = len(thru) // 2

    def body(*refs):
        src_refs, land_refs = refs[:n], refs[n:2 * n]
        ssem_ref, rsem_ref = refs[2 * n], refs[2 * n + 1]
        for cp in _chip_copies(mode, src_refs, land_refs, ssem_ref, rsem_ref):
            cp.wait_send()
            cp.wait_recv()

    outs = pl.pallas_call(
        body, name=name,
        out_shape=[pltpu.HBM(a.shape, a.dtype) for a in thru],
        in_specs=[HBM] * (2 * n) + [SEM, SEM, ANY],
        out_specs=[HBM] * (2 * n),
        input_output_aliases={i: i for i in range(2 * n)},
        compiler_params=pltpu.CompilerParams(has_side_effects=DATAFLOW),
    )(*thru, ssem, rsem, after)
    return outs[n:]


def sibling_exchange(p1, p2):
    def body(p1_ref, p2_ref, q1_ref, q2_ref, ssem, rsem):
        x, y, c, _ = _place()
        copies = [_remote(p_ref, q_ref, ssem.at[a], rsem.at[a], (x, y, 1 - c))
                  for a, (p_ref, q_ref) in enumerate(((p1_ref, q1_ref), (p2_ref, q2_ref)))]
        for cp in copies:
            cp.start()
        for cp in copies:
            cp.wait()

    return pl.pallas_call(
        body, name="sibling_exchange",
        in_specs=[ANY, ANY], out_specs=[ANY, ANY],
        out_shape=[jax.ShapeDtypeStruct(p.shape, p.dtype) for p in (p1, p2)],
        scratch_shapes=[pltpu.SemaphoreType.DMA((2,)), pltpu.SemaphoreType.DMA((2,))],
    )(p1, p2)


def all_reduce_small(v):
    r, n = v.shape
    piece = r // N_DEV

    def body(x_ref, out_ref, recv, ssem1, rsem1, ssem2, rsem2):
        x, y, c, _ = _place()
        me = 4 * x + 2 * y + c

        def peer(k):
            px = 1 - x if (k >> 2) & 1 else x
            py = 1 - y if (k >> 1) & 1 else y
            pc = 1 - c if k & 1 else c
            return (px, py, pc), 4 * px + 2 * py + pc

        def rows(ref, d):
            return ref.at[pl.ds(d * piece, piece), :]

        scatter = []
        for k in range(1, N_DEV):
            to, idx = peer(k)
            scatter.append(_remote(rows(x_ref, idx), recv.at[k], ssem1.at[k - 1], rsem1.at[k - 1], to))
            scatter[-1].start()
        acc = rows(x_ref, me)[...]
        for k in range(1, N_DEV):
            scatter[k - 1].wait_recv()
            acc = acc + recv[k]
        rows(out_ref, me)[...] = acc

        gather = []
        for k in range(1, N_DEV):
            to, _ = peer(k)
            gather.append(_remote(rows(out_ref, me), rows(out_ref, me), ssem2.at[k - 1], rsem2.at[k - 1], to))
            gather[-1].start()
        for k in range(1, N_DEV):
            to, idx = peer(k)
            _remote(rows(out_ref, idx), rows(out_ref, idx), ssem2.at[k - 1], rsem2.at[k - 1], to).wait_recv()
        for cp in scatter + gather:
            cp.wait_send()

    return pl.pallas_call(
        body, name="all_reduce_small",
        out_shape=jax.ShapeDtypeStruct((r, n), v.dtype),
        in_specs=[pl.BlockSpec(memory_space=pltpu.VMEM)],
        out_specs=pl.BlockSpec(memory_space=pltpu.VMEM),
        scratch_shapes=[pltpu.VMEM((N_DEV, piece, n), v.dtype)] + [pltpu.SemaphoreType.DMA((N_DEV - 1,))] * 4,
        compiler_params=pltpu.CompilerParams(vmem_limit_bytes=VMEM_LIMIT),
    )(v)


TILE_ROWS = 256


def _row_tile(r):
    return max(t for t in range(SUBLANES, TILE_ROWS + 1, SUBLANES) if r % t == 0)


def sum_partials(owns, parts):
    k, r, c = parts[0].shape
    depth = len(owns)
    tile = _row_tile(r)

    def body(buf_ref, o_ref, p_ref, out_ref):
        acc = o_ref[...]
        for i in range(k):
            acc = acc + p_ref[i].astype(F32)
        out_ref[0] = acc

    out = lax.empty((depth, r, c), F32)
    for l in range(depth):
        out = pl.pallas_call(
            functools.partial(body), name="sum_partials", grid=(r // tile,),
            in_specs=[ANY, pl.BlockSpec((tile, c), lambda i: (i, 0)), pl.BlockSpec((k, tile, c), lambda i: (0, i, 0))],
            out_specs=pl.BlockSpec((1, tile, c), lambda i, l=l: (l, i, 0)),
            out_shape=jax.ShapeDtypeStruct((depth, r, c), F32),
            input_output_aliases={0: 0},
            compiler_params=_params(("parallel",)),
        )(out, owns[l], parts[l])
    return out


def _adamw_update(w, g, m, v):
    m2 = ADAM_B1 * m + (1.0 - ADAM_B1) * g
    v2 = ADAM_B2 * v + (1.0 - ADAM_B2) * (g * g)
    m_hat = m2 / (1.0 - ADAM_B1 ** ADAM_STEP)
    v_hat = v2 / (1.0 - ADAM_B2 ** ADAM_STEP)
    return -ADAM_LR * (m_hat / (jnp.sqrt(v_hat) + ADAM_EPS) + ADAM_WD * w), m2, v2


def adamw_small(ws, gs, ms, vs):
    n = len(ws)

    def body(*refs):
        ins, outs = refs[:4 * n], refs[4 * n:]
        for i in range(n):
            d, m2, v2 = _adamw_update(ins[i][...], ins[n + i][...], ins[2 * n + i][...], ins[3 * n + i][...])
            outs[3 * i][...] = d
            outs[3 * i + 1][...] = m2
            outs[3 * i + 2][...] = v2

    outs = pl.pallas_call(
        body, name="adamw_small",
        out_shape=[jax.ShapeDtypeStruct(w.shape, F32) for w in ws for _ in range(3)],
    )(*ws, *gs, *ms, *vs)
    return [tuple(outs[3 * i:3 * i + 3]) for i in range(n)]


def adamw(w, ga, gb, m, v):
    n, r, c = w.shape
    tile = _row_tile(r)

    def body(w_ref, ga_ref, gb_ref, m_ref, v_ref, g_ref, d_ref, m2_ref, v2_ref):
        g = ga_ref[...] + gb_ref[...]
        g_ref[...] = g
        d_ref[...], m2_ref[...], v2_ref[...] = _adamw_update(w_ref[...], g, m_ref[...], v_ref[...])

    spec = pl.BlockSpec((1, tile, c), lambda j, i: (j, i, 0))
    return pl.pallas_call(
        body, name="adamw", grid=(n, r // tile),
        in_specs=[spec] * 5, out_specs=[spec] * 4,
        out_shape=[jax.ShapeDtypeStruct((n, r, c), F32)] * 4,
        compiler_params=_params(("parallel", "parallel")),
    )(w, ga, gb, m, v)


REPLICATED = ("norm_g", "conv_r_b", "lru_wa", "lru_ba", "lru_wx", "lru_bx", "lru_lambda", "gmlp_norm_g",
              "gmlp_ws", "gmlp_bs", "final_g")
CHIP_SHARDED_SMALL = ("conv_a_w", "conv_r_w")
PACK_LANES = LANES


def _pack(arrays):
    flat = jnp.concatenate([a.reshape(-1) for a in arrays])
    pad = (-flat.shape[0]) % (TILE_ROWS * PACK_LANES)
    return jnp.pad(flat, (0, pad)).reshape(-1, PACK_LANES)


def _unpack(packed, shapes):
    flat = packed.reshape(-1)
    out, off = [], 0
    for shp in shapes:
        n = math.prod(shp)
        out.append(flat[off:off + n].reshape(shp))
        off += n
    return out


def kernel(x, norm_g, w_in, conv_a_w, conv_r_w, conv_r_b, lru_wa, lru_ba, lru_wx, lru_bx, lru_lambda, gmlp_norm_g, gmlp_ws, gmlp_bs, w_out, final_g, loss_target, m_norm_g, m_w_in, m_conv_a_w, m_conv_r_w, m_conv_r_b, m_lru_wa, m_lru_ba, m_lru_wx, m_lru_bx, m_lru_lambda, m_gmlp_norm_g, m_gmlp_ws, m_gmlp_bs, m_w_out, m_final_g, v_norm_g, v_w_in, v_conv_a_w, v_conv_r_w, v_conv_r_b, v_lru_wa, v_lru_ba, v_lru_wx, v_lru_bx, v_lru_lambda, v_gmlp_norm_g, v_gmlp_ws, v_gmlp_bs, v_w_out, v_final_g):
    names = ("norm_g", "w_in", "conv_a_w", "conv_r_w", "conv_r_b", "lru_wa", "lru_ba", "lru_wx", "lru_bx",
             "lru_lambda", "gmlp_norm_g", "gmlp_ws", "gmlp_bs", "w_out", "final_g")
    w = dict(zip(names, (norm_g, w_in, conv_a_w, conv_r_w, conv_r_b, lru_wa, lru_ba, lru_wx, lru_bx, lru_lambda,
                         gmlp_norm_g, gmlp_ws, gmlp_bs, w_out, final_g)))
    m = dict(zip(names, (m_norm_g, m_w_in, m_conv_a_w, m_conv_r_w, m_conv_r_b, m_lru_wa, m_lru_ba, m_lru_wx, m_lru_bx,
                         m_lru_lambda, m_gmlp_norm_g, m_gmlp_ws, m_gmlp_bs, m_w_out, m_final_g)))
    v = dict(zip(names, (v_norm_g, v_w_in, v_conv_a_w, v_conv_r_w, v_conv_r_b, v_lru_wa, v_lru_ba, v_lru_wx, v_lru_bx,
                         v_lru_lambda, v_gmlp_norm_g, v_gmlp_ws, v_gmlp_bs, v_w_out, v_final_g)))
    depth = w_in.shape[0]
    out_rows = w_out.shape[1]
    conv_ch = conv_a_w.shape[2]
    chip = 2 * lax.axis_index("x") + lax.axis_index("y")

    taps = conv_a_w.shape[1] + conv_r_w.shape[1]
    w_in_t, m_w_in_t, v_w_in_t = (jnp.swapaxes(a, 1, 2) for a in (w_in, m_w_in, v_w_in))
    w_in_h, w_out_h = w_in_t.astype(BF16), w_out.astype(BF16)
    conv_own = jnp.concatenate([conv_a_w, conv_r_w], axis=1).reshape(depth * taps, conv_ch)
    gathers, token = [], None
    for l in range(depth):
        groups = [[w_in_h[l]], [w_out_h[l], conv_own]] if l == 0 else [[w_in_h[l], w_out_h[l]]]
        gathers.append([])
        for i, srcs in enumerate(groups):
            gathers[l].append(exchange_start(GATHER, srcs, token, f"gather_start_{l}_{i}"))
            token = gathers[l][-1][-1]
    p = dict(w)

    def with_own(land, own):
        return lax.dynamic_update_slice(land, own[None], (chip,) + (0,) * own.ndim)

    def layer_weights(l, x_l):
        lands = list(exchange_wait(GATHER, gathers[l][0], x_l, f"gather_wait_{l}_0"))
        w_in_l = with_own(lands[0], w_in_h[l]).reshape(D_IN, D_MODEL)
        gain = norm_g[l][None, :]
        if l == 0:
            gain = gain + token[0, 0]

        def rest(z_l):
            if l == 0:
                lands.extend(exchange_wait(GATHER, gathers[l][1], z_l, f"gather_wait_{l}_1"))
                conv = with_own(lands[2], conv_own).reshape(N_CHIPS, depth, taps, conv_ch)
                conv = conv.transpose(1, 2, 0, 3).reshape(depth, taps, GROUP_W)
                p["conv_a_w"] = conv[:, :conv_a_w.shape[1]]
                p["conv_r_w"] = conv[:, conv_a_w.shape[1]:]
                p["prepared"] = prepare_small_weights(p)
            return with_own(lands[1], w_out_h[l]).reshape(D_MIX, D_MODEL), p["prepared"]

        return gain, w_in_l, rest

    scatters, owns = [None] * depth, [None] * depth

    def projections_done(l, g_w_in_by_chip, g_w_in_own, g_w_out):
        go = g_w_out.reshape(N_CHIPS, out_rows, D_MODEL)
        owns[l] = (g_w_in_own, lax.dynamic_index_in_dim(go, chip, axis=0, keepdims=False))
        scatters[l] = exchange_start(SCATTER, [g_w_in_by_chip, go.astype(BF16)], None, f"scatter_start_{l}")
        return scatters[l][-1][0, 0]

    loss8, dx, grads = local_step(x[0], loss_target[0], final_g, depth, chip.reshape(1), layer_weights,
                                  projections_done)

    res = {}
    small = REPLICATED + CHIP_SHARDED_SMALL
    packed = _pack([grads[k] for k in small] + [loss8[0, :1]])
    total = all_reduce_small(packed)
    *sums, loss = _unpack(total, [grads[k].shape for k in small] + [()])
    gs = dict(zip(small, sums))
    for k in CHIP_SHARDED_SMALL:
        gs[k] = lax.dynamic_slice_in_dim(gs[k], chip * conv_ch, conv_ch, axis=2)
    as2d = lambda a: a[None] if a.ndim == 1 else a
    outs = adamw_small(*[[as2d(d[k]) for k in small] for d in (w, gs, m, v)])
    for k, (delta, m2, v2) in zip(small, outs):
        res[k] = [t.reshape(w[k].shape) for t in (gs[k], delta, m2, v2)]

    parts = [exchange_wait(SCATTER, scatters[l], total, f"scatter_wait_{l}") for l in range(depth)]
    p1 = sum_partials([owns[l][0] for l in range(depth)], [parts[l][0] for l in range(depth)])
    p2 = sum_partials([owns[l][1] for l in range(depth)], [parts[l][1] for l in range(depth)])
    q1, q2 = sibling_exchange(p1, p2)
    res["w_in"] = [jnp.swapaxes(t, 1, 2) for t in adamw(w_in_t, p1, q1, m_w_in_t, v_w_in_t)]
    res["w_out"] = adamw(w_out, p2, q2, m_w_out, v_w_out)

    return (loss, dx[None], *[res[k][0] for k in names], *[res[k][1] for k in names],
            *[res[k][2] for k in names], *[res[k][3] for k in names])
```

```python
import functools
import math

import jax
import jax.numpy as jnp
import numpy as np
from jax import lax
from jax.experimental import pallas as pl
from jax.experimental.pallas import tpu as pltpu

F32 = jnp.float32
BF16 = jnp.bfloat16

D_MODEL = 1024
GROUP_W = 256
N_HEADS = 4
HEAD_DIM = 64
N_CHUNKS = 13
D_IN = N_CHUNKS * GROUP_W
D_MIX = 4 * GROUP_W
NORM_EPS = 1e-6
RG_C = 8.0
GMLP_CHUNK = 128
ATTN_BLOCK = 128
PATTERN_DILS = (1, 4, 16)
ALIBI_SLOPES = tuple(2.0 ** (-8.0 * (h + 1) / N_HEADS) for h in range(N_HEADS))
ATTN_SCALE = 1.0 / math.sqrt(HEAD_DIM)
NEG_BIG = -1e30

ADAM_LR = 0.001
ADAM_B1 = 0.9
ADAM_B2 = 0.999
ADAM_EPS = 1e-08
ADAM_WD = 0.01
ADAM_STEP = 10

C_AX, C_AB, C_AC, C_AG, C_RX, C_RG, C_CU, C_CV, C_CG, C_DQ, C_DK, C_DV, C_DG = range(13)

SUBLANES = 8
LANES = 128
VMEM_LIMIT = 56 * 1024 * 1024
TILE_IN = 512
TILE_IN_BWD = 1024
TILE_OUT = 1024
TILE_MIX = 512
TILE_DW = 1024
TILE_DW_OUT = 2048
ATTN_QB = 16
GELU_K0 = math.sqrt(2.0 / math.pi)
GELU_K1 = 0.044715


def _resident(shape):
    return pl.BlockSpec(shape, lambda i: tuple(0 for _ in shape), pipeline_mode=pl.Buffered(1))


def _params(sem):
    return pltpu.CompilerParams(dimension_semantics=sem, vmem_limit_bytes=VMEM_LIMIT)


def _sigmoid(x):
    return 0.5 * jnp.tanh(0.5 * x) + 0.5


def _silu(x):
    return x * _sigmoid(x)


def _silu_and_grad(x):
    s = _sigmoid(x)
    return x * s, s * (1.0 + x * (1.0 - s))


def _gelu(x):
    return x * (0.5 + 0.5 * jnp.tanh(x * (GELU_K0 + (GELU_K0 * GELU_K1) * (x * x))))


def _gelu_and_grad(x):
    x2 = x * x
    t = jnp.tanh(x * (GELU_K0 + (GELU_K0 * GELU_K1) * x2))
    half = 0.5 + 0.5 * t
    return x * half, half + x * (0.5 - 0.5 * t * t) * (GELU_K0 + (3.0 * GELU_K0 * GELU_K1) * x2)


def _neg_expm1_2x(x):
    t = jnp.tanh(x)
    return -2.0 * t / (1.0 - t)


def _shift_down(v, halo, k):
    r = pltpu.roll(v, k, 0)
    rh = pltpu.roll(halo, k, 0)
    row = lax.broadcasted_iota(jnp.int32, halo.shape, 0)
    top = jnp.where(row < k, rh, r[:SUBLANES])
    return jnp.concatenate([top, r[SUBLANES:]], axis=0)


def _shift_up(v, halo, k):
    t = v.shape[0]
    r = pltpu.roll(v, t - k, 0)
    rh = pltpu.roll(halo, SUBLANES - k, 0)
    row = lax.broadcasted_iota(jnp.int32, halo.shape, 0)
    bot = jnp.where(row >= SUBLANES - k, rh, r[t - SUBLANES:])
    return jnp.concatenate([r[:t - SUBLANES], bot], axis=0)


def _scan_causal(a, b, h_in):
    t = a.shape[0]
    row8 = lax.broadcasted_iota(jnp.int32, a.shape, 0) % SUBLANES
    d = 1
    while d < SUBLANES:
        m = row8 >= d
        a_s = jnp.where(m, pltpu.roll(a, d, 0), 1.0)
        b_s = jnp.where(m, pltpu.roll(b, d, 0), 0.0)
        b = a * b_s + b
        a = a * a_s
        d *= 2
    out, carry = [], h_in
    for g in range(t // SUBLANES):
        sl = slice(g * SUBLANES, (g + 1) * SUBLANES)
        hg = b[sl] + a[sl] * carry
        out.append(hg)
        carry = hg[SUBLANES - 1:SUBLANES]
    return jnp.concatenate(out, axis=0)


def _scan_anticausal(a, b, g_in):
    t = a.shape[0]
    row8 = lax.broadcasted_iota(jnp.int32, a.shape, 0) % SUBLANES
    d = 1
    while d < SUBLANES:
        m = row8 < SUBLANES - d
        a_s = jnp.where(m, pltpu.roll(a, t - d, 0), 1.0)
        b_s = jnp.where(m, pltpu.roll(b, t - d, 0), 0.0)
        b = a * b_s + b
        a = a * a_s
        d *= 2
    out, carry = [], g_in
    for g in reversed(range(t // SUBLANES)):
        sl = slice(g * SUBLANES, (g + 1) * SUBLANES)
        gg = b[sl] + a[sl] * carry
        out.append(gg)
        carry = gg[0:1]
    return jnp.concatenate(out[::-1], axis=0)


def _head_of_lane(shape):
    return lax.broadcasted_iota(jnp.int32, shape, len(shape) - 1) // HEAD_DIM


def _per_head_lanes(cols):
    t = cols[0].shape[0]
    lane = lax.broadcasted_iota(jnp.int32, (t, LANES), 1)
    out = jnp.zeros((t, LANES), F32)
    for h, col in enumerate(cols):
        out = jnp.where(lane == h, col, out)
    return out


def _put_row(acc_shape, k, row_vec):
    row = lax.broadcasted_iota(jnp.int32, acc_shape, 0)
    return jnp.where(row == k, jnp.broadcast_to(row_vec, acc_shape), 0.0)


def _dot(a, b):
    return jnp.dot(a, b, preferred_element_type=F32)


def _dot_nt(a, b):
    return lax.dot_general(a, b, (((1,), (1,)), ((), ())), preferred_element_type=F32)


def _dot_tn(a, b):
    return lax.dot_general(a, b, (((0,), (0,)), ((), ())), preferred_element_type=F32)


def _deinterleave_store(val, stage, outs):
    t, c = val.shape
    for hh in range(c // LANES):
        stage[hh][...] = val[:, hh * LANES:(hh + 1) * LANES].astype(F32)
    for dil, ref in outs:
        for r in range(dil):
            for hh in range(c // LANES):
                ref[r, :, hh * LANES:(hh + 1) * LANES] = stage[hh][pl.ds(r, t // dil, stride=dil), :].astype(ref.dtype)


def _interleave_load(ref, dil, stage):
    _, n, c = ref.shape
    for r in range(dil):
        for hh in range(c // LANES):
            stage[hh][pl.ds(r, n, stride=dil), :] = ref[r, :, hh * LANES:(hh + 1) * LANES].astype(F32)
    return jnp.concatenate([stage[hh][...] for hh in range(c // LANES)], axis=1)


def _stage_scratch(tile, cols, copies):
    return [pltpu.VMEM((tile, LANES), F32)] * (copies * (cols // LANES))


def _by_residue(s, dil, cols, dtype):
    return jax.ShapeDtypeStruct((dil, s // dil, cols), dtype)


def _residue_block(dil, tile, cols):
    return pl.BlockSpec((dil, tile // dil, cols), lambda i: (0, i, 0))


def in_fwd(x, g, w):
    s = x.shape[0]
    qkv_w = 3 * GROUP_W

    def body(x_ref, g_ref, w_ref, z_ref, h_ref, qkv1_ref, qkv4_ref, qkv16_ref, *stage):
        xv = x_ref[...]
        rs = lax.rsqrt(jnp.mean(xv * xv, axis=-1, keepdims=True) + NORM_EPS)
        h = (xv * rs * g_ref[...]).astype(BF16)
        h_ref[...] = h
        z = _dot_nt(h, w_ref[...])
        z_ref[...] = z
        col = lax.broadcasted_iota(jnp.int32, (1, qkv_w), 1)
        qkv = z[:, C_DQ * GROUP_W:(C_DV + 1) * GROUP_W] * jnp.where(col < GROUP_W, ATTN_SCALE, 1.0)
        qkv1_ref[...] = qkv.astype(BF16)
        _deinterleave_store(qkv, stage, ((PATTERN_DILS[1], qkv4_ref), (PATTERN_DILS[2], qkv16_ref)))

    return pl.pallas_call(
        body, name="in_fwd", grid=(s // TILE_IN,),
        in_specs=[pl.BlockSpec((TILE_IN, D_MODEL), lambda i: (i, 0)),
                  pl.BlockSpec((1, D_MODEL), lambda i: (0, 0)),
                  _resident((D_IN, D_MODEL))],
        out_specs=[pl.BlockSpec((TILE_IN, D_IN), lambda i: (i, 0)),
                   pl.BlockSpec((TILE_IN, D_MODEL), lambda i: (i, 0)),
                   pl.BlockSpec((TILE_IN, qkv_w), lambda i: (i, 0)),
                   _residue_block(PATTERN_DILS[1], TILE_IN, qkv_w),
                   _residue_block(PATTERN_DILS[2], TILE_IN, qkv_w)],
        out_shape=[jax.ShapeDtypeStruct((s, D_IN), F32), jax.ShapeDtypeStruct((s, D_MODEL), BF16),
                   jax.ShapeDtypeStruct((s, qkv_w), BF16),
                   _by_residue(s, PATTERN_DILS[1], qkv_w, BF16), _by_residue(s, PATTERN_DILS[2], qkv_w, BF16)],
        scratch_shapes=_stage_scratch(TILE_IN, qkv_w, 1),
        compiler_params=_params(("parallel",)),
    )(x, g, w)


def out_fwd(y, w, x):
    s = x.shape[0]

    def body(y_ref, w_ref, x_ref, o_ref):
        o_ref[...] = x_ref[...] + _dot(y_ref[...], w_ref[...])

    return pl.pallas_call(
        body, name="out_fwd", grid=(s // TILE_OUT,),
        in_specs=[pl.BlockSpec((TILE_OUT, D_MIX), lambda i: (i, 0)),
                  pl.BlockSpec((D_MIX, D_MODEL), lambda i: (0, 0)),
                  pl.BlockSpec((TILE_OUT, D_MODEL), lambda i: (i, 0))],
        out_specs=pl.BlockSpec((TILE_OUT, D_MODEL), lambda i: (i, 0)),
        out_shape=jax.ShapeDtypeStruct((s, D_MODEL), F32),
        compiler_params=_params(("parallel",)),
    )(y, w, x)


def out_bwd(dx, w, z, o):
    s = dx.shape[0]
    abc = 3 * GROUP_W

    def body(dx_ref, w_ref, dg_ref, o_ref, dy_ref, ddg_ref, do1_ref, do4_ref, do16_ref, dl1_ref, dl4_ref, dl16_ref,
             *stage):
        stage_a, stage_b = stage[:2], stage[2:]
        dy = _dot_nt(dx_ref[...].astype(BF16), w_ref[...])
        dy_ref[...] = dy[:, :abc]
        dyd = dy[:, abc:]
        head = _head_of_lane((TILE_OUT, GROUP_W))
        dg = dg_ref[...]
        o = o_ref[...]
        sg, dsg = _silu_and_grad(dg)
        do = dyd * sg
        ddg_ref[...] = dyd * o * dsg
        prod = do * o
        dl = _per_head_lanes([jnp.sum(jnp.where(head == h, prod, 0.0), axis=-1, keepdims=True)
                              for h in range(N_HEADS)])
        do1_ref[...] = do.astype(BF16)
        dl1_ref[...] = dl
        _deinterleave_store(do, stage_a, ((PATTERN_DILS[1], do4_ref), (PATTERN_DILS[2], do16_ref)))
        _deinterleave_store(dl, stage_b, ((PATTERN_DILS[1], dl4_ref), (PATTERN_DILS[2], dl16_ref)))

    row = pl.BlockSpec((TILE_OUT, GROUP_W), lambda i: (i, 0))
    r4 = _residue_block(PATTERN_DILS[1], TILE_OUT, GROUP_W)
    r16 = _residue_block(PATTERN_DILS[2], TILE_OUT, GROUP_W)
    crow = pl.BlockSpec((TILE_OUT, LANES), lambda i: (i, 0))
    c4 = _residue_block(PATTERN_DILS[1], TILE_OUT, LANES)
    c16 = _residue_block(PATTERN_DILS[2], TILE_OUT, LANES)
    return pl.pallas_call(
        body, name="out_bwd", grid=(s // TILE_OUT,),
        in_specs=[pl.BlockSpec((TILE_OUT, D_MODEL), lambda i: (i, 0)),
                  pl.BlockSpec((D_MIX, D_MODEL), lambda i: (0, 0)),
                  pl.BlockSpec((TILE_OUT, GROUP_W), lambda i: (i, C_DG)), row],
        out_specs=[pl.BlockSpec((TILE_OUT, abc), lambda i: (i, 0)), row, row, r4, r16, crow, c4, c16],
        out_shape=[jax.ShapeDtypeStruct((s, abc), F32), jax.ShapeDtypeStruct((s, GROUP_W), F32),
                   jax.ShapeDtypeStruct((s, GROUP_W), BF16),
                   _by_residue(s, PATTERN_DILS[1], GROUP_W, BF16), _by_residue(s, PATTERN_DILS[2], GROUP_W, BF16),
                   jax.ShapeDtypeStruct((s, LANES), F32),
                   _by_residue(s, PATTERN_DILS[1], LANES, F32), _by_residue(s, PATTERN_DILS[2], LANES, F32)],
        scratch_shapes=_stage_scratch(TILE_OUT, GROUP_W, 1) + _stage_scratch(TILE_OUT, LANES, 1),
        compiler_params=_params(("parallel",)),
    )(dx, w, z, o)


def in_bwd(dz, w, x, g, dx_next):
    s = x.shape[0]

    def body(dz_ref, w_ref, x_ref, g_ref, dxn_ref, dx_ref, dg_ref):
        @pl.when(pl.program_id(0) == 0)
        def _():
            dg_ref[...] = jnp.zeros_like(dg_ref)

        dh = _dot(dz_ref[...], w_ref[...])
        xv = x_ref[...]
        rs = lax.rsqrt(jnp.mean(xv * xv, axis=-1, keepdims=True) + NORM_EPS)
        xh = xv * rs
        dg_ref[...] += _put_row(dg_ref.shape, 0, jnp.sum(dh * xh, axis=0, keepdims=True))
        dn = dh * g_ref[...]
        dx_ref[...] = dxn_ref[...] + rs * (dn - xh * jnp.mean(dn * xh, axis=-1, keepdims=True))

    return pl.pallas_call(
        body, name="in_bwd", grid=(s // TILE_IN_BWD,),
        in_specs=[pl.BlockSpec((TILE_IN_BWD, D_IN), lambda i: (i, 0)),
                  _resident((D_IN, D_MODEL)),
                  pl.BlockSpec((TILE_IN_BWD, D_MODEL), lambda i: (i, 0)),
                  pl.BlockSpec((1, D_MODEL), lambda i: (0, 0)),
                  pl.BlockSpec((TILE_IN_BWD, D_MODEL), lambda i: (i, 0))],
        out_specs=[pl.BlockSpec((TILE_IN_BWD, D_MODEL), lambda i: (i, 0)),
                   pl.BlockSpec((SUBLANES, D_MODEL), lambda i: (0, 0))],
        out_shape=[jax.ShapeDtypeStruct((s, D_MODEL), F32), jax.ShapeDtypeStruct((SUBLANES, D_MODEL), F32)],
        compiler_params=_params(("arbitrary",)),
    )(dz, w, x, g, dx_next)


def grad_w_out(y, dx):
    s = y.shape[0]

    def body(y_ref, dx_ref, o_ref):
        @pl.when(pl.program_id(0) == 0)
        def _():
            o_ref[...] = jnp.zeros_like(o_ref)

        o_ref[...] += _dot_tn(y_ref[...], dx_ref[...].astype(BF16))

    return pl.pallas_call(
        body, name="grad_w_out", grid=(s // TILE_DW_OUT,),
        in_specs=[pl.BlockSpec((TILE_DW_OUT, D_MIX), lambda k: (k, 0)),
                  pl.BlockSpec((TILE_DW_OUT, D_MODEL), lambda k: (k, 0))],
        out_specs=pl.BlockSpec((D_MIX, D_MODEL), lambda k: (0, 0)),
        out_shape=jax.ShapeDtypeStruct((D_MIX, D_MODEL), F32),
        compiler_params=_params(("arbitrary",)),
    )(y, dx)


def grad_w_in(h, dz, chip):
    s = h.shape[0]
    rows = D_IN // N_CHIPS

    def body(chip_ref, h_ref, dz_ref, staged_ref, own_ref, acc):
        k = pl.program_id(0)

        @pl.when(k == 0)
        def _():
            acc[...] = jnp.zeros_like(acc)

        acc[...] += _dot_tn(dz_ref[...], h_ref[...])

        @pl.when(k == s // TILE_DW - 1)
        def _():
            for j in range(N_CHIPS):
                part = acc[j * rows:(j + 1) * rows, :]
                staged_ref[j] = part.astype(BF16)

                @pl.when(chip_ref[0] == j)
                def _():
                    own_ref[...] = part

    return pl.pallas_call(
        body, name="grad_w_in",
        grid_spec=pltpu.PrefetchScalarGridSpec(
            num_scalar_prefetch=1, grid=(s // TILE_DW,),
            in_specs=[pl.BlockSpec((TILE_DW, D_MODEL), lambda k, c: (k, 0)),
                      pl.BlockSpec((TILE_DW, D_IN), lambda k, c: (k, 0))],
            out_specs=[pl.BlockSpec((N_CHIPS, rows, D_MODEL), lambda k, c: (0, 0, 0)),
                       pl.BlockSpec((rows, D_MODEL), lambda k, c: (0, 0))],
            scratch_shapes=[pltpu.VMEM((D_IN, D_MODEL), F32)]),
        out_shape=[jax.ShapeDtypeStruct((N_CHIPS, rows, D_MODEL), BF16), jax.ShapeDtypeStruct((rows, D_MODEL), F32)],
        compiler_params=_params(("arbitrary",)),
    )(chip, h, dz)


def out_fwd_loss(y, w, x, g, tgt):
    s = x.shape[0]

    def body(y_ref, w_ref, x_ref, g_ref, t_ref, l_ref, dx_ref, dg_ref):
        @pl.when(pl.program_id(0) == 0)
        def _():
            l_ref[...] = jnp.zeros_like(l_ref)
            dg_ref[...] = jnp.zeros_like(dg_ref)

        xv = x_ref[...] + _dot(y_ref[...], w_ref[...])
        gv = g_ref[...]
        rs = lax.rsqrt(jnp.mean(xv * xv, axis=-1, keepdims=True) + NORM_EPS)
        xh = xv * rs
        e = xh * gv - t_ref[...]
        part = 0.5 * jnp.sum(jnp.mean(e * e, axis=-1, keepdims=True), axis=0, keepdims=True)
        l_ref[...] += jnp.broadcast_to(part, l_ref.shape)
        dy = e * (1.0 / D_MODEL)
        dg_ref[...] += _put_row(dg_ref.shape, 0, jnp.sum(dy * xh, axis=0, keepdims=True))
        dn = dy * gv
        dx_ref[...] = rs * (dn - xh * jnp.mean(dn * xh, axis=-1, keepdims=True))

    return pl.pallas_call(
        body, name="out_fwd_loss", grid=(s // TILE_OUT,),
        in_specs=[pl.BlockSpec((TILE_OUT, D_MIX), lambda i: (i, 0)),
                  pl.BlockSpec((D_MIX, D_MODEL), lambda i: (0, 0)),
                  pl.BlockSpec((TILE_OUT, D_MODEL), lambda i: (i, 0)),
                  pl.BlockSpec((1, D_MODEL), lambda i: (0, 0)),
                  pl.BlockSpec((TILE_OUT, D_MODEL), lambda i: (i, 0))],
        out_specs=[pl.BlockSpec((SUBLANES, LANES), lambda i: (0, 0)),
                   pl.BlockSpec((TILE_OUT, D_MODEL), lambda i: (i, 0)),
                   pl.BlockSpec((SUBLANES, D_MODEL), lambda i: (0, 0))],
        out_shape=[jax.ShapeDtypeStruct((SUBLANES, LANES), F32), jax.ShapeDtypeStruct((s, D_MODEL), F32),
                   jax.ShapeDtypeStruct((SUBLANES, D_MODEL), F32)],
        compiler_params=_params(("arbitrary",)),
    )(y, w, x, g, tgt)


def _attn_bias(dil):
    qi = np.arange(ATTN_BLOCK)[:, None]
    ki = np.arange(2 * ATTN_BLOCK)[None, :]
    delta = qi + ATTN_BLOCK - ki
    band = (delta >= 0) & (delta <= ATTN_BLOCK)
    out = np.empty((2, N_HEADS, ATTN_BLOCK, 2 * ATTN_BLOCK), np.float32)
    for f in range(2):
        ok = band & ((ki >= ATTN_BLOCK) | (f == 0))
        for h in range(N_HEADS):
            out[f, h] = np.where(ok, -ALIBI_SLOPES[h] * dil * delta, NEG_BIG)
    return jnp.asarray(out.reshape(2, N_HEADS * ATTN_BLOCK, 2 * ATTN_BLOCK))


def _stack_heads(a, head):
    return jnp.concatenate([jnp.where(head == h, a, jnp.zeros_like(a)) for h in range(N_HEADS)], axis=0)


def _unstack_heads(a, head):
    out = a[:ATTN_BLOCK]
    for h in range(1, N_HEADS):
        out = jnp.where(head == h, a[h * ATTN_BLOCK:(h + 1) * ATTN_BLOCK], out)
    return out


def _head_column(a):
    return jnp.concatenate([a[:, h:h + 1] for h in range(N_HEADS)], axis=0)


def _attn_specs(n_blocks):
    rows = ATTN_QB * ATTN_BLOCK
    cur = lambda c, w=GROUP_W: pl.BlockSpec((rows, w), lambda n, c=c: (n, c))
    prev = lambda c: pl.BlockSpec((ATTN_BLOCK, GROUP_W), lambda n, c=c: (jnp.maximum(n * ATTN_QB - 1, 0), c))
    nxt = lambda c, w=GROUP_W: pl.BlockSpec((ATTN_BLOCK, w),
                                            lambda n, c=c: (jnp.minimum(n * ATTN_QB + ATTN_QB, n_blocks - 1), c))
    return cur, prev, nxt


def _keys(kp_ref, k_ref, j):
    prev = kp_ref[...] if j == 0 else k_ref[(j - 1) * ATTN_BLOCK:j * ATTN_BLOCK, :]
    return jnp.concatenate([prev, k_ref[j * ATTN_BLOCK:(j + 1) * ATTN_BLOCK, :]], axis=0)


def attn_fwd(qkv, dil):
    s = qkv.shape[0]
    n_blocks = s // ATTN_BLOCK
    bps = n_blocks // dil
    rows = ATTN_QB * ATTN_BLOCK

    def body(q_ref, kp_ref, k_ref, vp_ref, v_ref, bias_ref, o_ref, lse_ref):
        n = pl.program_id(0)
        head = _head_of_lane((ATTN_BLOCK, GROUP_W))
        for j in range(ATTN_QB):
            sl = slice(j * ATTN_BLOCK, (j + 1) * ATTN_BLOCK)
            first = (((n * ATTN_QB + j) % bps) == 0).astype(jnp.int32)
            qs = _stack_heads(q_ref[sl, :], head)
            sc = _dot_nt(qs, _keys(kp_ref, k_ref, j)) + bias_ref[first]
            m = jnp.max(sc, axis=-1, keepdims=True)
            pr = jnp.exp(sc - m)
            l = jnp.sum(pr, axis=-1, keepdims=True)
            oh = _dot(pr.astype(BF16), _keys(vp_ref, v_ref, j)) / l
            o_ref[sl, :] = _unstack_heads(oh, head).astype(BF16)
            ml = m + jnp.log(l)
            lse_ref[sl, :] = _per_head_lanes([ml[h * ATTN_BLOCK:(h + 1) * ATTN_BLOCK] for h in range(N_HEADS)])

    cur, prev, _ = _attn_specs(n_blocks)
    bias = _attn_bias(dil)
    return pl.pallas_call(
        body, name=f"attn_fwd_d{dil}", grid=(n_blocks // ATTN_QB,),
        in_specs=[cur(0), prev(1), cur(1), prev(2), cur(2), pl.BlockSpec(bias.shape, lambda n: (0, 0, 0))],
        out_specs=[cur(0), cur(0, LANES)],
        out_shape=[jax.ShapeDtypeStruct((s, GROUP_W), BF16), jax.ShapeDtypeStruct((s, LANES), F32)],
        compiler_params=_params(("parallel",)),
    )(qkv, qkv, qkv, qkv, qkv, bias)


def attn_bwd(qkv, do, lse, dlt, dil):
    s = qkv.shape[0]
    n_blocks = s // ATTN_BLOCK
    bps = n_blocks // dil
    rows = ATTN_QB * ATTN_BLOCK

    def body(q_ref, qn_ref, kp_ref, k_ref, vp_ref, v_ref, do_ref, don_ref, lse_ref, lsen_ref, dl_ref, dln_ref,
             bias_ref, out_ref, dk_acc, dv_acc):
        n = pl.program_id(0)
        head = _head_of_lane((ATTN_BLOCK, GROUP_W))
        dk_acc[...] = jnp.zeros_like(dk_acc)
        dv_acc[...] = jnp.zeros_like(dv_acc)

        def pair(qj, doj, lsej, dlj, kk, vv, bias, keep):
            qs = _stack_heads(qj, head)
            dos = _stack_heads(doj, head)
            sc = _dot_nt(qs, kk) + bias
            if keep is None:
                pr = jnp.exp(sc - _head_column(lsej))
            else:
                pr = jnp.exp(jnp.minimum(sc - _head_column(lsej), 0.0)) * keep
            dp = _dot_nt(dos, vv)
            ds = (pr * (dp - _head_column(dlj))).astype(BF16)
            return ds, _dot_tn(ds, qs), _dot_tn(pr.astype(BF16), dos)

        for j in range(ATTN_QB):
            sl = slice(j * ATTN_BLOCK, (j + 1) * ATTN_BLOCK)
            first = (((n * ATTN_QB + j) % bps) == 0).astype(jnp.int32)
            kk = _keys(kp_ref, k_ref, j)
            ds, dks, dvs = pair(q_ref[sl, :], do_ref[sl, :], lse_ref[sl, :], dl_ref[sl, :],
                                kk, _keys(vp_ref, v_ref, j), bias_ref[first], None)
            out_ref[sl, 0:GROUP_W] = _unstack_heads(_dot(ds, kk), head) * ATTN_SCALE
            acc = slice(j * ATTN_BLOCK, (j + 2) * ATTN_BLOCK)
            dk_acc[acc, :] += dks
            dv_acc[acc, :] += dvs

        nxt = n * ATTN_QB + ATTN_QB
        valid = ((nxt < n_blocks) & ((nxt % bps) != 0)).astype(F32)
        last = slice((ATTN_QB - 1) * ATTN_BLOCK, ATTN_QB * ATTN_BLOCK)
        _, dks, dvs = pair(qn_ref[...], don_ref[...], lsen_ref[...], dln_ref[...], k_ref[last, :], v_ref[last, :],
                           bias_ref[0][:, :ATTN_BLOCK], valid)
        acc = slice(ATTN_QB * ATTN_BLOCK, (ATTN_QB + 1) * ATTN_BLOCK)
        dk_acc[acc, :] += dks
        dv_acc[acc, :] += dvs
        out_ref[:, GROUP_W:2 * GROUP_W] = dk_acc[ATTN_BLOCK:, :]
        out_ref[:, 2 * GROUP_W:3 * GROUP_W] = dv_acc[ATTN_BLOCK:, :]

    cur, prev, nxt = _attn_specs(n_blocks)
    bias = _attn_bias(dil)
    return pl.pallas_call(
        body, name=f"attn_bwd_d{dil}", grid=(n_blocks // ATTN_QB,),
        in_specs=[cur(0), nxt(0), prev(1), cur(1), prev(2), cur(2), cur(0), nxt(0),
                  cur(0, LANES), nxt(0, LANES), cur(0, LANES), nxt(0, LANES),
                  pl.BlockSpec(bias.shape, lambda n: (0, 0, 0))],
        out_specs=pl.BlockSpec((rows, 3 * GROUP_W), lambda n: (n, 0)),
        out_shape=jax.ShapeDtypeStruct((s, 3 * GROUP_W), F32),
        scratch_shapes=[pltpu.VMEM(((ATTN_QB + 1) * ATTN_BLOCK, GROUP_W), F32),
                        pltpu.VMEM(((ATTN_QB + 1) * ATTN_BLOCK, GROUP_W), F32)],
        compiler_params=_params(("parallel",)),
    )(qkv, qkv, qkv, qkv, qkv, qkv, do, do, lse, lse, dlt, dlt, bias)


def _zcol(c):
    return pl.BlockSpec((TILE_MIX, GROUP_W), lambda i, c=c: (i, c))


def _zhalo(c):
    per = TILE_MIX // SUBLANES
    return pl.BlockSpec((SUBLANES, GROUP_W), lambda i, c=c: (jnp.maximum(i * per - 1, 0), c))


def _full(shape):
    return pl.BlockSpec(shape, lambda i: tuple(0 for _ in shape))


def _of_layer(a, l):
    rest = a.shape[1:]
    return pl.BlockSpec((None,) + rest, lambda i: (l,) + tuple(0 for _ in rest))


def _softplus_neg(lam):
    nl = -lam
    return jnp.maximum(nl, 0.0) + jnp.log1p(jnp.exp(-jnp.abs(nl)))


def _lru_gates(xb, wa_ref, wx_ref, ba, bx, lam):
    xbb = xb.astype(BF16)
    r = _sigmoid(_dot(xbb, wa_ref[...]) + ba)
    ig = _sigmoid(_dot(xbb, wx_ref[...]) + bx)
    log_a = (-RG_C * r) * _softplus_neg(lam)
    a = jnp.exp(log_a)
    mult = jnp.sqrt(_neg_expm1_2x(log_a))
    return r, ig, a, mult


LRU_HALF = 3
LRU_FULL = 2


def _gmlp_spatial(ws_ref, vvb, head):
    outs = []
    for j in range(vvb.shape[0] // GMLP_CHUNK):
        blk = vvb[j * GMLP_CHUNK:(j + 1) * GMLP_CHUNK, :]
        acc = jnp.zeros((GMLP_CHUNK, GROUP_W), F32)
        for h in range(N_HEADS):
            acc = jnp.where(head[:GMLP_CHUNK] == h, _dot(ws_ref[h], blk), acc)
        outs.append(acc)
    return jnp.concatenate(outs, axis=0)


def mix_fwd(z, attn, wts, l):
    s = z.shape[0]
    d4, d16 = PATTERN_DILS[1], PATTERN_DILS[2]

    def body(ax_ref, ab_ref, ac_ref, ag_ref, rx_ref, rg_ref, cu_ref, cv_ref, cg_ref, dg_ref,
             axh_ref, ach_ref, rxh_ref, o1_ref, l1_ref, o4_ref, l4_ref, o16_ref, l16_ref,
             caw_ref, crw_ref, crb_ref, wa_ref, wx_ref, ba_ref, bx_ref, lam_ref, gng_ref, ws_ref, bs_ref,
             y_ref, hl_ref, o_ref, lse_ref, lse4_ref, lse16_ref, lruh_ref, lruf_ref, carry, *stage):
        st_a, st_b, st_c, st_d, st_e = (stage[2 * k:2 * k + 2] for k in range(5))
        i = pl.program_id(0)

        @pl.when(i == 0)
        def _():
            carry[...] = jnp.zeros_like(carry)

        nz = (i > 0).astype(F32)
        head = _head_of_lane((TILE_MIX, GROUP_W))

        pa = ac_ref[...] * ax_ref[...]
        pah = ach_ref[...] * axh_ref[...] * nz
        cv = caw_ref[2:3, :] * pa + caw_ref[1:2, :] * _shift_down(pa, pah, 1) + caw_ref[0:1, :] * _shift_down(pa, pah, 2)
        y_ref[:, 0:GROUP_W] = (ab_ref[...] * cv * _silu(ag_ref[...])).astype(BF16)

        rx = rx_ref[...]
        rxh = rxh_ref[...] * nz
        xb = (crw_ref[3:4, :] * rx + crw_ref[2:3, :] * _shift_down(rx, rxh, 1) + crw_ref[1:2, :] * _shift_down(rx, rxh, 2)
              + crw_ref[0:1, :] * _shift_down(rx, rxh, 3) + crb_ref[...])
        r, ig, a, mult = _lru_gates(xb, wa_ref, wx_ref, ba_ref[...], bx_ref[...], lam_ref[...])
        for k, val in enumerate((xb, r, ig)):
            lruh_ref[:, k * GROUP_W:(k + 1) * GROUP_W] = val.astype(BF16)
        for k, val in enumerate((a, mult)):
            lruf_ref[:, k * GROUP_W:(k + 1) * GROUP_W] = val
        hl = _scan_causal(a, mult * (ig * xb), carry[SUBLANES - 1:SUBLANES, :])
        hl_ref[...] = hl
        carry[...] = hl[TILE_MIX - SUBLANES:, :]
        y_ref[:, GROUP_W:2 * GROUP_W] = (hl * _silu(rg_ref[...])).astype(BF16)

        u = _gelu(cu_ref[...])
        gv = _gelu(cv_ref[...])
        rs = lax.rsqrt(jnp.mean(gv * gv, axis=-1, keepdims=True) + NORM_EPS)
        vvb = (gv * rs * gng_ref[...]).astype(BF16)
        sp = _gmlp_spatial(ws_ref, vvb, head) + jnp.concatenate([bs_ref[...]] * (TILE_MIX // GMLP_CHUNK), axis=0)
        y_ref[:, 2 * GROUP_W:3 * GROUP_W] = (u * sp * _silu(cg_ref[...])).astype(BF16)

        ops = (o1_ref[...].astype(F32), _interleave_load(o4_ref, d4, st_a), _interleave_load(o16_ref, d16, st_b))
        lps = (l1_ref[...], _interleave_load(l4_ref, d4, st_c), _interleave_load(l16_ref, d16, st_d))
        m = jnp.maximum(jnp.maximum(lps[0], lps[1]), lps[2])
        ws = [jnp.exp(lp - m) for lp in lps]
        zsum = ws[0] + ws[1] + ws[2]
        lse = m + jnp.log(zsum)
        o = jnp.zeros((TILE_MIX, GROUP_W), F32)
        for op, w in zip(ops, ws):
            wn = w / zsum
            wide = jnp.zeros((TILE_MIX, GROUP_W), F32)
            for h in range(N_HEADS):
                wide = jnp.where(head == h, wn[:, h:h + 1], wide)
            o = o + wide * op
        o_ref[...] = o
        lse_ref[...] = lse
        _deinterleave_store(lse, st_e, ((d4, lse4_ref), (d16, lse16_ref)))
        y_ref[:, 3 * GROUP_W:4 * GROUP_W] = (o * _silu(dg_ref[...])).astype(BF16)

    row = pl.BlockSpec((TILE_MIX, GROUP_W), lambda i: (i, 0))
    r4 = _residue_block(d4, TILE_MIX, GROUP_W)
    r16 = _residue_block(d16, TILE_MIX, GROUP_W)
    crow = pl.BlockSpec((TILE_MIX, LANES), lambda i: (i, 0))
    c4 = _residue_block(d4, TILE_MIX, LANES)
    c16 = _residue_block(d16, TILE_MIX, LANES)
    names = ("caw", "crw", "crb", "wa", "wx", "ba", "bx", "lam", "gng", "ws", "bs")
    in_specs = ([_zcol(c) for c in (C_AX, C_AB, C_AC, C_AG, C_RX, C_RG, C_CU, C_CV, C_CG, C_DG)]
                + [_zhalo(C_AX), _zhalo(C_AC), _zhalo(C_RX), row, crow, r4, c4, r16, c16]
                + [_of_layer(wts[k], l) for k in names])
    return pl.pallas_call(
        body, name="mix_fwd", grid=(s // TILE_MIX,),
        in_specs=in_specs,
        out_specs=[pl.BlockSpec((TILE_MIX, D_MIX), lambda i: (i, 0)), row, row, crow, c4, c16,
                   pl.BlockSpec((TILE_MIX, LRU_HALF * GROUP_W), lambda i: (i, 0)),
                   pl.BlockSpec((TILE_MIX, LRU_FULL * GROUP_W), lambda i: (i, 0))],
        out_shape=([jax.ShapeDtypeStruct((s, D_MIX), BF16)] + [jax.ShapeDtypeStruct((s, GROUP_W), F32)] * 2
                   + [jax.ShapeDtypeStruct((s, LANES), F32), _by_residue(s, d4, LANES, F32),
                      _by_residue(s, d16, LANES, F32), jax.ShapeDtypeStruct((s, LRU_HALF * GROUP_W), BF16),
                      jax.ShapeDtypeStruct((s, LRU_FULL * GROUP_W), F32)]),
        scratch_shapes=([pltpu.VMEM((SUBLANES, GROUP_W), F32)] + _stage_scratch(TILE_MIX, GROUP_W, 4)
                        + _stage_scratch(TILE_MIX, LANES, 1)),
        compiler_params=_params(("arbitrary",)),
    )(*([z] * 13), *[a for pair in attn for a in pair], *[wts[k] for k in names])


def mix_bwd(dy, z, hl, lru, dqkv, ddg, wts, l):
    s = z.shape[0]
    d4, d16 = PATTERN_DILS[1], PATTERN_DILS[2]
    n_tiles = s // TILE_MIX

    def body(dya_ref, dyb_ref, dyc_ref, ax_ref, ab_ref, ac_ref, ag_ref, rx_ref, rg_ref, cu_ref, cv_ref, cg_ref,
             axh_ref, ach_ref, rxh_ref, hl_ref, hlh_ref, lruh_ref, lruf_ref, dqkv1_ref, dqkv4_ref, dqkv16_ref, ddg_ref,
             caw_ref, crw_ref, crb_ref, wa_ref, wx_ref, ba_ref, bx_ref, lam_ref, gng_ref, ws_ref, wst_ref, bs_ref,
             dz_ref, ga_ref, gr_ref, gn_ref, gwa_ref, gwx_ref, gws_ref, gbs_ref,
             c_dcv, c_g, c_a, c_dxb, *stage):
        st_a, st_b = stage[:len(stage) // 2], stage[len(stage) // 2:]
        step = pl.program_id(0)
        i = n_tiles - 1 - step

        @pl.when(step == 0)
        def _():
            for r in (c_dcv, c_g, c_a, c_dxb, ga_ref, gr_ref, gn_ref, gwa_ref, gwx_ref, gws_ref, gbs_ref):
                r[...] = jnp.zeros_like(r)

        nz = (i > 0).astype(F32)
        head = _head_of_lane((TILE_MIX, GROUP_W))
        shp8 = (SUBLANES, GROUP_W)
        colsum = lambda v: jnp.sum(v, axis=0, keepdims=True)

        ax, ab, ac, ag = ax_ref[...], ab_ref[...], ac_ref[...], ag_ref[...]
        dya = dya_ref[...]
        pa = ac * ax
        pah = ach_ref[...] * axh_ref[...] * nz
        pa1 = _shift_down(pa, pah, 1)
        pa2 = _shift_down(pa, pah, 2)
        cv = caw_ref[2:3, :] * pa + caw_ref[1:2, :] * pa1 + caw_ref[0:1, :] * pa2
        sg, dsg = _silu_and_grad(ag)
        dz_ref[:, C_AB * GROUP_W:(C_AB + 1) * GROUP_W] = (dya * cv * sg).astype(BF16)
        dz_ref[:, C_AG * GROUP_W:(C_AG + 1) * GROUP_W] = (dya * ab * cv * dsg).astype(BF16)
        dcv = dya * ab * sg
        nxt = c_dcv[...]
        dpa = caw_ref[2:3, :] * dcv + caw_ref[1:2, :] * _shift_up(dcv, nxt, 1) + caw_ref[0:1, :] * _shift_up(dcv, nxt, 2)
        c_dcv[...] = dcv[:SUBLANES, :]
        dz_ref[:, C_AC * GROUP_W:(C_AC + 1) * GROUP_W] = (dpa * ax).astype(BF16)
        dz_ref[:, C_AX * GROUP_W:(C_AX + 1) * GROUP_W] = (dpa * ac).astype(BF16)
        ga_ref[...] += (_put_row(shp8, 2, colsum(dcv * pa)) + _put_row(shp8, 1, colsum(dcv * pa1))
                        + _put_row(shp8, 0, colsum(dcv * pa2)))

        rx, rg = rx_ref[...], rg_ref[...]
        dyb = dyb_ref[...]
        rxh = rxh_ref[...] * nz
        rx1, rx2, rx3 = _shift_down(rx, rxh, 1), _shift_down(rx, rxh, 2), _shift_down(rx, rxh, 3)
        xb, r, ig = (lruh_ref[:, k * GROUP_W:(k + 1) * GROUP_W].astype(F32) for k in range(LRU_HALF))
        a, mult = (lruf_ref[:, k * GROUP_W:(k + 1) * GROUP_W] for k in range(LRU_FULL))
        lam = lam_ref[...]
        sp = _softplus_neg(lam)
        hl = hl_ref[...]
        hprev = _shift_down(hl, hlh_ref[...] * nz, 1)
        sgr, dsgr = _silu_and_grad(rg)
        dz_ref[:, C_RG * GROUP_W:(C_RG + 1) * GROUP_W] = (dyb * hl * dsgr).astype(BF16)
        dh = dyb * sgr
        a_next = _shift_up(a, c_a[...], 1)
        g = _scan_anticausal(a_next, dh, c_g[0:1, :])
        c_g[...] = g[:SUBLANES, :]
        c_a[...] = a[:SUBLANES, :]
        u = ig * xb
        da = g * hprev
        dmult = g * u
        du = g * mult
        dlog_a = da * a - dmult * (a * a) / mult
        dr = dlog_a * (-RG_C * sp)
        dga = dr * r * (1.0 - r)
        dgx = (du * xb) * ig * (1.0 - ig)
        dgab, dgxb = dga.astype(BF16), dgx.astype(BF16)
        dxb = du * ig + _dot_nt(dgab, wa_ref[...]) + _dot_nt(dgxb, wx_ref[...])
        xbb = xb.astype(BF16)
        gwa_ref[...] += _dot_tn(xbb, dgab)
        gwx_ref[...] += _dot_tn(xbb, dgxb)
        nxt = c_dxb[...]
        drx = (crw_ref[3:4, :] * dxb + crw_ref[2:3, :] * _shift_up(dxb, nxt, 1) + crw_ref[1:2, :] * _shift_up(dxb, nxt, 2)
               + crw_ref[0:1, :] * _shift_up(dxb, nxt, 3))
        c_dxb[...] = dxb[:SUBLANES, :]
        dz_ref[:, C_RX * GROUP_W:(C_RX + 1) * GROUP_W] = drx.astype(BF16)
        dlam = colsum(dlog_a * (-RG_C * r)) * (-_sigmoid(-lam))
        gr_ref[...] += (_put_row(shp8, 3, colsum(dxb * rx)) + _put_row(shp8, 2, colsum(dxb * rx1))
                        + _put_row(shp8, 1, colsum(dxb * rx2)) + _put_row(shp8, 0, colsum(dxb * rx3))
                        + _put_row(shp8, 4, colsum(dxb)) + _put_row(shp8, 5, colsum(dga))
                        + _put_row(shp8, 6, colsum(dgx)) + _put_row(shp8, 7, dlam))

        cu, cvv, cg = cu_ref[...], cv_ref[...], cg_ref[...]
        dyc = dyc_ref[...]
        u_c, du_c = _gelu_and_grad(cu)
        gv, dgv_c = _gelu_and_grad(cvv)
        rs = lax.rsqrt(jnp.mean(gv * gv, axis=-1, keepdims=True) + NORM_EPS)
        vh = gv * rs
        gng = gng_ref[...]
        vvb = (vh * gng).astype(BF16)
        spat = _gmlp_spatial(ws_ref, vvb, head) + jnp.concatenate([bs_ref[...]] * (TILE_MIX // GMLP_CHUNK), axis=0)
        sgc, dsgc = _silu_and_grad(cg)
        dz_ref[:, C_CU * GROUP_W:(C_CU + 1) * GROUP_W] = (dyc * spat * sgc * du_c).astype(BF16)
        dz_ref[:, C_CG * GROUP_W:(C_CG + 1) * GROUP_W] = (dyc * u_c * spat * dsgc).astype(BF16)
        dsp = dyc * u_c * sgc
        dspb = dsp.astype(BF16)
        tril = (lax.broadcasted_iota(jnp.int32, (GMLP_CHUNK, GMLP_CHUNK), 0)
                >= lax.broadcasted_iota(jnp.int32, (GMLP_CHUNK, GMLP_CHUNK), 1))
        head_c = head[:GMLP_CHUNK]
        dvv_parts = []
        gbs = jnp.zeros((GMLP_CHUNK, GROUP_W), F32)
        for j in range(TILE_MIX // GMLP_CHUNK):
            sl = slice(j * GMLP_CHUNK, (j + 1) * GMLP_CHUNK)
            dblk = dspb[sl, :]
            vblk = vvb[sl, :]
            gbs = gbs + dsp[sl, :]
            acc = jnp.zeros((GMLP_CHUNK, GROUP_W), F32)
            for h in range(N_HEADS):
                acc = jnp.where(head_c == h, _dot(wst_ref[h], dblk), acc)
                dm = jnp.where(head_c == h, dblk, jnp.zeros_like(dblk))
                gws_ref[h] += jnp.where(tril, _dot_nt(dm, vblk), 0.0)
            dvv_parts.append(acc)
        gbs_ref[...] += gbs
        dvv = jnp.concatenate(dvv_parts, axis=0)
        gn_ref[...] += _put_row(shp8, 0, colsum(dvv * vh))
        dvh = dvv * gng
        dgv = rs * (dvh - vh * jnp.mean(dvh * vh, axis=-1, keepdims=True))
        dz_ref[:, C_CV * GROUP_W:(C_CV + 1) * GROUP_W] = (dgv * dgv_c).astype(BF16)

        dsum = dqkv1_ref[...] + _interleave_load(dqkv4_ref, d4, st_a) + _interleave_load(dqkv16_ref, d16, st_b)
        dz_ref[:, C_DQ * GROUP_W:(C_DV + 1) * GROUP_W] = dsum.astype(BF16)
        dz_ref[:, C_DG * GROUP_W:(C_DG + 1) * GROUP_W] = ddg_ref[...].astype(BF16)

    per = TILE_MIX // SUBLANES
    qkv_w = 3 * GROUP_W
    rev = lambda c: pl.BlockSpec((TILE_MIX, GROUP_W), lambda t, c=c: (n_tiles - 1 - t, c))
    revh = lambda c: pl.BlockSpec((SUBLANES, GROUP_W),
                                  lambda t, c=c: (jnp.maximum((n_tiles - 1 - t) * per - 1, 0), c))
    revr = lambda dil: pl.BlockSpec((dil, TILE_MIX // dil, qkv_w), lambda t: (0, n_tiles - 1 - t, 0))
    names = ("caw", "crw", "crb", "wa", "wx", "ba", "bx", "lam", "gng", "ws", "wst", "bs")
    in_specs = ([rev(0), rev(1), rev(2)]
                + [rev(c) for c in (C_AX, C_AB, C_AC, C_AG, C_RX, C_RG, C_CU, C_CV, C_CG)]
                + [revh(C_AX), revh(C_AC), revh(C_RX), rev(0), revh(0),
                   pl.BlockSpec((TILE_MIX, LRU_HALF * GROUP_W), lambda t: (n_tiles - 1 - t, 0)),
                   pl.BlockSpec((TILE_MIX, LRU_FULL * GROUP_W), lambda t: (n_tiles - 1 - t, 0)),
                   pl.BlockSpec((TILE_MIX, qkv_w), lambda t: (n_tiles - 1 - t, 0)), revr(d4), revr(d16), rev(0)]
                + [_of_layer(wts[k], l) for k in names])
    small = jax.ShapeDtypeStruct((SUBLANES, GROUP_W), F32)
    sq = jax.ShapeDtypeStruct((GROUP_W, GROUP_W), F32)
    out_shape = [jax.ShapeDtypeStruct((s, D_IN), BF16), small, small, small, sq, sq,
                 jax.ShapeDtypeStruct((N_HEADS, GMLP_CHUNK, GMLP_CHUNK), F32),
                 jax.ShapeDtypeStruct((GMLP_CHUNK, GROUP_W), F32)]
    out_specs = ([pl.BlockSpec((TILE_MIX, D_IN), lambda t: (n_tiles - 1 - t, 0))]
                 + [_full(o.shape) for o in out_shape[1:]])
    return pl.pallas_call(
        body, name="mix_bwd", grid=(n_tiles,),
        in_specs=in_specs, out_specs=out_specs, out_shape=out_shape,
        scratch_shapes=[pltpu.VMEM((SUBLANES, GROUP_W), F32)] * 4 + _stage_scratch(TILE_MIX, qkv_w, 2),
        compiler_params=_params(("arbitrary",)),
    )(dy, dy, dy, *([z] * 12), hl, hl, *lru, *dqkv, ddg, *[wts[k] for k in names])


def prepare_small_weights(p):
    tril = jnp.tril(jnp.ones((GMLP_CHUNK, GMLP_CHUNK), dtype=bool))
    ws = jnp.where(tril, p["gmlp_ws"], 0.0).astype(BF16)
    row = lambda a: a[:, None, :]
    eye = jnp.eye(N_HEADS, dtype=F32)
    bd = lambda w: (w[:, :, :, None, :] * eye[None, :, None, :, None]).reshape(-1, GROUP_W, GROUP_W).astype(BF16)
    return dict(
        caw=p["conv_a_w"], crw=p["conv_r_w"], crb=row(p["conv_r_b"]),
        wa=bd(p["lru_wa"]), wx=bd(p["lru_wx"]),
        ba=row(p["lru_ba"]), bx=row(p["lru_bx"]), lam=row(p["lru_lambda"]), gng=row(p["gmlp_norm_g"]),
        ws=ws, wst=jnp.swapaxes(ws, 2, 3),
        bs=jnp.repeat(jnp.swapaxes(p["gmlp_bs"], 1, 2), HEAD_DIM, axis=2))


def _flat(a):
    return a.reshape(a.shape[0] * a.shape[1], a.shape[2])


def _split(a, dil):
    return a.reshape(dil, a.shape[0] // dil, a.shape[1])


def local_step(x, tgt, final_g, depth, chip, layer_weights, projections_done):
    saved = []
    for l in range(depth):
        gain, w_in_l, rest = layer_weights(l, x)
        z, h, *qkvs = in_fwd(x, gain, w_in_l)
        w_out_l, wts = rest(z)
        qkvs = [_flat(q) if q.ndim == 3 else q for q in qkvs]
        attn = []
        for q, d in zip(qkvs, PATTERN_DILS):
            o_p, lse_p = attn_fwd(q, d)
            attn.append((o_p, lse_p) if d == 1 else (_split(o_p, d), _split(lse_p, d)))
        y, hl, o, lse, lse4, lse16, *lru = mix_fwd(z, attn, wts, l)
        saved.append(dict(x=x, z=z, h=h, y=y, hl=hl, o=o, qkvs=qkvs, lses=(lse, _flat(lse4), _flat(lse16)), wts=wts, lru=lru,
                          gain=gain, w_in=w_in_l, w_out=w_out_l))
        if l < depth - 1:
            x = out_fwd(y, w_out_l, x)
        else:
            loss, dx, dfg = out_fwd_loss(y, w_out_l, x, final_g[None, :], tgt)
    raw = {k: [None] * depth for k in ("gain", "a", "r", "n", "wa", "wx", "ws", "bs")}
    zero = None
    for l in reversed(range(depth)):
        sv = saved[l]
        dy, ddg, do1, do4, do16, dl1, dl4, dl16 = out_bwd(dx, sv["w_out"], sv["z"], sv["o"])
        g_w_out = grad_w_out(sv["y"], dx)
        dqkv = []
        for q, do, lse, dl, d in zip(sv["qkvs"], (do1, _flat(do4), _flat(do16)), sv["lses"],
                                     (dl1, _flat(dl4), _flat(dl16)), PATTERN_DILS):
            g = attn_bwd(q, do, lse, dl, d)
            dqkv.append(g if d == 1 else _split(g, d))
        dz, ga, gr, gn, gwa, gwx, gws, gbs = mix_bwd(dy, sv["z"], sv["hl"], sv["lru"], dqkv, ddg, sv["wts"], l)
        gain = sv["gain"] if zero is None else sv["gain"] + zero
        zero = projections_done(l, *grad_w_in(sv["h"], dz, chip), g_w_out)
        if l == 0 and zero is not None:
            gain = gain + zero
        dx, dgn = in_bwd(dz, sv["w_in"], sv["x"], gain, dx)
        for k, g in zip(("gain", "a", "r", "n", "wa", "wx", "ws", "bs"), (dgn, ga, gr, gn, gwa, gwx, gws, gbs)):
            raw[k][l] = g
    st = {k: jnp.stack(v) for k, v in raw.items()}
    eye = jnp.eye(N_HEADS, dtype=F32)[None, :, None, :, None]
    diag = lambda g: (g.reshape(depth, N_HEADS, HEAD_DIM, N_HEADS, HEAD_DIM) * eye).sum(axis=3)
    grads = dict(
        norm_g=st["gain"][:, 0], conv_a_w=st["a"][:, :3], conv_r_w=st["r"][:, :4], conv_r_b=st["r"][:, 4],
        lru_ba=st["r"][:, 5], lru_bx=st["r"][:, 6], lru_lambda=st["r"][:, 7], gmlp_norm_g=st["n"][:, 0],
        lru_wa=diag(st["wa"]), lru_wx=diag(st["wx"]), gmlp_ws=st["ws"],
        gmlp_bs=jnp.swapaxes(st["bs"].reshape(depth, GMLP_CHUNK, N_HEADS, HEAD_DIM).sum(-1), 1, 2),
        final_g=dfg[0])
    return loss, dx, grads


MESH = pl.DeviceIdType.MESH
N_CHIPS = 4
N_DEV = 8
ANY = pl.BlockSpec(memory_space=pl.ANY)


def _place():
    x, y, c = lax.axis_index("x"), lax.axis_index("y"), lax.axis_index("c")
    chips = [(1 - x, y), (x, 1 - y), (1 - x, 1 - y)]
    return x, y, c, chips


def _remote(src, dst, ssem, rsem, to):
    return pltpu.make_async_remote_copy(src_ref=src, dst_ref=dst, send_sem=ssem, recv_sem=rsem,
                                        device_id=to, device_id_type=MESH)


HBM = pl.BlockSpec(memory_space=pltpu.HBM)
SEM = pl.BlockSpec(memory_space=pltpu.SEMAPHORE)
DATAFLOW = pltpu.SideEffectType.DATAFLOW_SIDE_EFFECTING
GATHER, SCATTER = "gather", "scatter"


def _chip_copies(mode, src_refs, land_refs, ssem, rsem):
    x, y, c, chips = _place()
    me = 2 * x + y
    n = len(src_refs)
    copies = []
    for k, (cx, cy) in enumerate(chips):
        for a in range(n):
            if mode == GATHER:
                src, dst = src_refs[a], land_refs[a].at[me]
            else:
                src, dst = src_refs[a].at[2 * cx + cy], land_refs[a].at[k]
            copies.append(_remote(src, dst, ssem.at[n * k + a], rsem.at[n * k + a], (cx, cy, c)))
    return copies


def exchange_start(mode, srcs, after, name):
    n = len(srcs)
    if mode == GATHER:
        lands = [lax.empty((N_CHIPS,) + s.shape, s.dtype) for s in srcs]
    else:
        lands = [lax.empty((N_CHIPS - 1,) + s.shape[1:], s.dtype) for s in srcs]
    extra = [] if after is None else [after]

    def body(*refs):
        src_refs, land_refs = refs[:n], refs[n:2 * n]
        ssem, rsem = refs[2 * n + len(extra)], refs[2 * n + len(extra) + 1]
        token = refs[-1]
        for cp in _chip_copies(mode, src_refs, land_refs, ssem, rsem):
            cp.start()
        token[...] = jnp.zeros_like(token)

    arrays = list(srcs) + lands
    return pl.pallas_call(
        body, name=name,
        out_shape=(pltpu.SemaphoreType.DMA((3 * n,)), pltpu.SemaphoreType.DMA((3 * n,)),
                   *[pltpu.HBM(a.shape, a.dtype) for a in arrays], jax.ShapeDtypeStruct((SUBLANES, LANES), F32)),
        in_specs=[HBM] * (2 * n) + [ANY] * len(extra),
        out_specs=(SEM, SEM, *[HBM] * (2 * n), pl.BlockSpec(memory_space=pltpu.VMEM)),
        input_output_aliases={i: 2 + i for i in range(2 * n)},
        compiler_params=pltpu.CompilerParams(has_side_effects=DATAFLOW),
    )(*[pltpu.with_memory_space_constraint(a, pltpu.HBM) for a in arrays], *extra)


def exchange_wait(mode, started, after, name):
    ssem, rsem, *thru, _ = started
    n = len(thru) // 2

    def body(*refs):
        src_refs, land_refs = refs[:n], refs[n:2 * n]
        ssem_ref, rsem_ref = refs[2 * n], refs[2 * n + 1]
        for cp in _chip_copies(mode, src_refs, land_refs, ssem_ref, rsem_ref):
            cp.wait_send()
            cp.wait_recv()

    outs = pl.pallas_call(
        body, name=name,
        out_shape=[pltpu.HBM(a.shape, a.dtype) for a in thru],
        in_specs=[HBM] * (2 * n) + [SEM, SEM, ANY],
        out_specs=[HBM] * (2 * n),
        input_output_aliases={i: i for i in range(2 * n)},
        compiler_params=pltpu.CompilerParams(has_side_effects=DATAFLOW),
    )(*thru, ssem, rsem, after)
    return outs[n:]


def sibling_exchange(p1, p2):
    def body(p1_ref, p2_ref, q1_ref, q2_ref, ssem, rsem):
        x, y, c, _ = _place()
        copies = [_remote(p_ref, q_ref, ssem.at[a], rsem.at[a], (x, y, 1 - c))
                  for a, (p_ref, q_ref) in enumerate(((p1_ref, q1_ref), (p2_ref, q2_ref)))]
        for cp in copies:
            cp.start()
        for cp in copies:
            cp.wait()

    return pl.pallas_call(
        body, name="sibling_exchange",
        in_specs=[ANY, ANY], out_specs=[ANY, ANY],
        out_shape=[jax.ShapeDtypeStruct(p.shape, p.dtype) for p in (p1, p2)],
        scratch_shapes=[pltpu.SemaphoreType.DMA((2,)), pltpu.SemaphoreType.DMA((2,))],
    )(p1, p2)


def all_reduce_small(v):
    r, n = v.shape
    piece = r // N_DEV

    def body(x_ref, out_ref, recv, ssem1, rsem1, ssem2, rsem2):
        x, y, c, _ = _place()
        me = 4 * x + 2 * y + c

        def peer(k):
            px = 1 - x if (k >> 2) & 1 else x
            py = 1 - y if (k >> 1) & 1 else y
            pc = 1 - c if k & 1 else c
            return (px, py, pc), 4 * px + 2 * py + pc

        def rows(ref, d):
            return ref.at[pl.ds(d * piece, piece), :]

        scatter = []
        for k in range(1, N_DEV):
            to, idx = peer(k)
            scatter.append(_remote(rows(x_ref, idx), recv.at[k], ssem1.at[k - 1], rsem1.at[k - 1], to))
            scatter[-1].start()
        acc = rows(x_ref, me)[...]
        for k in range(1, N_DEV):
            scatter[k - 1].wait_recv()
            acc = acc + recv[k]
        rows(out_ref, me)[...] = acc

        gather = []
        for k in range(1, N_DEV):
            to, _ = peer(k)
            gather.append(_remote(rows(out_ref, me), rows(out_ref, me), ssem2.at[k - 1], rsem2.at[k - 1], to))
            gather[-1].start()
        for k in range(1, N_DEV):
            to, idx = peer(k)
            _remote(rows(out_ref, idx), rows(out_ref, idx), ssem2.at[k - 1], rsem2.at[k - 1], to).wait_recv()
        for cp in scatter + gather:
            cp.wait_send()

    return pl.pallas_call(
        body, name="all_reduce_small",
        out_shape=jax.ShapeDtypeStruct((r, n), v.dtype),
        in_specs=[pl.BlockSpec(memory_space=pltpu.VMEM)],
        out_specs=pl.BlockSpec(memory_space=pltpu.VMEM),
        scratch_shapes=[pltpu.VMEM((N_DEV, piece, n), v.dtype)] + [pltpu.SemaphoreType.DMA((N_DEV - 1,))] * 4,
        compiler_params=pltpu.CompilerParams(vmem_limit_bytes=VMEM_LIMIT),
    )(v)


TILE_ROWS = 256


def _row_tile(r):
    return max(t for t in range(SUBLANES, TILE_ROWS + 1, SUBLANES) if r % t == 0)


def sum_partials(owns, parts):
    k, r, c = parts[0].shape
    depth = len(owns)
    tile = _row_tile(r)

    def body(buf_ref, o_ref, p_ref, out_ref):
        acc = o_ref[...]
        for i in range(k):
            acc = acc + p_ref[i].astype(F32)
        out_ref[0] = acc

    out = lax.empty((depth, r, c), F32)
    for l in range(depth):
        out = pl.pallas_call(
            functools.partial(body), name="sum_partials", grid=(r // tile,),
            in_specs=[ANY, pl.BlockSpec((tile, c), lambda i: (i, 0)), pl.BlockSpec((k, tile, c), lambda i: (0, i, 0))],
            out_specs=pl.BlockSpec((1, tile, c), lambda i, l=l: (l, i, 0)),
            out_shape=jax.ShapeDtypeStruct((depth, r, c), F32),
            input_output_aliases={0: 0},
            compiler_params=_params(("parallel",)),
        )(out, owns[l], parts[l])
    return out


def _adamw_update(w, g, m, v):
    m2 = ADAM_B1 * m + (1.0 - ADAM_B1) * g
    v2 = ADAM_B2 * v + (1.0 - ADAM_B2) * (g * g)
    m_hat = m2 / (1.0 - ADAM_B1 ** ADAM_STEP)
    v_hat = v2 / (1.0 - ADAM_B2 ** ADAM_STEP)
    return -ADAM_LR * (m_hat / (jnp.sqrt(v_hat) + ADAM_EPS) + ADAM_WD * w), m2, v2


def adamw_small(ws, gs, ms, vs):
    n = len(ws)

    def body(*refs):
        ins, outs = refs[:4 * n], refs[4 * n:]
        for i in range(n):
            d, m2, v2 = _adamw_update(ins[i][...], ins[n + i][...], ins[2 * n + i][...], ins[3 * n + i][...])
            outs[3 * i][...] = d
            outs[3 * i + 1][...] = m2
            outs[3 * i + 2][...] = v2

    outs = pl.pallas_call(
        body, name="adamw_small",
        out_shape=[jax.ShapeDtypeStruct(w.shape, F32) for w in ws for _ in range(3)],
    )(*ws, *gs, *ms, *vs)
    return [tuple(outs[3 * i:3 * i + 3]) for i in range(n)]


def adamw(w, ga, gb, m, v):
    n, r, c = w.shape
    tile = _row_tile(r)

    def body(w_ref, ga_ref, gb_ref, m_ref, v_ref, g_ref, d_ref, m2_ref, v2_ref):
        g = ga_ref[...] + gb_ref[...]
        g_ref[...] = g
        d_ref[...], m2_ref[...], v2_ref[...] = _adamw_update(w_ref[...], g, m_ref[...], v_ref[...])

    spec = pl.BlockSpec((1, tile, c), lambda j, i: (j, i, 0))
    return pl.pallas_call(
        body, name="adamw", grid=(n, r // tile),
        in_specs=[spec] * 5, out_specs=[spec] * 4,
        out_shape=[jax.ShapeDtypeStruct((n, r, c), F32)] * 4,
        compiler_params=_params(("parallel", "parallel")),
    )(w, ga, gb, m, v)


REPLICATED = ("norm_g", "conv_r_b", "lru_wa", "lru_ba", "lru_wx", "lru_bx", "lru_lambda", "gmlp_norm_g",
              "gmlp_ws", "gmlp_bs", "final_g")
CHIP_SHARDED_SMALL = ("conv_a_w", "conv_r_w")
PACK_LANES = LANES


def _pack(arrays):
    flat = jnp.concatenate([a.reshape(-1) for a in arrays])
    pad = (-flat.shape[0]) % (TILE_ROWS * PACK_LANES)
    return jnp.pad(flat, (0, pad)).reshape(-1, PACK_LANES)


def _unpack(packed, shapes):
    flat = packed.reshape(-1)
    out, off = [], 0
    for shp in shapes:
        n = math.prod(shp)
        out.append(flat[off:off + n].reshape(shp))
        off += n
    return out


def kernel(x, norm_g, w_in, conv_a_w, conv_r_w, conv_r_b, lru_wa, lru_ba, lru_wx, lru_bx, lru_lambda, gmlp_norm_g, gmlp_ws, gmlp_bs, w_out, final_g, loss_target, m_norm_g, m_w_in, m_conv_a_w, m_conv_r_w, m_conv_r_b, m_lru_wa, m_lru_ba, m_lru_wx, m_lru_bx, m_lru_lambda, m_gmlp_norm_g, m_gmlp_ws, m_gmlp_bs, m_w_out, m_final_g, v_norm_g, v_w_in, v_conv_a_w, v_conv_r_w, v_conv_r_b, v_lru_wa, v_lru_ba, v_lru_wx, v_lru_bx, v_lru_lambda, v_gmlp_norm_g, v_gmlp_ws, v_gmlp_bs, v_w_out, v_final_g):
    names = ("norm_g", "w_in", "conv_a_w", "conv_r_w", "conv_r_b", "lru_wa", "lru_ba", "lru_wx", "lru_bx",
             "lru_lambda", "gmlp_norm_g", "gmlp_ws", "gmlp_bs", "w_out", "final_g")
    w = dict(zip(names, (norm_g, w_in, conv_a_w, conv_r_w, conv_r_b, lru_wa, lru_ba, lru_wx, lru_bx, lru_lambda,
                         gmlp_norm_g, gmlp_ws, gmlp_bs, w_out, final_g)))
    m = dict(zip(names, (m_norm_g, m_w_in, m_conv_a_w, m_conv_r_w, m_conv_r_b, m_lru_wa, m_lru_ba, m_lru_wx, m_lru_bx,
                         m_lru_lambda, m_gmlp_norm_g, m_gmlp_ws, m_gmlp_bs, m_w_out, m_final_g)))
    v = dict(zip(names, (v_norm_g, v_w_in, v_conv_a_w, v_conv_r_w, v_conv_r_b, v_lru_wa, v_lru_ba, v_lru_wx, v_lru_bx,
                         v_lru_lambda, v_gmlp_norm_g, v_gmlp_ws, v_gmlp_bs, v_w_out, v_final_g)))
    depth = w_in.shape[0]
    out_rows = w_out.shape[1]
    conv_ch = conv_a_w.shape[2]
    chip = 2 * lax.axis_index("x") + lax.axis_index("y")

    taps = conv_a_w.shape[1] + conv_r_w.shape[1]
    w_in_t, m_w_in_t, v_w_in_t = (jnp.swapaxes(a, 1, 2) for a in (w_in, m_w_in, v_w_in))
    w_in_h, w_out_h = w_in_t.astype(BF16), w_out.astype(BF16)
    conv_own = jnp.concatenate([conv_a_w, conv_r_w], axis=1).reshape(depth * taps, conv_ch)
    gathers, token = [], None
    for l in range(depth):
        groups = [[w_in_h[l]], [w_out_h[l], conv_own]] if l == 0 else [[w_in_h[l], w_out_h[l]]]
        gathers.append([])
        for i, srcs in enumerate(groups):
            gathers[l].append(exchange_start(GATHER, srcs, token, f"gather_start_{l}_{i}"))
            token = gathers[l][-1][-1]
    p = dict(w)

    def with_own(land, own):
        return lax.dynamic_update_slice(land, own[None], (chip,) + (0,) * own.ndim)

    def layer_weights(l, x_l):
        lands = list(exchange_wait(GATHER, gathers[l][0], x_l, f"gather_wait_{l}_0"))
        w_in_l = with_own(lands[0], w_in_h[l]).reshape(D_IN, D_MODEL)
        gain = norm_g[l][None, :]
        if l == 0:
            gain = gain + token[0, 0]

        def rest(z_l):
            if l == 0:
                lands.extend(exchange_wait(GATHER, gathers[l][1], z_l, f"gather_wait_{l}_1"))
                conv = with_own(lands[2], conv_own).reshape(N_CHIPS, depth, taps, conv_ch)
                conv = conv.transpose(1, 2, 0, 3).reshape(depth, taps, GROUP_W)
                p["conv_a_w"] = conv[:, :conv_a_w.shape[1]]
                p["conv_r_w"] = conv[:, conv_a_w.shape[1]:]
                p["prepared"] = prepare_small_weights(p)
            return with_own(lands[1], w_out_h[l]).reshape(D_MIX, D_MODEL), p["prepared"]

        return gain, w_in_l, rest

    scatters, owns = [None] * depth, [None] * depth

    def projections_done(l, g_w_in_by_chip, g_w_in_own, g_w_out):
        go = g_w_out.reshape(N_CHIPS, out_rows, D_MODEL)
        owns[l] = (g_w_in_own, lax.dynamic_index_in_dim(go, chip, axis=0, keepdims=False))
        scatters[l] = exchange_start(SCATTER, [g_w_in_by_chip, go.astype(BF16)], None, f"scatter_start_{l}")
        return scatters[l][-1][0, 0]

    loss8, dx, grads = local_step(x[0], loss_target[0], final_g, depth, chip.reshape(1), layer_weights,
                                  projections_done)

    res = {}
    small = REPLICATED + CHIP_SHARDED_SMALL
    packed = _pack([grads[k] for k in small] + [loss8[0, :1]])
    total = all_reduce_small(packed)
    *sums, loss = _unpack(total, [grads[k].shape for k in small] + [()])
    gs = dict(zip(small, sums))
    for k in CHIP_SHARDED_SMALL:
        gs[k] = lax.dynamic_slice_in_dim(gs[k], chip * conv_ch, conv_ch, axis=2)
    as2d = lambda a: a[None] if a.ndim == 1 else a
    outs = adamw_small(*[[as2d(d[k]) for k in small] for d in (w, gs, m, v)])
    for k, (delta, m2, v2) in zip(small, outs):
        res[k] = [t.reshape(w[k].shape) for t in (gs[k], delta, m2, v2)]

    parts = [exchange_wait(SCATTER, scatters[l], total, f"scatter_wait_{l}") for l in range(depth)]
    p1 = sum_partials([owns[l][0] for l in range(depth)], [parts[l][0] for l in range(depth)])
    p2 = sum_partials([owns[l][1] for l in range(depth)], [parts[l][1] for l in range(depth)])
    q1, q2 = sibling_exchange(p1, p2)
    res["w_in"] = [jnp.swapaxes(t, 1, 2) for t in adamw(w_in_t, p1, q1, m_w_in_t, v_w_in_t)]
    res["w_out"] = adamw(w_out, p2, q2, m_w_out, v_w_out)

    return (loss, dx[None], *[res[k][0] for k in names], *[res[k][1] for k in names],
            *[res[k][2] for k in names], *[res[k][3] for k in names])
```

```python
import functools
import math

import jax
import jax.numpy as jnp
import numpy as np
from jax import lax
from jax.experimental import pallas as pl
from jax.experimental.pallas import tpu as pltpu

F32 = jnp.float32
BF16 = jnp.bfloat16

D_MODEL = 1024
GROUP_W = 256
N_HEADS = 4
HEAD_DIM = 64
N_CHUNKS = 13
D_IN = N_CHUNKS * GROUP_W
D_MIX = 4 * GROUP_W
NORM_EPS = 1e-6
RG_C = 8.0
GMLP_CHUNK = 128
ATTN_BLOCK = 128
PATTERN_DILS = (1, 4, 16)
ALIBI_SLOPES = tuple(2.0 ** (-8.0 * (h + 1) / N_HEADS) for h in range(N_HEADS))
ATTN_SCALE = 1.0 / math.sqrt(HEAD_DIM)
NEG_BIG = -1e30

ADAM_LR = 0.001
ADAM_B1 = 0.9
ADAM_B2 = 0.999
ADAM_EPS = 1e-08
ADAM_WD = 0.01
ADAM_STEP = 10

C_AX, C_AB, C_AC, C_AG, C_RX, C_RG, C_CU, C_CV, C_CG, C_DQ, C_DK, C_DV, C_DG = range(13)

SUBLANES = 8
LANES = 128
VMEM_LIMIT = 56 * 1024 * 1024
TILE_IN = 512
TILE_IN_BWD = 1024
TILE_OUT = 1024
TILE_MIX = 512
TILE_DW = 1024
TILE_DW_OUT = 2048
ATTN_QB = 16
GELU_K0 = math.sqrt(2.0 / math.pi)
GELU_K1 = 0.044715


def _resident(shape):
    return pl.BlockSpec(shape, lambda i: tuple(0 for _ in shape), pipeline_mode=pl.Buffered(1))


def _params(sem):
    return pltpu.CompilerParams(dimension_semantics=sem, vmem_limit_bytes=VMEM_LIMIT)


def _sigmoid(x):
    return 0.5 * jnp.tanh(0.5 * x) + 0.5


def _silu(x):
    return x * _sigmoid(x)


def _silu_and_grad(x):
    s = _sigmoid(x)
    return x * s, s * (1.0 + x * (1.0 - s))


def _gelu(x):
    return x * (0.5 + 0.5 * jnp.tanh(x * (GELU_K0 + (GELU_K0 * GELU_K1) * (x * x))))


def _gelu_and_grad(x):
    x2 = x * x
    t = jnp.tanh(x * (GELU_K0 + (GELU_K0 * GELU_K1) * x2))
    half = 0.5 + 0.5 * t
    return x * half, half + x * (0.5 - 0.5 * t * t) * (GELU_K0 + (3.0 * GELU_K0 * GELU_K1) * x2)


def _neg_expm1_2x(x):
    t = jnp.tanh(x)
    return -2.0 * t / (1.0 - t)


def _shift_down(v, halo, k):
    r = pltpu.roll(v, k, 0)
    rh = pltpu.roll(halo, k, 0)
    row = lax.broadcasted_iota(jnp.int32, halo.shape, 0)
    top = jnp.where(row < k, rh, r[:SUBLANES])
    return jnp.concatenate([top, r[SUBLANES:]], axis=0)


def _shift_up(v, halo, k):
    t = v.shape[0]
    r = pltpu.roll(v, t - k, 0)
    rh = pltpu.roll(halo, SUBLANES - k, 0)
    row = lax.broadcasted_iota(jnp.int32, halo.shape, 0)
    bot = jnp.where(row >= SUBLANES - k, rh, r[t - SUBLANES:])
    return jnp.concatenate([r[:t - SUBLANES], bot], axis=0)


def _scan_causal(a, b, h_in):
    t = a.shape[0]
    row8 = lax.broadcasted_iota(jnp.int32, a.shape, 0) % SUBLANES
    d = 1
    while d < SUBLANES:
        m = row8 >= d
        a_s = jnp.where(m, pltpu.roll(a, d, 0), 1.0)
        b_s = jnp.where(m, pltpu.roll(b, d, 0), 0.0)
        b = a * b_s + b
        a = a * a_s
        d *= 2
    out, carry = [], h_in
    for g in range(t // SUBLANES):
        sl = slice(g * SUBLANES, (g + 1) * SUBLANES)
        hg = b[sl] + a[sl] * carry
        out.append(hg)
        carry = hg[SUBLANES - 1:SUBLANES]
    return jnp.concatenate(out, axis=0)


def _scan_anticausal(a, b, g_in):
    t = a.shape[0]
    row8 = lax.broadcasted_iota(jnp.int32, a.shape, 0) % SUBLANES
    d = 1
    while d < SUBLANES:
        m = row8 < SUBLANES - d
        a_s = jnp.where(m, pltpu.roll(a, t - d, 0), 1.0)
        b_s = jnp.where(m, pltpu.roll(b, t - d, 0), 0.0)
        b = a * b_s + b
        a = a * a_s
        d *= 2
    out, carry = [], g_in
    for g in reversed(range(t // SUBLANES)):
        sl = slice(g * SUBLANES, (g + 1) * SUBLANES)
        gg = b[sl] + a[sl] * carry
        out.append(gg)
        carry = gg[0:1]
    return jnp.concatenate(out[::-1], axis=0)


def _head_of_lane(shape):
    return lax.broadcasted_iota(jnp.int32, shape, len(shape) - 1) // HEAD_DIM


def _per_head_lanes(cols):
    t = cols[0].shape[0]
    lane = lax.broadcasted_iota(jnp.int32, (t, LANES), 1)
    out = jnp.zeros((t, LANES), F32)
    for h, col in enumerate(cols):
        out = jnp.where(lane == h, col, out)
    return out


def _put_row(acc_shape, k, row_vec):
    row = lax.broadcasted_iota(jnp.int32, acc_shape, 0)
    return jnp.where(row == k, jnp.broadcast_to(row_vec, acc_shape), 0.0)


def _dot(a, b):
    return jnp.dot(a, b, preferred_element_type=F32)


def _dot_nt(a, b):
    return lax.dot_general(a, b, (((1,), (1,)), ((), ())), preferred_element_type=F32)


def _dot_tn(a, b):
    return lax.dot_general(a, b, (((0,), (0,)), ((), ())), preferred_element_type=F32)


def _deinterleave_store(val, stage, outs):
    t, c = val.shape
    for hh in range(c // LANES):
        stage[hh][...] = val[:, hh * LANES:(hh + 1) * LANES].astype(F32)
    for dil, ref in outs:
        for r in range(dil):
            for hh in range(c // LANES):
                ref[r, :, hh * LANES:(hh + 1) * LANES] = stage[hh][pl.ds(r, t // dil, stride=dil), :].astype(ref.dtype)


def _interleave_load(ref, dil, stage):
    _, n, c = ref.shape
    for r in range(dil):
        for hh in range(c // LANES):
            stage[hh][pl.ds(r, n, stride=dil), :] = ref[r, :, hh * LANES:(hh + 1) * LANES].astype(F32)
    return jnp.concatenate([stage[hh][...] for hh in range(c // LANES)], axis=1)


def _stage_scratch(tile, cols, copies):
    return [pltpu.VMEM((tile, LANES), F32)] * (copies * (cols // LANES))


def _by_residue(s, dil, cols, dtype):
    return jax.ShapeDtypeStruct((dil, s // dil, cols), dtype)


def _residue_block(dil, tile, cols):
    return pl.BlockSpec((dil, tile // dil, cols), lambda i: (0, i, 0))


def in_fwd(x, g, w):
    s = x.shape[0]
    qkv_w = 3 * GROUP_W

    def body(x_ref, g_ref, w_ref, z_ref, h_ref, qkv1_ref, qkv4_ref, qkv16_ref, *stage):
        xv = x_ref[...]
        rs = lax.rsqrt(jnp.mean(xv * xv, axis=-1, keepdims=True) + NORM_EPS)
        h = (xv * rs * g_ref[...]).astype(BF16)
        h_ref[...] = h
        z = _dot_nt(h, w_ref[...])
        z_ref[...] = z
        col = lax.broadcasted_iota(jnp.int32, (1, qkv_w), 1)
        qkv = z[:, C_DQ * GROUP_W:(C_DV + 1) * GROUP_W] * jnp.where(col < GROUP_W, ATTN_SCALE, 1.0)
        qkv1_ref[...] = qkv.astype(BF16)
        _deinterleave_store(qkv, stage, ((PATTERN_DILS[1], qkv4_ref), (PATTERN_DILS[2], qkv16_ref)))

    return pl.pallas_call(
        body, name="in_fwd", grid=(s // TILE_IN,),
        in_specs=[pl.BlockSpec((TILE_IN, D_MODEL), lambda i: (i, 0)),
                  pl.BlockSpec((1, D_MODEL), lambda i: (0, 0)),
                  _resident((D_IN, D_MODEL))],
        out_specs=[pl.BlockSpec((TILE_IN, D_IN), lambda i: (i, 0)),
                   pl.BlockSpec((TILE_IN, D_MODEL), lambda i: (i, 0)),
                   pl.BlockSpec((TILE_IN, qkv_w), lambda i: (i, 0)),
                   _residue_block(PATTERN_DILS[1], TILE_IN, qkv_w),
                   _residue_block(PATTERN_DILS[2], TILE_IN, qkv_w)],
        out_shape=[jax.ShapeDtypeStruct((s, D_IN), F32), jax.ShapeDtypeStruct((s, D_MODEL), BF16),
                   jax.ShapeDtypeStruct((s, qkv_w), BF16),
                   _by_residue(s, PATTERN_DILS[1], qkv_w, BF16), _by_residue(s, PATTERN_DILS[2], qkv_w, BF16)],
        scratch_shapes=_stage_scratch(TILE_IN, qkv_w, 1),
        compiler_params=_params(("parallel",)),
    )(x, g, w)


def out_fwd(y, w, x):
    s = x.shape[0]

    def body(y_ref, w_ref, x_ref, o_ref):
        o_ref[...] = x_ref[...] + _dot(y_ref[...], w_ref[...])

    return pl.pallas_call(
        body, name="out_fwd", grid=(s // TILE_OUT,),
        in_specs=[pl.BlockSpec((TILE_OUT, D_MIX), lambda i: (i, 0)),
                  pl.BlockSpec((D_MIX, D_MODEL), lambda i: (0, 0)),
                  pl.BlockSpec((TILE_OUT, D_MODEL), lambda i: (i, 0))],
        out_specs=pl.BlockSpec((TILE_OUT, D_MODEL), lambda i: (i, 0)),
        out_shape=jax.ShapeDtypeStruct((s, D_MODEL), F32),
        compiler_params=_params(("parallel",)),
    )(y, w, x)


def out_bwd(dx, w, z, o):
    s = dx.shape[0]
    abc = 3 * GROUP_W

    def body(dx_ref, w_ref, dg_ref, o_ref, dy_ref, ddg_ref, do1_ref, do4_ref, do16_ref, dl1_ref, dl4_ref, dl16_ref,
             *stage):
        stage_a, stage_b = stage[:2], stage[2:]
        dy = _dot_nt(dx_ref[...].astype(BF16), w_ref[...])
        dy_ref[...] = dy[:, :abc]
        dyd = dy[:, abc:]
        head = _head_of_lane((TILE_OUT, GROUP_W))
        dg = dg_ref[...]
        o = o_ref[...]
        sg, dsg = _silu_and_grad(dg)
        do = dyd * sg
        ddg_ref[...] = dyd * o * dsg
        prod = do * o
        dl = _per_head_lanes([jnp.sum(jnp.where(head == h, prod, 0.0), axis=-1, keepdims=True)
                              for h in range(N_HEADS)])
        do1_ref[...] = do.astype(BF16)
        dl1_ref[...] = dl
        _deinterleave_store(do, stage_a, ((PATTERN_DILS[1], do4_ref), (PATTERN_DILS[2], do16_ref)))
        _deinterleave_store(dl, stage_b, ((PATTERN_DILS[1], dl4_ref), (PATTERN_DILS[2], dl16_ref)))

    row = pl.BlockSpec((TILE_OUT, GROUP_W), lambda i: (i, 0))
    r4 = _residue_block(PATTERN_DILS[1], TILE_OUT, GROUP_W)
    r16 = _residue_block(PATTERN_DILS[2], TILE_OUT, GROUP_W)
    crow = pl.BlockSpec((TILE_OUT, LANES), lambda i: (i, 0))
    c4 = _residue_block(PATTERN_DILS[1], TILE_OUT, LANES)
    c16 = _residue_block(PATTERN_DILS[2], TILE_OUT, LANES)
    return pl.pallas_call(
        body, name="out_bwd", grid=(s // TILE_OUT,),
        in_specs=[pl.BlockSpec((TILE_OUT, D_MODEL), lambda i: (i, 0)),
                  pl.BlockSpec((D_MIX, D_MODEL), lambda i: (0, 0)),
                  pl.BlockSpec((TILE_OUT, GROUP_W), lambda i: (i, C_DG)), row],
        out_specs=[pl.BlockSpec((TILE_OUT, abc), lambda i: (i, 0)), row, row, r4, r16, crow, c4, c16],
        out_shape=[jax.ShapeDtypeStruct((s, abc), F32), jax.ShapeDtypeStruct((s, GROUP_W), F32),
                   jax.ShapeDtypeStruct((s, GROUP_W), BF16),
                   _by_residue(s, PATTERN_DILS[1], GROUP_W, BF16), _by_residue(s, PATTERN_DILS[2], GROUP_W, BF16),
                   jax.ShapeDtypeStruct((s, LANES), F32),
                   _by_residue(s, PATTERN_DILS[1], LANES, F32), _by_residue(s, PATTERN_DILS[2], LANES, F32)],
        scratch_shapes=_stage_scratch(TILE_OUT, GROUP_W, 1) + _stage_scratch(TILE_OUT, LANES, 1),
        compiler_params=_params(("parallel",)),
    )(dx, w, z, o)


def in_bwd(dz, w, x, g, dx_next):
    s = x.shape[0]

    def body(dz_ref, w_ref, x_ref, g_ref, dxn_ref, dx_ref, dg_ref):
        @pl.when(pl.program_id(0) == 0)
        def _():
            dg_ref[...] = jnp.zeros_like(dg_ref)

        dh = _dot(dz_ref[...], w_ref[...])
        xv = x_ref[...]
        rs = lax.rsqrt(jnp.mean(xv * xv, axis=-1, keepdims=True) + NORM_EPS)
        xh = xv * rs
        dg_ref[...] += _put_row(dg_ref.shape, 0, jnp.sum(dh * xh, axis=0, keepdims=True))
        dn = dh * g_ref[...]
        dx_ref[...] = dxn_ref[...] + rs * (dn - xh * jnp.mean(dn * xh, axis=-1, keepdims=True))

    return pl.pallas_call(
        body, name="in_bwd", grid=(s // TILE_IN_BWD,),
        in_specs=[pl.BlockSpec((TILE_IN_BWD, D_IN), lambda i: (i, 0)),
                  _resident((D_IN, D_MODEL)),
                  pl.BlockSpec((TILE_IN_BWD, D_MODEL), lambda i: (i, 0)),
                  pl.BlockSpec((1, D_MODEL), lambda i: (0, 0)),
                  pl.BlockSpec((TILE_IN_BWD, D_MODEL), lambda i: (i, 0))],
        out_specs=[pl.BlockSpec((TILE_IN_BWD, D_MODEL), lambda i: (i, 0)),
                   pl.BlockSpec((SUBLANES, D_MODEL), lambda i: (0, 0))],
        out_shape=[jax.ShapeDtypeStruct((s, D_MODEL), F32), jax.ShapeDtypeStruct((SUBLANES, D_MODEL), F32)],
        compiler_params=_params(("arbitrary",)),
    )(dz, w, x, g, dx_next)


def grad_w_out(y, dx):
    s = y.shape[0]

    def body(y_ref, dx_ref, o_ref):
        @pl.when(pl.program_id(0) == 0)
        def _():
            o_ref[...] = jnp.zeros_like(o_ref)

        o_ref[...] += _dot_tn(y_ref[...], dx_ref[...].astype(BF16))

    return pl.pallas_call(
        body, name="grad_w_out", grid=(s // TILE_DW_OUT,),
        in_specs=[pl.BlockSpec((TILE_DW_OUT, D_MIX), lambda k: (k, 0)),
                  pl.BlockSpec((TILE_DW_OUT, D_MODEL), lambda k: (k, 0))],
        out_specs=pl.BlockSpec((D_MIX, D_MODEL), lambda k: (0, 0)),
        out_shape=jax.ShapeDtypeStruct((D_MIX, D_MODEL), F32),
        compiler_params=_params(("arbitrary",)),
    )(y, dx)


def grad_w_in(h, dz, chip):
    s = h.shape[0]
    rows = D_IN // N_CHIPS

    def body(chip_ref, h_ref, dz_ref, staged_ref, own_ref, acc):
        k = pl.program_id(0)

        @pl.when(k == 0)
        def _():
            acc[...] = jnp.zeros_like(acc)

        acc[...] += _dot_tn(dz_ref[...], h_ref[...])

        @pl.when(k == s // TILE_DW - 1)
        def _():
            for j in range(N_CHIPS):
                part = acc[j * rows:(j + 1) * rows, :]
                staged_ref[j] = part.astype(BF16)

                @pl.when(chip_ref[0] == j)
                def _():
                    own_ref[...] = part

    return pl.pallas_call(
        body, name="grad_w_in",
        grid_spec=pltpu.PrefetchScalarGridSpec(
            num_scalar_prefetch=1, grid=(s // TILE_DW,),
            in_specs=[pl.BlockSpec((TILE_DW, D_MODEL), lambda k, c: (k, 0)),
                      pl.BlockSpec((TILE_DW, D_IN), lambda k, c: (k, 0))],
            out_specs=[pl.BlockSpec((N_CHIPS, rows, D_MODEL), lambda k, c: (0, 0, 0)),
                       pl.BlockSpec((rows, D_MODEL), lambda k, c: (0, 0))],
            scratch_shapes=[pltpu.VMEM((D_IN, D_MODEL), F32)]),
        out_shape=[jax.ShapeDtypeStruct((N_CHIPS, rows, D_MODEL), BF16), jax.ShapeDtypeStruct((rows, D_MODEL), F32)],
        compiler_params=_params(("arbitrary",)),
    )(chip, h, dz)


def out_fwd_loss(y, w, x, g, tgt):
    s = x.shape[0]

    def body(y_ref, w_ref, x_ref, g_ref, t_ref, l_ref, dx_ref, dg_ref):
        @pl.when(pl.program_id(0) == 0)
        def _():
            l_ref[...] = jnp.zeros_like(l_ref)
            dg_ref[...] = jnp.zeros_like(dg_ref)

        xv = x_ref[...] + _dot(y_ref[...], w_ref[...])
        gv = g_ref[...]
        rs = lax.rsqrt(jnp.mean(xv * xv, axis=-1, keepdims=True) + NORM_EPS)
        xh = xv * rs
        e = xh * gv - t_ref[...]
        part = 0.5 * jnp.sum(jnp.mean(e * e, axis=-1, keepdims=True), axis=0, keepdims=True)
        l_ref[...] += jnp.broadcast_to(part, l_ref.shape)
        dy = e * (1.0 / D_MODEL)
        dg_ref[...] += _put_row(dg_ref.shape, 0, jnp.sum(dy * xh, axis=0, keepdims=True))
        dn = dy * gv
        dx_ref[...] = rs * (dn - xh * jnp.mean(dn * xh, axis=-1, keepdims=True))

    return pl.pallas_call(
        body, name="out_fwd_loss", grid=(s // TILE_OUT,),
        in_specs=[pl.BlockSpec((TILE_OUT, D_MIX), lambda i: (i, 0)),
                  pl.BlockSpec((D_MIX, D_MODEL), lambda i: (0, 0)),
                  pl.BlockSpec((TILE_OUT, D_MODEL), lambda i: (i, 0)),
                  pl.BlockSpec((1, D_MODEL), lambda i: (0, 0)),
                  pl.BlockSpec((TILE_OUT, D_MODEL), lambda i: (i, 0))],
        out_specs=[pl.BlockSpec((SUBLANES, LANES), lambda i: (0, 0)),
                   pl.BlockSpec((TILE_OUT, D_MODEL), lambda i: (i, 0)),
                   pl.BlockSpec((SUBLANES, D_MODEL), lambda i: (0, 0))],
        out_shape=[jax.ShapeDtypeStruct((SUBLANES, LANES), F32), jax.ShapeDtypeStruct((s, D_MODEL), F32),
                   jax.ShapeDtypeStruct((SUBLANES, D_MODEL), F32)],
        compiler_params=_params(("arbitrary",)),
    )(y, w, x, g, tgt)


def _attn_bias(dil):
    qi = np.arange(ATTN_BLOCK)[:, None]
    ki = np.arange(2 * ATTN_BLOCK)[None, :]
    delta = qi + ATTN_BLOCK - ki
    band = (delta >= 0) & (delta <= ATTN_BLOCK)
    out = np.empty((2, N_HEADS, ATTN_BLOCK, 2 * ATTN_BLOCK), np.float32)
    for f in range(2):
        ok = band & ((ki >= ATTN_BLOCK) | (f == 0))
        for h in range(N_HEADS):
            out[f, h] = np.where(ok, -ALIBI_SLOPES[h] * dil * delta, NEG_BIG)
    return jnp.asarray(out.reshape(2, N_HEADS * ATTN_BLOCK, 2 * ATTN_BLOCK))


def _stack_heads(a, head):
    return jnp.concatenate([jnp.where(head == h, a, jnp.zeros_like(a)) for h in range(N_HEADS)], axis=0)


def _unstack_heads(a, head):
    out = a[:ATTN_BLOCK]
    for h in range(1, N_HEADS):
        out = jnp.where(head == h, a[h * ATTN_BLOCK:(h + 1) * ATTN_BLOCK], out)
    return out


def _head_column(a):
    return jnp.concatenate([a[:, h:h + 1] for h in range(N_HEADS)], axis=0)


def _attn_specs(n_blocks):
    rows = ATTN_QB * ATTN_BLOCK
    cur = lambda c, w=GROUP_W: pl.BlockSpec((rows, w), lambda n, c=c: (n, c))
    prev = lambda c: pl.BlockSpec((ATTN_BLOCK, GROUP_W), lambda n, c=c: (jnp.maximum(n * ATTN_QB - 1, 0), c))
    nxt = lambda c, w=GROUP_W: pl.BlockSpec((ATTN_BLOCK, w),
                                            lambda n, c=c: (jnp.minimum(n * ATTN_QB + ATTN_QB, n_blocks - 1), c))
    return cur, prev, nxt


def _keys(kp_ref, k_ref, j):
    prev = kp_ref[...] if j == 0 else k_ref[(j - 1) * ATTN_BLOCK:j * ATTN_BLOCK, :]
    return jnp.concatenate([prev, k_ref[j * ATTN_BLOCK:(j + 1) * ATTN_BLOCK, :]], axis=0)


def attn_fwd(qkv, dil):
    s = qkv.shape[0]
    n_blocks = s // ATTN_BLOCK
    bps = n_blocks // dil
    rows = ATTN_QB * ATTN_BLOCK

    def body(q_ref, kp_ref, k_ref, vp_ref, v_ref, bias_ref, o_ref, lse_ref):
        n = pl.program_id(0)
        head = _head_of_lane((ATTN_BLOCK, GROUP_W))
        for j in range(ATTN_QB):
            sl = slice(j * ATTN_BLOCK, (j + 1) * ATTN_BLOCK)
            first = (((n * ATTN_QB + j) % bps) == 0).astype(jnp.int32)
            qs = _stack_heads(q_ref[sl, :], head)
            sc = _dot_nt(qs, _keys(kp_ref, k_ref, j)) + bias_ref[first]
            m = jnp.max(sc, axis=-1, keepdims=True)
            pr = jnp.exp(sc - m)
            l = jnp.sum(pr, axis=-1, keepdims=True)
            oh = _dot(pr.astype(BF16), _keys(vp_ref, v_ref, j)) / l
            o_ref[sl, :] = _unstack_heads(oh, head).astype(BF16)
            ml = m + jnp.log(l)
            lse_ref[sl, :] = _per_head_lanes([ml[h * ATTN_BLOCK:(h + 1) * ATTN_BLOCK] for h in range(N_HEADS)])

    cur, prev, _ = _attn_specs(n_blocks)
    bias = _attn_bias(dil)
    return pl.pallas_call(
        body, name=f"attn_fwd_d{dil}", grid=(n_blocks // ATTN_QB,),
        in_specs=[cur(0), prev(1), cur(1), prev(2), cur(2), pl.BlockSpec(bias.shape, lambda n: (0, 0, 0))],
        out_specs=[cur(0), cur(0, LANES)],
        out_shape=[jax.ShapeDtypeStruct((s, GROUP_W), BF16), jax.ShapeDtypeStruct((s, LANES), F32)],
        compiler_params=_params(("parallel",)),
    )(qkv, qkv, qkv, qkv, qkv, bias)


def attn_bwd(qkv, do, lse, dlt, dil):
    s = qkv.shape[0]
    n_blocks = s // ATTN_BLOCK
    bps = n_blocks // dil
    rows = ATTN_QB * ATTN_BLOCK

    def body(q_ref, qn_ref, kp_ref, k_ref, vp_ref, v_ref, do_ref, don_ref, lse_ref, lsen_ref, dl_ref, dln_ref,
             bias_ref, out_ref, dk_acc, dv_acc):
        n = pl.program_id(0)
        head = _head_of_lane((ATTN_BLOCK, GROUP_W))
        dk_acc[...] = jnp.zeros_like(dk_acc)
        dv_acc[...] = jnp.zeros_like(dv_acc)

        def pair(qj, doj, lsej, dlj, kk, vv, bias, keep):
            qs = _stack_heads(qj, head)
            dos = _stack_heads(doj, head)
            sc = _dot_nt(qs, kk) + bias
            if keep is None:
                pr = jnp.exp(sc - _head_column(lsej))
            else:
                pr = jnp.exp(jnp.minimum(sc - _head_column(lsej), 0.0)) * keep
            dp = _dot_nt(dos, vv)
            ds = (pr * (dp - _head_column(dlj))).astype(BF16)
            return ds, _dot_tn(ds, qs), _dot_tn(pr.astype(BF16), dos)

        for j in range(ATTN_QB):
            sl = slice(j * ATTN_BLOCK, (j + 1) * ATTN_BLOCK)
            first = (((n * ATTN_QB + j) % bps) == 0).astype(jnp.int32)
            kk = _keys(kp_ref, k_ref, j)
            ds, dks, dvs = pair(q_ref[sl, :], do_ref[sl, :], lse_ref[sl, :], dl_ref[sl, :],
                                kk, _keys(vp_ref, v_ref, j), bias_ref[first], None)
            out_ref[sl, 0:GROUP_W] = _unstack_heads(_dot(ds, kk), head) * ATTN_SCALE
            acc = slice(j * ATTN_BLOCK, (j + 2) * ATTN_BLOCK)
            dk_acc[acc, :] += dks
            dv_acc[acc, :] += dvs

        nxt = n * ATTN_QB + ATTN_QB
        valid = ((nxt < n_blocks) & ((nxt % bps) != 0)).astype(F32)
        last = slice((ATTN_QB - 1) * ATTN_BLOCK, ATTN_QB * ATTN_BLOCK)
        _, dks, dvs = pair(qn_ref[...], don_ref[...], lsen_ref[...], dln_ref[...], k_ref[last, :], v_ref[last, :],
                           bias_ref[0][:, :ATTN_BLOCK], valid)
        acc = slice(ATTN_QB * ATTN_BLOCK, (ATTN_QB + 1) * ATTN_BLOCK)
        dk_acc[acc, :] += dks
        dv_acc[acc, :] += dvs
        out_ref[:, GROUP_W:2 * GROUP_W] = dk_acc[ATTN_BLOCK:, :]
        out_ref[:, 2 * GROUP_W:3 * GROUP_W] = dv_acc[ATTN_BLOCK:, :]

    cur, prev, nxt = _attn_specs(n_blocks)
    bias = _attn_bias(dil)
    return pl.pallas_call(
        body, name=f"attn_bwd_d{dil}", grid=(n_blocks // ATTN_QB,),
        in_specs=[cur(0), nxt(0), prev(1), cur(1), prev(2), cur(2), cur(0), nxt(0),
                  cur(0, LANES), nxt(0, LANES), cur(0, LANES), nxt(0, LANES),
                  pl.BlockSpec(bias.shape, lambda n: (0, 0, 0))],
        out_specs=pl.BlockSpec((rows, 3 * GROUP_W), lambda n: (n, 0)),
        out_shape=jax.ShapeDtypeStruct((s, 3 * GROUP_W), F32),
        scratch_shapes=[pltpu.VMEM(((ATTN_QB + 1) * ATTN_BLOCK, GROUP_W), F32),
                        pltpu.VMEM(((ATTN_QB + 1) * ATTN_BLOCK, GROUP_W), F32)],
        compiler_params=_params(("parallel",)),
    )(qkv, qkv, qkv, qkv, qkv, qkv, do, do, lse, lse, dlt, dlt, bias)


def _zcol(c):
    return pl.BlockSpec((TILE_MIX, GROUP_W), lambda i, c=c: (i, c))


def _zhalo(c):
    per = TILE_MIX // SUBLANES
    return pl.BlockSpec((SUBLANES, GROUP_W), lambda i, c=c: (jnp.maximum(i * per - 1, 0), c))


def _full(shape):
    return pl.BlockSpec(shape, lambda i: tuple(0 for _ in shape))


def _of_layer(a, l):
    rest = a.shape[1:]
    return pl.BlockSpec((None,) + rest, lambda i: (l,) + tuple(0 for _ in rest))


def _softplus_neg(lam):
    nl = -lam
    return jnp.maximum(nl, 0.0) + jnp.log1p(jnp.exp(-jnp.abs(nl)))


def _lru_gates(xb, wa_ref, wx_ref, ba, bx, lam):
    xbb = xb.astype(BF16)
    r = _sigmoid(_dot(xbb, wa_ref[...]) + ba)
    ig = _sigmoid(_dot(xbb, wx_ref[...]) + bx)
    log_a = (-RG_C * r) * _softplus_neg(lam)
    a = jnp.exp(log_a)
    mult = jnp.sqrt(_neg_expm1_2x(log_a))
    return r, ig, a, mult


LRU_SAVED = 5


def _gmlp_spatial(ws_ref, vvb, head):
    outs = []
    for j in range(vvb.shape[0] // GMLP_CHUNK):
        blk = vvb[j * GMLP_CHUNK:(j + 1) * GMLP_CHUNK, :]
        acc = jnp.zeros((GMLP_CHUNK, GROUP_W), F32)
        for h in range(N_HEADS):
            acc = jnp.where(head[:GMLP_CHUNK] == h, _dot(ws_ref[h], blk), acc)
        outs.append(acc)
    return jnp.concatenate(outs, axis=0)


def mix_fwd(z, attn, wts, l):
    s = z.shape[0]
    d4, d16 = PATTERN_DILS[1], PATTERN_DILS[2]

    def body(ax_ref, ab_ref, ac_ref, ag_ref, rx_ref, rg_ref, cu_ref, cv_ref, cg_ref, dg_ref,
             axh_ref, ach_ref, rxh_ref, o1_ref, l1_ref, o4_ref, l4_ref, o16_ref, l16_ref,
             caw_ref, crw_ref, crb_ref, wa_ref, wx_ref, ba_ref, bx_ref, lam_ref, gng_ref, ws_ref, bs_ref,
             y_ref, hl_ref, o_ref, lse_ref, lse4_ref, lse16_ref, lru_ref, carry, *stage):
        st_a, st_b, st_c, st_d, st_e = (stage[2 * k:2 * k + 2] for k in range(5))
        i = pl.program_id(0)

        @pl.when(i == 0)
        def _():
            carry[...] = jnp.zeros_like(carry)

        nz = (i > 0).astype(F32)
        head = _head_of_lane((TILE_MIX, GROUP_W))

        pa = ac_ref[...] * ax_ref[...]
        pah = ach_ref[...] * axh_ref[...] * nz
        cv = caw_ref[2:3, :] * pa + caw_ref[1:2, :] * _shift_down(pa, pah, 1) + caw_ref[0:1, :] * _shift_down(pa, pah, 2)
        y_ref[:, 0:GROUP_W] = (ab_ref[...] * cv * _silu(ag_ref[...])).astype(BF16)

        rx = rx_ref[...]
        rxh = rxh_ref[...] * nz
        xb = (crw_ref[3:4, :] * rx + crw_ref[2:3, :] * _shift_down(rx, rxh, 1) + crw_ref[1:2, :] * _shift_down(rx, rxh, 2)
              + crw_ref[0:1, :] * _shift_down(rx, rxh, 3) + crb_ref[...])
        r, ig, a, mult = _lru_gates(xb, wa_ref, wx_ref, ba_ref[...], bx_ref[...], lam_ref[...])
        for k, val in enumerate((xb, r, ig, a, mult)):
            lru_ref[:, k * GROUP_W:(k + 1) * GROUP_W] = val
        hl = _scan_causal(a, mult * (ig * xb), carry[SUBLANES - 1:SUBLANES, :])
        hl_ref[...] = hl
        carry[...] = hl[TILE_MIX - SUBLANES:, :]
        y_ref[:, GROUP_W:2 * GROUP_W] = (hl * _silu(rg_ref[...])).astype(BF16)

        u = _gelu(cu_ref[...])
        gv = _gelu(cv_ref[...])
        rs = lax.rsqrt(jnp.mean(gv * gv, axis=-1, keepdims=True) + NORM_EPS)
        vvb = (gv * rs * gng_ref[...]).astype(BF16)
        sp = _gmlp_spatial(ws_ref, vvb, head) + jnp.concatenate([bs_ref[...]] * (TILE_MIX // GMLP_CHUNK), axis=0)
        y_ref[:, 2 * GROUP_W:3 * GROUP_W] = (u * sp * _silu(cg_ref[...])).astype(BF16)

        ops = (o1_ref[...].astype(F32), _interleave_load(o4_ref, d4, st_a), _interleave_load(o16_ref, d16, st_b))
        lps = (l1_ref[...], _interleave_load(l4_ref, d4, st_c), _interleave_load(l16_ref, d16, st_d))
        m = jnp.maximum(jnp.maximum(lps[0], lps[1]), lps[2])
        ws = [jnp.exp(lp - m) for lp in lps]
        zsum = ws[0] + ws[1] + ws[2]
        lse = m + jnp.log(zsum)
        o = jnp.zeros((TILE_MIX, GROUP_W), F32)
        for op, w in zip(ops, ws):
            wn = w / zsum
            wide = jnp.zeros((TILE_MIX, GROUP_W), F32)
            for h in range(N_HEADS):
                wide = jnp.where(head == h, wn[:, h:h + 1], wide)
            o = o + wide * op
        o_ref[...] = o
        lse_ref[...] = lse
        _deinterleave_store(lse, st_e, ((d4, lse4_ref), (d16, lse16_ref)))
        y_ref[:, 3 * GROUP_W:4 * GROUP_W] = (o * _silu(dg_ref[...])).astype(BF16)

    row = pl.BlockSpec((TILE_MIX, GROUP_W), lambda i: (i, 0))
    r4 = _residue_block(d4, TILE_MIX, GROUP_W)
    r16 = _residue_block(d16, TILE_MIX, GROUP_W)
    crow = pl.BlockSpec((TILE_MIX, LANES), lambda i: (i, 0))
    c4 = _residue_block(d4, TILE_MIX, LANES)
    c16 = _residue_block(d16, TILE_MIX, LANES)
    names = ("caw", "crw", "crb", "wa", "wx", "ba", "bx", "lam", "gng", "ws", "bs")
    in_specs = ([_zcol(c) for c in (C_AX, C_AB, C_AC, C_AG, C_RX, C_RG, C_CU, C_CV, C_CG, C_DG)]
                + [_zhalo(C_AX), _zhalo(C_AC), _zhalo(C_RX), row, crow, r4, c4, r16, c16]
                + [_of_layer(wts[k], l) for k in names])
    return pl.pallas_call(
        body, name="mix_fwd", grid=(s // TILE_MIX,),
        in_specs=in_specs,
        out_specs=[pl.BlockSpec((TILE_MIX, D_MIX), lambda i: (i, 0)), row, row, crow, c4, c16,
                   pl.BlockSpec((TILE_MIX, LRU_SAVED * GROUP_W), lambda i: (i, 0))],
        out_shape=([jax.ShapeDtypeStruct((s, D_MIX), BF16)] + [jax.ShapeDtypeStruct((s, GROUP_W), F32)] * 2
                   + [jax.ShapeDtypeStruct((s, LANES), F32), _by_residue(s, d4, LANES, F32),
                      _by_residue(s, d16, LANES, F32), jax.ShapeDtypeStruct((s, LRU_SAVED * GROUP_W), F32)]),
        scratch_shapes=([pltpu.VMEM((SUBLANES, GROUP_W), F32)] + _stage_scratch(TILE_MIX, GROUP_W, 4)
                        + _stage_scratch(TILE_MIX, LANES, 1)),
        compiler_params=_params(("arbitrary",)),
    )(*([z] * 13), *[a for pair in attn for a in pair], *[wts[k] for k in names])


def mix_bwd(dy, z, hl, lru, dqkv, ddg, wts, l):
    s = z.shape[0]
    d4, d16 = PATTERN_DILS[1], PATTERN_DILS[2]
    n_tiles = s // TILE_MIX

    def body(dya_ref, dyb_ref, dyc_ref, ax_ref, ab_ref, ac_ref, ag_ref, rx_ref, rg_ref, cu_ref, cv_ref, cg_ref,
             axh_ref, ach_ref, rxh_ref, hl_ref, hlh_ref, lru_ref, dqkv1_ref, dqkv4_ref, dqkv16_ref, ddg_ref,
             caw_ref, crw_ref, crb_ref, wa_ref, wx_ref, ba_ref, bx_ref, lam_ref, gng_ref, ws_ref, wst_ref, bs_ref,
             dz_ref, ga_ref, gr_ref, gn_ref, gwa_ref, gwx_ref, gws_ref, gbs_ref,
             c_dcv, c_g, c_a, c_dxb, *stage):
        st_a, st_b = stage[:len(stage) // 2], stage[len(stage) // 2:]
        step = pl.program_id(0)
        i = n_tiles - 1 - step

        @pl.when(step == 0)
        def _():
            for r in (c_dcv, c_g, c_a, c_dxb, ga_ref, gr_ref, gn_ref, gwa_ref, gwx_ref, gws_ref, gbs_ref):
                r[...] = jnp.zeros_like(r)

        nz = (i > 0).astype(F32)
        head = _head_of_lane((TILE_MIX, GROUP_W))
        shp8 = (SUBLANES, GROUP_W)
        colsum = lambda v: jnp.sum(v, axis=0, keepdims=True)

        ax, ab, ac, ag = ax_ref[...], ab_ref[...], ac_ref[...], ag_ref[...]
        dya = dya_ref[...]
        pa = ac * ax
        pah = ach_ref[...] * axh_ref[...] * nz
        pa1 = _shift_down(pa, pah, 1)
        pa2 = _shift_down(pa, pah, 2)
        cv = caw_ref[2:3, :] * pa + caw_ref[1:2, :] * pa1 + caw_ref[0:1, :] * pa2
        sg, dsg = _silu_and_grad(ag)
        dz_ref[:, C_AB * GROUP_W:(C_AB + 1) * GROUP_W] = (dya * cv * sg).astype(BF16)
        dz_ref[:, C_AG * GROUP_W:(C_AG + 1) * GROUP_W] = (dya * ab * cv * dsg).astype(BF16)
        dcv = dya * ab * sg
        nxt = c_dcv[...]
        dpa = caw_ref[2:3, :] * dcv + caw_ref[1:2, :] * _shift_up(dcv, nxt, 1) + caw_ref[0:1, :] * _shift_up(dcv, nxt, 2)
        c_dcv[...] = dcv[:SUBLANES, :]
        dz_ref[:, C_AC * GROUP_W:(C_AC + 1) * GROUP_W] = (dpa * ax).astype(BF16)
        dz_ref[:, C_AX * GROUP_W:(C_AX + 1) * GROUP_W] = (dpa * ac).astype(BF16)
        ga_ref[...] += (_put_row(shp8, 2, colsum(dcv * pa)) + _put_row(shp8, 1, colsum(dcv * pa1))
                        + _put_row(shp8, 0, colsum(dcv * pa2)))

        rx, rg = rx_ref[...], rg_ref[...]
        dyb = dyb_ref[...]
        rxh = rxh_ref[...] * nz
        rx1, rx2, rx3 = _shift_down(rx, rxh, 1), _shift_down(rx, rxh, 2), _shift_down(rx, rxh, 3)
        xb, r, ig, a, mult = (lru_ref[:, k * GROUP_W:(k + 1) * GROUP_W] for k in range(LRU_SAVED))
        lam = lam_ref[...]
        sp = _softplus_neg(lam)
        hl = hl_ref[...]
        hprev = _shift_down(hl, hlh_ref[...] * nz, 1)
        sgr, dsgr = _silu_and_grad(rg)
        dz_ref[:, C_RG * GROUP_W:(C_RG + 1) * GROUP_W] = (dyb * hl * dsgr).astype(BF16)
        dh = dyb * sgr
        a_next = _shift_up(a, c_a[...], 1)
        g = _scan_anticausal(a_next, dh, c_g[0:1, :])
        c_g[...] = g[:SUBLANES, :]
        c_a[...] = a[:SUBLANES, :]
        u = ig * xb
        da = g * hprev
        dmult = g * u
        du = g * mult
        dlog_a = da * a - dmult * (a * a) / mult
        dr = dlog_a * (-RG_C * sp)
        dga = dr * r * (1.0 - r)
        dgx = (du * xb) * ig * (1.0 - ig)
        dgab, dgxb = dga.astype(BF16), dgx.astype(BF16)
        dxb = du * ig + _dot_nt(dgab, wa_ref[...]) + _dot_nt(dgxb, wx_ref[...])
        xbb = xb.astype(BF16)
        gwa_ref[...] += _dot_tn(xbb, dgab)
        gwx_ref[...] += _dot_tn(xbb, dgxb)
        nxt = c_dxb[...]
        drx = (crw_ref[3:4, :] * dxb + crw_ref[2:3, :] * _shift_up(dxb, nxt, 1) + crw_ref[1:2, :] * _shift_up(dxb, nxt, 2)
               + crw_ref[0:1, :] * _shift_up(dxb, nxt, 3))
        c_dxb[...] = dxb[:SUBLANES, :]
        dz_ref[:, C_RX * GROUP_W:(C_RX + 1) * GROUP_W] = drx.astype(BF16)
        dlam = colsum(dlog_a * (-RG_C * r)) * (-_sigmoid(-lam))
        gr_ref[...] += (_put_row(shp8, 3, colsum(dxb * rx)) + _put_row(shp8, 2, colsum(dxb * rx1))
                        + _put_row(shp8, 1, colsum(dxb * rx2)) + _put_row(shp8, 0, colsum(dxb * rx3))
                        + _put_row(shp8, 4, colsum(dxb)) + _put_row(shp8, 5, colsum(dga))
                        + _put_row(shp8, 6, colsum(dgx)) + _put_row(shp8, 7, dlam))

        cu, cvv, cg = cu_ref[...], cv_ref[...], cg_ref[...]
        dyc = dyc_ref[...]
        u_c, du_c = _gelu_and_grad(cu)
        gv, dgv_c = _gelu_and_grad(cvv)
        rs = lax.rsqrt(jnp.mean(gv * gv, axis=-1, keepdims=True) + NORM_EPS)
        vh = gv * rs
        gng = gng_ref[...]
        vvb = (vh * gng).astype(BF16)
        spat = _gmlp_spatial(ws_ref, vvb, head) + jnp.concatenate([bs_ref[...]] * (TILE_MIX // GMLP_CHUNK), axis=0)
        sgc, dsgc = _silu_and_grad(cg)
        dz_ref[:, C_CU * GROUP_W:(C_CU + 1) * GROUP_W] = (dyc * spat * sgc * du_c).astype(BF16)
        dz_ref[:, C_CG * GROUP_W:(C_CG + 1) * GROUP_W] = (dyc * u_c * spat * dsgc).astype(BF16)
        dsp = dyc * u_c * sgc
        dspb = dsp.astype(BF16)
        tril = (lax.broadcasted_iota(jnp.int32, (GMLP_CHUNK, GMLP_CHUNK), 0)
                >= lax.broadcasted_iota(jnp.int32, (GMLP_CHUNK, GMLP_CHUNK), 1))
        head_c = head[:GMLP_CHUNK]
        dvv_parts = []
        gbs = jnp.zeros((GMLP_CHUNK, GROUP_W), F32)
        for j in range(TILE_MIX // GMLP_CHUNK):
            sl = slice(j * GMLP_CHUNK, (j + 1) * GMLP_CHUNK)
            dblk = dspb[sl, :]
            vblk = vvb[sl, :]
            gbs = gbs + dsp[sl, :]
            acc = jnp.zeros((GMLP_CHUNK, GROUP_W), F32)
            for h in range(N_HEADS):
                acc = jnp.where(head_c == h, _dot(wst_ref[h], dblk), acc)
                dm = jnp.where(head_c == h, dblk, jnp.zeros_like(dblk))
                gws_ref[h] += jnp.where(tril, _dot_nt(dm, vblk), 0.0)
            dvv_parts.append(acc)
        gbs_ref[...] += gbs
        dvv = jnp.concatenate(dvv_parts, axis=0)
        gn_ref[...] += _put_row(shp8, 0, colsum(dvv * vh))
        dvh = dvv * gng
        dgv = rs * (dvh - vh * jnp.mean(dvh * vh, axis=-1, keepdims=True))
        dz_ref[:, C_CV * GROUP_W:(C_CV + 1) * GROUP_W] = (dgv * dgv_c).astype(BF16)

        dsum = dqkv1_ref[...] + _interleave_load(dqkv4_ref, d4, st_a) + _interleave_load(dqkv16_ref, d16, st_b)
        dz_ref[:, C_DQ * GROUP_W:(C_DV + 1) * GROUP_W] = dsum.astype(BF16)
        dz_ref[:, C_DG * GROUP_W:(C_DG + 1) * GROUP_W] = ddg_ref[...].astype(BF16)

    per = TILE_MIX // SUBLANES
    qkv_w = 3 * GROUP_W
    rev = lambda c: pl.BlockSpec((TILE_MIX, GROUP_W), lambda t, c=c: (n_tiles - 1 - t, c))
    revh = lambda c: pl.BlockSpec((SUBLANES, GROUP_W),
                                  lambda t, c=c: (jnp.maximum((n_tiles - 1 - t) * per - 1, 0), c))
    revr = lambda dil: pl.BlockSpec((dil, TILE_MIX // dil, qkv_w), lambda t: (0, n_tiles - 1 - t, 0))
    names = ("caw", "crw", "crb", "wa", "wx", "ba", "bx", "lam", "gng", "ws", "wst", "bs")
    in_specs = ([rev(0), rev(1), rev(2)]
                + [rev(c) for c in (C_AX, C_AB, C_AC, C_AG, C_RX, C_RG, C_CU, C_CV, C_CG)]
                + [revh(C_AX), revh(C_AC), revh(C_RX), rev(0), revh(0),
                   pl.BlockSpec((TILE_MIX, LRU_SAVED * GROUP_W), lambda t: (n_tiles - 1 - t, 0)),
                   pl.BlockSpec((TILE_MIX, qkv_w), lambda t: (n_tiles - 1 - t, 0)), revr(d4), revr(d16), rev(0)]
                + [_of_layer(wts[k], l) for k in names])
    small = jax.ShapeDtypeStruct((SUBLANES, GROUP_W), F32)
    sq = jax.ShapeDtypeStruct((GROUP_W, GROUP_W), F32)
    out_shape = [jax.ShapeDtypeStruct((s, D_IN), BF16), small, small, small, sq, sq,
                 jax.ShapeDtypeStruct((N_HEADS, GMLP_CHUNK, GMLP_CHUNK), F32),
                 jax.ShapeDtypeStruct((GMLP_CHUNK, GROUP_W), F32)]
    out_specs = ([pl.BlockSpec((TILE_MIX, D_IN), lambda t: (n_tiles - 1 - t, 0))]
                 + [_full(o.shape) for o in out_shape[1:]])
    return pl.pallas_call(
        body, name="mix_bwd", grid=(n_tiles,),
        in_specs=in_specs, out_specs=out_specs, out_shape=out_shape,
        scratch_shapes=[pltpu.VMEM((SUBLANES, GROUP_W), F32)] * 4 + _stage_scratch(TILE_MIX, qkv_w, 2),
        compiler_params=_params(("arbitrary",)),
    )(dy, dy, dy, *([z] * 12), hl, hl, lru, *dqkv, ddg, *[wts[k] for k in names])


def prepare_small_weights(p):
    tril = jnp.tril(jnp.ones((GMLP_CHUNK, GMLP_CHUNK), dtype=bool))
    ws = jnp.where(tril, p["gmlp_ws"], 0.0).astype(BF16)
    row = lambda a: a[:, None, :]
    eye = jnp.eye(N_HEADS, dtype=F32)
    bd = lambda w: (w[:, :, :, None, :] * eye[None, :, None, :, None]).reshape(-1, GROUP_W, GROUP_W).astype(BF16)
    return dict(
        caw=p["conv_a_w"], crw=p["conv_r_w"], crb=row(p["conv_r_b"]),
        wa=bd(p["lru_wa"]), wx=bd(p["lru_wx"]),
        ba=row(p["lru_ba"]), bx=row(p["lru_bx"]), lam=row(p["lru_lambda"]), gng=row(p["gmlp_norm_g"]),
        ws=ws, wst=jnp.swapaxes(ws, 2, 3),
        bs=jnp.repeat(jnp.swapaxes(p["gmlp_bs"], 1, 2), HEAD_DIM, axis=2))


def _flat(a):
    return a.reshape(a.shape[0] * a.shape[1], a.shape[2])


def _split(a, dil):
    return a.reshape(dil, a.shape[0] // dil, a.shape[1])


def local_step(x, tgt, final_g, depth, chip, layer_weights, projections_done):
    saved = []
    for l in range(depth):
        gain, w_in_l, rest = layer_weights(l, x)
        z, h, *qkvs = in_fwd(x, gain, w_in_l)
        w_out_l, wts = rest(z)
        qkvs = [_flat(q) if q.ndim == 3 else q for q in qkvs]
        attn = []
        for q, d in zip(qkvs, PATTERN_DILS):
            o_p, lse_p = attn_fwd(q, d)
            attn.append((o_p, lse_p) if d == 1 else (_split(o_p, d), _split(lse_p, d)))
        y, hl, o, lse, lse4, lse16, lru = mix_fwd(z, attn, wts, l)
        saved.append(dict(x=x, z=z, h=h, y=y, hl=hl, o=o, qkvs=qkvs, lses=(lse, _flat(lse4), _flat(lse16)), wts=wts, lru=lru,
                          gain=gain, w_in=w_in_l, w_out=w_out_l))
        if l < depth - 1:
            x = out_fwd(y, w_out_l, x)
        else:
            loss, dx, dfg = out_fwd_loss(y, w_out_l, x, final_g[None, :], tgt)
    raw = {k: [None] * depth for k in ("gain", "a", "r", "n", "wa", "wx", "ws", "bs")}
    zero = None
    for l in reversed(range(depth)):
        sv = saved[l]
        dy, ddg, do1, do4, do16, dl1, dl4, dl16 = out_bwd(dx, sv["w_out"], sv["z"], sv["o"])
        g_w_out = grad_w_out(sv["y"], dx)
        dqkv = []
        for q, do, lse, dl, d in zip(sv["qkvs"], (do1, _flat(do4), _flat(do16)), sv["lses"],
                                     (dl1, _flat(dl4), _flat(dl16)), PATTERN_DILS):
            g = attn_bwd(q, do, lse, dl, d)
            dqkv.append(g if d == 1 else _split(g, d))
        dz, ga, gr, gn, gwa, gwx, gws, gbs = mix_bwd(dy, sv["z"], sv["hl"], sv["lru"], dqkv, ddg, sv["wts"], l)
        gain = sv["gain"] if zero is None else sv["gain"] + zero
        zero = projections_done(l, *grad_w_in(sv["h"], dz, chip), g_w_out)
        if l == 0 and zero is not None:
            gain = gain + zero
        dx, dgn = in_bwd(dz, sv["w_in"], sv["x"], gain, dx)
        for k, g in zip(("gain", "a", "r", "n", "wa", "wx", "ws", "bs"), (dgn, ga, gr, gn, gwa, gwx, gws, gbs)):
            raw[k][l] = g
    st = {k: jnp.stack(v) for k, v in raw.items()}
    eye = jnp.eye(N_HEADS, dtype=F32)[None, :, None, :, None]
    diag = lambda g: (g.reshape(depth, N_HEADS, HEAD_DIM, N_HEADS, HEAD_DIM) * eye).sum(axis=3)
    grads = dict(
        norm_g=st["gain"][:, 0], conv_a_w=st["a"][:, :3], conv_r_w=st["r"][:, :4], conv_r_b=st["r"][:, 4],
        lru_ba=st["r"][:, 5], lru_bx=st["r"][:, 6], lru_lambda=st["r"][:, 7], gmlp_norm_g=st["n"][:, 0],
        lru_wa=diag(st["wa"]), lru_wx=diag(st["wx"]), gmlp_ws=st["ws"],
        gmlp_bs=jnp.swapaxes(st["bs"].reshape(depth, GMLP_CHUNK, N_HEADS, HEAD_DIM).sum(-1), 1, 2),
        final_g=dfg[0])
    return loss, dx, grads


MESH = pl.DeviceIdType.MESH
N_CHIPS = 4
N_DEV = 8
ANY = pl.BlockSpec(memory_space=pl.ANY)


def _place():
    x, y, c = lax.axis_index("x"), lax.axis_index("y"), lax.axis_index("c")
    chips = [(1 - x, y), (x, 1 - y), (1 - x, 1 - y)]
    return x, y, c, chips


def _remote(src, dst, ssem, rsem, to):
    return pltpu.make_async_remote_copy(src_ref=src, dst_ref=dst, send_sem=ssem, recv_sem=rsem,
                                        device_id=to, device_id_type=MESH)


HBM = pl.BlockSpec(memory_space=pltpu.HBM)
SEM = pl.BlockSpec(memory_space=pltpu.SEMAPHORE)
DATAFLOW = pltpu.SideEffectType.DATAFLOW_SIDE_EFFECTING
GATHER, SCATTER = "gather", "scatter"


def _chip_copies(mode, src_refs, land_refs, ssem, rsem):
    x, y, c, chips = _place()
    me = 2 * x + y
    n = len(src_refs)
    copies = []
    for k, (cx, cy) in enumerate(chips):
        for a in range(n):
            if mode == GATHER:
                src, dst = src_refs[a], land_refs[a].at[me]
            else:
                src, dst = src_refs[a].at[2 * cx + cy], land_refs[a].at[k]
            copies.append(_remote(src, dst, ssem.at[n * k + a], rsem.at[n * k + a], (cx, cy, c)))
    return copies


def exchange_start(mode, srcs, after, name):
    n = len(srcs)
    if mode == GATHER:
        lands = [lax.empty((N_CHIPS,) + s.shape, s.dtype) for s in srcs]
    else:
        lands = [lax.empty((N_CHIPS - 1,) + s.shape[1:], s.dtype) for s in srcs]
    extra = [] if after is None else [after]

    def body(*refs):
        src_refs, land_refs = refs[:n], refs[n:2 * n]
        ssem, rsem = refs[2 * n + len(extra)], refs[2 * n + len(extra) + 1]
        token = refs[-1]
        for cp in _chip_copies(mode, src_refs, land_refs, ssem, rsem):
            cp.start()
        token[...] = jnp.zeros_like(token)

    arrays = list(srcs) + lands
    return pl.pallas_call(
        body, name=name,
        out_shape=(pltpu.SemaphoreType.DMA((3 * n,)), pltpu.SemaphoreType.DMA((3 * n,)),
                   *[pltpu.HBM(a.shape, a.dtype) for a in arrays], jax.ShapeDtypeStruct((SUBLANES, LANES), F32)),
        in_specs=[HBM] * (2 * n) + [ANY] * len(extra),
        out_specs=(SEM, SEM, *[HBM] * (2 * n), pl.BlockSpec(memory_space=pltpu.VMEM)),
        input_output_aliases={i: 2 + i for i in range(2 * n)},
        compiler_params=pltpu.CompilerParams(has_side_effects=DATAFLOW),
    )(*[pltpu.with_memory_space_constraint(a, pltpu.HBM) for a in arrays], *extra)


def exchange_wait(mode, started, after, name):
    ssem, rsem, *thru, _ = started
    n = len(thru) // 2

    def body(*refs):
        src_refs, land_refs = refs[:n], refs[n:2 * n]
        ssem_ref, rsem_ref = refs[2 * n], refs[2 * n + 1]
        for cp in _chip_copies(mode, src_refs, land_refs, ssem_ref, rsem_ref):
            cp.wait_send()
            cp.wait_recv()

    outs = pl.pallas_call(
        body, name=name,
        out_shape=[pltpu.HBM(a.shape, a.dtype) for a in thru],
        in_specs=[HBM] * (2 * n) + [SEM, SEM, ANY],
        out_specs=[HBM] * (2 * n),
        input_output_aliases={i: i for i in range(2 * n)},
        compiler_params=pltpu.CompilerParams(has_side_effects=DATAFLOW),
    )(*thru, ssem, rsem, after)
    return outs[n:]


def sibling_exchange(p1, p2):
    def body(p1_ref, p2_ref, q1_ref, q2_ref, ssem, rsem):
        x, y, c, _ = _place()
        copies = [_remote(p_ref, q_ref, ssem.at[a], rsem.at[a], (x, y, 1 - c))
                  for a, (p_ref, q_ref) in enumerate(((p1_ref, q1_ref), (p2_ref, q2_ref)))]
        for cp in copies:
            cp.start()
        for cp in copies:
            cp.wait()

    return pl.pallas_call(
        body, name="sibling_exchange",
        in_specs=[ANY, ANY], out_specs=[ANY, ANY],
        out_shape=[jax.ShapeDtypeStruct(p.shape, p.dtype) for p in (p1, p2)],
        scratch_shapes=[pltpu.SemaphoreType.DMA((2,)), pltpu.SemaphoreType.DMA((2,))],
    )(p1, p2)


def all_reduce_small(v):
    r, n = v.shape
    piece = r // N_DEV

    def body(x_ref, out_ref, recv, ssem1, rsem1, ssem2, rsem2):
        x, y, c, _ = _place()
        me = 4 * x + 2 * y + c

        def peer(k):
            px = 1 - x if (k >> 2) & 1 else x
            py = 1 - y if (k >> 1) & 1 else y
            pc = 1 - c if k & 1 else c
            return (px, py, pc), 4 * px + 2 * py + pc

        def rows(ref, d):
            return ref.at[pl.ds(d * piece, piece), :]

        scatter = []
        for k in range(1, N_DEV):
            to, idx = peer(k)
            scatter.append(_remote(rows(x_ref, idx), recv.at[k], ssem1.at[k - 1], rsem1.at[k - 1], to))
            scatter[-1].start()
        acc = rows(x_ref, me)[...]
        for k in range(1, N_DEV):
            scatter[k - 1].wait_recv()
            acc = acc + recv[k]
        rows(out_ref, me)[...] = acc

        gather = []
        for k in range(1, N_DEV):
            to, _ = peer(k)
            gather.append(_remote(rows(out_ref, me), rows(out_ref, me), ssem2.at[k - 1], rsem2.at[k - 1], to))
            gather[-1].start()
        for k in range(1, N_DEV):
            to, idx = peer(k)
            _remote(rows(out_ref, idx), rows(out_ref, idx), ssem2.at[k - 1], rsem2.at[k - 1], to).wait_recv()
        for cp in scatter + gather:
            cp.wait_send()

    return pl.pallas_call(
        body, name="all_reduce_small",
        out_shape=jax.ShapeDtypeStruct((r, n), v.dtype),
        in_specs=[pl.BlockSpec(memory_space=pltpu.VMEM)],
        out_specs=pl.BlockSpec(memory_space=pltpu.VMEM),
        scratch_shapes=[pltpu.VMEM((N_DEV, piece, n), v.dtype)] + [pltpu.SemaphoreType.DMA((N_DEV - 1,))] * 4,
        compiler_params=pltpu.CompilerParams(vmem_limit_bytes=VMEM_LIMIT),
    )(v)


TILE_ROWS = 512
PACK_ROW_UNIT = 256


def _row_tile(r):
    return max(t for t in range(SUBLANES, TILE_ROWS + 1, SUBLANES) if r % t == 0)


def sum_partials(owns, parts):
    k, r, c = parts[0].shape
    depth = len(owns)
    tile = _row_tile(r)

    def body(buf_ref, o_ref, p_ref, out_ref):
        acc = o_ref[...]
        for i in range(k):
            acc = acc + p_ref[i].astype(F32)
        out_ref[0] = acc

    out = lax.empty((depth, r, c), F32)
    for l in range(depth):
        out = pl.pallas_call(
            functools.partial(body), name="sum_partials", grid=(r // tile,),
            in_specs=[ANY, pl.BlockSpec((tile, c), lambda i: (i, 0)), pl.BlockSpec((k, tile, c), lambda i: (0, i, 0))],
            out_specs=pl.BlockSpec((1, tile, c), lambda i, l=l: (l, i, 0)),
            out_shape=jax.ShapeDtypeStruct((depth, r, c), F32),
            input_output_aliases={0: 0},
            compiler_params=_params(("parallel",)),
        )(out, owns[l], parts[l])
    return out


def _adamw_update(w, g, m, v):
    m2 = ADAM_B1 * m + (1.0 - ADAM_B1) * g
    v2 = ADAM_B2 * v + (1.0 - ADAM_B2) * (g * g)
    m_hat = m2 / (1.0 - ADAM_B1 ** ADAM_STEP)
    v_hat = v2 / (1.0 - ADAM_B2 ** ADAM_STEP)
    return -ADAM_LR * (m_hat / (jnp.sqrt(v_hat) + ADAM_EPS) + ADAM_WD * w), m2, v2


def adamw_small(ws, gs, ms, vs):
    n = len(ws)

    def body(*refs):
        ins, outs = refs[:4 * n], refs[4 * n:]
        for i in range(n):
            d, m2, v2 = _adamw_update(ins[i][...], ins[n + i][...], ins[2 * n + i][...], ins[3 * n + i][...])
            outs[3 * i][...] = d
            outs[3 * i + 1][...] = m2
            outs[3 * i + 2][...] = v2

    outs = pl.pallas_call(
        body, name="adamw_small",
        out_shape=[jax.ShapeDtypeStruct(w.shape, F32) for w in ws for _ in range(3)],
    )(*ws, *gs, *ms, *vs)
    return [tuple(outs[3 * i:3 * i + 3]) for i in range(n)]


def adamw(w, ga, gb, m, v):
    n, r, c = w.shape
    tile = _row_tile(r)

    def body(w_ref, ga_ref, gb_ref, m_ref, v_ref, g_ref, d_ref, m2_ref, v2_ref):
        g = ga_ref[...] + gb_ref[...]
        g_ref[...] = g
        d_ref[...], m2_ref[...], v2_ref[...] = _adamw_update(w_ref[...], g, m_ref[...], v_ref[...])

    spec = pl.BlockSpec((1, tile, c), lambda j, i: (j, i, 0))
    return pl.pallas_call(
        body, name="adamw", grid=(n, r // tile),
        in_specs=[spec] * 5, out_specs=[spec] * 4,
        out_shape=[jax.ShapeDtypeStruct((n, r, c), F32)] * 4,
        compiler_params=_params(("parallel", "parallel")),
    )(w, ga, gb, m, v)


REPLICATED = ("norm_g", "conv_r_b", "lru_wa", "lru_ba", "lru_wx", "lru_bx", "lru_lambda", "gmlp_norm_g",
              "gmlp_ws", "gmlp_bs", "final_g")
CHIP_SHARDED_SMALL = ("conv_a_w", "conv_r_w")
PACK_LANES = LANES


def _pack(arrays):
    flat = jnp.concatenate([a.reshape(-1) for a in arrays])
    pad = (-flat.shape[0]) % (PACK_ROW_UNIT * PACK_LANES)
    return jnp.pad(flat, (0, pad)).reshape(-1, PACK_LANES)


def _unpack(packed, shapes):
    flat = packed.reshape(-1)
    out, off = [], 0
    for shp in shapes:
        n = math.prod(shp)
        out.append(flat[off:off + n].reshape(shp))
        off += n
    return out


def kernel(x, norm_g, w_in, conv_a_w, conv_r_w, conv_r_b, lru_wa, lru_ba, lru_wx, lru_bx, lru_lambda, gmlp_norm_g, gmlp_ws, gmlp_bs, w_out, final_g, loss_target, m_norm_g, m_w_in, m_conv_a_w, m_conv_r_w, m_conv_r_b, m_lru_wa, m_lru_ba, m_lru_wx, m_lru_bx, m_lru_lambda, m_gmlp_norm_g, m_gmlp_ws, m_gmlp_bs, m_w_out, m_final_g, v_norm_g, v_w_in, v_conv_a_w, v_conv_r_w, v_conv_r_b, v_lru_wa, v_lru_ba, v_lru_wx, v_lru_bx, v_lru_lambda, v_gmlp_norm_g, v_gmlp_ws, v_gmlp_bs, v_w_out, v_final_g):
    names = ("norm_g", "w_in", "conv_a_w", "conv_r_w", "conv_r_b", "lru_wa", "lru_ba", "lru_wx", "lru_bx",
             "lru_lambda", "gmlp_norm_g", "gmlp_ws", "gmlp_bs", "w_out", "final_g")
    w = dict(zip(names, (norm_g, w_in, conv_a_w, conv_r_w, conv_r_b, lru_wa, lru_ba, lru_wx, lru_bx, lru_lambda,
                         gmlp_norm_g, gmlp_ws, gmlp_bs, w_out, final_g)))
    m = dict(zip(names, (m_norm_g, m_w_in, m_conv_a_w, m_conv_r_w, m_conv_r_b, m_lru_wa, m_lru_ba, m_lru_wx, m_lru_bx,
                         m_lru_lambda, m_gmlp_norm_g, m_gmlp_ws, m_gmlp_bs, m_w_out, m_final_g)))
    v = dict(zip(names, (v_norm_g, v_w_in, v_conv_a_w, v_conv_r_w, v_conv_r_b, v_lru_wa, v_lru_ba, v_lru_wx, v_lru_bx,
                         v_lru_lambda, v_gmlp_norm_g, v_gmlp_ws, v_gmlp_bs, v_w_out, v_final_g)))
    depth = w_in.shape[0]
    out_rows = w_out.shape[1]
    conv_ch = conv_a_w.shape[2]
    chip = 2 * lax.axis_index("x") + lax.axis_index("y")

    taps = conv_a_w.shape[1] + conv_r_w.shape[1]
    w_in_t, m_w_in_t, v_w_in_t = (jnp.swapaxes(a, 1, 2) for a in (w_in, m_w_in, v_w_in))
    w_in_h, w_out_h = w_in_t.astype(BF16), w_out.astype(BF16)
    conv_own = jnp.concatenate([conv_a_w, conv_r_w], axis=1).reshape(depth * taps, conv_ch)
    gathers, token = [], None
    for l in range(depth):
        groups = [[w_in_h[l]], [w_out_h[l], conv_own]] if l == 0 else [[w_in_h[l], w_out_h[l]]]
        gathers.append([])
        for i, srcs in enumerate(groups):
            gathers[l].append(exchange_start(GATHER, srcs, token, f"gather_start_{l}_{i}"))
            token = gathers[l][-1][-1]
    p = dict(w)

    def with_own(land, own):
        return lax.dynamic_update_slice(land, own[None], (chip,) + (0,) * own.ndim)

    def layer_weights(l, x_l):
        lands = list(exchange_wait(GATHER, gathers[l][0], x_l, f"gather_wait_{l}_0"))
        w_in_l = with_own(lands[0], w_in_h[l]).reshape(D_IN, D_MODEL)
        gain = norm_g[l][None, :]
        if l == 0:
            gain = gain + token[0, 0]

        def rest(z_l):
            if l == 0:
                lands.extend(exchange_wait(GATHER, gathers[l][1], z_l, f"gather_wait_{l}_1"))
                conv = with_own(lands[2], conv_own).reshape(N_CHIPS, depth, taps, conv_ch)
                conv = conv.transpose(1, 2, 0, 3).reshape(depth, taps, GROUP_W)
                p["conv_a_w"] = conv[:, :conv_a_w.shape[1]]
                p["conv_r_w"] = conv[:, conv_a_w.shape[1]:]
                p["prepared"] = prepare_small_weights(p)
            return with_own(lands[1], w_out_h[l]).reshape(D_MIX, D_MODEL), p["prepared"]

        return gain, w_in_l, rest

    scatters, owns = [None] * depth, [None] * depth

    def projections_done(l, g_w_in_by_chip, g_w_in_own, g_w_out):
        go = g_w_out.reshape(N_CHIPS, out_rows, D_MODEL)
        owns[l] = (g_w_in_own, lax.dynamic_index_in_dim(go, chip, axis=0, keepdims=False))
        scatters[l] = exchange_start(SCATTER, [g_w_in_by_chip, go.astype(BF16)], None, f"scatter_start_{l}")
        return scatters[l][-1][0, 0]

    loss8, dx, grads = local_step(x[0], loss_target[0], final_g, depth, chip.reshape(1), layer_weights,
                                  projections_done)

    res = {}
    small = REPLICATED + CHIP_SHARDED_SMALL
    packed = _pack([grads[k] for k in small] + [loss8[0, :1]])
    total = all_reduce_small(packed)
    *sums, loss = _unpack(total, [grads[k].shape for k in small] + [()])
    gs = dict(zip(small, sums))
    for k in CHIP_SHARDED_SMALL:
        gs[k] = lax.dynamic_slice_in_dim(gs[k], chip * conv_ch, conv_ch, axis=2)
    as2d = lambda a: a[None] if a.ndim == 1 else a
    outs = adamw_small(*[[as2d(d[k]) for k in small] for d in (w, gs, m, v)])
    for k, (delta, m2, v2) in zip(small, outs):
        res[k] = [t.reshape(w[k].shape) for t in (gs[k], delta, m2, v2)]

    parts = [exchange_wait(SCATTER, scatters[l], total, f"scatter_wait_{l}") for l in range(depth)]
    p1 = sum_partials([owns[l][0] for l in range(depth)], [parts[l][0] for l in range(depth)])
    p2 = sum_partials([owns[l][1] for l in range(depth)], [parts[l][1] for l in range(depth)])
    q1, q2 = sibling_exchange(p1, p2)
    res["w_in"] = [jnp.swapaxes(t, 1, 2) for t in adamw(w_in_t, p1, q1, m_w_in_t, v_w_in_t)]
    res["w_out"] = adamw(w_out, p2, q2, m_w_out, v_w_out)

    return (loss, dx[None], *[res[k][0] for k in names], *[res[k][1] for k in names],
            *[res[k][2] for k in names], *[res[k][3] for k in names])
```
